```python
import math
import jax, jax.numpy as jnp
from jax import lax
import numpy as np

D_MODEL = 2048
BATCH = 8
SEQ = 4096
DEPTH = 1

HEAD_DIM = 64
N_Q_HEADS = 16
N_KV_HEADS = 2
Q_PER_KV = N_Q_HEADS // N_KV_HEADS
ATTN_WIDTH = N_Q_HEADS * HEAD_DIM
KV_WIDTH = N_KV_HEADS * HEAD_DIM
WINDOW = 128
BLOCK = 128
NEG_INF = -1e30
SSM_WIDTH = D_MODEL // 2
SSM_GROUP = 16
N_SSM_GROUPS = SSM_WIDTH // SSM_GROUP
SSM_STATE = 64
DT_MIN = 1e-3
DT_MAX = 1e-1
D_FF = 5632
CONV_WIDTH = 3
N_BRANCHES = 2
IN_WIDTH = ATTN_WIDTH + 2 * KV_WIDTH + SSM_WIDTH + N_BRANCHES * D_MODEL
RMS_EPS = 1e-6

kernel_name = 'hybrid_swa_s5_convffn_adaln'


def rmsnorm(x, g):
    xf = x.astype(jnp.float32)
    xf = xf * lax.rsqrt(jnp.mean(xf * xf, axis=-1, keepdims=True) + RMS_EPS)
    return xf.astype(x.dtype) * g


def sliding_window_attention(q, k, v, sinks):
    b, l = q.shape[0], q.shape[1]
    nb = l // BLOCK
    qb = q.reshape(b, nb, BLOCK, N_KV_HEADS, Q_PER_KV, HEAD_DIM)
    pad = ((0, 0), (BLOCK, 0), (0, 0), (0, 0))
    kp = jnp.pad(k, pad).reshape(b, nb + 1, BLOCK, N_KV_HEADS, HEAD_DIM)
    vp = jnp.pad(v, pad).reshape(b, nb + 1, BLOCK, N_KV_HEADS, HEAD_DIM)
    kw = jnp.concatenate([kp[:, :-1], kp[:, 1:]], axis=2)
    vw = jnp.concatenate([vp[:, :-1], vp[:, 1:]], axis=2)
    s = jnp.einsum('bnqhgd,bnkhd->bnhgqk', qb, kw).astype(jnp.float32) * (HEAD_DIM ** -0.5)
    qi = jnp.arange(BLOCK)[:, None]
    kj = jnp.arange(2 * BLOCK)[None, :]
    rel = qi + BLOCK - kj
    band = (rel >= 0) & (rel < WINDOW)
    key_pos = jnp.arange(nb)[:, None, None] * BLOCK - BLOCK + kj[None]
    mask = band[None] & (key_pos >= 0)
    s = jnp.where(mask[None, :, None, None], s, NEG_INF)
    sink = sinks.astype(jnp.float32).reshape(N_KV_HEADS, Q_PER_KV)[None, None, :, :, None, None]
    m = jnp.maximum(jnp.max(s, axis=-1, keepdims=True), sink)
    p = jnp.exp(s - m)
    p = p / (jnp.sum(p, axis=-1, keepdims=True) + jnp.exp(sink - m))
    o = jnp.einsum('bnhgqk,bnkhd->bnqhgd', p.astype(v.dtype), vw)
    return o.reshape(b, l, ATTN_WIDTH)


def s5_layer(u, a_re, a_im, log_dt, b_re, b_im, c_re, c_im, d_skip):
    bsz, l = u.shape[0], u.shape[1]
    f32 = jnp.float32
    ug = u.reshape(bsz, l, N_SSM_GROUPS, SSM_GROUP).astype(f32)
    dt = jnp.exp(log_dt.astype(f32))[:, None]
    ar, ai = a_re.astype(f32), a_im.astype(f32)
    mag = jnp.exp(ar * dt)
    lr, li = mag * jnp.cos(ai * dt), mag * jnp.sin(ai * dt)
    den = ar * ar + ai * ai
    zr = ((lr - 1.0) * ar + li * ai) / den
    zi = (li * ar - (lr - 1.0) * ai) / den
    br, bi = b_re.astype(f32), b_im.astype(f32)
    bbar_r = zr[:, :, None] * br - zi[:, :, None] * bi
    bbar_i = zr[:, :, None] * bi + zi[:, :, None] * br
    xr = jnp.einsum('blgp,gnp->blgn', ug, bbar_r)
    xi = jnp.einsum('blgp,gnp->blgn', ug, bbar_i)
    seq_r = jnp.broadcast_to(lr[None, None], (1, l, N_SSM_GROUPS, SSM_STATE))
    seq_i = jnp.broadcast_to(li[None, None], (1, l, N_SSM_GROUPS, SSM_STATE))

    def combine(e1, e2):
        a1r, a1i, b1r, b1i = e1
        a2r, a2i, b2r, b2i = e2
        return (a1r * a2r - a1i * a2i,
                a1r * a2i + a1i * a2r,
                a2r * b1r - a2i * b1i + b2r,
                a2r * b1i + a2i * b1r + b2i)

    _, _, hr, hi = lax.associative_scan(combine, (seq_r, seq_i, xr, xi), axis=1)
    y = (jnp.einsum('blgn,gpn->blgp', hr, c_re.astype(f32))
         - jnp.einsum('blgn,gpn->blgp', hi, c_im.astype(f32))
         + d_skip.astype(f32) * ug)
    return y.reshape(bsz, l, SSM_WIDTH).astype(u.dtype)


def conv_ffn(h, w_up, conv_w, conv_b, w_down):
    up = h @ w_up
    gate, val = jnp.split(up, 2, axis=-1)
    gate = lax.conv_general_dilated(
        gate, conv_w[:, None, :], window_strides=(1,), padding=((CONV_WIDTH - 1, 0),),
        dimension_numbers=('NWC', 'WIO', 'NWC'), feature_group_count=D_FF) + conv_b
    return (jax.nn.silu(gate) * val) @ w_down


def _fwd_setup_inputs(seed: int = 0) -> dict:
    key = jax.random.key(seed)
    ks = jax.random.split(key, 24)
    f32 = jnp.float32

    def nrm(k, shape, scale):
        return jax.random.normal(k, shape, f32) * scale

    G, N, P = N_SSM_GROUPS, SSM_STATE, SSM_GROUP
    a_im0 = jnp.pi * jnp.arange(N, dtype=f32)
    return {
        'x': nrm(ks[0], (BATCH, SEQ, D_MODEL), 1.0),
        'c': nrm(ks[1], (BATCH, D_MODEL), 1.0),
        'ada_w': nrm(ks[2], (DEPTH, D_MODEL, 6 * D_MODEL), D_MODEL ** -0.5),
        'ada_b': nrm(ks[3], (DEPTH, 6 * D_MODEL), 0.01),
        'norm_mix_g': 1.0 + nrm(ks[4], (DEPTH, D_MODEL), 0.01),
        'w_in': nrm(ks[5], (DEPTH, D_MODEL, IN_WIDTH), D_MODEL ** -0.5),
        'attn_sinks': nrm(ks[6], (DEPTH, N_Q_HEADS), 0.5),
        'w_attn_proj': nrm(ks[7], (DEPTH, ATTN_WIDTH, D_MODEL), ATTN_WIDTH ** -0.5),
        'ssm_a_re': -0.5 + nrm(ks[8], (DEPTH, G, N), 0.01),
        'ssm_a_im': a_im0 + nrm(ks[9], (DEPTH, G, N), 0.01),
        'ssm_log_dt': jax.random.uniform(ks[10], (DEPTH, G), f32, math.log(DT_MIN), math.log(DT_MAX)),
        'ssm_b_re': nrm(ks[11], (DEPTH, G, N, P), (2 * P) ** -0.5),
        'ssm_b_im': nrm(ks[12], (DEPTH, G, N, P), (2 * P) ** -0.5),
        'ssm_c_re': nrm(ks[13], (DEPTH, G, P, N), N ** -0.5),
        'ssm_c_im': nrm(ks[14], (DEPTH, G, P, N), N ** -0.5),
        'ssm_d': nrm(ks[15], (DEPTH, G, P), 1.0),
        'w_ssm_glu': nrm(ks[16], (DEPTH, SSM_WIDTH, 2 * D_MODEL), SSM_WIDTH ** -0.5),
        'w_out': nrm(ks[17], (DEPTH, D_MODEL, D_MODEL), D_MODEL ** -0.5),
        'norm_ffn_g': 1.0 + nrm(ks[18], (DEPTH, D_MODEL), 0.01),
        'w_ffn_up': nrm(ks[19], (DEPTH, D_MODEL, 2 * D_FF), D_MODEL ** -0.5),
        'ffn_conv_w': nrm(ks[20], (DEPTH, CONV_WIDTH, D_FF), CONV_WIDTH ** -0.5),
        'ffn_conv_b': nrm(ks[21], (DEPTH, D_FF), 0.01),
        'w_ffn_down': nrm(ks[22], (DEPTH, D_FF, D_MODEL), D_FF ** -0.5),
        'final_g': 1.0 + nrm(ks[23], (D_MODEL,), 0.01),
    }


def _fwd_reference(x, c, ada_w, ada_b, norm_mix_g, w_in, attn_sinks, w_attn_proj,
              ssm_a_re, ssm_a_im, ssm_log_dt, ssm_b_re, ssm_b_im, ssm_c_re, ssm_c_im, ssm_d,
              w_ssm_glu, w_out, norm_ffn_g, w_ffn_up, ffn_conv_w, ffn_conv_b, w_ffn_down, final_g):
    bsz, l = x.shape[0], x.shape[1]
    cond = jax.nn.silu(c)
    splits = [ATTN_WIDTH, ATTN_WIDTH + KV_WIDTH, ATTN_WIDTH + 2 * KV_WIDTH,
              ATTN_WIDTH + 2 * KV_WIDTH + SSM_WIDTH, ATTN_WIDTH + 2 * KV_WIDTH + SSM_WIDTH + D_MODEL]
    for i in range(DEPTH):
        mod = (cond @ ada_w[i] + ada_b[i])[:, None, :]
        sh1, sc1, g1, sh2, sc2, g2 = jnp.split(mod, 6, axis=-1)
        h = rmsnorm(x, norm_mix_g[i]) * (1.0 + sc1) + sh1
        proj = h @ w_in[i]
        q, k, v, u, g_attn, g_ssm = jnp.split(proj, splits, axis=-1)
        q = q.reshape(bsz, l, N_Q_HEADS, HEAD_DIM)
        k = k.reshape(bsz, l, N_KV_HEADS, HEAD_DIM)
        v = v.reshape(bsz, l, N_KV_HEADS, HEAD_DIM)
        attn = sliding_window_attention(q, k, v, attn_sinks[i]) @ w_attn_proj[i]
        y = s5_layer(u, ssm_a_re[i], ssm_a_im[i], ssm_log_dt[i], ssm_b_re[i], ssm_b_im[i],
                     ssm_c_re[i], ssm_c_im[i], ssm_d[i])
        glu_a, glu_b = jnp.split(jax.nn.gelu(y) @ w_ssm_glu[i], 2, axis=-1)
        ssm = glu_a * jax.nn.sigmoid(glu_b)
        mixed = jax.nn.sigmoid(g_attn) * attn + jax.nn.sigmoid(g_ssm) * ssm
        x = x + g1 * (mixed @ w_out[i])
        h = rmsnorm(x, norm_ffn_g[i]) * (1.0 + sc2) + sh2
        x = x + g2 * conv_ffn(h, w_ffn_up[i], ffn_conv_w[i], ffn_conv_b[i], w_ffn_down[i])
    return rmsnorm(x, final_g)


import jax as _jax
import jax.numpy as _jnp

TWIN_FORMAT = 'train_step'
FWD_PARAMS = ['x', 'c', 'ada_w', 'ada_b', 'norm_mix_g', 'w_in', 'attn_sinks', 'w_attn_proj', 'ssm_a_re', 'ssm_a_im', 'ssm_log_dt', 'ssm_b_re', 'ssm_b_im', 'ssm_c_re', 'ssm_c_im', 'ssm_d', 'w_ssm_glu', 'w_out', 'norm_ffn_g', 'w_ffn_up', 'ffn_conv_w', 'ffn_conv_b', 'w_ffn_down', 'final_g']
TWIN_WEIGHTS = ['ada_w', 'ada_b', 'norm_mix_g', 'w_in', 'attn_sinks', 'w_attn_proj', 'ssm_a_re', 'ssm_a_im', 'ssm_log_dt', 'ssm_b_re', 'ssm_b_im', 'ssm_c_re', 'ssm_c_im', 'ssm_d', 'w_ssm_glu', 'w_out', 'norm_ffn_g', 'w_ffn_up', 'ffn_conv_w', 'ffn_conv_b', 'w_ffn_down', 'final_g']
TWIN_DIFF_INPUT = 'x'
TWIN_INPUTS = ['x', 'c', 'ada_w', 'ada_b', 'norm_mix_g', 'w_in', 'attn_sinks', 'w_attn_proj', 'ssm_a_re', 'ssm_a_im', 'ssm_log_dt', 'ssm_b_re', 'ssm_b_im', 'ssm_c_re', 'ssm_c_im', 'ssm_d', 'w_ssm_glu', 'w_out', 'norm_ffn_g', 'w_ffn_up', 'ffn_conv_w', 'ffn_conv_b', 'w_ffn_down', 'final_g', 'loss_target', 'm_ada_w', 'm_ada_b', 'm_norm_mix_g', 'm_w_in', 'm_attn_sinks', 'm_w_attn_proj', 'm_ssm_a_re', 'm_ssm_a_im', 'm_ssm_log_dt', 'm_ssm_b_re', 'm_ssm_b_im', 'm_ssm_c_re', 'm_ssm_c_im', 'm_ssm_d', 'm_w_ssm_glu', 'm_w_out', 'm_norm_ffn_g', 'm_w_ffn_up', 'm_ffn_conv_w', 'm_ffn_conv_b', 'm_w_ffn_down', 'm_final_g', 'v_ada_w', 'v_ada_b', 'v_norm_mix_g', 'v_w_in', 'v_attn_sinks', 'v_w_attn_proj', 'v_ssm_a_re', 'v_ssm_a_im', 'v_ssm_log_dt', 'v_ssm_b_re', 'v_ssm_b_im', 'v_ssm_c_re', 'v_ssm_c_im', 'v_ssm_d', 'v_w_ssm_glu', 'v_w_out', 'v_norm_ffn_g', 'v_w_ffn_up', 'v_ffn_conv_w', 'v_ffn_conv_b', 'v_w_ffn_down', 'v_final_g']
TWIN_OUTPUTS = ['loss', 'grad_x', 'grad_ada_w', 'grad_ada_b', 'grad_norm_mix_g', 'grad_w_in', 'grad_attn_sinks', 'grad_w_attn_proj', 'grad_ssm_a_re', 'grad_ssm_a_im', 'grad_ssm_log_dt', 'grad_ssm_b_re', 'grad_ssm_b_im', 'grad_ssm_c_re', 'grad_ssm_c_im', 'grad_ssm_d', 'grad_w_ssm_glu', 'grad_w_out', 'grad_norm_ffn_g', 'grad_w_ffn_up', 'grad_ffn_conv_w', 'grad_ffn_conv_b', 'grad_w_ffn_down', 'grad_final_g', 'delta_ada_w', 'delta_ada_b', 'delta_norm_mix_g', 'delta_w_in', 'delta_attn_sinks', 'delta_w_attn_proj', 'delta_ssm_a_re', 'delta_ssm_a_im', 'delta_ssm_log_dt', 'delta_ssm_b_re', 'delta_ssm_b_im', 'delta_ssm_c_re', 'delta_ssm_c_im', 'delta_ssm_d', 'delta_w_ssm_glu', 'delta_w_out', 'delta_norm_ffn_g', 'delta_w_ffn_up', 'delta_ffn_conv_w', 'delta_ffn_conv_b', 'delta_w_ffn_down', 'delta_final_g', 'new_m_ada_w', 'new_m_ada_b', 'new_m_norm_mix_g', 'new_m_w_in', 'new_m_attn_sinks', 'new_m_w_attn_proj', 'new_m_ssm_a_re', 'new_m_ssm_a_im', 'new_m_ssm_log_dt', 'new_m_ssm_b_re', 'new_m_ssm_b_im', 'new_m_ssm_c_re', 'new_m_ssm_c_im', 'new_m_ssm_d', 'new_m_w_ssm_glu', 'new_m_w_out', 'new_m_norm_ffn_g', 'new_m_w_ffn_up', 'new_m_ffn_conv_w', 'new_m_ffn_conv_b', 'new_m_w_ffn_down', 'new_m_final_g', 'new_v_ada_w', 'new_v_ada_b', 'new_v_norm_mix_g', 'new_v_w_in', 'new_v_attn_sinks', 'new_v_w_attn_proj', 'new_v_ssm_a_re', 'new_v_ssm_a_im', 'new_v_ssm_log_dt', 'new_v_ssm_b_re', 'new_v_ssm_b_im', 'new_v_ssm_c_re', 'new_v_ssm_c_im', 'new_v_ssm_d', 'new_v_w_ssm_glu', 'new_v_w_out', 'new_v_norm_ffn_g', 'new_v_w_ffn_up', 'new_v_ffn_conv_w', 'new_v_ffn_conv_b', 'new_v_w_ffn_down', 'new_v_final_g']
TWIN_LEAF_KINDS = {'loss': 'loss', 'grad_x': 'grad_x', 'grad_ada_w': 'grad_w', 'grad_ada_b': 'grad_w', 'grad_norm_mix_g': 'grad_w', 'grad_w_in': 'grad_w', 'grad_attn_sinks': 'grad_w', 'grad_w_attn_proj': 'grad_w', 'grad_ssm_a_re': 'grad_w', 'grad_ssm_a_im': 'grad_w', 'grad_ssm_log_dt': 'grad_w', 'grad_ssm_b_re': 'grad_w', 'grad_ssm_b_im': 'grad_w', 'grad_ssm_c_re': 'grad_w', 'grad_ssm_c_im': 'grad_w', 'grad_ssm_d': 'grad_w', 'grad_w_ssm_glu': 'grad_w', 'grad_w_out': 'grad_w', 'grad_norm_ffn_g': 'grad_w', 'grad_w_ffn_up': 'grad_w', 'grad_ffn_conv_w': 'grad_w', 'grad_ffn_conv_b': 'grad_w', 'grad_w_ffn_down': 'grad_w', 'grad_final_g': 'grad_w', 'delta_ada_w': 'delta_w', 'delta_ada_b': 'delta_w', 'delta_norm_mix_g': 'delta_w', 'delta_w_in': 'delta_w', 'delta_attn_sinks': 'delta_w', 'delta_w_attn_proj': 'delta_w', 'delta_ssm_a_re': 'delta_w', 'delta_ssm_a_im': 'delta_w', 'delta_ssm_log_dt': 'delta_w', 'delta_ssm_b_re': 'delta_w', 'delta_ssm_b_im': 'delta_w', 'delta_ssm_c_re': 'delta_w', 'delta_ssm_c_im': 'delta_w', 'delta_ssm_d': 'delta_w', 'delta_w_ssm_glu': 'delta_w', 'delta_w_out': 'delta_w', 'delta_norm_ffn_g': 'delta_w', 'delta_w_ffn_up': 'delta_w', 'delta_ffn_conv_w': 'delta_w', 'delta_ffn_conv_b': 'delta_w', 'delta_w_ffn_down': 'delta_w', 'delta_final_g': 'delta_w', 'new_m_ada_w': 'new_m', 'new_m_ada_b': 'new_m', 'new_m_norm_mix_g': 'new_m', 'new_m_w_in': 'new_m', 'new_m_attn_sinks': 'new_m', 'new_m_w_attn_proj': 'new_m', 'new_m_ssm_a_re': 'new_m', 'new_m_ssm_a_im': 'new_m', 'new_m_ssm_log_dt': 'new_m', 'new_m_ssm_b_re': 'new_m', 'new_m_ssm_b_im': 'new_m', 'new_m_ssm_c_re': 'new_m', 'new_m_ssm_c_im': 'new_m', 'new_m_ssm_d': 'new_m', 'new_m_w_ssm_glu': 'new_m', 'new_m_w_out': 'new_m', 'new_m_norm_ffn_g': 'new_m', 'new_m_w_ffn_up': 'new_m', 'new_m_ffn_conv_w': 'new_m', 'new_m_ffn_conv_b': 'new_m', 'new_m_w_ffn_down': 'new_m', 'new_m_final_g': 'new_m', 'new_v_ada_w': 'new_v', 'new_v_ada_b': 'new_v', 'new_v_norm_mix_g': 'new_v', 'new_v_w_in': 'new_v', 'new_v_attn_sinks': 'new_v', 'new_v_w_attn_proj': 'new_v', 'new_v_ssm_a_re': 'new_v', 'new_v_ssm_a_im': 'new_v', 'new_v_ssm_log_dt': 'new_v', 'new_v_ssm_b_re': 'new_v', 'new_v_ssm_b_im': 'new_v', 'new_v_ssm_c_re': 'new_v', 'new_v_ssm_c_im': 'new_v', 'new_v_ssm_d': 'new_v', 'new_v_w_ssm_glu': 'new_v', 'new_v_w_out': 'new_v', 'new_v_norm_ffn_g': 'new_v', 'new_v_w_ffn_up': 'new_v', 'new_v_ffn_conv_w': 'new_v', 'new_v_ffn_conv_b': 'new_v', 'new_v_w_ffn_down': 'new_v', 'new_v_final_g': 'new_v'}


def _forward(args):
    return _fwd_reference(*[args[k] for k in FWD_PARAMS])


def _output_shape():
    def fwd():
        inp = _fwd_setup_inputs(0)
        return _fwd_reference(*[inp[k] for k in FWD_PARAMS])
    out = _jax.eval_shape(fwd)
    return out.shape, out.dtype

N_MICROBATCH = 1
ADAM_LR = 0.001
ADAM_B1 = 0.9
ADAM_B2 = 0.999
ADAM_EPS = 1e-08
ADAM_WD = 0.01
ADAM_STEP = 10
PER_EXAMPLE_BATCH_AXIS = {'x': 0, 'c': 0, 'loss_target': 0}
SHARED_INPUTS = []
_WEIGHT_DTYPES = {'ada_w': _jnp.float32, 'ada_b': _jnp.float32, 'norm_mix_g': _jnp.float32, 'w_in': _jnp.float32, 'attn_sinks': _jnp.float32, 'w_attn_proj': _jnp.float32, 'ssm_a_re': _jnp.float32, 'ssm_a_im': _jnp.float32, 'ssm_log_dt': _jnp.float32, 'ssm_b_re': _jnp.float32, 'ssm_b_im': _jnp.float32, 'ssm_c_re': _jnp.float32, 'ssm_c_im': _jnp.float32, 'ssm_d': _jnp.float32, 'w_ssm_glu': _jnp.float32, 'w_out': _jnp.float32, 'norm_ffn_g': _jnp.float32, 'w_ffn_up': _jnp.float32, 'ffn_conv_w': _jnp.float32, 'ffn_conv_b': _jnp.float32, 'w_ffn_down': _jnp.float32, 'final_g': _jnp.float32}
MOMENT_SCALE = {'ada_w': 3.168994e-02, 'ada_b': 5.403835e-02, 'norm_mix_g': 2.700145e-02, 'w_in': 1.953491e-02, 'attn_sinks': 1.196849e-02, 'w_attn_proj': 2.107219e-02, 'ssm_a_re': 2.429486e-03, 'ssm_a_im': 3.366238e-03, 'ssm_log_dt': 8.912282e-01, 'ssm_b_re': 1.460479e-03, 'ssm_b_im': 1.427787e-03, 'ssm_c_re': 2.105527e-03, 'ssm_c_im': 2.303570e-03, 'ssm_d': 2.327376e-02, 'w_ssm_glu': 1.209464e-02, 'w_out': 2.614665e-02, 'norm_ffn_g': 6.236515e-02, 'w_ffn_up': 2.945702e-02, 'ffn_conv_w': 3.059286e-02, 'ffn_conv_b': 2.181122e-02, 'w_ffn_down': 4.809893e-02, 'final_g': 1.618579e+01}


def _to_microbatches(a, axis):
    t = _jnp.moveaxis(a, axis, 0)
    t = t.reshape((N_MICROBATCH, t.shape[0] // N_MICROBATCH) + t.shape[1:])
    return _jnp.moveaxis(t, 1, axis + 1)


def setup_inputs(seed: int = 0) -> dict:
    inp = _fwd_setup_inputs(seed)
    key = _jax.random.fold_in(_jax.random.key(seed), 7919)
    shape, _ = _output_shape()
    out = dict(inp)
    out["loss_target"] = _jax.random.normal(_jax.random.fold_in(key, 0), shape, _jnp.float32)
    for i, name in enumerate(TWIN_WEIGHTS):
        w = inp[name].astype(_jnp.float32)
        if MOMENT_SCALE is None:
            s = _jnp.sqrt(_jnp.mean(_jnp.square(w)) + 1e-30)
        else:
            s = MOMENT_SCALE[name]
        km, kv = _jax.random.split(_jax.random.fold_in(key, i + 1))
        out[name] = w
        out["m_" + name] = s * _jax.random.normal(km, w.shape, _jnp.float32)
        out["v_" + name] = (s * s) * _jax.random.uniform(kv, w.shape, _jnp.float32, 0.5, 1.5)
    if N_MICROBATCH > 1:
        for name, axis in PER_EXAMPLE_BATCH_AXIS.items():
            out[name] = _to_microbatches(out[name], axis)
    return {'x': out['x'], 'c': out['c'], 'ada_w': out['ada_w'], 'ada_b': out['ada_b'], 'norm_mix_g': out['norm_mix_g'], 'w_in': out['w_in'], 'attn_sinks': out['attn_sinks'], 'w_attn_proj': out['w_attn_proj'], 'ssm_a_re': out['ssm_a_re'], 'ssm_a_im': out['ssm_a_im'], 'ssm_log_dt': out['ssm_log_dt'], 'ssm_b_re': out['ssm_b_re'], 'ssm_b_im': out['ssm_b_im'], 'ssm_c_re': out['ssm_c_re'], 'ssm_c_im': out['ssm_c_im'], 'ssm_d': out['ssm_d'], 'w_ssm_glu': out['w_ssm_glu'], 'w_out': out['w_out'], 'norm_ffn_g': out['norm_ffn_g'], 'w_ffn_up': out['w_ffn_up'], 'ffn_conv_w': out['ffn_conv_w'], 'ffn_conv_b': out['ffn_conv_b'], 'w_ffn_down': out['w_ffn_down'], 'final_g': out['final_g'], 'loss_target': out['loss_target'], 'm_ada_w': out['m_ada_w'], 'm_ada_b': out['m_ada_b'], 'm_norm_mix_g': out['m_norm_mix_g'], 'm_w_in': out['m_w_in'], 'm_attn_sinks': out['m_attn_sinks'], 'm_w_attn_proj': out['m_w_attn_proj'], 'm_ssm_a_re': out['m_ssm_a_re'], 'm_ssm_a_im': out['m_ssm_a_im'], 'm_ssm_log_dt': out['m_ssm_log_dt'], 'm_ssm_b_re': out['m_ssm_b_re'], 'm_ssm_b_im': out['m_ssm_b_im'], 'm_ssm_c_re': out['m_ssm_c_re'], 'm_ssm_c_im': out['m_ssm_c_im'], 'm_ssm_d': out['m_ssm_d'], 'm_w_ssm_glu': out['m_w_ssm_glu'], 'm_w_out': out['m_w_out'], 'm_norm_ffn_g': out['m_norm_ffn_g'], 'm_w_ffn_up': out['m_w_ffn_up'], 'm_ffn_conv_w': out['m_ffn_conv_w'], 'm_ffn_conv_b': out['m_ffn_conv_b'], 'm_w_ffn_down': out['m_w_ffn_down'], 'm_final_g': out['m_final_g'], 'v_ada_w': out['v_ada_w'], 'v_ada_b': out['v_ada_b'], 'v_norm_mix_g': out['v_norm_mix_g'], 'v_w_in': out['v_w_in'], 'v_attn_sinks': out['v_attn_sinks'], 'v_w_attn_proj': out['v_w_attn_proj'], 'v_ssm_a_re': out['v_ssm_a_re'], 'v_ssm_a_im': out['v_ssm_a_im'], 'v_ssm_log_dt': out['v_ssm_log_dt'], 'v_ssm_b_re': out['v_ssm_b_re'], 'v_ssm_b_im': out['v_ssm_b_im'], 'v_ssm_c_re': out['v_ssm_c_re'], 'v_ssm_c_im': out['v_ssm_c_im'], 'v_ssm_d': out['v_ssm_d'], 'v_w_ssm_glu': out['v_w_ssm_glu'], 'v_w_out': out['v_w_out'], 'v_norm_ffn_g': out['v_norm_ffn_g'], 'v_w_ffn_up': out['v_w_ffn_up'], 'v_ffn_conv_w': out['v_ffn_conv_w'], 'v_ffn_conv_b': out['v_ffn_conv_b'], 'v_w_ffn_down': out['v_w_ffn_down'], 'v_final_g': out['v_final_g']}


def _loss(weights, diff, rest, loss_target):
    with _jax.named_scope("forward"):
        args = {**rest, TWIN_DIFF_INPUT: diff, **{k: w.astype(_WEIGHT_DTYPES[k]) for k, w in weights.items()}}
        y = _forward(args)
    with _jax.named_scope("loss_head"):
        err = _jnp.square(y.astype(_jnp.float32) - loss_target)
        return 0.5 * _jnp.sum(_jnp.mean(err, axis=-1)) if err.ndim else 0.5 * err


def _adamw(w, g, m, v):
    m = ADAM_B1 * m + (1.0 - ADAM_B1) * g
    v = ADAM_B2 * v + (1.0 - ADAM_B2) * _jnp.square(g)
    m_hat = m / (1.0 - ADAM_B1 ** ADAM_STEP)
    v_hat = v / (1.0 - ADAM_B2 ** ADAM_STEP)
    delta = -ADAM_LR * (m_hat / (_jnp.sqrt(v_hat) + ADAM_EPS) + ADAM_WD * w)
    return delta, m, v


def reference(x, c, ada_w, ada_b, norm_mix_g, w_in, attn_sinks, w_attn_proj, ssm_a_re, ssm_a_im, ssm_log_dt, ssm_b_re, ssm_b_im, ssm_c_re, ssm_c_im, ssm_d, w_ssm_glu, w_out, norm_ffn_g, w_ffn_up, ffn_conv_w, ffn_conv_b, w_ffn_down, final_g, loss_target, m_ada_w, m_ada_b, m_norm_mix_g, m_w_in, m_attn_sinks, m_w_attn_proj, m_ssm_a_re, m_ssm_a_im, m_ssm_log_dt, m_ssm_b_re, m_ssm_b_im, m_ssm_c_re, m_ssm_c_im, m_ssm_d, m_w_ssm_glu, m_w_out, m_norm_ffn_g, m_w_ffn_up, m_ffn_conv_w, m_ffn_conv_b, m_w_ffn_down, m_final_g, v_ada_w, v_ada_b, v_norm_mix_g, v_w_in, v_attn_sinks, v_w_attn_proj, v_ssm_a_re, v_ssm_a_im, v_ssm_log_dt, v_ssm_b_re, v_ssm_b_im, v_ssm_c_re, v_ssm_c_im, v_ssm_d, v_w_ssm_glu, v_w_out, v_norm_ffn_g, v_w_ffn_up, v_ffn_conv_w, v_ffn_conv_b, v_w_ffn_down, v_final_g):
    given = dict(x=x, c=c, ada_w=ada_w, ada_b=ada_b, norm_mix_g=norm_mix_g, w_in=w_in, attn_sinks=attn_sinks, w_attn_proj=w_attn_proj, ssm_a_re=ssm_a_re, ssm_a_im=ssm_a_im, ssm_log_dt=ssm_log_dt, ssm_b_re=ssm_b_re, ssm_b_im=ssm_b_im, ssm_c_re=ssm_c_re, ssm_c_im=ssm_c_im, ssm_d=ssm_d, w_ssm_glu=w_ssm_glu, w_out=w_out, norm_ffn_g=norm_ffn_g, w_ffn_up=w_ffn_up, ffn_conv_w=ffn_conv_w, ffn_conv_b=ffn_conv_b, w_ffn_down=w_ffn_down, final_g=final_g, loss_target=loss_target, m_ada_w=m_ada_w, m_ada_b=m_ada_b, m_norm_mix_g=m_norm_mix_g, m_w_in=m_w_in, m_attn_sinks=m_attn_sinks, m_w_attn_proj=m_w_attn_proj, m_ssm_a_re=m_ssm_a_re, m_ssm_a_im=m_ssm_a_im, m_ssm_log_dt=m_ssm_log_dt, m_ssm_b_re=m_ssm_b_re, m_ssm_b_im=m_ssm_b_im, m_ssm_c_re=m_ssm_c_re, m_ssm_c_im=m_ssm_c_im, m_ssm_d=m_ssm_d, m_w_ssm_glu=m_w_ssm_glu, m_w_out=m_w_out, m_norm_ffn_g=m_norm_ffn_g, m_w_ffn_up=m_w_ffn_up, m_ffn_conv_w=m_ffn_conv_w, m_ffn_conv_b=m_ffn_conv_b, m_w_ffn_down=m_w_ffn_down, m_final_g=m_final_g, v_ada_w=v_ada_w, v_ada_b=v_ada_b, v_norm_mix_g=v_norm_mix_g, v_w_in=v_w_in, v_attn_sinks=v_attn_sinks, v_w_attn_proj=v_w_attn_proj, v_ssm_a_re=v_ssm_a_re, v_ssm_a_im=v_ssm_a_im, v_ssm_log_dt=v_ssm_log_dt, v_ssm_b_re=v_ssm_b_re, v_ssm_b_im=v_ssm_b_im, v_ssm_c_re=v_ssm_c_re, v_ssm_c_im=v_ssm_c_im, v_ssm_d=v_ssm_d, v_w_ssm_glu=v_w_ssm_glu, v_w_out=v_w_out, v_norm_ffn_g=v_norm_ffn_g, v_w_ffn_up=v_w_ffn_up, v_ffn_conv_w=v_ffn_conv_w, v_ffn_conv_b=v_ffn_conv_b, v_w_ffn_down=v_w_ffn_down, v_final_g=v_final_g)
    weights = {n: given[n] for n in TWIN_WEIGHTS}
    shared = {n: given[n] for n in SHARED_INPUTS}
    per_example = {n: given[n] for n in ['x', 'c']}
    grad_fn = _jax.value_and_grad(_loss, argnums=(0, 1))

    def one_microbatch(ex, loss_target):
        ex = dict(ex)
        diff = ex.pop(TWIN_DIFF_INPUT)
        return grad_fn(weights, diff, {**shared, **ex}, loss_target)

    if N_MICROBATCH == 1:
        loss, (grad_w, grad_x) = one_microbatch(per_example, given["loss_target"])
    else:
        def body(carry, xs):
            loss_sum, grad_sum = carry
            l_k, (gw_k, gx_k) = one_microbatch(xs[0], xs[1])
            with _jax.named_scope("update"):
                return (loss_sum + l_k, _jax.tree.map(_jnp.add, grad_sum, gw_k)), gx_k

        init = (_jnp.zeros((), _jnp.float32), _jax.tree.map(_jnp.zeros_like, weights))
        (loss, grad_w), grad_x = _jax.lax.scan(body, init, (per_example, given["loss_target"]))
    with _jax.named_scope("update"):
        delta_w, new_m, new_v = {}, {}, {}
        for n in TWIN_WEIGHTS:
            delta_w[n], new_m[n], new_v[n] = _adamw(weights[n], grad_w[n], given["m_" + n], given["v_" + n])
    return (loss, grad_x, *[grad_w[n] for n in TWIN_WEIGHTS], *[delta_w[n] for n in TWIN_WEIGHTS],
            *[new_m[n] for n in TWIN_WEIGHTS], *[new_v[n] for n in TWIN_WEIGHTS])
```

```python
import math

import jax
import jax.numpy as jnp
from jax import lax
from jax.experimental import pallas as pl
from jax.experimental.pallas import tpu as pltpu

F32 = jnp.float32
BF16 = jnp.bfloat16
MESH = pl.DeviceIdType.MESH

HEAD_DIM = 64
N_KV_HEADS = 2
KV_WIDTH = N_KV_HEADS * HEAD_DIM
ATTN_BLOCK = 128
NEG_INF = -1e30
SSM_GROUP = 16
SSM_STATE = 64
GROUPS_PER_TILE = 8
RMS_EPS = 1e-6
ADAM_LR = 0.001
ADAM_B1 = 0.9
ADAM_B2 = 0.999
ADAM_EPS = 1e-08
ADAM_WD = 0.01
ADAM_STEP = 10
N_CHIPS = 4
N_DEV = 8
VMEM_LIMIT_BYTES = 56 * 1024 * 1024
LANES = 128
SMALL_COLS = 1024


def _tile(dim, target, mult=LANES):
    if dim <= target:
        return dim
    for t in range(target // mult * mult, 0, -mult):
        if dim % t == 0:
            return t
    raise ValueError(f"no tile for {dim}")


def _params(sem=None):
    return pltpu.CompilerParams(dimension_semantics=sem, vmem_limit_bytes=VMEM_LIMIT_BYTES)


def _sigmoid(x):
    return 1.0 / (1.0 + jnp.exp(-x))


def _matmul(a, b, mode, name, out_dtype=BF16, tm=1024, tn=1024, tk=512):
    if mode == "nn":
        (m, k), (k2, n) = a.shape, b.shape
    elif mode == "nt":
        (m, k), (n, k2) = a.shape, b.shape
    else:
        (k, m), (k2, n) = a.shape, b.shape
    assert k == k2, (a.shape, b.shape, mode)
    tm, tn, tk = _tile(m, tm), _tile(n, tn), _tile(k, tk)
    nk = k // tk
    if mode == "tn":
        a_spec = pl.BlockSpec((tk, tm), lambda i, j, kk: (kk, i))
    else:
        a_spec = pl.BlockSpec((tm, tk), lambda i, j, kk: (i, kk))
    if mode == "nt":
        b_spec = pl.BlockSpec((tn, tk), lambda i, j, kk: (j, kk))
    else:
        b_spec = pl.BlockSpec((tk, tn), lambda i, j, kk: (kk, j))
    dims = {"nn": (((1,), (0,)), ((), ())), "nt": (((1,), (1,)), ((), ())), "tn": (((0,), (0,)), ((), ()))}[mode]

    def body(a_ref, b_ref, o_ref, acc_ref):
        kk = pl.program_id(2)

        @pl.when(kk == 0)
        def _():
            acc_ref[...] = jnp.zeros_like(acc_ref)

        acc_ref[...] += lax.dot_general(a_ref[...], b_ref[...], dims, preferred_element_type=F32)

        @pl.when(kk == nk - 1)
        def _():
            o_ref[...] = acc_ref[...].astype(o_ref.dtype)

    return pl.pallas_call(
        body,
        out_shape=jax.ShapeDtypeStruct((m, n), out_dtype),
        grid=(m // tm, n // tn, nk),
        in_specs=[a_spec, b_spec],
        out_specs=pl.BlockSpec((tm, tn), lambda i, j, kk: (i, j)),
        scratch_shapes=[pltpu.VMEM((tm, tn), F32)],
        name=name,
        compiler_params=_params(("parallel", "parallel", "arbitrary")),
    )(a, b)


def _row_spec(tl, w, col=0):
    return pl.BlockSpec((tl, w), lambda i, col=col: (i, col))


def _vec_spec(w, col=0):
    return pl.BlockSpec((1, w), lambda i, col=col: (0, col))


def _norm_mod(x, gain, mod, sc_col, sh_col, name):
    l, d = x.shape
    tl = _tile(l, 256, 8)

    def body(x_ref, g_ref, sc_ref, sh_ref, h_ref):
        xv = x_ref[...]
        r = lax.rsqrt(jnp.mean(xv * xv, axis=-1, keepdims=True) + RMS_EPS)
        h_ref[...] = ((xv * r) * g_ref[...] * (1.0 + sc_ref[...]) + sh_ref[...]).astype(BF16)

    return pl.pallas_call(
        body,
        out_shape=jax.ShapeDtypeStruct((l, d), BF16),
        grid=(l // tl,),
        in_specs=[_row_spec(tl, d), _vec_spec(d), _vec_spec(d, sc_col), _vec_spec(d, sh_col)],
        out_specs=_row_spec(tl, d),
        name=name,
        compiler_params=_params(("parallel",)),
    )(x, gain, mod, mod)


def _resid_norm_mod(x, mo, gain, mod, gate_col, sc_col, sh_col, name):
    l, d = x.shape
    tl = _tile(l, 256, 8)

    def body(x_ref, mo_ref, g_ref, gate_ref, sc_ref, sh_ref, x2_ref, h_ref):
        xv = x_ref[...] + gate_ref[...] * mo_ref[...]
        x2_ref[...] = xv
        r = lax.rsqrt(jnp.mean(xv * xv, axis=-1, keepdims=True) + RMS_EPS)
        h_ref[...] = ((xv * r) * g_ref[...] * (1.0 + sc_ref[...]) + sh_ref[...]).astype(BF16)

    return pl.pallas_call(
        body,
        out_shape=(jax.ShapeDtypeStruct((l, d), F32), jax.ShapeDtypeStruct((l, d), BF16)),
        grid=(l // tl,),
        in_specs=[_row_spec(tl, d), _row_spec(tl, d), _vec_spec(d), _vec_spec(d, gate_col), _vec_spec(d, sc_col),
                  _vec_spec(d, sh_col)],
        out_specs=(_row_spec(tl, d), _row_spec(tl, d)),
        name=name,
        compiler_params=_params(("parallel",)),
    )(x, mo, gain, mod, mod, mod)


def _final_loss(x2, f, mod, gate_col, final_g, target, name):
    l, d = x2.shape
    tl = _tile(l, 256, 8)

    def body(x2_ref, f_ref, gate_ref, fg_ref, t_ref, loss_ref, dfg_ref, dgate_ref, dx3_ref, df_ref):
        i = pl.program_id(0)
        fv = f_ref[...]
        x3 = x2_ref[...] + gate_ref[...] * fv
        r = lax.rsqrt(jnp.mean(x3 * x3, axis=-1, keepdims=True) + RMS_EPS)
        xh = x3 * r
        err = xh * fg_ref[...] - t_ref[...]
        part = 0.5 * jnp.sum(jnp.mean(err * err, axis=-1, keepdims=True), axis=0, keepdims=True)
        dout = err * (1.0 / d)
        dxh = dout * fg_ref[...]
        dx3 = r * (dxh - xh * jnp.mean(dxh * xh, axis=-1, keepdims=True))
        dx3_ref[...] = dx3
        df_ref[...] = (gate_ref[...] * dx3).astype(BF16)

        @pl.when(i == 0)
        def _():
            loss_ref[...] = jnp.zeros_like(loss_ref)
            dfg_ref[...] = jnp.zeros_like(dfg_ref)
            dgate_ref[...] = jnp.zeros_like(dgate_ref)

        loss_ref[...] += jnp.broadcast_to(part, loss_ref.shape)
        dfg_ref[...] += jnp.sum(dout * xh, axis=0, keepdims=True)
        dgate_ref[...] += jnp.sum(dx3 * fv, axis=0, keepdims=True)

    vec = pl.BlockSpec((1, d), lambda i: (0, 0))
    return pl.pallas_call(
        body,
        out_shape=(jax.ShapeDtypeStruct((1, LANES), F32), jax.ShapeDtypeStruct((1, d), F32),
                   jax.ShapeDtypeStruct((1, d), F32), jax.ShapeDtypeStruct((l, d), F32),
                   jax.ShapeDtypeStruct((l, d), BF16)),
        grid=(l // tl,),
        in_specs=[_row_spec(tl, d), _row_spec(tl, d), _vec_spec(d, gate_col), vec, _row_spec(tl, d)],
        out_specs=(pl.BlockSpec((1, LANES), lambda i: (0, 0)), vec, vec, _row_spec(tl, d), _row_spec(tl, d)),
        name=name,
        compiler_params=_params(("arbitrary",)),
    )(x2, f, mod, final_g, target)


def _norm_mod_bwd(dh, x, dx_res, gain, mod, sc_col, name, branch=None, gate_col=None):
    l, d = x.shape
    tl = _tile(l, 256, 8)
    with_gate = branch is not None

    def body(*refs):
        if with_gate:
            dh_ref, x_ref, dr_ref, g_ref, sc_ref, br_ref, gate_ref, dx_ref, dsh_ref, dsc_ref, dg_ref, dm_ref, dgate_ref = refs
        else:
            dh_ref, x_ref, dr_ref, g_ref, sc_ref, dx_ref, dsh_ref, dsc_ref, dg_ref = refs
        i = pl.program_id(0)
        xv = x_ref[...]
        dhv = dh_ref[...].astype(F32)
        r = lax.rsqrt(jnp.mean(xv * xv, axis=-1, keepdims=True) + RMS_EPS)
        xh = xv * r
        dn = dhv * (1.0 + sc_ref[...])
        dxh = dn * g_ref[...]
        dx = dr_ref[...] + r * (dxh - xh * jnp.mean(dxh * xh, axis=-1, keepdims=True))
        dx_ref[...] = dx

        @pl.when(i == 0)
        def _():
            dsh_ref[...] = jnp.zeros_like(dsh_ref)
            dsc_ref[...] = jnp.zeros_like(dsc_ref)
            dg_ref[...] = jnp.zeros_like(dg_ref)
            if with_gate:
                dgate_ref[...] = jnp.zeros_like(dgate_ref)

        dsh_ref[...] += jnp.sum(dhv, axis=0, keepdims=True)
        dsc_ref[...] += jnp.sum(dhv * (xh * g_ref[...]), axis=0, keepdims=True)
        dg_ref[...] += jnp.sum(dn * xh, axis=0, keepdims=True)
        if with_gate:
            dm_ref[...] = (gate_ref[...] * dx).astype(BF16)
            dgate_ref[...] += jnp.sum(dx * br_ref[...], axis=0, keepdims=True)

    vec = pl.BlockSpec((1, d), lambda i: (0, 0))
    in_specs = [_row_spec(tl, d), _row_spec(tl, d), _row_spec(tl, d), vec, _vec_spec(d, sc_col)]
    args = [dh, x, dx_res, gain, mod]
    out_shape = [jax.ShapeDtypeStruct((l, d), F32)] + [jax.ShapeDtypeStruct((1, d), F32)] * 3
    out_specs = [_row_spec(tl, d), vec, vec, vec]
    if with_gate:
        in_specs += [_row_spec(tl, d), _vec_spec(d, gate_col)]
        args += [branch, mod]
        out_shape += [jax.ShapeDtypeStruct((l, d), BF16), jax.ShapeDtypeStruct((1, d), F32)]
        out_specs += [_row_spec(tl, d), vec]
    return pl.pallas_call(
        body, out_shape=tuple(out_shape), grid=(l // tl,), in_specs=in_specs, out_specs=tuple(out_specs),
        name=name, compiler_params=_params(("arbitrary",)),
    )(*args)


def _attn_mask(n):
    qi = lax.broadcasted_iota(jnp.int32, (ATTN_BLOCK, 2 * ATTN_BLOCK), 0)
    kj = lax.broadcasted_iota(jnp.int32, (ATTN_BLOCK, 2 * ATTN_BLOCK), 1)
    rel = qi + ATTN_BLOCK - kj
    return (rel >= 0) & (rel < ATTN_BLOCK) & ((kj >= ATTN_BLOCK) | (n > 0))


def _attn_probs(qh, kh, sink, mask):
    s = lax.dot_general(qh, kh, (((1,), (1,)), ((), ())), preferred_element_type=F32) * (HEAD_DIM ** -0.5)
    s = jnp.where(mask, s, NEG_INF)
    m = jnp.maximum(jnp.max(s, axis=-1, keepdims=True), sink)
    p = jnp.exp(s - m)
    es = jnp.exp(sink - m)
    inv = 1.0 / (jnp.sum(p, axis=-1, keepdims=True) + es)
    return p * inv, es * inv


def _attn_specs(aw):
    kvb = aw // (2 * KV_WIDTH)
    q_spec = pl.BlockSpec((ATTN_BLOCK, aw), lambda n: (n, 0))
    kv_cur = pl.BlockSpec((ATTN_BLOCK, 2 * KV_WIDTH), lambda n: (n, kvb))
    kv_prev = pl.BlockSpec((ATTN_BLOCK, 2 * KV_WIDTH), lambda n: (jnp.maximum(n - 1, 0), kvb))
    return q_spec, kv_cur, kv_prev


def _attn_fwd(proj, sinks, aw, name):
    l = proj.shape[0]
    nq = aw // HEAD_DIM
    qpk = nq // N_KV_HEADS
    assert aw % (2 * KV_WIDTH) == 0

    def body(q_ref, kvc_ref, kvp_ref, sink_ref, o_ref):
        n = pl.program_id(0)
        mask = _attn_mask(n)
        kv = jnp.concatenate([kvp_ref[...], kvc_ref[...]], axis=0)
        for h in range(nq):
            g = h // qpk
            qh = q_ref[:, h * HEAD_DIM:(h + 1) * HEAD_DIM]
            kh = kv[:, g * HEAD_DIM:(g + 1) * HEAD_DIM]
            vh = kv[:, KV_WIDTH + g * HEAD_DIM:KV_WIDTH + (g + 1) * HEAD_DIM]
            p, _ = _attn_probs(qh, kh, sink_ref[0:1, h:h + 1], mask)
            o = jnp.dot(p.astype(BF16), vh, preferred_element_type=F32)
            o_ref[:, h * HEAD_DIM:(h + 1) * HEAD_DIM] = o.astype(BF16)

    q_spec, kv_cur, kv_prev = _attn_specs(aw)
    return pl.pallas_call(
        body,
        out_shape=jax.ShapeDtypeStruct((l, aw), BF16),
        grid=(l // ATTN_BLOCK,),
        in_specs=[q_spec, kv_cur, kv_prev, pl.BlockSpec((1, nq), lambda n: (0, 0))],
        out_specs=pl.BlockSpec((ATTN_BLOCK, aw), lambda n: (n, 0)),
        name=name,
        compiler_params=_params(("parallel",)),
    )(proj, proj, proj, sinks)


def _attn_bwd(proj, sinks, dattn, aw, name):
    l = proj.shape[0]
    nq = aw // HEAD_DIM
    qpk = nq // N_KV_HEADS
    scale = HEAD_DIM ** -0.5

    def body(q_ref, kvc_ref, kvp_ref, sink_ref, do_ref, dq_ref, dcur_ref, dprev_ref, dsink_ref):
        n = pl.program_id(0)
        mask = _attn_mask(n)
        kv = jnp.concatenate([kvp_ref[...], kvc_ref[...]], axis=0)
        lane = lax.broadcasted_iota(jnp.int32, (1, nq), 1)
        dsink = jnp.zeros((1, nq), F32)
        dks, dvs = [], []
        for g in range(N_KV_HEADS):
            kh = kv[:, g * HEAD_DIM:(g + 1) * HEAD_DIM]
            vh = kv[:, KV_WIDTH + g * HEAD_DIM:KV_WIDTH + (g + 1) * HEAD_DIM]
            dk = jnp.zeros((2 * ATTN_BLOCK, HEAD_DIM), F32)
            dv = jnp.zeros((2 * ATTN_BLOCK, HEAD_DIM), F32)
            for h in range(g * qpk, (g + 1) * qpk):
                qh = q_ref[:, h * HEAD_DIM:(h + 1) * HEAD_DIM]
                doh = do_ref[:, h * HEAD_DIM:(h + 1) * HEAD_DIM]
                p, ps = _attn_probs(qh, kh, sink_ref[0:1, h:h + 1], mask)
                pb = p.astype(BF16)
                o = jnp.dot(pb, vh, preferred_element_type=F32)
                delta = jnp.sum(doh.astype(F32) * o, axis=-1, keepdims=True)
                dp = lax.dot_general(doh, vh, (((1,), (1,)), ((), ())), preferred_element_type=F32)
                ds = (p * (dp - delta)).astype(BF16)
                dq = jnp.dot(ds, kh, preferred_element_type=F32) * scale
                dq_ref[:, h * HEAD_DIM:(h + 1) * HEAD_DIM] = dq.astype(BF16)
                dk += lax.dot_general(ds, qh, (((0,), (0,)), ((), ())), preferred_element_type=F32) * scale
                dv += lax.dot_general(pb, doh, (((0,), (0,)), ((), ())), preferred_element_type=F32)
                dsink += jnp.where(lane == h, -jnp.sum(ps * delta, axis=0, keepdims=True), 0.0)
            dks.append(dk)
            dvs.append(dv)
        dkv = jnp.concatenate(dks + dvs, axis=1)
        dprev_ref[...] = dkv[:ATTN_BLOCK]
        dcur_ref[...] = dkv[ATTN_BLOCK:]

        @pl.when(n == 0)
        def _():
            dsink_ref[...] = jnp.zeros_like(dsink_ref)

        dsink_ref[...] += dsink

    q_spec, kv_cur, kv_prev = _attn_specs(aw)
    blk = pl.BlockSpec((ATTN_BLOCK, 2 * KV_WIDTH), lambda n: (n, 0))
    return pl.pallas_call(
        body,
        out_shape=(jax.ShapeDtypeStruct((l, aw), BF16), jax.ShapeDtypeStruct((l, 2 * KV_WIDTH), F32),
                   jax.ShapeDtypeStruct((l, 2 * KV_WIDTH), F32), jax.ShapeDtypeStruct((1, nq), F32)),
        grid=(l // ATTN_BLOCK,),
        in_specs=[q_spec, kv_cur, kv_prev, pl.BlockSpec((1, nq), lambda n: (0, 0)),
                  pl.BlockSpec((ATTN_BLOCK, aw), lambda n: (n, 0))],
        out_specs=(pl.BlockSpec((ATTN_BLOCK, aw), lambda n: (n, 0)), blk, blk, pl.BlockSpec((1, nq), lambda n: (0, 0))),
        name=name,
        compiler_params=_params(("arbitrary",)),
    )(proj, proj, proj, sinks, dattn)


def _ssm_discretize(a_re, a_im, log_dt, b_re, b_im):
    dt = jnp.exp(log_dt)[:, None]
    mag = jnp.exp(a_re * dt)
    lr, li = mag * jnp.cos(a_im * dt), mag * jnp.sin(a_im * dt)
    den = a_re * a_re + a_im * a_im
    zr = ((lr - 1.0) * a_re + li * a_im) / den
    zi = (li * a_re - (lr - 1.0) * a_im) / den
    bbar_r = zr[:, :, None] * b_re - zi[:, :, None] * b_im
    bbar_i = zr[:, :, None] * b_im + zi[:, :, None] * b_re
    return lr, li, bbar_r, bbar_i


def _cmul(ar, ai, br, bi):
    return ar * br - ai * bi, ar * bi + ai * br


def _scan_tables(lr, li):
    lr, li = lr.reshape(1, -1), li.reshape(1, -1)
    pows = [(lr, li)]
    for _ in range(7):
        pows.append(_cmul(*pows[-1], lr, li))
    row = jnp.arange(8)[:, None]
    fwd, bwd = [], []
    for d in (1, 2, 4):
        pr, pi = pows[d - 1]
        fwd += [jnp.where(row >= d, pr, 0.0), jnp.where(row >= d, pi, 0.0)]
        bwd += [jnp.where(row < 8 - d, pr, 0.0), jnp.where(row < 8 - d, -pi, 0.0)]
    fwd += [jnp.concatenate([p[0] for p in pows], 0), jnp.concatenate([p[1] for p in pows], 0)]
    bwd += [jnp.concatenate([p[0] for p in pows[::-1]], 0), jnp.concatenate([-p[1] for p in pows[::-1]], 0)]
    return jnp.concatenate(fwd, 0), jnp.concatenate(bwd, 0)


def _pack_in(b):
    g, n, p = b.shape
    t = g // GROUPS_PER_TILE
    eye = jnp.eye(GROUPS_PER_TILE, dtype=b.dtype)
    bb = b.reshape(t, GROUPS_PER_TILE, n, p)
    return jnp.einsum("tgnp,gh->tgphn", bb, eye).reshape(t, GROUPS_PER_TILE * p, GROUPS_PER_TILE * n)


def _pack_out(c):
    g, p, n = c.shape
    t = g // GROUPS_PER_TILE
    eye = jnp.eye(GROUPS_PER_TILE, dtype=c.dtype)
    cc = c.reshape(t, GROUPS_PER_TILE, p, n)
    return jnp.einsum("tgpn,gh->tgnhp", cc, eye).reshape(t, GROUPS_PER_TILE * n, GROUPS_PER_TILE * p)


def _unpack_diag(x, n, p):
    t = x.shape[0]
    xx = x.reshape(t, GROUPS_PER_TILE, n, GROUPS_PER_TILE, p)
    eye = jnp.eye(GROUPS_PER_TILE, dtype=x.dtype)
    return jnp.einsum("tgnhp,gh->tgnp", xx, eye).reshape(t * GROUPS_PER_TILE, n, p)


def _scan_rows(hr_ref, hi_ref, tab_ref, l, reverse, prev_refs=None):
    w = hr_ref.shape[1]
    tabs = [tab_ref[pl.ds(8 * i, 8), :] for i in range(8)]
    nchunk = l // 8
    row = lax.broadcasted_iota(jnp.int32, (8, w), 0)

    def step(s, carry):
        k = nchunk - 1 - s if reverse else s
        t8 = pl.multiple_of(k * 8, 8)
        hr = hr_ref[pl.ds(t8, 8), :]
        hi = hi_ref[pl.ds(t8, 8), :]
        for idx, d in enumerate((1, 2, 4)):
            mr, mi = tabs[2 * idx], tabs[2 * idx + 1]
            shift = 8 - d if reverse else d
            sr = pltpu.roll(hr, shift, 0)
            si = pltpu.roll(hi, shift, 0)
            hr, hi = hr + mr * sr - mi * si, hi + mr * si + mi * sr
        cr, ci = carry[0], carry[1]
        hr, hi = hr + tabs[6] * cr - tabs[7] * ci, hi + tabs[6] * ci + tabs[7] * cr
        hr_ref[pl.ds(t8, 8), :] = hr
        hi_ref[pl.ds(t8, 8), :] = hi
        if not reverse:
            return hr[7:8, :], hi[7:8, :]
        out = (hr[0:1, :], hi[0:1, :])
        if prev_refs is None:
            return out
        fr_ref, fi_ref = prev_refs
        tp = pl.multiple_of(jnp.maximum(k - 1, 0) * 8, 8)
        keep = jnp.where(k > 0, 1.0, 0.0)
        lr_last = fr_ref[pl.ds(tp, 8), :][7:8, :] * keep
        li_last = fi_ref[pl.ds(tp, 8), :][7:8, :] * keep
        pr = jnp.where(row == 0, lr_last, pltpu.roll(fr_ref[pl.ds(t8, 8), :], 1, 0))
        pi = jnp.where(row == 0, li_last, pltpu.roll(fi_ref[pl.ds(t8, 8), :], 1, 0))
        return out + (carry[2] + hr * pr + hi * pi, carry[3] + hi * pr - hr * pi)

    zero = jnp.zeros((1, w), F32)
    init = (zero, zero)
    if reverse and prev_refs is not None:
        init += (jnp.zeros((8, w), F32), jnp.zeros((8, w), F32))
    return lax.fori_loop(0, nchunk, step, init)


def _s5_dims(sw):
    chan = GROUPS_PER_TILE * SSM_GROUP
    states = GROUPS_PER_TILE * SSM_STATE
    assert chan == LANES and sw % chan == 0
    return sw // chan, chan, states


def _s5_fwd(proj, u_off, packs, dvec, tab_f, sw, name):
    l = proj.shape[0]
    nt, chan, states = _s5_dims(sw)
    ch = _tile(l, 512, 8)
    ub = u_off // chan
    assert u_off % chan == 0

    def body(u_ref, br_ref, bi_ref, cr_ref, ci_ref, d_ref, tab_ref, y_ref, hr_ref, hi_ref):
        for i in range(l // ch):
            rows = pl.ds(i * ch, ch)
            u = u_ref[rows, :]
            hr_ref[rows, :] = jnp.dot(u, br_ref[0], preferred_element_type=F32)
            hi_ref[rows, :] = jnp.dot(u, bi_ref[0], preferred_element_type=F32)
        _scan_rows(hr_ref, hi_ref, tab_ref, l, reverse=False)
        for i in range(l // ch):
            rows = pl.ds(i * ch, ch)
            y = jnp.dot(hr_ref[rows, :].astype(BF16), cr_ref[0], preferred_element_type=F32)
            y -= jnp.dot(hi_ref[rows, :].astype(BF16), ci_ref[0], preferred_element_type=F32)
            y_ref[rows, :] = y + d_ref[...] * u_ref[rows, :].astype(F32)

    pin = pl.BlockSpec((1, chan, states), lambda t: (t, 0, 0))
    pout = pl.BlockSpec((1, states, chan), lambda t: (t, 0, 0))
    return pl.pallas_call(
        body,
        out_shape=jax.ShapeDtypeStruct((l, sw), F32),
        grid=(nt,),
        in_specs=[pl.BlockSpec((l, chan), lambda t: (0, ub + t)), pin, pin, pout, pout,
                  pl.BlockSpec((1, chan), lambda t: (0, t)), pl.BlockSpec((64, states), lambda t: (0, t))],
        out_specs=pl.BlockSpec((l, chan), lambda t: (0, t)),
        scratch_shapes=[pltpu.VMEM((l, states), F32), pltpu.VMEM((l, states), F32)],
        name=name,
        compiler_params=_params(("parallel",)),
    )(proj, packs["br"], packs["bi"], packs["cr"], packs["ci"], dvec, tab_f)


def _s5_bwd(proj, u_off, dy, packs, dvec, tab_f, tab_b, sw, name):
    l = proj.shape[0]
    nt, chan, states = _s5_dims(sw)
    ch = _tile(l, 512, 8)
    ub = u_off // chan
    tn_dims = (((0,), (0,)), ((), ()))

    def body(u_ref, dy_ref, br_ref, bi_ref, brt_ref, bit_ref, crt_ref, cit_ref, d_ref, tabf_ref, tabb_ref,
             du_ref, dlam_ref, dbr_ref, dbi_ref, dcr_ref, dci_ref, dd_ref, hr_ref, hi_ref, gr_ref, gi_ref):
        for i in range(l // ch):
            rows = pl.ds(i * ch, ch)
            u = u_ref[rows, :]
            hr_ref[rows, :] = jnp.dot(u, br_ref[0], preferred_element_type=F32)
            hi_ref[rows, :] = jnp.dot(u, bi_ref[0], preferred_element_type=F32)
            dyv = dy_ref[rows, :]
            gr_ref[rows, :] = jnp.dot(dyv, crt_ref[0], preferred_element_type=F32)
            gi_ref[rows, :] = -jnp.dot(dyv, cit_ref[0], preferred_element_type=F32)
        _scan_rows(hr_ref, hi_ref, tabf_ref, l, reverse=False)
        _, _, acc_r, acc_i = _scan_rows(gr_ref, gi_ref, tabb_ref, l, reverse=True, prev_refs=(hr_ref, hi_ref))
        dlam_ref[...] = jnp.concatenate(
            [jnp.sum(acc_r, axis=0, keepdims=True), jnp.sum(acc_i, axis=0, keepdims=True), jnp.zeros((6, states), F32)], axis=0)
        dbr_ref[...] = jnp.zeros_like(dbr_ref)
        dbi_ref[...] = jnp.zeros_like(dbi_ref)
        dcr_ref[...] = jnp.zeros_like(dcr_ref)
        dci_ref[...] = jnp.zeros_like(dci_ref)
        dd = jnp.zeros((1, chan), F32)
        for i in range(l // ch):
            rows = pl.ds(i * ch, ch)
            u = u_ref[rows, :]
            dyv = dy_ref[rows, :]
            grb = gr_ref[rows, :].astype(BF16)
            gib = gi_ref[rows, :].astype(BF16)
            dbr_ref[0] += lax.dot_general(grb, u, tn_dims, preferred_element_type=F32)
            dbi_ref[0] += lax.dot_general(gib, u, tn_dims, preferred_element_type=F32)
            dcr_ref[0] += lax.dot_general(hr_ref[rows, :].astype(BF16), dyv, tn_dims, preferred_element_type=F32)
            dci_ref[0] -= lax.dot_general(hi_ref[rows, :].astype(BF16), dyv, tn_dims, preferred_element_type=F32)
            du = jnp.dot(grb, brt_ref[0], preferred_element_type=F32) + jnp.dot(gib, bit_ref[0], preferred_element_type=F32)
            dyf = dyv.astype(F32)
            du_ref[rows, :] = (du + d_ref[...] * dyf).astype(BF16)
            dd += jnp.sum(dyf * u.astype(F32), axis=0, keepdims=True)
        dd_ref[...] = dd

    pin = pl.BlockSpec((1, chan, states), lambda t: (t, 0, 0))
    pout = pl.BlockSpec((1, states, chan), lambda t: (t, 0, 0))
    seq = pl.BlockSpec((l, chan), lambda t: (0, t))
    tab = pl.BlockSpec((64, states), lambda t: (0, t))
    vec = pl.BlockSpec((1, chan), lambda t: (0, t))
    pack_shape = jax.ShapeDtypeStruct((nt, states, chan), F32)
    return pl.pallas_call(
        body,
        out_shape=(jax.ShapeDtypeStruct((l, sw), BF16), jax.ShapeDtypeStruct((8, nt * states), F32),
                   pack_shape, pack_shape, pack_shape, pack_shape, jax.ShapeDtypeStruct((1, sw), F32)),
        grid=(nt,),
        in_specs=[pl.BlockSpec((l, chan), lambda t: (0, ub + t)), seq, pin, pin, pout, pout, pin, pin, vec, tab, tab],
        out_specs=(seq, pl.BlockSpec((8, states), lambda t: (0, t)), pout, pout, pout, pout, vec),
        scratch_shapes=[pltpu.VMEM((l, states), F32)] * 4,
        name=name,
        compiler_params=_params(("parallel",)),
    )(proj, dy, packs["br"], packs["bi"], packs["brt"], packs["bit"], packs["crt"], packs["cit"], dvec, tab_f, tab_b)


GELU_K = math.sqrt(2.0 / math.pi)
GELU_C = 0.044715


def _gelu(y, name):
    l, w = y.shape
    tl = _tile(l, 512, 8)

    def body(y_ref, o_ref):
        v = y_ref[...]
        o_ref[...] = (0.5 * v * (1.0 + jnp.tanh(GELU_K * (v + GELU_C * v * v * v)))).astype(BF16)

    return pl.pallas_call(body, out_shape=jax.ShapeDtypeStruct((l, w), BF16), grid=(l // tl,),
                          in_specs=[_row_spec(tl, w)], out_specs=_row_spec(tl, w), name=name,
                          compiler_params=_params(("parallel",)))(y)


def _gelu_bwd(y, dg, name):
    l, w = y.shape
    tl = _tile(l, 512, 8)

    def body(y_ref, dg_ref, o_ref):
        v = y_ref[...]
        t = jnp.tanh(GELU_K * (v + GELU_C * v * v * v))
        grad = 0.5 * (1.0 + t) + 0.5 * v * (1.0 - t * t) * GELU_K * (1.0 + 3.0 * GELU_C * v * v)
        o_ref[...] = (dg_ref[...].astype(F32) * grad).astype(BF16)

    return pl.pallas_call(body, out_shape=jax.ShapeDtypeStruct((l, w), BF16), grid=(l // tl,),
                          in_specs=[_row_spec(tl, w), _row_spec(tl, w)], out_specs=_row_spec(tl, w), name=name,
                          compiler_params=_params(("parallel",)))(y, dg)


MIX_COLS = 256


def _mix(proj, ga_off, gs_off, attn_out, glu, d, name):
    l = proj.shape[0]
    tl = _tile(l, 1024, 8)
    cb = MIX_COLS
    nj = d // cb
    assert d % cb == 0 and ga_off % cb == 0 and gs_off % cb == 0

    def body(ga_ref, gs_ref, a_ref, ua_ref, ub_ref, o_ref):
        ssm = ua_ref[...].astype(F32) * _sigmoid(ub_ref[...].astype(F32))
        o_ref[...] = (_sigmoid(ga_ref[...].astype(F32)) * a_ref[...].astype(F32)
                      + _sigmoid(gs_ref[...].astype(F32)) * ssm).astype(BF16)

    def spec(off):
        return pl.BlockSpec((tl, cb), lambda i, j, off=off: (i, off // cb + j))

    return pl.pallas_call(
        body, out_shape=jax.ShapeDtypeStruct((l, d), BF16), grid=(l // tl, nj),
        in_specs=[spec(ga_off), spec(gs_off), spec(0), spec(0), spec(d)], out_specs=spec(0), name=name,
        compiler_params=_params(("parallel", "parallel")),
    )(proj, proj, attn_out, glu, glu)


def _mix_bwd(proj, ga_off, gs_off, attn_out, glu, dmixed, d, name):
    l = proj.shape[0]
    tl = _tile(l, 1024, 8)
    cb = MIX_COLS
    nj = d // cb

    def body(ga_ref, gs_ref, a_ref, ua_ref, ub_ref, dm_ref, dga_ref, dgs_ref, da_ref, dglu_ref):
        s = pl.program_id(2)
        dm = dm_ref[...].astype(F32)
        sa = _sigmoid(ga_ref[...].astype(F32))
        ss = _sigmoid(gs_ref[...].astype(F32))
        sb = _sigmoid(ub_ref[...].astype(F32))
        ua = ua_ref[...].astype(F32)
        dga_ref[...] = (dm * a_ref[...].astype(F32) * sa * (1.0 - sa)).astype(BF16)
        da_ref[...] = (dm * sa).astype(BF16)
        dgs_ref[...] = (dm * (ua * sb) * ss * (1.0 - ss)).astype(BF16)
        dssm = dm * ss
        dglu_ref[...] = jnp.where(s == 0, dssm * sb, dssm * ua * sb * (1.0 - sb)).astype(BF16)

    def spec(off):
        return pl.BlockSpec((tl, cb), lambda i, j, s, off=off: (i, off // cb + j))

    out = jax.ShapeDtypeStruct((l, d), BF16)
    return pl.pallas_call(
        body, out_shape=(out, out, out, jax.ShapeDtypeStruct((l, 2 * d), BF16)), grid=(l // tl, nj, 2),
        in_specs=[spec(ga_off), spec(gs_off), spec(0), spec(0), spec(d), spec(0)],
        out_specs=(spec(0), spec(0), spec(0), pl.BlockSpec((tl, cb), lambda i, j, s: (i, j + s * nj))), name=name,
        compiler_params=_params(("parallel", "parallel", "arbitrary")),
    )(proj, proj, attn_out, glu, glu, dmixed)


CONV_COLS = 512
HALO = 16


def _shift_rows(v, k, head):
    row = lax.broadcasted_iota(jnp.int32, v.shape, 0)
    out = pltpu.roll(v, k, 0)
    for r in range(k):
        out = jnp.where(row == r, head[HALO - k + r:HALO - k + r + 1, :], out)
    return out


def _shift_rows_up(v, k, tail):
    n = v.shape[0]
    row = lax.broadcasted_iota(jnp.int32, v.shape, 0)
    out = pltpu.roll(v, n - k, 0)
    for r in range(k):
        out = jnp.where(row == n - k + r, tail[r:r + 1, :], out)
    return out


def _conv_gate(g, head, w_ref, b_ref):
    return w_ref[0:1, :] * _shift_rows(g, 2, head) + w_ref[1:2, :] * _shift_rows(g, 1, head) + w_ref[2:3, :] * g + b_ref[...]


def _conv_act(up, conv_w, conv_b, ff, name):
    l = up.shape[0]
    tl = _tile(l, 512, HALO)
    cw = _tile(ff, CONV_COLS)
    nj = ff // cw
    hb = tl // HALO

    def body(g_ref, gp_ref, v_ref, w_ref, b_ref, o_ref):
        i = pl.program_id(0)
        head = gp_ref[...].astype(F32) * jnp.where(i > 0, 1.0, 0.0)
        gc = _conv_gate(g_ref[...].astype(F32), head, w_ref, b_ref)
        o_ref[...] = (gc * _sigmoid(gc) * v_ref[...].astype(F32)).astype(BF16)

    return pl.pallas_call(
        body, out_shape=jax.ShapeDtypeStruct((l, ff), BF16), grid=(l // tl, nj),
        in_specs=[pl.BlockSpec((tl, cw), lambda i, j: (i, j)),
                  pl.BlockSpec((HALO, cw), lambda i, j: (jnp.maximum(i * hb - 1, 0), j)),
                  pl.BlockSpec((tl, cw), lambda i, j: (i, nj + j)),
                  pl.BlockSpec((3, cw), lambda i, j: (0, j)), pl.BlockSpec((1, cw), lambda i, j: (0, j))],
        out_specs=pl.BlockSpec((tl, cw), lambda i, j: (i, j)), name=name,
        compiler_params=_params(("parallel", "parallel")),
    )(up, up, up, conv_w, conv_b)


def _conv_act_bwd(up, da, conv_w, conv_b, ff, name):
    l = up.shape[0]
    tl = _tile(l, 512, HALO)
    cw = _tile(ff, CONV_COLS)
    nj = ff // cw
    hb = tl // HALO
    ni = l // tl

    def body(g_ref, gp_ref, gn_ref, v_ref, vn_ref, da_ref, dan_ref, w_ref, b_ref, dup_ref, dw_ref, db_ref):
        i = pl.program_id(0)
        s = pl.program_id(2)
        g = g_ref[...].astype(F32)
        head = gp_ref[...].astype(F32) * jnp.where(i > 0, 1.0, 0.0)
        g1 = _shift_rows(g, 1, head)
        g2 = _shift_rows(g, 2, head)
        gc = w_ref[0:1, :] * g2 + w_ref[1:2, :] * g1 + w_ref[2:3, :] * g + b_ref[...]
        sg = _sigmoid(gc)
        dav = da_ref[...].astype(F32)
        dgc = dav * v_ref[...].astype(F32) * (sg * (1.0 + gc * (1.0 - sg)))
        gn = gn_ref[...].astype(F32)
        gcn = _conv_gate(gn, g[tl - HALO:, :], w_ref, b_ref)
        sgn = _sigmoid(gcn)
        dgcn = dan_ref[...].astype(F32) * vn_ref[...].astype(F32) * (sgn * (1.0 + gcn * (1.0 - sgn)))
        dgcn = dgcn * jnp.where(i < ni - 1, 1.0, 0.0)
        dgate = w_ref[2:3, :] * dgc + w_ref[1:2, :] * _shift_rows_up(dgc, 1, dgcn) + w_ref[0:1, :] * _shift_rows_up(dgc, 2, dgcn)
        dup_ref[...] = jnp.where(s == 0, dgate, dav * (gc * sg)).astype(BF16)
        zero = jnp.zeros((1, cw), F32)
        dw_ref[...] = jnp.concatenate(
            [jnp.sum(dgc * g2, axis=0, keepdims=True), jnp.sum(dgc * g1, axis=0, keepdims=True),
             jnp.sum(dgc * g, axis=0, keepdims=True)] + [zero] * 5, axis=0)
        db_ref[...] = jnp.concatenate([jnp.sum(dgc, axis=0, keepdims=True)] + [zero] * 7, axis=0)

    def cur(off):
        return pl.BlockSpec((tl, cw), lambda i, j, s, off=off: (i, off + j))

    def prev(off):
        return pl.BlockSpec((HALO, cw), lambda i, j, s, off=off: (jnp.maximum(i * hb - 1, 0), off + j))

    def nxt(off):
        return pl.BlockSpec((HALO, cw), lambda i, j, s, off=off: (jnp.minimum((i + 1) * hb, l // HALO - 1), off + j))

    part = jax.ShapeDtypeStruct((ni * 8, ff), F32)
    part_spec = pl.BlockSpec((8, cw), lambda i, j, s: (i, j))
    return pl.pallas_call(
        body, out_shape=(jax.ShapeDtypeStruct((l, 2 * ff), BF16), part, part), grid=(ni, nj, 2),
        in_specs=[cur(0), prev(0), nxt(0), cur(nj), nxt(nj), cur(0), nxt(0),
                  pl.BlockSpec((3, cw), lambda i, j, s: (0, j)), pl.BlockSpec((1, cw), lambda i, j, s: (0, j))],
        out_specs=(pl.BlockSpec((tl, cw), lambda i, j, s: (i, j + s * nj)), part_spec, part_spec), name=name,
        compiler_params=_params(("parallel", "parallel", "arbitrary")),
    )(up, up, up, up, up, da, da, conv_w, conv_b)


def _sum_rows8(parts, name):
    n8, w = parts.shape
    n = n8 // 8
    cw = _tile(w, 2048)

    def body(p_ref, o_ref):
        acc = p_ref[0:8, :]
        for k in range(1, n):
            acc = acc + p_ref[8 * k:8 * k + 8, :]
        o_ref[...] = acc

    return pl.pallas_call(body, out_shape=jax.ShapeDtypeStruct((8, w), F32), grid=(w // cw,),
                          in_specs=[pl.BlockSpec((n8, cw), lambda j: (0, j))], out_specs=pl.BlockSpec((8, cw), lambda j: (0, j)),
                          name=name, compiler_params=_params(("parallel",)))(parts)


def _ada_fwd(c_all, w_shard, b_shard, name):
    nb, d = c_all.shape
    n = w_shard.shape[1]
    tn = _tile(n, 512)

    def body(c_ref, w_ref, b_ref, o_ref):
        cv = c_ref[...]
        cond = (cv * _sigmoid(cv)).astype(BF16)
        o_ref[...] = jnp.dot(cond, w_ref[...].astype(BF16), preferred_element_type=F32) + b_ref[...]

    return pl.pallas_call(
        body, out_shape=jax.ShapeDtypeStruct((nb, n), F32), grid=(n // tn,),
        in_specs=[pl.BlockSpec((nb, d), lambda j: (0, 0)), pl.BlockSpec((d, tn), lambda j: (0, j)),
                  pl.BlockSpec((1, tn), lambda j: (0, j))],
        out_specs=pl.BlockSpec((nb, tn), lambda j: (0, j)), name=name, compiler_params=_params(("parallel",)),
    )(c_all, w_shard, b_shard)


def _adam_update(w, g, m, v):
    m2 = ADAM_B1 * m + (1.0 - ADAM_B1) * g
    v2 = ADAM_B2 * v + (1.0 - ADAM_B2) * (g * g)
    m_hat = m2 / (1.0 - ADAM_B1 ** ADAM_STEP)
    v_hat = v2 / (1.0 - ADAM_B2 ** ADAM_STEP)
    return -ADAM_LR * (m_hat / (jnp.sqrt(v_hat) + ADAM_EPS) + ADAM_WD * w), m2, v2


def _ada_bwd_adam(c_all_t, dmod_shard, w, m, v, name):
    d, nb = c_all_t.shape
    n = w.shape[1]
    tr, tn = _tile(d, 512, 8), _tile(n, 512)

    def body(c_ref, dm_ref, w_ref, m_ref, v_ref, g_ref, dl_ref, m2_ref, v2_ref):
        cv = c_ref[...]
        cond = cv * _sigmoid(cv)
        g = cond[:, 0:1] * dm_ref[0:1, :]
        for b in range(1, nb):
            g = g + cond[:, b:b + 1] * dm_ref[b:b + 1, :]
        g_ref[...] = g
        dl_ref[...], m2_ref[...], v2_ref[...] = _adam_update(w_ref[...], g, m_ref[...], v_ref[...])

    blk = pl.BlockSpec((tr, tn), lambda i, j: (i, j))
    out = jax.ShapeDtypeStruct((d, n), F32)
    return pl.pallas_call(
        body, out_shape=(out, out, out, out), grid=(d // tr, n // tn),
        in_specs=[pl.BlockSpec((tr, nb), lambda i, j: (i, 0)), pl.BlockSpec((nb, tn), lambda i, j: (0, j)), blk, blk, blk],
        out_specs=(blk, blk, blk, blk), name=name, compiler_params=_params(("parallel", "parallel")),
    )(c_all_t, dmod_shard, w, m, v)


def _adam(w, g, m, v, name):
    r, c = w.shape
    tr = _tile(r, 256, 8)

    def body(w_ref, g_ref, m_ref, v_ref, dl_ref, m2_ref, v2_ref):
        dl_ref[...], m2_ref[...], v2_ref[...] = _adam_update(w_ref[...], g_ref[...], m_ref[...], v_ref[...])

    blk = pl.BlockSpec((tr, c), lambda i: (i, 0))
    out = jax.ShapeDtypeStruct((r, c), F32)
    return pl.pallas_call(body, out_shape=(out, out, out), grid=(r // tr,), in_specs=[blk] * 4, out_specs=(blk,) * 3,
                          name=name, compiler_params=_params(("parallel",)))(w, g, m, v)


def _sum_devices(gathered, name):
    nd, r, c = gathered.shape
    tr = _tile(r, 64, 8)

    def body(g_ref, o_ref):
        acc = g_ref[0]
        for k in range(1, nd):
            acc = acc + g_ref[k]
        o_ref[...] = acc

    return pl.pallas_call(body, out_shape=jax.ShapeDtypeStruct((r, c), F32), grid=(r // tr,),
                          in_specs=[pl.BlockSpec((nd, tr, c), lambda i: (0, i, 0))], out_specs=pl.BlockSpec((tr, c), lambda i: (i, 0)),
                          name=name, compiler_params=_params(("parallel",)))(gathered)


def _place():
    x, y, c = lax.axis_index("x"), lax.axis_index("y"), lax.axis_index("c")
    chips = [(1 - x, y), (x, 1 - y), (1 - x, 1 - y)]
    return x, y, c, chips


def _all_gather8(block, name):
    m_per, n = block.shape

    def body(x_ref, out_ref, send_sems, recv_sems, local_sem):
        x, y, c, chips = _place()
        me, sibling = (x, y, c), (x, y, 1 - c)

        def rows(px, py, pc):
            return out_ref.at[pl.ds((4 * px + 2 * py + pc) * m_per, m_per), :]

        def copy(k, blk, to, src=None):
            return pltpu.make_async_remote_copy(
                src_ref=rows(*blk) if src is None else src, dst_ref=rows(*blk), send_sem=send_sems.at[k],
                recv_sem=recv_sems.at[k], device_id=to, device_id_type=MESH)

        mine = pltpu.make_async_copy(x_ref, rows(*me), local_sem)
        mine.start()
        first = [copy(0, me, sibling, src=x_ref)]
        first += [copy(1 + j, me, (*chip, c), src=x_ref) for j, chip in enumerate(chips)]
        for cp in first:
            cp.start()
        passed = [copy(4 + j, (*chip, c), sibling) for j, chip in enumerate(chips)]
        for j, chip in enumerate(chips):
            copy(1 + j, (*chip, c), me).wait_recv()
            passed[j].start()
        copy(0, sibling, me).wait_recv()
        for j, chip in enumerate(chips):
            copy(4 + j, (*chip, 1 - c), me).wait_recv()
        for cp in first + passed:
            cp.wait_send()
        mine.wait()

    return pl.pallas_call(
        body,
        out_shape=jax.ShapeDtypeStruct((N_DEV * m_per, n), block.dtype),
        in_specs=[pl.BlockSpec(memory_space=pltpu.VMEM)],
        out_specs=pl.BlockSpec(memory_space=pltpu.VMEM),
        scratch_shapes=[pltpu.SemaphoreType.DMA((7,)), pltpu.SemaphoreType.DMA((7,)), pltpu.SemaphoreType.DMA],
        name=name,
        compiler_params=pltpu.CompilerParams(vmem_limit_bytes=VMEM_LIMIT_BYTES),
    )(block)


ANY = pl.BlockSpec(memory_space=pl.ANY)


def _gather_weights(shards, name):
    nw = len(shards)

    def body(*refs):
        ins, outs = refs[:nw], refs[nw:2 * nw]
        ici_send, ici_recv, d2d_send, d2d_recv, local_sems = refs[2 * nw:]
        x, y, c, chips = _place()
        sibling = (x, y, 1 - c)

        def piece(w, px, py, half):
            r = shards[w].shape[0]
            start = pl.multiple_of((2 * px + py) * r + half * (r // 2), 16)
            return outs[w].at[pl.ds(start, r // 2), :]

        local = []
        for w in range(nw):
            r = shards[w].shape[0]
            local.append(pltpu.make_async_copy(ins[w], outs[w].at[pl.ds(pl.multiple_of((2 * x + y) * r, 16), r), :], local_sems.at[w]))
            local[-1].start()
        sends = []
        for k, chip in enumerate(chips):
            for w in range(nw):
                h = shards[w].shape[0] // 2
                sends.append(pltpu.make_async_remote_copy(
                    src_ref=ins[w].at[pl.ds(pl.multiple_of(c * h, 16), h), :], dst_ref=piece(w, x, y, c),
                    send_sem=ici_send.at[k, w], recv_sem=ici_recv.at[k, w], device_id=(*chip, c), device_id_type=MESH))
                sends[-1].start()
        for k, chip in enumerate(chips):
            for w in range(nw):
                landed = piece(w, *chip, c)
                pltpu.make_async_remote_copy(src_ref=landed, dst_ref=landed, send_sem=ici_send.at[k, w], recv_sem=ici_recv.at[k, w],
                                             device_id=(*chip, c), device_id_type=MESH).wait_recv()
                sends.append(pltpu.make_async_remote_copy(src_ref=landed, dst_ref=landed, send_sem=d2d_send.at[k, w],
                                                          recv_sem=d2d_recv.at[k, w], device_id=sibling, device_id_type=MESH))
                sends[-1].start()
        for k, chip in enumerate(chips):
            for w in range(nw):
                theirs = piece(w, *chip, 1 - c)
                pltpu.make_async_remote_copy(src_ref=theirs, dst_ref=theirs, send_sem=d2d_send.at[k, w], recv_sem=d2d_recv.at[k, w],
                                             device_id=sibling, device_id_type=MESH).wait_recv()
        for cp in sends:
            cp.wait_send()
        for cp in local:
            cp.wait()

    sem = pltpu.SemaphoreType.DMA((3, nw))
    return pl.pallas_call(
        body,
        out_shape=tuple(jax.ShapeDtypeStruct((N_CHIPS * s.shape[0], s.shape[1]), s.dtype) for s in shards),
        in_specs=[ANY] * nw, out_specs=(ANY,) * nw,
        scratch_shapes=[sem, sem, sem, sem, pltpu.SemaphoreType.DMA((nw,))],
        name=name,
    )(*shards)


def _swap_halves(grads, name):
    nw = len(grads)

    def body(*refs):
        ins, outs = refs[:nw], refs[nw:2 * nw]
        send_sems, recv_sems = refs[2 * nw:]
        x, y, c, _ = _place()
        copies = []
        for w in range(nw):
            r = grads[w].shape[0] // N_CHIPS
            h = r // 2
            for j in range(N_CHIPS):
                copies.append(pltpu.make_async_remote_copy(
                    src_ref=ins[w].at[pl.ds(pl.multiple_of(j * r + (1 - c) * h, 16), h), :], dst_ref=outs[w].at[pl.ds(j * h, h), :],
                    send_sem=send_sems.at[w, j], recv_sem=recv_sems.at[w, j], device_id=(x, y, 1 - c), device_id_type=MESH))
                copies[-1].start()
        for cp in copies:
            cp.wait()

    sem = pltpu.SemaphoreType.DMA((nw, N_CHIPS))
    return pl.pallas_call(
        body, out_shape=tuple(jax.ShapeDtypeStruct((g.shape[0] // 2, g.shape[1]), g.dtype) for g in grads),
        in_specs=[ANY] * nw, out_specs=(ANY,) * nw, scratch_shapes=[sem, sem], name=name,
    )(*grads)


def _add_halves(grad, other, name):
    k = grad.shape[1]
    h = other.shape[0] // N_CHIPS
    tb = _tile(h, 512, 16)
    g4 = grad.reshape(N_CHIPS, 2, h, k)
    o3 = other.reshape(N_CHIPS, h, k)
    core = lax.axis_index("c").astype(jnp.int32).reshape(1)

    def body(c_ref, g_ref, o_ref, p_ref):
        p_ref[...] = (g_ref[...].astype(F32) + o_ref[...].astype(F32)).astype(BF16)

    return pl.pallas_call(
        body, out_shape=jax.ShapeDtypeStruct((N_CHIPS, h, k), BF16),
        grid_spec=pltpu.PrefetchScalarGridSpec(
            num_scalar_prefetch=1, grid=(N_CHIPS, h // tb),
            in_specs=[pl.BlockSpec((None, None, tb, k), lambda j, i, c_ref: (j, c_ref[0], i, 0)),
                      pl.BlockSpec((None, tb, k), lambda j, i, c_ref: (j, i, 0))],
            out_specs=pl.BlockSpec((None, tb, k), lambda j, i, c_ref: (j, i, 0))),
        name=name, compiler_params=_params(("parallel", "parallel")),
    )(core, g4, o3)


def _scatter_partials(partials, name):
    nw = len(partials)

    def body(*refs):
        ins, outs = refs[:nw], refs[nw:2 * nw]
        send_sems, recv_sems = refs[2 * nw:]
        x, y, c, chips = _place()
        copies = []
        for k, chip in enumerate(chips):
            for w in range(nw):
                copies.append(pltpu.make_async_remote_copy(
                    src_ref=ins[w].at[2 * chip[0] + chip[1]], dst_ref=outs[w].at[k], send_sem=send_sems.at[k, w],
                    recv_sem=recv_sems.at[k, w], device_id=(*chip, c), device_id_type=MESH))
                copies[-1].start()
        for cp in copies:
            cp.wait()

    sem = pltpu.SemaphoreType.DMA((3, nw))
    return pl.pallas_call(
        body, out_shape=tuple(jax.ShapeDtypeStruct((3,) + p.shape[1:], p.dtype) for p in partials),
        in_specs=[ANY] * nw, out_specs=(ANY,) * nw, scratch_shapes=[sem, sem], name=name,
    )(*partials)


def _add_partials(partial, others, name):
    _, h, k = partial.shape
    tb = _tile(h, 512, 16)
    chip = (2 * lax.axis_index("x") + lax.axis_index("y")).astype(jnp.int32).reshape(1)

    def body(j_ref, p_ref, o0_ref, o1_ref, o2_ref, f_ref):
        f_ref[...] = ((p_ref[...].astype(F32) + o0_ref[...].astype(F32)) + o1_ref[...].astype(F32)) + o2_ref[...].astype(F32)

    def other(s):
        return pl.BlockSpec((None, tb, k), lambda i, j_ref, s=s: (s, i, 0))

    return pl.pallas_call(
        body, out_shape=jax.ShapeDtypeStruct((h, k), F32),
        grid_spec=pltpu.PrefetchScalarGridSpec(
            num_scalar_prefetch=1, grid=(h // tb,),
            in_specs=[pl.BlockSpec((None, tb, k), lambda i, j_ref: (j_ref[0], i, 0)), other(0), other(1), other(2)],
            out_specs=pl.BlockSpec((tb, k), lambda i, j_ref: (i, 0))),
        name=name, compiler_params=_params(("parallel",)),
    )(chip, partial, others, others, others)


def _share_halves(halves, name):
    nw = len(halves)

    def body(*refs):
        ins, outs = refs[:nw], refs[nw:2 * nw]
        send_sems, recv_sems, local_sems = refs[2 * nw:]
        x, y, c, _ = _place()
        copies, local = [], []
        for w in range(nw):
            h = halves[w].shape[0]
            mine = outs[w].at[pl.ds(pl.multiple_of(c * h, 8), h), :]
            local.append(pltpu.make_async_copy(ins[w], mine, local_sems.at[w]))
            local[-1].start()
            copies.append(pltpu.make_async_remote_copy(src_ref=ins[w], dst_ref=mine, send_sem=send_sems.at[w], recv_sem=recv_sems.at[w],
                                                       device_id=(x, y, 1 - c), device_id_type=MESH))
            copies[-1].start()
        for cp in copies:
            cp.wait()
        for cp in local:
            cp.wait()

    sem = pltpu.SemaphoreType.DMA((nw,))
    return pl.pallas_call(
        body, out_shape=tuple(jax.ShapeDtypeStruct((2 * f.shape[0], f.shape[1]), f.dtype) for f in halves),
        in_specs=[ANY] * nw, out_specs=(ANY,) * nw, scratch_shapes=[sem, sem, sem], name=name,
    )(*halves)


def _flatten_pad(parts, cols=SMALL_COLS):
    flat = jnp.concatenate([p.reshape(-1) for p in parts])
    rows = -(-flat.shape[0] // (8 * cols)) * 8
    return jnp.pad(flat, (0, rows * cols - flat.shape[0])).reshape(rows, cols)


def _split_flat(buf, shapes):
    flat = buf.reshape(-1)
    out, off = [], 0
    for s in shapes:
        n = math.prod(s)
        out.append(flat[off:off + n].reshape(s))
        off += n
    return out


def _ssm_setup(ssm_a_re, ssm_a_im, ssm_log_dt, ssm_b_re, ssm_b_im, ssm_c_re, ssm_c_im):
    lam_r, lam_i, bbar_r, bbar_i = _ssm_discretize(ssm_a_re, ssm_a_im, ssm_log_dt, ssm_b_re, ssm_b_im)
    tab_f, tab_b = _scan_tables(lam_r, lam_i)
    pk = {"br": _pack_in(bbar_r), "bi": _pack_in(bbar_i), "cr": _pack_out(ssm_c_re), "ci": _pack_out(ssm_c_im)}
    packs = {k: v.astype(BF16) for k, v in pk.items()}
    packs.update({"brt": jnp.swapaxes(packs["br"], 1, 2), "bit": jnp.swapaxes(packs["bi"], 1, 2),
                  "crt": jnp.swapaxes(packs["cr"], 1, 2), "cit": jnp.swapaxes(packs["ci"], 1, 2)})
    return packs, tab_f, tab_b


def _local_step(xs, target, mod, wts, norm_mix_g, attn_sinks, ssm, norm_ffn_g, conv_w_full, ffn_conv_b, final_g, aw, sw, ff):
    l, d = xs.shape
    w_in_t, w_ap_t, w_glu_t, w_out_f, w_up_t, w_down_f = wts
    u_off = aw + 2 * KV_WIDTH
    ga_off = u_off + sw
    gs_off = ga_off + d
    packs, tab_f, tab_b = _ssm_setup(*ssm[:7])
    dvec = ssm[7].reshape(1, sw)

    h1 = _norm_mod(xs, norm_mix_g, mod, 1, 0, "norm_mod1")
    proj = _matmul(h1, w_in_t, "nt", "mm_in")
    attn = _attn_fwd(proj, attn_sinks, aw, "attn_fwd")
    attn_out = _matmul(attn, w_ap_t, "nt", "mm_attn_proj")
    ys = _s5_fwd(proj, u_off, packs, dvec, tab_f, sw, "s5_fwd")
    gy = _gelu(ys, "gelu")
    glu = _matmul(gy, w_glu_t, "nt", "mm_glu")
    mixed = _mix(proj, ga_off, gs_off, attn_out, glu, d, "mix")
    mo = _matmul(mixed, w_out_f, "nn", "mm_out", out_dtype=F32)
    x2, h2 = _resid_norm_mod(xs, mo, norm_ffn_g, mod, 2, 4, 3, "resid_norm_mod2")
    up = _matmul(h2, w_up_t, "nt", "mm_up")
    act = _conv_act(up, conv_w_full, ffn_conv_b, ff, "conv_act")
    fo = _matmul(act, w_down_f, "nn", "mm_down", out_dtype=F32)
    loss_part, d_final_g, d_gate2, dx3, dfo = _final_loss(x2, fo, mod, 5, final_g.reshape(1, d), target, "final_loss")

    dact = _matmul(dfo, w_down_f, "nt", "mm_down_dx")
    g_down = _matmul(act, dfo, "tn", "mm_down_dw")
    dup, dcw_parts, dcb_parts = _conv_act_bwd(up, dact, conv_w_full, ffn_conv_b, ff, "conv_act_bwd")
    d_conv_w = _sum_rows8(dcw_parts, "sum_conv_w")[:3]
    d_conv_b = _sum_rows8(dcb_parts, "sum_conv_b")[:1]
    dh2 = _matmul(dup, w_up_t, "nn", "mm_up_dx", out_dtype=F32)
    g_up = _matmul(dup, h2, "tn", "mm_up_dw")
    dx2, d_shift2, d_scale2, d_gain2, dmo, d_gate1 = _norm_mod_bwd(dh2, x2, dx3, norm_ffn_g, mod, 4, "norm_mod2_bwd", branch=mo, gate_col=2)
    dmixed = _matmul(dmo, w_out_f, "nt", "mm_out_dx")
    g_out = _matmul(mixed, dmo, "tn", "mm_out_dw")
    dga, dgs, dattn_out, dglu = _mix_bwd(proj, ga_off, gs_off, attn_out, glu, dmixed, d, "mix_bwd")
    dgy = _matmul(dglu, w_glu_t, "nn", "mm_glu_dx")
    g_glu = _matmul(dglu, gy, "tn", "mm_glu_dw")
    dys = _gelu_bwd(ys, dgy, "gelu_bwd")
    du, dlam, dbr_p, dbi_p, dcr_p, dci_p, d_dvec = _s5_bwd(proj, u_off, dys, packs, dvec, tab_f, tab_b, sw, "s5_bwd")
    dattn = _matmul(dattn_out, w_ap_t, "nn", "mm_attn_proj_dx")
    g_ap = _matmul(dattn_out, attn, "tn", "mm_attn_proj_dw")
    dq, dkv_cur, dkv_prev, d_sinks = _attn_bwd(proj, attn_sinks, dattn, aw, "attn_bwd")
    dkv = dkv_cur + jnp.concatenate([dkv_prev[ATTN_BLOCK:], jnp.zeros((ATTN_BLOCK, 2 * KV_WIDTH), F32)], axis=0)
    dproj = jnp.concatenate([dq, dkv.astype(BF16), du, dga, dgs], axis=1)
    dh1 = _matmul(dproj, w_in_t, "nn", "mm_in_dx", out_dtype=F32)
    g_in = _matmul(dproj, h1, "tn", "mm_in_dw")
    grad_x, d_shift1, d_scale1, d_gain1 = _norm_mod_bwd(dh1, xs, dx2, norm_mix_g, mod, 1, "norm_mod1_bwd")

    dmod = jnp.concatenate([d_shift1, d_scale1, d_gate1, d_shift2, d_scale2, d_gate2], axis=1)
    small_parts = [dmod, d_gain1, d_sinks, dlam[0], dlam[1], _unpack_diag(dbr_p, SSM_STATE, SSM_GROUP),
                   _unpack_diag(dbi_p, SSM_STATE, SSM_GROUP), _unpack_diag(dcr_p, SSM_STATE, SSM_GROUP),
                   _unpack_diag(dci_p, SSM_STATE, SSM_GROUP), d_dvec, d_gain2, d_conv_b, d_conv_w, d_final_g]
    return loss_part, grad_x, [g_in, g_ap, g_glu, g_out, g_up, g_down], small_parts


def _kernel_impl(x, c, ada_w, ada_b, norm_mix_g, w_in, attn_sinks, w_attn_proj, ssm_a_re, ssm_a_im, ssm_log_dt, ssm_b_re, ssm_b_im,
                 ssm_c_re, ssm_c_im, ssm_d, w_ssm_glu, w_out, norm_ffn_g, w_ffn_up, ffn_conv_w, ffn_conv_b, w_ffn_down, final_g,
                 loss_target, ms, vs):
    ax, ay, ac = lax.axis_index("x"), lax.axis_index("y"), lax.axis_index("c")
    chip = 2 * ax + ay
    batch_row = 4 * ax + 2 * ay + ac
    d = x.shape[2]
    aw = w_attn_proj.shape[1]
    sw = w_ssm_glu.shape[1]
    ff = N_CHIPS * ffn_conv_w.shape[2]
    ngroups = sw // SSM_GROUP

    c_all = _all_gather8(jnp.pad(c, ((0, 7), (0, 0))), "gather_c").reshape(N_DEV, 8, d)[:, 0, :]
    ncol = ada_w.shape[2]
    b_shard = lax.dynamic_slice(ada_b, (0, chip * ncol), (1, ncol))
    mod_blk = _ada_fwd(c_all, ada_w[0], b_shard, "ada_fwd")
    mod_all = _all_gather8(mod_blk, "gather_mod").reshape(N_CHIPS, 2, 8, ncol)[:, 0]
    mod = lax.dynamic_slice(mod_all, (0, batch_row, 0), (N_CHIPS, 1, ncol)).reshape(1, 6 * d)

    shards = [w_in[0].T.astype(BF16), w_attn_proj[0].T.astype(BF16), w_ssm_glu[0].T.astype(BF16), w_out[0].astype(BF16),
              w_ffn_up[0].T.astype(BF16), w_ffn_down[0].astype(BF16)]
    wts = _gather_weights(shards, "gather_weights")
    conv_w_all = _all_gather8(jnp.pad(ffn_conv_w[0], ((0, 5), (0, 0))), "gather_conv_w")
    conv_w_full = conv_w_all.reshape(N_CHIPS, 2, 8, ff // N_CHIPS)[:, 0, :3].transpose(1, 0, 2).reshape(3, ff)

    ssm = (ssm_a_re[0], ssm_a_im[0], ssm_log_dt[0], ssm_b_re[0], ssm_b_im[0], ssm_c_re[0], ssm_c_im[0], ssm_d[0])
    loss_part, grad_x, grads, small_parts = _local_step(
        x[0], loss_target[0], mod, wts, norm_mix_g, attn_sinks, ssm, norm_ffn_g, conv_w_full, ffn_conv_b, final_g, aw, sw, ff)
    loss = lax.psum(loss_part[0, 0], ("x", "y", "c"))

    from_sibling = _swap_halves(grads, "swap_halves")
    chip_sums = [_add_halves(g, o, f"add_halves_{i}") for i, (g, o) in enumerate(zip(grads, from_sibling))]
    from_chips = _scatter_partials(chip_sums, "scatter_partials")
    halves = [_add_partials(p, o, f"add_partials_{i}") for i, (p, o) in enumerate(zip(chip_sums, from_chips))]
    gi_t, gap_t, gglu_t, grad_w_out, gup_t, grad_w_down = _share_halves(halves, "share_halves")
    grad_w_in, grad_w_ap, grad_w_glu, grad_w_up = gi_t.T, gap_t.T, gglu_t.T, gup_t.T

    small_shapes = [p.shape for p in small_parts]
    part_buf = _flatten_pad(small_parts)
    rows = part_buf.shape[0]
    gathered = _all_gather8(part_buf, "gather_small").reshape(N_DEV, rows, SMALL_COLS)
    summed = _sum_devices(gathered, "sum_small")
    (s_dmod, s_gain1, s_sinks, s_lr, s_li, s_bbr, s_bbi, s_cr, s_ci, s_dd, s_gain2, s_cb, s_cw, s_fg) = _split_flat(summed, small_shapes)
    _, ssm_vjp = jax.vjp(_ssm_discretize, *ssm[:5])
    g_a_re, g_a_im, g_log_dt, g_b_re, g_b_im = ssm_vjp((s_lr.reshape(ngroups, SSM_STATE), s_li.reshape(ngroups, SSM_STATE), s_bbr, s_bbi))
    g_c_re, g_c_im = jnp.swapaxes(s_cr, 1, 2), jnp.swapaxes(s_ci, 1, 2)
    g_conv_w = lax.dynamic_slice(s_cw, (0, chip * (ff // N_CHIPS)), (3, ff // N_CHIPS))

    dmod_all = gathered.reshape(N_DEV, -1)[:, :6 * d]
    dmod_shard = lax.dynamic_slice(dmod_all, (0, chip * ncol), (N_DEV, ncol))
    ada_res = _ada_bwd_adam(c_all.T, dmod_shard, ada_w[0], ms["ada_w"][0], vs["ada_w"][0], "ada_bwd_adam")

    res = {"ada_w": tuple(o[None] for o in ada_res)}
    for nm, w, g in (("w_in", w_in, grad_w_in), ("w_attn_proj", w_attn_proj, grad_w_ap), ("w_ssm_glu", w_ssm_glu, grad_w_glu),
                     ("w_out", w_out, grad_w_out), ("w_ffn_up", w_ffn_up, grad_w_up), ("w_ffn_down", w_ffn_down, grad_w_down)):
        res[nm] = (g[None],) + tuple(o[None] for o in _adam(w[0], g, ms[nm][0], vs[nm][0], "adam_" + nm))

    small = [("ada_b", ada_b, s_dmod), ("norm_mix_g", norm_mix_g, s_gain1), ("attn_sinks", attn_sinks, s_sinks),
             ("ssm_a_re", ssm_a_re, g_a_re), ("ssm_a_im", ssm_a_im, g_a_im), ("ssm_log_dt", ssm_log_dt, g_log_dt),
             ("ssm_b_re", ssm_b_re, g_b_re), ("ssm_b_im", ssm_b_im, g_b_im), ("ssm_c_re", ssm_c_re, g_c_re),
             ("ssm_c_im", ssm_c_im, g_c_im), ("ssm_d", ssm_d, s_dd), ("norm_ffn_g", norm_ffn_g, s_gain2),
             ("ffn_conv_w", ffn_conv_w, g_conv_w), ("ffn_conv_b", ffn_conv_b, s_cb), ("final_g", final_g, s_fg)]
    shapes = [t[1].shape for t in small]
    bufs = [_flatten_pad([t[1] for t in small]), _flatten_pad([t[2] for t in small]),
            _flatten_pad([ms[t[0]] for t in small]), _flatten_pad([vs[t[0]] for t in small])]
    s_delta, s_m, s_v = _adam(*bufs, "adam_small")
    for t, dl, m2, v2 in zip(small, _split_flat(s_delta, shapes), _split_flat(s_m, shapes), _split_flat(s_v, shapes)):
        res[t[0]] = (t[2].reshape(t[1].shape), dl, m2, v2)

    outs = [loss, grad_x[None]]
    for i in range(4):
        outs += [res[nm][i] for nm in WEIGHT_ORDER]
    return tuple(outs)


WEIGHT_ORDER = ("ada_w", "ada_b", "norm_mix_g", "w_in", "attn_sinks", "w_attn_proj", "ssm_a_re", "ssm_a_im", "ssm_log_dt", "ssm_b_re",
                "ssm_b_im", "ssm_c_re", "ssm_c_im", "ssm_d", "w_ssm_glu", "w_out", "norm_ffn_g", "w_ffn_up", "ffn_conv_w", "ffn_conv_b",
                "w_ffn_down", "final_g")


def kernel(x, c, ada_w, ada_b, norm_mix_g, w_in, attn_sinks, w_attn_proj, ssm_a_re, ssm_a_im, ssm_log_dt, ssm_b_re, ssm_b_im, ssm_c_re, ssm_c_im, ssm_d, w_ssm_glu, w_out, norm_ffn_g, w_ffn_up, ffn_conv_w, ffn_conv_b, w_ffn_down, final_g, loss_target, m_ada_w, m_ada_b, m_norm_mix_g, m_w_in, m_attn_sinks, m_w_attn_proj, m_ssm_a_re, m_ssm_a_im, m_ssm_log_dt, m_ssm_b_re, m_ssm_b_im, m_ssm_c_re, m_ssm_c_im, m_ssm_d, m_w_ssm_glu, m_w_out, m_norm_ffn_g, m_w_ffn_up, m_ffn_conv_w, m_ffn_conv_b, m_w_ffn_down, m_final_g, v_ada_w, v_ada_b, v_norm_mix_g, v_w_in, v_attn_sinks, v_w_attn_proj, v_ssm_a_re, v_ssm_a_im, v_ssm_log_dt, v_ssm_b_re, v_ssm_b_im, v_ssm_c_re, v_ssm_c_im, v_ssm_d, v_w_ssm_glu, v_w_out, v_norm_ffn_g, v_w_ffn_up, v_ffn_conv_w, v_ffn_conv_b, v_w_ffn_down, v_final_g):
    ms = dict(zip(WEIGHT_ORDER, (m_ada_w, m_ada_b, m_norm_mix_g, m_w_in, m_attn_sinks, m_w_attn_proj, m_ssm_a_re, m_ssm_a_im, m_ssm_log_dt,
                                 m_ssm_b_re, m_ssm_b_im, m_ssm_c_re, m_ssm_c_im, m_ssm_d, m_w_ssm_glu, m_w_out, m_norm_ffn_g, m_w_ffn_up,
                                 m_ffn_conv_w, m_ffn_conv_b, m_w_ffn_down, m_final_g)))
    vs = dict(zip(WEIGHT_ORDER, (v_ada_w, v_ada_b, v_norm_mix_g, v_w_in, v_attn_sinks, v_w_attn_proj, v_ssm_a_re, v_ssm_a_im, v_ssm_log_dt,
                                 v_ssm_b_re, v_ssm_b_im, v_ssm_c_re, v_ssm_c_im, v_ssm_d, v_w_ssm_glu, v_w_out, v_norm_ffn_g, v_w_ffn_up,
                                 v_ffn_conv_w, v_ffn_conv_b, v_w_ffn_down, v_final_g)))
    return _kernel_impl(x, c, ada_w, ada_b, norm_mix_g, w_in, attn_sinks, w_attn_proj, ssm_a_re, ssm_a_im, ssm_log_dt, ssm_b_re, ssm_b_im,
                        ssm_c_re, ssm_c_im, ssm_d, w_ssm_glu, w_out, norm_ffn_g, w_ffn_up, ffn_conv_w, ffn_conv_b, w_ffn_down, final_g,
                        loss_target, ms, vs)
```

```python
import math

import jax
import jax.numpy as jnp
from jax import lax
from jax.experimental import pallas as pl
from jax.experimental.pallas import tpu as pltpu

F32 = jnp.float32
BF16 = jnp.bfloat16
MESH = pl.DeviceIdType.MESH

HEAD_DIM = 64
N_KV_HEADS = 2
KV_WIDTH = N_KV_HEADS * HEAD_DIM
ATTN_BLOCK = 128
NEG_INF = -1e30
SSM_GROUP = 16
SSM_STATE = 64
GROUPS_PER_TILE = 8
RMS_EPS = 1e-6
ADAM_LR = 0.001
ADAM_B1 = 0.9
ADAM_B2 = 0.999
ADAM_EPS = 1e-08
ADAM_WD = 0.01
ADAM_STEP = 10
N_CHIPS = 4
N_DEV = 8
VMEM_LIMIT_BYTES = 56 * 1024 * 1024
LANES = 128
SMALL_COLS = 1024


def _tile(dim, target, mult=LANES):
    if dim <= target:
        return dim
    for t in range(target // mult * mult, 0, -mult):
        if dim % t == 0:
            return t
    raise ValueError(f"no tile for {dim}")


def _params(sem=None):
    return pltpu.CompilerParams(dimension_semantics=sem, vmem_limit_bytes=VMEM_LIMIT_BYTES)


def _sigmoid(x):
    return 1.0 / (1.0 + jnp.exp(-x))


def _matmul(a, b, mode, name, out_dtype=BF16, tm=1536, tn=1536, tk=2048):
    if mode == "nn":
        (m, k), (k2, n) = a.shape, b.shape
    elif mode == "nt":
        (m, k), (n, k2) = a.shape, b.shape
    else:
        (k, m), (k2, n) = a.shape, b.shape
    assert k == k2, (a.shape, b.shape, mode)
    tm, tn, tk = _tile(m, tm), _tile(n, tn), _tile(k, tk)
    nk = k // tk
    if mode == "tn":
        a_spec = pl.BlockSpec((tk, tm), lambda i, j, kk: (kk, i))
    else:
        a_spec = pl.BlockSpec((tm, tk), lambda i, j, kk: (i, kk))
    if mode == "nt":
        b_spec = pl.BlockSpec((tn, tk), lambda i, j, kk: (j, kk))
    else:
        b_spec = pl.BlockSpec((tk, tn), lambda i, j, kk: (kk, j))
    dims = {"nn": (((1,), (0,)), ((), ())), "nt": (((1,), (1,)), ((), ())), "tn": (((0,), (0,)), ((), ()))}[mode]

    def body(a_ref, b_ref, o_ref, acc_ref):
        kk = pl.program_id(2)

        @pl.when(kk == 0)
        def _():
            acc_ref[...] = jnp.zeros_like(acc_ref)

        acc_ref[...] += lax.dot_general(a_ref[...], b_ref[...], dims, preferred_element_type=F32)

        @pl.when(kk == nk - 1)
        def _():
            o_ref[...] = acc_ref[...].astype(o_ref.dtype)

    return pl.pallas_call(
        body,
        out_shape=jax.ShapeDtypeStruct((m, n), out_dtype),
        grid=(m // tm, n // tn, nk),
        in_specs=[a_spec, b_spec],
        out_specs=pl.BlockSpec((tm, tn), lambda i, j, kk: (i, j)),
        scratch_shapes=[pltpu.VMEM((tm, tn), F32)],
        name=name,
        compiler_params=_params(("parallel", "parallel", "arbitrary")),
    )(a, b)


def _row_spec(tl, w, col=0):
    return pl.BlockSpec((tl, w), lambda i, col=col: (i, col))


def _vec_spec(w, col=0):
    return pl.BlockSpec((1, w), lambda i, col=col: (0, col))


def _norm_mod(x, gain, mod, sc_col, sh_col, name):
    l, d = x.shape
    tl = _tile(l, 256, 8)

    def body(x_ref, g_ref, sc_ref, sh_ref, h_ref):
        xv = x_ref[...]
        r = lax.rsqrt(jnp.mean(xv * xv, axis=-1, keepdims=True) + RMS_EPS)
        h_ref[...] = ((xv * r) * g_ref[...] * (1.0 + sc_ref[...]) + sh_ref[...]).astype(BF16)

    return pl.pallas_call(
        body,
        out_shape=jax.ShapeDtypeStruct((l, d), BF16),
        grid=(l // tl,),
        in_specs=[_row_spec(tl, d), _vec_spec(d), _vec_spec(d, sc_col), _vec_spec(d, sh_col)],
        out_specs=_row_spec(tl, d),
        name=name,
        compiler_params=_params(("parallel",)),
    )(x, gain, mod, mod)


def _resid_norm_mod(x, mo, gain, mod, gate_col, sc_col, sh_col, name):
    l, d = x.shape
    tl = _tile(l, 256, 8)

    def body(x_ref, mo_ref, g_ref, gate_ref, sc_ref, sh_ref, x2_ref, h_ref):
        xv = x_ref[...] + gate_ref[...] * mo_ref[...]
        x2_ref[...] = xv
        r = lax.rsqrt(jnp.mean(xv * xv, axis=-1, keepdims=True) + RMS_EPS)
        h_ref[...] = ((xv * r) * g_ref[...] * (1.0 + sc_ref[...]) + sh_ref[...]).astype(BF16)

    return pl.pallas_call(
        body,
        out_shape=(jax.ShapeDtypeStruct((l, d), F32), jax.ShapeDtypeStruct((l, d), BF16)),
        grid=(l // tl,),
        in_specs=[_row_spec(tl, d), _row_spec(tl, d), _vec_spec(d), _vec_spec(d, gate_col), _vec_spec(d, sc_col),
                  _vec_spec(d, sh_col)],
        out_specs=(_row_spec(tl, d), _row_spec(tl, d)),
        name=name,
        compiler_params=_params(("parallel",)),
    )(x, mo, gain, mod, mod, mod)


def _final_loss(x2, f, mod, gate_col, final_g, target, name):
    l, d = x2.shape
    tl = _tile(l, 256, 8)

    def body(x2_ref, f_ref, gate_ref, fg_ref, t_ref, loss_ref, dfg_ref, dgate_ref, dx3_ref, df_ref):
        i = pl.program_id(0)
        fv = f_ref[...]
        x3 = x2_ref[...] + gate_ref[...] * fv
        r = lax.rsqrt(jnp.mean(x3 * x3, axis=-1, keepdims=True) + RMS_EPS)
        xh = x3 * r
        err = xh * fg_ref[...] - t_ref[...]
        part = 0.5 * jnp.sum(jnp.mean(err * err, axis=-1, keepdims=True), axis=0, keepdims=True)
        dout = err * (1.0 / d)
        dxh = dout * fg_ref[...]
        dx3 = r * (dxh - xh * jnp.mean(dxh * xh, axis=-1, keepdims=True))
        dx3_ref[...] = dx3
        df_ref[...] = (gate_ref[...] * dx3).astype(BF16)

        @pl.when(i == 0)
        def _():
            loss_ref[...] = jnp.zeros_like(loss_ref)
            dfg_ref[...] = jnp.zeros_like(dfg_ref)
            dgate_ref[...] = jnp.zeros_like(dgate_ref)

        loss_ref[...] += jnp.broadcast_to(part, loss_ref.shape)
        dfg_ref[...] += jnp.sum(dout * xh, axis=0, keepdims=True)
        dgate_ref[...] += jnp.sum(dx3 * fv, axis=0, keepdims=True)

    vec = pl.BlockSpec((1, d), lambda i: (0, 0))
    return pl.pallas_call(
        body,
        out_shape=(jax.ShapeDtypeStruct((1, LANES), F32), jax.ShapeDtypeStruct((1, d), F32),
                   jax.ShapeDtypeStruct((1, d), F32), jax.ShapeDtypeStruct((l, d), F32),
                   jax.ShapeDtypeStruct((l, d), BF16)),
        grid=(l // tl,),
        in_specs=[_row_spec(tl, d), _row_spec(tl, d), _vec_spec(d, gate_col), vec, _row_spec(tl, d)],
        out_specs=(pl.BlockSpec((1, LANES), lambda i: (0, 0)), vec, vec, _row_spec(tl, d), _row_spec(tl, d)),
        name=name,
        compiler_params=_params(("arbitrary",)),
    )(x2, f, mod, final_g, target)


def _norm_mod_bwd(dh, x, dx_res, gain, mod, sc_col, name, branch=None, gate_col=None):
    l, d = x.shape
    tl = _tile(l, 256, 8)
    with_gate = branch is not None

    def body(*refs):
        if with_gate:
            dh_ref, x_ref, dr_ref, g_ref, sc_ref, br_ref, gate_ref, dx_ref, dsh_ref, dsc_ref, dg_ref, dm_ref, dgate_ref = refs
        else:
            dh_ref, x_ref, dr_ref, g_ref, sc_ref, dx_ref, dsh_ref, dsc_ref, dg_ref = refs
        i = pl.program_id(0)
        xv = x_ref[...]
        dhv = dh_ref[...].astype(F32)
        r = lax.rsqrt(jnp.mean(xv * xv, axis=-1, keepdims=True) + RMS_EPS)
        xh = xv * r
        dn = dhv * (1.0 + sc_ref[...])
        dxh = dn * g_ref[...]
        dx = dr_ref[...] + r * (dxh - xh * jnp.mean(dxh * xh, axis=-1, keepdims=True))
        dx_ref[...] = dx

        @pl.when(i == 0)
        def _():
            dsh_ref[...] = jnp.zeros_like(dsh_ref)
            dsc_ref[...] = jnp.zeros_like(dsc_ref)
            dg_ref[...] = jnp.zeros_like(dg_ref)
            if with_gate:
                dgate_ref[...] = jnp.zeros_like(dgate_ref)

        dsh_ref[...] += jnp.sum(dhv, axis=0, keepdims=True)
        dsc_ref[...] += jnp.sum(dhv * (xh * g_ref[...]), axis=0, keepdims=True)
        dg_ref[...] += jnp.sum(dn * xh, axis=0, keepdims=True)
        if with_gate:
            dm_ref[...] = (gate_ref[...] * dx).astype(BF16)
            dgate_ref[...] += jnp.sum(dx * br_ref[...], axis=0, keepdims=True)

    vec = pl.BlockSpec((1, d), lambda i: (0, 0))
    in_specs = [_row_spec(tl, d), _row_spec(tl, d), _row_spec(tl, d), vec, _vec_spec(d, sc_col)]
    args = [dh, x, dx_res, gain, mod]
    out_shape = [jax.ShapeDtypeStruct((l, d), F32)] + [jax.ShapeDtypeStruct((1, d), F32)] * 3
    out_specs = [_row_spec(tl, d), vec, vec, vec]
    if with_gate:
        in_specs += [_row_spec(tl, d), _vec_spec(d, gate_col)]
        args += [branch, mod]
        out_shape += [jax.ShapeDtypeStruct((l, d), BF16), jax.ShapeDtypeStruct((1, d), F32)]
        out_specs += [_row_spec(tl, d), vec]
    return pl.pallas_call(
        body, out_shape=tuple(out_shape), grid=(l // tl,), in_specs=in_specs, out_specs=tuple(out_specs),
        name=name, compiler_params=_params(("arbitrary",)),
    )(*args)


def _attn_mask(n):
    qi = lax.broadcasted_iota(jnp.int32, (ATTN_BLOCK, 2 * ATTN_BLOCK), 0)
    kj = lax.broadcasted_iota(jnp.int32, (ATTN_BLOCK, 2 * ATTN_BLOCK), 1)
    rel = qi + ATTN_BLOCK - kj
    return (rel >= 0) & (rel < ATTN_BLOCK) & ((kj >= ATTN_BLOCK) | (n > 0))


def _attn_probs(qh, kh, sink, mask):
    s = lax.dot_general(qh, kh, (((1,), (1,)), ((), ())), preferred_element_type=F32) * (HEAD_DIM ** -0.5)
    s = jnp.where(mask, s, NEG_INF)
    m = jnp.maximum(jnp.max(s, axis=-1, keepdims=True), sink)
    p = jnp.exp(s - m)
    es = jnp.exp(sink - m)
    inv = 1.0 / (jnp.sum(p, axis=-1, keepdims=True) + es)
    return p * inv, es * inv


def _attn_specs(aw):
    kvb = aw // (2 * KV_WIDTH)
    q_spec = pl.BlockSpec((ATTN_BLOCK, aw), lambda n: (n, 0))
    kv_cur = pl.BlockSpec((ATTN_BLOCK, 2 * KV_WIDTH), lambda n: (n, kvb))
    kv_prev = pl.BlockSpec((ATTN_BLOCK, 2 * KV_WIDTH), lambda n: (jnp.maximum(n - 1, 0), kvb))
    return q_spec, kv_cur, kv_prev


def _attn_fwd(proj, sinks, aw, name):
    l = proj.shape[0]
    nq = aw // HEAD_DIM
    qpk = nq // N_KV_HEADS
    assert aw % (2 * KV_WIDTH) == 0

    def body(q_ref, kvc_ref, kvp_ref, sink_ref, o_ref):
        n = pl.program_id(0)
        mask = _attn_mask(n)
        kv = jnp.concatenate([kvp_ref[...], kvc_ref[...]], axis=0)
        for h in range(nq):
            g = h // qpk
            qh = q_ref[:, h * HEAD_DIM:(h + 1) * HEAD_DIM]
            kh = kv[:, g * HEAD_DIM:(g + 1) * HEAD_DIM]
            vh = kv[:, KV_WIDTH + g * HEAD_DIM:KV_WIDTH + (g + 1) * HEAD_DIM]
            p, _ = _attn_probs(qh, kh, sink_ref[0:1, h:h + 1], mask)
            o = jnp.dot(p.astype(BF16), vh, preferred_element_type=F32)
            o_ref[:, h * HEAD_DIM:(h + 1) * HEAD_DIM] = o.astype(BF16)

    q_spec, kv_cur, kv_prev = _attn_specs(aw)
    return pl.pallas_call(
        body,
        out_shape=jax.ShapeDtypeStruct((l, aw), BF16),
        grid=(l // ATTN_BLOCK,),
        in_specs=[q_spec, kv_cur, kv_prev, pl.BlockSpec((1, nq), lambda n: (0, 0))],
        out_specs=pl.BlockSpec((ATTN_BLOCK, aw), lambda n: (n, 0)),
        name=name,
        compiler_params=_params(("parallel",)),
    )(proj, proj, proj, sinks)


def _attn_bwd(proj, sinks, dattn, aw, name):
    l = proj.shape[0]
    nq = aw // HEAD_DIM
    qpk = nq // N_KV_HEADS
    scale = HEAD_DIM ** -0.5

    def body(q_ref, kvc_ref, kvp_ref, sink_ref, do_ref, dq_ref, dcur_ref, dprev_ref, dsink_ref):
        n = pl.program_id(0)
        mask = _attn_mask(n)
        kv = jnp.concatenate([kvp_ref[...], kvc_ref[...]], axis=0)
        lane = lax.broadcasted_iota(jnp.int32, (1, nq), 1)
        dsink = jnp.zeros((1, nq), F32)
        dks, dvs = [], []
        for g in range(N_KV_HEADS):
            kh = kv[:, g * HEAD_DIM:(g + 1) * HEAD_DIM]
            vh = kv[:, KV_WIDTH + g * HEAD_DIM:KV_WIDTH + (g + 1) * HEAD_DIM]
            dk = jnp.zeros((2 * ATTN_BLOCK, HEAD_DIM), F32)
            dv = jnp.zeros((2 * ATTN_BLOCK, HEAD_DIM), F32)
            for h in range(g * qpk, (g + 1) * qpk):
                qh = q_ref[:, h * HEAD_DIM:(h + 1) * HEAD_DIM]
                doh = do_ref[:, h * HEAD_DIM:(h + 1) * HEAD_DIM]
                p, ps = _attn_probs(qh, kh, sink_ref[0:1, h:h + 1], mask)
                pb = p.astype(BF16)
                o = jnp.dot(pb, vh, preferred_element_type=F32)
                delta = jnp.sum(doh.astype(F32) * o, axis=-1, keepdims=True)
                dp = lax.dot_general(doh, vh, (((1,), (1,)), ((), ())), preferred_element_type=F32)
                ds = (p * (dp - delta)).astype(BF16)
                dq = jnp.dot(ds, kh, preferred_element_type=F32) * scale
                dq_ref[:, h * HEAD_DIM:(h + 1) * HEAD_DIM] = dq.astype(BF16)
                dk += lax.dot_general(ds, qh, (((0,), (0,)), ((), ())), preferred_element_type=F32) * scale
                dv += lax.dot_general(pb, doh, (((0,), (0,)), ((), ())), preferred_element_type=F32)
                dsink += jnp.where(lane == h, -jnp.sum(ps * delta, axis=0, keepdims=True), 0.0)
            dks.append(dk)
            dvs.append(dv)
        dkv = jnp.concatenate(dks + dvs, axis=1)
        dprev_ref[...] = dkv[:ATTN_BLOCK]
        dcur_ref[...] = dkv[ATTN_BLOCK:]

        @pl.when(n == 0)
        def _():
            dsink_ref[...] = jnp.zeros_like(dsink_ref)

        dsink_ref[...] += dsink

    q_spec, kv_cur, kv_prev = _attn_specs(aw)
    blk = pl.BlockSpec((ATTN_BLOCK, 2 * KV_WIDTH), lambda n: (n, 0))
    return pl.pallas_call(
        body,
        out_shape=(jax.ShapeDtypeStruct((l, aw), BF16), jax.ShapeDtypeStruct((l, 2 * KV_WIDTH), F32),
                   jax.ShapeDtypeStruct((l, 2 * KV_WIDTH), F32), jax.ShapeDtypeStruct((1, nq), F32)),
        grid=(l // ATTN_BLOCK,),
        in_specs=[q_spec, kv_cur, kv_prev, pl.BlockSpec((1, nq), lambda n: (0, 0)),
                  pl.BlockSpec((ATTN_BLOCK, aw), lambda n: (n, 0))],
        out_specs=(pl.BlockSpec((ATTN_BLOCK, aw), lambda n: (n, 0)), blk, blk, pl.BlockSpec((1, nq), lambda n: (0, 0))),
        name=name,
        compiler_params=_params(("arbitrary",)),
    )(proj, proj, proj, sinks, dattn)


def _ssm_discretize(a_re, a_im, log_dt, b_re, b_im):
    dt = jnp.exp(log_dt)[:, None]
    mag = jnp.exp(a_re * dt)
    lr, li = mag * jnp.cos(a_im * dt), mag * jnp.sin(a_im * dt)
    den = a_re * a_re + a_im * a_im
    zr = ((lr - 1.0) * a_re + li * a_im) / den
    zi = (li * a_re - (lr - 1.0) * a_im) / den
    bbar_r = zr[:, :, None] * b_re - zi[:, :, None] * b_im
    bbar_i = zr[:, :, None] * b_im + zi[:, :, None] * b_re
    return lr, li, bbar_r, bbar_i


def _cmul(ar, ai, br, bi):
    return ar * br - ai * bi, ar * bi + ai * br


def _scan_tables(lr, li):
    lr, li = lr.reshape(1, -1), li.reshape(1, -1)
    pows = [(lr, li)]
    for _ in range(7):
        pows.append(_cmul(*pows[-1], lr, li))
    row = jnp.arange(8)[:, None]
    fwd, bwd = [], []
    for d in (1, 2, 4):
        pr, pi = pows[d - 1]
        fwd += [jnp.where(row >= d, pr, 0.0), jnp.where(row >= d, pi, 0.0)]
        bwd += [jnp.where(row < 8 - d, pr, 0.0), jnp.where(row < 8 - d, -pi, 0.0)]
    fwd += [jnp.concatenate([p[0] for p in pows], 0), jnp.concatenate([p[1] for p in pows], 0)]
    bwd += [jnp.concatenate([p[0] for p in pows[::-1]], 0), jnp.concatenate([-p[1] for p in pows[::-1]], 0)]
    return jnp.concatenate(fwd, 0), jnp.concatenate(bwd, 0)


def _pack_in(b):
    g, n, p = b.shape
    t = g // GROUPS_PER_TILE
    eye = jnp.eye(GROUPS_PER_TILE, dtype=b.dtype)
    bb = b.reshape(t, GROUPS_PER_TILE, n, p)
    return jnp.einsum("tgnp,gh->tgphn", bb, eye).reshape(t, GROUPS_PER_TILE * p, GROUPS_PER_TILE * n)


def _pack_out(c):
    g, p, n = c.shape
    t = g // GROUPS_PER_TILE
    eye = jnp.eye(GROUPS_PER_TILE, dtype=c.dtype)
    cc = c.reshape(t, GROUPS_PER_TILE, p, n)
    return jnp.einsum("tgpn,gh->tgnhp", cc, eye).reshape(t, GROUPS_PER_TILE * n, GROUPS_PER_TILE * p)


def _unpack_diag(x, n, p):
    t = x.shape[0]
    xx = x.reshape(t, GROUPS_PER_TILE, n, GROUPS_PER_TILE, p)
    eye = jnp.eye(GROUPS_PER_TILE, dtype=x.dtype)
    return jnp.einsum("tgnhp,gh->tgnp", xx, eye).reshape(t * GROUPS_PER_TILE, n, p)


def _scan_rows(hr_ref, hi_ref, tab_ref, l, reverse, prev_refs=None):
    w = hr_ref.shape[1]
    tabs = [tab_ref[pl.ds(8 * i, 8), :] for i in range(8)]
    nchunk = l // 8
    row = lax.broadcasted_iota(jnp.int32, (8, w), 0)

    def step(s, carry):
        k = nchunk - 1 - s if reverse else s
        t8 = pl.multiple_of(k * 8, 8)
        hr = hr_ref[pl.ds(t8, 8), :]
        hi = hi_ref[pl.ds(t8, 8), :]
        for idx, d in enumerate((1, 2, 4)):
            mr, mi = tabs[2 * idx], tabs[2 * idx + 1]
            shift = 8 - d if reverse else d
            sr = pltpu.roll(hr, shift, 0)
            si = pltpu.roll(hi, shift, 0)
            hr, hi = hr + mr * sr - mi * si, hi + mr * si + mi * sr
        cr, ci = carry[0], carry[1]
        hr, hi = hr + tabs[6] * cr - tabs[7] * ci, hi + tabs[6] * ci + tabs[7] * cr
        hr_ref[pl.ds(t8, 8), :] = hr
        hi_ref[pl.ds(t8, 8), :] = hi
        if not reverse:
            return hr[7:8, :], hi[7:8, :]
        out = (hr[0:1, :], hi[0:1, :])
        if prev_refs is None:
            return out
        fr_ref, fi_ref = prev_refs
        tp = pl.multiple_of(jnp.maximum(k - 1, 0) * 8, 8)
        keep = jnp.where(k > 0, 1.0, 0.0)
        lr_last = fr_ref[pl.ds(tp, 8), :][7:8, :] * keep
        li_last = fi_ref[pl.ds(tp, 8), :][7:8, :] * keep
        pr = jnp.where(row == 0, lr_last, pltpu.roll(fr_ref[pl.ds(t8, 8), :], 1, 0))
        pi = jnp.where(row == 0, li_last, pltpu.roll(fi_ref[pl.ds(t8, 8), :], 1, 0))
        return out + (carry[2] + hr * pr + hi * pi, carry[3] + hi * pr - hr * pi)

    zero = jnp.zeros((1, w), F32)
    init = (zero, zero)
    if reverse and prev_refs is not None:
        init += (jnp.zeros((8, w), F32), jnp.zeros((8, w), F32))
    return lax.fori_loop(0, nchunk, step, init)


def _s5_dims(sw):
    chan = GROUPS_PER_TILE * SSM_GROUP
    states = GROUPS_PER_TILE * SSM_STATE
    assert chan == LANES and sw % chan == 0
    return sw // chan, chan, states


def _s5_fwd(proj, u_off, packs, dvec, tab_f, sw, name):
    l = proj.shape[0]
    nt, chan, states = _s5_dims(sw)
    ch = _tile(l, 512, 8)
    ub = u_off // chan
    assert u_off % chan == 0

    def body(u_ref, br_ref, bi_ref, cr_ref, ci_ref, d_ref, tab_ref, y_ref, hr_ref, hi_ref):
        for i in range(l // ch):
            rows = pl.ds(i * ch, ch)
            u = u_ref[rows, :]
            hr_ref[rows, :] = jnp.dot(u, br_ref[0], preferred_element_type=F32)
            hi_ref[rows, :] = jnp.dot(u, bi_ref[0], preferred_element_type=F32)
        _scan_rows(hr_ref, hi_ref, tab_ref, l, reverse=False)
        for i in range(l // ch):
            rows = pl.ds(i * ch, ch)
            y = jnp.dot(hr_ref[rows, :].astype(BF16), cr_ref[0], preferred_element_type=F32)
            y -= jnp.dot(hi_ref[rows, :].astype(BF16), ci_ref[0], preferred_element_type=F32)
            y_ref[rows, :] = y + d_ref[...] * u_ref[rows, :].astype(F32)

    pin = pl.BlockSpec((1, chan, states), lambda t: (t, 0, 0))
    pout = pl.BlockSpec((1, states, chan), lambda t: (t, 0, 0))
    return pl.pallas_call(
        body,
        out_shape=jax.ShapeDtypeStruct((l, sw), F32),
        grid=(nt,),
        in_specs=[pl.BlockSpec((l, chan), lambda t: (0, ub + t)), pin, pin, pout, pout,
                  pl.BlockSpec((1, chan), lambda t: (0, t)), pl.BlockSpec((64, states), lambda t: (0, t))],
        out_specs=pl.BlockSpec((l, chan), lambda t: (0, t)),
        scratch_shapes=[pltpu.VMEM((l, states), F32), pltpu.VMEM((l, states), F32)],
        name=name,
        compiler_params=_params(("parallel",)),
    )(proj, packs["br"], packs["bi"], packs["cr"], packs["ci"], dvec, tab_f)


def _s5_bwd(proj, u_off, dy, packs, dvec, tab_f, tab_b, sw, name):
    l = proj.shape[0]
    nt, chan, states = _s5_dims(sw)
    ch = _tile(l, 512, 8)
    ub = u_off // chan
    tn_dims = (((0,), (0,)), ((), ()))

    def body(u_ref, dy_ref, br_ref, bi_ref, brt_ref, bit_ref, crt_ref, cit_ref, d_ref, tabf_ref, tabb_ref,
             du_ref, dlam_ref, dbr_ref, dbi_ref, dcr_ref, dci_ref, dd_ref, hr_ref, hi_ref, gr_ref, gi_ref):
        for i in range(l // ch):
            rows = pl.ds(i * ch, ch)
            u = u_ref[rows, :]
            hr_ref[rows, :] = jnp.dot(u, br_ref[0], preferred_element_type=F32)
            hi_ref[rows, :] = jnp.dot(u, bi_ref[0], preferred_element_type=F32)
            dyv = dy_ref[rows, :]
            gr_ref[rows, :] = jnp.dot(dyv, crt_ref[0], preferred_element_type=F32)
            gi_ref[rows, :] = -jnp.dot(dyv, cit_ref[0], preferred_element_type=F32)
        _scan_rows(hr_ref, hi_ref, tabf_ref, l, reverse=False)
        _, _, acc_r, acc_i = _scan_rows(gr_ref, gi_ref, tabb_ref, l, reverse=True, prev_refs=(hr_ref, hi_ref))
        dlam_ref[...] = jnp.concatenate(
            [jnp.sum(acc_r, axis=0, keepdims=True), jnp.sum(acc_i, axis=0, keepdims=True), jnp.zeros((6, states), F32)], axis=0)
        dbr_ref[...] = jnp.zeros_like(dbr_ref)
        dbi_ref[...] = jnp.zeros_like(dbi_ref)
        dcr_ref[...] = jnp.zeros_like(dcr_ref)
        dci_ref[...] = jnp.zeros_like(dci_ref)
        dd = jnp.zeros((1, chan), F32)
        for i in range(l // ch):
            rows = pl.ds(i * ch, ch)
            u = u_ref[rows, :]
            dyv = dy_ref[rows, :]
            grb = gr_ref[rows, :].astype(BF16)
            gib = gi_ref[rows, :].astype(BF16)
            dbr_ref[0] += lax.dot_general(grb, u, tn_dims, preferred_element_type=F32)
            dbi_ref[0] += lax.dot_general(gib, u, tn_dims, preferred_element_type=F32)
            dcr_ref[0] += lax.dot_general(hr_ref[rows, :].astype(BF16), dyv, tn_dims, preferred_element_type=F32)
            dci_ref[0] -= lax.dot_general(hi_ref[rows, :].astype(BF16), dyv, tn_dims, preferred_element_type=F32)
            du = jnp.dot(grb, brt_ref[0], preferred_element_type=F32) + jnp.dot(gib, bit_ref[0], preferred_element_type=F32)
            dyf = dyv.astype(F32)
            du_ref[rows, :] = (du + d_ref[...] * dyf).astype(BF16)
            dd += jnp.sum(dyf * u.astype(F32), axis=0, keepdims=True)
        dd_ref[...] = dd

    pin = pl.BlockSpec((1, chan, states), lambda t: (t, 0, 0))
    pout = pl.BlockSpec((1, states, chan), lambda t: (t, 0, 0))
    seq = pl.BlockSpec((l, chan), lambda t: (0, t))
    tab = pl.BlockSpec((64, states), lambda t: (0, t))
    vec = pl.BlockSpec((1, chan), lambda t: (0, t))
    pack_shape = jax.ShapeDtypeStruct((nt, states, chan), F32)
    return pl.pallas_call(
        body,
        out_shape=(jax.ShapeDtypeStruct((l, sw), BF16), jax.ShapeDtypeStruct((8, nt * states), F32),
                   pack_shape, pack_shape, pack_shape, pack_shape, jax.ShapeDtypeStruct((1, sw), F32)),
        grid=(nt,),
        in_specs=[pl.BlockSpec((l, chan), lambda t: (0, ub + t)), seq, pin, pin, pout, pout, pin, pin, vec, tab, tab],
        out_specs=(seq, pl.BlockSpec((8, states), lambda t: (0, t)), pout, pout, pout, pout, vec),
        scratch_shapes=[pltpu.VMEM((l, states), F32)] * 4,
        name=name,
        compiler_params=_params(("parallel",)),
    )(proj, dy, packs["br"], packs["bi"], packs["brt"], packs["bit"], packs["crt"], packs["cit"], dvec, tab_f, tab_b)


GELU_K = math.sqrt(2.0 / math.pi)
GELU_C = 0.044715


def _gelu(y, name):
    l, w = y.shape
    tl = _tile(l, 512, 8)

    def body(y_ref, o_ref):
        v = y_ref[...]
        o_ref[...] = (0.5 * v * (1.0 + jnp.tanh(GELU_K * (v + GELU_C * v * v * v)))).astype(BF16)

    return pl.pallas_call(body, out_shape=jax.ShapeDtypeStruct((l, w), BF16), grid=(l // tl,),
                          in_specs=[_row_spec(tl, w)], out_specs=_row_spec(tl, w), name=name,
                          compiler_params=_params(("parallel",)))(y)


def _gelu_bwd(y, dg, name):
    l, w = y.shape
    tl = _tile(l, 512, 8)

    def body(y_ref, dg_ref, o_ref):
        v = y_ref[...]
        t = jnp.tanh(GELU_K * (v + GELU_C * v * v * v))
        grad = 0.5 * (1.0 + t) + 0.5 * v * (1.0 - t * t) * GELU_K * (1.0 + 3.0 * GELU_C * v * v)
        o_ref[...] = (dg_ref[...].astype(F32) * grad).astype(BF16)

    return pl.pallas_call(body, out_shape=jax.ShapeDtypeStruct((l, w), BF16), grid=(l // tl,),
                          in_specs=[_row_spec(tl, w), _row_spec(tl, w)], out_specs=_row_spec(tl, w), name=name,
                          compiler_params=_params(("parallel",)))(y, dg)


MIX_COLS = 256


def _mix(proj, ga_off, gs_off, attn_out, glu, d, name):
    l = proj.shape[0]
    tl = _tile(l, 1024, 8)
    cb = MIX_COLS
    nj = d // cb
    assert d % cb == 0 and ga_off % cb == 0 and gs_off % cb == 0

    def body(ga_ref, gs_ref, a_ref, ua_ref, ub_ref, o_ref):
        ssm = ua_ref[...].astype(F32) * _sigmoid(ub_ref[...].astype(F32))
        o_ref[...] = (_sigmoid(ga_ref[...].astype(F32)) * a_ref[...].astype(F32)
                      + _sigmoid(gs_ref[...].astype(F32)) * ssm).astype(BF16)

    def spec(off):
        return pl.BlockSpec((tl, cb), lambda i, j, off=off: (i, off // cb + j))

    return pl.pallas_call(
        body, out_shape=jax.ShapeDtypeStruct((l, d), BF16), grid=(l // tl, nj),
        in_specs=[spec(ga_off), spec(gs_off), spec(0), spec(0), spec(d)], out_specs=spec(0), name=name,
        compiler_params=_params(("parallel", "parallel")),
    )(proj, proj, attn_out, glu, glu)


def _mix_bwd(proj, ga_off, gs_off, attn_out, glu, dmixed, d, name):
    l = proj.shape[0]
    tl = _tile(l, 1024, 8)
    cb = MIX_COLS
    nj = d // cb

    def body(ga_ref, gs_ref, a_ref, ua_ref, ub_ref, dm_ref, dga_ref, dgs_ref, da_ref, dglu_ref):
        s = pl.program_id(2)
        dm = dm_ref[...].astype(F32)
        sa = _sigmoid(ga_ref[...].astype(F32))
        ss = _sigmoid(gs_ref[...].astype(F32))
        sb = _sigmoid(ub_ref[...].astype(F32))
        ua = ua_ref[...].astype(F32)
        dga_ref[...] = (dm * a_ref[...].astype(F32) * sa * (1.0 - sa)).astype(BF16)
        da_ref[...] = (dm * sa).astype(BF16)
        dgs_ref[...] = (dm * (ua * sb) * ss * (1.0 - ss)).astype(BF16)
        dssm = dm * ss
        dglu_ref[...] = jnp.where(s == 0, dssm * sb, dssm * ua * sb * (1.0 - sb)).astype(BF16)

    def spec(off):
        return pl.BlockSpec((tl, cb), lambda i, j, s, off=off: (i, off // cb + j))

    out = jax.ShapeDtypeStruct((l, d), BF16)
    return pl.pallas_call(
        body, out_shape=(out, out, out, jax.ShapeDtypeStruct((l, 2 * d), BF16)), grid=(l // tl, nj, 2),
        in_specs=[spec(ga_off), spec(gs_off), spec(0), spec(0), spec(d), spec(0)],
        out_specs=(spec(0), spec(0), spec(0), pl.BlockSpec((tl, cb), lambda i, j, s: (i, j + s * nj))), name=name,
        compiler_params=_params(("parallel", "parallel", "arbitrary")),
    )(proj, proj, attn_out, glu, glu, dmixed)


CONV_COLS = 512
HALO = 16


def _shift_rows(v, k, head):
    row = lax.broadcasted_iota(jnp.int32, v.shape, 0)
    out = pltpu.roll(v, k, 0)
    for r in range(k):
        out = jnp.where(row == r, head[HALO - k + r:HALO - k + r + 1, :], out)
    return out


def _shift_rows_up(v, k, tail):
    n = v.shape[0]
    row = lax.broadcasted_iota(jnp.int32, v.shape, 0)
    out = pltpu.roll(v, n - k, 0)
    for r in range(k):
        out = jnp.where(row == n - k + r, tail[r:r + 1, :], out)
    return out


def _conv_gate(g, head, w_ref, b_ref):
    return w_ref[0:1, :] * _shift_rows(g, 2, head) + w_ref[1:2, :] * _shift_rows(g, 1, head) + w_ref[2:3, :] * g + b_ref[...]


def _conv_act(up, conv_w, conv_b, ff, name):
    l = up.shape[0]
    tl = _tile(l, 512, HALO)
    cw = _tile(ff, CONV_COLS)
    nj = ff // cw
    hb = tl // HALO

    def body(g_ref, gp_ref, v_ref, w_ref, b_ref, o_ref):
        i = pl.program_id(0)
        head = gp_ref[...].astype(F32) * jnp.where(i > 0, 1.0, 0.0)
        gc = _conv_gate(g_ref[...].astype(F32), head, w_ref, b_ref)
        o_ref[...] = (gc * _sigmoid(gc) * v_ref[...].astype(F32)).astype(BF16)

    return pl.pallas_call(
        body, out_shape=jax.ShapeDtypeStruct((l, ff), BF16), grid=(l // tl, nj),
        in_specs=[pl.BlockSpec((tl, cw), lambda i, j: (i, j)),
                  pl.BlockSpec((HALO, cw), lambda i, j: (jnp.maximum(i * hb - 1, 0), j)),
                  pl.BlockSpec((tl, cw), lambda i, j: (i, nj + j)),
                  pl.BlockSpec((3, cw), lambda i, j: (0, j)), pl.BlockSpec((1, cw), lambda i, j: (0, j))],
        out_specs=pl.BlockSpec((tl, cw), lambda i, j: (i, j)), name=name,
        compiler_params=_params(("parallel", "parallel")),
    )(up, up, up, conv_w, conv_b)


def _conv_act_bwd(up, da, conv_w, conv_b, ff, name):
    l = up.shape[0]
    tl = _tile(l, 512, HALO)
    cw = _tile(ff, CONV_COLS)
    nj = ff // cw
    hb = tl // HALO
    ni = l // tl

    def body(g_ref, gp_ref, gn_ref, v_ref, vn_ref, da_ref, dan_ref, w_ref, b_ref, dup_ref, dw_ref, db_ref):
        i = pl.program_id(0)
        s = pl.program_id(2)
        g = g_ref[...].astype(F32)
        head = gp_ref[...].astype(F32) * jnp.where(i > 0, 1.0, 0.0)
        g1 = _shift_rows(g, 1, head)
        g2 = _shift_rows(g, 2, head)
        gc = w_ref[0:1, :] * g2 + w_ref[1:2, :] * g1 + w_ref[2:3, :] * g + b_ref[...]
        sg = _sigmoid(gc)
        dav = da_ref[...].astype(F32)
        dgc = dav * v_ref[...].astype(F32) * (sg * (1.0 + gc * (1.0 - sg)))
        gn = gn_ref[...].astype(F32)
        gcn = _conv_gate(gn, g[tl - HALO:, :], w_ref, b_ref)
        sgn = _sigmoid(gcn)
        dgcn = dan_ref[...].astype(F32) * vn_ref[...].astype(F32) * (sgn * (1.0 + gcn * (1.0 - sgn)))
        dgcn = dgcn * jnp.where(i < ni - 1, 1.0, 0.0)
        dgate = w_ref[2:3, :] * dgc + w_ref[1:2, :] * _shift_rows_up(dgc, 1, dgcn) + w_ref[0:1, :] * _shift_rows_up(dgc, 2, dgcn)
        dup_ref[...] = jnp.where(s == 0, dgate, dav * (gc * sg)).astype(BF16)
        zero = jnp.zeros((1, cw), F32)
        dw_ref[...] = jnp.concatenate(
            [jnp.sum(dgc * g2, axis=0, keepdims=True), jnp.sum(dgc * g1, axis=0, keepdims=True),
             jnp.sum(dgc * g, axis=0, keepdims=True)] + [zero] * 5, axis=0)
        db_ref[...] = jnp.concatenate([jnp.sum(dgc, axis=0, keepdims=True)] + [zero] * 7, axis=0)

    def cur(off):
        return pl.BlockSpec((tl, cw), lambda i, j, s, off=off: (i, off + j))

    def prev(off):
        return pl.BlockSpec((HALO, cw), lambda i, j, s, off=off: (jnp.maximum(i * hb - 1, 0), off + j))

    def nxt(off):
        return pl.BlockSpec((HALO, cw), lambda i, j, s, off=off: (jnp.minimum((i + 1) * hb, l // HALO - 1), off + j))

    part = jax.ShapeDtypeStruct((ni * 8, ff), F32)
    part_spec = pl.BlockSpec((8, cw), lambda i, j, s: (i, j))
    return pl.pallas_call(
        body, out_shape=(jax.ShapeDtypeStruct((l, 2 * ff), BF16), part, part), grid=(ni, nj, 2),
        in_specs=[cur(0), prev(0), nxt(0), cur(nj), nxt(nj), cur(0), nxt(0),
                  pl.BlockSpec((3, cw), lambda i, j, s: (0, j)), pl.BlockSpec((1, cw), lambda i, j, s: (0, j))],
        out_specs=(pl.BlockSpec((tl, cw), lambda i, j, s: (i, j + s * nj)), part_spec, part_spec), name=name,
        compiler_params=_params(("parallel", "parallel", "arbitrary")),
    )(up, up, up, up, up, da, da, conv_w, conv_b)


def _sum_rows8(parts, name):
    n8, w = parts.shape
    n = n8 // 8
    cw = _tile(w, 2048)

    def body(p_ref, o_ref):
        acc = p_ref[0:8, :]
        for k in range(1, n):
            acc = acc + p_ref[8 * k:8 * k + 8, :]
        o_ref[...] = acc

    return pl.pallas_call(body, out_shape=jax.ShapeDtypeStruct((8, w), F32), grid=(w // cw,),
                          in_specs=[pl.BlockSpec((n8, cw), lambda j: (0, j))], out_specs=pl.BlockSpec((8, cw), lambda j: (0, j)),
                          name=name, compiler_params=_params(("parallel",)))(parts)


def _ada_fwd(c_all, w_shard, b_shard, name):
    nb, d = c_all.shape
    n = w_shard.shape[1]
    tn = _tile(n, 512)

    def body(c_ref, w_ref, b_ref, o_ref):
        cv = c_ref[...]
        cond = (cv * _sigmoid(cv)).astype(BF16)
        o_ref[...] = jnp.dot(cond, w_ref[...].astype(BF16), preferred_element_type=F32) + b_ref[...]

    return pl.pallas_call(
        body, out_shape=jax.ShapeDtypeStruct((nb, n), F32), grid=(n // tn,),
        in_specs=[pl.BlockSpec((nb, d), lambda j: (0, 0)), pl.BlockSpec((d, tn), lambda j: (0, j)),
                  pl.BlockSpec((1, tn), lambda j: (0, j))],
        out_specs=pl.BlockSpec((nb, tn), lambda j: (0, j)), name=name, compiler_params=_params(("parallel",)),
    )(c_all, w_shard, b_shard)


def _adam_update(w, g, m, v):
    m2 = ADAM_B1 * m + (1.0 - ADAM_B1) * g
    v2 = ADAM_B2 * v + (1.0 - ADAM_B2) * (g * g)
    m_hat = m2 / (1.0 - ADAM_B1 ** ADAM_STEP)
    v_hat = v2 / (1.0 - ADAM_B2 ** ADAM_STEP)
    return -ADAM_LR * (m_hat / (jnp.sqrt(v_hat) + ADAM_EPS) + ADAM_WD * w), m2, v2


def _ada_bwd_adam(c_all_t, dmod_shard, w, m, v, name):
    d, nb = c_all_t.shape
    n = w.shape[1]
    tr, tn = _tile(d, 512, 8), _tile(n, 512)

    def body(c_ref, dm_ref, w_ref, m_ref, v_ref, g_ref, dl_ref, m2_ref, v2_ref):
        cv = c_ref[...]
        cond = cv * _sigmoid(cv)
        g = cond[:, 0:1] * dm_ref[0:1, :]
        for b in range(1, nb):
            g = g + cond[:, b:b + 1] * dm_ref[b:b + 1, :]
        g_ref[...] = g
        dl_ref[...], m2_ref[...], v2_ref[...] = _adam_update(w_ref[...], g, m_ref[...], v_ref[...])

    blk = pl.BlockSpec((tr, tn), lambda i, j: (i, j))
    out = jax.ShapeDtypeStruct((d, n), F32)
    return pl.pallas_call(
        body, out_shape=(out, out, out, out), grid=(d // tr, n // tn),
        in_specs=[pl.BlockSpec((tr, nb), lambda i, j: (i, 0)), pl.BlockSpec((nb, tn), lambda i, j: (0, j)), blk, blk, blk],
        out_specs=(blk, blk, blk, blk), name=name, compiler_params=_params(("parallel", "parallel")),
    )(c_all_t, dmod_shard, w, m, v)


def _adam(w, g, m, v, name):
    r, c = w.shape
    tr = _tile(r, 256, 8)

    def body(w_ref, g_ref, m_ref, v_ref, dl_ref, m2_ref, v2_ref):
        dl_ref[...], m2_ref[...], v2_ref[...] = _adam_update(w_ref[...], g_ref[...], m_ref[...], v_ref[...])

    blk = pl.BlockSpec((tr, c), lambda i: (i, 0))
    out = jax.ShapeDtypeStruct((r, c), F32)
    return pl.pallas_call(body, out_shape=(out, out, out), grid=(r // tr,), in_specs=[blk] * 4, out_specs=(blk,) * 3,
                          name=name, compiler_params=_params(("parallel",)))(w, g, m, v)


def _sum_devices(gathered, name):
    nd, r, c = gathered.shape
    tr = _tile(r, 64, 8)

    def body(g_ref, o_ref):
        acc = g_ref[0]
        for k in range(1, nd):
            acc = acc + g_ref[k]
        o_ref[...] = acc

    return pl.pallas_call(body, out_shape=jax.ShapeDtypeStruct((r, c), F32), grid=(r // tr,),
                          in_specs=[pl.BlockSpec((nd, tr, c), lambda i: (0, i, 0))], out_specs=pl.BlockSpec((tr, c), lambda i: (i, 0)),
                          name=name, compiler_params=_params(("parallel",)))(gathered)


def _place():
    x, y, c = lax.axis_index("x"), lax.axis_index("y"), lax.axis_index("c")
    chips = [(1 - x, y), (x, 1 - y), (1 - x, 1 - y)]
    return x, y, c, chips


def _all_gather8(block, name):
    m_per, n = block.shape

    def body(x_ref, out_ref, send_sems, recv_sems, local_sem):
        x, y, c, chips = _place()
        me, sibling = (x, y, c), (x, y, 1 - c)

        def rows(px, py, pc):
            return out_ref.at[pl.ds((4 * px + 2 * py + pc) * m_per, m_per), :]

        def copy(k, blk, to, src=None):
            return pltpu.make_async_remote_copy(
                src_ref=rows(*blk) if src is None else src, dst_ref=rows(*blk), send_sem=send_sems.at[k],
                recv_sem=recv_sems.at[k], device_id=to, device_id_type=MESH)

        mine = pltpu.make_async_copy(x_ref, rows(*me), local_sem)
        mine.start()
        first = [copy(0, me, sibling, src=x_ref)]
        first += [copy(1 + j, me, (*chip, c), src=x_ref) for j, chip in enumerate(chips)]
        for cp in first:
            cp.start()
        passed = [copy(4 + j, (*chip, c), sibling) for j, chip in enumerate(chips)]
        for j, chip in enumerate(chips):
            copy(1 + j, (*chip, c), me).wait_recv()
            passed[j].start()
        copy(0, sibling, me).wait_recv()
        for j, chip in enumerate(chips):
            copy(4 + j, (*chip, 1 - c), me).wait_recv()
        for cp in first + passed:
            cp.wait_send()
        mine.wait()

    return pl.pallas_call(
        body,
        out_shape=jax.ShapeDtypeStruct((N_DEV * m_per, n), block.dtype),
        in_specs=[pl.BlockSpec(memory_space=pltpu.VMEM)],
        out_specs=pl.BlockSpec(memory_space=pltpu.VMEM),
        scratch_shapes=[pltpu.SemaphoreType.DMA((7,)), pltpu.SemaphoreType.DMA((7,)), pltpu.SemaphoreType.DMA],
        name=name,
        compiler_params=pltpu.CompilerParams(vmem_limit_bytes=VMEM_LIMIT_BYTES),
    )(block)


ANY = pl.BlockSpec(memory_space=pl.ANY)


def _place_shard(shard, name):
    r, k = shard.shape
    tb = _tile(r, 512, 16)
    nb = r // tb
    chip = (2 * lax.axis_index("x") + lax.axis_index("y")).astype(jnp.int32).reshape(1)

    def body(j_ref, s_ref, o_ref):
        o_ref[...] = s_ref[...].astype(BF16)

    return pl.pallas_call(
        body, out_shape=jax.ShapeDtypeStruct((N_CHIPS * r, k), BF16),
        grid_spec=pltpu.PrefetchScalarGridSpec(
            num_scalar_prefetch=1, grid=(nb,),
            in_specs=[pl.BlockSpec((tb, k), lambda i, j_ref: (i, 0))],
            out_specs=pl.BlockSpec((tb, k), lambda i, j_ref: (j_ref[0] * nb + i, 0))),
        name=name, compiler_params=_params(("parallel",)),
    )(chip, shard)


def _gather_weights(placed, name):
    nw = len(placed)

    def body(*refs):
        ins, outs = refs[:nw], refs[nw:2 * nw]
        ici_send, ici_recv, d2d_send, d2d_recv = refs[2 * nw:]
        x, y, c, chips = _place()
        sibling = (x, y, 1 - c)

        def rows(w, px, py, half):
            r = placed[w].shape[0] // N_CHIPS
            return pl.ds(pl.multiple_of((2 * px + py) * r + half * (r // 2), 16), r // 2)

        def piece(w, px, py, half):
            return outs[w].at[rows(w, px, py, half), :]

        sends = []
        for k, chip in enumerate(chips):
            for w in range(nw):
                sends.append(pltpu.make_async_remote_copy(
                    src_ref=ins[w].at[rows(w, x, y, c), :], dst_ref=piece(w, x, y, c),
                    send_sem=ici_send.at[k, w], recv_sem=ici_recv.at[k, w], device_id=(*chip, c), device_id_type=MESH))
                sends[-1].start()
        for k, chip in enumerate(chips):
            for w in range(nw):
                landed = piece(w, *chip, c)
                pltpu.make_async_remote_copy(src_ref=landed, dst_ref=landed, send_sem=ici_send.at[k, w], recv_sem=ici_recv.at[k, w],
                                             device_id=(*chip, c), device_id_type=MESH).wait_recv()
                sends.append(pltpu.make_async_remote_copy(src_ref=landed, dst_ref=landed, send_sem=d2d_send.at[k, w],
                                                          recv_sem=d2d_recv.at[k, w], device_id=sibling, device_id_type=MESH))
                sends[-1].start()
        for k, chip in enumerate(chips):
            for w in range(nw):
                theirs = piece(w, *chip, 1 - c)
                pltpu.make_async_remote_copy(src_ref=theirs, dst_ref=theirs, send_sem=d2d_send.at[k, w], recv_sem=d2d_recv.at[k, w],
                                             device_id=sibling, device_id_type=MESH).wait_recv()
        for cp in sends:
            cp.wait_send()

    sem = pltpu.SemaphoreType.DMA((3, nw))
    return pl.pallas_call(
        body,
        out_shape=tuple(jax.ShapeDtypeStruct(p.shape, p.dtype) for p in placed),
        in_specs=[ANY] * nw, out_specs=(ANY,) * nw,
        scratch_shapes=[sem, sem, sem, sem],
        name=name, input_output_aliases={w: w for w in range(nw)},
    )(*placed)


def _swap_halves(grads, name):
    nw = len(grads)

    def body(*refs):
        ins, outs = refs[:nw], refs[nw:2 * nw]
        send_sems, recv_sems = refs[2 * nw:]
        x, y, c, _ = _place()
        copies = []
        for w in range(nw):
            r = grads[w].shape[0] // N_CHIPS
            h = r // 2
            for j in range(N_CHIPS):
                copies.append(pltpu.make_async_remote_copy(
                    src_ref=ins[w].at[pl.ds(pl.multiple_of(j * r + (1 - c) * h, 16), h), :], dst_ref=outs[w].at[pl.ds(j * h, h), :],
                    send_sem=send_sems.at[w, j], recv_sem=recv_sems.at[w, j], device_id=(x, y, 1 - c), device_id_type=MESH))
                copies[-1].start()
        for cp in copies:
            cp.wait()

    sem = pltpu.SemaphoreType.DMA((nw, N_CHIPS))
    return pl.pallas_call(
        body, out_shape=tuple(jax.ShapeDtypeStruct((g.shape[0] // 2, g.shape[1]), g.dtype) for g in grads),
        in_specs=[ANY] * nw, out_specs=(ANY,) * nw, scratch_shapes=[sem, sem], name=name,
    )(*grads)


def _add_halves(grad, other, name):
    k = grad.shape[1]
    h = other.shape[0] // N_CHIPS
    tb = _tile(h, 512, 16)
    g4 = grad.reshape(N_CHIPS, 2, h, k)
    o3 = other.reshape(N_CHIPS, h, k)
    core = lax.axis_index("c").astype(jnp.int32).reshape(1)

    def body(c_ref, g_ref, o_ref, p_ref):
        p_ref[...] = (g_ref[...].astype(F32) + o_ref[...].astype(F32)).astype(BF16)

    return pl.pallas_call(
        body, out_shape=jax.ShapeDtypeStruct((N_CHIPS, h, k), BF16),
        grid_spec=pltpu.PrefetchScalarGridSpec(
            num_scalar_prefetch=1, grid=(N_CHIPS, h // tb),
            in_specs=[pl.BlockSpec((None, None, tb, k), lambda j, i, c_ref: (j, c_ref[0], i, 0)),
                      pl.BlockSpec((None, tb, k), lambda j, i, c_ref: (j, i, 0))],
            out_specs=pl.BlockSpec((None, tb, k), lambda j, i, c_ref: (j, i, 0))),
        name=name, compiler_params=_params(("parallel", "parallel")),
    )(core, g4, o3)


def _scatter_partials(partials, name):
    nw = len(partials)

    def body(*refs):
        ins, outs = refs[:nw], refs[nw:2 * nw]
        send_sems, recv_sems = refs[2 * nw:]
        x, y, c, chips = _place()
        copies = []
        for k, chip in enumerate(chips):
            for w in range(nw):
                copies.append(pltpu.make_async_remote_copy(
                    src_ref=ins[w].at[2 * chip[0] + chip[1]], dst_ref=outs[w].at[k], send_sem=send_sems.at[k, w],
                    recv_sem=recv_sems.at[k, w], device_id=(*chip, c), device_id_type=MESH))
                copies[-1].start()
        for cp in copies:
            cp.wait()

    sem = pltpu.SemaphoreType.DMA((3, nw))
    return pl.pallas_call(
        body, out_shape=tuple(jax.ShapeDtypeStruct((3,) + p.shape[1:], p.dtype) for p in partials),
        in_specs=[ANY] * nw, out_specs=(ANY,) * nw, scratch_shapes=[sem, sem], name=name,
    )(*partials)


def _add_partials(partial, others, name):
    _, h, k = partial.shape
    tb = _tile(h, 512, 16)
    nb = h // tb
    place = jnp.stack([2 * lax.axis_index("x") + lax.axis_index("y"), lax.axis_index("c")]).astype(jnp.int32)

    def body(s_ref, p_ref, o0_ref, o1_ref, o2_ref, f_ref):
        f_ref[...] = ((p_ref[...].astype(F32) + o0_ref[...].astype(F32)) + o1_ref[...].astype(F32)) + o2_ref[...].astype(F32)

    def other(s):
        return pl.BlockSpec((None, tb, k), lambda i, s_ref, s=s: (s, i, 0))

    return pl.pallas_call(
        body, out_shape=jax.ShapeDtypeStruct((2 * h, k), F32),
        grid_spec=pltpu.PrefetchScalarGridSpec(
            num_scalar_prefetch=1, grid=(nb,),
            in_specs=[pl.BlockSpec((None, tb, k), lambda i, s_ref: (s_ref[0], i, 0)), other(0), other(1), other(2)],
            out_specs=pl.BlockSpec((tb, k), lambda i, s_ref: (s_ref[1] * nb + i, 0))),
        name=name, compiler_params=_params(("parallel",)),
    )(place, partial, others, others, others)


def _share_halves(fulls, name):
    nw = len(fulls)

    def body(*refs):
        ins, outs = refs[:nw], refs[nw:2 * nw]
        send_sems, recv_sems = refs[2 * nw:]
        x, y, c, _ = _place()
        copies = []
        for w in range(nw):
            h = fulls[w].shape[0] // 2
            start = pl.multiple_of(c * h, 8)
            copies.append(pltpu.make_async_remote_copy(
                src_ref=ins[w].at[pl.ds(start, h), :], dst_ref=outs[w].at[pl.ds(start, h), :], send_sem=send_sems.at[w],
                recv_sem=recv_sems.at[w], device_id=(x, y, 1 - c), device_id_type=MESH))
            copies[-1].start()
        for cp in copies:
            cp.wait()

    sem = pltpu.SemaphoreType.DMA((nw,))
    return pl.pallas_call(
        body, out_shape=tuple(jax.ShapeDtypeStruct(f.shape, f.dtype) for f in fulls),
        in_specs=[ANY] * nw, out_specs=(ANY,) * nw, scratch_shapes=[sem, sem], name=name,
        input_output_aliases={w: w for w in range(nw)},
    )(*fulls)


def _flatten_pad(parts, cols=SMALL_COLS):
    flat = jnp.concatenate([p.reshape(-1) for p in parts])
    rows = -(-flat.shape[0] // (8 * cols)) * 8
    return jnp.pad(flat, (0, rows * cols - flat.shape[0])).reshape(rows, cols)


def _split_flat(buf, shapes):
    flat = buf.reshape(-1)
    out, off = [], 0
    for s in shapes:
        n = math.prod(s)
        out.append(flat[off:off + n].reshape(s))
        off += n
    return out


def _ssm_setup(ssm_a_re, ssm_a_im, ssm_log_dt, ssm_b_re, ssm_b_im, ssm_c_re, ssm_c_im):
    lam_r, lam_i, bbar_r, bbar_i = _ssm_discretize(ssm_a_re, ssm_a_im, ssm_log_dt, ssm_b_re, ssm_b_im)
    tab_f, tab_b = _scan_tables(lam_r, lam_i)
    pk = {"br": _pack_in(bbar_r), "bi": _pack_in(bbar_i), "cr": _pack_out(ssm_c_re), "ci": _pack_out(ssm_c_im)}
    packs = {k: v.astype(BF16) for k, v in pk.items()}
    packs.update({"brt": jnp.swapaxes(packs["br"], 1, 2), "bit": jnp.swapaxes(packs["bi"], 1, 2),
                  "crt": jnp.swapaxes(packs["cr"], 1, 2), "cit": jnp.swapaxes(packs["ci"], 1, 2)})
    return packs, tab_f, tab_b


def _local_step(xs, target, mod, wts, norm_mix_g, attn_sinks, ssm, norm_ffn_g, conv_w_full, ffn_conv_b, final_g, aw, sw, ff):
    l, d = xs.shape
    w_in_t, w_ap_t, w_glu_t, w_out_f, w_up_t, w_down_f = wts
    u_off = aw + 2 * KV_WIDTH
    ga_off = u_off + sw
    gs_off = ga_off + d
    packs, tab_f, tab_b = _ssm_setup(*ssm[:7])
    dvec = ssm[7].reshape(1, sw)

    h1 = _norm_mod(xs, norm_mix_g, mod, 1, 0, "norm_mod1")
    proj = _matmul(h1, w_in_t, "nt", "mm_in")
    attn = _attn_fwd(proj, attn_sinks, aw, "attn_fwd")
    attn_out = _matmul(attn, w_ap_t, "nt", "mm_attn_proj")
    ys = _s5_fwd(proj, u_off, packs, dvec, tab_f, sw, "s5_fwd")
    gy = _gelu(ys, "gelu")
    glu = _matmul(gy, w_glu_t, "nt", "mm_glu")
    mixed = _mix(proj, ga_off, gs_off, attn_out, glu, d, "mix")
    mo = _matmul(mixed, w_out_f, "nn", "mm_out", out_dtype=F32)
    x2, h2 = _resid_norm_mod(xs, mo, norm_ffn_g, mod, 2, 4, 3, "resid_norm_mod2")
    up = _matmul(h2, w_up_t, "nt", "mm_up")
    act = _conv_act(up, conv_w_full, ffn_conv_b, ff, "conv_act")
    fo = _matmul(act, w_down_f, "nn", "mm_down", out_dtype=F32)
    loss_part, d_final_g, d_gate2, dx3, dfo = _final_loss(x2, fo, mod, 5, final_g.reshape(1, d), target, "final_loss")

    dact = _matmul(dfo, w_down_f, "nt", "mm_down_dx")
    g_down = _matmul(act, dfo, "tn", "mm_down_dw")
    dup, dcw_parts, dcb_parts = _conv_act_bwd(up, dact, conv_w_full, ffn_conv_b, ff, "conv_act_bwd")
    d_conv_w = _sum_rows8(dcw_parts, "sum_conv_w")[:3]
    d_conv_b = _sum_rows8(dcb_parts, "sum_conv_b")[:1]
    dh2 = _matmul(dup, w_up_t, "nn", "mm_up_dx", out_dtype=F32)
    g_up = _matmul(dup, h2, "tn", "mm_up_dw")
    dx2, d_shift2, d_scale2, d_gain2, dmo, d_gate1 = _norm_mod_bwd(dh2, x2, dx3, norm_ffn_g, mod, 4, "norm_mod2_bwd", branch=mo, gate_col=2)
    dmixed = _matmul(dmo, w_out_f, "nt", "mm_out_dx")
    g_out = _matmul(mixed, dmo, "tn", "mm_out_dw")
    dga, dgs, dattn_out, dglu = _mix_bwd(proj, ga_off, gs_off, attn_out, glu, dmixed, d, "mix_bwd")
    dgy = _matmul(dglu, w_glu_t, "nn", "mm_glu_dx")
    g_glu = _matmul(dglu, gy, "tn", "mm_glu_dw")
    dys = _gelu_bwd(ys, dgy, "gelu_bwd")
    du, dlam, dbr_p, dbi_p, dcr_p, dci_p, d_dvec = _s5_bwd(proj, u_off, dys, packs, dvec, tab_f, tab_b, sw, "s5_bwd")
    dattn = _matmul(dattn_out, w_ap_t, "nn", "mm_attn_proj_dx")
    g_ap = _matmul(dattn_out, attn, "tn", "mm_attn_proj_dw")
    dq, dkv_cur, dkv_prev, d_sinks = _attn_bwd(proj, attn_sinks, dattn, aw, "attn_bwd")
    dkv = dkv_cur + jnp.concatenate([dkv_prev[ATTN_BLOCK:], jnp.zeros((ATTN_BLOCK, 2 * KV_WIDTH), F32)], axis=0)
    dproj = jnp.concatenate([dq, dkv.astype(BF16), du, dga, dgs], axis=1)
    dh1 = _matmul(dproj, w_in_t, "nn", "mm_in_dx", out_dtype=F32)
    g_in = _matmul(dproj, h1, "tn", "mm_in_dw")
    grad_x, d_shift1, d_scale1, d_gain1 = _norm_mod_bwd(dh1, xs, dx2, norm_mix_g, mod, 1, "norm_mod1_bwd")

    dmod = jnp.concatenate([d_shift1, d_scale1, d_gate1, d_shift2, d_scale2, d_gate2], axis=1)
    small_parts = [dmod, d_gain1, d_sinks, dlam[0], dlam[1], _unpack_diag(dbr_p, SSM_STATE, SSM_GROUP),
                   _unpack_diag(dbi_p, SSM_STATE, SSM_GROUP), _unpack_diag(dcr_p, SSM_STATE, SSM_GROUP),
                   _unpack_diag(dci_p, SSM_STATE, SSM_GROUP), d_dvec, d_gain2, d_conv_b, d_conv_w, d_final_g]
    return loss_part, grad_x, [g_in, g_ap, g_glu, g_out, g_up, g_down], small_parts


def _kernel_impl(x, c, ada_w, ada_b, norm_mix_g, w_in, attn_sinks, w_attn_proj, ssm_a_re, ssm_a_im, ssm_log_dt, ssm_b_re, ssm_b_im,
                 ssm_c_re, ssm_c_im, ssm_d, w_ssm_glu, w_out, norm_ffn_g, w_ffn_up, ffn_conv_w, ffn_conv_b, w_ffn_down, final_g,
                 loss_target, ms, vs):
    ax, ay, ac = lax.axis_index("x"), lax.axis_index("y"), lax.axis_index("c")
    chip = 2 * ax + ay
    batch_row = 4 * ax + 2 * ay + ac
    d = x.shape[2]
    aw = w_attn_proj.shape[1]
    sw = w_ssm_glu.shape[1]
    ff = N_CHIPS * ffn_conv_w.shape[2]
    ngroups = sw // SSM_GROUP

    c_all = _all_gather8(jnp.pad(c, ((0, 7), (0, 0))), "gather_c").reshape(N_DEV, 8, d)[:, 0, :]
    ncol = ada_w.shape[2]
    b_shard = lax.dynamic_slice(ada_b, (0, chip * ncol), (1, ncol))
    mod_blk = _ada_fwd(c_all, ada_w[0], b_shard, "ada_fwd")
    mod_all = _all_gather8(mod_blk, "gather_mod").reshape(N_CHIPS, 2, 8, ncol)[:, 0]
    mod = lax.dynamic_slice(mod_all, (0, batch_row, 0), (N_CHIPS, 1, ncol)).reshape(1, 6 * d)

    shards = [w_in[0].T.astype(BF16), w_attn_proj[0].T.astype(BF16), w_ssm_glu[0].T.astype(BF16), w_out[0],
              w_ffn_up[0].T.astype(BF16), w_ffn_down[0]]
    wts = _gather_weights([_place_shard(s, f"place_shard_{i}") for i, s in enumerate(shards)], "gather_weights")
    conv_w_all = _all_gather8(jnp.pad(ffn_conv_w[0], ((0, 5), (0, 0))), "gather_conv_w")
    conv_w_full = conv_w_all.reshape(N_CHIPS, 2, 8, ff // N_CHIPS)[:, 0, :3].transpose(1, 0, 2).reshape(3, ff)

    ssm = (ssm_a_re[0], ssm_a_im[0], ssm_log_dt[0], ssm_b_re[0], ssm_b_im[0], ssm_c_re[0], ssm_c_im[0], ssm_d[0])
    loss_part, grad_x, grads, small_parts = _local_step(
        x[0], loss_target[0], mod, wts, norm_mix_g, attn_sinks, ssm, norm_ffn_g, conv_w_full, ffn_conv_b, final_g, aw, sw, ff)
    loss = lax.psum(loss_part[0, 0], ("x", "y", "c"))

    from_sibling = _swap_halves(grads, "swap_halves")
    chip_sums = [_add_halves(g, o, f"add_halves_{i}") for i, (g, o) in enumerate(zip(grads, from_sibling))]
    from_chips = _scatter_partials(chip_sums, "scatter_partials")
    fulls = [_add_partials(p, o, f"add_partials_{i}") for i, (p, o) in enumerate(zip(chip_sums, from_chips))]
    gi_t, gap_t, gglu_t, grad_w_out, gup_t, grad_w_down = _share_halves(fulls, "share_halves")
    grad_w_in, grad_w_ap, grad_w_glu, grad_w_up = gi_t.T, gap_t.T, gglu_t.T, gup_t.T

    small_shapes = [p.shape for p in small_parts]
    part_buf = _flatten_pad(small_parts)
    rows = part_buf.shape[0]
    gathered = _all_gather8(part_buf, "gather_small").reshape(N_DEV, rows, SMALL_COLS)
    summed = _sum_devices(gathered, "sum_small")
    (s_dmod, s_gain1, s_sinks, s_lr, s_li, s_bbr, s_bbi, s_cr, s_ci, s_dd, s_gain2, s_cb, s_cw, s_fg) = _split_flat(summed, small_shapes)
    _, ssm_vjp = jax.vjp(_ssm_discretize, *ssm[:5])
    g_a_re, g_a_im, g_log_dt, g_b_re, g_b_im = ssm_vjp((s_lr.reshape(ngroups, SSM_STATE), s_li.reshape(ngroups, SSM_STATE), s_bbr, s_bbi))
    g_c_re, g_c_im = jnp.swapaxes(s_cr, 1, 2), jnp.swapaxes(s_ci, 1, 2)
    g_conv_w = lax.dynamic_slice(s_cw, (0, chip * (ff // N_CHIPS)), (3, ff // N_CHIPS))

    dmod_all = gathered.reshape(N_DEV, -1)[:, :6 * d]
    dmod_shard = lax.dynamic_slice(dmod_all, (0, chip * ncol), (N_DEV, ncol))
    ada_res = _ada_bwd_adam(c_all.T, dmod_shard, ada_w[0], ms["ada_w"][0], vs["ada_w"][0], "ada_bwd_adam")

    res = {"ada_w": tuple(o[None] for o in ada_res)}
    for nm, w, g in (("w_in", w_in, grad_w_in), ("w_attn_proj", w_attn_proj, grad_w_ap), ("w_ssm_glu", w_ssm_glu, grad_w_glu),
                     ("w_out", w_out, grad_w_out), ("w_ffn_up", w_ffn_up, grad_w_up), ("w_ffn_down", w_ffn_down, grad_w_down)):
        res[nm] = (g[None],) + tuple(o[None] for o in _adam(w[0], g, ms[nm][0], vs[nm][0], "adam_" + nm))

    small = [("ada_b", ada_b, s_dmod), ("norm_mix_g", norm_mix_g, s_gain1), ("attn_sinks", attn_sinks, s_sinks),
             ("ssm_a_re", ssm_a_re, g_a_re), ("ssm_a_im", ssm_a_im, g_a_im), ("ssm_log_dt", ssm_log_dt, g_log_dt),
             ("ssm_b_re", ssm_b_re, g_b_re), ("ssm_b_im", ssm_b_im, g_b_im), ("ssm_c_re", ssm_c_re, g_c_re),
             ("ssm_c_im", ssm_c_im, g_c_im), ("ssm_d", ssm_d, s_dd), ("norm_ffn_g", norm_ffn_g, s_gain2),
             ("ffn_conv_w", ffn_conv_w, g_conv_w), ("ffn_conv_b", ffn_conv_b, s_cb), ("final_g", final_g, s_fg)]
    shapes = [t[1].shape for t in small]
    bufs = [_flatten_pad([t[1] for t in small]), _flatten_pad([t[2] for t in small]),
            _flatten_pad([ms[t[0]] for t in small]), _flatten_pad([vs[t[0]] for t in small])]
    s_delta, s_m, s_v = _adam(*bufs, "adam_small")
    for t, dl, m2, v2 in zip(small, _split_flat(s_delta, shapes), _split_flat(s_m, shapes), _split_flat(s_v, shapes)):
        res[t[0]] = (t[2].reshape(t[1].shape), dl, m2, v2)

    outs = [loss, grad_x[None]]
    for i in range(4):
        outs += [res[nm][i] for nm in WEIGHT_ORDER]
    return tuple(outs)


WEIGHT_ORDER = ("ada_w", "ada_b", "norm_mix_g", "w_in", "attn_sinks", "w_attn_proj", "ssm_a_re", "ssm_a_im", "ssm_log_dt", "ssm_b_re",
                "ssm_b_im", "ssm_c_re", "ssm_c_im", "ssm_d", "w_ssm_glu", "w_out", "norm_ffn_g", "w_ffn_up", "ffn_conv_w", "ffn_conv_b",
                "w_ffn_down", "final_g")


def kernel(x, c, ada_w, ada_b, norm_mix_g, w_in, attn_sinks, w_attn_proj, ssm_a_re, ssm_a_im, ssm_log_dt, ssm_b_re, ssm_b_im, ssm_c_re, ssm_c_im, ssm_d, w_ssm_glu, w_out, norm_ffn_g, w_ffn_up, ffn_conv_w, ffn_conv_b, w_ffn_down, final_g, loss_target, m_ada_w, m_ada_b, m_norm_mix_g, m_w_in, m_attn_sinks, m_w_attn_proj, m_ssm_a_re, m_ssm_a_im, m_ssm_log_dt, m_ssm_b_re, m_ssm_b_im, m_ssm_c_re, m_ssm_c_im, m_ssm_d, m_w_ssm_glu, m_w_out, m_norm_ffn_g, m_w_ffn_up, m_ffn_conv_w, m_ffn_conv_b, m_w_ffn_down, m_final_g, v_ada_w, v_ada_b, v_norm_mix_g, v_w_in, v_attn_sinks, v_w_attn_proj, v_ssm_a_re, v_ssm_a_im, v_ssm_log_dt, v_ssm_b_re, v_ssm_b_im, v_ssm_c_re, v_ssm_c_im, v_ssm_d, v_w_ssm_glu, v_w_out, v_norm_ffn_g, v_w_ffn_up, v_ffn_conv_w, v_ffn_conv_b, v_w_ffn_down, v_final_g):
    ms = dict(zip(WEIGHT_ORDER, (m_ada_w, m_ada_b, m_norm_mix_g, m_w_in, m_attn_sinks, m_w_attn_proj, m_ssm_a_re, m_ssm_a_im, m_ssm_log_dt,
                                 m_ssm_b_re, m_ssm_b_im, m_ssm_c_re, m_ssm_c_im, m_ssm_d, m_w_ssm_glu, m_w_out, m_norm_ffn_g, m_w_ffn_up,
                                 m_ffn_conv_w, m_ffn_conv_b, m_w_ffn_down, m_final_g)))
    vs = dict(zip(WEIGHT_ORDER, (v_ada_w, v_ada_b, v_norm_mix_g, v_w_in, v_attn_sinks, v_w_attn_proj, v_ssm_a_re, v_ssm_a_im, v_ssm_log_dt,
                                 v_ssm_b_re, v_ssm_b_im, v_ssm_c_re, v_ssm_c_im, v_ssm_d, v_w_ssm_glu, v_w_out, v_norm_ffn_g, v_w_ffn_up,
                                 v_ffn_conv_w, v_ffn_conv_b, v_w_ffn_down, v_final_g)))
    return _kernel_impl(x, c, ada_w, ada_b, norm_mix_g, w_in, attn_sinks, w_attn_proj, ssm_a_re, ssm_a_im, ssm_log_dt, ssm_b_re, ssm_b_im,
                        ssm_c_re, ssm_c_im, ssm_d, w_ssm_glu, w_out, norm_ffn_g, w_ffn_up, ffn_conv_w, ffn_conv_b, w_ffn_down, final_g,
                        loss_target, ms, vs)
```

```python
import math

import jax
import jax.numpy as jnp
from jax import lax
from jax.experimental import pallas as pl
from jax.experimental.pallas import tpu as pltpu

F32 = jnp.float32
BF16 = jnp.bfloat16
MESH = pl.DeviceIdType.MESH

HEAD_DIM = 64
N_KV_HEADS = 2
KV_WIDTH = N_KV_HEADS * HEAD_DIM
ATTN_BLOCK = 128
NEG_INF = -1e30
SSM_GROUP = 16
SSM_STATE = 64
GROUPS_PER_TILE = 8
RMS_EPS = 1e-6
ADAM_LR = 0.001
ADAM_B1 = 0.9
ADAM_B2 = 0.999
ADAM_EPS = 1e-08
ADAM_WD = 0.01
ADAM_STEP = 10
N_CHIPS = 4
N_DEV = 8
VMEM_LIMIT_BYTES = 56 * 1024 * 1024
LANES = 128
SMALL_COLS = 1024


def _tile(dim, target, mult=LANES):
    if dim <= target:
        return dim
    for t in range(target // mult * mult, 0, -mult):
        if dim % t == 0:
            return t
    raise ValueError(f"no tile for {dim}")


def _params(sem=None):
    return pltpu.CompilerParams(dimension_semantics=sem, vmem_limit_bytes=VMEM_LIMIT_BYTES)


def _sigmoid(x):
    return 1.0 / (1.0 + jnp.exp(-x))


def _matmul(a, b, mode, name, out_dtype=BF16, tm=1536, tn=1536, tk=2048):
    if mode == "nn":
        (m, k), (k2, n) = a.shape, b.shape
    elif mode == "nt":
        (m, k), (n, k2) = a.shape, b.shape
    else:
        (k, m), (k2, n) = a.shape, b.shape
    assert k == k2, (a.shape, b.shape, mode)
    tm, tn, tk = _tile(m, tm), _tile(n, tn), _tile(k, tk)
    nk = k // tk
    if mode == "tn":
        a_spec = pl.BlockSpec((tk, tm), lambda i, j, kk: (kk, i))
    else:
        a_spec = pl.BlockSpec((tm, tk), lambda i, j, kk: (i, kk))
    if mode == "nt":
        b_spec = pl.BlockSpec((tn, tk), lambda i, j, kk: (j, kk))
    else:
        b_spec = pl.BlockSpec((tk, tn), lambda i, j, kk: (kk, j))
    dims = {"nn": (((1,), (0,)), ((), ())), "nt": (((1,), (1,)), ((), ())), "tn": (((0,), (0,)), ((), ()))}[mode]

    def body(a_ref, b_ref, o_ref, acc_ref):
        kk = pl.program_id(2)

        @pl.when(kk == 0)
        def _():
            acc_ref[...] = jnp.zeros_like(acc_ref)

        acc_ref[...] += lax.dot_general(a_ref[...], b_ref[...], dims, preferred_element_type=F32)

        @pl.when(kk == nk - 1)
        def _():
            o_ref[...] = acc_ref[...].astype(o_ref.dtype)

    return pl.pallas_call(
        body,
        out_shape=jax.ShapeDtypeStruct((m, n), out_dtype),
        grid=(m // tm, n // tn, nk),
        in_specs=[a_spec, b_spec],
        out_specs=pl.BlockSpec((tm, tn), lambda i, j, kk: (i, j)),
        scratch_shapes=[pltpu.VMEM((tm, tn), F32)],
        name=name,
        compiler_params=_params(("parallel", "parallel", "arbitrary")),
    )(a, b)


def _row_spec(tl, w, col=0):
    return pl.BlockSpec((tl, w), lambda i, col=col: (i, col))


def _vec_spec(w, col=0):
    return pl.BlockSpec((1, w), lambda i, col=col: (0, col))


def _norm_mod(x, gain, mod, sc_col, sh_col, name):
    l, d = x.shape
    tl = _tile(l, 256, 8)

    def body(x_ref, g_ref, sc_ref, sh_ref, h_ref):
        xv = x_ref[...]
        r = lax.rsqrt(jnp.mean(xv * xv, axis=-1, keepdims=True) + RMS_EPS)
        h_ref[...] = ((xv * r) * g_ref[...] * (1.0 + sc_ref[...]) + sh_ref[...]).astype(BF16)

    return pl.pallas_call(
        body,
        out_shape=jax.ShapeDtypeStruct((l, d), BF16),
        grid=(l // tl,),
        in_specs=[_row_spec(tl, d), _vec_spec(d), _vec_spec(d, sc_col), _vec_spec(d, sh_col)],
        out_specs=_row_spec(tl, d),
        name=name,
        compiler_params=_params(("parallel",)),
    )(x, gain, mod, mod)


def _resid_norm_mod(x, mo, gain, mod, gate_col, sc_col, sh_col, name):
    l, d = x.shape
    tl = _tile(l, 256, 8)

    def body(x_ref, mo_ref, g_ref, gate_ref, sc_ref, sh_ref, x2_ref, h_ref):
        xv = x_ref[...] + gate_ref[...] * mo_ref[...]
        x2_ref[...] = xv
        r = lax.rsqrt(jnp.mean(xv * xv, axis=-1, keepdims=True) + RMS_EPS)
        h_ref[...] = ((xv * r) * g_ref[...] * (1.0 + sc_ref[...]) + sh_ref[...]).astype(BF16)

    return pl.pallas_call(
        body,
        out_shape=(jax.ShapeDtypeStruct((l, d), F32), jax.ShapeDtypeStruct((l, d), BF16)),
        grid=(l // tl,),
        in_specs=[_row_spec(tl, d), _row_spec(tl, d), _vec_spec(d), _vec_spec(d, gate_col), _vec_spec(d, sc_col),
                  _vec_spec(d, sh_col)],
        out_specs=(_row_spec(tl, d), _row_spec(tl, d)),
        name=name,
        compiler_params=_params(("parallel",)),
    )(x, mo, gain, mod, mod, mod)


def _final_loss(x2, f, mod, gate_col, final_g, target, name):
    l, d = x2.shape
    tl = _tile(l, 256, 8)

    def body(x2_ref, f_ref, gate_ref, fg_ref, t_ref, loss_ref, dfg_ref, dgate_ref, dx3_ref, df_ref):
        i = pl.program_id(0)
        fv = f_ref[...]
        x3 = x2_ref[...] + gate_ref[...] * fv
        r = lax.rsqrt(jnp.mean(x3 * x3, axis=-1, keepdims=True) + RMS_EPS)
        xh = x3 * r
        err = xh * fg_ref[...] - t_ref[...]
        part = 0.5 * jnp.sum(jnp.mean(err * err, axis=-1, keepdims=True), axis=0, keepdims=True)
        dout = err * (1.0 / d)
        dxh = dout * fg_ref[...]
        dx3 = r * (dxh - xh * jnp.mean(dxh * xh, axis=-1, keepdims=True))
        dx3_ref[...] = dx3
        df_ref[...] = (gate_ref[...] * dx3).astype(BF16)

        @pl.when(i == 0)
        def _():
            loss_ref[...] = jnp.zeros_like(loss_ref)
            dfg_ref[...] = jnp.zeros_like(dfg_ref)
            dgate_ref[...] = jnp.zeros_like(dgate_ref)

        loss_ref[...] += jnp.broadcast_to(part, loss_ref.shape)
        dfg_ref[...] += jnp.sum(dout * xh, axis=0, keepdims=True)
        dgate_ref[...] += jnp.sum(dx3 * fv, axis=0, keepdims=True)

    vec = pl.BlockSpec((1, d), lambda i: (0, 0))
    return pl.pallas_call(
        body,
        out_shape=(jax.ShapeDtypeStruct((1, LANES), F32), jax.ShapeDtypeStruct((1, d), F32),
                   jax.ShapeDtypeStruct((1, d), F32), jax.ShapeDtypeStruct((l, d), F32),
                   jax.ShapeDtypeStruct((l, d), BF16)),
        grid=(l // tl,),
        in_specs=[_row_spec(tl, d), _row_spec(tl, d), _vec_spec(d, gate_col), vec, _row_spec(tl, d)],
        out_specs=(pl.BlockSpec((1, LANES), lambda i: (0, 0)), vec, vec, _row_spec(tl, d), _row_spec(tl, d)),
        name=name,
        compiler_params=_params(("arbitrary",)),
    )(x2, f, mod, final_g, target)


def _norm_mod_bwd(dh, x, dx_res, gain, mod, sc_col, name, branch=None, gate_col=None):
    l, d = x.shape
    tl = _tile(l, 256, 8)
    with_gate = branch is not None

    def body(*refs):
        if with_gate:
            dh_ref, x_ref, dr_ref, g_ref, sc_ref, br_ref, gate_ref, dx_ref, dsh_ref, dsc_ref, dg_ref, dm_ref, dgate_ref = refs
        else:
            dh_ref, x_ref, dr_ref, g_ref, sc_ref, dx_ref, dsh_ref, dsc_ref, dg_ref = refs
        i = pl.program_id(0)
        xv = x_ref[...]
        dhv = dh_ref[...].astype(F32)
        r = lax.rsqrt(jnp.mean(xv * xv, axis=-1, keepdims=True) + RMS_EPS)
        xh = xv * r
        dn = dhv * (1.0 + sc_ref[...])
        dxh = dn * g_ref[...]
        dx = dr_ref[...] + r * (dxh - xh * jnp.mean(dxh * xh, axis=-1, keepdims=True))
        dx_ref[...] = dx

        @pl.when(i == 0)
        def _():
            dsh_ref[...] = jnp.zeros_like(dsh_ref)
            dsc_ref[...] = jnp.zeros_like(dsc_ref)
            dg_ref[...] = jnp.zeros_like(dg_ref)
            if with_gate:
                dgate_ref[...] = jnp.zeros_like(dgate_ref)

        dsh_ref[...] += jnp.sum(dhv, axis=0, keepdims=True)
        dsc_ref[...] += jnp.sum(dhv * (xh * g_ref[...]), axis=0, keepdims=True)
        dg_ref[...] += jnp.sum(dn * xh, axis=0, keepdims=True)
        if with_gate:
            dm_ref[...] = (gate_ref[...] * dx).astype(BF16)
            dgate_ref[...] += jnp.sum(dx * br_ref[...], axis=0, keepdims=True)

    vec = pl.BlockSpec((1, d), lambda i: (0, 0))
    in_specs = [_row_spec(tl, d), _row_spec(tl, d), _row_spec(tl, d), vec, _vec_spec(d, sc_col)]
    args = [dh, x, dx_res, gain, mod]
    out_shape = [jax.ShapeDtypeStruct((l, d), F32)] + [jax.ShapeDtypeStruct((1, d), F32)] * 3
    out_specs = [_row_spec(tl, d), vec, vec, vec]
    if with_gate:
        in_specs += [_row_spec(tl, d), _vec_spec(d, gate_col)]
        args += [branch, mod]
        out_shape += [jax.ShapeDtypeStruct((l, d), BF16), jax.ShapeDtypeStruct((1, d), F32)]
        out_specs += [_row_spec(tl, d), vec]
    return pl.pallas_call(
        body, out_shape=tuple(out_shape), grid=(l // tl,), in_specs=in_specs, out_specs=tuple(out_specs),
        name=name, compiler_params=_params(("arbitrary",)),
    )(*args)


def _attn_mask(n):
    qi = lax.broadcasted_iota(jnp.int32, (ATTN_BLOCK, 2 * ATTN_BLOCK), 0)
    kj = lax.broadcasted_iota(jnp.int32, (ATTN_BLOCK, 2 * ATTN_BLOCK), 1)
    rel = qi + ATTN_BLOCK - kj
    return (rel >= 0) & (rel < ATTN_BLOCK) & ((kj >= ATTN_BLOCK) | (n > 0))


def _attn_probs(qh, kh, sink, mask):
    s = lax.dot_general(qh, kh, (((1,), (1,)), ((), ())), preferred_element_type=F32) * (HEAD_DIM ** -0.5)
    s = jnp.where(mask, s, NEG_INF)
    m = jnp.maximum(jnp.max(s, axis=-1, keepdims=True), sink)
    p = jnp.exp(s - m)
    es = jnp.exp(sink - m)
    inv = 1.0 / (jnp.sum(p, axis=-1, keepdims=True) + es)
    return p * inv, es * inv


def _attn_specs(aw):
    kvb = aw // (2 * KV_WIDTH)
    q_spec = pl.BlockSpec((ATTN_BLOCK, aw), lambda n: (n, 0))
    kv_cur = pl.BlockSpec((ATTN_BLOCK, 2 * KV_WIDTH), lambda n: (n, kvb))
    kv_prev = pl.BlockSpec((ATTN_BLOCK, 2 * KV_WIDTH), lambda n: (jnp.maximum(n - 1, 0), kvb))
    return q_spec, kv_cur, kv_prev


def _attn_fwd(proj, sinks, aw, name):
    l = proj.shape[0]
    nq = aw // HEAD_DIM
    qpk = nq // N_KV_HEADS
    assert aw % (2 * KV_WIDTH) == 0

    def body(q_ref, kvc_ref, kvp_ref, sink_ref, o_ref):
        n = pl.program_id(0)
        mask = _attn_mask(n)
        kv = jnp.concatenate([kvp_ref[...], kvc_ref[...]], axis=0)
        for h in range(nq):
            g = h // qpk
            qh = q_ref[:, h * HEAD_DIM:(h + 1) * HEAD_DIM]
            kh = kv[:, g * HEAD_DIM:(g + 1) * HEAD_DIM]
            vh = kv[:, KV_WIDTH + g * HEAD_DIM:KV_WIDTH + (g + 1) * HEAD_DIM]
            p, _ = _attn_probs(qh, kh, sink_ref[0:1, h:h + 1], mask)
            o = jnp.dot(p.astype(BF16), vh, preferred_element_type=F32)
            o_ref[:, h * HEAD_DIM:(h + 1) * HEAD_DIM] = o.astype(BF16)

    q_spec, kv_cur, kv_prev = _attn_specs(aw)
    return pl.pallas_call(
        body,
        out_shape=jax.ShapeDtypeStruct((l, aw), BF16),
        grid=(l // ATTN_BLOCK,),
        in_specs=[q_spec, kv_cur, kv_prev, pl.BlockSpec((1, nq), lambda n: (0, 0))],
        out_specs=pl.BlockSpec((ATTN_BLOCK, aw), lambda n: (n, 0)),
        name=name,
        compiler_params=_params(("parallel",)),
    )(proj, proj, proj, sinks)


def _attn_bwd(proj, sinks, dattn, aw, name):
    l = proj.shape[0]
    nq = aw // HEAD_DIM
    qpk = nq // N_KV_HEADS
    scale = HEAD_DIM ** -0.5

    def body(q_ref, kvc_ref, kvp_ref, sink_ref, do_ref, dq_ref, dcur_ref, dprev_ref, dsink_ref):
        n = pl.program_id(0)
        mask = _attn_mask(n)
        kv = jnp.concatenate([kvp_ref[...], kvc_ref[...]], axis=0)
        lane = lax.broadcasted_iota(jnp.int32, (1, nq), 1)
        dsink = jnp.zeros((1, nq), F32)
        dks, dvs = [], []
        for g in range(N_KV_HEADS):
            kh = kv[:, g * HEAD_DIM:(g + 1) * HEAD_DIM]
            vh = kv[:, KV_WIDTH + g * HEAD_DIM:KV_WIDTH + (g + 1) * HEAD_DIM]
            dk = jnp.zeros((2 * ATTN_BLOCK, HEAD_DIM), F32)
            dv = jnp.zeros((2 * ATTN_BLOCK, HEAD_DIM), F32)
            for h in range(g * qpk, (g + 1) * qpk):
                qh = q_ref[:, h * HEAD_DIM:(h + 1) * HEAD_DIM]
                doh = do_ref[:, h * HEAD_DIM:(h + 1) * HEAD_DIM]
                p, ps = _attn_probs(qh, kh, sink_ref[0:1, h:h + 1], mask)
                pb = p.astype(BF16)
                o = jnp.dot(pb, vh, preferred_element_type=F32)
                delta = jnp.sum(doh.astype(F32) * o, axis=-1, keepdims=True)
                dp = lax.dot_general(doh, vh, (((1,), (1,)), ((), ())), preferred_element_type=F32)
                ds = (p * (dp - delta)).astype(BF16)
                dq = jnp.dot(ds, kh, preferred_element_type=F32) * scale
                dq_ref[:, h * HEAD_DIM:(h + 1) * HEAD_DIM] = dq.astype(BF16)
                dk += lax.dot_general(ds, qh, (((0,), (0,)), ((), ())), preferred_element_type=F32) * scale
                dv += lax.dot_general(pb, doh, (((0,), (0,)), ((), ())), preferred_element_type=F32)
                dsink += jnp.where(lane == h, -jnp.sum(ps * delta, axis=0, keepdims=True), 0.0)
            dks.append(dk)
            dvs.append(dv)
        dkv = jnp.concatenate(dks + dvs, axis=1)
        dprev_ref[...] = dkv[:ATTN_BLOCK]
        dcur_ref[...] = dkv[ATTN_BLOCK:]

        @pl.when(n == 0)
        def _():
            dsink_ref[...] = jnp.zeros_like(dsink_ref)

        dsink_ref[...] += dsink

    q_spec, kv_cur, kv_prev = _attn_specs(aw)
    blk = pl.BlockSpec((ATTN_BLOCK, 2 * KV_WIDTH), lambda n: (n, 0))
    return pl.pallas_call(
        body,
        out_shape=(jax.ShapeDtypeStruct((l, aw), BF16), jax.ShapeDtypeStruct((l, 2 * KV_WIDTH), F32),
                   jax.ShapeDtypeStruct((l, 2 * KV_WIDTH), F32), jax.ShapeDtypeStruct((1, nq), F32)),
        grid=(l // ATTN_BLOCK,),
        in_specs=[q_spec, kv_cur, kv_prev, pl.BlockSpec((1, nq), lambda n: (0, 0)),
                  pl.BlockSpec((ATTN_BLOCK, aw), lambda n: (n, 0))],
        out_specs=(pl.BlockSpec((ATTN_BLOCK, aw), lambda n: (n, 0)), blk, blk, pl.BlockSpec((1, nq), lambda n: (0, 0))),
        name=name,
        compiler_params=_params(("arbitrary",)),
    )(proj, proj, proj, sinks, dattn)


def _ssm_discretize(a_re, a_im, log_dt, b_re, b_im):
    dt = jnp.exp(log_dt)[:, None]
    mag = jnp.exp(a_re * dt)
    lr, li = mag * jnp.cos(a_im * dt), mag * jnp.sin(a_im * dt)
    den = a_re * a_re + a_im * a_im
    zr = ((lr - 1.0) * a_re + li * a_im) / den
    zi = (li * a_re - (lr - 1.0) * a_im) / den
    bbar_r = zr[:, :, None] * b_re - zi[:, :, None] * b_im
    bbar_i = zr[:, :, None] * b_im + zi[:, :, None] * b_re
    return lr, li, bbar_r, bbar_i


def _cmul(ar, ai, br, bi):
    return ar * br - ai * bi, ar * bi + ai * br


def _scan_tables(lr, li):
    lr, li = lr.reshape(1, -1), li.reshape(1, -1)
    pows = [(lr, li)]
    for _ in range(7):
        pows.append(_cmul(*pows[-1], lr, li))
    row = jnp.arange(8)[:, None]
    fwd, bwd = [], []
    for d in (1, 2, 4):
        pr, pi = pows[d - 1]
        fwd += [jnp.where(row >= d, pr, 0.0), jnp.where(row >= d, pi, 0.0)]
        bwd += [jnp.where(row < 8 - d, pr, 0.0), jnp.where(row < 8 - d, -pi, 0.0)]
    fwd += [jnp.concatenate([p[0] for p in pows], 0), jnp.concatenate([p[1] for p in pows], 0)]
    bwd += [jnp.concatenate([p[0] for p in pows[::-1]], 0), jnp.concatenate([-p[1] for p in pows[::-1]], 0)]
    return jnp.concatenate(fwd, 0), jnp.concatenate(bwd, 0)


def _pack_in(b):
    g, n, p = b.shape
    t = g // GROUPS_PER_TILE
    eye = jnp.eye(GROUPS_PER_TILE, dtype=b.dtype)
    bb = b.reshape(t, GROUPS_PER_TILE, n, p)
    return jnp.einsum("tgnp,gh->tgphn", bb, eye).reshape(t, GROUPS_PER_TILE * p, GROUPS_PER_TILE * n)


def _pack_out(c):
    g, p, n = c.shape
    t = g // GROUPS_PER_TILE
    eye = jnp.eye(GROUPS_PER_TILE, dtype=c.dtype)
    cc = c.reshape(t, GROUPS_PER_TILE, p, n)
    return jnp.einsum("tgpn,gh->tgnhp", cc, eye).reshape(t, GROUPS_PER_TILE * n, GROUPS_PER_TILE * p)


def _unpack_diag(x, n, p):
    t = x.shape[0]
    xx = x.reshape(t, GROUPS_PER_TILE, n, GROUPS_PER_TILE, p)
    eye = jnp.eye(GROUPS_PER_TILE, dtype=x.dtype)
    return jnp.einsum("tgnhp,gh->tgnp", xx, eye).reshape(t * GROUPS_PER_TILE, n, p)


def _scan_rows(hr_ref, hi_ref, tab_ref, l, reverse, prev_refs=None):
    w = hr_ref.shape[1]
    tabs = [tab_ref[pl.ds(8 * i, 8), :] for i in range(8)]
    nchunk = l // 8
    row = lax.broadcasted_iota(jnp.int32, (8, w), 0)

    def step(s, carry):
        k = nchunk - 1 - s if reverse else s
        t8 = pl.multiple_of(k * 8, 8)
        hr = hr_ref[pl.ds(t8, 8), :]
        hi = hi_ref[pl.ds(t8, 8), :]
        for idx, d in enumerate((1, 2, 4)):
            mr, mi = tabs[2 * idx], tabs[2 * idx + 1]
            shift = 8 - d if reverse else d
            sr = pltpu.roll(hr, shift, 0)
            si = pltpu.roll(hi, shift, 0)
            hr, hi = hr + mr * sr - mi * si, hi + mr * si + mi * sr
        cr, ci = carry[0], carry[1]
        hr, hi = hr + tabs[6] * cr - tabs[7] * ci, hi + tabs[6] * ci + tabs[7] * cr
        hr_ref[pl.ds(t8, 8), :] = hr
        hi_ref[pl.ds(t8, 8), :] = hi
        if not reverse:
            return hr[7:8, :], hi[7:8, :]
        out = (hr[0:1, :], hi[0:1, :])
        if prev_refs is None:
            return out
        fr_ref, fi_ref = prev_refs
        tp = pl.multiple_of(jnp.maximum(k - 1, 0) * 8, 8)
        keep = jnp.where(k > 0, 1.0, 0.0)
        lr_last = fr_ref[pl.ds(tp, 8), :][7:8, :] * keep
        li_last = fi_ref[pl.ds(tp, 8), :][7:8, :] * keep
        pr = jnp.where(row == 0, lr_last, pltpu.roll(fr_ref[pl.ds(t8, 8), :], 1, 0))
        pi = jnp.where(row == 0, li_last, pltpu.roll(fi_ref[pl.ds(t8, 8), :], 1, 0))
        return out + (carry[2] + hr * pr + hi * pi, carry[3] + hi * pr - hr * pi)

    zero = jnp.zeros((1, w), F32)
    init = (zero, zero)
    if reverse and prev_refs is not None:
        init += (jnp.zeros((8, w), F32), jnp.zeros((8, w), F32))
    return lax.fori_loop(0, nchunk, step, init)


def _s5_dims(sw):
    chan = GROUPS_PER_TILE * SSM_GROUP
    states = GROUPS_PER_TILE * SSM_STATE
    assert chan == LANES and sw % chan == 0
    return sw // chan, chan, states


def _s5_fwd(proj, u_off, packs, dvec, tab_f, sw, name):
    l = proj.shape[0]
    nt, chan, states = _s5_dims(sw)
    ch = _tile(l, 512, 8)
    ub = u_off // chan
    assert u_off % chan == 0

    def body(u_ref, br_ref, bi_ref, cr_ref, ci_ref, d_ref, tab_ref, y_ref, hr_ref, hi_ref):
        for i in range(l // ch):
            rows = pl.ds(i * ch, ch)
            u = u_ref[rows, :]
            hr_ref[rows, :] = jnp.dot(u, br_ref[0], preferred_element_type=F32)
            hi_ref[rows, :] = jnp.dot(u, bi_ref[0], preferred_element_type=F32)
        _scan_rows(hr_ref, hi_ref, tab_ref, l, reverse=False)
        for i in range(l // ch):
            rows = pl.ds(i * ch, ch)
            y = jnp.dot(hr_ref[rows, :].astype(BF16), cr_ref[0], preferred_element_type=F32)
            y -= jnp.dot(hi_ref[rows, :].astype(BF16), ci_ref[0], preferred_element_type=F32)
            y_ref[rows, :] = y + d_ref[...] * u_ref[rows, :].astype(F32)

    pin = pl.BlockSpec((1, chan, states), lambda t: (t, 0, 0))
    pout = pl.BlockSpec((1, states, chan), lambda t: (t, 0, 0))
    return pl.pallas_call(
        body,
        out_shape=jax.ShapeDtypeStruct((l, sw), F32),
        grid=(nt,),
        in_specs=[pl.BlockSpec((l, chan), lambda t: (0, ub + t)), pin, pin, pout, pout,
                  pl.BlockSpec((1, chan), lambda t: (0, t)), pl.BlockSpec((64, states), lambda t: (0, t))],
        out_specs=pl.BlockSpec((l, chan), lambda t: (0, t)),
        scratch_shapes=[pltpu.VMEM((l, states), F32), pltpu.VMEM((l, states), F32)],
        name=name,
        compiler_params=_params(("parallel",)),
    )(proj, packs["br"], packs["bi"], packs["cr"], packs["ci"], dvec, tab_f)


def _s5_bwd(proj, u_off, dy, packs, dvec, tab_f, tab_b, sw, name):
    l = proj.shape[0]
    nt, chan, states = _s5_dims(sw)
    ch = _tile(l, 512, 8)
    ub = u_off // chan
    tn_dims = (((0,), (0,)), ((), ()))

    def body(u_ref, dy_ref, br_ref, bi_ref, brt_ref, bit_ref, crt_ref, cit_ref, d_ref, tabf_ref, tabb_ref,
             du_ref, dlam_ref, dbr_ref, dbi_ref, dcr_ref, dci_ref, dd_ref, hr_ref, hi_ref, gr_ref, gi_ref):
        for i in range(l // ch):
            rows = pl.ds(i * ch, ch)
            u = u_ref[rows, :]
            hr_ref[rows, :] = jnp.dot(u, br_ref[0], preferred_element_type=F32)
            hi_ref[rows, :] = jnp.dot(u, bi_ref[0], preferred_element_type=F32)
            dyv = dy_ref[rows, :]
            gr_ref[rows, :] = jnp.dot(dyv, crt_ref[0], preferred_element_type=F32)
            gi_ref[rows, :] = -jnp.dot(dyv, cit_ref[0], preferred_element_type=F32)
        _scan_rows(hr_ref, hi_ref, tabf_ref, l, reverse=False)
        _, _, acc_r, acc_i = _scan_rows(gr_ref, gi_ref, tabb_ref, l, reverse=True, prev_refs=(hr_ref, hi_ref))
        dlam_ref[...] = jnp.concatenate(
            [jnp.sum(acc_r, axis=0, keepdims=True), jnp.sum(acc_i, axis=0, keepdims=True), jnp.zeros((6, states), F32)], axis=0)
        dbr_ref[...] = jnp.zeros_like(dbr_ref)
        dbi_ref[...] = jnp.zeros_like(dbi_ref)
        dcr_ref[...] = jnp.zeros_like(dcr_ref)
        dci_ref[...] = jnp.zeros_like(dci_ref)
        dd = jnp.zeros((1, chan), F32)
        for i in range(l // ch):
            rows = pl.ds(i * ch, ch)
            u = u_ref[rows, :]
            dyv = dy_ref[rows, :]
            grb = gr_ref[rows, :].astype(BF16)
            gib = gi_ref[rows, :].astype(BF16)
            dbr_ref[0] += lax.dot_general(grb, u, tn_dims, preferred_element_type=F32)
            dbi_ref[0] += lax.dot_general(gib, u, tn_dims, preferred_element_type=F32)
            dcr_ref[0] += lax.dot_general(hr_ref[rows, :].astype(BF16), dyv, tn_dims, preferred_element_type=F32)
            dci_ref[0] -= lax.dot_general(hi_ref[rows, :].astype(BF16), dyv, tn_dims, preferred_element_type=F32)
            du = jnp.dot(grb, brt_ref[0], preferred_element_type=F32) + jnp.dot(gib, bit_ref[0], preferred_element_type=F32)
            dyf = dyv.astype(F32)
            du_ref[rows, :] = (du + d_ref[...] * dyf).astype(BF16)
            dd += jnp.sum(dyf * u.astype(F32), axis=0, keepdims=True)
        dd_ref[...] = dd

    pin = pl.BlockSpec((1, chan, states), lambda t: (t, 0, 0))
    pout = pl.BlockSpec((1, states, chan), lambda t: (t, 0, 0))
    seq = pl.BlockSpec((l, chan), lambda t: (0, t))
    tab = pl.BlockSpec((64, states), lambda t: (0, t))
    vec = pl.BlockSpec((1, chan), lambda t: (0, t))
    pack_shape = jax.ShapeDtypeStruct((nt, states, chan), F32)
    return pl.pallas_call(
        body,
        out_shape=(jax.ShapeDtypeStruct((l, sw), BF16), jax.ShapeDtypeStruct((8, nt * states), F32),
                   pack_shape, pack_shape, pack_shape, pack_shape, jax.ShapeDtypeStruct((1, sw), F32)),
        grid=(nt,),
        in_specs=[pl.BlockSpec((l, chan), lambda t: (0, ub + t)), seq, pin, pin, pout, pout, pin, pin, vec, tab, tab],
        out_specs=(seq, pl.BlockSpec((8, states), lambda t: (0, t)), pout, pout, pout, pout, vec),
        scratch_shapes=[pltpu.VMEM((l, states), F32)] * 4,
        name=name,
        compiler_params=_params(("parallel",)),
    )(proj, dy, packs["br"], packs["bi"], packs["brt"], packs["bit"], packs["crt"], packs["cit"], dvec, tab_f, tab_b)


GELU_K = math.sqrt(2.0 / math.pi)
GELU_C = 0.044715


def _gelu(y, name):
    l, w = y.shape
    tl = _tile(l, 512, 8)

    def body(y_ref, o_ref):
        v = y_ref[...]
        o_ref[...] = (0.5 * v * (1.0 + jnp.tanh(GELU_K * (v + GELU_C * v * v * v)))).astype(BF16)

    return pl.pallas_call(body, out_shape=jax.ShapeDtypeStruct((l, w), BF16), grid=(l // tl,),
                          in_specs=[_row_spec(tl, w)], out_specs=_row_spec(tl, w), name=name,
                          compiler_params=_params(("parallel",)))(y)


def _gelu_bwd(y, dg, name):
    l, w = y.shape
    tl = _tile(l, 512, 8)

    def body(y_ref, dg_ref, o_ref):
        v = y_ref[...]
        t = jnp.tanh(GELU_K * (v + GELU_C * v * v * v))
        grad = 0.5 * (1.0 + t) + 0.5 * v * (1.0 - t * t) * GELU_K * (1.0 + 3.0 * GELU_C * v * v)
        o_ref[...] = (dg_ref[...].astype(F32) * grad).astype(BF16)

    return pl.pallas_call(body, out_shape=jax.ShapeDtypeStruct((l, w), BF16), grid=(l // tl,),
                          in_specs=[_row_spec(tl, w), _row_spec(tl, w)], out_specs=_row_spec(tl, w), name=name,
                          compiler_params=_params(("parallel",)))(y, dg)


MIX_COLS = 256


def _mix(proj, ga_off, gs_off, attn_out, glu, d, name):
    l = proj.shape[0]
    tl = _tile(l, 1024, 8)
    cb = MIX_COLS
    nj = d // cb
    assert d % cb == 0 and ga_off % cb == 0 and gs_off % cb == 0

    def body(ga_ref, gs_ref, a_ref, ua_ref, ub_ref, o_ref):
        ssm = ua_ref[...].astype(F32) * _sigmoid(ub_ref[...].astype(F32))
        o_ref[...] = (_sigmoid(ga_ref[...].astype(F32)) * a_ref[...].astype(F32)
                      + _sigmoid(gs_ref[...].astype(F32)) * ssm).astype(BF16)

    def spec(off):
        return pl.BlockSpec((tl, cb), lambda i, j, off=off: (i, off // cb + j))

    return pl.pallas_call(
        body, out_shape=jax.ShapeDtypeStruct((l, d), BF16), grid=(l // tl, nj),
        in_specs=[spec(ga_off), spec(gs_off), spec(0), spec(0), spec(d)], out_specs=spec(0), name=name,
        compiler_params=_params(("parallel", "parallel")),
    )(proj, proj, attn_out, glu, glu)


def _mix_bwd(proj, ga_off, gs_off, attn_out, glu, dmixed, d, name):
    l = proj.shape[0]
    tl = _tile(l, 1024, 8)
    cb = MIX_COLS
    nj = d // cb

    def body(ga_ref, gs_ref, a_ref, ua_ref, ub_ref, dm_ref, dga_ref, dgs_ref, da_ref, dglu_ref):
        s = pl.program_id(2)
        dm = dm_ref[...].astype(F32)
        sa = _sigmoid(ga_ref[...].astype(F32))
        ss = _sigmoid(gs_ref[...].astype(F32))
        sb = _sigmoid(ub_ref[...].astype(F32))
        ua = ua_ref[...].astype(F32)
        dga_ref[...] = (dm * a_ref[...].astype(F32) * sa * (1.0 - sa)).astype(BF16)
        da_ref[...] = (dm * sa).astype(BF16)
        dgs_ref[...] = (dm * (ua * sb) * ss * (1.0 - ss)).astype(BF16)
        dssm = dm * ss
        dglu_ref[...] = jnp.where(s == 0, dssm * sb, dssm * ua * sb * (1.0 - sb)).astype(BF16)

    def spec(off):
        return pl.BlockSpec((tl, cb), lambda i, j, s, off=off: (i, off // cb + j))

    out = jax.ShapeDtypeStruct((l, d), BF16)
    return pl.pallas_call(
        body, out_shape=(out, out, out, jax.ShapeDtypeStruct((l, 2 * d), BF16)), grid=(l // tl, nj, 2),
        in_specs=[spec(ga_off), spec(gs_off), spec(0), spec(0), spec(d), spec(0)],
        out_specs=(spec(0), spec(0), spec(0), pl.BlockSpec((tl, cb), lambda i, j, s: (i, j + s * nj))), name=name,
        compiler_params=_params(("parallel", "parallel", "arbitrary")),
    )(proj, proj, attn_out, glu, glu, dmixed)


CONV_COLS = 512
HALO = 16


def _shift_rows(v, k, head):
    row = lax.broadcasted_iota(jnp.int32, v.shape, 0)
    out = pltpu.roll(v, k, 0)
    for r in range(k):
        out = jnp.where(row == r, head[HALO - k + r:HALO - k + r + 1, :], out)
    return out


def _shift_rows_up(v, k, tail):
    n = v.shape[0]
    row = lax.broadcasted_iota(jnp.int32, v.shape, 0)
    out = pltpu.roll(v, n - k, 0)
    for r in range(k):
        out = jnp.where(row == n - k + r, tail[r:r + 1, :], out)
    return out


def _conv_gate(g, head, w_ref, b_ref):
    return w_ref[0:1, :] * _shift_rows(g, 2, head) + w_ref[1:2, :] * _shift_rows(g, 1, head) + w_ref[2:3, :] * g + b_ref[...]


def _conv_act(up, conv_w, conv_b, ff, name):
    l = up.shape[0]
    tl = _tile(l, 512, HALO)
    cw = _tile(ff, CONV_COLS)
    nj = ff // cw
    hb = tl // HALO

    def body(g_ref, gp_ref, v_ref, w_ref, b_ref, o_ref):
        i = pl.program_id(0)
        head = gp_ref[...].astype(F32) * jnp.where(i > 0, 1.0, 0.0)
        gc = _conv_gate(g_ref[...].astype(F32), head, w_ref, b_ref)
        o_ref[...] = (gc * _sigmoid(gc) * v_ref[...].astype(F32)).astype(BF16)

    return pl.pallas_call(
        body, out_shape=jax.ShapeDtypeStruct((l, ff), BF16), grid=(l // tl, nj),
        in_specs=[pl.BlockSpec((tl, cw), lambda i, j: (i, j)),
                  pl.BlockSpec((HALO, cw), lambda i, j: (jnp.maximum(i * hb - 1, 0), j)),
                  pl.BlockSpec((tl, cw), lambda i, j: (i, nj + j)),
                  pl.BlockSpec((3, cw), lambda i, j: (0, j)), pl.BlockSpec((1, cw), lambda i, j: (0, j))],
        out_specs=pl.BlockSpec((tl, cw), lambda i, j: (i, j)), name=name,
        compiler_params=_params(("parallel", "parallel")),
    )(up, up, up, conv_w, conv_b)


def _conv_act_bwd(up, da, conv_w, conv_b, ff, name):
    l = up.shape[0]
    tl = _tile(l, 512, HALO)
    cw = _tile(ff, CONV_COLS)
    nj = ff // cw
    hb = tl // HALO
    ni = l // tl

    def body(g_ref, gp_ref, gn_ref, v_ref, vn_ref, da_ref, dan_ref, w_ref, b_ref, dup_ref, dw_ref, db_ref):
        i = pl.program_id(0)
        s = pl.program_id(2)
        g = g_ref[...].astype(F32)
        head = gp_ref[...].astype(F32) * jnp.where(i > 0, 1.0, 0.0)
        g1 = _shift_rows(g, 1, head)
        g2 = _shift_rows(g, 2, head)
        gc = w_ref[0:1, :] * g2 + w_ref[1:2, :] * g1 + w_ref[2:3, :] * g + b_ref[...]
        sg = _sigmoid(gc)
        dav = da_ref[...].astype(F32)
        dgc = dav * v_ref[...].astype(F32) * (sg * (1.0 + gc * (1.0 - sg)))
        gn = gn_ref[...].astype(F32)
        gcn = _conv_gate(gn, g[tl - HALO:, :], w_ref, b_ref)
        sgn = _sigmoid(gcn)
        dgcn = dan_ref[...].astype(F32) * vn_ref[...].astype(F32) * (sgn * (1.0 + gcn * (1.0 - sgn)))
        dgcn = dgcn * jnp.where(i < ni - 1, 1.0, 0.0)
        dgate = w_ref[2:3, :] * dgc + w_ref[1:2, :] * _shift_rows_up(dgc, 1, dgcn) + w_ref[0:1, :] * _shift_rows_up(dgc, 2, dgcn)
        dup_ref[...] = jnp.where(s == 0, dgate, dav * (gc * sg)).astype(BF16)
        zero = jnp.zeros((1, cw), F32)
        dw_ref[...] = jnp.concatenate(
            [jnp.sum(dgc * g2, axis=0, keepdims=True), jnp.sum(dgc * g1, axis=0, keepdims=True),
             jnp.sum(dgc * g, axis=0, keepdims=True)] + [zero] * 5, axis=0)
        db_ref[...] = jnp.concatenate([jnp.sum(dgc, axis=0, keepdims=True)] + [zero] * 7, axis=0)

    def cur(off):
        return pl.BlockSpec((tl, cw), lambda i, j, s, off=off: (i, off + j))

    def prev(off):
        return pl.BlockSpec((HALO, cw), lambda i, j, s, off=off: (jnp.maximum(i * hb - 1, 0), off + j))

    def nxt(off):
        return pl.BlockSpec((HALO, cw), lambda i, j, s, off=off: (jnp.minimum((i + 1) * hb, l // HALO - 1), off + j))

    part = jax.ShapeDtypeStruct((ni * 8, ff), F32)
    part_spec = pl.BlockSpec((8, cw), lambda i, j, s: (i, j))
    return pl.pallas_call(
        body, out_shape=(jax.ShapeDtypeStruct((l, 2 * ff), BF16), part, part), grid=(ni, nj, 2),
        in_specs=[cur(0), prev(0), nxt(0), cur(nj), nxt(nj), cur(0), nxt(0),
                  pl.BlockSpec((3, cw), lambda i, j, s: (0, j)), pl.BlockSpec((1, cw), lambda i, j, s: (0, j))],
        out_specs=(pl.BlockSpec((tl, cw), lambda i, j, s: (i, j + s * nj)), part_spec, part_spec), name=name,
        compiler_params=_params(("parallel", "parallel", "arbitrary")),
    )(up, up, up, up, up, da, da, conv_w, conv_b)


def _sum_rows8(parts, name):
    n8, w = parts.shape
    n = n8 // 8
    cw = _tile(w, 2048)

    def body(p_ref, o_ref):
        acc = p_ref[0:8, :]
        for k in range(1, n):
            acc = acc + p_ref[8 * k:8 * k + 8, :]
        o_ref[...] = acc

    return pl.pallas_call(body, out_shape=jax.ShapeDtypeStruct((8, w), F32), grid=(w // cw,),
                          in_specs=[pl.BlockSpec((n8, cw), lambda j: (0, j))], out_specs=pl.BlockSpec((8, cw), lambda j: (0, j)),
                          name=name, compiler_params=_params(("parallel",)))(parts)


def _ada_fwd(c_all, w_shard, b_shard, name):
    nb, d = c_all.shape
    n = w_shard.shape[1]
    tn = _tile(n, 512)

    def body(c_ref, w_ref, b_ref, o_ref):
        cv = c_ref[...]
        cond = (cv * _sigmoid(cv)).astype(BF16)
        o_ref[...] = jnp.dot(cond, w_ref[...].astype(BF16), preferred_element_type=F32) + b_ref[...]

    return pl.pallas_call(
        body, out_shape=jax.ShapeDtypeStruct((nb, n), F32), grid=(n // tn,),
        in_specs=[pl.BlockSpec((nb, d), lambda j: (0, 0)), pl.BlockSpec((d, tn), lambda j: (0, j)),
                  pl.BlockSpec((1, tn), lambda j: (0, j))],
        out_specs=pl.BlockSpec((nb, tn), lambda j: (0, j)), name=name, compiler_params=_params(("parallel",)),
    )(c_all, w_shard, b_shard)


def _adam_update(w, g, m, v):
    m2 = ADAM_B1 * m + (1.0 - ADAM_B1) * g
    v2 = ADAM_B2 * v + (1.0 - ADAM_B2) * (g * g)
    m_hat = m2 / (1.0 - ADAM_B1 ** ADAM_STEP)
    v_hat = v2 / (1.0 - ADAM_B2 ** ADAM_STEP)
    return -ADAM_LR * (m_hat / (jnp.sqrt(v_hat) + ADAM_EPS) + ADAM_WD * w), m2, v2


def _ada_bwd_adam(c_all_t, dmod_shard, w, m, v, name):
    d, nb = c_all_t.shape
    n = w.shape[1]
    tr, tn = _tile(d, 512, 8), _tile(n, 512)

    def body(c_ref, dm_ref, w_ref, m_ref, v_ref, g_ref, dl_ref, m2_ref, v2_ref):
        cv = c_ref[...]
        cond = cv * _sigmoid(cv)
        g = cond[:, 0:1] * dm_ref[0:1, :]
        for b in range(1, nb):
            g = g + cond[:, b:b + 1] * dm_ref[b:b + 1, :]
        g_ref[...] = g
        dl_ref[...], m2_ref[...], v2_ref[...] = _adam_update(w_ref[...], g, m_ref[...], v_ref[...])

    blk = pl.BlockSpec((tr, tn), lambda i, j: (i, j))
    out = jax.ShapeDtypeStruct((d, n), F32)
    return pl.pallas_call(
        body, out_shape=(out, out, out, out), grid=(d // tr, n // tn),
        in_specs=[pl.BlockSpec((tr, nb), lambda i, j: (i, 0)), pl.BlockSpec((nb, tn), lambda i, j: (0, j)), blk, blk, blk],
        out_specs=(blk, blk, blk, blk), name=name, compiler_params=_params(("parallel", "parallel")),
    )(c_all_t, dmod_shard, w, m, v)


def _adam(w, g, m, v, name):
    r, c = w.shape
    tr = _tile(r, 256, 8)

    def body(w_ref, g_ref, m_ref, v_ref, dl_ref, m2_ref, v2_ref):
        dl_ref[...], m2_ref[...], v2_ref[...] = _adam_update(w_ref[...], g_ref[...], m_ref[...], v_ref[...])

    blk = pl.BlockSpec((tr, c), lambda i: (i, 0))
    out = jax.ShapeDtypeStruct((r, c), F32)
    return pl.pallas_call(body, out_shape=(out, out, out), grid=(r // tr,), in_specs=[blk] * 4, out_specs=(blk,) * 3,
                          name=name, compiler_params=_params(("parallel",)))(w, g, m, v)


def _sum_devices(gathered, name):
    nd, r, c = gathered.shape
    tr = _tile(r, 64, 8)

    def body(g_ref, o_ref):
        acc = g_ref[0]
        for k in range(1, nd):
            acc = acc + g_ref[k]
        o_ref[...] = acc

    return pl.pallas_call(body, out_shape=jax.ShapeDtypeStruct((r, c), F32), grid=(r // tr,),
                          in_specs=[pl.BlockSpec((nd, tr, c), lambda i: (0, i, 0))], out_specs=pl.BlockSpec((tr, c), lambda i: (i, 0)),
                          name=name, compiler_params=_params(("parallel",)))(gathered)


def _place():
    x, y, c = lax.axis_index("x"), lax.axis_index("y"), lax.axis_index("c")
    chips = [(1 - x, y), (x, 1 - y), (1 - x, 1 - y)]
    return x, y, c, chips


def _all_gather8(block, name):
    m_per, n = block.shape

    def body(x_ref, out_ref, send_sems, recv_sems, local_sem):
        x, y, c, chips = _place()
        me, sibling = (x, y, c), (x, y, 1 - c)

        def rows(px, py, pc):
            return out_ref.at[pl.ds((4 * px + 2 * py + pc) * m_per, m_per), :]

        def copy(k, blk, to, src=None):
            return pltpu.make_async_remote_copy(
                src_ref=rows(*blk) if src is None else src, dst_ref=rows(*blk), send_sem=send_sems.at[k],
                recv_sem=recv_sems.at[k], device_id=to, device_id_type=MESH)

        mine = pltpu.make_async_copy(x_ref, rows(*me), local_sem)
        mine.start()
        first = [copy(0, me, sibling, src=x_ref)]
        first += [copy(1 + j, me, (*chip, c), src=x_ref) for j, chip in enumerate(chips)]
        for cp in first:
            cp.start()
        passed = [copy(4 + j, (*chip, c), sibling) for j, chip in enumerate(chips)]
        for j, chip in enumerate(chips):
            copy(1 + j, (*chip, c), me).wait_recv()
            passed[j].start()
        copy(0, sibling, me).wait_recv()
        for j, chip in enumerate(chips):
            copy(4 + j, (*chip, 1 - c), me).wait_recv()
        for cp in first + passed:
            cp.wait_send()
        mine.wait()

    return pl.pallas_call(
        body,
        out_shape=jax.ShapeDtypeStruct((N_DEV * m_per, n), block.dtype),
        in_specs=[pl.BlockSpec(memory_space=pltpu.VMEM)],
        out_specs=pl.BlockSpec(memory_space=pltpu.VMEM),
        scratch_shapes=[pltpu.SemaphoreType.DMA((7,)), pltpu.SemaphoreType.DMA((7,)), pltpu.SemaphoreType.DMA],
        name=name,
        compiler_params=pltpu.CompilerParams(vmem_limit_bytes=VMEM_LIMIT_BYTES),
    )(block)


ANY = pl.BlockSpec(memory_space=pl.ANY)


def _place_shard(shard, name):
    r, k = shard.shape
    tb = _tile(r, 512, 16)
    nb = r // tb
    chip = (2 * lax.axis_index("x") + lax.axis_index("y")).astype(jnp.int32).reshape(1)

    def body(j_ref, s_ref, o_ref):
        o_ref[...] = s_ref[...].astype(BF16)

    return pl.pallas_call(
        body, out_shape=jax.ShapeDtypeStruct((N_CHIPS * r, k), BF16),
        grid_spec=pltpu.PrefetchScalarGridSpec(
            num_scalar_prefetch=1, grid=(nb,),
            in_specs=[pl.BlockSpec((tb, k), lambda i, j_ref: (i, 0))],
            out_specs=pl.BlockSpec((tb, k), lambda i, j_ref: (j_ref[0] * nb + i, 0))),
        name=name, compiler_params=_params(("parallel",)),
    )(chip, shard)


HBM_SPEC = pl.BlockSpec(memory_space=pltpu.HBM)
SEM_SPEC = pl.BlockSpec(memory_space=pltpu.SEMAPHORE)
TOKEN_SPEC = pl.BlockSpec(memory_space=pltpu.VMEM)
SPLIT_COPY = pltpu.CompilerParams(has_side_effects=pltpu.SideEffectType.DATAFLOW_SIDE_EFFECTING)


def _in_hbm(arrays):
    return [pltpu.with_memory_space_constraint(a, pltpu.HBM) for a in arrays]


def _hbm_like(arrays):
    return tuple(pltpu.HBM(a.shape, a.dtype) for a in arrays)


def _token_shape():
    return jax.ShapeDtypeStruct((8, LANES), F32)


def _gathered_rows(buf, px, py, half):
    r = buf.shape[0] // N_CHIPS
    return buf.at[pl.ds(pl.multiple_of((2 * px + py) * r + half * (r // 2), 16), r // 2), :]


def _gather_start(groups, name):
    sizes = [len(g) for g in groups]
    flat = [b for g in groups for b in g]
    nb, ng = len(flat), len(groups)

    def body(*refs):
        bufs = refs[:nb]
        sems = refs[nb:nb + 2 * ng]
        token = refs[-1]
        x, y, c, chips = _place()
        pos = 0
        for gi, nw in enumerate(sizes):
            for k, chip in enumerate(chips):
                for w in range(nw):
                    mine = _gathered_rows(bufs[pos + w], x, y, c)
                    pltpu.make_async_remote_copy(src_ref=mine, dst_ref=mine, send_sem=sems[2 * gi].at[k * nw + w], recv_sem=sems[2 * gi + 1].at[k * nw + w],
                                                 device_id=(*chip, c), device_id_type=MESH).start()
            pos += nw
        token[...] = jnp.zeros_like(token)

    sem_shapes = tuple(pltpu.SemaphoreType.DMA((3 * n,)) for n in sizes for _ in range(2))
    outs = pl.pallas_call(
        body, name=name, out_shape=sem_shapes + _hbm_like(flat) + (_token_shape(),),
        in_specs=[HBM_SPEC] * nb, out_specs=(SEM_SPEC,) * (2 * ng) + (HBM_SPEC,) * nb + (TOKEN_SPEC,),
        input_output_aliases={i: 2 * ng + i for i in range(nb)}, compiler_params=SPLIT_COPY,
    )(*_in_hbm(flat))
    res, pos = [], 2 * ng
    for gi, n in enumerate(sizes):
        res.append((outs[2 * gi], outs[2 * gi + 1], list(outs[pos:pos + n])))
        pos += n
    return res, outs[-1]


def _gather_forward(bufs, ici_send, ici_recv, after, name):
    nw = len(bufs)

    def body(*refs):
        b = refs[:nw]
        isend, irecv = refs[nw], refs[nw + 1]
        dsend, drecv = refs[nw + 3], refs[nw + 4]
        x, y, c, chips = _place()
        for k, chip in enumerate(chips):
            for w in range(nw):
                landed = _gathered_rows(b[w], *chip, c)
                pltpu.make_async_remote_copy(src_ref=landed, dst_ref=landed, send_sem=isend.at[k * nw + w], recv_sem=irecv.at[k * nw + w],
                                             device_id=(*chip, c), device_id_type=MESH).wait_recv()
                pltpu.make_async_remote_copy(src_ref=landed, dst_ref=landed, send_sem=dsend.at[k * nw + w], recv_sem=drecv.at[k * nw + w],
                                             device_id=(x, y, 1 - c), device_id_type=MESH).start()
        for k, chip in enumerate(chips):
            for w in range(nw):
                mine = _gathered_rows(b[w], x, y, c)
                pltpu.make_async_remote_copy(src_ref=mine, dst_ref=mine, send_sem=isend.at[k * nw + w], recv_sem=irecv.at[k * nw + w],
                                             device_id=(*chip, c), device_id_type=MESH).wait_send()

    sem = pltpu.SemaphoreType.DMA((3 * nw,))
    outs = pl.pallas_call(
        body, name=name, out_shape=(sem, sem) + _hbm_like(bufs),
        in_specs=[HBM_SPEC] * nw + [SEM_SPEC, SEM_SPEC, ANY], out_specs=(SEM_SPEC, SEM_SPEC) + (HBM_SPEC,) * nw,
        input_output_aliases={i: 2 + i for i in range(nw)}, compiler_params=SPLIT_COPY,
    )(*bufs, ici_send, ici_recv, after)
    return outs[0], outs[1], list(outs[2:])


def _gather_finish(bufs, d2d_send, d2d_recv, name):
    nw = len(bufs)

    def body(*refs):
        b = refs[:nw]
        dsend, drecv = refs[nw], refs[nw + 1]
        x, y, c, chips = _place()
        for k, chip in enumerate(chips):
            for w in range(nw):
                theirs = _gathered_rows(b[w], *chip, 1 - c)
                pltpu.make_async_remote_copy(src_ref=theirs, dst_ref=theirs, send_sem=dsend.at[k * nw + w], recv_sem=drecv.at[k * nw + w],
                                             device_id=(x, y, 1 - c), device_id_type=MESH).wait_recv()
                passed = _gathered_rows(b[w], *chip, c)
                pltpu.make_async_remote_copy(src_ref=passed, dst_ref=passed, send_sem=dsend.at[k * nw + w], recv_sem=drecv.at[k * nw + w],
                                             device_id=(x, y, 1 - c), device_id_type=MESH).wait_send()

    outs = pl.pallas_call(
        body, name=name, out_shape=_hbm_like(bufs), in_specs=[HBM_SPEC] * nw + [SEM_SPEC, SEM_SPEC], out_specs=(HBM_SPEC,) * nw,
        input_output_aliases={i: i for i in range(nw)}, compiler_params=SPLIT_COPY,
    )(*bufs, d2d_send, d2d_recv)
    return list(outs)


def _scatter_start(partials, name):
    nw = len(partials)
    landing = [lax.empty((3,) + p.shape[1:], p.dtype) for p in partials]

    def body(*refs):
        src, land = refs[:nw], refs[nw:2 * nw]
        send_sems, recv_sems = refs[2 * nw], refs[2 * nw + 1]
        token = refs[-1]
        x, y, c, chips = _place()
        for k, chip in enumerate(chips):
            for w in range(nw):
                pltpu.make_async_remote_copy(src_ref=src[w].at[2 * chip[0] + chip[1]], dst_ref=land[w].at[k], send_sem=send_sems.at[k * nw + w],
                                             recv_sem=recv_sems.at[k * nw + w], device_id=(*chip, c), device_id_type=MESH).start()
        token[...] = jnp.zeros_like(token)

    sem = pltpu.SemaphoreType.DMA((3 * nw,))
    outs = pl.pallas_call(
        body, name=name, out_shape=(sem, sem) + _hbm_like(partials) + _hbm_like(landing) + (_token_shape(),),
        in_specs=[HBM_SPEC] * (2 * nw), out_specs=(SEM_SPEC, SEM_SPEC) + (HBM_SPEC,) * (2 * nw) + (TOKEN_SPEC,),
        input_output_aliases={i: 2 + i for i in range(2 * nw)}, compiler_params=SPLIT_COPY,
    )(*_in_hbm(partials), *_in_hbm(landing))
    return (outs[0], outs[1], list(outs[2:2 + nw]), list(outs[2 + nw:2 + 2 * nw])), outs[-1]


def _scatter_wait(started, after, name):
    send_sems, recv_sems, partials, landing = started
    nw = len(partials)

    def body(*refs):
        src, land = refs[:nw], refs[nw:2 * nw]
        ssem, rsem = refs[2 * nw], refs[2 * nw + 1]
        x, y, c, chips = _place()
        for k, chip in enumerate(chips):
            for w in range(nw):
                cp = pltpu.make_async_remote_copy(src_ref=src[w].at[2 * chip[0] + chip[1]], dst_ref=land[w].at[k], send_sem=ssem.at[k * nw + w],
                                                  recv_sem=rsem.at[k * nw + w], device_id=(*chip, c), device_id_type=MESH)
                cp.wait_send()
                cp.wait_recv()

    outs = pl.pallas_call(
        body, name=name, out_shape=_hbm_like(partials) + _hbm_like(landing),
        in_specs=[HBM_SPEC] * (2 * nw) + [SEM_SPEC, SEM_SPEC, ANY], out_specs=(HBM_SPEC,) * (2 * nw),
        input_output_aliases={i: i for i in range(2 * nw)}, compiler_params=SPLIT_COPY,
    )(*partials, *landing, send_sems, recv_sems, after)
    return list(outs[:nw]), list(outs[nw:])


def _swap_halves(grads, name):
    nw = len(grads)

    def body(*refs):
        ins, outs = refs[:nw], refs[nw:2 * nw]
        send_sems, recv_sems = refs[2 * nw:]
        x, y, c, _ = _place()
        copies = []
        for w in range(nw):
            r = grads[w].shape[0] // N_CHIPS
            h = r // 2
            for j in range(N_CHIPS):
                copies.append(pltpu.make_async_remote_copy(
                    src_ref=ins[w].at[pl.ds(pl.multiple_of(j * r + (1 - c) * h, 16), h), :], dst_ref=outs[w].at[pl.ds(j * h, h), :],
                    send_sem=send_sems.at[w, j], recv_sem=recv_sems.at[w, j], device_id=(x, y, 1 - c), device_id_type=MESH))
                copies[-1].start()
        for cp in copies:
            cp.wait()

    sem = pltpu.SemaphoreType.DMA((nw, N_CHIPS))
    return pl.pallas_call(
        body, out_shape=tuple(jax.ShapeDtypeStruct((g.shape[0] // 2, g.shape[1]), g.dtype) for g in grads),
        in_specs=[ANY] * nw, out_specs=(ANY,) * nw, scratch_shapes=[sem, sem], name=name,
    )(*grads)


def _add_halves(grad, other, name):
    k = grad.shape[1]
    h = other.shape[0] // N_CHIPS
    tb = _tile(h, 512, 16)
    g4 = grad.reshape(N_CHIPS, 2, h, k)
    o3 = other.reshape(N_CHIPS, h, k)
    core = lax.axis_index("c").astype(jnp.int32).reshape(1)

    def body(c_ref, g_ref, o_ref, p_ref):
        p_ref[...] = (g_ref[...].astype(F32) + o_ref[...].astype(F32)).astype(BF16)

    return pl.pallas_call(
        body, out_shape=jax.ShapeDtypeStruct((N_CHIPS, h, k), BF16),
        grid_spec=pltpu.PrefetchScalarGridSpec(
            num_scalar_prefetch=1, grid=(N_CHIPS, h // tb),
            in_specs=[pl.BlockSpec((None, None, tb, k), lambda j, i, c_ref: (j, c_ref[0], i, 0)),
                      pl.BlockSpec((None, tb, k), lambda j, i, c_ref: (j, i, 0))],
            out_specs=pl.BlockSpec((None, tb, k), lambda j, i, c_ref: (j, i, 0))),
        name=name, compiler_params=_params(("parallel", "parallel")),
    )(core, g4, o3)


def _add_partials(partial, others, name):
    _, h, k = partial.shape
    tb = _tile(h, 512, 16)
    nb = h // tb
    place = jnp.stack([2 * lax.axis_index("x") + lax.axis_index("y"), lax.axis_index("c")]).astype(jnp.int32)

    def body(s_ref, p_ref, o0_ref, o1_ref, o2_ref, f_ref):
        f_ref[...] = ((p_ref[...].astype(F32) + o0_ref[...].astype(F32)) + o1_ref[...].astype(F32)) + o2_ref[...].astype(F32)

    def other(s):
        return pl.BlockSpec((None, tb, k), lambda i, s_ref, s=s: (s, i, 0))

    return pl.pallas_call(
        body, out_shape=jax.ShapeDtypeStruct((2 * h, k), F32),
        grid_spec=pltpu.PrefetchScalarGridSpec(
            num_scalar_prefetch=1, grid=(nb,),
            in_specs=[pl.BlockSpec((None, tb, k), lambda i, s_ref: (s_ref[0], i, 0)), other(0), other(1), other(2)],
            out_specs=pl.BlockSpec((tb, k), lambda i, s_ref: (s_ref[1] * nb + i, 0))),
        name=name, compiler_params=_params(("parallel",)),
    )(place, partial, others, others, others)


def _share_halves(fulls, name):
    nw = len(fulls)

    def body(*refs):
        ins, outs = refs[:nw], refs[nw:2 * nw]
        send_sems, recv_sems = refs[2 * nw:]
        x, y, c, _ = _place()
        copies = []
        for w in range(nw):
            h = fulls[w].shape[0] // 2
            start = pl.multiple_of(c * h, 8)
            copies.append(pltpu.make_async_remote_copy(
                src_ref=ins[w].at[pl.ds(start, h), :], dst_ref=outs[w].at[pl.ds(start, h), :], send_sem=send_sems.at[w],
                recv_sem=recv_sems.at[w], device_id=(x, y, 1 - c), device_id_type=MESH))
            copies[-1].start()
        for cp in copies:
            cp.wait()

    sem = pltpu.SemaphoreType.DMA((nw,))
    return pl.pallas_call(
        body, out_shape=tuple(jax.ShapeDtypeStruct(f.shape, f.dtype) for f in fulls),
        in_specs=[ANY] * nw, out_specs=(ANY,) * nw, scratch_shapes=[sem, sem], name=name,
        input_output_aliases={w: w for w in range(nw)},
    )(*fulls)


def _forward_then_finish(started_group, after, tag):
    ici_send, ici_recv, bufs = started_group
    d2d_send, d2d_recv, bufs = _gather_forward(bufs, ici_send, ici_recv, after, f"gather_forward_{tag}")
    return _gather_finish(bufs, d2d_send, d2d_recv, f"gather_finish_{tag}")


def _reduce_start(grads, tag):
    from_sibling = _swap_halves(grads, f"swap_halves_{tag}")
    chip_sums = [_add_halves(g, o, f"add_halves_{tag}_{i}") for i, (g, o) in enumerate(zip(grads, from_sibling))]
    return _scatter_start(chip_sums, f"scatter_start_{tag}")


def _reduce_finish(started, after, tag):
    chip_sums, from_chips = _scatter_wait(started, after, f"scatter_wait_{tag}")
    fulls = [_add_partials(p, o, f"add_partials_{tag}_{i}") for i, (p, o) in enumerate(zip(chip_sums, from_chips))]
    return _share_halves(fulls, f"share_halves_{tag}")


def _flatten_pad(parts, cols=SMALL_COLS):
    flat = jnp.concatenate([p.reshape(-1) for p in parts])
    rows = -(-flat.shape[0] // (8 * cols)) * 8
    return jnp.pad(flat, (0, rows * cols - flat.shape[0])).reshape(rows, cols)


def _split_flat(buf, shapes):
    flat = buf.reshape(-1)
    out, off = [], 0
    for s in shapes:
        n = math.prod(s)
        out.append(flat[off:off + n].reshape(s))
        off += n
    return out


def _ssm_setup(ssm_a_re, ssm_a_im, ssm_log_dt, ssm_b_re, ssm_b_im, ssm_c_re, ssm_c_im):
    lam_r, lam_i, bbar_r, bbar_i = _ssm_discretize(ssm_a_re, ssm_a_im, ssm_log_dt, ssm_b_re, ssm_b_im)
    tab_f, tab_b = _scan_tables(lam_r, lam_i)
    pk = {"br": _pack_in(bbar_r), "bi": _pack_in(bbar_i), "cr": _pack_out(ssm_c_re), "ci": _pack_out(ssm_c_im)}
    packs = {k: v.astype(BF16) for k, v in pk.items()}
    packs.update({"brt": jnp.swapaxes(packs["br"], 1, 2), "bit": jnp.swapaxes(packs["bi"], 1, 2),
                  "crt": jnp.swapaxes(packs["cr"], 1, 2), "cit": jnp.swapaxes(packs["ci"], 1, 2)})
    return packs, tab_f, tab_b


def _local_step(xs, target, mod, w_in_t, later_weights, ffn_grads_ready, norm_mix_g, attn_sinks, ssm, norm_ffn_g, conv_w_full,
                ffn_conv_b, final_g, aw, sw, ff):
    l, d = xs.shape
    u_off = aw + 2 * KV_WIDTH
    ga_off = u_off + sw
    gs_off = ga_off + d
    packs, tab_f, tab_b = _ssm_setup(*ssm[:7])
    dvec = ssm[7].reshape(1, sw)

    h1 = _norm_mod(xs, norm_mix_g, mod, 1, 0, "norm_mod1")
    proj = _matmul(h1, w_in_t, "nt", "mm_in")
    attn = _attn_fwd(proj, attn_sinks, aw, "attn_fwd")
    ys = _s5_fwd(proj, u_off, packs, dvec, tab_f, sw, "s5_fwd")
    gy = _gelu(ys, "gelu")
    w_ap_t, w_glu_t, w_out_f, w_up_t, w_down_f = later_weights(gy)
    attn_out = _matmul(attn, w_ap_t, "nt", "mm_attn_proj")
    glu = _matmul(gy, w_glu_t, "nt", "mm_glu")
    mixed = _mix(proj, ga_off, gs_off, attn_out, glu, d, "mix")
    mo = _matmul(mixed, w_out_f, "nn", "mm_out", out_dtype=F32)
    x2, h2 = _resid_norm_mod(xs, mo, norm_ffn_g, mod, 2, 4, 3, "resid_norm_mod2")
    up = _matmul(h2, w_up_t, "nt", "mm_up")
    act = _conv_act(up, conv_w_full, ffn_conv_b, ff, "conv_act")
    fo = _matmul(act, w_down_f, "nn", "mm_down", out_dtype=F32)
    loss_part, d_final_g, d_gate2, dx3, dfo = _final_loss(x2, fo, mod, 5, final_g.reshape(1, d), target, "final_loss")

    dact = _matmul(dfo, w_down_f, "nt", "mm_down_dx")
    g_down = _matmul(act, dfo, "tn", "mm_down_dw")
    dup, dcw_parts, dcb_parts = _conv_act_bwd(up, dact, conv_w_full, ffn_conv_b, ff, "conv_act_bwd")
    d_conv_w = _sum_rows8(dcw_parts, "sum_conv_w")[:3]
    d_conv_b = _sum_rows8(dcb_parts, "sum_conv_b")[:1]
    dh2 = _matmul(dup, w_up_t, "nn", "mm_up_dx", out_dtype=F32)
    g_up = _matmul(dup, h2, "tn", "mm_up_dw")
    mod = ffn_grads_ready(g_up, g_down, mod)
    dx2, d_shift2, d_scale2, d_gain2, dmo, d_gate1 = _norm_mod_bwd(dh2, x2, dx3, norm_ffn_g, mod, 4, "norm_mod2_bwd", branch=mo, gate_col=2)
    dmixed = _matmul(dmo, w_out_f, "nt", "mm_out_dx")
    g_out = _matmul(mixed, dmo, "tn", "mm_out_dw")
    dga, dgs, dattn_out, dglu = _mix_bwd(proj, ga_off, gs_off, attn_out, glu, dmixed, d, "mix_bwd")
    dgy = _matmul(dglu, w_glu_t, "nn", "mm_glu_dx")
    g_glu = _matmul(dglu, gy, "tn", "mm_glu_dw")
    dys = _gelu_bwd(ys, dgy, "gelu_bwd")
    du, dlam, dbr_p, dbi_p, dcr_p, dci_p, d_dvec = _s5_bwd(proj, u_off, dys, packs, dvec, tab_f, tab_b, sw, "s5_bwd")
    dattn = _matmul(dattn_out, w_ap_t, "nn", "mm_attn_proj_dx")
    g_ap = _matmul(dattn_out, attn, "tn", "mm_attn_proj_dw")
    dq, dkv_cur, dkv_prev, d_sinks = _attn_bwd(proj, attn_sinks, dattn, aw, "attn_bwd")
    dkv = dkv_cur + jnp.concatenate([dkv_prev[ATTN_BLOCK:], jnp.zeros((ATTN_BLOCK, 2 * KV_WIDTH), F32)], axis=0)
    dproj = jnp.concatenate([dq, dkv.astype(BF16), du, dga, dgs], axis=1)
    dh1 = _matmul(dproj, w_in_t, "nn", "mm_in_dx", out_dtype=F32)
    g_in = _matmul(dproj, h1, "tn", "mm_in_dw")
    grad_x, d_shift1, d_scale1, d_gain1 = _norm_mod_bwd(dh1, xs, dx2, norm_mix_g, mod, 1, "norm_mod1_bwd")

    dmod = jnp.concatenate([d_shift1, d_scale1, d_gate1, d_shift2, d_scale2, d_gate2], axis=1)
    small_parts = [dmod, d_gain1, d_sinks, dlam[0], dlam[1], _unpack_diag(dbr_p, SSM_STATE, SSM_GROUP),
                   _unpack_diag(dbi_p, SSM_STATE, SSM_GROUP), _unpack_diag(dcr_p, SSM_STATE, SSM_GROUP),
                   _unpack_diag(dci_p, SSM_STATE, SSM_GROUP), d_dvec, d_gain2, d_conv_b, d_conv_w, d_final_g]
    return loss_part, grad_x, [g_in, g_ap, g_glu, g_out], small_parts


def _kernel_impl(x, c, ada_w, ada_b, norm_mix_g, w_in, attn_sinks, w_attn_proj, ssm_a_re, ssm_a_im, ssm_log_dt, ssm_b_re, ssm_b_im,
                 ssm_c_re, ssm_c_im, ssm_d, w_ssm_glu, w_out, norm_ffn_g, w_ffn_up, ffn_conv_w, ffn_conv_b, w_ffn_down, final_g,
                 loss_target, ms, vs):
    ax, ay, ac = lax.axis_index("x"), lax.axis_index("y"), lax.axis_index("c")
    chip = 2 * ax + ay
    batch_row = 4 * ax + 2 * ay + ac
    d = x.shape[2]
    aw = w_attn_proj.shape[1]
    sw = w_ssm_glu.shape[1]
    ff = N_CHIPS * ffn_conv_w.shape[2]
    ngroups = sw // SSM_GROUP

    c_all = _all_gather8(jnp.pad(c, ((0, 7), (0, 0))), "gather_c").reshape(N_DEV, 8, d)[:, 0, :]
    ncol = ada_w.shape[2]
    b_shard = lax.dynamic_slice(ada_b, (0, chip * ncol), (1, ncol))
    mod_blk = _ada_fwd(c_all, ada_w[0], b_shard, "ada_fwd")
    mod_all = _all_gather8(mod_blk, "gather_mod").reshape(N_CHIPS, 2, 8, ncol)[:, 0]
    mod = lax.dynamic_slice(mod_all, (0, batch_row, 0), (N_CHIPS, 1, ncol)).reshape(1, 6 * d)

    shards = [w_in[0].T.astype(BF16), w_attn_proj[0].T.astype(BF16), w_ssm_glu[0].T.astype(BF16), w_out[0],
              w_ffn_up[0].T.astype(BF16), w_ffn_down[0]]
    placed = [_place_shard(s, f"place_shard_{i}") for i, s in enumerate(shards)]
    conv_w_all = _all_gather8(jnp.pad(ffn_conv_w[0], ((0, 5), (0, 0))), "gather_conv_w")
    conv_w_full = conv_w_all.reshape(N_CHIPS, 2, 8, ff // N_CHIPS)[:, 0, :3].transpose(1, 0, 2).reshape(3, ff)
    (first, later), started = _gather_start([placed[:1], placed[1:]], "gather_start")
    (w_in_t,) = _forward_then_finish(first, started, "w_in")
    mod = mod + started[0:1, 0:1]

    def later_weights(after):
        return _forward_then_finish(later, after, "later")

    pending = {}

    def ffn_grads_ready(g_up, g_down, mod_now):
        pending["ffn"], token = _reduce_start([g_up, g_down], "ffn")
        return mod_now + token[0:1, 0:1]

    ssm = (ssm_a_re[0], ssm_a_im[0], ssm_log_dt[0], ssm_b_re[0], ssm_b_im[0], ssm_c_re[0], ssm_c_im[0], ssm_d[0])
    loss_part, grad_x, grads, small_parts = _local_step(
        x[0], loss_target[0], mod, w_in_t, later_weights, ffn_grads_ready, norm_mix_g, attn_sinks, ssm, norm_ffn_g, conv_w_full,
        ffn_conv_b, final_g, aw, sw, ff)
    loss = lax.psum(loss_part[0, 0], ("x", "y", "c"))
    pending["rest"], rest_token = _reduce_start(grads, "rest")
    gup_t, grad_w_down = _reduce_finish(pending["ffn"], rest_token, "ffn")
    grad_w_up = gup_t.T

    small_shapes = [p.shape for p in small_parts]
    part_buf = _flatten_pad(small_parts)
    rows = part_buf.shape[0]
    gathered = _all_gather8(part_buf, "gather_small").reshape(N_DEV, rows, SMALL_COLS)
    summed = _sum_devices(gathered, "sum_small")
    (s_dmod, s_gain1, s_sinks, s_lr, s_li, s_bbr, s_bbi, s_cr, s_ci, s_dd, s_gain2, s_cb, s_cw, s_fg) = _split_flat(summed, small_shapes)
    _, ssm_vjp = jax.vjp(_ssm_discretize, *ssm[:5])
    g_a_re, g_a_im, g_log_dt, g_b_re, g_b_im = ssm_vjp((s_lr.reshape(ngroups, SSM_STATE), s_li.reshape(ngroups, SSM_STATE), s_bbr, s_bbi))
    g_c_re, g_c_im = jnp.swapaxes(s_cr, 1, 2), jnp.swapaxes(s_ci, 1, 2)
    g_conv_w = lax.dynamic_slice(s_cw, (0, chip * (ff // N_CHIPS)), (3, ff // N_CHIPS))

    dmod_all = gathered.reshape(N_DEV, -1)[:, :6 * d]
    dmod_shard = lax.dynamic_slice(dmod_all, (0, chip * ncol), (N_DEV, ncol))
    ada_res = _ada_bwd_adam(c_all.T, dmod_shard, ada_w[0], ms["ada_w"][0], vs["ada_w"][0], "ada_bwd_adam")

    res = {"ada_w": tuple(o[None] for o in ada_res)}

    def adam_big(nm, w, g):
        res[nm] = (g[None],) + tuple(o[None] for o in _adam(w[0], g, ms[nm][0], vs[nm][0], "adam_" + nm))

    adam_big("w_ffn_up", w_ffn_up, grad_w_up)
    adam_big("w_ffn_down", w_ffn_down, grad_w_down)

    small = [("ada_b", ada_b, s_dmod), ("norm_mix_g", norm_mix_g, s_gain1), ("attn_sinks", attn_sinks, s_sinks),
             ("ssm_a_re", ssm_a_re, g_a_re), ("ssm_a_im", ssm_a_im, g_a_im), ("ssm_log_dt", ssm_log_dt, g_log_dt),
             ("ssm_b_re", ssm_b_re, g_b_re), ("ssm_b_im", ssm_b_im, g_b_im), ("ssm_c_re", ssm_c_re, g_c_re),
             ("ssm_c_im", ssm_c_im, g_c_im), ("ssm_d", ssm_d, s_dd), ("norm_ffn_g", norm_ffn_g, s_gain2),
             ("ffn_conv_w", ffn_conv_w, g_conv_w), ("ffn_conv_b", ffn_conv_b, s_cb), ("final_g", final_g, s_fg)]
    shapes = [t[1].shape for t in small]
    bufs = [_flatten_pad([t[1] for t in small]), _flatten_pad([t[2] for t in small]),
            _flatten_pad([ms[t[0]] for t in small]), _flatten_pad([vs[t[0]] for t in small])]
    s_delta, s_m, s_v = _adam(*bufs, "adam_small")
    for t, dl, m2, v2 in zip(small, _split_flat(s_delta, shapes), _split_flat(s_m, shapes), _split_flat(s_v, shapes)):
        res[t[0]] = (t[2].reshape(t[1].shape), dl, m2, v2)

    gi_t, gap_t, gglu_t, grad_w_out = _reduce_finish(pending["rest"], s_delta, "rest")
    adam_big("w_in", w_in, gi_t.T)
    adam_big("w_attn_proj", w_attn_proj, gap_t.T)
    adam_big("w_ssm_glu", w_ssm_glu, gglu_t.T)
    adam_big("w_out", w_out, grad_w_out)

    outs = [loss, grad_x[None]]
    for i in range(4):
        outs += [res[nm][i] for nm in WEIGHT_ORDER]
    return tuple(outs)


WEIGHT_ORDER = ("ada_w", "ada_b", "norm_mix_g", "w_in", "attn_sinks", "w_attn_proj", "ssm_a_re", "ssm_a_im", "ssm_log_dt", "ssm_b_re",
                "ssm_b_im", "ssm_c_re", "ssm_c_im", "ssm_d", "w_ssm_glu", "w_out", "norm_ffn_g", "w_ffn_up", "ffn_conv_w", "ffn_conv_b",
                "w_ffn_down", "final_g")


def kernel(x, c, ada_w, ada_b, norm_mix_g, w_in, attn_sinks, w_attn_proj, ssm_a_re, ssm_a_im, ssm_log_dt, ssm_b_re, ssm_b_im, ssm_c_re, ssm_c_im, ssm_d, w_ssm_glu, w_out, norm_ffn_g, w_ffn_up, ffn_conv_w, ffn_conv_b, w_ffn_down, final_g, loss_target, m_ada_w, m_ada_b, m_norm_mix_g, m_w_in, m_attn_sinks, m_w_attn_proj, m_ssm_a_re, m_ssm_a_im, m_ssm_log_dt, m_ssm_b_re, m_ssm_b_im, m_ssm_c_re, m_ssm_c_im, m_ssm_d, m_w_ssm_glu, m_w_out, m_norm_ffn_g, m_w_ffn_up, m_ffn_conv_w, m_ffn_conv_b, m_w_ffn_down, m_final_g, v_ada_w, v_ada_b, v_norm_mix_g, v_w_in, v_attn_sinks, v_w_attn_proj, v_ssm_a_re, v_ssm_a_im, v_ssm_log_dt, v_ssm_b_re, v_ssm_b_im, v_ssm_c_re, v_ssm_c_im, v_ssm_d, v_w_ssm_glu, v_w_out, v_norm_ffn_g, v_w_ffn_up, v_ffn_conv_w, v_ffn_conv_b, v_w_ffn_down, v_final_g):
    ms = dict(zip(WEIGHT_ORDER, (m_ada_w, m_ada_b, m_norm_mix_g, m_w_in, m_attn_sinks, m_w_attn_proj, m_ssm_a_re, m_ssm_a_im, m_ssm_log_dt,
                                 m_ssm_b_re, m_ssm_b_im, m_ssm_c_re, m_ssm_c_im, m_ssm_d, m_w_ssm_glu, m_w_out, m_norm_ffn_g, m_w_ffn_up,
                                 m_ffn_conv_w, m_ffn_conv_b, m_w_ffn_down, m_final_g)))
    vs = dict(zip(WEIGHT_ORDER, (v_ada_w, v_ada_b, v_norm_mix_g, v_w_in, v_attn_sinks, v_w_attn_proj, v_ssm_a_re, v_ssm_a_im, v_ssm_log_dt,
                                 v_ssm_b_re, v_ssm_b_im, v_ssm_c_re, v_ssm_c_im, v_ssm_d, v_w_ssm_glu, v_w_out, v_norm_ffn_g, v_w_ffn_up,
                                 v_ffn_conv_w, v_ffn_conv_b, v_w_ffn_down, v_final_g)))
    return _kernel_impl(x, c, ada_w, ada_b, norm_mix_g, w_in, attn_sinks, w_attn_proj, ssm_a_re, ssm_a_im, ssm_log_dt, ssm_b_re, ssm_b_im,
                        ssm_c_re, ssm_c_im, ssm_d, w_ssm_glu, w_out, norm_ffn_g, w_ffn_up, ffn_conv_w, ffn_conv_b, w_ffn_down, final_g,
                        loss_target, ms, vs)
```

```python
import math

import jax
import jax.numpy as jnp
from jax import lax
from jax.experimental import pallas as pl
from jax.experimental.pallas import tpu as pltpu

F32 = jnp.float32
BF16 = jnp.bfloat16
MESH = pl.DeviceIdType.MESH

HEAD_DIM = 64
N_KV_HEADS = 2
KV_WIDTH = N_KV_HEADS * HEAD_DIM
ATTN_BLOCK = 128
NEG_INF = -1e30
SSM_GROUP = 16
SSM_STATE = 64
GROUPS_PER_TILE = 8
RMS_EPS = 1e-6
ADAM_LR = 0.001
ADAM_B1 = 0.9
ADAM_B2 = 0.999
ADAM_EPS = 1e-08
ADAM_WD = 0.01
ADAM_STEP = 10
N_CHIPS = 4
N_DEV = 8
VMEM_LIMIT_BYTES = 56 * 1024 * 1024
LANES = 128
SMALL_COLS = 1024


def _tile(dim, target, mult=LANES):
    if dim <= target:
        return dim
    for t in range(target // mult * mult, 0, -mult):
        if dim % t == 0:
            return t
    raise ValueError(f"no tile for {dim}")


def _params(sem=None):
    return pltpu.CompilerParams(dimension_semantics=sem, vmem_limit_bytes=VMEM_LIMIT_BYTES)


def _sigmoid(x):
    return 1.0 / (1.0 + jnp.exp(-x))


def _matmul(a, b, mode, name, out_dtype=BF16, tm=1536, tn=1536, tk=2048):
    if mode == "nn":
        (m, k), (k2, n) = a.shape, b.shape
    elif mode == "nt":
        (m, k), (n, k2) = a.shape, b.shape
    else:
        (k, m), (k2, n) = a.shape, b.shape
    assert k == k2, (a.shape, b.shape, mode)
    tm, tn, tk = _tile(m, tm), _tile(n, tn), _tile(k, tk)
    nk = k // tk
    if mode == "tn":
        a_spec = pl.BlockSpec((tk, tm), lambda i, j, kk: (kk, i))
    else:
        a_spec = pl.BlockSpec((tm, tk), lambda i, j, kk: (i, kk))
    if mode == "nt":
        b_spec = pl.BlockSpec((tn, tk), lambda i, j, kk: (j, kk))
    else:
        b_spec = pl.BlockSpec((tk, tn), lambda i, j, kk: (kk, j))
    dims = {"nn": (((1,), (0,)), ((), ())), "nt": (((1,), (1,)), ((), ())), "tn": (((0,), (0,)), ((), ()))}[mode]

    def body(a_ref, b_ref, o_ref, acc_ref):
        kk = pl.program_id(2)

        @pl.when(kk == 0)
        def _():
            acc_ref[...] = jnp.zeros_like(acc_ref)

        acc_ref[...] += lax.dot_general(a_ref[...], b_ref[...], dims, preferred_element_type=F32)

        @pl.when(kk == nk - 1)
        def _():
            o_ref[...] = acc_ref[...].astype(o_ref.dtype)

    return pl.pallas_call(
        body,
        out_shape=jax.ShapeDtypeStruct((m, n), out_dtype),
        grid=(m // tm, n // tn, nk),
        in_specs=[a_spec, b_spec],
        out_specs=pl.BlockSpec((tm, tn), lambda i, j, kk: (i, j)),
        scratch_shapes=[pltpu.VMEM((tm, tn), F32)],
        name=name,
        compiler_params=_params(("parallel", "parallel", "arbitrary")),
    )(a, b)


def _row_spec(tl, w, col=0):
    return pl.BlockSpec((tl, w), lambda i, col=col: (i, col))


def _vec_spec(w, col=0):
    return pl.BlockSpec((1, w), lambda i, col=col: (0, col))


def _norm_mod(x, gain, mod, sc_col, sh_col, name):
    l, d = x.shape
    tl = _tile(l, 256, 8)

    def body(x_ref, g_ref, sc_ref, sh_ref, h_ref):
        xv = x_ref[...]
        r = lax.rsqrt(jnp.mean(xv * xv, axis=-1, keepdims=True) + RMS_EPS)
        h_ref[...] = ((xv * r) * g_ref[...] * (1.0 + sc_ref[...]) + sh_ref[...]).astype(BF16)

    return pl.pallas_call(
        body,
        out_shape=jax.ShapeDtypeStruct((l, d), BF16),
        grid=(l // tl,),
        in_specs=[_row_spec(tl, d), _vec_spec(d), _vec_spec(d, sc_col), _vec_spec(d, sh_col)],
        out_specs=_row_spec(tl, d),
        name=name,
        compiler_params=_params(("parallel",)),
    )(x, gain, mod, mod)


def _resid_norm_mod(x, mo, gain, mod, gate_col, sc_col, sh_col, name):
    l, d = x.shape
    tl = _tile(l, 256, 8)

    def body(x_ref, mo_ref, g_ref, gate_ref, sc_ref, sh_ref, x2_ref, h_ref):
        xv = x_ref[...] + gate_ref[...] * mo_ref[...]
        x2_ref[...] = xv
        r = lax.rsqrt(jnp.mean(xv * xv, axis=-1, keepdims=True) + RMS_EPS)
        h_ref[...] = ((xv * r) * g_ref[...] * (1.0 + sc_ref[...]) + sh_ref[...]).astype(BF16)

    return pl.pallas_call(
        body,
        out_shape=(jax.ShapeDtypeStruct((l, d), F32), jax.ShapeDtypeStruct((l, d), BF16)),
        grid=(l // tl,),
        in_specs=[_row_spec(tl, d), _row_spec(tl, d), _vec_spec(d), _vec_spec(d, gate_col), _vec_spec(d, sc_col),
                  _vec_spec(d, sh_col)],
        out_specs=(_row_spec(tl, d), _row_spec(tl, d)),
        name=name,
        compiler_params=_params(("parallel",)),
    )(x, mo, gain, mod, mod, mod)


def _final_loss(x2, f, mod, gate_col, final_g, target, name):
    l, d = x2.shape
    tl = _tile(l, 256, 8)

    def body(x2_ref, f_ref, gate_ref, fg_ref, t_ref, loss_ref, dfg_ref, dgate_ref, dx3_ref, df_ref):
        i = pl.program_id(0)
        fv = f_ref[...]
        x3 = x2_ref[...] + gate_ref[...] * fv
        r = lax.rsqrt(jnp.mean(x3 * x3, axis=-1, keepdims=True) + RMS_EPS)
        xh = x3 * r
        err = xh * fg_ref[...] - t_ref[...]
        part = 0.5 * jnp.sum(jnp.mean(err * err, axis=-1, keepdims=True), axis=0, keepdims=True)
        dout = err * (1.0 / d)
        dxh = dout * fg_ref[...]
        dx3 = r * (dxh - xh * jnp.mean(dxh * xh, axis=-1, keepdims=True))
        dx3_ref[...] = dx3
        df_ref[...] = (gate_ref[...] * dx3).astype(BF16)

        @pl.when(i == 0)
        def _():
            loss_ref[...] = jnp.zeros_like(loss_ref)
            dfg_ref[...] = jnp.zeros_like(dfg_ref)
            dgate_ref[...] = jnp.zeros_like(dgate_ref)

        loss_ref[...] += jnp.broadcast_to(part, loss_ref.shape)
        dfg_ref[...] += jnp.sum(dout * xh, axis=0, keepdims=True)
        dgate_ref[...] += jnp.sum(dx3 * fv, axis=0, keepdims=True)

    vec = pl.BlockSpec((1, d), lambda i: (0, 0))
    return pl.pallas_call(
        body,
        out_shape=(jax.ShapeDtypeStruct((1, LANES), F32), jax.ShapeDtypeStruct((1, d), F32),
                   jax.ShapeDtypeStruct((1, d), F32), jax.ShapeDtypeStruct((l, d), F32),
                   jax.ShapeDtypeStruct((l, d), BF16)),
        grid=(l // tl,),
        in_specs=[_row_spec(tl, d), _row_spec(tl, d), _vec_spec(d, gate_col), vec, _row_spec(tl, d)],
        out_specs=(pl.BlockSpec((1, LANES), lambda i: (0, 0)), vec, vec, _row_spec(tl, d), _row_spec(tl, d)),
        name=name,
        compiler_params=_params(("arbitrary",)),
    )(x2, f, mod, final_g, target)


def _norm_mod_bwd(dh, x, dx_res, gain, mod, sc_col, name, branch=None, gate_col=None):
    l, d = x.shape
    tl = _tile(l, 256, 8)
    with_gate = branch is not None

    def body(*refs):
        if with_gate:
            dh_ref, x_ref, dr_ref, g_ref, sc_ref, br_ref, gate_ref, dx_ref, dsh_ref, dsc_ref, dg_ref, dm_ref, dgate_ref = refs
        else:
            dh_ref, x_ref, dr_ref, g_ref, sc_ref, dx_ref, dsh_ref, dsc_ref, dg_ref = refs
        i = pl.program_id(0)
        xv = x_ref[...]
        dhv = dh_ref[...].astype(F32)
        r = lax.rsqrt(jnp.mean(xv * xv, axis=-1, keepdims=True) + RMS_EPS)
        xh = xv * r
        dn = dhv * (1.0 + sc_ref[...])
        dxh = dn * g_ref[...]
        dx = dr_ref[...] + r * (dxh - xh * jnp.mean(dxh * xh, axis=-1, keepdims=True))
        dx_ref[...] = dx

        @pl.when(i == 0)
        def _():
            dsh_ref[...] = jnp.zeros_like(dsh_ref)
            dsc_ref[...] = jnp.zeros_like(dsc_ref)
            dg_ref[...] = jnp.zeros_like(dg_ref)
            if with_gate:
                dgate_ref[...] = jnp.zeros_like(dgate_ref)

        dsh_ref[...] += jnp.sum(dhv, axis=0, keepdims=True)
        dsc_ref[...] += jnp.sum(dhv * (xh * g_ref[...]), axis=0, keepdims=True)
        dg_ref[...] += jnp.sum(dn * xh, axis=0, keepdims=True)
        if with_gate:
            dm_ref[...] = (gate_ref[...] * dx).astype(BF16)
            dgate_ref[...] += jnp.sum(dx * br_ref[...], axis=0, keepdims=True)

    vec = pl.BlockSpec((1, d), lambda i: (0, 0))
    in_specs = [_row_spec(tl, d), _row_spec(tl, d), _row_spec(tl, d), vec, _vec_spec(d, sc_col)]
    args = [dh, x, dx_res, gain, mod]
    out_shape = [jax.ShapeDtypeStruct((l, d), F32)] + [jax.ShapeDtypeStruct((1, d), F32)] * 3
    out_specs = [_row_spec(tl, d), vec, vec, vec]
    if with_gate:
        in_specs += [_row_spec(tl, d), _vec_spec(d, gate_col)]
        args += [branch, mod]
        out_shape += [jax.ShapeDtypeStruct((l, d), BF16), jax.ShapeDtypeStruct((1, d), F32)]
        out_specs += [_row_spec(tl, d), vec]
    return pl.pallas_call(
        body, out_shape=tuple(out_shape), grid=(l // tl,), in_specs=in_specs, out_specs=tuple(out_specs),
        name=name, compiler_params=_params(("arbitrary",)),
    )(*args)


def _attn_mask(n, rows):
    del rows
    qi = lax.broadcasted_iota(jnp.int32, (ATTN_BLOCK, 2 * ATTN_BLOCK), 0)
    kj = lax.broadcasted_iota(jnp.int32, (ATTN_BLOCK, 2 * ATTN_BLOCK), 1)
    rel = qi + ATTN_BLOCK - kj
    return jnp.where((rel >= 0) & (rel < ATTN_BLOCK) & ((kj >= ATTN_BLOCK) | (n > 0)), 0.0, NEG_INF)


def _attn_probs(qs, kh, sink, mask):
    rows = qs.shape[0]
    s = lax.dot_general(qs, kh, (((1,), (1,)), ((), ())), preferred_element_type=F32) * (HEAD_DIM ** -0.5)
    s = s.reshape(-1, ATTN_BLOCK, 2 * ATTN_BLOCK) + mask[None]
    m = jnp.maximum(jnp.max(s, axis=-1, keepdims=True), sink)
    p = jnp.exp(s - m)
    es = jnp.exp(sink - m)
    inv = 1.0 / (jnp.sum(p, axis=-1, keepdims=True) + es)
    return (p * inv).reshape(rows, 2 * ATTN_BLOCK), (es * inv).reshape(rows, 1)


def _stack_heads(src_ref, dst_ref, g, qpk):
    for i in range(qpk):
        h = g * qpk + i
        dst_ref[i * ATTN_BLOCK:(i + 1) * ATTN_BLOCK, :] = src_ref[:, h * HEAD_DIM:(h + 1) * HEAD_DIM]


def _unstack_heads(val, dst_ref, g, qpk):
    for i in range(qpk):
        h = g * qpk + i
        dst_ref[:, h * HEAD_DIM:(h + 1) * HEAD_DIM] = val[i * ATTN_BLOCK:(i + 1) * ATTN_BLOCK, :].astype(dst_ref.dtype)


def _sink_column(sinks):
    return sinks.reshape(-1, 1, 1)


def _sink_spec(nq):
    return pl.BlockSpec((nq, 1, 1), lambda n: (0, 0, 0))


def _attn_specs(aw):
    kvb = aw // (2 * KV_WIDTH)
    q_spec = pl.BlockSpec((ATTN_BLOCK, aw), lambda n: (n, 0))
    kv_cur = pl.BlockSpec((ATTN_BLOCK, 2 * KV_WIDTH), lambda n: (n, kvb))
    kv_prev = pl.BlockSpec((ATTN_BLOCK, 2 * KV_WIDTH), lambda n: (jnp.maximum(n - 1, 0), kvb))
    return q_spec, kv_cur, kv_prev


def _attn_fwd(proj, sinks, aw, name):
    l = proj.shape[0]
    nq = aw // HEAD_DIM
    qpk = nq // N_KV_HEADS
    assert aw % (2 * KV_WIDTH) == 0

    rows = qpk * ATTN_BLOCK

    def body(q_ref, kvc_ref, kvp_ref, sink_ref, o_ref):
        n = pl.program_id(0)
        valid = _attn_mask(n, rows) == 0.0
        kv = jnp.concatenate([kvp_ref[...], kvc_ref[...]], axis=0)
        for h in range(nq):
            g = h // qpk
            qh = q_ref[:, h * HEAD_DIM:(h + 1) * HEAD_DIM]
            kh = kv[:, g * HEAD_DIM:(g + 1) * HEAD_DIM]
            vh = kv[:, KV_WIDTH + g * HEAD_DIM:KV_WIDTH + (g + 1) * HEAD_DIM]
            sink = sink_ref[0:1, h:h + 1]
            s = lax.dot_general(qh, kh, (((1,), (1,)), ((), ())), preferred_element_type=F32) * (HEAD_DIM ** -0.5)
            s = jnp.where(valid, s, NEG_INF)
            m = jnp.maximum(jnp.max(s, axis=-1, keepdims=True), sink)
            p = jnp.exp(s - m)
            p = p * (1.0 / (jnp.sum(p, axis=-1, keepdims=True) + jnp.exp(sink - m)))
            o = jnp.dot(p.astype(BF16), vh, preferred_element_type=F32)
            o_ref[:, h * HEAD_DIM:(h + 1) * HEAD_DIM] = o.astype(BF16)

    q_spec, kv_cur, kv_prev = _attn_specs(aw)
    return pl.pallas_call(
        body,
        out_shape=jax.ShapeDtypeStruct((l, aw), BF16),
        grid=(l // ATTN_BLOCK,),
        in_specs=[q_spec, kv_cur, kv_prev, pl.BlockSpec((1, nq), lambda n: (0, 0))],
        out_specs=pl.BlockSpec((ATTN_BLOCK, aw), lambda n: (n, 0)),
        name=name,
        compiler_params=_params(("parallel",)),
    )(proj, proj, proj, sinks)


def _attn_bwd(proj, sinks, dattn, aw, name):
    l = proj.shape[0]
    nq = aw // HEAD_DIM
    qpk = nq // N_KV_HEADS
    scale = HEAD_DIM ** -0.5

    rows = qpk * ATTN_BLOCK
    tn_dims = (((0,), (0,)), ((), ()))

    def body(q_ref, kvc_ref, kvp_ref, sink_ref, do_ref, dq_ref, dcur_ref, dprev_ref, dsink_ref, q_scr, do_scr):
        n = pl.program_id(0)
        mask = _attn_mask(n, rows)
        kv = jnp.concatenate([kvp_ref[...], kvc_ref[...]], axis=0)
        lane = lax.broadcasted_iota(jnp.int32, (1, nq), 1)
        dsink = jnp.zeros((1, nq), F32)
        dks, dvs = [], []
        for g in range(N_KV_HEADS):
            kh = kv[:, g * HEAD_DIM:(g + 1) * HEAD_DIM]
            vh = kv[:, KV_WIDTH + g * HEAD_DIM:KV_WIDTH + (g + 1) * HEAD_DIM]
            _stack_heads(q_ref, q_scr.at[g], g, qpk)
            _stack_heads(do_ref, do_scr.at[g], g, qpk)
            qs, dos = q_scr[g], do_scr[g]
            p, ps = _attn_probs(qs, kh, sink_ref[g * qpk:(g + 1) * qpk], mask)
            pb = p.astype(BF16)
            o = jnp.dot(pb, vh, preferred_element_type=F32)
            delta = jnp.sum(dos.astype(F32) * o, axis=-1, keepdims=True)
            dp = lax.dot_general(dos, vh, (((1,), (1,)), ((), ())), preferred_element_type=F32)
            ds = (p * (dp - delta)).astype(BF16)
            _unstack_heads(jnp.dot(ds, kh, preferred_element_type=F32) * scale, dq_ref, g, qpk)
            dks.append(lax.dot_general(ds, qs, tn_dims, preferred_element_type=F32) * scale)
            dvs.append(lax.dot_general(pb, dos, tn_dims, preferred_element_type=F32))
            t = ps * delta
            for i in range(qpk):
                part = -jnp.sum(t[i * ATTN_BLOCK:(i + 1) * ATTN_BLOCK, :], axis=0, keepdims=True)
                dsink += jnp.where(lane == g * qpk + i, part, 0.0)
        dkv = jnp.concatenate(dks + dvs, axis=1)
        dprev_ref[...] = dkv[:ATTN_BLOCK]
        dcur_ref[...] = dkv[ATTN_BLOCK:]

        @pl.when(n == 0)
        def _():
            dsink_ref[...] = jnp.zeros_like(dsink_ref)

        dsink_ref[...] += dsink

    q_spec, kv_cur, kv_prev = _attn_specs(aw)
    blk = pl.BlockSpec((ATTN_BLOCK, 2 * KV_WIDTH), lambda n: (n, 0))
    return pl.pallas_call(
        body,
        out_shape=(jax.ShapeDtypeStruct((l, aw), BF16), jax.ShapeDtypeStruct((l, 2 * KV_WIDTH), F32),
                   jax.ShapeDtypeStruct((l, 2 * KV_WIDTH), F32), jax.ShapeDtypeStruct((1, nq), F32)),
        grid=(l // ATTN_BLOCK,),
        in_specs=[q_spec, kv_cur, kv_prev, _sink_spec(nq),
                  pl.BlockSpec((ATTN_BLOCK, aw), lambda n: (n, 0))],
        out_specs=(pl.BlockSpec((ATTN_BLOCK, aw), lambda n: (n, 0)), blk, blk, pl.BlockSpec((1, nq), lambda n: (0, 0))),
        scratch_shapes=[pltpu.VMEM((N_KV_HEADS, rows, HEAD_DIM), BF16)] * 2,
        name=name,
        compiler_params=_params(("arbitrary",)),
    )(proj, proj, proj, _sink_column(sinks), dattn)


def _ssm_discretize(a_re, a_im, log_dt, b_re, b_im):
    dt = jnp.exp(log_dt)[:, None]
    mag = jnp.exp(a_re * dt)
    lr, li = mag * jnp.cos(a_im * dt), mag * jnp.sin(a_im * dt)
    den = a_re * a_re + a_im * a_im
    zr = ((lr - 1.0) * a_re + li * a_im) / den
    zi = (li * a_re - (lr - 1.0) * a_im) / den
    bbar_r = zr[:, :, None] * b_re - zi[:, :, None] * b_im
    bbar_i = zr[:, :, None] * b_im + zi[:, :, None] * b_re
    return lr, li, bbar_r, bbar_i


def _cmul(ar, ai, br, bi):
    return ar * br - ai * bi, ar * bi + ai * br


def _scan_tables(lr, li):
    lr, li = lr.reshape(1, -1), li.reshape(1, -1)
    pows = [(lr, li)]
    for _ in range(7):
        pows.append(_cmul(*pows[-1], lr, li))
    row = jnp.arange(8)[:, None]
    fwd, bwd = [], []
    for d in (1, 2, 4):
        pr, pi = pows[d - 1]
        fwd += [jnp.where(row >= d, pr, 0.0), jnp.where(row >= d, pi, 0.0)]
        bwd += [jnp.where(row < 8 - d, pr, 0.0), jnp.where(row < 8 - d, -pi, 0.0)]
    fwd += [jnp.concatenate([p[0] for p in pows], 0), jnp.concatenate([p[1] for p in pows], 0)]
    bwd += [jnp.concatenate([p[0] for p in pows[::-1]], 0), jnp.concatenate([-p[1] for p in pows[::-1]], 0)]
    return jnp.concatenate(fwd, 0), jnp.concatenate(bwd, 0)


def _pack_in(b):
    g, n, p = b.shape
    t = g // GROUPS_PER_TILE
    eye = jnp.eye(GROUPS_PER_TILE, dtype=b.dtype)
    bb = b.reshape(t, GROUPS_PER_TILE, n, p)
    return jnp.einsum("tgnp,gh->tgphn", bb, eye).reshape(t, GROUPS_PER_TILE * p, GROUPS_PER_TILE * n)


def _pack_out(c):
    g, p, n = c.shape
    t = g // GROUPS_PER_TILE
    eye = jnp.eye(GROUPS_PER_TILE, dtype=c.dtype)
    cc = c.reshape(t, GROUPS_PER_TILE, p, n)
    return jnp.einsum("tgpn,gh->tgnhp", cc, eye).reshape(t, GROUPS_PER_TILE * n, GROUPS_PER_TILE * p)


def _unpack_diag(x, n, p):
    t = x.shape[0]
    xx = x.reshape(t, GROUPS_PER_TILE, n, GROUPS_PER_TILE, p)
    eye = jnp.eye(GROUPS_PER_TILE, dtype=x.dtype)
    return jnp.einsum("tgnhp,gh->tgnp", xx, eye).reshape(t * GROUPS_PER_TILE, n, p)


def _scan_rows(hr_ref, hi_ref, tab_ref, l, reverse, prev_refs=None):
    w = hr_ref.shape[1]
    tabs = [tab_ref[pl.ds(8 * i, 8), :] for i in range(8)]
    nchunk = l // 8
    row = lax.broadcasted_iota(jnp.int32, (8, w), 0)

    def step(s, carry):
        k = nchunk - 1 - s if reverse else s
        t8 = pl.multiple_of(k * 8, 8)
        hr = hr_ref[pl.ds(t8, 8), :]
        hi = hi_ref[pl.ds(t8, 8), :]
        for idx, d in enumerate((1, 2, 4)):
            mr, mi = tabs[2 * idx], tabs[2 * idx + 1]
            shift = 8 - d if reverse else d
            sr = pltpu.roll(hr, shift, 0)
            si = pltpu.roll(hi, shift, 0)
            hr, hi = hr + mr * sr - mi * si, hi + mr * si + mi * sr
        cr, ci = carry[0], carry[1]
        hr, hi = hr + tabs[6] * cr - tabs[7] * ci, hi + tabs[6] * ci + tabs[7] * cr
        hr_ref[pl.ds(t8, 8), :] = hr
        hi_ref[pl.ds(t8, 8), :] = hi
        if not reverse:
            return hr[7:8, :], hi[7:8, :]
        out = (hr[0:1, :], hi[0:1, :])
        if prev_refs is None:
            return out
        fr_ref, fi_ref = prev_refs
        tp = pl.multiple_of(jnp.maximum(k - 1, 0) * 8, 8)
        keep = jnp.where(k > 0, 1.0, 0.0)
        lr_last = fr_ref[pl.ds(tp, 8), :][7:8, :] * keep
        li_last = fi_ref[pl.ds(tp, 8), :][7:8, :] * keep
        pr = jnp.where(row == 0, lr_last, pltpu.roll(fr_ref[pl.ds(t8, 8), :], 1, 0))
        pi = jnp.where(row == 0, li_last, pltpu.roll(fi_ref[pl.ds(t8, 8), :], 1, 0))
        return out + (carry[2] + hr * pr + hi * pi, carry[3] + hi * pr - hr * pi)

    zero = jnp.zeros((1, w), F32)
    init = (zero, zero)
    if reverse and prev_refs is not None:
        init += (jnp.zeros((8, w), F32), jnp.zeros((8, w), F32))
    return lax.fori_loop(0, nchunk, step, init)


def _s5_dims(sw):
    chan = GROUPS_PER_TILE * SSM_GROUP
    states = GROUPS_PER_TILE * SSM_STATE
    assert chan == LANES and sw % chan == 0
    return sw // chan, chan, states


def _s5_fwd(proj, u_off, packs, dvec, tab_f, sw, name):
    l = proj.shape[0]
    nt, chan, states = _s5_dims(sw)
    ch = _tile(l, 512, 8)
    ub = u_off // chan
    assert u_off % chan == 0

    def body(u_ref, br_ref, bi_ref, cr_ref, ci_ref, d_ref, tab_ref, y_ref, hr_ref, hi_ref):
        for i in range(l // ch):
            rows = pl.ds(i * ch, ch)
            u = u_ref[rows, :]
            hr_ref[rows, :] = jnp.dot(u, br_ref[0], preferred_element_type=F32)
            hi_ref[rows, :] = jnp.dot(u, bi_ref[0], preferred_element_type=F32)
        _scan_rows(hr_ref, hi_ref, tab_ref, l, reverse=False)
        for i in range(l // ch):
            rows = pl.ds(i * ch, ch)
            y = jnp.dot(hr_ref[rows, :].astype(BF16), cr_ref[0], preferred_element_type=F32)
            y -= jnp.dot(hi_ref[rows, :].astype(BF16), ci_ref[0], preferred_element_type=F32)
            y_ref[rows, :] = y + d_ref[...] * u_ref[rows, :].astype(F32)

    pin = pl.BlockSpec((1, chan, states), lambda t: (t, 0, 0))
    pout = pl.BlockSpec((1, states, chan), lambda t: (t, 0, 0))
    return pl.pallas_call(
        body,
        out_shape=jax.ShapeDtypeStruct((l, sw), F32),
        grid=(nt,),
        in_specs=[pl.BlockSpec((l, chan), lambda t: (0, ub + t)), pin, pin, pout, pout,
                  pl.BlockSpec((1, chan), lambda t: (0, t)), pl.BlockSpec((64, states), lambda t: (0, t))],
        out_specs=pl.BlockSpec((l, chan), lambda t: (0, t)),
        scratch_shapes=[pltpu.VMEM((l, states), F32), pltpu.VMEM((l, states), F32)],
        name=name,
        compiler_params=_params(("parallel",)),
    )(proj, packs["br"], packs["bi"], packs["cr"], packs["ci"], dvec, tab_f)


def _s5_bwd(proj, u_off, dy, packs, dvec, tab_f, tab_b, sw, name):
    l = proj.shape[0]
    nt, chan, states = _s5_dims(sw)
    ch = _tile(l, 512, 8)
    ub = u_off // chan
    tn_dims = (((0,), (0,)), ((), ()))

    def body(u_ref, dy_ref, br_ref, bi_ref, brt_ref, bit_ref, crt_ref, cit_ref, d_ref, tabf_ref, tabb_ref,
             du_ref, dlam_ref, dbr_ref, dbi_ref, dcr_ref, dci_ref, dd_ref, hr_ref, hi_ref, gr_ref, gi_ref):
        for i in range(l // ch):
            rows = pl.ds(i * ch, ch)
            u = u_ref[rows, :]
            hr_ref[rows, :] = jnp.dot(u, br_ref[0], preferred_element_type=F32)
            hi_ref[rows, :] = jnp.dot(u, bi_ref[0], preferred_element_type=F32)
            dyv = dy_ref[rows, :]
            gr_ref[rows, :] = jnp.dot(dyv, crt_ref[0], preferred_element_type=F32)
            gi_ref[rows, :] = -jnp.dot(dyv, cit_ref[0], preferred_element_type=F32)
        _scan_rows(hr_ref, hi_ref, tabf_ref, l, reverse=False)
        _, _, acc_r, acc_i = _scan_rows(gr_ref, gi_ref, tabb_ref, l, reverse=True, prev_refs=(hr_ref, hi_ref))
        dlam_ref[...] = jnp.concatenate(
            [jnp.sum(acc_r, axis=0, keepdims=True), jnp.sum(acc_i, axis=0, keepdims=True), jnp.zeros((6, states), F32)], axis=0)
        dbr_ref[...] = jnp.zeros_like(dbr_ref)
        dbi_ref[...] = jnp.zeros_like(dbi_ref)
        dcr_ref[...] = jnp.zeros_like(dcr_ref)
        dci_ref[...] = jnp.zeros_like(dci_ref)
        dd = jnp.zeros((1, chan), F32)
        for i in range(l // ch):
            rows = pl.ds(i * ch, ch)
            u = u_ref[rows, :]
            dyv = dy_ref[rows, :]
            grb = gr_ref[rows, :].astype(BF16)
            gib = gi_ref[rows, :].astype(BF16)
            dbr_ref[0] += lax.dot_general(grb, u, tn_dims, preferred_element_type=F32)
            dbi_ref[0] += lax.dot_general(gib, u, tn_dims, preferred_element_type=F32)
            dcr_ref[0] += lax.dot_general(hr_ref[rows, :].astype(BF16), dyv, tn_dims, preferred_element_type=F32)
            dci_ref[0] -= lax.dot_general(hi_ref[rows, :].astype(BF16), dyv, tn_dims, preferred_element_type=F32)
            du = jnp.dot(grb, brt_ref[0], preferred_element_type=F32) + jnp.dot(gib, bit_ref[0], preferred_element_type=F32)
            dyf = dyv.astype(F32)
            du_ref[rows, :] = (du + d_ref[...] * dyf).astype(BF16)
            dd += jnp.sum(dyf * u.astype(F32), axis=0, keepdims=True)
        dd_ref[...] = dd

    pin = pl.BlockSpec((1, chan, states), lambda t: (t, 0, 0))
    pout = pl.BlockSpec((1, states, chan), lambda t: (t, 0, 0))
    seq = pl.BlockSpec((l, chan), lambda t: (0, t))
    tab = pl.BlockSpec((64, states), lambda t: (0, t))
    vec = pl.BlockSpec((1, chan), lambda t: (0, t))
    pack_shape = jax.ShapeDtypeStruct((nt, states, chan), F32)
    return pl.pallas_call(
        body,
        out_shape=(jax.ShapeDtypeStruct((l, sw), BF16), jax.ShapeDtypeStruct((8, nt * states), F32),
                   pack_shape, pack_shape, pack_shape, pack_shape, jax.ShapeDtypeStruct((1, sw), F32)),
        grid=(nt,),
        in_specs=[pl.BlockSpec((l, chan), lambda t: (0, ub + t)), seq, pin, pin, pout, pout, pin, pin, vec, tab, tab],
        out_specs=(seq, pl.BlockSpec((8, states), lambda t: (0, t)), pout, pout, pout, pout, vec),
        scratch_shapes=[pltpu.VMEM((l, states), F32)] * 4,
        name=name,
        compiler_params=_params(("parallel",)),
    )(proj, dy, packs["br"], packs["bi"], packs["brt"], packs["bit"], packs["crt"], packs["cit"], dvec, tab_f, tab_b)


GELU_K = math.sqrt(2.0 / math.pi)
GELU_C = 0.044715


def _gelu(y, name):
    l, w = y.shape
    tl = _tile(l, 512, 8)

    def body(y_ref, o_ref):
        v = y_ref[...]
        o_ref[...] = (0.5 * v * (1.0 + jnp.tanh(GELU_K * (v + GELU_C * v * v * v)))).astype(BF16)

    return pl.pallas_call(body, out_shape=jax.ShapeDtypeStruct((l, w), BF16), grid=(l // tl,),
                          in_specs=[_row_spec(tl, w)], out_specs=_row_spec(tl, w), name=name,
                          compiler_params=_params(("parallel",)))(y)


def _gelu_bwd(y, dg, name):
    l, w = y.shape
    tl = _tile(l, 512, 8)

    def body(y_ref, dg_ref, o_ref):
        v = y_ref[...]
        t = jnp.tanh(GELU_K * (v + GELU_C * v * v * v))
        grad = 0.5 * (1.0 + t) + 0.5 * v * (1.0 - t * t) * GELU_K * (1.0 + 3.0 * GELU_C * v * v)
        o_ref[...] = (dg_ref[...].astype(F32) * grad).astype(BF16)

    return pl.pallas_call(body, out_shape=jax.ShapeDtypeStruct((l, w), BF16), grid=(l // tl,),
                          in_specs=[_row_spec(tl, w), _row_spec(tl, w)], out_specs=_row_spec(tl, w), name=name,
                          compiler_params=_params(("parallel",)))(y, dg)


MIX_COLS = 256


def _mix(proj, ga_off, gs_off, attn_out, glu, d, name):
    l = proj.shape[0]
    tl = _tile(l, 1024, 8)
    cb = MIX_COLS
    nj = d // cb
    assert d % cb == 0 and ga_off % cb == 0 and gs_off % cb == 0

    def body(ga_ref, gs_ref, a_ref, ua_ref, ub_ref, o_ref):
        ssm = ua_ref[...].astype(F32) * _sigmoid(ub_ref[...].astype(F32))
        o_ref[...] = (_sigmoid(ga_ref[...].astype(F32)) * a_ref[...].astype(F32)
                      + _sigmoid(gs_ref[...].astype(F32)) * ssm).astype(BF16)

    def spec(off):
        return pl.BlockSpec((tl, cb), lambda i, j, off=off: (i, off // cb + j))

    return pl.pallas_call(
        body, out_shape=jax.ShapeDtypeStruct((l, d), BF16), grid=(l // tl, nj),
        in_specs=[spec(ga_off), spec(gs_off), spec(0), spec(0), spec(d)], out_specs=spec(0), name=name,
        compiler_params=_params(("parallel", "parallel")),
    )(proj, proj, attn_out, glu, glu)


def _mix_bwd(proj, ga_off, gs_off, attn_out, glu, dmixed, d, name):
    l = proj.shape[0]
    tl = _tile(l, 1024, 8)
    cb = MIX_COLS
    nj = d // cb

    def body(ga_ref, gs_ref, a_ref, ua_ref, ub_ref, dm_ref, dga_ref, dgs_ref, da_ref, dglu_ref):
        s = pl.program_id(2)
        dm = dm_ref[...].astype(F32)
        sa = _sigmoid(ga_ref[...].astype(F32))
        ss = _sigmoid(gs_ref[...].astype(F32))
        sb = _sigmoid(ub_ref[...].astype(F32))
        ua = ua_ref[...].astype(F32)
        dssm = dm * ss

        @pl.when(s == 0)
        def _():
            dga_ref[...] = (dm * a_ref[...].astype(F32) * sa * (1.0 - sa)).astype(BF16)
            da_ref[...] = (dm * sa).astype(BF16)
            dgs_ref[...] = (dm * (ua * sb) * ss * (1.0 - ss)).astype(BF16)
            dglu_ref[...] = (dssm * sb).astype(BF16)

        @pl.when(s == 1)
        def _():
            dglu_ref[...] = (dssm * ua * sb * (1.0 - sb)).astype(BF16)

    def spec(off):
        return pl.BlockSpec((tl, cb), lambda i, j, s, off=off: (i, off // cb + j))

    out = jax.ShapeDtypeStruct((l, d), BF16)
    return pl.pallas_call(
        body, out_shape=(out, out, out, jax.ShapeDtypeStruct((l, 2 * d), BF16)), grid=(l // tl, nj, 2),
        in_specs=[spec(ga_off), spec(gs_off), spec(0), spec(0), spec(d), spec(0)],
        out_specs=(spec(0), spec(0), spec(0), pl.BlockSpec((tl, cb), lambda i, j, s: (i, j + s * nj))), name=name,
        compiler_params=_params(("parallel", "parallel", "arbitrary")),
    )(proj, proj, attn_out, glu, glu, dmixed)


CONV_COLS = 512
HALO = 16


def _shift_rows(v, k, head):
    row = lax.broadcasted_iota(jnp.int32, v.shape, 0)
    out = pltpu.roll(v, k, 0)
    for r in range(k):
        out = jnp.where(row == r, head[HALO - k + r:HALO - k + r + 1, :], out)
    return out


def _shift_rows_up(v, k, tail):
    n = v.shape[0]
    row = lax.broadcasted_iota(jnp.int32, v.shape, 0)
    out = pltpu.roll(v, n - k, 0)
    for r in range(k):
        out = jnp.where(row == n - k + r, tail[r:r + 1, :], out)
    return out


def _conv_gate(g, head, w_ref, b_ref):
    return w_ref[0:1, :] * _shift_rows(g, 2, head) + w_ref[1:2, :] * _shift_rows(g, 1, head) + w_ref[2:3, :] * g + b_ref[...]


def _conv_act(up, conv_w, conv_b, ff, name):
    l = up.shape[0]
    tl = _tile(l, 512, HALO)
    cw = _tile(ff, CONV_COLS)
    nj = ff // cw
    hb = tl // HALO

    def body(g_ref, gp_ref, v_ref, w_ref, b_ref, o_ref):
        i = pl.program_id(0)
        head = gp_ref[...].astype(F32) * jnp.where(i > 0, 1.0, 0.0)
        gc = _conv_gate(g_ref[...].astype(F32), head, w_ref, b_ref)
        o_ref[...] = (gc * _sigmoid(gc) * v_ref[...].astype(F32)).astype(BF16)

    return pl.pallas_call(
        body, out_shape=jax.ShapeDtypeStruct((l, ff), BF16), grid=(l // tl, nj),
        in_specs=[pl.BlockSpec((tl, cw), lambda i, j: (i, j)),
                  pl.BlockSpec((HALO, cw), lambda i, j: (jnp.maximum(i * hb - 1, 0), j)),
                  pl.BlockSpec((tl, cw), lambda i, j: (i, nj + j)),
                  pl.BlockSpec((3, cw), lambda i, j: (0, j)), pl.BlockSpec((1, cw), lambda i, j: (0, j))],
        out_specs=pl.BlockSpec((tl, cw), lambda i, j: (i, j)), name=name,
        compiler_params=_params(("parallel", "parallel")),
    )(up, up, up, conv_w, conv_b)


def _conv_act_bwd(up, da, conv_w, conv_b, ff, name):
    l = up.shape[0]
    tl = _tile(l, 512, HALO)
    cw = _tile(ff, CONV_COLS)
    nj = ff // cw
    hb = tl // HALO
    ni = l // tl

    def body(g_ref, gp_ref, gn_ref, v_ref, vn_ref, da_ref, dan_ref, w_ref, b_ref, dup_ref, dw_ref, db_ref):
        i = pl.program_id(0)
        s = pl.program_id(2)
        g = g_ref[...].astype(F32)
        head = gp_ref[...].astype(F32) * jnp.where(i > 0, 1.0, 0.0)
        g1 = _shift_rows(g, 1, head)
        g2 = _shift_rows(g, 2, head)
        gc = w_ref[0:1, :] * g2 + w_ref[1:2, :] * g1 + w_ref[2:3, :] * g + b_ref[...]
        sg = _sigmoid(gc)
        dav = da_ref[...].astype(F32)

        @pl.when(s == 0)
        def _():
            dgc = dav * v_ref[...].astype(F32) * (sg * (1.0 + gc * (1.0 - sg)))
            gn = gn_ref[...].astype(F32)
            gcn = _conv_gate(gn, g[tl - HALO:, :], w_ref, b_ref)
            sgn = _sigmoid(gcn)
            dgcn = dan_ref[...].astype(F32) * vn_ref[...].astype(F32) * (sgn * (1.0 + gcn * (1.0 - sgn)))
            dgcn = dgcn * jnp.where(i < ni - 1, 1.0, 0.0)
            dgate = w_ref[2:3, :] * dgc + w_ref[1:2, :] * _shift_rows_up(dgc, 1, dgcn) + w_ref[0:1, :] * _shift_rows_up(dgc, 2, dgcn)
            dup_ref[...] = dgate.astype(BF16)
            zero = jnp.zeros((1, cw), F32)
            dw_ref[...] = jnp.concatenate(
                [jnp.sum(dgc * g2, axis=0, keepdims=True), jnp.sum(dgc * g1, axis=0, keepdims=True),
                 jnp.sum(dgc * g, axis=0, keepdims=True)] + [zero] * 5, axis=0)
            db_ref[...] = jnp.concatenate([jnp.sum(dgc, axis=0, keepdims=True)] + [zero] * 7, axis=0)

        @pl.when(s == 1)
        def _():
            dup_ref[...] = (dav * (gc * sg)).astype(BF16)

    def cur(off):
        return pl.BlockSpec((tl, cw), lambda i, j, s, off=off: (i, off + j))

    def prev(off):
        return pl.BlockSpec((HALO, cw), lambda i, j, s, off=off: (jnp.maximum(i * hb - 1, 0), off + j))

    def nxt(off):
        return pl.BlockSpec((HALO, cw), lambda i, j, s, off=off: (jnp.minimum((i + 1) * hb, l // HALO - 1), off + j))

    part = jax.ShapeDtypeStruct((ni * 8, ff), F32)
    part_spec = pl.BlockSpec((8, cw), lambda i, j, s: (i, j))
    return pl.pallas_call(
        body, out_shape=(jax.ShapeDtypeStruct((l, 2 * ff), BF16), part, part), grid=(ni, nj, 2),
        in_specs=[cur(0), prev(0), nxt(0), cur(nj), nxt(nj), cur(0), nxt(0),
                  pl.BlockSpec((3, cw), lambda i, j, s: (0, j)), pl.BlockSpec((1, cw), lambda i, j, s: (0, j))],
        out_specs=(pl.BlockSpec((tl, cw), lambda i, j, s: (i, j + s * nj)), part_spec, part_spec), name=name,
        compiler_params=_params(("parallel", "parallel", "arbitrary")),
    )(up, up, up, up, up, da, da, conv_w, conv_b)


def _sum_rows8(parts, name):
    n8, w = parts.shape
    n = n8 // 8
    cw = _tile(w, 2048)

    def body(p_ref, o_ref):
        acc = p_ref[0:8, :]
        for k in range(1, n):
            acc = acc + p_ref[8 * k:8 * k + 8, :]
        o_ref[...] = acc

    return pl.pallas_call(body, out_shape=jax.ShapeDtypeStruct((8, w), F32), grid=(w // cw,),
                          in_specs=[pl.BlockSpec((n8, cw), lambda j: (0, j))], out_specs=pl.BlockSpec((8, cw), lambda j: (0, j)),
                          name=name, compiler_params=_params(("parallel",)))(parts)


def _ada_fwd(c_all, w_shard, b_shard, name):
    nb, d = c_all.shape
    n = w_shard.shape[1]
    tn = _tile(n, 512)

    def body(c_ref, w_ref, b_ref, o_ref):
        cv = c_ref[...]
        cond = (cv * _sigmoid(cv)).astype(BF16)
        o_ref[...] = jnp.dot(cond, w_ref[...].astype(BF16), preferred_element_type=F32) + b_ref[...]

    return pl.pallas_call(
        body, out_shape=jax.ShapeDtypeStruct((nb, n), F32), grid=(n // tn,),
        in_specs=[pl.BlockSpec((nb, d), lambda j: (0, 0)), pl.BlockSpec((d, tn), lambda j: (0, j)),
                  pl.BlockSpec((1, tn), lambda j: (0, j))],
        out_specs=pl.BlockSpec((nb, tn), lambda j: (0, j)), name=name, compiler_params=_params(("parallel",)),
    )(c_all, w_shard, b_shard)


def _adam_update(w, g, m, v):
    m2 = ADAM_B1 * m + (1.0 - ADAM_B1) * g
    v2 = ADAM_B2 * v + (1.0 - ADAM_B2) * (g * g)
    m_hat = m2 / (1.0 - ADAM_B1 ** ADAM_STEP)
    v_hat = v2 / (1.0 - ADAM_B2 ** ADAM_STEP)
    return -ADAM_LR * (m_hat / (jnp.sqrt(v_hat) + ADAM_EPS) + ADAM_WD * w), m2, v2


def _ada_bwd_adam(c_all_t, dmod_shard, w, m, v, name):
    d, nb = c_all_t.shape
    n = w.shape[1]
    tr, tn = _tile(d, 512, 8), _tile(n, 512)

    def body(c_ref, dm_ref, w_ref, m_ref, v_ref, g_ref, dl_ref, m2_ref, v2_ref):
        cv = c_ref[...]
        cond = cv * _sigmoid(cv)
        g = cond[:, 0:1] * dm_ref[0:1, :]
        for b in range(1, nb):
            g = g + cond[:, b:b + 1] * dm_ref[b:b + 1, :]
        g_ref[...] = g
        dl_ref[...], m2_ref[...], v2_ref[...] = _adam_update(w_ref[...], g, m_ref[...], v_ref[...])

    blk = pl.BlockSpec((tr, tn), lambda i, j: (i, j))
    out = jax.ShapeDtypeStruct((d, n), F32)
    return pl.pallas_call(
        body, out_shape=(out, out, out, out), grid=(d // tr, n // tn),
        in_specs=[pl.BlockSpec((tr, nb), lambda i, j: (i, 0)), pl.BlockSpec((nb, tn), lambda i, j: (0, j)), blk, blk, blk],
        out_specs=(blk, blk, blk, blk), name=name, compiler_params=_params(("parallel", "parallel")),
    )(c_all_t, dmod_shard, w, m, v)


def _adam(w, g, m, v, name):
    r, c = w.shape
    tr = _tile(r, 256, 8)

    def body(w_ref, g_ref, m_ref, v_ref, dl_ref, m2_ref, v2_ref):
        dl_ref[...], m2_ref[...], v2_ref[...] = _adam_update(w_ref[...], g_ref[...], m_ref[...], v_ref[...])

    blk = pl.BlockSpec((tr, c), lambda i: (i, 0))
    out = jax.ShapeDtypeStruct((r, c), F32)
    return pl.pallas_call(body, out_shape=(out, out, out), grid=(r // tr,), in_specs=[blk] * 4, out_specs=(blk,) * 3,
                          name=name, compiler_params=_params(("parallel",)))(w, g, m, v)


def _sum_devices(gathered, name):
    nd, r, c = gathered.shape
    tr = _tile(r, 64, 8)

    def body(g_ref, o_ref):
        acc = g_ref[0]
        for k in range(1, nd):
            acc = acc + g_ref[k]
        o_ref[...] = acc

    return pl.pallas_call(body, out_shape=jax.ShapeDtypeStruct((r, c), F32), grid=(r // tr,),
                          in_specs=[pl.BlockSpec((nd, tr, c), lambda i: (0, i, 0))], out_specs=pl.BlockSpec((tr, c), lambda i: (i, 0)),
                          name=name, compiler_params=_params(("parallel",)))(gathered)


def _place():
    x, y, c = lax.axis_index("x"), lax.axis_index("y"), lax.axis_index("c")
    chips = [(1 - x, y), (x, 1 - y), (1 - x, 1 - y)]
    return x, y, c, chips


def _all_gather8(block, name):
    m_per, n = block.shape

    def body(x_ref, out_ref, send_sems, recv_sems, local_sem):
        x, y, c, chips = _place()
        me, sibling = (x, y, c), (x, y, 1 - c)

        def rows(px, py, pc):
            return out_ref.at[pl.ds((4 * px + 2 * py + pc) * m_per, m_per), :]

        def copy(k, blk, to, src=None):
            return pltpu.make_async_remote_copy(
                src_ref=rows(*blk) if src is None else src, dst_ref=rows(*blk), send_sem=send_sems.at[k],
                recv_sem=recv_sems.at[k], device_id=to, device_id_type=MESH)

        mine = pltpu.make_async_copy(x_ref, rows(*me), local_sem)
        mine.start()
        first = [copy(0, me, sibling, src=x_ref)]
        first += [copy(1 + j, me, (*chip, c), src=x_ref) for j, chip in enumerate(chips)]
        for cp in first:
            cp.start()
        passed = [copy(4 + j, (*chip, c), sibling) for j, chip in enumerate(chips)]
        for j, chip in enumerate(chips):
            copy(1 + j, (*chip, c), me).wait_recv()
            passed[j].start()
        copy(0, sibling, me).wait_recv()
        for j, chip in enumerate(chips):
            copy(4 + j, (*chip, 1 - c), me).wait_recv()
        for cp in first + passed:
            cp.wait_send()
        mine.wait()

    return pl.pallas_call(
        body,
        out_shape=jax.ShapeDtypeStruct((N_DEV * m_per, n), block.dtype),
        in_specs=[pl.BlockSpec(memory_space=pltpu.VMEM)],
        out_specs=pl.BlockSpec(memory_space=pltpu.VMEM),
        scratch_shapes=[pltpu.SemaphoreType.DMA((7,)), pltpu.SemaphoreType.DMA((7,)), pltpu.SemaphoreType.DMA],
        name=name,
        compiler_params=pltpu.CompilerParams(vmem_limit_bytes=VMEM_LIMIT_BYTES),
    )(block)


ANY = pl.BlockSpec(memory_space=pl.ANY)


def _place_shard(shard, name, after=()):
    r, k = shard.shape
    tb = _tile(r, 512, 16)
    nb = r // tb
    chip = (2 * lax.axis_index("x") + lax.axis_index("y")).astype(jnp.int32).reshape(1)

    def body(j_ref, s_ref, *rest):
        rest[-1][...] = s_ref[...].astype(BF16)

    return pl.pallas_call(
        body, out_shape=jax.ShapeDtypeStruct((N_CHIPS * r, k), BF16),
        grid_spec=pltpu.PrefetchScalarGridSpec(
            num_scalar_prefetch=1, grid=(nb,),
            in_specs=[pl.BlockSpec((tb, k), lambda i, j_ref: (i, 0))] + [ANY] * len(after),
            out_specs=pl.BlockSpec((tb, k), lambda i, j_ref: (j_ref[0] * nb + i, 0))),
        name=name, compiler_params=_params(("parallel",)),
    )(chip, shard, *after)


HBM_SPEC = pl.BlockSpec(memory_space=pltpu.HBM)
SEM_SPEC = pl.BlockSpec(memory_space=pltpu.SEMAPHORE)
TOKEN_SPEC = pl.BlockSpec(memory_space=pltpu.VMEM)
SPLIT_COPY = pltpu.CompilerParams(has_side_effects=pltpu.SideEffectType.DATAFLOW_SIDE_EFFECTING)


def _in_hbm(arrays):
    return [pltpu.with_memory_space_constraint(a, pltpu.HBM) for a in arrays]


def _hbm_like(arrays):
    return tuple(pltpu.HBM(a.shape, a.dtype) for a in arrays)


def _token_shape():
    return jax.ShapeDtypeStruct((8, LANES), F32)


def _gathered_rows(buf, px, py, half):
    r = buf.shape[0] // N_CHIPS
    return buf.at[pl.ds(pl.multiple_of((2 * px + py) * r + half * (r // 2), 16), r // 2), :]


def _gather_start(groups, name):
    sizes = [len(g) for g in groups]
    flat = [b for g in groups for b in g]
    nb, ng = len(flat), len(groups)

    def body(*refs):
        bufs = refs[:nb]
        sems = refs[nb:nb + 2 * ng]
        token = refs[-1]
        x, y, c, chips = _place()
        pos = 0
        for gi, nw in enumerate(sizes):
            for k, chip in enumerate(chips):
                for w in range(nw):
                    mine = _gathered_rows(bufs[pos + w], x, y, c)
                    pltpu.make_async_remote_copy(src_ref=mine, dst_ref=mine, send_sem=sems[2 * gi].at[k * nw + w], recv_sem=sems[2 * gi + 1].at[k * nw + w],
                                                 device_id=(*chip, c), device_id_type=MESH).start()
            pos += nw
        token[...] = jnp.zeros_like(token)

    sem_shapes = tuple(pltpu.SemaphoreType.DMA((3 * n,)) for n in sizes for _ in range(2))
    outs = pl.pallas_call(
        body, name=name, out_shape=sem_shapes + _hbm_like(flat) + (_token_shape(),),
        in_specs=[HBM_SPEC] * nb, out_specs=(SEM_SPEC,) * (2 * ng) + (HBM_SPEC,) * nb + (TOKEN_SPEC,),
        input_output_aliases={i: 2 * ng + i for i in range(nb)}, compiler_params=SPLIT_COPY,
    )(*_in_hbm(flat))
    res, pos = [], 2 * ng
    for gi, n in enumerate(sizes):
        res.append((outs[2 * gi], outs[2 * gi + 1], list(outs[pos:pos + n])))
        pos += n
    return res, outs[-1]


def _gather_forward(bufs, ici_send, ici_recv, after, name):
    nw = len(bufs)

    def body(*refs):
        b = refs[:nw]
        isend, irecv = refs[nw], refs[nw + 1]
        dsend, drecv = refs[nw + 3], refs[nw + 4]
        x, y, c, chips = _place()
        for k, chip in enumerate(chips):
            for w in range(nw):
                landed = _gathered_rows(b[w], *chip, c)
                pltpu.make_async_remote_copy(src_ref=landed, dst_ref=landed, send_sem=isend.at[k * nw + w], recv_sem=irecv.at[k * nw + w],
                                             device_id=(*chip, c), device_id_type=MESH).wait_recv()
                pltpu.make_async_remote_copy(src_ref=landed, dst_ref=landed, send_sem=dsend.at[k * nw + w], recv_sem=drecv.at[k * nw + w],
                                             device_id=(x, y, 1 - c), device_id_type=MESH).start()
        for k, chip in enumerate(chips):
            for w in range(nw):
                mine = _gathered_rows(b[w], x, y, c)
                pltpu.make_async_remote_copy(src_ref=mine, dst_ref=mine, send_sem=isend.at[k * nw + w], recv_sem=irecv.at[k * nw + w],
                                             device_id=(*chip, c), device_id_type=MESH).wait_send()

    sem = pltpu.SemaphoreType.DMA((3 * nw,))
    outs = pl.pallas_call(
        body, name=name, out_shape=(sem, sem) + _hbm_like(bufs),
        in_specs=[HBM_SPEC] * nw + [SEM_SPEC, SEM_SPEC, ANY], out_specs=(SEM_SPEC, SEM_SPEC) + (HBM_SPEC,) * nw,
        input_output_aliases={i: 2 + i for i in range(nw)}, compiler_params=SPLIT_COPY,
    )(*bufs, ici_send, ici_recv, after)
    return outs[0], outs[1], list(outs[2:])


def _gather_finish(bufs, d2d_send, d2d_recv, name):
    nw = len(bufs)

    def body(*refs):
        b = refs[:nw]
        dsend, drecv = refs[nw], refs[nw + 1]
        x, y, c, chips = _place()
        for k, chip in enumerate(chips):
            for w in range(nw):
                theirs = _gathered_rows(b[w], *chip, 1 - c)
                pltpu.make_async_remote_copy(src_ref=theirs, dst_ref=theirs, send_sem=dsend.at[k * nw + w], recv_sem=drecv.at[k * nw + w],
                                             device_id=(x, y, 1 - c), device_id_type=MESH).wait_recv()
                passed = _gathered_rows(b[w], *chip, c)
                pltpu.make_async_remote_copy(src_ref=passed, dst_ref=passed, send_sem=dsend.at[k * nw + w], recv_sem=drecv.at[k * nw + w],
                                             device_id=(x, y, 1 - c), device_id_type=MESH).wait_send()

    outs = pl.pallas_call(
        body, name=name, out_shape=_hbm_like(bufs), in_specs=[HBM_SPEC] * nw + [SEM_SPEC, SEM_SPEC], out_specs=(HBM_SPEC,) * nw,
        input_output_aliases={i: i for i in range(nw)}, compiler_params=SPLIT_COPY,
    )(*bufs, d2d_send, d2d_recv)
    return list(outs)


def _scatter_start(partials, name):
    nw = len(partials)
    landing = [lax.empty((3,) + p.shape[1:], p.dtype) for p in partials]

    def body(*refs):
        src, land = refs[:nw], refs[nw:2 * nw]
        send_sems, recv_sems = refs[2 * nw], refs[2 * nw + 1]
        token = refs[-1]
        x, y, c, chips = _place()
        for k, chip in enumerate(chips):
            for w in range(nw):
                pltpu.make_async_remote_copy(src_ref=src[w].at[2 * chip[0] + chip[1]], dst_ref=land[w].at[k], send_sem=send_sems.at[k * nw + w],
                                             recv_sem=recv_sems.at[k * nw + w], device_id=(*chip, c), device_id_type=MESH).start()
        token[...] = jnp.zeros_like(token)

    sem = pltpu.SemaphoreType.DMA((3 * nw,))
    outs = pl.pallas_call(
        body, name=name, out_shape=(sem, sem) + _hbm_like(partials) + _hbm_like(landing) + (_token_shape(),),
        in_specs=[HBM_SPEC] * (2 * nw), out_specs=(SEM_SPEC, SEM_SPEC) + (HBM_SPEC,) * (2 * nw) + (TOKEN_SPEC,),
        input_output_aliases={i: 2 + i for i in range(2 * nw)}, compiler_params=SPLIT_COPY,
    )(*_in_hbm(partials), *_in_hbm(landing))
    return (outs[0], outs[1], list(outs[2:2 + nw]), list(outs[2 + nw:2 + 2 * nw])), outs[-1]


def _scatter_wait(started, after, name):
    send_sems, recv_sems, partials, landing = started
    nw = len(partials)

    def body(*refs):
        src, land = refs[:nw], refs[nw:2 * nw]
        ssem, rsem = refs[2 * nw], refs[2 * nw + 1]
        x, y, c, chips = _place()
        for k, chip in enumerate(chips):
            for w in range(nw):
                cp = pltpu.make_async_remote_copy(src_ref=src[w].at[2 * chip[0] + chip[1]], dst_ref=land[w].at[k], send_sem=ssem.at[k * nw + w],
                                                  recv_sem=rsem.at[k * nw + w], device_id=(*chip, c), device_id_type=MESH)
                cp.wait_send()
                cp.wait_recv()

    outs = pl.pallas_call(
        body, name=name, out_shape=_hbm_like(partials) + _hbm_like(landing),
        in_specs=[HBM_SPEC] * (2 * nw) + [SEM_SPEC, SEM_SPEC] + [ANY] * len(after), out_specs=(HBM_SPEC,) * (2 * nw),
        input_output_aliases={i: i for i in range(2 * nw)}, compiler_params=SPLIT_COPY,
    )(*partials, *landing, send_sems, recv_sems, *after)
    return list(outs[:nw]), list(outs[nw:])


def _swap_halves(grads, name, after=()):
    nw, na = len(grads), len(after)

    def body(*refs):
        ins, outs = refs[:nw], refs[nw + na:2 * nw + na]
        send_sems, recv_sems = refs[2 * nw + na:]
        x, y, c, _ = _place()
        copies = []
        for w in range(nw):
            r = grads[w].shape[0] // N_CHIPS
            h = r // 2
            for j in range(N_CHIPS):
                copies.append(pltpu.make_async_remote_copy(
                    src_ref=ins[w].at[pl.ds(pl.multiple_of(j * r + (1 - c) * h, 16), h), :], dst_ref=outs[w].at[pl.ds(j * h, h), :],
                    send_sem=send_sems.at[w, j], recv_sem=recv_sems.at[w, j], device_id=(x, y, 1 - c), device_id_type=MESH))
                copies[-1].start()
        for cp in copies:
            cp.wait()

    sem = pltpu.SemaphoreType.DMA((nw, N_CHIPS))
    return pl.pallas_call(
        body, out_shape=tuple(jax.ShapeDtypeStruct((g.shape[0] // 2, g.shape[1]), g.dtype) for g in grads),
        in_specs=[ANY] * (nw + na), out_specs=(ANY,) * nw, scratch_shapes=[sem, sem], name=name,
    )(*grads, *after)


def _add_halves(grad, other, name):
    k = grad.shape[1]
    h = other.shape[0] // N_CHIPS
    tb = _tile(h, 512, 16)
    g4 = grad.reshape(N_CHIPS, 2, h, k)
    o3 = other.reshape(N_CHIPS, h, k)
    core = lax.axis_index("c").astype(jnp.int32).reshape(1)

    def body(c_ref, g_ref, o_ref, p_ref):
        p_ref[...] = (g_ref[...].astype(F32) + o_ref[...].astype(F32)).astype(BF16)

    return pl.pallas_call(
        body, out_shape=jax.ShapeDtypeStruct((N_CHIPS, h, k), BF16),
        grid_spec=pltpu.PrefetchScalarGridSpec(
            num_scalar_prefetch=1, grid=(N_CHIPS, h // tb),
            in_specs=[pl.BlockSpec((None, None, tb, k), lambda j, i, c_ref: (j, c_ref[0], i, 0)),
                      pl.BlockSpec((None, tb, k), lambda j, i, c_ref: (j, i, 0))],
            out_specs=pl.BlockSpec((None, tb, k), lambda j, i, c_ref: (j, i, 0))),
        name=name, compiler_params=_params(("parallel", "parallel")),
    )(core, g4, o3)


def _add_partials(partial, others, name):
    _, h, k = partial.shape
    tb = _tile(h, 512, 16)
    nb = h // tb
    place = jnp.stack([2 * lax.axis_index("x") + lax.axis_index("y"), lax.axis_index("c")]).astype(jnp.int32)

    def body(s_ref, p_ref, o0_ref, o1_ref, o2_ref, f_ref):
        f_ref[...] = ((p_ref[...].astype(F32) + o0_ref[...].astype(F32)) + o1_ref[...].astype(F32)) + o2_ref[...].astype(F32)

    def other(s):
        return pl.BlockSpec((None, tb, k), lambda i, s_ref, s=s: (s, i, 0))

    return pl.pallas_call(
        body, out_shape=jax.ShapeDtypeStruct((2 * h, k), F32),
        grid_spec=pltpu.PrefetchScalarGridSpec(
            num_scalar_prefetch=1, grid=(nb,),
            in_specs=[pl.BlockSpec((None, tb, k), lambda i, s_ref: (s_ref[0], i, 0)), other(0), other(1), other(2)],
            out_specs=pl.BlockSpec((tb, k), lambda i, s_ref: (s_ref[1] * nb + i, 0))),
        name=name, compiler_params=_params(("parallel",)),
    )(place, partial, others, others, others)


def _share_halves(fulls, name):
    nw = len(fulls)

    def body(*refs):
        ins, outs = refs[:nw], refs[nw:2 * nw]
        send_sems, recv_sems = refs[2 * nw:]
        x, y, c, _ = _place()
        copies = []
        for w in range(nw):
            h = fulls[w].shape[0] // 2
            start = pl.multiple_of(c * h, 8)
            copies.append(pltpu.make_async_remote_copy(
                src_ref=ins[w].at[pl.ds(start, h), :], dst_ref=outs[w].at[pl.ds(start, h), :], send_sem=send_sems.at[w],
                recv_sem=recv_sems.at[w], device_id=(x, y, 1 - c), device_id_type=MESH))
            copies[-1].start()
        for cp in copies:
            cp.wait()

    sem = pltpu.SemaphoreType.DMA((nw,))
    return pl.pallas_call(
        body, out_shape=tuple(jax.ShapeDtypeStruct(f.shape, f.dtype) for f in fulls),
        in_specs=[ANY] * nw, out_specs=(ANY,) * nw, scratch_shapes=[sem, sem], name=name,
        input_output_aliases={w: w for w in range(nw)},
    )(*fulls)


def _forward_then_finish(started_group, after, tag):
    ici_send, ici_recv, bufs = started_group
    d2d_send, d2d_recv, bufs = _gather_forward(bufs, ici_send, ici_recv, after, f"gather_forward_{tag}")
    return _gather_finish(bufs, d2d_send, d2d_recv, f"gather_finish_{tag}")


def _reduce_start(grads, tag, after=()):
    from_sibling = _swap_halves(grads, f"swap_halves_{tag}", after)
    chip_sums = [_add_halves(g, o, f"add_halves_{tag}_{i}") for i, (g, o) in enumerate(zip(grads, from_sibling))]
    return _scatter_start(chip_sums, f"scatter_start_{tag}")


def _reduce_finish(started, after, tag):
    chip_sums, from_chips = _scatter_wait(started, after, f"scatter_wait_{tag}")
    fulls = [_add_partials(p, o, f"add_partials_{tag}_{i}") for i, (p, o) in enumerate(zip(chip_sums, from_chips))]
    return _share_halves(fulls, f"share_halves_{tag}")


def _flatten_pad(parts, cols=SMALL_COLS):
    flat = jnp.concatenate([p.reshape(-1) for p in parts])
    rows = -(-flat.shape[0] // (8 * cols)) * 8
    return jnp.pad(flat, (0, rows * cols - flat.shape[0])).reshape(rows, cols)


def _split_flat(buf, shapes):
    flat = buf.reshape(-1)
    out, off = [], 0
    for s in shapes:
        n = math.prod(s)
        out.append(flat[off:off + n].reshape(s))
        off += n
    return out


def _ssm_setup(ssm_a_re, ssm_a_im, ssm_log_dt, ssm_b_re, ssm_b_im, ssm_c_re, ssm_c_im):
    lam_r, lam_i, bbar_r, bbar_i = _ssm_discretize(ssm_a_re, ssm_a_im, ssm_log_dt, ssm_b_re, ssm_b_im)
    tab_f, tab_b = _scan_tables(lam_r, lam_i)
    pk = {"br": _pack_in(bbar_r), "bi": _pack_in(bbar_i), "cr": _pack_out(ssm_c_re), "ci": _pack_out(ssm_c_im)}
    packs = {k: v.astype(BF16) for k, v in pk.items()}
    packs.update({"brt": jnp.swapaxes(packs["br"], 1, 2), "bit": jnp.swapaxes(packs["bi"], 1, 2),
                  "crt": jnp.swapaxes(packs["cr"], 1, 2), "cit": jnp.swapaxes(packs["ci"], 1, 2)})
    return packs, tab_f, tab_b


def _local_step(xs, target, mod, w_in_t, later_weights, ffn_grads_ready, norm_mix_g, attn_sinks, ssm, norm_ffn_g, conv_w_full,
                ffn_conv_b, final_g, aw, sw, ff):
    l, d = xs.shape
    u_off = aw + 2 * KV_WIDTH
    ga_off = u_off + sw
    gs_off = ga_off + d
    packs, tab_f, tab_b = _ssm_setup(*ssm[:7])
    dvec = ssm[7].reshape(1, sw)

    h1 = _norm_mod(xs, norm_mix_g, mod, 1, 0, "norm_mod1")
    proj = _matmul(h1, w_in_t, "nt", "mm_in")
    attn = _attn_fwd(proj, attn_sinks, aw, "attn_fwd")
    ys = _s5_fwd(proj, u_off, packs, dvec, tab_f, sw, "s5_fwd")
    gy = _gelu(ys, "gelu")
    w_ap_t, w_glu_t, w_out_f, w_up_t, w_down_f = later_weights(gy)
    attn_out = _matmul(attn, w_ap_t, "nt", "mm_attn_proj")
    glu = _matmul(gy, w_glu_t, "nt", "mm_glu")
    mixed = _mix(proj, ga_off, gs_off, attn_out, glu, d, "mix")
    mo = _matmul(mixed, w_out_f, "nn", "mm_out", out_dtype=F32)
    x2, h2 = _resid_norm_mod(xs, mo, norm_ffn_g, mod, 2, 4, 3, "resid_norm_mod2")
    up = _matmul(h2, w_up_t, "nt", "mm_up")
    act = _conv_act(up, conv_w_full, ffn_conv_b, ff, "conv_act")
    fo = _matmul(act, w_down_f, "nn", "mm_down", out_dtype=F32)
    loss_part, d_final_g, d_gate2, dx3, dfo = _final_loss(x2, fo, mod, 5, final_g.reshape(1, d), target, "final_loss")

    dact = _matmul(dfo, w_down_f, "nt", "mm_down_dx")
    g_down = _matmul(act, dfo, "tn", "mm_down_dw")
    dup, dcw_parts, dcb_parts = _conv_act_bwd(up, dact, conv_w_full, ffn_conv_b, ff, "conv_act_bwd")
    d_conv_w = _sum_rows8(dcw_parts, "sum_conv_w")[:3]
    d_conv_b = _sum_rows8(dcb_parts, "sum_conv_b")[:1]
    dh2 = _matmul(dup, w_up_t, "nn", "mm_up_dx")
    g_up = _matmul(dup, h2, "tn", "mm_up_dw")
    mod = ffn_grads_ready(g_up, g_down, mod)
    dx2, d_shift2, d_scale2, d_gain2, dmo, d_gate1 = _norm_mod_bwd(dh2, x2, dx3, norm_ffn_g, mod, 4, "norm_mod2_bwd", branch=mo, gate_col=2)
    dmixed = _matmul(dmo, w_out_f, "nt", "mm_out_dx")
    g_out = _matmul(mixed, dmo, "tn", "mm_out_dw")
    dga, dgs, dattn_out, dglu = _mix_bwd(proj, ga_off, gs_off, attn_out, glu, dmixed, d, "mix_bwd")
    dgy = _matmul(dglu, w_glu_t, "nn", "mm_glu_dx")
    g_glu = _matmul(dglu, gy, "tn", "mm_glu_dw")
    dys = _gelu_bwd(ys, dgy, "gelu_bwd")
    du, dlam, dbr_p, dbi_p, dcr_p, dci_p, d_dvec = _s5_bwd(proj, u_off, dys, packs, dvec, tab_f, tab_b, sw, "s5_bwd")
    dattn = _matmul(dattn_out, w_ap_t, "nn", "mm_attn_proj_dx")
    g_ap = _matmul(dattn_out, attn, "tn", "mm_attn_proj_dw")
    dq, dkv_cur, dkv_prev, d_sinks = _attn_bwd(proj, attn_sinks, dattn, aw, "attn_bwd")
    dkv = dkv_cur + jnp.concatenate([dkv_prev[ATTN_BLOCK:], jnp.zeros((ATTN_BLOCK, 2 * KV_WIDTH), F32)], axis=0)
    dproj = jnp.concatenate([dq, dkv.astype(BF16), du, dga, dgs], axis=1)
    dh1 = _matmul(dproj, w_in_t, "nn", "mm_in_dx")
    g_in = _matmul(dproj, h1, "tn", "mm_in_dw")
    grad_x, d_shift1, d_scale1, d_gain1 = _norm_mod_bwd(dh1, xs, dx2, norm_mix_g, mod, 1, "norm_mod1_bwd")

    dmod = jnp.concatenate([d_shift1, d_scale1, d_gate1, d_shift2, d_scale2, d_gate2], axis=1)
    small_parts = [dmod, d_gain1, d_sinks, dlam[0], dlam[1], _unpack_diag(dbr_p, SSM_STATE, SSM_GROUP),
                   _unpack_diag(dbi_p, SSM_STATE, SSM_GROUP), _unpack_diag(dcr_p, SSM_STATE, SSM_GROUP),
                   _unpack_diag(dci_p, SSM_STATE, SSM_GROUP), d_dvec, d_gain2, d_conv_b, d_conv_w, d_final_g]
    return loss_part, grad_x, [g_in, g_ap, g_glu, g_out], small_parts


def _kernel_impl(x, c, ada_w, ada_b, norm_mix_g, w_in, attn_sinks, w_attn_proj, ssm_a_re, ssm_a_im, ssm_log_dt, ssm_b_re, ssm_b_im,
                 ssm_c_re, ssm_c_im, ssm_d, w_ssm_glu, w_out, norm_ffn_g, w_ffn_up, ffn_conv_w, ffn_conv_b, w_ffn_down, final_g,
                 loss_target, ms, vs):
    ax, ay, ac = lax.axis_index("x"), lax.axis_index("y"), lax.axis_index("c")
    chip = 2 * ax + ay
    batch_row = 4 * ax + 2 * ay + ac
    d = x.shape[2]
    aw = w_attn_proj.shape[1]
    sw = w_ssm_glu.shape[1]
    ff = N_CHIPS * ffn_conv_w.shape[2]
    ngroups = sw // SSM_GROUP

    c_all = _all_gather8(jnp.pad(c, ((0, 7), (0, 0))), "gather_c").reshape(N_DEV, 8, d)[:, 0, :]
    ncol = ada_w.shape[2]
    b_shard = lax.dynamic_slice(ada_b, (0, chip * ncol), (1, ncol))
    mod_blk = _ada_fwd(c_all, ada_w[0], b_shard, "ada_fwd")
    mod_all = _all_gather8(mod_blk, "gather_mod").reshape(N_CHIPS, 2, 8, ncol)[:, 0]
    mod = lax.dynamic_slice(mod_all, (0, batch_row, 0), (N_CHIPS, 1, ncol)).reshape(1, 6 * d)

    shards = [w_in[0].T.astype(BF16), w_attn_proj[0].T.astype(BF16), w_ssm_glu[0].T.astype(BF16), w_out[0],
              w_ffn_up[0].T.astype(BF16), w_ffn_down[0]]
    conv_w_all = _all_gather8(jnp.pad(ffn_conv_w[0], ((0, 5), (0, 0))), "gather_conv_w")
    conv_w_full = conv_w_all.reshape(N_CHIPS, 2, 8, ff // N_CHIPS)[:, 0, :3].transpose(1, 0, 2).reshape(3, ff)
    placed = [_place_shard(s, f"place_shard_{i}", after=(mod, conv_w_full) if i == 0 else ()) for i, s in enumerate(shards)]
    (first, later), started = _gather_start([placed[:1], placed[1:]], "gather_start")
    (w_in_t,) = _forward_then_finish(first, started, "w_in")
    mod = mod + started[0:1, 0:1]

    def later_weights(after):
        return _forward_then_finish(later, after, "later")

    pending = {}

    def ffn_grads_ready(g_up, g_down, mod_now):
        pending["ffn"], token = _reduce_start([g_up, g_down], "ffn")
        return mod_now + token[0:1, 0:1]

    ssm = (ssm_a_re[0], ssm_a_im[0], ssm_log_dt[0], ssm_b_re[0], ssm_b_im[0], ssm_c_re[0], ssm_c_im[0], ssm_d[0])
    loss_part, grad_x, grads, small_parts = _local_step(
        x[0], loss_target[0], mod, w_in_t, later_weights, ffn_grads_ready, norm_mix_g, attn_sinks, ssm, norm_ffn_g, conv_w_full,
        ffn_conv_b, final_g, aw, sw, ff)
    loss = lax.psum(loss_part[0, 0], ("x", "y", "c"))

    small_shapes = [p.shape for p in small_parts]
    part_buf = _flatten_pad(small_parts)
    rows = part_buf.shape[0]
    gathered = _all_gather8(part_buf, "gather_small").reshape(N_DEV, rows, SMALL_COLS)
    pending["rest"], rest_token = _reduce_start(grads, "rest", after=(gathered,))
    gup_t, grad_w_down = _reduce_finish(pending["ffn"], (rest_token,), "ffn")
    grad_w_up = gup_t.T
    summed = _sum_devices(gathered, "sum_small")
    (s_dmod, s_gain1, s_sinks, s_lr, s_li, s_bbr, s_bbi, s_cr, s_ci, s_dd, s_gain2, s_cb, s_cw, s_fg) = _split_flat(summed, small_shapes)
    _, ssm_vjp = jax.vjp(_ssm_discretize, *ssm[:5])
    g_a_re, g_a_im, g_log_dt, g_b_re, g_b_im = ssm_vjp((s_lr.reshape(ngroups, SSM_STATE), s_li.reshape(ngroups, SSM_STATE), s_bbr, s_bbi))
    g_c_re, g_c_im = jnp.swapaxes(s_cr, 1, 2), jnp.swapaxes(s_ci, 1, 2)
    g_conv_w = lax.dynamic_slice(s_cw, (0, chip * (ff // N_CHIPS)), (3, ff // N_CHIPS))

    dmod_all = gathered.reshape(N_DEV, -1)[:, :6 * d]
    dmod_shard = lax.dynamic_slice(dmod_all, (0, chip * ncol), (N_DEV, ncol))
    ada_res = _ada_bwd_adam(c_all.T, dmod_shard, ada_w[0], ms["ada_w"][0], vs["ada_w"][0], "ada_bwd_adam")

    res = {"ada_w": tuple(o[None] for o in ada_res)}

    def adam_big(nm, w, g):
        res[nm] = (g[None],) + tuple(o[None] for o in _adam(w[0], g, ms[nm][0], vs[nm][0], "adam_" + nm))

    adam_big("w_ffn_up", w_ffn_up, grad_w_up)
    adam_big("w_ffn_down", w_ffn_down, grad_w_down)

    small = [("ada_b", ada_b, s_dmod), ("norm_mix_g", norm_mix_g, s_gain1), ("attn_sinks", attn_sinks, s_sinks),
             ("ssm_a_re", ssm_a_re, g_a_re), ("ssm_a_im", ssm_a_im, g_a_im), ("ssm_log_dt", ssm_log_dt, g_log_dt),
             ("ssm_b_re", ssm_b_re, g_b_re), ("ssm_b_im", ssm_b_im, g_b_im), ("ssm_c_re", ssm_c_re, g_c_re),
             ("ssm_c_im", ssm_c_im, g_c_im), ("ssm_d", ssm_d, s_dd), ("norm_ffn_g", norm_ffn_g, s_gain2),
             ("ffn_conv_w", ffn_conv_w, g_conv_w), ("ffn_conv_b", ffn_conv_b, s_cb), ("final_g", final_g, s_fg)]
    shapes = [t[1].shape for t in small]
    bufs = [_flatten_pad([t[1] for t in small]), _flatten_pad([t[2] for t in small]),
            _flatten_pad([ms[t[0]] for t in small]), _flatten_pad([vs[t[0]] for t in small])]
    s_delta, s_m, s_v = _adam(*bufs, "adam_small")
    for t, dl, m2, v2 in zip(small, _split_flat(s_delta, shapes), _split_flat(s_m, shapes), _split_flat(s_v, shapes)):
        res[t[0]] = (t[2].reshape(t[1].shape), dl, m2, v2)

    done = (s_delta, res["w_ffn_up"][1], res["w_ffn_down"][1], res["ada_w"][1])
    gi_t, gap_t, gglu_t, grad_w_out = _reduce_finish(pending["rest"], done, "rest")
    adam_big("w_in", w_in, gi_t.T)
    adam_big("w_attn_proj", w_attn_proj, gap_t.T)
    adam_big("w_ssm_glu", w_ssm_glu, gglu_t.T)
    adam_big("w_out", w_out, grad_w_out)

    outs = [loss, grad_x[None]]
    for i in range(4):
        outs += [res[nm][i] for nm in WEIGHT_ORDER]
    return tuple(outs)


WEIGHT_ORDER = ("ada_w", "ada_b", "norm_mix_g", "w_in", "attn_sinks", "w_attn_proj", "ssm_a_re", "ssm_a_im", "ssm_log_dt", "ssm_b_re",
                "ssm_b_im", "ssm_c_re", "ssm_c_im", "ssm_d", "w_ssm_glu", "w_out", "norm_ffn_g", "w_ffn_up", "ffn_conv_w", "ffn_conv_b",
                "w_ffn_down", "final_g")


def kernel(x, c, ada_w, ada_b, norm_mix_g, w_in, attn_sinks, w_attn_proj, ssm_a_re, ssm_a_im, ssm_log_dt, ssm_b_re, ssm_b_im, ssm_c_re, ssm_c_im, ssm_d, w_ssm_glu, w_out, norm_ffn_g, w_ffn_up, ffn_conv_w, ffn_conv_b, w_ffn_down, final_g, loss_target, m_ada_w, m_ada_b, m_norm_mix_g, m_w_in, m_attn_sinks, m_w_attn_proj, m_ssm_a_re, m_ssm_a_im, m_ssm_log_dt, m_ssm_b_re, m_ssm_b_im, m_ssm_c_re, m_ssm_c_im, m_ssm_d, m_w_ssm_glu, m_w_out, m_norm_ffn_g, m_w_ffn_up, m_ffn_conv_w, m_ffn_conv_b, m_w_ffn_down, m_final_g, v_ada_w, v_ada_b, v_norm_mix_g, v_w_in, v_attn_sinks, v_w_attn_proj, v_ssm_a_re, v_ssm_a_im, v_ssm_log_dt, v_ssm_b_re, v_ssm_b_im, v_ssm_c_re, v_ssm_c_im, v_ssm_d, v_w_ssm_glu, v_w_out, v_norm_ffn_g, v_w_ffn_up, v_ffn_conv_w, v_ffn_conv_b, v_w_ffn_down, v_final_g):
    ms = dict(zip(WEIGHT_ORDER, (m_ada_w, m_ada_b, m_norm_mix_g, m_w_in, m_attn_sinks, m_w_attn_proj, m_ssm_a_re, m_ssm_a_im, m_ssm_log_dt,
                                 m_ssm_b_re, m_ssm_b_im, m_ssm_c_re, m_ssm_c_im, m_ssm_d, m_w_ssm_glu, m_w_out, m_norm_ffn_g, m_w_ffn_up,
                                 m_ffn_conv_w, m_ffn_conv_b, m_w_ffn_down, m_final_g)))
    vs = dict(zip(WEIGHT_ORDER, (v_ada_w, v_ada_b, v_norm_mix_g, v_w_in, v_attn_sinks, v_w_attn_proj, v_ssm_a_re, v_ssm_a_im, v_ssm_log_dt,
                                 v_ssm_b_re, v_ssm_b_im, v_ssm_c_re, v_ssm_c_im, v_ssm_d, v_w_ssm_glu, v_w_out, v_norm_ffn_g, v_w_ffn_up,
                                 v_ffn_conv_w, v_ffn_conv_b, v_w_ffn_down, v_final_g)))
    return _kernel_impl(x, c, ada_w, ada_b, norm_mix_g, w_in, attn_sinks, w_attn_proj, ssm_a_re, ssm_a_im, ssm_log_dt, ssm_b_re, ssm_b_im,
                        ssm_c_re, ssm_c_im, ssm_d, w_ssm_glu, w_out, norm_ffn_g, w_ffn_up, ffn_conv_w, ffn_conv_b, w_ffn_down, final_g,
                        loss_target, ms, vs)
```

```python
import math

import jax
import jax.numpy as jnp
from jax import lax
from jax.experimental import pallas as pl
from jax.experimental.pallas import tpu as pltpu

F32 = jnp.float32
BF16 = jnp.bfloat16
MESH = pl.DeviceIdType.MESH

HEAD_DIM = 64
N_KV_HEADS = 2
KV_WIDTH = N_KV_HEADS * HEAD_DIM
ATTN_BLOCK = 128
NEG_INF = -1e30
SSM_GROUP = 16
SSM_STATE = 64
GROUPS_PER_TILE = 8
RMS_EPS = 1e-6
ADAM_LR = 0.001
ADAM_B1 = 0.9
ADAM_B2 = 0.999
ADAM_EPS = 1e-08
ADAM_WD = 0.01
ADAM_STEP = 10
N_CHIPS = 4
N_DEV = 8
VMEM_LIMIT_BYTES = 56 * 1024 * 1024
LANES = 128
SMALL_COLS = 1024


def _tile(dim, target, mult=LANES):
    if dim <= target:
        return dim
    for t in range(target // mult * mult, 0, -mult):
        if dim % t == 0:
            return t
    raise ValueError(f"no tile for {dim}")


def _params(sem=None):
    return pltpu.CompilerParams(dimension_semantics=sem, vmem_limit_bytes=VMEM_LIMIT_BYTES)


def _sigmoid(x):
    return 1.0 / (1.0 + jnp.exp(-x))


def _matmul(a, b, mode, name, out_dtype=BF16, tm=1536, tn=1536, tk=2048, interleave=None):
    if mode == "nn":
        (m, k), (k2, n) = a.shape, b.shape
    elif mode == "nt":
        (m, k), (n, k2) = a.shape, b.shape
    else:
        (k, m), (k2, n) = a.shape, b.shape
    assert k == k2, (a.shape, b.shape, mode)
    if interleave is not None:
        tn, tk, tm = (interleave, tk, tm) if mode == "nt" else (tn, interleave, tm) if mode == "nn" else (tn, tk, interleave)
        half = {"nt": n, "nn": k, "tn": m}[mode] // (2 * interleave)

        def perm(blk):
            return blk // 2 + (blk % 2) * half
    else:
        def perm(blk):
            return blk
    tm, tn, tk = _tile(m, tm), _tile(n, tn), _tile(k, tk)
    nk = k // tk
    if mode == "tn":
        a_spec = pl.BlockSpec((tk, tm), lambda i, j, kk: (kk, i))
    else:
        a_spec = pl.BlockSpec((tm, tk), lambda i, j, kk: (i, kk))
    if mode == "nt":
        b_spec = pl.BlockSpec((tn, tk), lambda i, j, kk: (perm(j), kk))
    elif mode == "nn":
        b_spec = pl.BlockSpec((tk, tn), lambda i, j, kk: (perm(kk), j))
    else:
        b_spec = pl.BlockSpec((tk, tn), lambda i, j, kk: (kk, j))
    out_rows = perm if mode == "tn" else (lambda blk: blk)
    dims = {"nn": (((1,), (0,)), ((), ())), "nt": (((1,), (1,)), ((), ())), "tn": (((0,), (0,)), ((), ()))}[mode]

    def body(a_ref, b_ref, o_ref, acc_ref):
        kk = pl.program_id(2)

        @pl.when(kk == 0)
        def _():
            acc_ref[...] = jnp.zeros_like(acc_ref)

        acc_ref[...] += lax.dot_general(a_ref[...], b_ref[...], dims, preferred_element_type=F32)

        @pl.when(kk == nk - 1)
        def _():
            o_ref[...] = acc_ref[...].astype(o_ref.dtype)

    return pl.pallas_call(
        body,
        out_shape=jax.ShapeDtypeStruct((m, n), out_dtype),
        grid=(m // tm, n // tn, nk),
        in_specs=[a_spec, b_spec],
        out_specs=pl.BlockSpec((tm, tn), lambda i, j, kk: (out_rows(i), j)),
        scratch_shapes=[pltpu.VMEM((tm, tn), F32)],
        name=name,
        compiler_params=_params(("parallel", "parallel", "arbitrary")),
    )(a, b)


def _row_spec(tl, w, col=0):
    return pl.BlockSpec((tl, w), lambda i, col=col: (i, col))


def _vec_spec(w, col=0):
    return pl.BlockSpec((1, w), lambda i, col=col: (0, col))


def _norm_mod(x, gain, mod, sc_col, sh_col, name):
    l, d = x.shape
    tl = _tile(l, 256, 8)

    def body(x_ref, g_ref, sc_ref, sh_ref, h_ref):
        xv = x_ref[...]
        r = lax.rsqrt(jnp.mean(xv * xv, axis=-1, keepdims=True) + RMS_EPS)
        h_ref[...] = ((xv * r) * g_ref[...] * (1.0 + sc_ref[...]) + sh_ref[...]).astype(BF16)

    return pl.pallas_call(
        body,
        out_shape=jax.ShapeDtypeStruct((l, d), BF16),
        grid=(l // tl,),
        in_specs=[_row_spec(tl, d), _vec_spec(d), _vec_spec(d, sc_col), _vec_spec(d, sh_col)],
        out_specs=_row_spec(tl, d),
        name=name,
        compiler_params=_params(("parallel",)),
    )(x, gain, mod, mod)


def _resid_norm_mod(x, mo, gain, mod, gate_col, sc_col, sh_col, name):
    l, d = x.shape
    tl = _tile(l, 256, 8)

    def body(x_ref, mo_ref, g_ref, gate_ref, sc_ref, sh_ref, x2_ref, h_ref):
        xv = x_ref[...] + gate_ref[...] * mo_ref[...]
        x2_ref[...] = xv
        r = lax.rsqrt(jnp.mean(xv * xv, axis=-1, keepdims=True) + RMS_EPS)
        h_ref[...] = ((xv * r) * g_ref[...] * (1.0 + sc_ref[...]) + sh_ref[...]).astype(BF16)

    return pl.pallas_call(
        body,
        out_shape=(jax.ShapeDtypeStruct((l, d), F32), jax.ShapeDtypeStruct((l, d), BF16)),
        grid=(l // tl,),
        in_specs=[_row_spec(tl, d), _row_spec(tl, d), _vec_spec(d), _vec_spec(d, gate_col), _vec_spec(d, sc_col),
                  _vec_spec(d, sh_col)],
        out_specs=(_row_spec(tl, d), _row_spec(tl, d)),
        name=name,
        compiler_params=_params(("parallel",)),
    )(x, mo, gain, mod, mod, mod)


def _final_loss(x2, f, mod, gate_col, final_g, target, name):
    l, d = x2.shape
    tl = _tile(l, 256, 8)

    def body(x2_ref, f_ref, gate_ref, fg_ref, t_ref, loss_ref, dfg_ref, dgate_ref, dx3_ref, df_ref):
        i = pl.program_id(0)
        fv = f_ref[...]
        x3 = x2_ref[...] + gate_ref[...] * fv
        r = lax.rsqrt(jnp.mean(x3 * x3, axis=-1, keepdims=True) + RMS_EPS)
        xh = x3 * r
        err = xh * fg_ref[...] - t_ref[...]
        part = 0.5 * jnp.sum(jnp.mean(err * err, axis=-1, keepdims=True), axis=0, keepdims=True)
        dout = err * (1.0 / d)
        dxh = dout * fg_ref[...]
        dx3 = r * (dxh - xh * jnp.mean(dxh * xh, axis=-1, keepdims=True))
        dx3_ref[...] = dx3
        df_ref[...] = (gate_ref[...] * dx3).astype(BF16)

        @pl.when(i == 0)
        def _():
            loss_ref[...] = jnp.zeros_like(loss_ref)
            dfg_ref[...] = jnp.zeros_like(dfg_ref)
            dgate_ref[...] = jnp.zeros_like(dgate_ref)

        loss_ref[...] += jnp.broadcast_to(part, loss_ref.shape)
        dfg_ref[...] += jnp.sum(dout * xh, axis=0, keepdims=True)
        dgate_ref[...] += jnp.sum(dx3 * fv, axis=0, keepdims=True)

    vec = pl.BlockSpec((1, d), lambda i: (0, 0))
    return pl.pallas_call(
        body,
        out_shape=(jax.ShapeDtypeStruct((1, LANES), F32), jax.ShapeDtypeStruct((1, d), F32),
                   jax.ShapeDtypeStruct((1, d), F32), jax.ShapeDtypeStruct((l, d), F32),
                   jax.ShapeDtypeStruct((l, d), BF16)),
        grid=(l // tl,),
        in_specs=[_row_spec(tl, d), _row_spec(tl, d), _vec_spec(d, gate_col), vec, _row_spec(tl, d)],
        out_specs=(pl.BlockSpec((1, LANES), lambda i: (0, 0)), vec, vec, _row_spec(tl, d), _row_spec(tl, d)),
        name=name,
        compiler_params=_params(("arbitrary",)),
    )(x2, f, mod, final_g, target)


def _norm_mod_bwd(dh, x, dx_res, gain, mod, sc_col, name, branch=None, gate_col=None):
    l, d = x.shape
    tl = _tile(l, 256, 8)
    with_gate = branch is not None

    def body(*refs):
        if with_gate:
            dh_ref, x_ref, dr_ref, g_ref, sc_ref, br_ref, gate_ref, dx_ref, dsh_ref, dsc_ref, dg_ref, dm_ref, dgate_ref = refs
        else:
            dh_ref, x_ref, dr_ref, g_ref, sc_ref, dx_ref, dsh_ref, dsc_ref, dg_ref = refs
        i = pl.program_id(0)
        xv = x_ref[...]
        dhv = dh_ref[...].astype(F32)
        r = lax.rsqrt(jnp.mean(xv * xv, axis=-1, keepdims=True) + RMS_EPS)
        xh = xv * r
        dn = dhv * (1.0 + sc_ref[...])
        dxh = dn * g_ref[...]
        dx = dr_ref[...] + r * (dxh - xh * jnp.mean(dxh * xh, axis=-1, keepdims=True))
        dx_ref[...] = dx

        @pl.when(i == 0)
        def _():
            dsh_ref[...] = jnp.zeros_like(dsh_ref)
            dsc_ref[...] = jnp.zeros_like(dsc_ref)
            dg_ref[...] = jnp.zeros_like(dg_ref)
            if with_gate:
                dgate_ref[...] = jnp.zeros_like(dgate_ref)

        dsh_ref[...] += jnp.sum(dhv, axis=0, keepdims=True)
        dsc_ref[...] += jnp.sum(dhv * (xh * g_ref[...]), axis=0, keepdims=True)
        dg_ref[...] += jnp.sum(dn * xh, axis=0, keepdims=True)
        if with_gate:
            dm_ref[...] = (gate_ref[...] * dx).astype(BF16)
            dgate_ref[...] += jnp.sum(dx * br_ref[...], axis=0, keepdims=True)

    vec = pl.BlockSpec((1, d), lambda i: (0, 0))
    in_specs = [_row_spec(tl, d), _row_spec(tl, d), _row_spec(tl, d), vec, _vec_spec(d, sc_col)]
    args = [dh, x, dx_res, gain, mod]
    out_shape = [jax.ShapeDtypeStruct((l, d), F32)] + [jax.ShapeDtypeStruct((1, d), F32)] * 3
    out_specs = [_row_spec(tl, d), vec, vec, vec]
    if with_gate:
        in_specs += [_row_spec(tl, d), _vec_spec(d, gate_col)]
        args += [branch, mod]
        out_shape += [jax.ShapeDtypeStruct((l, d), BF16), jax.ShapeDtypeStruct((1, d), F32)]
        out_specs += [_row_spec(tl, d), vec]
    return pl.pallas_call(
        body, out_shape=tuple(out_shape), grid=(l // tl,), in_specs=in_specs, out_specs=tuple(out_specs),
        name=name, compiler_params=_params(("arbitrary",)),
    )(*args)


def _attn_mask(n, rows):
    del rows
    qi = lax.broadcasted_iota(jnp.int32, (ATTN_BLOCK, 2 * ATTN_BLOCK), 0)
    kj = lax.broadcasted_iota(jnp.int32, (ATTN_BLOCK, 2 * ATTN_BLOCK), 1)
    rel = qi + ATTN_BLOCK - kj
    return jnp.where((rel >= 0) & (rel < ATTN_BLOCK) & ((kj >= ATTN_BLOCK) | (n > 0)), 0.0, NEG_INF)


def _attn_probs(qs, kh, sink, mask):
    rows = qs.shape[0]
    s = lax.dot_general(qs, kh, (((1,), (1,)), ((), ())), preferred_element_type=F32) * (HEAD_DIM ** -0.5)
    s = s.reshape(-1, ATTN_BLOCK, 2 * ATTN_BLOCK) + mask[None]
    m = jnp.maximum(jnp.max(s, axis=-1, keepdims=True), sink)
    p = jnp.exp(s - m)
    es = jnp.exp(sink - m)
    inv = 1.0 / (jnp.sum(p, axis=-1, keepdims=True) + es)
    return (p * inv).reshape(rows, 2 * ATTN_BLOCK), (es * inv).reshape(rows, 1)


def _stack_heads(src_ref, dst_ref, g, qpk):
    for i in range(qpk):
        h = g * qpk + i
        dst_ref[i * ATTN_BLOCK:(i + 1) * ATTN_BLOCK, :] = src_ref[:, h * HEAD_DIM:(h + 1) * HEAD_DIM]


def _unstack_heads(val, dst_ref, g, qpk):
    for i in range(qpk):
        h = g * qpk + i
        dst_ref[:, h * HEAD_DIM:(h + 1) * HEAD_DIM] = val[i * ATTN_BLOCK:(i + 1) * ATTN_BLOCK, :].astype(dst_ref.dtype)


def _sink_column(sinks):
    return sinks.reshape(-1, 1, 1)


def _sink_spec(nq):
    return pl.BlockSpec((nq, 1, 1), lambda n: (0, 0, 0))


def _attn_specs(aw):
    kvb = aw // (2 * KV_WIDTH)
    q_spec = pl.BlockSpec((ATTN_BLOCK, aw), lambda n: (n, 0))
    kv_cur = pl.BlockSpec((ATTN_BLOCK, 2 * KV_WIDTH), lambda n: (n, kvb))
    kv_prev = pl.BlockSpec((ATTN_BLOCK, 2 * KV_WIDTH), lambda n: (jnp.maximum(n - 1, 0), kvb))
    return q_spec, kv_cur, kv_prev


def _attn_fwd(proj, sinks, aw, name):
    l = proj.shape[0]
    nq = aw // HEAD_DIM
    qpk = nq // N_KV_HEADS
    assert aw % (2 * KV_WIDTH) == 0

    rows = qpk * ATTN_BLOCK

    def body(q_ref, kvc_ref, kvp_ref, sink_ref, o_ref):
        n = pl.program_id(0)
        valid = _attn_mask(n, rows) == 0.0
        kv = jnp.concatenate([kvp_ref[...], kvc_ref[...]], axis=0)
        for h in range(nq):
            g = h // qpk
            qh = q_ref[:, h * HEAD_DIM:(h + 1) * HEAD_DIM]
            kh = kv[:, g * HEAD_DIM:(g + 1) * HEAD_DIM]
            vh = kv[:, KV_WIDTH + g * HEAD_DIM:KV_WIDTH + (g + 1) * HEAD_DIM]
            sink = sink_ref[0:1, h:h + 1]
            s = lax.dot_general(qh, kh, (((1,), (1,)), ((), ())), preferred_element_type=F32) * (HEAD_DIM ** -0.5)
            s = jnp.where(valid, s, NEG_INF)
            m = jnp.maximum(jnp.max(s, axis=-1, keepdims=True), sink)
            p = jnp.exp(s - m)
            p = p * (1.0 / (jnp.sum(p, axis=-1, keepdims=True) + jnp.exp(sink - m)))
            o = jnp.dot(p.astype(BF16), vh, preferred_element_type=F32)
            o_ref[:, h * HEAD_DIM:(h + 1) * HEAD_DIM] = o.astype(BF16)

    q_spec, kv_cur, kv_prev = _attn_specs(aw)
    return pl.pallas_call(
        body,
        out_shape=jax.ShapeDtypeStruct((l, aw), BF16),
        grid=(l // ATTN_BLOCK,),
        in_specs=[q_spec, kv_cur, kv_prev, pl.BlockSpec((1, nq), lambda n: (0, 0))],
        out_specs=pl.BlockSpec((ATTN_BLOCK, aw), lambda n: (n, 0)),
        name=name,
        compiler_params=_params(("parallel",)),
    )(proj, proj, proj, sinks)


def _attn_bwd(proj, sinks, dattn, aw, name):
    l = proj.shape[0]
    nq = aw // HEAD_DIM
    qpk = nq // N_KV_HEADS
    scale = HEAD_DIM ** -0.5

    rows = qpk * ATTN_BLOCK
    tn_dims = (((0,), (0,)), ((), ()))

    def body(q_ref, kvc_ref, kvp_ref, sink_ref, do_ref, dq_ref, dcur_ref, dprev_ref, dsink_ref, q_scr, do_scr):
        n = pl.program_id(0)
        mask = _attn_mask(n, rows)
        kv = jnp.concatenate([kvp_ref[...], kvc_ref[...]], axis=0)
        lane = lax.broadcasted_iota(jnp.int32, (1, nq), 1)
        dsink = jnp.zeros((1, nq), F32)
        dks, dvs = [], []
        for g in range(N_KV_HEADS):
            kh = kv[:, g * HEAD_DIM:(g + 1) * HEAD_DIM]
            vh = kv[:, KV_WIDTH + g * HEAD_DIM:KV_WIDTH + (g + 1) * HEAD_DIM]
            _stack_heads(q_ref, q_scr.at[g], g, qpk)
            _stack_heads(do_ref, do_scr.at[g], g, qpk)
            qs, dos = q_scr[g], do_scr[g]
            p, ps = _attn_probs(qs, kh, sink_ref[g * qpk:(g + 1) * qpk], mask)
            pb = p.astype(BF16)
            o = jnp.dot(pb, vh, preferred_element_type=F32)
            delta = jnp.sum(dos.astype(F32) * o, axis=-1, keepdims=True)
            dp = lax.dot_general(dos, vh, (((1,), (1,)), ((), ())), preferred_element_type=F32)
            ds = (p * (dp - delta)).astype(BF16)
            _unstack_heads(jnp.dot(ds, kh, preferred_element_type=F32) * scale, dq_ref, g, qpk)
            dks.append(lax.dot_general(ds, qs, tn_dims, preferred_element_type=F32) * scale)
            dvs.append(lax.dot_general(pb, dos, tn_dims, preferred_element_type=F32))
            t = ps * delta
            for i in range(qpk):
                part = -jnp.sum(t[i * ATTN_BLOCK:(i + 1) * ATTN_BLOCK, :], axis=0, keepdims=True)
                dsink += jnp.where(lane == g * qpk + i, part, 0.0)
        dkv = jnp.concatenate(dks + dvs, axis=1)
        dprev_ref[...] = dkv[:ATTN_BLOCK]
        dcur_ref[...] = dkv[ATTN_BLOCK:]

        @pl.when(n == 0)
        def _():
            dsink_ref[...] = jnp.zeros_like(dsink_ref)

        dsink_ref[...] += dsink

    q_spec, kv_cur, kv_prev = _attn_specs(aw)
    blk = pl.BlockSpec((ATTN_BLOCK, 2 * KV_WIDTH), lambda n: (n, 0))
    return pl.pallas_call(
        body,
        out_shape=(jax.ShapeDtypeStruct((l, aw), BF16), jax.ShapeDtypeStruct((l, 2 * KV_WIDTH), F32),
                   jax.ShapeDtypeStruct((l, 2 * KV_WIDTH), F32), jax.ShapeDtypeStruct((1, nq), F32)),
        grid=(l // ATTN_BLOCK,),
        in_specs=[q_spec, kv_cur, kv_prev, _sink_spec(nq),
                  pl.BlockSpec((ATTN_BLOCK, aw), lambda n: (n, 0))],
        out_specs=(pl.BlockSpec((ATTN_BLOCK, aw), lambda n: (n, 0)), blk, blk, pl.BlockSpec((1, nq), lambda n: (0, 0))),
        scratch_shapes=[pltpu.VMEM((N_KV_HEADS, rows, HEAD_DIM), BF16)] * 2,
        name=name,
        compiler_params=_params(("arbitrary",)),
    )(proj, proj, proj, _sink_column(sinks), dattn)


def _ssm_discretize(a_re, a_im, log_dt, b_re, b_im):
    dt = jnp.exp(log_dt)[:, None]
    mag = jnp.exp(a_re * dt)
    lr, li = mag * jnp.cos(a_im * dt), mag * jnp.sin(a_im * dt)
    den = a_re * a_re + a_im * a_im
    zr = ((lr - 1.0) * a_re + li * a_im) / den
    zi = (li * a_re - (lr - 1.0) * a_im) / den
    bbar_r = zr[:, :, None] * b_re - zi[:, :, None] * b_im
    bbar_i = zr[:, :, None] * b_im + zi[:, :, None] * b_re
    return lr, li, bbar_r, bbar_i


def _cmul(ar, ai, br, bi):
    return ar * br - ai * bi, ar * bi + ai * br


def _scan_tables(lr, li):
    lr, li = lr.reshape(1, -1), li.reshape(1, -1)
    pows = [(lr, li)]
    for _ in range(7):
        pows.append(_cmul(*pows[-1], lr, li))
    row = jnp.arange(8)[:, None]
    fwd, bwd = [], []
    for d in (1, 2, 4):
        pr, pi = pows[d - 1]
        fwd += [jnp.where(row >= d, pr, 0.0), jnp.where(row >= d, pi, 0.0)]
        bwd += [jnp.where(row < 8 - d, pr, 0.0), jnp.where(row < 8 - d, -pi, 0.0)]
    fwd += [jnp.concatenate([p[0] for p in pows], 0), jnp.concatenate([p[1] for p in pows], 0)]
    bwd += [jnp.concatenate([p[0] for p in pows[::-1]], 0), jnp.concatenate([-p[1] for p in pows[::-1]], 0)]
    return jnp.concatenate(fwd, 0), jnp.concatenate(bwd, 0)


def _pack_in(b):
    g, n, p = b.shape
    t = g // GROUPS_PER_TILE
    eye = jnp.eye(GROUPS_PER_TILE, dtype=b.dtype)
    bb = b.reshape(t, GROUPS_PER_TILE, n, p)
    return jnp.einsum("tgnp,gh->tgphn", bb, eye).reshape(t, GROUPS_PER_TILE * p, GROUPS_PER_TILE * n)


def _pack_out(c):
    g, p, n = c.shape
    t = g // GROUPS_PER_TILE
    eye = jnp.eye(GROUPS_PER_TILE, dtype=c.dtype)
    cc = c.reshape(t, GROUPS_PER_TILE, p, n)
    return jnp.einsum("tgpn,gh->tgnhp", cc, eye).reshape(t, GROUPS_PER_TILE * n, GROUPS_PER_TILE * p)


def _unpack_diag(x, n, p):
    t = x.shape[0]
    xx = x.reshape(t, GROUPS_PER_TILE, n, GROUPS_PER_TILE, p)
    eye = jnp.eye(GROUPS_PER_TILE, dtype=x.dtype)
    return jnp.einsum("tgnhp,gh->tgnp", xx, eye).reshape(t * GROUPS_PER_TILE, n, p)


def _scan_rows(hr_ref, hi_ref, tab_ref, l, reverse, prev_refs=None):
    w = hr_ref.shape[1]
    tabs = [tab_ref[pl.ds(8 * i, 8), :] for i in range(8)]
    nchunk = l // 8
    row = lax.broadcasted_iota(jnp.int32, (8, w), 0)

    def step(s, carry):
        k = nchunk - 1 - s if reverse else s
        t8 = pl.multiple_of(k * 8, 8)
        hr = hr_ref[pl.ds(t8, 8), :]
        hi = hi_ref[pl.ds(t8, 8), :]
        for idx, d in enumerate((1, 2, 4)):
            mr, mi = tabs[2 * idx], tabs[2 * idx + 1]
            shift = 8 - d if reverse else d
            sr = pltpu.roll(hr, shift, 0)
            si = pltpu.roll(hi, shift, 0)
            hr, hi = hr + mr * sr - mi * si, hi + mr * si + mi * sr
        cr, ci = carry[0], carry[1]
        hr, hi = hr + tabs[6] * cr - tabs[7] * ci, hi + tabs[6] * ci + tabs[7] * cr
        hr_ref[pl.ds(t8, 8), :] = hr
        hi_ref[pl.ds(t8, 8), :] = hi
        if not reverse:
            return hr[7:8, :], hi[7:8, :]
        out = (hr[0:1, :], hi[0:1, :])
        if prev_refs is None:
            return out
        fr_ref, fi_ref = prev_refs
        tp = pl.multiple_of(jnp.maximum(k - 1, 0) * 8, 8)
        keep = jnp.where(k > 0, 1.0, 0.0)
        lr_last = fr_ref[pl.ds(tp, 8), :][7:8, :] * keep
        li_last = fi_ref[pl.ds(tp, 8), :][7:8, :] * keep
        pr = jnp.where(row == 0, lr_last, pltpu.roll(fr_ref[pl.ds(t8, 8), :], 1, 0))
        pi = jnp.where(row == 0, li_last, pltpu.roll(fi_ref[pl.ds(t8, 8), :], 1, 0))
        return out + (carry[2] + hr * pr + hi * pi, carry[3] + hi * pr - hr * pi)

    zero = jnp.zeros((1, w), F32)
    init = (zero, zero)
    if reverse and prev_refs is not None:
        init += (jnp.zeros((8, w), F32), jnp.zeros((8, w), F32))
    return lax.fori_loop(0, nchunk, step, init)


def _s5_dims(sw):
    chan = GROUPS_PER_TILE * SSM_GROUP
    states = GROUPS_PER_TILE * SSM_STATE
    assert chan == LANES and sw % chan == 0
    return sw // chan, chan, states


def _s5_fwd(proj, u_off, packs, dvec, tab_f, sw, name):
    l = proj.shape[0]
    nt, chan, states = _s5_dims(sw)
    ch = _tile(l, 512, 8)
    ub = u_off // chan
    assert u_off % chan == 0

    def body(u_ref, br_ref, bi_ref, cr_ref, ci_ref, d_ref, tab_ref, y_ref, hr_ref, hi_ref):
        for i in range(l // ch):
            rows = pl.ds(i * ch, ch)
            u = u_ref[rows, :]
            hr_ref[rows, :] = jnp.dot(u, br_ref[0], preferred_element_type=F32)
            hi_ref[rows, :] = jnp.dot(u, bi_ref[0], preferred_element_type=F32)
        _scan_rows(hr_ref, hi_ref, tab_ref, l, reverse=False)
        for i in range(l // ch):
            rows = pl.ds(i * ch, ch)
            y = jnp.dot(hr_ref[rows, :].astype(BF16), cr_ref[0], preferred_element_type=F32)
            y -= jnp.dot(hi_ref[rows, :].astype(BF16), ci_ref[0], preferred_element_type=F32)
            y_ref[rows, :] = y + d_ref[...] * u_ref[rows, :].astype(F32)

    pin = pl.BlockSpec((1, chan, states), lambda t: (t, 0, 0))
    pout = pl.BlockSpec((1, states, chan), lambda t: (t, 0, 0))
    return pl.pallas_call(
        body,
        out_shape=jax.ShapeDtypeStruct((l, sw), F32),
        grid=(nt,),
        in_specs=[pl.BlockSpec((l, chan), lambda t: (0, ub + t)), pin, pin, pout, pout,
                  pl.BlockSpec((1, chan), lambda t: (0, t)), pl.BlockSpec((64, states), lambda t: (0, t))],
        out_specs=pl.BlockSpec((l, chan), lambda t: (0, t)),
        scratch_shapes=[pltpu.VMEM((l, states), F32), pltpu.VMEM((l, states), F32)],
        name=name,
        compiler_params=_params(("parallel",)),
    )(proj, packs["br"], packs["bi"], packs["cr"], packs["ci"], dvec, tab_f)


def _s5_bwd(proj, u_off, dy, packs, dvec, tab_f, tab_b, sw, name):
    l = proj.shape[0]
    nt, chan, states = _s5_dims(sw)
    ch = _tile(l, 512, 8)
    ub = u_off // chan
    tn_dims = (((0,), (0,)), ((), ()))

    def body(u_ref, dy_ref, br_ref, bi_ref, brt_ref, bit_ref, crt_ref, cit_ref, d_ref, tabf_ref, tabb_ref,
             du_ref, dlam_ref, dbr_ref, dbi_ref, dcr_ref, dci_ref, dd_ref, hr_ref, hi_ref, gr_ref, gi_ref):
        for i in range(l // ch):
            rows = pl.ds(i * ch, ch)
            u = u_ref[rows, :]
            hr_ref[rows, :] = jnp.dot(u, br_ref[0], preferred_element_type=F32)
            hi_ref[rows, :] = jnp.dot(u, bi_ref[0], preferred_element_type=F32)
            dyv = dy_ref[rows, :]
            gr_ref[rows, :] = jnp.dot(dyv, crt_ref[0], preferred_element_type=F32)
            gi_ref[rows, :] = -jnp.dot(dyv, cit_ref[0], preferred_element_type=F32)
        _scan_rows(hr_ref, hi_ref, tabf_ref, l, reverse=False)
        _, _, acc_r, acc_i = _scan_rows(gr_ref, gi_ref, tabb_ref, l, reverse=True, prev_refs=(hr_ref, hi_ref))
        dlam_ref[...] = jnp.concatenate(
            [jnp.sum(acc_r, axis=0, keepdims=True), jnp.sum(acc_i, axis=0, keepdims=True), jnp.zeros((6, states), F32)], axis=0)
        dbr_ref[...] = jnp.zeros_like(dbr_ref)
        dbi_ref[...] = jnp.zeros_like(dbi_ref)
        dcr_ref[...] = jnp.zeros_like(dcr_ref)
        dci_ref[...] = jnp.zeros_like(dci_ref)
        dd = jnp.zeros((1, chan), F32)
        for i in range(l // ch):
            rows = pl.ds(i * ch, ch)
            u = u_ref[rows, :]
            dyv = dy_ref[rows, :]
            grb = gr_ref[rows, :].astype(BF16)
            gib = gi_ref[rows, :].astype(BF16)
            dbr_ref[0] += lax.dot_general(grb, u, tn_dims, preferred_element_type=F32)
            dbi_ref[0] += lax.dot_general(gib, u, tn_dims, preferred_element_type=F32)
            dcr_ref[0] += lax.dot_general(hr_ref[rows, :].astype(BF16), dyv, tn_dims, preferred_element_type=F32)
            dci_ref[0] -= lax.dot_general(hi_ref[rows, :].astype(BF16), dyv, tn_dims, preferred_element_type=F32)
            du = jnp.dot(grb, brt_ref[0], preferred_element_type=F32) + jnp.dot(gib, bit_ref[0], preferred_element_type=F32)
            dyf = dyv.astype(F32)
            du_ref[rows, :] = (du + d_ref[...] * dyf).astype(BF16)
            dd += jnp.sum(dyf * u.astype(F32), axis=0, keepdims=True)
        dd_ref[...] = dd

    pin = pl.BlockSpec((1, chan, states), lambda t: (t, 0, 0))
    pout = pl.BlockSpec((1, states, chan), lambda t: (t, 0, 0))
    seq = pl.BlockSpec((l, chan), lambda t: (0, t))
    tab = pl.BlockSpec((64, states), lambda t: (0, t))
    vec = pl.BlockSpec((1, chan), lambda t: (0, t))
    pack_shape = jax.ShapeDtypeStruct((nt, states, chan), F32)
    return pl.pallas_call(
        body,
        out_shape=(jax.ShapeDtypeStruct((l, sw), BF16), jax.ShapeDtypeStruct((8, nt * states), F32),
                   pack_shape, pack_shape, pack_shape, pack_shape, jax.ShapeDtypeStruct((1, sw), F32)),
        grid=(nt,),
        in_specs=[pl.BlockSpec((l, chan), lambda t: (0, ub + t)), seq, pin, pin, pout, pout, pin, pin, vec, tab, tab],
        out_specs=(seq, pl.BlockSpec((8, states), lambda t: (0, t)), pout, pout, pout, pout, vec),
        scratch_shapes=[pltpu.VMEM((l, states), F32)] * 4,
        name=name,
        compiler_params=_params(("parallel",)),
    )(proj, dy, packs["br"], packs["bi"], packs["brt"], packs["bit"], packs["crt"], packs["cit"], dvec, tab_f, tab_b)


GELU_K = math.sqrt(2.0 / math.pi)
GELU_C = 0.044715


def _gelu(y, name):
    l, w = y.shape
    tl = _tile(l, 512, 8)

    def body(y_ref, o_ref):
        v = y_ref[...]
        o_ref[...] = (0.5 * v * (1.0 + jnp.tanh(GELU_K * (v + GELU_C * v * v * v)))).astype(BF16)

    return pl.pallas_call(body, out_shape=jax.ShapeDtypeStruct((l, w), BF16), grid=(l // tl,),
                          in_specs=[_row_spec(tl, w)], out_specs=_row_spec(tl, w), name=name,
                          compiler_params=_params(("parallel",)))(y)


def _gelu_bwd(y, dg, name):
    l, w = y.shape
    tl = _tile(l, 512, 8)

    def body(y_ref, dg_ref, o_ref):
        v = y_ref[...]
        t = jnp.tanh(GELU_K * (v + GELU_C * v * v * v))
        grad = 0.5 * (1.0 + t) + 0.5 * v * (1.0 - t * t) * GELU_K * (1.0 + 3.0 * GELU_C * v * v)
        o_ref[...] = (dg_ref[...].astype(F32) * grad).astype(BF16)

    return pl.pallas_call(body, out_shape=jax.ShapeDtypeStruct((l, w), BF16), grid=(l // tl,),
                          in_specs=[_row_spec(tl, w), _row_spec(tl, w)], out_specs=_row_spec(tl, w), name=name,
                          compiler_params=_params(("parallel",)))(y, dg)


MIX_COLS = 256


def _mix(proj, ga_off, gs_off, attn_out, glu, d, name):
    l = proj.shape[0]
    tl = _tile(l, 1024, 8)
    cb = MIX_COLS
    nj = d // cb
    assert d % cb == 0 and ga_off % cb == 0 and gs_off % cb == 0

    def body(ga_ref, gs_ref, a_ref, ua_ref, ub_ref, o_ref):
        ssm = ua_ref[...].astype(F32) * _sigmoid(ub_ref[...].astype(F32))
        o_ref[...] = (_sigmoid(ga_ref[...].astype(F32)) * a_ref[...].astype(F32)
                      + _sigmoid(gs_ref[...].astype(F32)) * ssm).astype(BF16)

    def spec(off):
        return pl.BlockSpec((tl, cb), lambda i, j, off=off: (i, off // cb + j))

    return pl.pallas_call(
        body, out_shape=jax.ShapeDtypeStruct((l, d), BF16), grid=(l // tl, nj),
        in_specs=[spec(ga_off), spec(gs_off), spec(0), spec(0), spec(d)], out_specs=spec(0), name=name,
        compiler_params=_params(("parallel", "parallel")),
    )(proj, proj, attn_out, glu, glu)


def _mix_bwd(proj, ga_off, gs_off, attn_out, glu, dmixed, d, name):
    l = proj.shape[0]
    tl = _tile(l, 1024, 8)
    cb = MIX_COLS
    nj = d // cb

    def body(ga_ref, gs_ref, a_ref, ua_ref, ub_ref, dm_ref, dga_ref, dgs_ref, da_ref, dglu_ref):
        s = pl.program_id(2)
        dm = dm_ref[...].astype(F32)
        sa = _sigmoid(ga_ref[...].astype(F32))
        ss = _sigmoid(gs_ref[...].astype(F32))
        sb = _sigmoid(ub_ref[...].astype(F32))
        ua = ua_ref[...].astype(F32)
        dssm = dm * ss

        @pl.when(s == 0)
        def _():
            dga_ref[...] = (dm * a_ref[...].astype(F32) * sa * (1.0 - sa)).astype(BF16)
            da_ref[...] = (dm * sa).astype(BF16)
            dgs_ref[...] = (dm * (ua * sb) * ss * (1.0 - ss)).astype(BF16)
            dglu_ref[...] = (dssm * sb).astype(BF16)

        @pl.when(s == 1)
        def _():
            dglu_ref[...] = (dssm * ua * sb * (1.0 - sb)).astype(BF16)

    def spec(off):
        return pl.BlockSpec((tl, cb), lambda i, j, s, off=off: (i, off // cb + j))

    out = jax.ShapeDtypeStruct((l, d), BF16)
    return pl.pallas_call(
        body, out_shape=(out, out, out, jax.ShapeDtypeStruct((l, 2 * d), BF16)), grid=(l // tl, nj, 2),
        in_specs=[spec(ga_off), spec(gs_off), spec(0), spec(0), spec(d), spec(0)],
        out_specs=(spec(0), spec(0), spec(0), pl.BlockSpec((tl, cb), lambda i, j, s: (i, j + s * nj))), name=name,
        compiler_params=_params(("parallel", "parallel", "arbitrary")),
    )(proj, proj, attn_out, glu, glu, dmixed)


CONV_BLOCKS = 4
HALO = 16


def _shift_rows(v, k, head):
    row = lax.broadcasted_iota(jnp.int32, v.shape, 0)
    out = pltpu.roll(v, k, 0)
    for r in range(k):
        out = jnp.where(row == r, head[HALO - k + r:HALO - k + r + 1, :], out)
    return out


def _shift_rows_up(v, k, tail):
    n = v.shape[0]
    row = lax.broadcasted_iota(jnp.int32, v.shape, 0)
    out = pltpu.roll(v, n - k, 0)
    for r in range(k):
        out = jnp.where(row == n - k + r, tail[r:r + 1, :], out)
    return out


def _conv_gate(g, head, w_ref, b_ref):
    return w_ref[0:1, :] * _shift_rows(g, 2, head) + w_ref[1:2, :] * _shift_rows(g, 1, head) + w_ref[2:3, :] * g + b_ref[...]


def _conv_act(up, conv_w, conv_b, ff, cw, name):
    l = up.shape[0]
    tl = _tile(l, 256, HALO)
    nj = ff // cw
    hb = tl // HALO

    def body(g_ref, gp_ref, v_ref, w_ref, b_ref, o_ref):
        i = pl.program_id(0)
        head = gp_ref[...].astype(F32) * jnp.where(i > 0, 1.0, 0.0)
        gc = _conv_gate(g_ref[...].astype(F32), head, w_ref, b_ref)
        o_ref[...] = (gc * _sigmoid(gc) * v_ref[...].astype(F32)).astype(BF16)

    return pl.pallas_call(
        body, out_shape=jax.ShapeDtypeStruct((l, ff), BF16), grid=(l // tl, nj),
        in_specs=[pl.BlockSpec((tl, cw), lambda i, j: (i, 2 * j)),
                  pl.BlockSpec((HALO, cw), lambda i, j: (jnp.maximum(i * hb - 1, 0), 2 * j)),
                  pl.BlockSpec((tl, cw), lambda i, j: (i, 2 * j + 1)),
                  pl.BlockSpec((3, cw), lambda i, j: (0, j)), pl.BlockSpec((1, cw), lambda i, j: (0, j))],
        out_specs=pl.BlockSpec((tl, cw), lambda i, j: (i, j)), name=name,
        compiler_params=_params(("parallel", "parallel")),
    )(up, up, up, conv_w, conv_b)


def _conv_act_bwd(up, da, conv_w, conv_b, ff, cw, name):
    l = up.shape[0]
    tl = _tile(l, 256, HALO)
    nj = ff // cw
    hb = tl // HALO
    ni = l // tl

    def body(g_ref, gp_ref, gn_ref, v_ref, vn_ref, da_ref, dan_ref, w_ref, b_ref, dup_ref, dw_ref, db_ref):
        i = pl.program_id(0)
        g = g_ref[...].astype(F32)
        head = gp_ref[...].astype(F32) * jnp.where(i > 0, 1.0, 0.0)
        g1 = _shift_rows(g, 1, head)
        g2 = _shift_rows(g, 2, head)
        gc = w_ref[0:1, :] * g2 + w_ref[1:2, :] * g1 + w_ref[2:3, :] * g + b_ref[...]
        sg = _sigmoid(gc)
        dav = da_ref[...].astype(F32)
        dgc = dav * v_ref[...].astype(F32) * (sg * (1.0 + gc * (1.0 - sg)))
        gn = gn_ref[...].astype(F32)
        gcn = _conv_gate(gn, g[tl - HALO:, :], w_ref, b_ref)
        sgn = _sigmoid(gcn)
        dgcn = dan_ref[...].astype(F32) * vn_ref[...].astype(F32) * (sgn * (1.0 + gcn * (1.0 - sgn)))
        dgcn = dgcn * jnp.where(i < ni - 1, 1.0, 0.0)
        dgate = w_ref[2:3, :] * dgc + w_ref[1:2, :] * _shift_rows_up(dgc, 1, dgcn) + w_ref[0:1, :] * _shift_rows_up(dgc, 2, dgcn)
        dup_ref[:, :cw] = dgate.astype(BF16)
        dup_ref[:, cw:] = (dav * (gc * sg)).astype(BF16)
        zero = jnp.zeros((1, cw), F32)
        dw_ref[...] = jnp.concatenate(
            [jnp.sum(dgc * g2, axis=0, keepdims=True), jnp.sum(dgc * g1, axis=0, keepdims=True),
             jnp.sum(dgc * g, axis=0, keepdims=True)] + [zero] * 5, axis=0)
        db_ref[...] = jnp.concatenate([jnp.sum(dgc, axis=0, keepdims=True)] + [zero] * 7, axis=0)

    def cur(col):
        return pl.BlockSpec((tl, cw), lambda i, j, col=col: (i, 2 * j + col))

    def prev(col):
        return pl.BlockSpec((HALO, cw), lambda i, j, col=col: (jnp.maximum(i * hb - 1, 0), 2 * j + col))

    def nxt(col):
        return pl.BlockSpec((HALO, cw), lambda i, j, col=col: (jnp.minimum((i + 1) * hb, l // HALO - 1), 2 * j + col))

    part = jax.ShapeDtypeStruct((ni * 8, ff), F32)
    part_spec = pl.BlockSpec((8, cw), lambda i, j: (i, j))
    return pl.pallas_call(
        body, out_shape=(jax.ShapeDtypeStruct((l, 2 * ff), BF16), part, part), grid=(ni, nj),
        in_specs=[cur(0), prev(0), nxt(0), cur(1), nxt(1), pl.BlockSpec((tl, cw), lambda i, j: (i, j)),
                  pl.BlockSpec((HALO, cw), lambda i, j: (jnp.minimum((i + 1) * hb, l // HALO - 1), j)),
                  pl.BlockSpec((3, cw), lambda i, j: (0, j)), pl.BlockSpec((1, cw), lambda i, j: (0, j))],
        out_specs=(pl.BlockSpec((tl, 2 * cw), lambda i, j: (i, j)), part_spec, part_spec), name=name,
        compiler_params=_params(("parallel", "parallel")),
    )(up, up, up, up, up, da, da, conv_w, conv_b)


def _sum_rows8(parts, name):
    n8, w = parts.shape
    n = n8 // 8
    cw = _tile(w, 2048)

    def body(p_ref, o_ref):
        acc = p_ref[0:8, :]
        for k in range(1, n):
            acc = acc + p_ref[8 * k:8 * k + 8, :]
        o_ref[...] = acc

    return pl.pallas_call(body, out_shape=jax.ShapeDtypeStruct((8, w), F32), grid=(w // cw,),
                          in_specs=[pl.BlockSpec((n8, cw), lambda j: (0, j))], out_specs=pl.BlockSpec((8, cw), lambda j: (0, j)),
                          name=name, compiler_params=_params(("parallel",)))(parts)


def _ada_fwd(c_all, w_shard, b_shard, name):
    nb, d = c_all.shape
    n = w_shard.shape[1]
    tn = _tile(n, 512)

    def body(c_ref, w_ref, b_ref, o_ref):
        cv = c_ref[...]
        cond = (cv * _sigmoid(cv)).astype(BF16)
        o_ref[...] = jnp.dot(cond, w_ref[...].astype(BF16), preferred_element_type=F32) + b_ref[...]

    return pl.pallas_call(
        body, out_shape=jax.ShapeDtypeStruct((nb, n), F32), grid=(n // tn,),
        in_specs=[pl.BlockSpec((nb, d), lambda j: (0, 0)), pl.BlockSpec((d, tn), lambda j: (0, j)),
                  pl.BlockSpec((1, tn), lambda j: (0, j))],
        out_specs=pl.BlockSpec((nb, tn), lambda j: (0, j)), name=name, compiler_params=_params(("parallel",)),
    )(c_all, w_shard, b_shard)


def _adam_update(w, g, m, v):
    m2 = ADAM_B1 * m + (1.0 - ADAM_B1) * g
    v2 = ADAM_B2 * v + (1.0 - ADAM_B2) * (g * g)
    m_hat = m2 / (1.0 - ADAM_B1 ** ADAM_STEP)
    v_hat = v2 / (1.0 - ADAM_B2 ** ADAM_STEP)
    return -ADAM_LR * (m_hat / (jnp.sqrt(v_hat) + ADAM_EPS) + ADAM_WD * w), m2, v2


def _ada_bwd_adam(c_all_t, dmod_shard, w, m, v, name):
    d, nb = c_all_t.shape
    n = w.shape[1]
    tr, tn = _tile(d, 512, 8), _tile(n, 512)

    def body(c_ref, dm_ref, w_ref, m_ref, v_ref, g_ref, dl_ref, m2_ref, v2_ref):
        cv = c_ref[...]
        cond = cv * _sigmoid(cv)
        g = cond[:, 0:1] * dm_ref[0:1, :]
        for b in range(1, nb):
            g = g + cond[:, b:b + 1] * dm_ref[b:b + 1, :]
        g_ref[...] = g
        dl_ref[...], m2_ref[...], v2_ref[...] = _adam_update(w_ref[...], g, m_ref[...], v_ref[...])

    blk = pl.BlockSpec((tr, tn), lambda i, j: (i, j))
    out = jax.ShapeDtypeStruct((d, n), F32)
    return pl.pallas_call(
        body, out_shape=(out, out, out, out), grid=(d // tr, n // tn),
        in_specs=[pl.BlockSpec((tr, nb), lambda i, j: (i, 0)), pl.BlockSpec((nb, tn), lambda i, j: (0, j)), blk, blk, blk],
        out_specs=(blk, blk, blk, blk), name=name, compiler_params=_params(("parallel", "parallel")),
    )(c_all_t, dmod_shard, w, m, v)


def _adam(w, g, m, v, name):
    r, c = w.shape
    tr = _tile(r, 256, 8)

    def body(w_ref, g_ref, m_ref, v_ref, dl_ref, m2_ref, v2_ref):
        dl_ref[...], m2_ref[...], v2_ref[...] = _adam_update(w_ref[...], g_ref[...], m_ref[...], v_ref[...])

    blk = pl.BlockSpec((tr, c), lambda i: (i, 0))
    out = jax.ShapeDtypeStruct((r, c), F32)
    return pl.pallas_call(body, out_shape=(out, out, out), grid=(r // tr,), in_specs=[blk] * 4, out_specs=(blk,) * 3,
                          name=name, compiler_params=_params(("parallel",)))(w, g, m, v)


def _sum_devices(gathered, name):
    nd, r, c = gathered.shape
    tr = _tile(r, 64, 8)

    def body(g_ref, o_ref):
        acc = g_ref[0]
        for k in range(1, nd):
            acc = acc + g_ref[k]
        o_ref[...] = acc

    return pl.pallas_call(body, out_shape=jax.ShapeDtypeStruct((r, c), F32), grid=(r // tr,),
                          in_specs=[pl.BlockSpec((nd, tr, c), lambda i: (0, i, 0))], out_specs=pl.BlockSpec((tr, c), lambda i: (i, 0)),
                          name=name, compiler_params=_params(("parallel",)))(gathered)


def _place():
    x, y, c = lax.axis_index("x"), lax.axis_index("y"), lax.axis_index("c")
    chips = [(1 - x, y), (x, 1 - y), (1 - x, 1 - y)]
    return x, y, c, chips


def _all_gather8(block, name):
    m_per, n = block.shape

    def body(x_ref, out_ref, send_sems, recv_sems, local_sem):
        x, y, c, chips = _place()
        me, sibling = (x, y, c), (x, y, 1 - c)

        def rows(px, py, pc):
            return out_ref.at[pl.ds((4 * px + 2 * py + pc) * m_per, m_per), :]

        def copy(k, blk, to, src=None):
            return pltpu.make_async_remote_copy(
                src_ref=rows(*blk) if src is None else src, dst_ref=rows(*blk), send_sem=send_sems.at[k],
                recv_sem=recv_sems.at[k], device_id=to, device_id_type=MESH)

        mine = pltpu.make_async_copy(x_ref, rows(*me), local_sem)
        mine.start()
        first = [copy(0, me, sibling, src=x_ref)]
        first += [copy(1 + j, me, (*chip, c), src=x_ref) for j, chip in enumerate(chips)]
        for cp in first:
            cp.start()
        passed = [copy(4 + j, (*chip, c), sibling) for j, chip in enumerate(chips)]
        for j, chip in enumerate(chips):
            copy(1 + j, (*chip, c), me).wait_recv()
            passed[j].start()
        copy(0, sibling, me).wait_recv()
        for j, chip in enumerate(chips):
            copy(4 + j, (*chip, 1 - c), me).wait_recv()
        for cp in first + passed:
            cp.wait_send()
        mine.wait()

    return pl.pallas_call(
        body,
        out_shape=jax.ShapeDtypeStruct((N_DEV * m_per, n), block.dtype),
        in_specs=[pl.BlockSpec(memory_space=pltpu.VMEM)],
        out_specs=pl.BlockSpec(memory_space=pltpu.VMEM),
        scratch_shapes=[pltpu.SemaphoreType.DMA((7,)), pltpu.SemaphoreType.DMA((7,)), pltpu.SemaphoreType.DMA],
        name=name,
        compiler_params=pltpu.CompilerParams(vmem_limit_bytes=VMEM_LIMIT_BYTES),
    )(block)


ANY = pl.BlockSpec(memory_space=pl.ANY)


def _place_shard(shard, name, after=()):
    r, k = shard.shape
    tb = _tile(r, 512, 16)
    nb = r // tb
    chip = (2 * lax.axis_index("x") + lax.axis_index("y")).astype(jnp.int32).reshape(1)

    def body(j_ref, s_ref, *rest):
        rest[-1][...] = s_ref[...].astype(BF16)

    return pl.pallas_call(
        body, out_shape=jax.ShapeDtypeStruct((N_CHIPS * r, k), BF16),
        grid_spec=pltpu.PrefetchScalarGridSpec(
            num_scalar_prefetch=1, grid=(nb,),
            in_specs=[pl.BlockSpec((tb, k), lambda i, j_ref: (i, 0))] + [ANY] * len(after),
            out_specs=pl.BlockSpec((tb, k), lambda i, j_ref: (j_ref[0] * nb + i, 0))),
        name=name, compiler_params=_params(("parallel",)),
    )(chip, shard, *after)


HBM_SPEC = pl.BlockSpec(memory_space=pltpu.HBM)
SEM_SPEC = pl.BlockSpec(memory_space=pltpu.SEMAPHORE)
TOKEN_SPEC = pl.BlockSpec(memory_space=pltpu.VMEM)
SPLIT_COPY = pltpu.CompilerParams(has_side_effects=pltpu.SideEffectType.DATAFLOW_SIDE_EFFECTING)


def _in_hbm(arrays):
    return [pltpu.with_memory_space_constraint(a, pltpu.HBM) for a in arrays]


def _hbm_like(arrays):
    return tuple(pltpu.HBM(a.shape, a.dtype) for a in arrays)


def _token_shape():
    return jax.ShapeDtypeStruct((8, LANES), F32)


def _gathered_rows(buf, px, py, half):
    r = buf.shape[0] // N_CHIPS
    return buf.at[pl.ds(pl.multiple_of((2 * px + py) * r + half * (r // 2), 16), r // 2), :]


def _gather_start(groups, name):
    sizes = [len(g) for g in groups]
    flat = [b for g in groups for b in g]
    nb, ng = len(flat), len(groups)

    def body(*refs):
        bufs = refs[:nb]
        sems = refs[nb:nb + 2 * ng]
        token = refs[-1]
        x, y, c, chips = _place()
        pos = 0
        for gi, nw in enumerate(sizes):
            for k, chip in enumerate(chips):
                for w in range(nw):
                    mine = _gathered_rows(bufs[pos + w], x, y, c)
                    pltpu.make_async_remote_copy(src_ref=mine, dst_ref=mine, send_sem=sems[2 * gi].at[k * nw + w], recv_sem=sems[2 * gi + 1].at[k * nw + w],
                                                 device_id=(*chip, c), device_id_type=MESH).start()
            pos += nw
        token[...] = jnp.zeros_like(token)

    sem_shapes = tuple(pltpu.SemaphoreType.DMA((3 * n,)) for n in sizes for _ in range(2))
    outs = pl.pallas_call(
        body, name=name, out_shape=sem_shapes + _hbm_like(flat) + (_token_shape(),),
        in_specs=[HBM_SPEC] * nb, out_specs=(SEM_SPEC,) * (2 * ng) + (HBM_SPEC,) * nb + (TOKEN_SPEC,),
        input_output_aliases={i: 2 * ng + i for i in range(nb)}, compiler_params=SPLIT_COPY,
    )(*_in_hbm(flat))
    res, pos = [], 2 * ng
    for gi, n in enumerate(sizes):
        res.append((outs[2 * gi], outs[2 * gi + 1], list(outs[pos:pos + n])))
        pos += n
    return res, outs[-1]


def _gather_forward(bufs, ici_send, ici_recv, after, name):
    nw, na = len(bufs), len(after)

    def body(*refs):
        b = refs[:nw]
        isend, irecv = refs[nw], refs[nw + 1]
        dsend, drecv = refs[nw + 2 + na], refs[nw + 3 + na]
        x, y, c, chips = _place()
        for k, chip in enumerate(chips):
            for w in range(nw):
                landed = _gathered_rows(b[w], *chip, c)
                pltpu.make_async_remote_copy(src_ref=landed, dst_ref=landed, send_sem=isend.at[k * nw + w], recv_sem=irecv.at[k * nw + w],
                                             device_id=(*chip, c), device_id_type=MESH).wait_recv()
                pltpu.make_async_remote_copy(src_ref=landed, dst_ref=landed, send_sem=dsend.at[k * nw + w], recv_sem=drecv.at[k * nw + w],
                                             device_id=(x, y, 1 - c), device_id_type=MESH).start()
        for k, chip in enumerate(chips):
            for w in range(nw):
                mine = _gathered_rows(b[w], x, y, c)
                pltpu.make_async_remote_copy(src_ref=mine, dst_ref=mine, send_sem=isend.at[k * nw + w], recv_sem=irecv.at[k * nw + w],
                                             device_id=(*chip, c), device_id_type=MESH).wait_send()

    sem = pltpu.SemaphoreType.DMA((3 * nw,))
    outs = pl.pallas_call(
        body, name=name, out_shape=(sem, sem) + _hbm_like(bufs),
        in_specs=[HBM_SPEC] * nw + [SEM_SPEC, SEM_SPEC] + [ANY] * na, out_specs=(SEM_SPEC, SEM_SPEC) + (HBM_SPEC,) * nw,
        input_output_aliases={i: 2 + i for i in range(nw)}, compiler_params=SPLIT_COPY,
    )(*bufs, ici_send, ici_recv, *after)
    return outs[0], outs[1], list(outs[2:])


def _gather_finish(bufs, d2d_send, d2d_recv, name):
    nw = len(bufs)

    def body(*refs):
        b = refs[:nw]
        dsend, drecv = refs[nw], refs[nw + 1]
        x, y, c, chips = _place()
        for k, chip in enumerate(chips):
            for w in range(nw):
                theirs = _gathered_rows(b[w], *chip, 1 - c)
                pltpu.make_async_remote_copy(src_ref=theirs, dst_ref=theirs, send_sem=dsend.at[k * nw + w], recv_sem=drecv.at[k * nw + w],
                                             device_id=(x, y, 1 - c), device_id_type=MESH).wait_recv()
                passed = _gathered_rows(b[w], *chip, c)
                pltpu.make_async_remote_copy(src_ref=passed, dst_ref=passed, send_sem=dsend.at[k * nw + w], recv_sem=drecv.at[k * nw + w],
                                             device_id=(x, y, 1 - c), device_id_type=MESH).wait_send()

    outs = pl.pallas_call(
        body, name=name, out_shape=_hbm_like(bufs), in_specs=[HBM_SPEC] * nw + [SEM_SPEC, SEM_SPEC], out_specs=(HBM_SPEC,) * nw,
        input_output_aliases={i: i for i in range(nw)}, compiler_params=SPLIT_COPY,
    )(*bufs, d2d_send, d2d_recv)
    return list(outs)


def _scatter_start(partials, name):
    nw = len(partials)
    landing = [lax.empty((3,) + p.shape[1:], p.dtype) for p in partials]

    def body(*refs):
        src, land = refs[:nw], refs[nw:2 * nw]
        send_sems, recv_sems = refs[2 * nw], refs[2 * nw + 1]
        token = refs[-1]
        x, y, c, chips = _place()
        for k, chip in enumerate(chips):
            for w in range(nw):
                pltpu.make_async_remote_copy(src_ref=src[w].at[2 * chip[0] + chip[1]], dst_ref=land[w].at[k], send_sem=send_sems.at[k * nw + w],
                                             recv_sem=recv_sems.at[k * nw + w], device_id=(*chip, c), device_id_type=MESH).start()
        token[...] = jnp.zeros_like(token)

    sem = pltpu.SemaphoreType.DMA((3 * nw,))
    outs = pl.pallas_call(
        body, name=name, out_shape=(sem, sem) + _hbm_like(partials) + _hbm_like(landing) + (_token_shape(),),
        in_specs=[HBM_SPEC] * (2 * nw), out_specs=(SEM_SPEC, SEM_SPEC) + (HBM_SPEC,) * (2 * nw) + (TOKEN_SPEC,),
        input_output_aliases={i: 2 + i for i in range(2 * nw)}, compiler_params=SPLIT_COPY,
    )(*_in_hbm(partials), *_in_hbm(landing))
    return (outs[0], outs[1], list(outs[2:2 + nw]), list(outs[2 + nw:2 + 2 * nw])), outs[-1]


def _scatter_wait(started, after, name):
    send_sems, recv_sems, partials, landing = started
    nw = len(partials)

    def body(*refs):
        src, land = refs[:nw], refs[nw:2 * nw]
        ssem, rsem = refs[2 * nw], refs[2 * nw + 1]
        x, y, c, chips = _place()
        for k, chip in enumerate(chips):
            for w in range(nw):
                cp = pltpu.make_async_remote_copy(src_ref=src[w].at[2 * chip[0] + chip[1]], dst_ref=land[w].at[k], send_sem=ssem.at[k * nw + w],
                                                  recv_sem=rsem.at[k * nw + w], device_id=(*chip, c), device_id_type=MESH)
                cp.wait_send()
                cp.wait_recv()

    outs = pl.pallas_call(
        body, name=name, out_shape=_hbm_like(partials) + _hbm_like(landing),
        in_specs=[HBM_SPEC] * (2 * nw) + [SEM_SPEC, SEM_SPEC] + [ANY] * len(after), out_specs=(HBM_SPEC,) * (2 * nw),
        input_output_aliases={i: i for i in range(2 * nw)}, compiler_params=SPLIT_COPY,
    )(*partials, *landing, send_sems, recv_sems, *after)
    return list(outs[:nw]), list(outs[nw:])


def _swap_halves(grads, name, after=()):
    nw, na = len(grads), len(after)

    def body(*refs):
        ins, outs = refs[:nw], refs[nw + na:2 * nw + na]
        send_sems, recv_sems = refs[2 * nw + na:]
        x, y, c, _ = _place()
        copies = []
        for w in range(nw):
            r = grads[w].shape[0] // N_CHIPS
            h = r // 2
            for j in range(N_CHIPS):
                copies.append(pltpu.make_async_remote_copy(
                    src_ref=ins[w].at[pl.ds(pl.multiple_of(j * r + (1 - c) * h, 16), h), :], dst_ref=outs[w].at[pl.ds(j * h, h), :],
                    send_sem=send_sems.at[w, j], recv_sem=recv_sems.at[w, j], device_id=(x, y, 1 - c), device_id_type=MESH))
                copies[-1].start()
        for cp in copies:
            cp.wait()

    sem = pltpu.SemaphoreType.DMA((nw, N_CHIPS))
    return pl.pallas_call(
        body, out_shape=tuple(jax.ShapeDtypeStruct((g.shape[0] // 2, g.shape[1]), g.dtype) for g in grads),
        in_specs=[ANY] * (nw + na), out_specs=(ANY,) * nw, scratch_shapes=[sem, sem], name=name,
    )(*grads, *after)


def _add_halves(grad, other, name):
    k = grad.shape[1]
    h = other.shape[0] // N_CHIPS
    tb = _tile(h, 512, 16)
    g4 = grad.reshape(N_CHIPS, 2, h, k)
    o3 = other.reshape(N_CHIPS, h, k)
    core = lax.axis_index("c").astype(jnp.int32).reshape(1)

    def body(c_ref, g_ref, o_ref, p_ref):
        p_ref[...] = (g_ref[...].astype(F32) + o_ref[...].astype(F32)).astype(BF16)

    return pl.pallas_call(
        body, out_shape=jax.ShapeDtypeStruct((N_CHIPS, h, k), BF16),
        grid_spec=pltpu.PrefetchScalarGridSpec(
            num_scalar_prefetch=1, grid=(N_CHIPS, h // tb),
            in_specs=[pl.BlockSpec((None, None, tb, k), lambda j, i, c_ref: (j, c_ref[0], i, 0)),
                      pl.BlockSpec((None, tb, k), lambda j, i, c_ref: (j, i, 0))],
            out_specs=pl.BlockSpec((None, tb, k), lambda j, i, c_ref: (j, i, 0))),
        name=name, compiler_params=_params(("parallel", "parallel")),
    )(core, g4, o3)


def _add_partials(partial, others, name):
    _, h, k = partial.shape
    tb = _tile(h, 512, 16)
    nb = h // tb
    place = jnp.stack([2 * lax.axis_index("x") + lax.axis_index("y"), lax.axis_index("c")]).astype(jnp.int32)

    def body(s_ref, p_ref, o0_ref, o1_ref, o2_ref, f_ref):
        f_ref[...] = ((p_ref[...].astype(F32) + o0_ref[...].astype(F32)) + o1_ref[...].astype(F32)) + o2_ref[...].astype(F32)

    def other(s):
        return pl.BlockSpec((None, tb, k), lambda i, s_ref, s=s: (s, i, 0))

    return pl.pallas_call(
        body, out_shape=jax.ShapeDtypeStruct((2 * h, k), F32),
        grid_spec=pltpu.PrefetchScalarGridSpec(
            num_scalar_prefetch=1, grid=(nb,),
            in_specs=[pl.BlockSpec((None, tb, k), lambda i, s_ref: (s_ref[0], i, 0)), other(0), other(1), other(2)],
            out_specs=pl.BlockSpec((tb, k), lambda i, s_ref: (s_ref[1] * nb + i, 0))),
        name=name, compiler_params=_params(("parallel",)),
    )(place, partial, others, others, others)


def _share_halves(fulls, name):
    nw = len(fulls)

    def body(*refs):
        ins, outs = refs[:nw], refs[nw:2 * nw]
        send_sems, recv_sems = refs[2 * nw:]
        x, y, c, _ = _place()
        copies = []
        for w in range(nw):
            h = fulls[w].shape[0] // 2
            start = pl.multiple_of(c * h, 8)
            copies.append(pltpu.make_async_remote_copy(
                src_ref=ins[w].at[pl.ds(start, h), :], dst_ref=outs[w].at[pl.ds(start, h), :], send_sem=send_sems.at[w],
                recv_sem=recv_sems.at[w], device_id=(x, y, 1 - c), device_id_type=MESH))
            copies[-1].start()
        for cp in copies:
            cp.wait()

    sem = pltpu.SemaphoreType.DMA((nw,))
    return pl.pallas_call(
        body, out_shape=tuple(jax.ShapeDtypeStruct(f.shape, f.dtype) for f in fulls),
        in_specs=[ANY] * nw, out_specs=(ANY,) * nw, scratch_shapes=[sem, sem], name=name,
        input_output_aliases={w: w for w in range(nw)},
    )(*fulls)


def _forward_then_finish(started_group, after, tag):
    ici_send, ici_recv, bufs = started_group
    d2d_send, d2d_recv, bufs = _gather_forward(bufs, ici_send, ici_recv, after, f"gather_forward_{tag}")
    return _gather_finish(bufs, d2d_send, d2d_recv, f"gather_finish_{tag}")


def _reduce_start(grads, tag, after=()):
    from_sibling = _swap_halves(grads, f"swap_halves_{tag}", after)
    chip_sums = [_add_halves(g, o, f"add_halves_{tag}_{i}") for i, (g, o) in enumerate(zip(grads, from_sibling))]
    return _scatter_start(chip_sums, f"scatter_start_{tag}")


def _reduce_finish(started, after, tag):
    chip_sums, from_chips = _scatter_wait(started, after, f"scatter_wait_{tag}")
    fulls = [_add_partials(p, o, f"add_partials_{tag}_{i}") for i, (p, o) in enumerate(zip(chip_sums, from_chips))]
    return _share_halves(fulls, f"share_halves_{tag}")


def _flatten_pad(parts, cols=SMALL_COLS):
    flat = jnp.concatenate([p.reshape(-1) for p in parts])
    rows = -(-flat.shape[0] // (8 * cols)) * 8
    return jnp.pad(flat, (0, rows * cols - flat.shape[0])).reshape(rows, cols)


def _split_flat(buf, shapes):
    flat = buf.reshape(-1)
    out, off = [], 0
    for s in shapes:
        n = math.prod(s)
        out.append(flat[off:off + n].reshape(s))
        off += n
    return out


def _ssm_setup(ssm_a_re, ssm_a_im, ssm_log_dt, ssm_b_re, ssm_b_im, ssm_c_re, ssm_c_im):
    lam_r, lam_i, bbar_r, bbar_i = _ssm_discretize(ssm_a_re, ssm_a_im, ssm_log_dt, ssm_b_re, ssm_b_im)
    tab_f, tab_b = _scan_tables(lam_r, lam_i)
    pk = {"br": _pack_in(bbar_r), "bi": _pack_in(bbar_i), "cr": _pack_out(ssm_c_re), "ci": _pack_out(ssm_c_im)}
    packs = {k: v.astype(BF16) for k, v in pk.items()}
    packs.update({"brt": jnp.swapaxes(packs["br"], 1, 2), "bit": jnp.swapaxes(packs["bi"], 1, 2),
                  "crt": jnp.swapaxes(packs["cr"], 1, 2), "cit": jnp.swapaxes(packs["ci"], 1, 2)})
    return packs, tab_f, tab_b


def _local_step(xs, target, mod, w_in_t, later_weights, ffn_grads_ready, norm_mix_g, attn_sinks, ssm, norm_ffn_g, conv_w_full,
                ffn_conv_b, final_g, aw, sw, ff):
    l, d = xs.shape
    u_off = aw + 2 * KV_WIDTH
    ga_off = u_off + sw
    gs_off = ga_off + d
    packs, tab_f, tab_b = _ssm_setup(*ssm[:7])
    dvec = ssm[7].reshape(1, sw)

    h1 = _norm_mod(xs, norm_mix_g, mod, 1, 0, "norm_mod1")
    proj = _matmul(h1, w_in_t, "nt", "mm_in")
    attn = _attn_fwd(proj, attn_sinks, aw, "attn_fwd")
    ys = _s5_fwd(proj, u_off, packs, dvec, tab_f, sw, "s5_fwd")
    gy = _gelu(ys, "gelu")
    w_ap_t, w_glu_t, w_out_f, w_up_t, w_down_f = later_weights((gy, attn))
    attn_out = _matmul(attn, w_ap_t, "nt", "mm_attn_proj")
    glu = _matmul(gy, w_glu_t, "nt", "mm_glu")
    mixed = _mix(proj, ga_off, gs_off, attn_out, glu, d, "mix")
    mo = _matmul(mixed, w_out_f, "nn", "mm_out", out_dtype=F32)
    x2, h2 = _resid_norm_mod(xs, mo, norm_ffn_g, mod, 2, 4, 3, "resid_norm_mod2")
    cw = ff // CONV_BLOCKS
    up = _matmul(h2, w_up_t, "nt", "mm_up", interleave=cw)
    act = _conv_act(up, conv_w_full, ffn_conv_b, ff, cw, "conv_act")
    fo = _matmul(act, w_down_f, "nn", "mm_down", out_dtype=F32)
    loss_part, d_final_g, d_gate2, dx3, dfo = _final_loss(x2, fo, mod, 5, final_g.reshape(1, d), target, "final_loss")

    dact = _matmul(dfo, w_down_f, "nt", "mm_down_dx")
    g_down = _matmul(act, dfo, "tn", "mm_down_dw")
    dup, dcw_parts, dcb_parts = _conv_act_bwd(up, dact, conv_w_full, ffn_conv_b, ff, cw, "conv_act_bwd")
    d_conv_w = _sum_rows8(dcw_parts, "sum_conv_w")[:3]
    d_conv_b = _sum_rows8(dcb_parts, "sum_conv_b")[:1]
    dh2 = _matmul(dup, w_up_t, "nn", "mm_up_dx", interleave=cw)
    g_up = _matmul(dup, h2, "tn", "mm_up_dw", interleave=cw)
    mod = ffn_grads_ready(g_up, g_down, mod)
    dx2, d_shift2, d_scale2, d_gain2, dmo, d_gate1 = _norm_mod_bwd(dh2, x2, dx3, norm_ffn_g, mod, 4, "norm_mod2_bwd", branch=mo, gate_col=2)
    dmixed = _matmul(dmo, w_out_f, "nt", "mm_out_dx")
    g_out = _matmul(mixed, dmo, "tn", "mm_out_dw")
    dga, dgs, dattn_out, dglu = _mix_bwd(proj, ga_off, gs_off, attn_out, glu, dmixed, d, "mix_bwd")
    dgy = _matmul(dglu, w_glu_t, "nn", "mm_glu_dx")
    g_glu = _matmul(dglu, gy, "tn", "mm_glu_dw")
    dys = _gelu_bwd(ys, dgy, "gelu_bwd")
    du, dlam, dbr_p, dbi_p, dcr_p, dci_p, d_dvec = _s5_bwd(proj, u_off, dys, packs, dvec, tab_f, tab_b, sw, "s5_bwd")
    dattn = _matmul(dattn_out, w_ap_t, "nn", "mm_attn_proj_dx")
    g_ap = _matmul(dattn_out, attn, "tn", "mm_attn_proj_dw")
    dq, dkv_cur, dkv_prev, d_sinks = _attn_bwd(proj, attn_sinks, dattn, aw, "attn_bwd")
    dkv = dkv_cur + jnp.concatenate([dkv_prev[ATTN_BLOCK:], jnp.zeros((ATTN_BLOCK, 2 * KV_WIDTH), F32)], axis=0)
    dproj = jnp.concatenate([dq, dkv.astype(BF16), du, dga, dgs], axis=1)
    dh1 = _matmul(dproj, w_in_t, "nn", "mm_in_dx")
    g_in = _matmul(dproj, h1, "tn", "mm_in_dw")
    grad_x, d_shift1, d_scale1, d_gain1 = _norm_mod_bwd(dh1, xs, dx2, norm_mix_g, mod, 1, "norm_mod1_bwd")

    dmod = jnp.concatenate([d_shift1, d_scale1, d_gate1, d_shift2, d_scale2, d_gate2], axis=1)
    small_parts = [dmod, d_gain1, d_sinks, dlam[0], dlam[1], _unpack_diag(dbr_p, SSM_STATE, SSM_GROUP),
                   _unpack_diag(dbi_p, SSM_STATE, SSM_GROUP), _unpack_diag(dcr_p, SSM_STATE, SSM_GROUP),
                   _unpack_diag(dci_p, SSM_STATE, SSM_GROUP), d_dvec, d_gain2, d_conv_b, d_conv_w, d_final_g]
    return loss_part, grad_x, [g_in, g_ap, g_glu, g_out], small_parts


def _kernel_impl(x, c, ada_w, ada_b, norm_mix_g, w_in, attn_sinks, w_attn_proj, ssm_a_re, ssm_a_im, ssm_log_dt, ssm_b_re, ssm_b_im,
                 ssm_c_re, ssm_c_im, ssm_d, w_ssm_glu, w_out, norm_ffn_g, w_ffn_up, ffn_conv_w, ffn_conv_b, w_ffn_down, final_g,
                 loss_target, ms, vs):
    ax, ay, ac = lax.axis_index("x"), lax.axis_index("y"), lax.axis_index("c")
    chip = 2 * ax + ay
    batch_row = 4 * ax + 2 * ay + ac
    d = x.shape[2]
    aw = w_attn_proj.shape[1]
    sw = w_ssm_glu.shape[1]
    ff = N_CHIPS * ffn_conv_w.shape[2]
    ngroups = sw // SSM_GROUP

    c_all = _all_gather8(jnp.pad(c, ((0, 7), (0, 0))), "gather_c").reshape(N_DEV, 8, d)[:, 0, :]
    ncol = ada_w.shape[2]
    b_shard = lax.dynamic_slice(ada_b, (0, chip * ncol), (1, ncol))
    mod_blk = _ada_fwd(c_all, ada_w[0], b_shard, "ada_fwd")
    mod_all = _all_gather8(mod_blk, "gather_mod").reshape(N_CHIPS, 2, 8, ncol)[:, 0]
    mod = lax.dynamic_slice(mod_all, (0, batch_row, 0), (N_CHIPS, 1, ncol)).reshape(1, 6 * d)

    shards = [w_in[0].T.astype(BF16), w_attn_proj[0].T.astype(BF16), w_ssm_glu[0].T.astype(BF16), w_out[0],
              w_ffn_up[0].T.astype(BF16), w_ffn_down[0]]
    conv_w_all = _all_gather8(jnp.pad(ffn_conv_w[0], ((0, 5), (0, 0))), "gather_conv_w")
    conv_w_full = conv_w_all.reshape(N_CHIPS, 2, 8, ff // N_CHIPS)[:, 0, :3].transpose(1, 0, 2).reshape(3, ff)
    placed = [_place_shard(s, f"place_shard_{i}", after=(mod, conv_w_full) if i == 0 else ()) for i, s in enumerate(shards)]
    (first, later), started = _gather_start([placed[:1], placed[1:]], "gather_start")
    (w_in_t,) = _forward_then_finish(first, (started,), "w_in")
    mod = mod + started[0:1, 0:1]

    def later_weights(after):
        return _forward_then_finish(later, after, "later")

    pending = {}

    def ffn_grads_ready(g_up, g_down, mod_now):
        pending["ffn"], token = _reduce_start([g_up, g_down], "ffn")
        return mod_now + token[0:1, 0:1]

    ssm = (ssm_a_re[0], ssm_a_im[0], ssm_log_dt[0], ssm_b_re[0], ssm_b_im[0], ssm_c_re[0], ssm_c_im[0], ssm_d[0])
    loss_part, grad_x, grads, small_parts = _local_step(
        x[0], loss_target[0], mod, w_in_t, later_weights, ffn_grads_ready, norm_mix_g, attn_sinks, ssm, norm_ffn_g, conv_w_full,
        ffn_conv_b, final_g, aw, sw, ff)
    loss = lax.psum(loss_part[0, 0], ("x", "y", "c"))

    small_shapes = [p.shape for p in small_parts]
    part_buf = _flatten_pad(small_parts)
    rows = part_buf.shape[0]
    gathered = _all_gather8(part_buf, "gather_small").reshape(N_DEV, rows, SMALL_COLS)
    pending["rest"], rest_token = _reduce_start(grads, "rest", after=(gathered,))
    gup_t, grad_w_down = _reduce_finish(pending["ffn"], (rest_token,), "ffn")
    grad_w_up = gup_t.T
    summed = _sum_devices(gathered, "sum_small")
    (s_dmod, s_gain1, s_sinks, s_lr, s_li, s_bbr, s_bbi, s_cr, s_ci, s_dd, s_gain2, s_cb, s_cw, s_fg) = _split_flat(summed, small_shapes)
    _, ssm_vjp = jax.vjp(_ssm_discretize, *ssm[:5])
    g_a_re, g_a_im, g_log_dt, g_b_re, g_b_im = ssm_vjp((s_lr.reshape(ngroups, SSM_STATE), s_li.reshape(ngroups, SSM_STATE), s_bbr, s_bbi))
    g_c_re, g_c_im = jnp.swapaxes(s_cr, 1, 2), jnp.swapaxes(s_ci, 1, 2)
    g_conv_w = lax.dynamic_slice(s_cw, (0, chip * (ff // N_CHIPS)), (3, ff // N_CHIPS))

    dmod_all = gathered.reshape(N_DEV, -1)[:, :6 * d]
    dmod_shard = lax.dynamic_slice(dmod_all, (0, chip * ncol), (N_DEV, ncol))
    ada_res = _ada_bwd_adam(c_all.T, dmod_shard, ada_w[0], ms["ada_w"][0], vs["ada_w"][0], "ada_bwd_adam")

    res = {"ada_w": tuple(o[None] for o in ada_res)}

    def adam_big(nm, w, g):
        res[nm] = (g[None],) + tuple(o[None] for o in _adam(w[0], g, ms[nm][0], vs[nm][0], "adam_" + nm))

    adam_big("w_ffn_up", w_ffn_up, grad_w_up)
    adam_big("w_ffn_down", w_ffn_down, grad_w_down)

    small = [("ada_b", ada_b, s_dmod), ("norm_mix_g", norm_mix_g, s_gain1), ("attn_sinks", attn_sinks, s_sinks),
             ("ssm_a_re", ssm_a_re, g_a_re), ("ssm_a_im", ssm_a_im, g_a_im), ("ssm_log_dt", ssm_log_dt, g_log_dt),
             ("ssm_b_re", ssm_b_re, g_b_re), ("ssm_b_im", ssm_b_im, g_b_im), ("ssm_c_re", ssm_c_re, g_c_re),
             ("ssm_c_im", ssm_c_im, g_c_im), ("ssm_d", ssm_d, s_dd), ("norm_ffn_g", norm_ffn_g, s_gain2),
             ("ffn_conv_w", ffn_conv_w, g_conv_w), ("ffn_conv_b", ffn_conv_b, s_cb), ("final_g", final_g, s_fg)]
    shapes = [t[1].shape for t in small]
    bufs = [_flatten_pad([t[1] for t in small]), _flatten_pad([t[2] for t in small]),
            _flatten_pad([ms[t[0]] for t in small]), _flatten_pad([vs[t[0]] for t in small])]
    s_delta, s_m, s_v = _adam(*bufs, "adam_small")
    for t, dl, m2, v2 in zip(small, _split_flat(s_delta, shapes), _split_flat(s_m, shapes), _split_flat(s_v, shapes)):
        res[t[0]] = (t[2].reshape(t[1].shape), dl, m2, v2)

    done = (s_delta, res["w_ffn_up"][1], res["w_ffn_down"][1], res["ada_w"][1])
    gi_t, gap_t, gglu_t, grad_w_out = _reduce_finish(pending["rest"], done, "rest")
    adam_big("w_in", w_in, gi_t.T)
    adam_big("w_attn_proj", w_attn_proj, gap_t.T)
    adam_big("w_ssm_glu", w_ssm_glu, gglu_t.T)
    adam_big("w_out", w_out, grad_w_out)

    outs = [loss, grad_x[None]]
    for i in range(4):
        outs += [res[nm][i] for nm in WEIGHT_ORDER]
    return tuple(outs)


WEIGHT_ORDER = ("ada_w", "ada_b", "norm_mix_g", "w_in", "attn_sinks", "w_attn_proj", "ssm_a_re", "ssm_a_im", "ssm_log_dt", "ssm_b_re",
                "ssm_b_im", "ssm_c_re", "ssm_c_im", "ssm_d", "w_ssm_glu", "w_out", "norm_ffn_g", "w_ffn_up", "ffn_conv_w", "ffn_conv_b",
                "w_ffn_down", "final_g")


def kernel(x, c, ada_w, ada_b, norm_mix_g, w_in, attn_sinks, w_attn_proj, ssm_a_re, ssm_a_im, ssm_log_dt, ssm_b_re, ssm_b_im, ssm_c_re, ssm_c_im, ssm_d, w_ssm_glu, w_out, norm_ffn_g, w_ffn_up, ffn_conv_w, ffn_conv_b, w_ffn_down, final_g, loss_target, m_ada_w, m_ada_b, m_norm_mix_g, m_w_in, m_attn_sinks, m_w_attn_proj, m_ssm_a_re, m_ssm_a_im, m_ssm_log_dt, m_ssm_b_re, m_ssm_b_im, m_ssm_c_re, m_ssm_c_im, m_ssm_d, m_w_ssm_glu, m_w_out, m_norm_ffn_g, m_w_ffn_up, m_ffn_conv_w, m_ffn_conv_b, m_w_ffn_down, m_final_g, v_ada_w, v_ada_b, v_norm_mix_g, v_w_in, v_attn_sinks, v_w_attn_proj, v_ssm_a_re, v_ssm_a_im, v_ssm_log_dt, v_ssm_b_re, v_ssm_b_im, v_ssm_c_re, v_ssm_c_im, v_ssm_d, v_w_ssm_glu, v_w_out, v_norm_ffn_g, v_w_ffn_up, v_ffn_conv_w, v_ffn_conv_b, v_w_ffn_down, v_final_g):
    ms = dict(zip(WEIGHT_ORDER, (m_ada_w, m_ada_b, m_norm_mix_g, m_w_in, m_attn_sinks, m_w_attn_proj, m_ssm_a_re, m_ssm_a_im, m_ssm_log_dt,
                                 m_ssm_b_re, m_ssm_b_im, m_ssm_c_re, m_ssm_c_im, m_ssm_d, m_w_ssm_glu, m_w_out, m_norm_ffn_g, m_w_ffn_up,
                                 m_ffn_conv_w, m_ffn_conv_b, m_w_ffn_down, m_final_g)))
    vs = dict(zip(WEIGHT_ORDER, (v_ada_w, v_ada_b, v_norm_mix_g, v_w_in, v_attn_sinks, v_w_attn_proj, v_ssm_a_re, v_ssm_a_im, v_ssm_log_dt,
                                 v_ssm_b_re, v_ssm_b_im, v_ssm_c_re, v_ssm_c_im, v_ssm_d, v_w_ssm_glu, v_w_out, v_norm_ffn_g, v_w_ffn_up,
                                 v_ffn_conv_w, v_ffn_conv_b, v_w_ffn_down, v_final_g)))
    return _kernel_impl(x, c, ada_w, ada_b, norm_mix_g, w_in, attn_sinks, w_attn_proj, ssm_a_re, ssm_a_im, ssm_log_dt, ssm_b_re, ssm_b_im,
                        ssm_c_re, ssm_c_im, ssm_d, w_ssm_glu, w_out, norm_ffn_g, w_ffn_up, ffn_conv_w, ffn_conv_b, w_ffn_down, final_g,
                        loss_target, ms, vs)
```

```python
import math

import jax
import jax.numpy as jnp
from jax import lax
from jax.experimental import pallas as pl
from jax.experimental.pallas import tpu as pltpu

F32 = jnp.float32
BF16 = jnp.bfloat16
MESH = pl.DeviceIdType.MESH

HEAD_DIM = 64
N_KV_HEADS = 2
KV_WIDTH = N_KV_HEADS * HEAD_DIM
ATTN_BLOCK = 128
NEG_INF = -1e30
SSM_GROUP = 16
SSM_STATE = 64
GROUPS_PER_TILE = 8
RMS_EPS = 1e-6
ADAM_LR = 0.001
ADAM_B1 = 0.9
ADAM_B2 = 0.999
ADAM_EPS = 1e-08
ADAM_WD = 0.01
ADAM_STEP = 10
N_CHIPS = 4
N_DEV = 8
VMEM_LIMIT_BYTES = 56 * 1024 * 1024
LANES = 128
SMALL_COLS = 1024


def _tile(dim, target, mult=LANES):
    if dim <= target:
        return dim
    for t in range(target // mult * mult, 0, -mult):
        if dim % t == 0:
            return t
    raise ValueError(f"no tile for {dim}")


def _params(sem=None):
    return pltpu.CompilerParams(dimension_semantics=sem, vmem_limit_bytes=VMEM_LIMIT_BYTES)


def _sigmoid(x):
    return 1.0 / (1.0 + jnp.exp(-x))


def _matmul(a, b, mode, name, out_dtype=BF16, tm=1536, tn=1536, tk=2048, interleave=None):
    if mode == "nn":
        (m, k), (k2, n) = a.shape, b.shape
    elif mode == "nt":
        (m, k), (n, k2) = a.shape, b.shape
    else:
        (k, m), (k2, n) = a.shape, b.shape
    assert k == k2, (a.shape, b.shape, mode)
    if interleave is not None:
        tn, tk, tm = (interleave, tk, tm) if mode == "nt" else (tn, interleave, tm) if mode == "nn" else (tn, tk, interleave)
        half = {"nt": n, "nn": k, "tn": m}[mode] // (2 * interleave)

        def perm(blk):
            return blk // 2 + (blk % 2) * half
    else:
        def perm(blk):
            return blk
    tm, tn, tk = _tile(m, tm), _tile(n, tn), _tile(k, tk)
    nk = k // tk
    if mode == "tn":
        a_spec = pl.BlockSpec((tk, tm), lambda i, j, kk: (kk, i))
    else:
        a_spec = pl.BlockSpec((tm, tk), lambda i, j, kk: (i, kk))
    if mode == "nt":
        b_spec = pl.BlockSpec((tn, tk), lambda i, j, kk: (perm(j), kk))
    elif mode == "nn":
        b_spec = pl.BlockSpec((tk, tn), lambda i, j, kk: (perm(kk), j))
    else:
        b_spec = pl.BlockSpec((tk, tn), lambda i, j, kk: (kk, j))
    out_rows = perm if mode == "tn" else (lambda blk: blk)
    dims = {"nn": (((1,), (0,)), ((), ())), "nt": (((1,), (1,)), ((), ())), "tn": (((0,), (0,)), ((), ()))}[mode]

    def body(a_ref, b_ref, o_ref, acc_ref):
        kk = pl.program_id(2)

        @pl.when(kk == 0)
        def _():
            acc_ref[...] = jnp.zeros_like(acc_ref)

        acc_ref[...] += lax.dot_general(a_ref[...], b_ref[...], dims, preferred_element_type=F32)

        @pl.when(kk == nk - 1)
        def _():
            o_ref[...] = acc_ref[...].astype(o_ref.dtype)

    return pl.pallas_call(
        body,
        out_shape=jax.ShapeDtypeStruct((m, n), out_dtype),
        grid=(m // tm, n // tn, nk),
        in_specs=[a_spec, b_spec],
        out_specs=pl.BlockSpec((tm, tn), lambda i, j, kk: (out_rows(i), j)),
        scratch_shapes=[pltpu.VMEM((tm, tn), F32)],
        name=name,
        compiler_params=_params(("parallel", "parallel", "arbitrary")),
    )(a, b)


def _row_spec(tl, w, col=0):
    return pl.BlockSpec((tl, w), lambda i, col=col: (i, col))


def _vec_spec(w, col=0):
    return pl.BlockSpec((1, w), lambda i, col=col: (0, col))


def _norm_mod(x, gain, mod, sc_col, sh_col, name):
    l, d = x.shape
    tl = _tile(l, 256, 8)

    def body(x_ref, g_ref, sc_ref, sh_ref, h_ref):
        xv = x_ref[...]
        r = lax.rsqrt(jnp.mean(xv * xv, axis=-1, keepdims=True) + RMS_EPS)
        h_ref[...] = ((xv * r) * g_ref[...] * (1.0 + sc_ref[...]) + sh_ref[...]).astype(BF16)

    return pl.pallas_call(
        body,
        out_shape=jax.ShapeDtypeStruct((l, d), BF16),
        grid=(l // tl,),
        in_specs=[_row_spec(tl, d), _vec_spec(d), _vec_spec(d, sc_col), _vec_spec(d, sh_col)],
        out_specs=_row_spec(tl, d),
        name=name,
        compiler_params=_params(("parallel",)),
    )(x, gain, mod, mod)


def _resid_norm_mod(x, mo, gain, mod, gate_col, sc_col, sh_col, name):
    l, d = x.shape
    tl = _tile(l, 256, 8)

    def body(x_ref, mo_ref, g_ref, gate_ref, sc_ref, sh_ref, x2_ref, h_ref):
        xv = x_ref[...] + gate_ref[...] * mo_ref[...]
        x2_ref[...] = xv
        r = lax.rsqrt(jnp.mean(xv * xv, axis=-1, keepdims=True) + RMS_EPS)
        h_ref[...] = ((xv * r) * g_ref[...] * (1.0 + sc_ref[...]) + sh_ref[...]).astype(BF16)

    return pl.pallas_call(
        body,
        out_shape=(jax.ShapeDtypeStruct((l, d), F32), jax.ShapeDtypeStruct((l, d), BF16)),
        grid=(l // tl,),
        in_specs=[_row_spec(tl, d), _row_spec(tl, d), _vec_spec(d), _vec_spec(d, gate_col), _vec_spec(d, sc_col),
                  _vec_spec(d, sh_col)],
        out_specs=(_row_spec(tl, d), _row_spec(tl, d)),
        name=name,
        compiler_params=_params(("parallel",)),
    )(x, mo, gain, mod, mod, mod)


def _final_loss(x2, f, mod, gate_col, final_g, target, name):
    l, d = x2.shape
    tl = _tile(l, 256, 8)

    def body(x2_ref, f_ref, gate_ref, fg_ref, t_ref, loss_ref, dfg_ref, dgate_ref, dx3_ref, df_ref):
        i = pl.program_id(0)
        fv = f_ref[...]
        x3 = x2_ref[...] + gate_ref[...] * fv
        r = lax.rsqrt(jnp.mean(x3 * x3, axis=-1, keepdims=True) + RMS_EPS)
        xh = x3 * r
        err = xh * fg_ref[...] - t_ref[...]
        part = 0.5 * jnp.sum(jnp.mean(err * err, axis=-1, keepdims=True), axis=0, keepdims=True)
        dout = err * (1.0 / d)
        dxh = dout * fg_ref[...]
        dx3 = r * (dxh - xh * jnp.mean(dxh * xh, axis=-1, keepdims=True))
        dx3_ref[...] = dx3
        df_ref[...] = (gate_ref[...] * dx3).astype(BF16)

        @pl.when(i == 0)
        def _():
            loss_ref[...] = jnp.zeros_like(loss_ref)
            dfg_ref[...] = jnp.zeros_like(dfg_ref)
            dgate_ref[...] = jnp.zeros_like(dgate_ref)

        loss_ref[...] += jnp.broadcast_to(part, loss_ref.shape)
        dfg_ref[...] += jnp.sum(dout * xh, axis=0, keepdims=True)
        dgate_ref[...] += jnp.sum(dx3 * fv, axis=0, keepdims=True)

    vec = pl.BlockSpec((1, d), lambda i: (0, 0))
    return pl.pallas_call(
        body,
        out_shape=(jax.ShapeDtypeStruct((1, LANES), F32), jax.ShapeDtypeStruct((1, d), F32),
                   jax.ShapeDtypeStruct((1, d), F32), jax.ShapeDtypeStruct((l, d), F32),
                   jax.ShapeDtypeStruct((l, d), BF16)),
        grid=(l // tl,),
        in_specs=[_row_spec(tl, d), _row_spec(tl, d), _vec_spec(d, gate_col), vec, _row_spec(tl, d)],
        out_specs=(pl.BlockSpec((1, LANES), lambda i: (0, 0)), vec, vec, _row_spec(tl, d), _row_spec(tl, d)),
        name=name,
        compiler_params=_params(("arbitrary",)),
    )(x2, f, mod, final_g, target)


def _norm_mod_bwd(dh, x, dx_res, gain, mod, sc_col, name, branch=None, gate_col=None):
    l, d = x.shape
    tl = _tile(l, 256, 8)
    with_gate = branch is not None

    def body(*refs):
        if with_gate:
            dh_ref, x_ref, dr_ref, g_ref, sc_ref, br_ref, gate_ref, dx_ref, dsh_ref, dsc_ref, dg_ref, dm_ref, dgate_ref = refs
        else:
            dh_ref, x_ref, dr_ref, g_ref, sc_ref, dx_ref, dsh_ref, dsc_ref, dg_ref = refs
        i = pl.program_id(0)
        xv = x_ref[...]
        dhv = dh_ref[...].astype(F32)
        r = lax.rsqrt(jnp.mean(xv * xv, axis=-1, keepdims=True) + RMS_EPS)
        xh = xv * r
        dn = dhv * (1.0 + sc_ref[...])
        dxh = dn * g_ref[...]
        dx = dr_ref[...] + r * (dxh - xh * jnp.mean(dxh * xh, axis=-1, keepdims=True))
        dx_ref[...] = dx

        @pl.when(i == 0)
        def _():
            dsh_ref[...] = jnp.zeros_like(dsh_ref)
            dsc_ref[...] = jnp.zeros_like(dsc_ref)
            dg_ref[...] = jnp.zeros_like(dg_ref)
            if with_gate:
                dgate_ref[...] = jnp.zeros_like(dgate_ref)

        dsh_ref[...] += jnp.sum(dhv, axis=0, keepdims=True)
        dsc_ref[...] += jnp.sum(dhv * (xh * g_ref[...]), axis=0, keepdims=True)
        dg_ref[...] += jnp.sum(dn * xh, axis=0, keepdims=True)
        if with_gate:
            dm_ref[...] = (gate_ref[...] * dx).astype(BF16)
            dgate_ref[...] += jnp.sum(dx * br_ref[...], axis=0, keepdims=True)

    vec = pl.BlockSpec((1, d), lambda i: (0, 0))
    in_specs = [_row_spec(tl, d), _row_spec(tl, d), _row_spec(tl, d), vec, _vec_spec(d, sc_col)]
    args = [dh, x, dx_res, gain, mod]
    out_shape = [jax.ShapeDtypeStruct((l, d), F32)] + [jax.ShapeDtypeStruct((1, d), F32)] * 3
    out_specs = [_row_spec(tl, d), vec, vec, vec]
    if with_gate:
        in_specs += [_row_spec(tl, d), _vec_spec(d, gate_col)]
        args += [branch, mod]
        out_shape += [jax.ShapeDtypeStruct((l, d), BF16), jax.ShapeDtypeStruct((1, d), F32)]
        out_specs += [_row_spec(tl, d), vec]
    return pl.pallas_call(
        body, out_shape=tuple(out_shape), grid=(l // tl,), in_specs=in_specs, out_specs=tuple(out_specs),
        name=name, compiler_params=_params(("arbitrary",)),
    )(*args)


def _attn_mask(n, rows):
    del rows
    qi = lax.broadcasted_iota(jnp.int32, (ATTN_BLOCK, 2 * ATTN_BLOCK), 0)
    kj = lax.broadcasted_iota(jnp.int32, (ATTN_BLOCK, 2 * ATTN_BLOCK), 1)
    rel = qi + ATTN_BLOCK - kj
    return jnp.where((rel >= 0) & (rel < ATTN_BLOCK) & ((kj >= ATTN_BLOCK) | (n > 0)), 0.0, NEG_INF)


def _attn_probs(qs, kh, sink, mask):
    rows = qs.shape[0]
    s = lax.dot_general(qs, kh, (((1,), (1,)), ((), ())), preferred_element_type=F32) * (HEAD_DIM ** -0.5)
    s = s.reshape(-1, ATTN_BLOCK, 2 * ATTN_BLOCK) + mask[None]
    m = jnp.maximum(jnp.max(s, axis=-1, keepdims=True), sink)
    p = jnp.exp(s - m)
    es = jnp.exp(sink - m)
    inv = 1.0 / (jnp.sum(p, axis=-1, keepdims=True) + es)
    return (p * inv).reshape(rows, 2 * ATTN_BLOCK), (es * inv).reshape(rows, 1)


def _stack_heads(src_ref, dst_ref, g, qpk):
    for i in range(qpk):
        h = g * qpk + i
        dst_ref[i * ATTN_BLOCK:(i + 1) * ATTN_BLOCK, :] = src_ref[:, h * HEAD_DIM:(h + 1) * HEAD_DIM]


def _unstack_heads(val, dst_ref, g, qpk):
    for i in range(qpk):
        h = g * qpk + i
        dst_ref[:, h * HEAD_DIM:(h + 1) * HEAD_DIM] = val[i * ATTN_BLOCK:(i + 1) * ATTN_BLOCK, :].astype(dst_ref.dtype)


def _sink_column(sinks):
    return sinks.reshape(-1, 1, 1)


def _sink_spec(nq):
    return pl.BlockSpec((nq, 1, 1), lambda n: (0, 0, 0))


def _attn_specs(aw):
    kvb = aw // (2 * KV_WIDTH)
    q_spec = pl.BlockSpec((ATTN_BLOCK, aw), lambda n: (n, 0))
    kv_cur = pl.BlockSpec((ATTN_BLOCK, 2 * KV_WIDTH), lambda n: (n, kvb))
    kv_prev = pl.BlockSpec((ATTN_BLOCK, 2 * KV_WIDTH), lambda n: (jnp.maximum(n - 1, 0), kvb))
    return q_spec, kv_cur, kv_prev


def _attn_fwd(proj, sinks, aw, name):
    l = proj.shape[0]
    nq = aw // HEAD_DIM
    qpk = nq // N_KV_HEADS
    assert aw % (2 * KV_WIDTH) == 0

    rows = qpk * ATTN_BLOCK

    def body(q_ref, kvc_ref, kvp_ref, sink_ref, o_ref):
        n = pl.program_id(0)
        valid = _attn_mask(n, rows) == 0.0
        kv = jnp.concatenate([kvp_ref[...], kvc_ref[...]], axis=0)
        for h in range(nq):
            g = h // qpk
            qh = q_ref[:, h * HEAD_DIM:(h + 1) * HEAD_DIM]
            kh = kv[:, g * HEAD_DIM:(g + 1) * HEAD_DIM]
            vh = kv[:, KV_WIDTH + g * HEAD_DIM:KV_WIDTH + (g + 1) * HEAD_DIM]
            sink = sink_ref[0:1, h:h + 1]
            s = lax.dot_general(qh, kh, (((1,), (1,)), ((), ())), preferred_element_type=F32) * (HEAD_DIM ** -0.5)
            s = jnp.where(valid, s, NEG_INF)
            m = jnp.maximum(jnp.max(s, axis=-1, keepdims=True), sink)
            p = jnp.exp(s - m)
            p = p * (1.0 / (jnp.sum(p, axis=-1, keepdims=True) + jnp.exp(sink - m)))
            o = jnp.dot(p.astype(BF16), vh, preferred_element_type=F32)
            o_ref[:, h * HEAD_DIM:(h + 1) * HEAD_DIM] = o.astype(BF16)

    q_spec, kv_cur, kv_prev = _attn_specs(aw)
    return pl.pallas_call(
        body,
        out_shape=jax.ShapeDtypeStruct((l, aw), BF16),
        grid=(l // ATTN_BLOCK,),
        in_specs=[q_spec, kv_cur, kv_prev, pl.BlockSpec((1, nq), lambda n: (0, 0))],
        out_specs=pl.BlockSpec((ATTN_BLOCK, aw), lambda n: (n, 0)),
        name=name,
        compiler_params=_params(("parallel",)),
    )(proj, proj, proj, sinks)


def _attn_bwd(proj, sinks, dattn, aw, name):
    l = proj.shape[0]
    nq = aw // HEAD_DIM
    qpk = nq // N_KV_HEADS
    scale = HEAD_DIM ** -0.5

    rows = qpk * ATTN_BLOCK
    tn_dims = (((0,), (0,)), ((), ()))

    def body(q_ref, kvc_ref, kvp_ref, sink_ref, do_ref, dq_ref, dcur_ref, dprev_ref, dsink_ref, q_scr, do_scr):
        n = pl.program_id(0)
        mask = _attn_mask(n, rows)
        kv = jnp.concatenate([kvp_ref[...], kvc_ref[...]], axis=0)
        lane = lax.broadcasted_iota(jnp.int32, (1, nq), 1)
        dsink = jnp.zeros((1, nq), F32)
        dks, dvs = [], []
        for g in range(N_KV_HEADS):
            kh = kv[:, g * HEAD_DIM:(g + 1) * HEAD_DIM]
            vh = kv[:, KV_WIDTH + g * HEAD_DIM:KV_WIDTH + (g + 1) * HEAD_DIM]
            _stack_heads(q_ref, q_scr.at[g], g, qpk)
            _stack_heads(do_ref, do_scr.at[g], g, qpk)
            qs, dos = q_scr[g], do_scr[g]
            p, ps = _attn_probs(qs, kh, sink_ref[g * qpk:(g + 1) * qpk], mask)
            pb = p.astype(BF16)
            o = jnp.dot(pb, vh, preferred_element_type=F32)
            delta = jnp.sum(dos.astype(F32) * o, axis=-1, keepdims=True)
            dp = lax.dot_general(dos, vh, (((1,), (1,)), ((), ())), preferred_element_type=F32)
            ds = (p * (dp - delta)).astype(BF16)
            _unstack_heads(jnp.dot(ds, kh, preferred_element_type=F32) * scale, dq_ref, g, qpk)
            dks.append(lax.dot_general(ds, qs, tn_dims, preferred_element_type=F32) * scale)
            dvs.append(lax.dot_general(pb, dos, tn_dims, preferred_element_type=F32))
            t = ps * delta
            for i in range(qpk):
                part = -jnp.sum(t[i * ATTN_BLOCK:(i + 1) * ATTN_BLOCK, :], axis=0, keepdims=True)
                dsink += jnp.where(lane == g * qpk + i, part, 0.0)
        dkv = jnp.concatenate(dks + dvs, axis=1)
        dprev_ref[...] = dkv[:ATTN_BLOCK]
        dcur_ref[...] = dkv[ATTN_BLOCK:]

        @pl.when(n == 0)
        def _():
            dsink_ref[...] = jnp.zeros_like(dsink_ref)

        dsink_ref[...] += dsink

    q_spec, kv_cur, kv_prev = _attn_specs(aw)
    blk = pl.BlockSpec((ATTN_BLOCK, 2 * KV_WIDTH), lambda n: (n, 0))
    return pl.pallas_call(
        body,
        out_shape=(jax.ShapeDtypeStruct((l, aw), BF16), jax.ShapeDtypeStruct((l, 2 * KV_WIDTH), F32),
                   jax.ShapeDtypeStruct((l, 2 * KV_WIDTH), F32), jax.ShapeDtypeStruct((1, nq), F32)),
        grid=(l // ATTN_BLOCK,),
        in_specs=[q_spec, kv_cur, kv_prev, _sink_spec(nq),
                  pl.BlockSpec((ATTN_BLOCK, aw), lambda n: (n, 0))],
        out_specs=(pl.BlockSpec((ATTN_BLOCK, aw), lambda n: (n, 0)), blk, blk, pl.BlockSpec((1, nq), lambda n: (0, 0))),
        scratch_shapes=[pltpu.VMEM((N_KV_HEADS, rows, HEAD_DIM), BF16)] * 2,
        name=name,
        compiler_params=_params(("arbitrary",)),
    )(proj, proj, proj, _sink_column(sinks), dattn)


def _ssm_discretize(a_re, a_im, log_dt, b_re, b_im):
    dt = jnp.exp(log_dt)[:, None]
    mag = jnp.exp(a_re * dt)
    lr, li = mag * jnp.cos(a_im * dt), mag * jnp.sin(a_im * dt)
    den = a_re * a_re + a_im * a_im
    zr = ((lr - 1.0) * a_re + li * a_im) / den
    zi = (li * a_re - (lr - 1.0) * a_im) / den
    bbar_r = zr[:, :, None] * b_re - zi[:, :, None] * b_im
    bbar_i = zr[:, :, None] * b_im + zi[:, :, None] * b_re
    return lr, li, bbar_r, bbar_i


def _cmul(ar, ai, br, bi):
    return ar * br - ai * bi, ar * bi + ai * br


N_SEG = 8


def _cpow(ar, ai, n):
    out, br, bi = None, ar, ai
    while n:
        if n & 1:
            out = (br, bi) if out is None else _cmul(*out, br, bi)
        br, bi = _cmul(br, bi, br, bi)
        n >>= 1
    return out


def _scan_tables(lr, li, seg):
    lr, li = lr.reshape(1, -1), li.reshape(1, -1)
    row = jnp.arange(N_SEG)[:, None]
    ones = jnp.ones((N_SEG, 1), F32)
    fwd, bwd = [], []
    for d in (1, 2, 4):
        pr, pi = _cpow(lr, li, seg * d)
        fwd += [jnp.where(row >= d, pr, 0.0), jnp.where(row >= d, pi, 0.0)]
        bwd += [jnp.where(row < N_SEG - d, pr, 0.0), jnp.where(row < N_SEG - d, -pi, 0.0)]
    fwd += [ones * lr, ones * li]
    bwd += [ones * lr, ones * -li]
    return jnp.concatenate(fwd, 0), jnp.concatenate(bwd, 0)


def _pack_in(b):
    g, n, p = b.shape
    t = g // GROUPS_PER_TILE
    eye = jnp.eye(GROUPS_PER_TILE, dtype=b.dtype)
    bb = b.reshape(t, GROUPS_PER_TILE, n, p)
    return jnp.einsum("tgnp,gh->tgphn", bb, eye).reshape(t, GROUPS_PER_TILE * p, GROUPS_PER_TILE * n)


def _pack_out(c):
    g, p, n = c.shape
    t = g // GROUPS_PER_TILE
    eye = jnp.eye(GROUPS_PER_TILE, dtype=c.dtype)
    cc = c.reshape(t, GROUPS_PER_TILE, p, n)
    return jnp.einsum("tgpn,gh->tgnhp", cc, eye).reshape(t, GROUPS_PER_TILE * n, GROUPS_PER_TILE * p)


def _unpack_diag(x, n, p):
    t = x.shape[0]
    xx = x.reshape(t, GROUPS_PER_TILE, n, GROUPS_PER_TILE, p)
    eye = jnp.eye(GROUPS_PER_TILE, dtype=x.dtype)
    return jnp.einsum("tgnhp,gh->tgnp", xx, eye).reshape(t * GROUPS_PER_TILE, n, p)


def _scan_rows(hr_ref, hi_ref, tab_ref, l, reverse, prev_refs=None):
    w = hr_ref.shape[1]
    tabs = [tab_ref[pl.ds(8 * i, 8), :] for i in range(8)]
    nchunk = l // 8
    row = lax.broadcasted_iota(jnp.int32, (8, w), 0)

    def step(s, carry):
        k = nchunk - 1 - s if reverse else s
        t8 = pl.multiple_of(k * 8, 8)
        hr = hr_ref[pl.ds(t8, 8), :]
        hi = hi_ref[pl.ds(t8, 8), :]
        for idx, d in enumerate((1, 2, 4)):
            mr, mi = tabs[2 * idx], tabs[2 * idx + 1]
            shift = 8 - d if reverse else d
            sr = pltpu.roll(hr, shift, 0)
            si = pltpu.roll(hi, shift, 0)
            hr, hi = hr + mr * sr - mi * si, hi + mr * si + mi * sr
        cr, ci = carry[0], carry[1]
        hr, hi = hr + tabs[6] * cr - tabs[7] * ci, hi + tabs[6] * ci + tabs[7] * cr
        hr_ref[pl.ds(t8, 8), :] = hr
        hi_ref[pl.ds(t8, 8), :] = hi
        if not reverse:
            return hr[7:8, :], hi[7:8, :]
        out = (hr[0:1, :], hi[0:1, :])
        if prev_refs is None:
            return out
        fr_ref, fi_ref = prev_refs
        tp = pl.multiple_of(jnp.maximum(k - 1, 0) * 8, 8)
        keep = jnp.where(k > 0, 1.0, 0.0)
        lr_last = fr_ref[pl.ds(tp, 8), :][7:8, :] * keep
        li_last = fi_ref[pl.ds(tp, 8), :][7:8, :] * keep
        pr = jnp.where(row == 0, lr_last, pltpu.roll(fr_ref[pl.ds(t8, 8), :], 1, 0))
        pi = jnp.where(row == 0, li_last, pltpu.roll(fi_ref[pl.ds(t8, 8), :], 1, 0))
        return out + (carry[2] + hr * pr + hi * pi, carry[3] + hi * pr - hr * pi)

    zero = jnp.zeros((1, w), F32)
    init = (zero, zero)
    if reverse and prev_refs is not None:
        init += (jnp.zeros((8, w), F32), jnp.zeros((8, w), F32))
    return lax.fori_loop(0, nchunk, step, init)


def _seg_scan(hr_ref, hi_ref, tab_ref, l, reverse, states_refs=None):
    nq = hr_ref.shape[0]
    seg = l // N_SEG
    span = 8 * N_SEG
    nblk = seg // 8
    row = lax.broadcasted_iota(jnp.int32, (N_SEG, LANES), 0)

    def tab(r0, q):
        return tab_ref[r0:r0 + 8, q * LANES:(q + 1) * LANES]

    lam = [(tab(48, q), tab(56, q)) for q in range(nq)]

    def views(refs, q, jb):
        base = pl.multiple_of((nblk - 1 - jb if reverse else jb) * span, span)
        return [r.at[q, pl.ds(base, span), :] for r in refs]

    def local_rows():
        return range(7, -1, -1) if reverse else range(8)

    def at(r):
        return pl.ds(r * N_SEG, N_SEG)

    def pass1(jb, carry):
        hs = list(carry)
        for q in range(nq):
            vr, vi = views((hr_ref, hi_ref), q, jb)
            lr, li = lam[q]
            h_r, h_i = hs[2 * q], hs[2 * q + 1]
            for r in local_rows():
                h_r, h_i = lr * h_r - li * h_i + vr[at(r), :], lr * h_i + li * h_r + vi[at(r), :]
                vr[at(r), :] = h_r
                vi[at(r), :] = h_i
            hs[2 * q], hs[2 * q + 1] = h_r, h_i
        return tuple(hs)

    zero = jnp.zeros((N_SEG, LANES), F32)
    ends = lax.fori_loop(0, nblk, pass1, (zero,) * (2 * nq))

    carry_in = []
    for q in range(nq):
        er, ei = ends[2 * q], ends[2 * q + 1]
        for idx, d in enumerate((1, 2, 4)):
            mr, mi = tab(16 * idx, q), tab(16 * idx + 8, q)
            shift = N_SEG - d if reverse else d
            sr, si = pltpu.roll(er, shift, 0), pltpu.roll(ei, shift, 0)
            er, ei = er + mr * sr - mi * si, ei + mr * si + mi * sr
        if reverse:
            keep, shift = row < N_SEG - 1, N_SEG - 1
        else:
            keep, shift = row >= 1, 1
        carry_in += [jnp.where(keep, pltpu.roll(er, shift, 0), 0.0), jnp.where(keep, pltpu.roll(ei, shift, 0), 0.0)]

    with_acc = states_refs is not None

    def pass2(jb, carry):
        cs = list(carry)
        for q in range(nq):
            vr, vi = views((hr_ref, hi_ref), q, jb)
            lr, li = lam[q]
            d_r, d_i = cs[2 * q], cs[2 * q + 1]
            if with_acc:
                fr, fi = views(states_refs, q, jb)
                n_r, n_i, a_r, a_i = cs[2 * nq + 4 * q:2 * nq + 4 * q + 4]
            for r in local_rows():
                d_r, d_i = lr * d_r - li * d_i, lr * d_i + li * d_r
                g_r, g_i = vr[at(r), :] + d_r, vi[at(r), :] + d_i
                vr[at(r), :] = g_r
                vi[at(r), :] = g_i
                if with_acc:
                    p_r, p_i = fr[at(r), :], fi[at(r), :]
                    a_r, a_i = a_r + n_r * p_r + n_i * p_i, a_i + n_i * p_r - n_r * p_i
                    n_r, n_i = g_r, g_i
            cs[2 * q], cs[2 * q + 1] = d_r, d_i
            if with_acc:
                cs[2 * nq + 4 * q:2 * nq + 4 * q + 4] = [n_r, n_i, a_r, a_i]
        return tuple(cs)

    init = list(carry_in)
    if with_acc:
        for q in range(nq):
            init += [carry_in[2 * q], carry_in[2 * q + 1], zero, zero]
    out = lax.fori_loop(0, nblk, pass2, tuple(init))
    if with_acc:
        return [(out[2 * nq + 4 * q + 2], out[2 * nq + 4 * q + 3]) for q in range(nq)]
    return None


def _put_states(ref, rows, val):
    for q in range(ref.shape[0]):
        ref[q, rows, :] = val[:, q * LANES:(q + 1) * LANES]


def _get_states(ref, rows):
    return jnp.concatenate([ref[q, rows, :] for q in range(ref.shape[0])], axis=1)


def _s5_dims(sw):
    chan = GROUPS_PER_TILE * SSM_GROUP
    states = GROUPS_PER_TILE * SSM_STATE
    assert chan == LANES and sw % chan == 0
    return sw // chan, chan, states


def _interleave(x):
    l, w = x.shape
    return x.reshape(N_SEG, l // N_SEG, w).transpose(1, 0, 2).reshape(l, w)


def _deinterleave(x):
    l, w = x.shape
    return x.reshape(l // N_SEG, N_SEG, w).transpose(1, 0, 2).reshape(l, w)


def _s5_fwd(proj, u_off, packs, dvec, tab_f, sw, name):
    l = proj.shape[0]
    nt, chan, states = _s5_dims(sw)
    ch = _tile(l, 512, 8)
    ub = u_off // chan
    assert u_off % chan == 0

    def body(u_ref, br_ref, bi_ref, cr_ref, ci_ref, d_ref, tab_ref, y_ref, hr_ref, hi_ref):
        for i in range(l // ch):
            rows = pl.ds(i * ch, ch)
            u = u_ref[rows, :]
            _put_states(hr_ref, rows, jnp.dot(u, br_ref[0], preferred_element_type=F32))
            _put_states(hi_ref, rows, jnp.dot(u, bi_ref[0], preferred_element_type=F32))
        _seg_scan(hr_ref, hi_ref, tab_ref, l, reverse=False)
        for i in range(l // ch):
            rows = pl.ds(i * ch, ch)
            y = jnp.dot(_get_states(hr_ref, rows).astype(BF16), cr_ref[0], preferred_element_type=F32)
            y -= jnp.dot(_get_states(hi_ref, rows).astype(BF16), ci_ref[0], preferred_element_type=F32)
            y_ref[rows, :] = y + d_ref[...] * u_ref[rows, :].astype(F32)

    pin = pl.BlockSpec((1, chan, states), lambda t: (t, 0, 0))
    pout = pl.BlockSpec((1, states, chan), lambda t: (t, 0, 0))
    return pl.pallas_call(
        body,
        out_shape=jax.ShapeDtypeStruct((l, sw), F32),
        grid=(nt,),
        in_specs=[pl.BlockSpec((l, chan), lambda t: (0, ub + t)), pin, pin, pout, pout,
                  pl.BlockSpec((1, chan), lambda t: (0, t)), pl.BlockSpec((64, states), lambda t: (0, t))],
        out_specs=pl.BlockSpec((l, chan), lambda t: (0, t)),
        scratch_shapes=[pltpu.VMEM((states // LANES, l, LANES), F32)] * 2,
        name=name,
        compiler_params=_params(("parallel",)),
    )(proj, packs["br"], packs["bi"], packs["cr"], packs["ci"], dvec, tab_f)


def _s5_bwd(proj, u_off, dy, packs, dvec, tab_f, tab_b, sw, name):
    l = proj.shape[0]
    nt, chan, states = _s5_dims(sw)
    ch = _tile(l, 512, 8)
    ub = u_off // chan
    tn_dims = (((0,), (0,)), ((), ()))

    def body(u_ref, dy_ref, br_ref, bi_ref, brt_ref, bit_ref, crt_ref, cit_ref, d_ref, tabf_ref, tabb_ref,
             du_ref, dlam_ref, dbr_ref, dbi_ref, dcr_ref, dci_ref, dd_ref, hr_ref, hi_ref, gr_ref, gi_ref):
        for i in range(l // ch):
            rows = pl.ds(i * ch, ch)
            u = u_ref[rows, :]
            _put_states(hr_ref, rows, jnp.dot(u, br_ref[0], preferred_element_type=F32))
            _put_states(hi_ref, rows, jnp.dot(u, bi_ref[0], preferred_element_type=F32))
            dyv = dy_ref[rows, :]
            _put_states(gr_ref, rows, jnp.dot(dyv, crt_ref[0], preferred_element_type=F32))
            _put_states(gi_ref, rows, -jnp.dot(dyv, cit_ref[0], preferred_element_type=F32))
        _seg_scan(hr_ref, hi_ref, tabf_ref, l, reverse=False)
        accs = _seg_scan(gr_ref, gi_ref, tabb_ref, l, reverse=True, states_refs=(hr_ref, hi_ref))
        dlam_ref[...] = jnp.concatenate(
            [jnp.concatenate([jnp.sum(a[0], axis=0, keepdims=True) for a in accs], axis=1),
             jnp.concatenate([jnp.sum(a[1], axis=0, keepdims=True) for a in accs], axis=1), jnp.zeros((6, states), F32)], axis=0)
        dbr_ref[...] = jnp.zeros_like(dbr_ref)
        dbi_ref[...] = jnp.zeros_like(dbi_ref)
        dcr_ref[...] = jnp.zeros_like(dcr_ref)
        dci_ref[...] = jnp.zeros_like(dci_ref)
        dd = jnp.zeros((1, chan), F32)
        for i in range(l // ch):
            rows = pl.ds(i * ch, ch)
            u = u_ref[rows, :]
            dyv = dy_ref[rows, :]
            grb = _get_states(gr_ref, rows).astype(BF16)
            gib = _get_states(gi_ref, rows).astype(BF16)
            dbr_ref[0] += lax.dot_general(grb, u, tn_dims, preferred_element_type=F32)
            dbi_ref[0] += lax.dot_general(gib, u, tn_dims, preferred_element_type=F32)
            dcr_ref[0] += lax.dot_general(_get_states(hr_ref, rows).astype(BF16), dyv, tn_dims, preferred_element_type=F32)
            dci_ref[0] -= lax.dot_general(_get_states(hi_ref, rows).astype(BF16), dyv, tn_dims, preferred_element_type=F32)
            du = jnp.dot(grb, brt_ref[0], preferred_element_type=F32) + jnp.dot(gib, bit_ref[0], preferred_element_type=F32)
            dyf = dyv.astype(F32)
            du_ref[rows, :] = (du + d_ref[...] * dyf).astype(BF16)
            dd += jnp.sum(dyf * u.astype(F32), axis=0, keepdims=True)
        dd_ref[...] = dd

    pin = pl.BlockSpec((1, chan, states), lambda t: (t, 0, 0))
    pout = pl.BlockSpec((1, states, chan), lambda t: (t, 0, 0))
    seq = pl.BlockSpec((l, chan), lambda t: (0, t))
    tab = pl.BlockSpec((64, states), lambda t: (0, t))
    vec = pl.BlockSpec((1, chan), lambda t: (0, t))
    pack_shape = jax.ShapeDtypeStruct((nt, states, chan), F32)
    return pl.pallas_call(
        body,
        out_shape=(jax.ShapeDtypeStruct((l, sw), BF16), jax.ShapeDtypeStruct((8, nt * states), F32),
                   pack_shape, pack_shape, pack_shape, pack_shape, jax.ShapeDtypeStruct((1, sw), F32)),
        grid=(nt,),
        in_specs=[pl.BlockSpec((l, chan), lambda t: (0, ub + t)), seq, pin, pin, pout, pout, pin, pin, vec, tab, tab],
        out_specs=(seq, pl.BlockSpec((8, states), lambda t: (0, t)), pout, pout, pout, pout, vec),
        scratch_shapes=[pltpu.VMEM((states // LANES, l, LANES), F32)] * 4,
        name=name,
        compiler_params=_params(("parallel",)),
    )(proj, dy, packs["br"], packs["bi"], packs["brt"], packs["bit"], packs["crt"], packs["cit"], dvec, tab_f, tab_b)


GELU_K = math.sqrt(2.0 / math.pi)
GELU_C = 0.044715


def _gelu(y, name):
    l, w = y.shape
    tl = _tile(l, 512, 8)

    def body(y_ref, o_ref):
        v = y_ref[...]
        o_ref[...] = (0.5 * v * (1.0 + jnp.tanh(GELU_K * (v + GELU_C * v * v * v)))).astype(BF16)

    return pl.pallas_call(body, out_shape=jax.ShapeDtypeStruct((l, w), BF16), grid=(l // tl,),
                          in_specs=[_row_spec(tl, w)], out_specs=_row_spec(tl, w), name=name,
                          compiler_params=_params(("parallel",)))(y)


def _gelu_bwd(y, dg, name):
    l, w = y.shape
    tl = _tile(l, 512, 8)

    def body(y_ref, dg_ref, o_ref):
        v = y_ref[...]
        t = jnp.tanh(GELU_K * (v + GELU_C * v * v * v))
        grad = 0.5 * (1.0 + t) + 0.5 * v * (1.0 - t * t) * GELU_K * (1.0 + 3.0 * GELU_C * v * v)
        o_ref[...] = (dg_ref[...].astype(F32) * grad).astype(BF16)

    return pl.pallas_call(body, out_shape=jax.ShapeDtypeStruct((l, w), BF16), grid=(l // tl,),
                          in_specs=[_row_spec(tl, w), _row_spec(tl, w)], out_specs=_row_spec(tl, w), name=name,
                          compiler_params=_params(("parallel",)))(y, dg)


MIX_COLS = 256


def _mix(proj, ga_off, gs_off, attn_out, glu, d, name):
    l = proj.shape[0]
    tl = _tile(l, 1024, 8)
    cb = MIX_COLS
    nj = d // cb
    assert d % cb == 0 and ga_off % cb == 0 and gs_off % cb == 0

    def body(ga_ref, gs_ref, a_ref, ua_ref, ub_ref, o_ref):
        ssm = ua_ref[...].astype(F32) * _sigmoid(ub_ref[...].astype(F32))
        o_ref[...] = (_sigmoid(ga_ref[...].astype(F32)) * a_ref[...].astype(F32)
                      + _sigmoid(gs_ref[...].astype(F32)) * ssm).astype(BF16)

    def spec(off):
        return pl.BlockSpec((tl, cb), lambda i, j, off=off: (i, off // cb + j))

    return pl.pallas_call(
        body, out_shape=jax.ShapeDtypeStruct((l, d), BF16), grid=(l // tl, nj),
        in_specs=[spec(ga_off), spec(gs_off), spec(0), spec(0), spec(d)], out_specs=spec(0), name=name,
        compiler_params=_params(("parallel", "parallel")),
    )(proj, proj, attn_out, glu, glu)


def _mix_bwd(proj, ga_off, gs_off, attn_out, glu, dmixed, d, name):
    l = proj.shape[0]
    tl = _tile(l, 1024, 8)
    cb = MIX_COLS
    nj = d // cb

    def body(ga_ref, gs_ref, a_ref, ua_ref, ub_ref, dm_ref, dga_ref, dgs_ref, da_ref, dglu_ref):
        s = pl.program_id(2)
        dm = dm_ref[...].astype(F32)
        sa = _sigmoid(ga_ref[...].astype(F32))
        ss = _sigmoid(gs_ref[...].astype(F32))
        sb = _sigmoid(ub_ref[...].astype(F32))
        ua = ua_ref[...].astype(F32)
        dssm = dm * ss

        @pl.when(s == 0)
        def _():
            dga_ref[...] = (dm * a_ref[...].astype(F32) * sa * (1.0 - sa)).astype(BF16)
            da_ref[...] = (dm * sa).astype(BF16)
            dgs_ref[...] = (dm * (ua * sb) * ss * (1.0 - ss)).astype(BF16)
            dglu_ref[...] = (dssm * sb).astype(BF16)

        @pl.when(s == 1)
        def _():
            dglu_ref[...] = (dssm * ua * sb * (1.0 - sb)).astype(BF16)

    def spec(off):
        return pl.BlockSpec((tl, cb), lambda i, j, s, off=off: (i, off // cb + j))

    out = jax.ShapeDtypeStruct((l, d), BF16)
    return pl.pallas_call(
        body, out_shape=(out, out, out, jax.ShapeDtypeStruct((l, 2 * d), BF16)), grid=(l // tl, nj, 2),
        in_specs=[spec(ga_off), spec(gs_off), spec(0), spec(0), spec(d), spec(0)],
        out_specs=(spec(0), spec(0), spec(0), pl.BlockSpec((tl, cb), lambda i, j, s: (i, j + s * nj))), name=name,
        compiler_params=_params(("parallel", "parallel", "arbitrary")),
    )(proj, proj, attn_out, glu, glu, dmixed)


CONV_BLOCKS = 4
HALO = 16


def _shift_rows(v, k, head):
    row = lax.broadcasted_iota(jnp.int32, v.shape, 0)
    out = pltpu.roll(v, k, 0)
    for r in range(k):
        out = jnp.where(row == r, head[HALO - k + r:HALO - k + r + 1, :], out)
    return out


def _shift_rows_up(v, k, tail):
    n = v.shape[0]
    row = lax.broadcasted_iota(jnp.int32, v.shape, 0)
    out = pltpu.roll(v, n - k, 0)
    for r in range(k):
        out = jnp.where(row == n - k + r, tail[r:r + 1, :], out)
    return out


def _conv_gate(g, head, w_ref, b_ref):
    return w_ref[0:1, :] * _shift_rows(g, 2, head) + w_ref[1:2, :] * _shift_rows(g, 1, head) + w_ref[2:3, :] * g + b_ref[...]


def _conv_act(up, conv_w, conv_b, ff, cw, name):
    l = up.shape[0]
    tl = _tile(l, 256, HALO)
    nj = ff // cw
    hb = tl // HALO

    def body(g_ref, gp_ref, v_ref, w_ref, b_ref, o_ref):
        i = pl.program_id(0)
        head = gp_ref[...].astype(F32) * jnp.where(i > 0, 1.0, 0.0)
        gc = _conv_gate(g_ref[...].astype(F32), head, w_ref, b_ref)
        o_ref[...] = (gc * _sigmoid(gc) * v_ref[...].astype(F32)).astype(BF16)

    return pl.pallas_call(
        body, out_shape=jax.ShapeDtypeStruct((l, ff), BF16), grid=(l // tl, nj),
        in_specs=[pl.BlockSpec((tl, cw), lambda i, j: (i, 2 * j)),
                  pl.BlockSpec((HALO, cw), lambda i, j: (jnp.maximum(i * hb - 1, 0), 2 * j)),
                  pl.BlockSpec((tl, cw), lambda i, j: (i, 2 * j + 1)),
                  pl.BlockSpec((3, cw), lambda i, j: (0, j)), pl.BlockSpec((1, cw), lambda i, j: (0, j))],
        out_specs=pl.BlockSpec((tl, cw), lambda i, j: (i, j)), name=name,
        compiler_params=_params(("parallel", "parallel")),
    )(up, up, up, conv_w, conv_b)


def _conv_act_bwd(up, da, conv_w, conv_b, ff, cw, name):
    l = up.shape[0]
    tl = _tile(l, 256, HALO)
    nj = ff // cw
    hb = tl // HALO
    ni = l // tl

    def body(g_ref, gp_ref, gn_ref, v_ref, vn_ref, da_ref, dan_ref, w_ref, b_ref, dup_ref, dw_ref, db_ref):
        i = pl.program_id(0)
        g = g_ref[...].astype(F32)
        head = gp_ref[...].astype(F32) * jnp.where(i > 0, 1.0, 0.0)
        g1 = _shift_rows(g, 1, head)
        g2 = _shift_rows(g, 2, head)
        gc = w_ref[0:1, :] * g2 + w_ref[1:2, :] * g1 + w_ref[2:3, :] * g + b_ref[...]
        sg = _sigmoid(gc)
        dav = da_ref[...].astype(F32)
        dgc = dav * v_ref[...].astype(F32) * (sg * (1.0 + gc * (1.0 - sg)))
        gn = gn_ref[...].astype(F32)
        gcn = _conv_gate(gn, g[tl - HALO:, :], w_ref, b_ref)
        sgn = _sigmoid(gcn)
        dgcn = dan_ref[...].astype(F32) * vn_ref[...].astype(F32) * (sgn * (1.0 + gcn * (1.0 - sgn)))
        dgcn = dgcn * jnp.where(i < ni - 1, 1.0, 0.0)
        dgate = w_ref[2:3, :] * dgc + w_ref[1:2, :] * _shift_rows_up(dgc, 1, dgcn) + w_ref[0:1, :] * _shift_rows_up(dgc, 2, dgcn)
        dup_ref[:, :cw] = dgate.astype(BF16)
        dup_ref[:, cw:] = (dav * (gc * sg)).astype(BF16)
        zero = jnp.zeros((1, cw), F32)
        dw_ref[...] = jnp.concatenate(
            [jnp.sum(dgc * g2, axis=0, keepdims=True), jnp.sum(dgc * g1, axis=0, keepdims=True),
             jnp.sum(dgc * g, axis=0, keepdims=True)] + [zero] * 5, axis=0)
        db_ref[...] = jnp.concatenate([jnp.sum(dgc, axis=0, keepdims=True)] + [zero] * 7, axis=0)

    def cur(col):
        return pl.BlockSpec((tl, cw), lambda i, j, col=col: (i, 2 * j + col))

    def prev(col):
        return pl.BlockSpec((HALO, cw), lambda i, j, col=col: (jnp.maximum(i * hb - 1, 0), 2 * j + col))

    def nxt(col):
        return pl.BlockSpec((HALO, cw), lambda i, j, col=col: (jnp.minimum((i + 1) * hb, l // HALO - 1), 2 * j + col))

    part = jax.ShapeDtypeStruct((ni * 8, ff), F32)
    part_spec = pl.BlockSpec((8, cw), lambda i, j: (i, j))
    return pl.pallas_call(
        body, out_shape=(jax.ShapeDtypeStruct((l, 2 * ff), BF16), part, part), grid=(ni, nj),
        in_specs=[cur(0), prev(0), nxt(0), cur(1), nxt(1), pl.BlockSpec((tl, cw), lambda i, j: (i, j)),
                  pl.BlockSpec((HALO, cw), lambda i, j: (jnp.minimum((i + 1) * hb, l // HALO - 1), j)),
                  pl.BlockSpec((3, cw), lambda i, j: (0, j)), pl.BlockSpec((1, cw), lambda i, j: (0, j))],
        out_specs=(pl.BlockSpec((tl, 2 * cw), lambda i, j: (i, j)), part_spec, part_spec), name=name,
        compiler_params=_params(("parallel", "parallel")),
    )(up, up, up, up, up, da, da, conv_w, conv_b)


def _sum_rows8(parts, name):
    n8, w = parts.shape
    n = n8 // 8
    cw = _tile(w, 2048)

    def body(p_ref, o_ref):
        acc = p_ref[0:8, :]
        for k in range(1, n):
            acc = acc + p_ref[8 * k:8 * k + 8, :]
        o_ref[...] = acc

    return pl.pallas_call(body, out_shape=jax.ShapeDtypeStruct((8, w), F32), grid=(w // cw,),
                          in_specs=[pl.BlockSpec((n8, cw), lambda j: (0, j))], out_specs=pl.BlockSpec((8, cw), lambda j: (0, j)),
                          name=name, compiler_params=_params(("parallel",)))(parts)


def _ada_fwd(c_all, w_shard, b_shard, name):
    nb, d = c_all.shape
    n = w_shard.shape[1]
    tn = _tile(n, 512)

    def body(c_ref, w_ref, b_ref, o_ref):
        cv = c_ref[...]
        cond = (cv * _sigmoid(cv)).astype(BF16)
        o_ref[...] = jnp.dot(cond, w_ref[...].astype(BF16), preferred_element_type=F32) + b_ref[...]

    return pl.pallas_call(
        body, out_shape=jax.ShapeDtypeStruct((nb, n), F32), grid=(n // tn,),
        in_specs=[pl.BlockSpec((nb, d), lambda j: (0, 0)), pl.BlockSpec((d, tn), lambda j: (0, j)),
                  pl.BlockSpec((1, tn), lambda j: (0, j))],
        out_specs=pl.BlockSpec((nb, tn), lambda j: (0, j)), name=name, compiler_params=_params(("parallel",)),
    )(c_all, w_shard, b_shard)


def _adam_update(w, g, m, v):
    m2 = ADAM_B1 * m + (1.0 - ADAM_B1) * g
    v2 = ADAM_B2 * v + (1.0 - ADAM_B2) * (g * g)
    m_hat = m2 / (1.0 - ADAM_B1 ** ADAM_STEP)
    v_hat = v2 / (1.0 - ADAM_B2 ** ADAM_STEP)
    return -ADAM_LR * (m_hat / (jnp.sqrt(v_hat) + ADAM_EPS) + ADAM_WD * w), m2, v2


def _ada_bwd_adam(c_all_t, dmod_shard, w, m, v, name):
    d, nb = c_all_t.shape
    n = w.shape[1]
    tr, tn = _tile(d, 512, 8), _tile(n, 512)

    def body(c_ref, dm_ref, w_ref, m_ref, v_ref, g_ref, dl_ref, m2_ref, v2_ref):
        cv = c_ref[...]
        cond = cv * _sigmoid(cv)
        g = cond[:, 0:1] * dm_ref[0:1, :]
        for b in range(1, nb):
            g = g + cond[:, b:b + 1] * dm_ref[b:b + 1, :]
        g_ref[...] = g
        dl_ref[...], m2_ref[...], v2_ref[...] = _adam_update(w_ref[...], g, m_ref[...], v_ref[...])

    blk = pl.BlockSpec((tr, tn), lambda i, j: (i, j))
    out = jax.ShapeDtypeStruct((d, n), F32)
    return pl.pallas_call(
        body, out_shape=(out, out, out, out), grid=(d // tr, n // tn),
        in_specs=[pl.BlockSpec((tr, nb), lambda i, j: (i, 0)), pl.BlockSpec((nb, tn), lambda i, j: (0, j)), blk, blk, blk],
        out_specs=(blk, blk, blk, blk), name=name, compiler_params=_params(("parallel", "parallel")),
    )(c_all_t, dmod_shard, w, m, v)


def _adam(w, g, m, v, name):
    r, c = w.shape
    tr = _tile(r, 256, 8)

    def body(w_ref, g_ref, m_ref, v_ref, dl_ref, m2_ref, v2_ref):
        dl_ref[...], m2_ref[...], v2_ref[...] = _adam_update(w_ref[...], g_ref[...], m_ref[...], v_ref[...])

    blk = pl.BlockSpec((tr, c), lambda i: (i, 0))
    out = jax.ShapeDtypeStruct((r, c), F32)
    return pl.pallas_call(body, out_shape=(out, out, out), grid=(r // tr,), in_specs=[blk] * 4, out_specs=(blk,) * 3,
                          name=name, compiler_params=_params(("parallel",)))(w, g, m, v)


def _sum_devices(gathered, name):
    nd, r, c = gathered.shape
    tr = _tile(r, 64, 8)

    def body(g_ref, o_ref):
        acc = g_ref[0]
        for k in range(1, nd):
            acc = acc + g_ref[k]
        o_ref[...] = acc

    return pl.pallas_call(body, out_shape=jax.ShapeDtypeStruct((r, c), F32), grid=(r // tr,),
                          in_specs=[pl.BlockSpec((nd, tr, c), lambda i: (0, i, 0))], out_specs=pl.BlockSpec((tr, c), lambda i: (i, 0)),
                          name=name, compiler_params=_params(("parallel",)))(gathered)


def _place():
    x, y, c = lax.axis_index("x"), lax.axis_index("y"), lax.axis_index("c")
    chips = [(1 - x, y), (x, 1 - y), (1 - x, 1 - y)]
    return x, y, c, chips


def _all_gather8(block, name):
    m_per, n = block.shape

    def body(x_ref, out_ref, send_sems, recv_sems, local_sem):
        x, y, c, chips = _place()
        me, sibling = (x, y, c), (x, y, 1 - c)

        def rows(px, py, pc):
            return out_ref.at[pl.ds((4 * px + 2 * py + pc) * m_per, m_per), :]

        def copy(k, blk, to, src=None):
            return pltpu.make_async_remote_copy(
                src_ref=rows(*blk) if src is None else src, dst_ref=rows(*blk), send_sem=send_sems.at[k],
                recv_sem=recv_sems.at[k], device_id=to, device_id_type=MESH)

        mine = pltpu.make_async_copy(x_ref, rows(*me), local_sem)
        mine.start()
        first = [copy(0, me, sibling, src=x_ref)]
        first += [copy(1 + j, me, (*chip, c), src=x_ref) for j, chip in enumerate(chips)]
        for cp in first:
            cp.start()
        passed = [copy(4 + j, (*chip, c), sibling) for j, chip in enumerate(chips)]
        for j, chip in enumerate(chips):
            copy(1 + j, (*chip, c), me).wait_recv()
            passed[j].start()
        copy(0, sibling, me).wait_recv()
        for j, chip in enumerate(chips):
            copy(4 + j, (*chip, 1 - c), me).wait_recv()
        for cp in first + passed:
            cp.wait_send()
        mine.wait()

    return pl.pallas_call(
        body,
        out_shape=jax.ShapeDtypeStruct((N_DEV * m_per, n), block.dtype),
        in_specs=[pl.BlockSpec(memory_space=pltpu.VMEM)],
        out_specs=pl.BlockSpec(memory_space=pltpu.VMEM),
        scratch_shapes=[pltpu.SemaphoreType.DMA((7,)), pltpu.SemaphoreType.DMA((7,)), pltpu.SemaphoreType.DMA],
        name=name,
        compiler_params=pltpu.CompilerParams(vmem_limit_bytes=VMEM_LIMIT_BYTES),
    )(block)


ANY = pl.BlockSpec(memory_space=pl.ANY)


def _place_shard(shard, name, after=()):
    r, k = shard.shape
    tb = _tile(r, 512, 16)
    nb = r // tb
    chip = (2 * lax.axis_index("x") + lax.axis_index("y")).astype(jnp.int32).reshape(1)

    def body(j_ref, s_ref, *rest):
        rest[-1][...] = s_ref[...].astype(BF16)

    return pl.pallas_call(
        body, out_shape=jax.ShapeDtypeStruct((N_CHIPS * r, k), BF16),
        grid_spec=pltpu.PrefetchScalarGridSpec(
            num_scalar_prefetch=1, grid=(nb,),
            in_specs=[pl.BlockSpec((tb, k), lambda i, j_ref: (i, 0))] + [ANY] * len(after),
            out_specs=pl.BlockSpec((tb, k), lambda i, j_ref: (j_ref[0] * nb + i, 0))),
        name=name, compiler_params=_params(("parallel",)),
    )(chip, shard, *after)


HBM_SPEC = pl.BlockSpec(memory_space=pltpu.HBM)
SEM_SPEC = pl.BlockSpec(memory_space=pltpu.SEMAPHORE)
TOKEN_SPEC = pl.BlockSpec(memory_space=pltpu.VMEM)
SPLIT_COPY = pltpu.CompilerParams(has_side_effects=pltpu.SideEffectType.DATAFLOW_SIDE_EFFECTING)


def _in_hbm(arrays):
    return [pltpu.with_memory_space_constraint(a, pltpu.HBM) for a in arrays]


def _hbm_like(arrays):
    return tuple(pltpu.HBM(a.shape, a.dtype) for a in arrays)


def _token_shape():
    return jax.ShapeDtypeStruct((8, LANES), F32)


def _gathered_rows(buf, px, py, half):
    r = buf.shape[0] // N_CHIPS
    return buf.at[pl.ds(pl.multiple_of((2 * px + py) * r + half * (r // 2), 16), r // 2), :]


def _gather_start(groups, name):
    sizes = [len(g) for g in groups]
    flat = [b for g in groups for b in g]
    nb, ng = len(flat), len(groups)

    def body(*refs):
        bufs = refs[:nb]
        sems = refs[nb:nb + 2 * ng]
        token = refs[-1]
        x, y, c, chips = _place()
        pos = 0
        for gi, nw in enumerate(sizes):
            for k, chip in enumerate(chips):
                for w in range(nw):
                    mine = _gathered_rows(bufs[pos + w], x, y, c)
                    pltpu.make_async_remote_copy(src_ref=mine, dst_ref=mine, send_sem=sems[2 * gi].at[k * nw + w], recv_sem=sems[2 * gi + 1].at[k * nw + w],
                                                 device_id=(*chip, c), device_id_type=MESH).start()
            pos += nw
        token[...] = jnp.zeros_like(token)

    sem_shapes = tuple(pltpu.SemaphoreType.DMA((3 * n,)) for n in sizes for _ in range(2))
    outs = pl.pallas_call(
        body, name=name, out_shape=sem_shapes + _hbm_like(flat) + (_token_shape(),),
        in_specs=[HBM_SPEC] * nb, out_specs=(SEM_SPEC,) * (2 * ng) + (HBM_SPEC,) * nb + (TOKEN_SPEC,),
        input_output_aliases={i: 2 * ng + i for i in range(nb)}, compiler_params=SPLIT_COPY,
    )(*_in_hbm(flat))
    res, pos = [], 2 * ng
    for gi, n in enumerate(sizes):
        res.append((outs[2 * gi], outs[2 * gi + 1], list(outs[pos:pos + n])))
        pos += n
    return res, outs[-1]


def _gather_forward(bufs, ici_send, ici_recv, after, name):
    nw, na = len(bufs), len(after)

    def body(*refs):
        b = refs[:nw]
        isend, irecv = refs[nw], refs[nw + 1]
        dsend, drecv = refs[nw + 2 + na], refs[nw + 3 + na]
        x, y, c, chips = _place()
        for k, chip in enumerate(chips):
            for w in range(nw):
                landed = _gathered_rows(b[w], *chip, c)
                pltpu.make_async_remote_copy(src_ref=landed, dst_ref=landed, send_sem=isend.at[k * nw + w], recv_sem=irecv.at[k * nw + w],
                                             device_id=(*chip, c), device_id_type=MESH).wait_recv()
                pltpu.make_async_remote_copy(src_ref=landed, dst_ref=landed, send_sem=dsend.at[k * nw + w], recv_sem=drecv.at[k * nw + w],
                                             device_id=(x, y, 1 - c), device_id_type=MESH).start()
        for k, chip in enumerate(chips):
            for w in range(nw):
                mine = _gathered_rows(b[w], x, y, c)
                pltpu.make_async_remote_copy(src_ref=mine, dst_ref=mine, send_sem=isend.at[k * nw + w], recv_sem=irecv.at[k * nw + w],
                                             device_id=(*chip, c), device_id_type=MESH).wait_send()

    sem = pltpu.SemaphoreType.DMA((3 * nw,))
    outs = pl.pallas_call(
        body, name=name, out_shape=(sem, sem) + _hbm_like(bufs),
        in_specs=[HBM_SPEC] * nw + [SEM_SPEC, SEM_SPEC] + [ANY] * na, out_specs=(SEM_SPEC, SEM_SPEC) + (HBM_SPEC,) * nw,
        input_output_aliases={i: 2 + i for i in range(nw)}, compiler_params=SPLIT_COPY,
    )(*bufs, ici_send, ici_recv, *after)
    return outs[0], outs[1], list(outs[2:])


def _gather_finish(bufs, d2d_send, d2d_recv, name):
    nw = len(bufs)

    def body(*refs):
        b = refs[:nw]
        dsend, drecv = refs[nw], refs[nw + 1]
        x, y, c, chips = _place()
        for k, chip in enumerate(chips):
            for w in range(nw):
                theirs = _gathered_rows(b[w], *chip, 1 - c)
                pltpu.make_async_remote_copy(src_ref=theirs, dst_ref=theirs, send_sem=dsend.at[k * nw + w], recv_sem=drecv.at[k * nw + w],
                                             device_id=(x, y, 1 - c), device_id_type=MESH).wait_recv()
                passed = _gathered_rows(b[w], *chip, c)
                pltpu.make_async_remote_copy(src_ref=passed, dst_ref=passed, send_sem=dsend.at[k * nw + w], recv_sem=drecv.at[k * nw + w],
                                             device_id=(x, y, 1 - c), device_id_type=MESH).wait_send()

    outs = pl.pallas_call(
        body, name=name, out_shape=_hbm_like(bufs), in_specs=[HBM_SPEC] * nw + [SEM_SPEC, SEM_SPEC], out_specs=(HBM_SPEC,) * nw,
        input_output_aliases={i: i for i in range(nw)}, compiler_params=SPLIT_COPY,
    )(*bufs, d2d_send, d2d_recv)
    return list(outs)


def _scatter_start(partials, name):
    nw = len(partials)
    landing = [lax.empty((3,) + p.shape[1:], p.dtype) for p in partials]

    def body(*refs):
        src, land = refs[:nw], refs[nw:2 * nw]
        send_sems, recv_sems = refs[2 * nw], refs[2 * nw + 1]
        token = refs[-1]
        x, y, c, chips = _place()
        for k, chip in enumerate(chips):
            for w in range(nw):
                pltpu.make_async_remote_copy(src_ref=src[w].at[2 * chip[0] + chip[1]], dst_ref=land[w].at[k], send_sem=send_sems.at[k * nw + w],
                                             recv_sem=recv_sems.at[k * nw + w], device_id=(*chip, c), device_id_type=MESH).start()
        token[...] = jnp.zeros_like(token)

    sem = pltpu.SemaphoreType.DMA((3 * nw,))
    outs = pl.pallas_call(
        body, name=name, out_shape=(sem, sem) + _hbm_like(partials) + _hbm_like(landing) + (_token_shape(),),
        in_specs=[HBM_SPEC] * (2 * nw), out_specs=(SEM_SPEC, SEM_SPEC) + (HBM_SPEC,) * (2 * nw) + (TOKEN_SPEC,),
        input_output_aliases={i: 2 + i for i in range(2 * nw)}, compiler_params=SPLIT_COPY,
    )(*_in_hbm(partials), *_in_hbm(landing))
    return (outs[0], outs[1], list(outs[2:2 + nw]), list(outs[2 + nw:2 + 2 * nw])), outs[-1]


def _scatter_wait(started, after, name):
    send_sems, recv_sems, partials, landing = started
    nw = len(partials)

    def body(*refs):
        src, land = refs[:nw], refs[nw:2 * nw]
        ssem, rsem = refs[2 * nw], refs[2 * nw + 1]
        x, y, c, chips = _place()
        for k, chip in enumerate(chips):
            for w in range(nw):
                cp = pltpu.make_async_remote_copy(src_ref=src[w].at[2 * chip[0] + chip[1]], dst_ref=land[w].at[k], send_sem=ssem.at[k * nw + w],
                                                  recv_sem=rsem.at[k * nw + w], device_id=(*chip, c), device_id_type=MESH)
                cp.wait_send()
                cp.wait_recv()

    outs = pl.pallas_call(
        body, name=name, out_shape=_hbm_like(partials) + _hbm_like(landing),
        in_specs=[HBM_SPEC] * (2 * nw) + [SEM_SPEC, SEM_SPEC] + [ANY] * len(after), out_specs=(HBM_SPEC,) * (2 * nw),
        input_output_aliases={i: i for i in range(2 * nw)}, compiler_params=SPLIT_COPY,
    )(*partials, *landing, send_sems, recv_sems, *after)
    return list(outs[:nw]), list(outs[nw:])


def _swap_halves(grads, name, after=()):
    nw, na = len(grads), len(after)

    def body(*refs):
        ins, outs = refs[:nw], refs[nw + na:2 * nw + na]
        send_sems, recv_sems = refs[2 * nw + na:]
        x, y, c, _ = _place()
        copies = []
        for w in range(nw):
            r = grads[w].shape[0] // N_CHIPS
            h = r // 2
            for j in range(N_CHIPS):
                copies.append(pltpu.make_async_remote_copy(
                    src_ref=ins[w].at[pl.ds(pl.multiple_of(j * r + (1 - c) * h, 16), h), :], dst_ref=outs[w].at[pl.ds(j * h, h), :],
                    send_sem=send_sems.at[w, j], recv_sem=recv_sems.at[w, j], device_id=(x, y, 1 - c), device_id_type=MESH))
                copies[-1].start()
        for cp in copies:
            cp.wait()

    sem = pltpu.SemaphoreType.DMA((nw, N_CHIPS))
    return pl.pallas_call(
        body, out_shape=tuple(jax.ShapeDtypeStruct((g.shape[0] // 2, g.shape[1]), g.dtype) for g in grads),
        in_specs=[ANY] * (nw + na), out_specs=(ANY,) * nw, scratch_shapes=[sem, sem], name=name,
    )(*grads, *after)


def _add_halves(grad, other, name):
    k = grad.shape[1]
    h = other.shape[0] // N_CHIPS
    tb = _tile(h, 512, 16)
    g4 = grad.reshape(N_CHIPS, 2, h, k)
    o3 = other.reshape(N_CHIPS, h, k)
    core = lax.axis_index("c").astype(jnp.int32).reshape(1)

    def body(c_ref, g_ref, o_ref, p_ref):
        p_ref[...] = (g_ref[...].astype(F32) + o_ref[...].astype(F32)).astype(BF16)

    return pl.pallas_call(
        body, out_shape=jax.ShapeDtypeStruct((N_CHIPS, h, k), BF16),
        grid_spec=pltpu.PrefetchScalarGridSpec(
            num_scalar_prefetch=1, grid=(N_CHIPS, h // tb),
            in_specs=[pl.BlockSpec((None, None, tb, k), lambda j, i, c_ref: (j, c_ref[0], i, 0)),
                      pl.BlockSpec((None, tb, k), lambda j, i, c_ref: (j, i, 0))],
            out_specs=pl.BlockSpec((None, tb, k), lambda j, i, c_ref: (j, i, 0))),
        name=name, compiler_params=_params(("parallel", "parallel")),
    )(core, g4, o3)


def _add_partials(partial, others, name):
    _, h, k = partial.shape
    tb = _tile(h, 512, 16)
    nb = h // tb
    place = jnp.stack([2 * lax.axis_index("x") + lax.axis_index("y"), lax.axis_index("c")]).astype(jnp.int32)

    def body(s_ref, p_ref, o0_ref, o1_ref, o2_ref, f_ref):
        f_ref[...] = ((p_ref[...].astype(F32) + o0_ref[...].astype(F32)) + o1_ref[...].astype(F32)) + o2_ref[...].astype(F32)

    def other(s):
        return pl.BlockSpec((None, tb, k), lambda i, s_ref, s=s: (s, i, 0))

    return pl.pallas_call(
        body, out_shape=jax.ShapeDtypeStruct((2 * h, k), F32),
        grid_spec=pltpu.PrefetchScalarGridSpec(
            num_scalar_prefetch=1, grid=(nb,),
            in_specs=[pl.BlockSpec((None, tb, k), lambda i, s_ref: (s_ref[0], i, 0)), other(0), other(1), other(2)],
            out_specs=pl.BlockSpec((tb, k), lambda i, s_ref: (s_ref[1] * nb + i, 0))),
        name=name, compiler_params=_params(("parallel",)),
    )(place, partial, others, others, others)


def _share_halves(fulls, name):
    nw = len(fulls)

    def body(*refs):
        ins, outs = refs[:nw], refs[nw:2 * nw]
        send_sems, recv_sems = refs[2 * nw:]
        x, y, c, _ = _place()
        copies = []
        for w in range(nw):
            h = fulls[w].shape[0] // 2
            start = pl.multiple_of(c * h, 8)
            copies.append(pltpu.make_async_remote_copy(
                src_ref=ins[w].at[pl.ds(start, h), :], dst_ref=outs[w].at[pl.ds(start, h), :], send_sem=send_sems.at[w],
                recv_sem=recv_sems.at[w], device_id=(x, y, 1 - c), device_id_type=MESH))
            copies[-1].start()
        for cp in copies:
            cp.wait()

    sem = pltpu.SemaphoreType.DMA((nw,))
    return pl.pallas_call(
        body, out_shape=tuple(jax.ShapeDtypeStruct(f.shape, f.dtype) for f in fulls),
        in_specs=[ANY] * nw, out_specs=(ANY,) * nw, scratch_shapes=[sem, sem], name=name,
        input_output_aliases={w: w for w in range(nw)},
    )(*fulls)


def _forward_then_finish(started_group, after, tag):
    ici_send, ici_recv, bufs = started_group
    d2d_send, d2d_recv, bufs = _gather_forward(bufs, ici_send, ici_recv, after, f"gather_forward_{tag}")
    return _gather_finish(bufs, d2d_send, d2d_recv, f"gather_finish_{tag}")


def _reduce_start(grads, tag, after=()):
    from_sibling = _swap_halves(grads, f"swap_halves_{tag}", after)
    chip_sums = [_add_halves(g, o, f"add_halves_{tag}_{i}") for i, (g, o) in enumerate(zip(grads, from_sibling))]
    return _scatter_start(chip_sums, f"scatter_start_{tag}")


def _reduce_finish(started, after, tag):
    chip_sums, from_chips = _scatter_wait(started, after, f"scatter_wait_{tag}")
    fulls = [_add_partials(p, o, f"add_partials_{tag}_{i}") for i, (p, o) in enumerate(zip(chip_sums, from_chips))]
    return _share_halves(fulls, f"share_halves_{tag}")


def _flatten_pad(parts, cols=SMALL_COLS):
    flat = jnp.concatenate([p.reshape(-1) for p in parts])
    rows = -(-flat.shape[0] // (8 * cols)) * 8
    return jnp.pad(flat, (0, rows * cols - flat.shape[0])).reshape(rows, cols)


def _split_flat(buf, shapes):
    flat = buf.reshape(-1)
    out, off = [], 0
    for s in shapes:
        n = math.prod(s)
        out.append(flat[off:off + n].reshape(s))
        off += n
    return out


def _ssm_setup(seq_len, ssm_a_re, ssm_a_im, ssm_log_dt, ssm_b_re, ssm_b_im, ssm_c_re, ssm_c_im):
    lam_r, lam_i, bbar_r, bbar_i = _ssm_discretize(ssm_a_re, ssm_a_im, ssm_log_dt, ssm_b_re, ssm_b_im)
    tab_f, tab_b = _scan_tables(lam_r, lam_i, seq_len // N_SEG)
    pk = {"br": _pack_in(bbar_r), "bi": _pack_in(bbar_i), "cr": _pack_out(ssm_c_re), "ci": _pack_out(ssm_c_im)}
    packs = {k: v.astype(BF16) for k, v in pk.items()}
    packs.update({"brt": jnp.swapaxes(packs["br"], 1, 2), "bit": jnp.swapaxes(packs["bi"], 1, 2),
                  "crt": jnp.swapaxes(packs["cr"], 1, 2), "cit": jnp.swapaxes(packs["ci"], 1, 2)})
    return packs, tab_f, tab_b


def _local_step(xs, target, mod, w_in_t, later_weights, ffn_grads_ready, norm_mix_g, attn_sinks, ssm, norm_ffn_g, conv_w_full,
                ffn_conv_b, final_g, aw, sw, ff):
    l, d = xs.shape
    u_off = aw + 2 * KV_WIDTH
    ga_off = u_off + sw
    gs_off = ga_off + d
    packs, tab_f, tab_b = _ssm_setup(l, *ssm[:7])
    dvec = ssm[7].reshape(1, sw)

    h1 = _norm_mod(xs, norm_mix_g, mod, 1, 0, "norm_mod1")
    proj = _matmul(h1, w_in_t, "nt", "mm_in")
    attn = _attn_fwd(proj, attn_sinks, aw, "attn_fwd")
    u_il = _interleave(proj[:, u_off:u_off + sw])
    ys_il = _s5_fwd(u_il, 0, packs, dvec, tab_f, sw, "s5_fwd")
    gy = _deinterleave(_gelu(ys_il, "gelu"))
    w_ap_t, w_glu_t, w_out_f, w_up_t, w_down_f = later_weights((gy, attn))
    attn_out = _matmul(attn, w_ap_t, "nt", "mm_attn_proj")
    glu = _matmul(gy, w_glu_t, "nt", "mm_glu")
    mixed = _mix(proj, ga_off, gs_off, attn_out, glu, d, "mix")
    mo = _matmul(mixed, w_out_f, "nn", "mm_out", out_dtype=F32)
    x2, h2 = _resid_norm_mod(xs, mo, norm_ffn_g, mod, 2, 4, 3, "resid_norm_mod2")
    cw = ff // CONV_BLOCKS
    up = _matmul(h2, w_up_t, "nt", "mm_up", interleave=cw)
    act = _conv_act(up, conv_w_full, ffn_conv_b, ff, cw, "conv_act")
    fo = _matmul(act, w_down_f, "nn", "mm_down", out_dtype=F32)
    loss_part, d_final_g, d_gate2, dx3, dfo = _final_loss(x2, fo, mod, 5, final_g.reshape(1, d), target, "final_loss")

    dact = _matmul(dfo, w_down_f, "nt", "mm_down_dx")
    g_down = _matmul(act, dfo, "tn", "mm_down_dw")
    dup, dcw_parts, dcb_parts = _conv_act_bwd(up, dact, conv_w_full, ffn_conv_b, ff, cw, "conv_act_bwd")
    d_conv_w = _sum_rows8(dcw_parts, "sum_conv_w")[:3]
    d_conv_b = _sum_rows8(dcb_parts, "sum_conv_b")[:1]
    dh2 = _matmul(dup, w_up_t, "nn", "mm_up_dx", interleave=cw)
    g_up = _matmul(dup, h2, "tn", "mm_up_dw", interleave=cw)
    mod = ffn_grads_ready(g_up, g_down, mod)
    dx2, d_shift2, d_scale2, d_gain2, dmo, d_gate1 = _norm_mod_bwd(dh2, x2, dx3, norm_ffn_g, mod, 4, "norm_mod2_bwd", branch=mo, gate_col=2)
    dmixed = _matmul(dmo, w_out_f, "nt", "mm_out_dx")
    g_out = _matmul(mixed, dmo, "tn", "mm_out_dw")
    dga, dgs, dattn_out, dglu = _mix_bwd(proj, ga_off, gs_off, attn_out, glu, dmixed, d, "mix_bwd")
    dgy = _matmul(dglu, w_glu_t, "nn", "mm_glu_dx")
    g_glu = _matmul(dglu, gy, "tn", "mm_glu_dw")
    dys_il = _gelu_bwd(ys_il, _interleave(dgy), "gelu_bwd")
    du_il, dlam, dbr_p, dbi_p, dcr_p, dci_p, d_dvec = _s5_bwd(u_il, 0, dys_il, packs, dvec, tab_f, tab_b, sw, "s5_bwd")
    du = _deinterleave(du_il)
    dattn = _matmul(dattn_out, w_ap_t, "nn", "mm_attn_proj_dx")
    g_ap = _matmul(dattn_out, attn, "tn", "mm_attn_proj_dw")
    dq, dkv_cur, dkv_prev, d_sinks = _attn_bwd(proj, attn_sinks, dattn, aw, "attn_bwd")
    dkv = dkv_cur + jnp.concatenate([dkv_prev[ATTN_BLOCK:], jnp.zeros((ATTN_BLOCK, 2 * KV_WIDTH), F32)], axis=0)
    dproj = jnp.concatenate([dq, dkv.astype(BF16), du, dga, dgs], axis=1)
    dh1 = _matmul(dproj, w_in_t, "nn", "mm_in_dx")
    g_in = _matmul(dproj, h1, "tn", "mm_in_dw")
    grad_x, d_shift1, d_scale1, d_gain1 = _norm_mod_bwd(dh1, xs, dx2, norm_mix_g, mod, 1, "norm_mod1_bwd")

    dmod = jnp.concatenate([d_shift1, d_scale1, d_gate1, d_shift2, d_scale2, d_gate2], axis=1)
    small_parts = [dmod, d_gain1, d_sinks, dlam[0], dlam[1], _unpack_diag(dbr_p, SSM_STATE, SSM_GROUP),
                   _unpack_diag(dbi_p, SSM_STATE, SSM_GROUP), _unpack_diag(dcr_p, SSM_STATE, SSM_GROUP),
                   _unpack_diag(dci_p, SSM_STATE, SSM_GROUP), d_dvec, d_gain2, d_conv_b, d_conv_w, d_final_g]
    return loss_part, grad_x, [g_in, g_ap, g_glu, g_out], small_parts


def _kernel_impl(x, c, ada_w, ada_b, norm_mix_g, w_in, attn_sinks, w_attn_proj, ssm_a_re, ssm_a_im, ssm_log_dt, ssm_b_re, ssm_b_im,
                 ssm_c_re, ssm_c_im, ssm_d, w_ssm_glu, w_out, norm_ffn_g, w_ffn_up, ffn_conv_w, ffn_conv_b, w_ffn_down, final_g,
                 loss_target, ms, vs):
    ax, ay, ac = lax.axis_index("x"), lax.axis_index("y"), lax.axis_index("c")
    chip = 2 * ax + ay
    batch_row = 4 * ax + 2 * ay + ac
    d = x.shape[2]
    aw = w_attn_proj.shape[1]
    sw = w_ssm_glu.shape[1]
    ff = N_CHIPS * ffn_conv_w.shape[2]
    ngroups = sw // SSM_GROUP

    c_all = _all_gather8(jnp.pad(c, ((0, 7), (0, 0))), "gather_c").reshape(N_DEV, 8, d)[:, 0, :]
    ncol = ada_w.shape[2]
    b_shard = lax.dynamic_slice(ada_b, (0, chip * ncol), (1, ncol))
    mod_blk = _ada_fwd(c_all, ada_w[0], b_shard, "ada_fwd")
    mod_all = _all_gather8(mod_blk, "gather_mod").reshape(N_CHIPS, 2, 8, ncol)[:, 0]
    mod = lax.dynamic_slice(mod_all, (0, batch_row, 0), (N_CHIPS, 1, ncol)).reshape(1, 6 * d)

    shards = [w_in[0].T.astype(BF16), w_attn_proj[0].T.astype(BF16), w_ssm_glu[0].T.astype(BF16), w_out[0],
              w_ffn_up[0].T.astype(BF16), w_ffn_down[0]]
    conv_w_all = _all_gather8(jnp.pad(ffn_conv_w[0], ((0, 5), (0, 0))), "gather_conv_w")
    conv_w_full = conv_w_all.reshape(N_CHIPS, 2, 8, ff // N_CHIPS)[:, 0, :3].transpose(1, 0, 2).reshape(3, ff)
    placed = [_place_shard(s, f"place_shard_{i}", after=(mod, conv_w_full) if i == 0 else ()) for i, s in enumerate(shards)]
    (first, later), started = _gather_start([placed[:1], placed[1:]], "gather_start")
    (w_in_t,) = _forward_then_finish(first, (started,), "w_in")
    mod = mod + started[0:1, 0:1]

    def later_weights(after):
        return _forward_then_finish(later, after, "later")

    pending = {}

    def ffn_grads_ready(g_up, g_down, mod_now):
        pending["ffn"], token = _reduce_start([g_up, g_down], "ffn")
        return mod_now + token[0:1, 0:1]

    ssm = (ssm_a_re[0], ssm_a_im[0], ssm_log_dt[0], ssm_b_re[0], ssm_b_im[0], ssm_c_re[0], ssm_c_im[0], ssm_d[0])
    loss_part, grad_x, grads, small_parts = _local_step(
        x[0], loss_target[0], mod, w_in_t, later_weights, ffn_grads_ready, norm_mix_g, attn_sinks, ssm, norm_ffn_g, conv_w_full,
        ffn_conv_b, final_g, aw, sw, ff)
    loss = lax.psum(loss_part[0, 0], ("x", "y", "c"))

    small_shapes = [p.shape for p in small_parts]
    part_buf = _flatten_pad(small_parts)
    rows = part_buf.shape[0]
    gathered = _all_gather8(part_buf, "gather_small").reshape(N_DEV, rows, SMALL_COLS)
    pending["rest"], rest_token = _reduce_start(grads, "rest", after=(gathered,))
    gup_t, grad_w_down = _reduce_finish(pending["ffn"], (rest_token,), "ffn")
    grad_w_up = gup_t.T
    summed = _sum_devices(gathered, "sum_small")
    (s_dmod, s_gain1, s_sinks, s_lr, s_li, s_bbr, s_bbi, s_cr, s_ci, s_dd, s_gain2, s_cb, s_cw, s_fg) = _split_flat(summed, small_shapes)
    _, ssm_vjp = jax.vjp(_ssm_discretize, *ssm[:5])
    g_a_re, g_a_im, g_log_dt, g_b_re, g_b_im = ssm_vjp((s_lr.reshape(ngroups, SSM_STATE), s_li.reshape(ngroups, SSM_STATE), s_bbr, s_bbi))
    g_c_re, g_c_im = jnp.swapaxes(s_cr, 1, 2), jnp.swapaxes(s_ci, 1, 2)
    g_conv_w = lax.dynamic_slice(s_cw, (0, chip * (ff // N_CHIPS)), (3, ff // N_CHIPS))

    dmod_all = gathered.reshape(N_DEV, -1)[:, :6 * d]
    dmod_shard = lax.dynamic_slice(dmod_all, (0, chip * ncol), (N_DEV, ncol))
    ada_res = _ada_bwd_adam(c_all.T, dmod_shard, ada_w[0], ms["ada_w"][0], vs["ada_w"][0], "ada_bwd_adam")

    res = {"ada_w": tuple(o[None] for o in ada_res)}

    def adam_big(nm, w, g):
        res[nm] = (g[None],) + tuple(o[None] for o in _adam(w[0], g, ms[nm][0], vs[nm][0], "adam_" + nm))

    adam_big("w_ffn_up", w_ffn_up, grad_w_up)
    adam_big("w_ffn_down", w_ffn_down, grad_w_down)

    small = [("ada_b", ada_b, s_dmod), ("norm_mix_g", norm_mix_g, s_gain1), ("attn_sinks", attn_sinks, s_sinks),
             ("ssm_a_re", ssm_a_re, g_a_re), ("ssm_a_im", ssm_a_im, g_a_im), ("ssm_log_dt", ssm_log_dt, g_log_dt),
             ("ssm_b_re", ssm_b_re, g_b_re), ("ssm_b_im", ssm_b_im, g_b_im), ("ssm_c_re", ssm_c_re, g_c_re),
             ("ssm_c_im", ssm_c_im, g_c_im), ("ssm_d", ssm_d, s_dd), ("norm_ffn_g", norm_ffn_g, s_gain2),
             ("ffn_conv_w", ffn_conv_w, g_conv_w), ("ffn_conv_b", ffn_conv_b, s_cb), ("final_g", final_g, s_fg)]
    shapes = [t[1].shape for t in small]
    bufs = [_flatten_pad([t[1] for t in small]), _flatten_pad([t[2] for t in small]),
            _flatten_pad([ms[t[0]] for t in small]), _flatten_pad([vs[t[0]] for t in small])]
    s_delta, s_m, s_v = _adam(*bufs, "adam_small")
    for t, dl, m2, v2 in zip(small, _split_flat(s_delta, shapes), _split_flat(s_m, shapes), _split_flat(s_v, shapes)):
        res[t[0]] = (t[2].reshape(t[1].shape), dl, m2, v2)

    done = (s_delta, res["w_ffn_up"][1], res["w_ffn_down"][1], res["ada_w"][1])
    gi_t, gap_t, gglu_t, grad_w_out = _reduce_finish(pending["rest"], done, "rest")
    adam_big("w_in", w_in, gi_t.T)
    adam_big("w_attn_proj", w_attn_proj, gap_t.T)
    adam_big("w_ssm_glu", w_ssm_glu, gglu_t.T)
    adam_big("w_out", w_out, grad_w_out)

    outs = [loss, grad_x[None]]
    for i in range(4):
        outs += [res[nm][i] for nm in WEIGHT_ORDER]
    return tuple(outs)


WEIGHT_ORDER = ("ada_w", "ada_b", "norm_mix_g", "w_in", "attn_sinks", "w_attn_proj", "ssm_a_re", "ssm_a_im", "ssm_log_dt", "ssm_b_re",
                "ssm_b_im", "ssm_c_re", "ssm_c_im", "ssm_d", "w_ssm_glu", "w_out", "norm_ffn_g", "w_ffn_up", "ffn_conv_w", "ffn_conv_b",
                "w_ffn_down", "final_g")


def kernel(x, c, ada_w, ada_b, norm_mix_g, w_in, attn_sinks, w_attn_proj, ssm_a_re, ssm_a_im, ssm_log_dt, ssm_b_re, ssm_b_im, ssm_c_re, ssm_c_im, ssm_d, w_ssm_glu, w_out, norm_ffn_g, w_ffn_up, ffn_conv_w, ffn_conv_b, w_ffn_down, final_g, loss_target, m_ada_w, m_ada_b, m_norm_mix_g, m_w_in, m_attn_sinks, m_w_attn_proj, m_ssm_a_re, m_ssm_a_im, m_ssm_log_dt, m_ssm_b_re, m_ssm_b_im, m_ssm_c_re, m_ssm_c_im, m_ssm_d, m_w_ssm_glu, m_w_out, m_norm_ffn_g, m_w_ffn_up, m_ffn_conv_w, m_ffn_conv_b, m_w_ffn_down, m_final_g, v_ada_w, v_ada_b, v_norm_mix_g, v_w_in, v_attn_sinks, v_w_attn_proj, v_ssm_a_re, v_ssm_a_im, v_ssm_log_dt, v_ssm_b_re, v_ssm_b_im, v_ssm_c_re, v_ssm_c_im, v_ssm_d, v_w_ssm_glu, v_w_out, v_norm_ffn_g, v_w_ffn_up, v_ffn_conv_w, v_ffn_conv_b, v_w_ffn_down, v_final_g):
    ms = dict(zip(WEIGHT_ORDER, (m_ada_w, m_ada_b, m_norm_mix_g, m_w_in, m_attn_sinks, m_w_attn_proj, m_ssm_a_re, m_ssm_a_im, m_ssm_log_dt,
                                 m_ssm_b_re, m_ssm_b_im, m_ssm_c_re, m_ssm_c_im, m_ssm_d, m_w_ssm_glu, m_w_out, m_norm_ffn_g, m_w_ffn_up,
                                 m_ffn_conv_w, m_ffn_conv_b, m_w_ffn_down, m_final_g)))
    vs = dict(zip(WEIGHT_ORDER, (v_ada_w, v_ada_b, v_norm_mix_g, v_w_in, v_attn_sinks, v_w_attn_proj, v_ssm_a_re, v_ssm_a_im, v_ssm_log_dt,
                                 v_ssm_b_re, v_ssm_b_im, v_ssm_c_re, v_ssm_c_im, v_ssm_d, v_w_ssm_glu, v_w_out, v_norm_ffn_g, v_w_ffn_up,
                                 v_ffn_conv_w, v_ffn_conv_b, v_w_ffn_down, v_final_g)))
    return _kernel_impl(x, c, ada_w, ada_b, norm_mix_g, w_in, attn_sinks, w_attn_proj, ssm_a_re, ssm_a_im, ssm_log_dt, ssm_b_re, ssm_b_im,
                        ssm_c_re, ssm_c_im, ssm_d, w_ssm_glu, w_out, norm_ffn_g, w_ffn_up, ffn_conv_w, ffn_conv_b, w_ffn_down, final_g,
                        loss_target, ms, vs)
```

```python
import math

import jax
import jax.numpy as jnp
from jax import lax
from jax.experimental import pallas as pl
from jax.experimental.pallas import tpu as pltpu

F32 = jnp.float32
BF16 = jnp.bfloat16
MESH = pl.DeviceIdType.MESH

HEAD_DIM = 64
N_KV_HEADS = 2
KV_WIDTH = N_KV_HEADS * HEAD_DIM
ATTN_BLOCK = 128
NEG_INF = -1e30
SSM_GROUP = 16
SSM_STATE = 64
GROUPS_PER_TILE = 8
RMS_EPS = 1e-6
ADAM_LR = 0.001
ADAM_B1 = 0.9
ADAM_B2 = 0.999
ADAM_EPS = 1e-08
ADAM_WD = 0.01
ADAM_STEP = 10
N_CHIPS = 4
N_DEV = 8
VMEM_LIMIT_BYTES = 56 * 1024 * 1024
LANES = 128
SMALL_COLS = 1024


def _tile(dim, target, mult=LANES):
    if dim <= target:
        return dim
    for t in range(target // mult * mult, 0, -mult):
        if dim % t == 0:
            return t
    raise ValueError(f"no tile for {dim}")


def _params(sem=None):
    return pltpu.CompilerParams(dimension_semantics=sem, vmem_limit_bytes=VMEM_LIMIT_BYTES)


def _sigmoid(x):
    return 1.0 / (1.0 + jnp.exp(-x))


def _matmul(a, b, mode, name, out_dtype=BF16, tm=1536, tn=1536, tk=2048, interleave=None):
    if mode == "nn":
        (m, k), (k2, n) = a.shape, b.shape
    elif mode == "nt":
        (m, k), (n, k2) = a.shape, b.shape
    else:
        (k, m), (k2, n) = a.shape, b.shape
    assert k == k2, (a.shape, b.shape, mode)
    if interleave is not None:
        tn, tk, tm = (interleave, tk, tm) if mode == "nt" else (tn, interleave, tm) if mode == "nn" else (tn, tk, interleave)
        half = {"nt": n, "nn": k, "tn": m}[mode] // (2 * interleave)

        def perm(blk):
            return blk // 2 + (blk % 2) * half
    else:
        def perm(blk):
            return blk
    tm, tn, tk = _tile(m, tm), _tile(n, tn), _tile(k, tk)
    nk = k // tk
    if mode == "tn":
        a_spec = pl.BlockSpec((tk, tm), lambda i, j, kk: (kk, i))
    else:
        a_spec = pl.BlockSpec((tm, tk), lambda i, j, kk: (i, kk))
    if mode == "nt":
        b_spec = pl.BlockSpec((tn, tk), lambda i, j, kk: (perm(j), kk))
    elif mode == "nn":
        b_spec = pl.BlockSpec((tk, tn), lambda i, j, kk: (perm(kk), j))
    else:
        b_spec = pl.BlockSpec((tk, tn), lambda i, j, kk: (kk, j))
    out_rows = perm if mode == "tn" else (lambda blk: blk)
    dims = {"nn": (((1,), (0,)), ((), ())), "nt": (((1,), (1,)), ((), ())), "tn": (((0,), (0,)), ((), ()))}[mode]

    def body(a_ref, b_ref, o_ref, acc_ref):
        kk = pl.program_id(2)

        @pl.when(kk == 0)
        def _():
            acc_ref[...] = jnp.zeros_like(acc_ref)

        acc_ref[...] += lax.dot_general(a_ref[...], b_ref[...], dims, preferred_element_type=F32)

        @pl.when(kk == nk - 1)
        def _():
            o_ref[...] = acc_ref[...].astype(o_ref.dtype)

    return pl.pallas_call(
        body,
        out_shape=jax.ShapeDtypeStruct((m, n), out_dtype),
        grid=(m // tm, n // tn, nk),
        in_specs=[a_spec, b_spec],
        out_specs=pl.BlockSpec((tm, tn), lambda i, j, kk: (out_rows(i), j)),
        scratch_shapes=[pltpu.VMEM((tm, tn), F32)],
        name=name,
        compiler_params=_params(("parallel", "parallel", "arbitrary")),
    )(a, b)


def _row_spec(tl, w, col=0):
    return pl.BlockSpec((tl, w), lambda i, col=col: (i, col))


def _vec_spec(w, col=0):
    return pl.BlockSpec((1, w), lambda i, col=col: (0, col))


def _norm_mod(x, gain, mod, sc_col, sh_col, name):
    l, d = x.shape
    tl = _tile(l, 256, 8)

    def body(x_ref, g_ref, sc_ref, sh_ref, h_ref):
        xv = x_ref[...]
        r = lax.rsqrt(jnp.mean(xv * xv, axis=-1, keepdims=True) + RMS_EPS)
        h_ref[...] = ((xv * r) * g_ref[...] * (1.0 + sc_ref[...]) + sh_ref[...]).astype(BF16)

    return pl.pallas_call(
        body,
        out_shape=jax.ShapeDtypeStruct((l, d), BF16),
        grid=(l // tl,),
        in_specs=[_row_spec(tl, d), _vec_spec(d), _vec_spec(d, sc_col), _vec_spec(d, sh_col)],
        out_specs=_row_spec(tl, d),
        name=name,
        compiler_params=_params(("parallel",)),
    )(x, gain, mod, mod)


def _resid_norm_mod(x, mo, gain, mod, gate_col, sc_col, sh_col, name):
    l, d = x.shape
    tl = _tile(l, 256, 8)

    def body(x_ref, mo_ref, g_ref, gate_ref, sc_ref, sh_ref, x2_ref, h_ref):
        xv = x_ref[...] + gate_ref[...] * mo_ref[...]
        x2_ref[...] = xv
        r = lax.rsqrt(jnp.mean(xv * xv, axis=-1, keepdims=True) + RMS_EPS)
        h_ref[...] = ((xv * r) * g_ref[...] * (1.0 + sc_ref[...]) + sh_ref[...]).astype(BF16)

    return pl.pallas_call(
        body,
        out_shape=(jax.ShapeDtypeStruct((l, d), F32), jax.ShapeDtypeStruct((l, d), BF16)),
        grid=(l // tl,),
        in_specs=[_row_spec(tl, d), _row_spec(tl, d), _vec_spec(d), _vec_spec(d, gate_col), _vec_spec(d, sc_col),
                  _vec_spec(d, sh_col)],
        out_specs=(_row_spec(tl, d), _row_spec(tl, d)),
        name=name,
        compiler_params=_params(("parallel",)),
    )(x, mo, gain, mod, mod, mod)


def _final_loss(x2, f, mod, gate_col, final_g, target, name):
    l, d = x2.shape
    tl = _tile(l, 256, 8)

    def body(x2_ref, f_ref, gate_ref, fg_ref, t_ref, loss_ref, dfg_ref, dgate_ref, dx3_ref, df_ref):
        i = pl.program_id(0)
        fv = f_ref[...]
        x3 = x2_ref[...] + gate_ref[...] * fv
        r = lax.rsqrt(jnp.mean(x3 * x3, axis=-1, keepdims=True) + RMS_EPS)
        xh = x3 * r
        err = xh * fg_ref[...] - t_ref[...]
        part = 0.5 * jnp.sum(jnp.mean(err * err, axis=-1, keepdims=True), axis=0, keepdims=True)
        dout = err * (1.0 / d)
        dxh = dout * fg_ref[...]
        dx3 = r * (dxh - xh * jnp.mean(dxh * xh, axis=-1, keepdims=True))
        dx3_ref[...] = dx3
        df_ref[...] = (gate_ref[...] * dx3).astype(BF16)

        @pl.when(i == 0)
        def _():
            loss_ref[...] = jnp.zeros_like(loss_ref)
            dfg_ref[...] = jnp.zeros_like(dfg_ref)
            dgate_ref[...] = jnp.zeros_like(dgate_ref)

        loss_ref[...] += jnp.broadcast_to(part, loss_ref.shape)
        dfg_ref[...] += jnp.sum(dout * xh, axis=0, keepdims=True)
        dgate_ref[...] += jnp.sum(dx3 * fv, axis=0, keepdims=True)

    vec = pl.BlockSpec((1, d), lambda i: (0, 0))
    return pl.pallas_call(
        body,
        out_shape=(jax.ShapeDtypeStruct((1, LANES), F32), jax.ShapeDtypeStruct((1, d), F32),
                   jax.ShapeDtypeStruct((1, d), F32), jax.ShapeDtypeStruct((l, d), F32),
                   jax.ShapeDtypeStruct((l, d), BF16)),
        grid=(l // tl,),
        in_specs=[_row_spec(tl, d), _row_spec(tl, d), _vec_spec(d, gate_col), vec, _row_spec(tl, d)],
        out_specs=(pl.BlockSpec((1, LANES), lambda i: (0, 0)), vec, vec, _row_spec(tl, d), _row_spec(tl, d)),
        name=name,
        compiler_params=_params(("arbitrary",)),
    )(x2, f, mod, final_g, target)


def _norm_mod_bwd(dh, x, dx_res, gain, mod, sc_col, name, branch=None, gate_col=None):
    l, d = x.shape
    tl = _tile(l, 256, 8)
    with_gate = branch is not None

    def body(*refs):
        if with_gate:
            dh_ref, x_ref, dr_ref, g_ref, sc_ref, br_ref, gate_ref, dx_ref, dsh_ref, dsc_ref, dg_ref, dm_ref, dgate_ref = refs
        else:
            dh_ref, x_ref, dr_ref, g_ref, sc_ref, dx_ref, dsh_ref, dsc_ref, dg_ref = refs
        i = pl.program_id(0)
        xv = x_ref[...]
        dhv = dh_ref[...].astype(F32)
        r = lax.rsqrt(jnp.mean(xv * xv, axis=-1, keepdims=True) + RMS_EPS)
        xh = xv * r
        dn = dhv * (1.0 + sc_ref[...])
        dxh = dn * g_ref[...]
        dx = dr_ref[...] + r * (dxh - xh * jnp.mean(dxh * xh, axis=-1, keepdims=True))
        dx_ref[...] = dx

        @pl.when(i == 0)
        def _():
            dsh_ref[...] = jnp.zeros_like(dsh_ref)
            dsc_ref[...] = jnp.zeros_like(dsc_ref)
            dg_ref[...] = jnp.zeros_like(dg_ref)
            if with_gate:
                dgate_ref[...] = jnp.zeros_like(dgate_ref)

        dsh_ref[...] += jnp.sum(dhv, axis=0, keepdims=True)
        dsc_ref[...] += jnp.sum(dhv * (xh * g_ref[...]), axis=0, keepdims=True)
        dg_ref[...] += jnp.sum(dn * xh, axis=0, keepdims=True)
        if with_gate:
            dm_ref[...] = (gate_ref[...] * dx).astype(BF16)
            dgate_ref[...] += jnp.sum(dx * br_ref[...], axis=0, keepdims=True)

    vec = pl.BlockSpec((1, d), lambda i: (0, 0))
    in_specs = [_row_spec(tl, d), _row_spec(tl, d), _row_spec(tl, d), vec, _vec_spec(d, sc_col)]
    args = [dh, x, dx_res, gain, mod]
    out_shape = [jax.ShapeDtypeStruct((l, d), F32)] + [jax.ShapeDtypeStruct((1, d), F32)] * 3
    out_specs = [_row_spec(tl, d), vec, vec, vec]
    if with_gate:
        in_specs += [_row_spec(tl, d), _vec_spec(d, gate_col)]
        args += [branch, mod]
        out_shape += [jax.ShapeDtypeStruct((l, d), BF16), jax.ShapeDtypeStruct((1, d), F32)]
        out_specs += [_row_spec(tl, d), vec]
    return pl.pallas_call(
        body, out_shape=tuple(out_shape), grid=(l // tl,), in_specs=in_specs, out_specs=tuple(out_specs),
        name=name, compiler_params=_params(("arbitrary",)),
    )(*args)


def _attn_mask(n, rows):
    del rows
    qi = lax.broadcasted_iota(jnp.int32, (ATTN_BLOCK, 2 * ATTN_BLOCK), 0)
    kj = lax.broadcasted_iota(jnp.int32, (ATTN_BLOCK, 2 * ATTN_BLOCK), 1)
    rel = qi + ATTN_BLOCK - kj
    return jnp.where((rel >= 0) & (rel < ATTN_BLOCK) & ((kj >= ATTN_BLOCK) | (n > 0)), 0.0, NEG_INF)


def _attn_probs(qs, kh, sink, mask):
    rows = qs.shape[0]
    s = lax.dot_general(qs, kh, (((1,), (1,)), ((), ())), preferred_element_type=F32) * (HEAD_DIM ** -0.5)
    s = s.reshape(-1, ATTN_BLOCK, 2 * ATTN_BLOCK) + mask[None]
    m = jnp.maximum(jnp.max(s, axis=-1, keepdims=True), sink)
    p = jnp.exp(s - m)
    es = jnp.exp(sink - m)
    inv = 1.0 / (jnp.sum(p, axis=-1, keepdims=True) + es)
    return (p * inv).reshape(rows, 2 * ATTN_BLOCK), (es * inv).reshape(rows, 1)


def _stack_heads(src_ref, dst_ref, g, qpk):
    for i in range(qpk):
        h = g * qpk + i
        dst_ref[i * ATTN_BLOCK:(i + 1) * ATTN_BLOCK, :] = src_ref[:, h * HEAD_DIM:(h + 1) * HEAD_DIM]


def _unstack_heads(val, dst_ref, g, qpk):
    for i in range(qpk):
        h = g * qpk + i
        dst_ref[:, h * HEAD_DIM:(h + 1) * HEAD_DIM] = val[i * ATTN_BLOCK:(i + 1) * ATTN_BLOCK, :].astype(dst_ref.dtype)


def _sink_column(sinks):
    return sinks.reshape(-1, 1, 1)


def _sink_spec(nq):
    return pl.BlockSpec((nq, 1, 1), lambda n: (0, 0, 0))


def _attn_specs(aw):
    kvb = aw // (2 * KV_WIDTH)
    q_spec = pl.BlockSpec((ATTN_BLOCK, aw), lambda n: (n, 0))
    kv_cur = pl.BlockSpec((ATTN_BLOCK, 2 * KV_WIDTH), lambda n: (n, kvb))
    kv_prev = pl.BlockSpec((ATTN_BLOCK, 2 * KV_WIDTH), lambda n: (jnp.maximum(n - 1, 0), kvb))
    return q_spec, kv_cur, kv_prev


def _attn_fwd(proj, sinks, aw, name):
    l = proj.shape[0]
    nq = aw // HEAD_DIM
    qpk = nq // N_KV_HEADS
    assert aw % (2 * KV_WIDTH) == 0

    rows = qpk * ATTN_BLOCK

    def body(q_ref, kvc_ref, kvp_ref, sink_ref, o_ref):
        n = pl.program_id(0)
        valid = _attn_mask(n, rows) == 0.0
        kv = jnp.concatenate([kvp_ref[...], kvc_ref[...]], axis=0)
        for h in range(nq):
            g = h // qpk
            qh = q_ref[:, h * HEAD_DIM:(h + 1) * HEAD_DIM]
            kh = kv[:, g * HEAD_DIM:(g + 1) * HEAD_DIM]
            vh = kv[:, KV_WIDTH + g * HEAD_DIM:KV_WIDTH + (g + 1) * HEAD_DIM]
            sink = sink_ref[0:1, h:h + 1]
            s = lax.dot_general(qh, kh, (((1,), (1,)), ((), ())), preferred_element_type=F32) * (HEAD_DIM ** -0.5)
            s = jnp.where(valid, s, NEG_INF)
            m = jnp.maximum(jnp.max(s, axis=-1, keepdims=True), sink)
            p = jnp.exp(s - m)
            p = p * (1.0 / (jnp.sum(p, axis=-1, keepdims=True) + jnp.exp(sink - m)))
            o = jnp.dot(p.astype(BF16), vh, preferred_element_type=F32)
            o_ref[:, h * HEAD_DIM:(h + 1) * HEAD_DIM] = o.astype(BF16)

    q_spec, kv_cur, kv_prev = _attn_specs(aw)
    return pl.pallas_call(
        body,
        out_shape=jax.ShapeDtypeStruct((l, aw), BF16),
        grid=(l // ATTN_BLOCK,),
        in_specs=[q_spec, kv_cur, kv_prev, pl.BlockSpec((1, nq), lambda n: (0, 0))],
        out_specs=pl.BlockSpec((ATTN_BLOCK, aw), lambda n: (n, 0)),
        name=name,
        compiler_params=_params(("parallel",)),
    )(proj, proj, proj, sinks)


def _attn_bwd(proj, sinks, dattn, aw, name):
    l = proj.shape[0]
    nq = aw // HEAD_DIM
    qpk = nq // N_KV_HEADS
    scale = HEAD_DIM ** -0.5

    rows = qpk * ATTN_BLOCK
    tn_dims = (((0,), (0,)), ((), ()))

    def body(q_ref, kvc_ref, kvp_ref, sink_ref, do_ref, dq_ref, dcur_ref, dprev_ref, dsink_ref, q_scr, do_scr):
        n = pl.program_id(0)
        mask = _attn_mask(n, rows)
        kv = jnp.concatenate([kvp_ref[...], kvc_ref[...]], axis=0)
        lane = lax.broadcasted_iota(jnp.int32, (1, nq), 1)
        dsink = jnp.zeros((1, nq), F32)
        dks, dvs = [], []
        for g in range(N_KV_HEADS):
            kh = kv[:, g * HEAD_DIM:(g + 1) * HEAD_DIM]
            vh = kv[:, KV_WIDTH + g * HEAD_DIM:KV_WIDTH + (g + 1) * HEAD_DIM]
            _stack_heads(q_ref, q_scr.at[g], g, qpk)
            _stack_heads(do_ref, do_scr.at[g], g, qpk)
            qs, dos = q_scr[g], do_scr[g]
            p, ps = _attn_probs(qs, kh, sink_ref[g * qpk:(g + 1) * qpk], mask)
            pb = p.astype(BF16)
            o = jnp.dot(pb, vh, preferred_element_type=F32)
            delta = jnp.sum(dos.astype(F32) * o, axis=-1, keepdims=True)
            dp = lax.dot_general(dos, vh, (((1,), (1,)), ((), ())), preferred_element_type=F32)
            ds = (p * (dp - delta)).astype(BF16)
            _unstack_heads(jnp.dot(ds, kh, preferred_element_type=F32) * scale, dq_ref, g, qpk)
            dks.append(lax.dot_general(ds, qs, tn_dims, preferred_element_type=F32) * scale)
            dvs.append(lax.dot_general(pb, dos, tn_dims, preferred_element_type=F32))
            t = ps * delta
            for i in range(qpk):
                part = -jnp.sum(t[i * ATTN_BLOCK:(i + 1) * ATTN_BLOCK, :], axis=0, keepdims=True)
                dsink += jnp.where(lane == g * qpk + i, part, 0.0)
        dkv = jnp.concatenate(dks + dvs, axis=1)
        dprev_ref[...] = dkv[:ATTN_BLOCK]
        dcur_ref[...] = dkv[ATTN_BLOCK:]

        @pl.when(n == 0)
        def _():
            dsink_ref[...] = jnp.zeros_like(dsink_ref)

        dsink_ref[...] += dsink

    q_spec, kv_cur, kv_prev = _attn_specs(aw)
    blk = pl.BlockSpec((ATTN_BLOCK, 2 * KV_WIDTH), lambda n: (n, 0))
    return pl.pallas_call(
        body,
        out_shape=(jax.ShapeDtypeStruct((l, aw), BF16), jax.ShapeDtypeStruct((l, 2 * KV_WIDTH), F32),
                   jax.ShapeDtypeStruct((l, 2 * KV_WIDTH), F32), jax.ShapeDtypeStruct((1, nq), F32)),
        grid=(l // ATTN_BLOCK,),
        in_specs=[q_spec, kv_cur, kv_prev, _sink_spec(nq),
                  pl.BlockSpec((ATTN_BLOCK, aw), lambda n: (n, 0))],
        out_specs=(pl.BlockSpec((ATTN_BLOCK, aw), lambda n: (n, 0)), blk, blk, pl.BlockSpec((1, nq), lambda n: (0, 0))),
        scratch_shapes=[pltpu.VMEM((N_KV_HEADS, rows, HEAD_DIM), BF16)] * 2,
        name=name,
        compiler_params=_params(("arbitrary",)),
    )(proj, proj, proj, _sink_column(sinks), dattn)


def _ssm_discretize(a_re, a_im, log_dt, b_re, b_im):
    dt = jnp.exp(log_dt)[:, None]
    mag = jnp.exp(a_re * dt)
    lr, li = mag * jnp.cos(a_im * dt), mag * jnp.sin(a_im * dt)
    den = a_re * a_re + a_im * a_im
    zr = ((lr - 1.0) * a_re + li * a_im) / den
    zi = (li * a_re - (lr - 1.0) * a_im) / den
    bbar_r = zr[:, :, None] * b_re - zi[:, :, None] * b_im
    bbar_i = zr[:, :, None] * b_im + zi[:, :, None] * b_re
    return lr, li, bbar_r, bbar_i


def _cmul(ar, ai, br, bi):
    return ar * br - ai * bi, ar * bi + ai * br


N_SEG = 8


def _cpow(ar, ai, n):
    out, br, bi = None, ar, ai
    while n:
        if n & 1:
            out = (br, bi) if out is None else _cmul(*out, br, bi)
        br, bi = _cmul(br, bi, br, bi)
        n >>= 1
    return out


def _scan_tables(lr, li, seg):
    lr, li = lr.reshape(1, -1), li.reshape(1, -1)
    row = jnp.arange(N_SEG)[:, None]
    ones = jnp.ones((N_SEG, 1), F32)
    fwd, bwd = [], []
    for d in (1, 2, 4):
        pr, pi = _cpow(lr, li, seg * d)
        fwd += [jnp.where(row >= d, pr, 0.0), jnp.where(row >= d, pi, 0.0)]
        bwd += [jnp.where(row < N_SEG - d, pr, 0.0), jnp.where(row < N_SEG - d, -pi, 0.0)]
    fwd += [ones * lr, ones * li]
    bwd += [ones * lr, ones * -li]
    return jnp.concatenate(fwd, 0), jnp.concatenate(bwd, 0)


def _pack_in(b):
    g, n, p = b.shape
    t = g // GROUPS_PER_TILE
    eye = jnp.eye(GROUPS_PER_TILE, dtype=b.dtype)
    bb = b.reshape(t, GROUPS_PER_TILE, n, p)
    return jnp.einsum("tgnp,gh->tgphn", bb, eye).reshape(t, GROUPS_PER_TILE * p, GROUPS_PER_TILE * n)


def _pack_out(c):
    g, p, n = c.shape
    t = g // GROUPS_PER_TILE
    eye = jnp.eye(GROUPS_PER_TILE, dtype=c.dtype)
    cc = c.reshape(t, GROUPS_PER_TILE, p, n)
    return jnp.einsum("tgpn,gh->tgnhp", cc, eye).reshape(t, GROUPS_PER_TILE * n, GROUPS_PER_TILE * p)


def _unpack_diag(x, n, p):
    t = x.shape[0]
    xx = x.reshape(t, GROUPS_PER_TILE, n, GROUPS_PER_TILE, p)
    eye = jnp.eye(GROUPS_PER_TILE, dtype=x.dtype)
    return jnp.einsum("tgnhp,gh->tgnp", xx, eye).reshape(t * GROUPS_PER_TILE, n, p)


def _scan_rows(hr_ref, hi_ref, tab_ref, l, reverse, prev_refs=None):
    w = hr_ref.shape[1]
    tabs = [tab_ref[pl.ds(8 * i, 8), :] for i in range(8)]
    nchunk = l // 8
    row = lax.broadcasted_iota(jnp.int32, (8, w), 0)

    def step(s, carry):
        k = nchunk - 1 - s if reverse else s
        t8 = pl.multiple_of(k * 8, 8)
        hr = hr_ref[pl.ds(t8, 8), :]
        hi = hi_ref[pl.ds(t8, 8), :]
        for idx, d in enumerate((1, 2, 4)):
            mr, mi = tabs[2 * idx], tabs[2 * idx + 1]
            shift = 8 - d if reverse else d
            sr = pltpu.roll(hr, shift, 0)
            si = pltpu.roll(hi, shift, 0)
            hr, hi = hr + mr * sr - mi * si, hi + mr * si + mi * sr
        cr, ci = carry[0], carry[1]
        hr, hi = hr + tabs[6] * cr - tabs[7] * ci, hi + tabs[6] * ci + tabs[7] * cr
        hr_ref[pl.ds(t8, 8), :] = hr
        hi_ref[pl.ds(t8, 8), :] = hi
        if not reverse:
            return hr[7:8, :], hi[7:8, :]
        out = (hr[0:1, :], hi[0:1, :])
        if prev_refs is None:
            return out
        fr_ref, fi_ref = prev_refs
        tp = pl.multiple_of(jnp.maximum(k - 1, 0) * 8, 8)
        keep = jnp.where(k > 0, 1.0, 0.0)
        lr_last = fr_ref[pl.ds(tp, 8), :][7:8, :] * keep
        li_last = fi_ref[pl.ds(tp, 8), :][7:8, :] * keep
        pr = jnp.where(row == 0, lr_last, pltpu.roll(fr_ref[pl.ds(t8, 8), :], 1, 0))
        pi = jnp.where(row == 0, li_last, pltpu.roll(fi_ref[pl.ds(t8, 8), :], 1, 0))
        return out + (carry[2] + hr * pr + hi * pi, carry[3] + hi * pr - hr * pi)

    zero = jnp.zeros((1, w), F32)
    init = (zero, zero)
    if reverse and prev_refs is not None:
        init += (jnp.zeros((8, w), F32), jnp.zeros((8, w), F32))
    return lax.fori_loop(0, nchunk, step, init)


def _seg_scan(hr_ref, hi_ref, tab_ref, l, reverse, states_refs=None):
    nq = hr_ref.shape[0]
    seg = l // N_SEG
    span = 8 * N_SEG
    nblk = seg // 8
    row = lax.broadcasted_iota(jnp.int32, (N_SEG, LANES), 0)

    def tab(r0, q):
        return tab_ref[r0:r0 + 8, q * LANES:(q + 1) * LANES]

    lam = [(tab(48, q), tab(56, q)) for q in range(nq)]

    def views(refs, q, jb):
        base = pl.multiple_of((nblk - 1 - jb if reverse else jb) * span, span)
        return [r.at[q, pl.ds(base, span), :] for r in refs]

    def local_rows():
        return range(7, -1, -1) if reverse else range(8)

    def at(r):
        return pl.ds(r * N_SEG, N_SEG)

    def pass1(jb, carry):
        hs = list(carry)
        for q in range(nq):
            vr, vi = views((hr_ref, hi_ref), q, jb)
            lr, li = lam[q]
            h_r, h_i = hs[2 * q], hs[2 * q + 1]
            for r in local_rows():
                h_r, h_i = lr * h_r - li * h_i + vr[at(r), :], lr * h_i + li * h_r + vi[at(r), :]
                vr[at(r), :] = h_r
                vi[at(r), :] = h_i
            hs[2 * q], hs[2 * q + 1] = h_r, h_i
        return tuple(hs)

    zero = jnp.zeros((N_SEG, LANES), F32)
    ends = lax.fori_loop(0, nblk, pass1, (zero,) * (2 * nq))

    carry_in = []
    for q in range(nq):
        er, ei = ends[2 * q], ends[2 * q + 1]
        for idx, d in enumerate((1, 2, 4)):
            mr, mi = tab(16 * idx, q), tab(16 * idx + 8, q)
            shift = N_SEG - d if reverse else d
            sr, si = pltpu.roll(er, shift, 0), pltpu.roll(ei, shift, 0)
            er, ei = er + mr * sr - mi * si, ei + mr * si + mi * sr
        if reverse:
            keep, shift = row < N_SEG - 1, N_SEG - 1
        else:
            keep, shift = row >= 1, 1
        carry_in += [jnp.where(keep, pltpu.roll(er, shift, 0), 0.0), jnp.where(keep, pltpu.roll(ei, shift, 0), 0.0)]

    with_acc = states_refs is not None

    def pass2(jb, carry):
        cs = list(carry)
        for q in range(nq):
            vr, vi = views((hr_ref, hi_ref), q, jb)
            lr, li = lam[q]
            d_r, d_i = cs[2 * q], cs[2 * q + 1]
            if with_acc:
                fr, fi = views(states_refs, q, jb)
                n_r, n_i, a_r, a_i = cs[2 * nq + 4 * q:2 * nq + 4 * q + 4]
            for r in local_rows():
                d_r, d_i = lr * d_r - li * d_i, lr * d_i + li * d_r
                g_r, g_i = vr[at(r), :] + d_r, vi[at(r), :] + d_i
                vr[at(r), :] = g_r
                vi[at(r), :] = g_i
                if with_acc:
                    p_r, p_i = fr[at(r), :], fi[at(r), :]
                    a_r, a_i = a_r + n_r * p_r + n_i * p_i, a_i + n_i * p_r - n_r * p_i
                    n_r, n_i = g_r, g_i
            cs[2 * q], cs[2 * q + 1] = d_r, d_i
            if with_acc:
                cs[2 * nq + 4 * q:2 * nq + 4 * q + 4] = [n_r, n_i, a_r, a_i]
        return tuple(cs)

    init = list(carry_in)
    if with_acc:
        for q in range(nq):
            init += [carry_in[2 * q], carry_in[2 * q + 1], zero, zero]
    out = lax.fori_loop(0, nblk, pass2, tuple(init))
    if with_acc:
        return [(out[2 * nq + 4 * q + 2], out[2 * nq + 4 * q + 3]) for q in range(nq)]
    return None


def _put_states(ref, rows, val):
    for q in range(ref.shape[0]):
        ref[q, rows, :] = val[:, q * LANES:(q + 1) * LANES]


def _get_states(ref, rows):
    return jnp.concatenate([ref[q, rows, :] for q in range(ref.shape[0])], axis=1)


def _s5_dims(sw):
    chan = GROUPS_PER_TILE * SSM_GROUP
    states = GROUPS_PER_TILE * SSM_STATE
    assert chan == LANES and sw % chan == 0
    return sw // chan, chan, states


def _interleave(x):
    l, w = x.shape
    return x.reshape(N_SEG, l // N_SEG, w).transpose(1, 0, 2).reshape(l, w)


def _deinterleave(x):
    l, w = x.shape
    return x.reshape(l // N_SEG, N_SEG, w).transpose(1, 0, 2).reshape(l, w)


def _s5_fwd(proj, u_off, packs, dvec, tab_f, sw, name):
    l = proj.shape[0]
    nt, chan, states = _s5_dims(sw)
    ch = _tile(l, 512, 8)
    ub = u_off // chan
    assert u_off % chan == 0

    def body(u_ref, br_ref, bi_ref, cr_ref, ci_ref, d_ref, tab_ref, y_ref, hr_ref, hi_ref):
        for i in range(l // ch):
            rows = pl.ds(i * ch, ch)
            u = u_ref[rows, :]
            _put_states(hr_ref, rows, jnp.dot(u, br_ref[0], preferred_element_type=F32))
            _put_states(hi_ref, rows, jnp.dot(u, bi_ref[0], preferred_element_type=F32))
        _seg_scan(hr_ref, hi_ref, tab_ref, l, reverse=False)
        for i in range(l // ch):
            rows = pl.ds(i * ch, ch)
            y = jnp.dot(_get_states(hr_ref, rows).astype(BF16), cr_ref[0], preferred_element_type=F32)
            y -= jnp.dot(_get_states(hi_ref, rows).astype(BF16), ci_ref[0], preferred_element_type=F32)
            y_ref[rows, :] = y + d_ref[...] * u_ref[rows, :].astype(F32)

    pin = pl.BlockSpec((1, chan, states), lambda t: (t, 0, 0))
    pout = pl.BlockSpec((1, states, chan), lambda t: (t, 0, 0))
    return pl.pallas_call(
        body,
        out_shape=jax.ShapeDtypeStruct((l, sw), F32),
        grid=(nt,),
        in_specs=[pl.BlockSpec((l, chan), lambda t: (0, ub + t)), pin, pin, pout, pout,
                  pl.BlockSpec((1, chan), lambda t: (0, t)), pl.BlockSpec((64, states), lambda t: (0, t))],
        out_specs=pl.BlockSpec((l, chan), lambda t: (0, t)),
        scratch_shapes=[pltpu.VMEM((states // LANES, l, LANES), F32)] * 2,
        name=name,
        compiler_params=_params(("parallel",)),
    )(proj, packs["br"], packs["bi"], packs["cr"], packs["ci"], dvec, tab_f)


def _s5_bwd(proj, u_off, dy, packs, dvec, tab_f, tab_b, sw, name):
    l = proj.shape[0]
    nt, chan, states = _s5_dims(sw)
    ch = _tile(l, 512, 8)
    ub = u_off // chan
    tn_dims = (((0,), (0,)), ((), ()))

    def body(u_ref, dy_ref, br_ref, bi_ref, brt_ref, bit_ref, crt_ref, cit_ref, d_ref, tabf_ref, tabb_ref,
             du_ref, dlam_ref, dbr_ref, dbi_ref, dcr_ref, dci_ref, dd_ref, hr_ref, hi_ref, gr_ref, gi_ref):
        for i in range(l // ch):
            rows = pl.ds(i * ch, ch)
            u = u_ref[rows, :]
            _put_states(hr_ref, rows, jnp.dot(u, br_ref[0], preferred_element_type=F32))
            _put_states(hi_ref, rows, jnp.dot(u, bi_ref[0], preferred_element_type=F32))
            dyv = dy_ref[rows, :]
            _put_states(gr_ref, rows, jnp.dot(dyv, crt_ref[0], preferred_element_type=F32))
            _put_states(gi_ref, rows, -jnp.dot(dyv, cit_ref[0], preferred_element_type=F32))
        _seg_scan(hr_ref, hi_ref, tabf_ref, l, reverse=False)
        accs = _seg_scan(gr_ref, gi_ref, tabb_ref, l, reverse=True, states_refs=(hr_ref, hi_ref))
        dlam_ref[...] = jnp.concatenate(
            [jnp.concatenate([jnp.sum(a[0], axis=0, keepdims=True) for a in accs], axis=1),
             jnp.concatenate([jnp.sum(a[1], axis=0, keepdims=True) for a in accs], axis=1), jnp.zeros((6, states), F32)], axis=0)
        dbr_ref[...] = jnp.zeros_like(dbr_ref)
        dbi_ref[...] = jnp.zeros_like(dbi_ref)
        dcr_ref[...] = jnp.zeros_like(dcr_ref)
        dci_ref[...] = jnp.zeros_like(dci_ref)
        dd = jnp.zeros((1, chan), F32)
        for i in range(l // ch):
            rows = pl.ds(i * ch, ch)
            u = u_ref[rows, :]
            dyv = dy_ref[rows, :]
            grb = _get_states(gr_ref, rows).astype(BF16)
            gib = _get_states(gi_ref, rows).astype(BF16)
            dbr_ref[0] += lax.dot_general(grb, u, tn_dims, preferred_element_type=F32)
            dbi_ref[0] += lax.dot_general(gib, u, tn_dims, preferred_element_type=F32)
            dcr_ref[0] += lax.dot_general(_get_states(hr_ref, rows).astype(BF16), dyv, tn_dims, preferred_element_type=F32)
            dci_ref[0] -= lax.dot_general(_get_states(hi_ref, rows).astype(BF16), dyv, tn_dims, preferred_element_type=F32)
            du = jnp.dot(grb, brt_ref[0], preferred_element_type=F32) + jnp.dot(gib, bit_ref[0], preferred_element_type=F32)
            dyf = dyv.astype(F32)
            du_ref[rows, :] = (du + d_ref[...] * dyf).astype(BF16)
            dd += jnp.sum(dyf * u.astype(F32), axis=0, keepdims=True)
        dd_ref[...] = dd

    pin = pl.BlockSpec((1, chan, states), lambda t: (t, 0, 0))
    pout = pl.BlockSpec((1, states, chan), lambda t: (t, 0, 0))
    seq = pl.BlockSpec((l, chan), lambda t: (0, t))
    tab = pl.BlockSpec((64, states), lambda t: (0, t))
    vec = pl.BlockSpec((1, chan), lambda t: (0, t))
    pack_shape = jax.ShapeDtypeStruct((nt, states, chan), F32)
    return pl.pallas_call(
        body,
        out_shape=(jax.ShapeDtypeStruct((l, sw), BF16), jax.ShapeDtypeStruct((8, nt * states), F32),
                   pack_shape, pack_shape, pack_shape, pack_shape, jax.ShapeDtypeStruct((1, sw), F32)),
        grid=(nt,),
        in_specs=[pl.BlockSpec((l, chan), lambda t: (0, ub + t)), seq, pin, pin, pout, pout, pin, pin, vec, tab, tab],
        out_specs=(seq, pl.BlockSpec((8, states), lambda t: (0, t)), pout, pout, pout, pout, vec),
        scratch_shapes=[pltpu.VMEM((states // LANES, l, LANES), F32)] * 4,
        name=name,
        compiler_params=_params(("parallel",)),
    )(proj, dy, packs["br"], packs["bi"], packs["brt"], packs["bit"], packs["crt"], packs["cit"], dvec, tab_f, tab_b)


GELU_K = math.sqrt(2.0 / math.pi)
GELU_C = 0.044715


def _gelu(y, name):
    l, w = y.shape
    tl = _tile(l, 512, 8)

    def body(y_ref, o_ref):
        v = y_ref[...]
        o_ref[...] = (0.5 * v * (1.0 + jnp.tanh(GELU_K * (v + GELU_C * v * v * v)))).astype(BF16)

    return pl.pallas_call(body, out_shape=jax.ShapeDtypeStruct((l, w), BF16), grid=(l // tl,),
                          in_specs=[_row_spec(tl, w)], out_specs=_row_spec(tl, w), name=name,
                          compiler_params=_params(("parallel",)))(y)


def _gelu_bwd(y, dg, name):
    l, w = y.shape
    tl = _tile(l, 512, 8)

    def body(y_ref, dg_ref, o_ref):
        v = y_ref[...]
        t = jnp.tanh(GELU_K * (v + GELU_C * v * v * v))
        grad = 0.5 * (1.0 + t) + 0.5 * v * (1.0 - t * t) * GELU_K * (1.0 + 3.0 * GELU_C * v * v)
        o_ref[...] = (dg_ref[...].astype(F32) * grad).astype(BF16)

    return pl.pallas_call(body, out_shape=jax.ShapeDtypeStruct((l, w), BF16), grid=(l // tl,),
                          in_specs=[_row_spec(tl, w), _row_spec(tl, w)], out_specs=_row_spec(tl, w), name=name,
                          compiler_params=_params(("parallel",)))(y, dg)


def _mix(ga, gs, attn_out, glu, name):
    l, d = ga.shape
    tl = _tile(l, 256, 16)

    def body(ga_ref, gs_ref, a_ref, u_ref, o_ref):
        ssm = u_ref[:, :d].astype(F32) * _sigmoid(u_ref[:, d:].astype(F32))
        o_ref[...] = (_sigmoid(ga_ref[...].astype(F32)) * a_ref[...].astype(F32)
                      + _sigmoid(gs_ref[...].astype(F32)) * ssm).astype(BF16)

    return pl.pallas_call(
        body, out_shape=jax.ShapeDtypeStruct((l, d), BF16), grid=(l // tl,),
        in_specs=[_row_spec(tl, d), _row_spec(tl, d), _row_spec(tl, d), _row_spec(tl, 2 * d)], out_specs=_row_spec(tl, d),
        name=name, compiler_params=_params(("parallel",)),
    )(ga, gs, attn_out, glu)


def _mix_bwd(ga, gs, attn_out, glu, dmixed, name):
    l, d = ga.shape
    tl = _tile(l, 256, 16)

    def body(ga_ref, gs_ref, a_ref, u_ref, dm_ref, dga_ref, dgs_ref, da_ref, dglu_ref):
        dm = dm_ref[...].astype(F32)
        sa = _sigmoid(ga_ref[...].astype(F32))
        ss = _sigmoid(gs_ref[...].astype(F32))
        sb = _sigmoid(u_ref[:, d:].astype(F32))
        ua = u_ref[:, :d].astype(F32)
        dssm = dm * ss
        dga_ref[...] = (dm * a_ref[...].astype(F32) * sa * (1.0 - sa)).astype(BF16)
        da_ref[...] = (dm * sa).astype(BF16)
        dgs_ref[...] = (dm * (ua * sb) * ss * (1.0 - ss)).astype(BF16)
        dglu_ref[:, :d] = (dssm * sb).astype(BF16)
        dglu_ref[:, d:] = (dssm * ua * sb * (1.0 - sb)).astype(BF16)

    out = jax.ShapeDtypeStruct((l, d), BF16)
    return pl.pallas_call(
        body, out_shape=(out, out, out, jax.ShapeDtypeStruct((l, 2 * d), BF16)), grid=(l // tl,),
        in_specs=[_row_spec(tl, d), _row_spec(tl, d), _row_spec(tl, d), _row_spec(tl, 2 * d), _row_spec(tl, d)],
        out_specs=(_row_spec(tl, d), _row_spec(tl, d), _row_spec(tl, d), _row_spec(tl, 2 * d)), name=name,
        compiler_params=_params(("parallel",)),
    )(ga, gs, attn_out, glu, dmixed)


CONV_BLOCKS = 4
HALO = 16


def _shift_rows(v, k, head):
    row = lax.broadcasted_iota(jnp.int32, v.shape, 0)
    out = pltpu.roll(v, k, 0)
    for r in range(k):
        out = jnp.where(row == r, head[HALO - k + r:HALO - k + r + 1, :], out)
    return out


def _shift_rows_up(v, k, tail):
    n = v.shape[0]
    row = lax.broadcasted_iota(jnp.int32, v.shape, 0)
    out = pltpu.roll(v, n - k, 0)
    for r in range(k):
        out = jnp.where(row == n - k + r, tail[r:r + 1, :], out)
    return out


def _conv_gate(g, head, w_ref, b_ref):
    return w_ref[0:1, :] * _shift_rows(g, 2, head) + w_ref[1:2, :] * _shift_rows(g, 1, head) + w_ref[2:3, :] * g + b_ref[...]


def _conv_act(up, conv_w, conv_b, ff, cw, name):
    l = up.shape[0]
    tl = _tile(l, 256, HALO)
    nj = ff // cw
    hb = tl // HALO

    def body(g_ref, gp_ref, v_ref, w_ref, b_ref, o_ref):
        i = pl.program_id(0)
        head = gp_ref[...].astype(F32) * jnp.where(i > 0, 1.0, 0.0)
        gc = _conv_gate(g_ref[...].astype(F32), head, w_ref, b_ref)
        o_ref[...] = (gc * _sigmoid(gc) * v_ref[...].astype(F32)).astype(BF16)

    return pl.pallas_call(
        body, out_shape=jax.ShapeDtypeStruct((l, ff), BF16), grid=(l // tl, nj),
        in_specs=[pl.BlockSpec((tl, cw), lambda i, j: (i, 2 * j)),
                  pl.BlockSpec((HALO, cw), lambda i, j: (jnp.maximum(i * hb - 1, 0), 2 * j)),
                  pl.BlockSpec((tl, cw), lambda i, j: (i, 2 * j + 1)),
                  pl.BlockSpec((3, cw), lambda i, j: (0, j)), pl.BlockSpec((1, cw), lambda i, j: (0, j))],
        out_specs=pl.BlockSpec((tl, cw), lambda i, j: (i, j)), name=name,
        compiler_params=_params(("parallel", "parallel")),
    )(up, up, up, conv_w, conv_b)


def _conv_act_bwd(up, da, conv_w, conv_b, ff, cw, name):
    l = up.shape[0]
    tl = _tile(l, 256, HALO)
    nj = ff // cw
    hb = tl // HALO
    ni = l // tl

    def body(g_ref, gp_ref, gn_ref, v_ref, vn_ref, da_ref, dan_ref, w_ref, b_ref, dup_ref, dw_ref, db_ref):
        i = pl.program_id(0)
        g = g_ref[...].astype(F32)
        head = gp_ref[...].astype(F32) * jnp.where(i > 0, 1.0, 0.0)
        g1 = _shift_rows(g, 1, head)
        g2 = _shift_rows(g, 2, head)
        gc = w_ref[0:1, :] * g2 + w_ref[1:2, :] * g1 + w_ref[2:3, :] * g + b_ref[...]
        sg = _sigmoid(gc)
        dav = da_ref[...].astype(F32)
        dgc = dav * v_ref[...].astype(F32) * (sg * (1.0 + gc * (1.0 - sg)))
        gn = gn_ref[...].astype(F32)
        gcn = _conv_gate(gn, g[tl - HALO:, :], w_ref, b_ref)
        sgn = _sigmoid(gcn)
        dgcn = dan_ref[...].astype(F32) * vn_ref[...].astype(F32) * (sgn * (1.0 + gcn * (1.0 - sgn)))
        dgcn = dgcn * jnp.where(i < ni - 1, 1.0, 0.0)
        dgate = w_ref[2:3, :] * dgc + w_ref[1:2, :] * _shift_rows_up(dgc, 1, dgcn) + w_ref[0:1, :] * _shift_rows_up(dgc, 2, dgcn)
        dup_ref[:, :cw] = dgate.astype(BF16)
        dup_ref[:, cw:] = (dav * (gc * sg)).astype(BF16)
        zero = jnp.zeros((1, cw), F32)
        dw_ref[...] = jnp.concatenate(
            [jnp.sum(dgc * g2, axis=0, keepdims=True), jnp.sum(dgc * g1, axis=0, keepdims=True),
             jnp.sum(dgc * g, axis=0, keepdims=True)] + [zero] * 5, axis=0)
        db_ref[...] = jnp.concatenate([jnp.sum(dgc, axis=0, keepdims=True)] + [zero] * 7, axis=0)

    def cur(col):
        return pl.BlockSpec((tl, cw), lambda i, j, col=col: (i, 2 * j + col))

    def prev(col):
        return pl.BlockSpec((HALO, cw), lambda i, j, col=col: (jnp.maximum(i * hb - 1, 0), 2 * j + col))

    def nxt(col):
        return pl.BlockSpec((HALO, cw), lambda i, j, col=col: (jnp.minimum((i + 1) * hb, l // HALO - 1), 2 * j + col))

    part = jax.ShapeDtypeStruct((ni * 8, ff), F32)
    part_spec = pl.BlockSpec((8, cw), lambda i, j: (i, j))
    return pl.pallas_call(
        body, out_shape=(jax.ShapeDtypeStruct((l, 2 * ff), BF16), part, part), grid=(ni, nj),
        in_specs=[cur(0), prev(0), nxt(0), cur(1), nxt(1), pl.BlockSpec((tl, cw), lambda i, j: (i, j)),
                  pl.BlockSpec((HALO, cw), lambda i, j: (jnp.minimum((i + 1) * hb, l // HALO - 1), j)),
                  pl.BlockSpec((3, cw), lambda i, j: (0, j)), pl.BlockSpec((1, cw), lambda i, j: (0, j))],
        out_specs=(pl.BlockSpec((tl, 2 * cw), lambda i, j: (i, j)), part_spec, part_spec), name=name,
        compiler_params=_params(("parallel", "parallel")),
    )(up, up, up, up, up, da, da, conv_w, conv_b)


def _sum_rows8(parts, name):
    n8, w = parts.shape
    n = n8 // 8
    cw = _tile(w, 2048)

    def body(p_ref, o_ref):
        acc = p_ref[0:8, :]
        for k in range(1, n):
            acc = acc + p_ref[8 * k:8 * k + 8, :]
        o_ref[...] = acc

    return pl.pallas_call(body, out_shape=jax.ShapeDtypeStruct((8, w), F32), grid=(w // cw,),
                          in_specs=[pl.BlockSpec((n8, cw), lambda j: (0, j))], out_specs=pl.BlockSpec((8, cw), lambda j: (0, j)),
                          name=name, compiler_params=_params(("parallel",)))(parts)


def _ada_fwd(c_all, w_shard, b_shard, name):
    nb, d = c_all.shape
    n = w_shard.shape[1]
    tn = _tile(n, 512)

    def body(c_ref, w_ref, b_ref, o_ref):
        cv = c_ref[...]
        cond = (cv * _sigmoid(cv)).astype(BF16)
        o_ref[...] = jnp.dot(cond, w_ref[...].astype(BF16), preferred_element_type=F32) + b_ref[...]

    return pl.pallas_call(
        body, out_shape=jax.ShapeDtypeStruct((nb, n), F32), grid=(n // tn,),
        in_specs=[pl.BlockSpec((nb, d), lambda j: (0, 0)), pl.BlockSpec((d, tn), lambda j: (0, j)),
                  pl.BlockSpec((1, tn), lambda j: (0, j))],
        out_specs=pl.BlockSpec((nb, tn), lambda j: (0, j)), name=name, compiler_params=_params(("parallel",)),
    )(c_all, w_shard, b_shard)


def _adam_update(w, g, m, v):
    m2 = ADAM_B1 * m + (1.0 - ADAM_B1) * g
    v2 = ADAM_B2 * v + (1.0 - ADAM_B2) * (g * g)
    m_hat = m2 / (1.0 - ADAM_B1 ** ADAM_STEP)
    v_hat = v2 / (1.0 - ADAM_B2 ** ADAM_STEP)
    return -ADAM_LR * (m_hat / (jnp.sqrt(v_hat) + ADAM_EPS) + ADAM_WD * w), m2, v2


def _ada_bwd_adam(c_all_t, dmod_shard, w, m, v, name):
    d, nb = c_all_t.shape
    n = w.shape[1]
    tr, tn = _tile(d, 512, 8), _tile(n, 512)

    def body(c_ref, dm_ref, w_ref, m_ref, v_ref, g_ref, dl_ref, m2_ref, v2_ref):
        cv = c_ref[...]
        cond = cv * _sigmoid(cv)
        g = cond[:, 0:1] * dm_ref[0:1, :]
        for b in range(1, nb):
            g = g + cond[:, b:b + 1] * dm_ref[b:b + 1, :]
        g_ref[...] = g
        dl_ref[...], m2_ref[...], v2_ref[...] = _adam_update(w_ref[...], g, m_ref[...], v_ref[...])

    blk = pl.BlockSpec((tr, tn), lambda i, j: (i, j))
    out = jax.ShapeDtypeStruct((d, n), F32)
    return pl.pallas_call(
        body, out_shape=(out, out, out, out), grid=(d // tr, n // tn),
        in_specs=[pl.BlockSpec((tr, nb), lambda i, j: (i, 0)), pl.BlockSpec((nb, tn), lambda i, j: (0, j)), blk, blk, blk],
        out_specs=(blk, blk, blk, blk), name=name, compiler_params=_params(("parallel", "parallel")),
    )(c_all_t, dmod_shard, w, m, v)


def _adam(w, g, m, v, name):
    r, c = w.shape
    tr = _tile(r, 256, 8)

    def body(w_ref, g_ref, m_ref, v_ref, dl_ref, m2_ref, v2_ref):
        dl_ref[...], m2_ref[...], v2_ref[...] = _adam_update(w_ref[...], g_ref[...], m_ref[...], v_ref[...])

    blk = pl.BlockSpec((tr, c), lambda i: (i, 0))
    out = jax.ShapeDtypeStruct((r, c), F32)
    return pl.pallas_call(body, out_shape=(out, out, out), grid=(r // tr,), in_specs=[blk] * 4, out_specs=(blk,) * 3,
                          name=name, compiler_params=_params(("parallel",)))(w, g, m, v)


def _sum_devices(gathered, name):
    nd, r, c = gathered.shape
    tr = _tile(r, 64, 16)

    def body(g_ref, o_ref):
        acc = g_ref[0].astype(F32)
        for k in range(1, nd):
            acc = acc + g_ref[k].astype(F32)
        o_ref[...] = acc

    return pl.pallas_call(body, out_shape=jax.ShapeDtypeStruct((r, c), F32), grid=(r // tr,),
                          in_specs=[pl.BlockSpec((nd, tr, c), lambda i: (0, i, 0))], out_specs=pl.BlockSpec((tr, c), lambda i: (i, 0)),
                          name=name, compiler_params=_params(("parallel",)))(gathered)


def _place():
    x, y, c = lax.axis_index("x"), lax.axis_index("y"), lax.axis_index("c")
    chips = [(1 - x, y), (x, 1 - y), (1 - x, 1 - y)]
    return x, y, c, chips


def _all_gather8(block, name):
    m_per, n = block.shape

    def body(x_ref, out_ref, send_sems, recv_sems, local_sem):
        x, y, c, chips = _place()
        me, sibling = (x, y, c), (x, y, 1 - c)

        def rows(px, py, pc):
            return out_ref.at[pl.ds((4 * px + 2 * py + pc) * m_per, m_per), :]

        def copy(k, blk, to, src=None):
            return pltpu.make_async_remote_copy(
                src_ref=rows(*blk) if src is None else src, dst_ref=rows(*blk), send_sem=send_sems.at[k],
                recv_sem=recv_sems.at[k], device_id=to, device_id_type=MESH)

        mine = pltpu.make_async_copy(x_ref, rows(*me), local_sem)
        mine.start()
        first = [copy(0, me, sibling, src=x_ref)]
        first += [copy(1 + j, me, (*chip, c), src=x_ref) for j, chip in enumerate(chips)]
        for cp in first:
            cp.start()
        passed = [copy(4 + j, (*chip, c), sibling) for j, chip in enumerate(chips)]
        for j, chip in enumerate(chips):
            copy(1 + j, (*chip, c), me).wait_recv()
            passed[j].start()
        copy(0, sibling, me).wait_recv()
        for j, chip in enumerate(chips):
            copy(4 + j, (*chip, 1 - c), me).wait_recv()
        for cp in first + passed:
            cp.wait_send()
        mine.wait()

    return pl.pallas_call(
        body,
        out_shape=jax.ShapeDtypeStruct((N_DEV * m_per, n), block.dtype),
        in_specs=[pl.BlockSpec(memory_space=pltpu.VMEM)],
        out_specs=pl.BlockSpec(memory_space=pltpu.VMEM),
        scratch_shapes=[pltpu.SemaphoreType.DMA((7,)), pltpu.SemaphoreType.DMA((7,)), pltpu.SemaphoreType.DMA],
        name=name,
        compiler_params=pltpu.CompilerParams(vmem_limit_bytes=VMEM_LIMIT_BYTES),
    )(block)


ANY = pl.BlockSpec(memory_space=pl.ANY)


def _place_shard(shard, name, after=()):
    r, k = shard.shape
    tb = _tile(r, 512, 16)
    nb = r // tb
    chip = (2 * lax.axis_index("x") + lax.axis_index("y")).astype(jnp.int32).reshape(1)

    def body(j_ref, s_ref, *rest):
        rest[-1][...] = s_ref[...].astype(BF16)

    return pl.pallas_call(
        body, out_shape=jax.ShapeDtypeStruct((N_CHIPS * r, k), BF16),
        grid_spec=pltpu.PrefetchScalarGridSpec(
            num_scalar_prefetch=1, grid=(nb,),
            in_specs=[pl.BlockSpec((tb, k), lambda i, j_ref: (i, 0))] + [ANY] * len(after),
            out_specs=pl.BlockSpec((tb, k), lambda i, j_ref: (j_ref[0] * nb + i, 0))),
        name=name, compiler_params=_params(("parallel",)),
    )(chip, shard, *after)


HBM_SPEC = pl.BlockSpec(memory_space=pltpu.HBM)
SEM_SPEC = pl.BlockSpec(memory_space=pltpu.SEMAPHORE)
TOKEN_SPEC = pl.BlockSpec(memory_space=pltpu.VMEM)
SPLIT_COPY = pltpu.CompilerParams(has_side_effects=pltpu.SideEffectType.DATAFLOW_SIDE_EFFECTING)


def _in_hbm(arrays):
    return [pltpu.with_memory_space_constraint(a, pltpu.HBM) for a in arrays]


def _hbm_like(arrays):
    return tuple(pltpu.HBM(a.shape, a.dtype) for a in arrays)


def _token_shape():
    return jax.ShapeDtypeStruct((8, LANES), F32)


def _gathered_rows(buf, px, py, half):
    r = buf.shape[0] // N_CHIPS
    return buf.at[pl.ds(pl.multiple_of((2 * px + py) * r + half * (r // 2), 16), r // 2), :]


def _gather_start(groups, name):
    sizes = [len(g) for g in groups]
    flat = [b for g in groups for b in g]
    nb, ng = len(flat), len(groups)

    def body(*refs):
        bufs = refs[:nb]
        sems = refs[nb:nb + 2 * ng]
        token = refs[-1]
        x, y, c, chips = _place()
        pos = 0
        for gi, nw in enumerate(sizes):
            for k, chip in enumerate(chips):
                for w in range(nw):
                    mine = _gathered_rows(bufs[pos + w], x, y, c)
                    pltpu.make_async_remote_copy(src_ref=mine, dst_ref=mine, send_sem=sems[2 * gi].at[k * nw + w], recv_sem=sems[2 * gi + 1].at[k * nw + w],
                                                 device_id=(*chip, c), device_id_type=MESH).start()
            pos += nw
        token[...] = jnp.zeros_like(token)

    sem_shapes = tuple(pltpu.SemaphoreType.DMA((3 * n,)) for n in sizes for _ in range(2))
    outs = pl.pallas_call(
        body, name=name, out_shape=sem_shapes + _hbm_like(flat) + (_token_shape(),),
        in_specs=[HBM_SPEC] * nb, out_specs=(SEM_SPEC,) * (2 * ng) + (HBM_SPEC,) * nb + (TOKEN_SPEC,),
        input_output_aliases={i: 2 * ng + i for i in range(nb)}, compiler_params=SPLIT_COPY,
    )(*_in_hbm(flat))
    res, pos = [], 2 * ng
    for gi, n in enumerate(sizes):
        res.append((outs[2 * gi], outs[2 * gi + 1], list(outs[pos:pos + n])))
        pos += n
    return res, outs[-1]


def _gather_forward(bufs, ici_send, ici_recv, after, name):
    nw, na = len(bufs), len(after)

    def body(*refs):
        b = refs[:nw]
        isend, irecv = refs[nw], refs[nw + 1]
        dsend, drecv = refs[nw + 2 + na], refs[nw + 3 + na]
        x, y, c, chips = _place()
        for k, chip in enumerate(chips):
            for w in range(nw):
                landed = _gathered_rows(b[w], *chip, c)
                pltpu.make_async_remote_copy(src_ref=landed, dst_ref=landed, send_sem=isend.at[k * nw + w], recv_sem=irecv.at[k * nw + w],
                                             device_id=(*chip, c), device_id_type=MESH).wait_recv()
                pltpu.make_async_remote_copy(src_ref=landed, dst_ref=landed, send_sem=dsend.at[k * nw + w], recv_sem=drecv.at[k * nw + w],
                                             device_id=(x, y, 1 - c), device_id_type=MESH).start()
        for k, chip in enumerate(chips):
            for w in range(nw):
                mine = _gathered_rows(b[w], x, y, c)
                pltpu.make_async_remote_copy(src_ref=mine, dst_ref=mine, send_sem=isend.at[k * nw + w], recv_sem=irecv.at[k * nw + w],
                                             device_id=(*chip, c), device_id_type=MESH).wait_send()

    sem = pltpu.SemaphoreType.DMA((3 * nw,))
    outs = pl.pallas_call(
        body, name=name, out_shape=(sem, sem) + _hbm_like(bufs),
        in_specs=[HBM_SPEC] * nw + [SEM_SPEC, SEM_SPEC] + [ANY] * na, out_specs=(SEM_SPEC, SEM_SPEC) + (HBM_SPEC,) * nw,
        input_output_aliases={i: 2 + i for i in range(nw)}, compiler_params=SPLIT_COPY,
    )(*bufs, ici_send, ici_recv, *after)
    return outs[0], outs[1], list(outs[2:])


def _gather_finish(bufs, d2d_send, d2d_recv, name, after=()):
    nw = len(bufs)

    def body(*refs):
        b = refs[:nw]
        dsend, drecv = refs[nw], refs[nw + 1]
        x, y, c, chips = _place()
        for k, chip in enumerate(chips):
            for w in range(nw):
                theirs = _gathered_rows(b[w], *chip, 1 - c)
                pltpu.make_async_remote_copy(src_ref=theirs, dst_ref=theirs, send_sem=dsend.at[k * nw + w], recv_sem=drecv.at[k * nw + w],
                                             device_id=(x, y, 1 - c), device_id_type=MESH).wait_recv()
                passed = _gathered_rows(b[w], *chip, c)
                pltpu.make_async_remote_copy(src_ref=passed, dst_ref=passed, send_sem=dsend.at[k * nw + w], recv_sem=drecv.at[k * nw + w],
                                             device_id=(x, y, 1 - c), device_id_type=MESH).wait_send()

    outs = pl.pallas_call(
        body, name=name, out_shape=_hbm_like(bufs), in_specs=[HBM_SPEC] * nw + [SEM_SPEC, SEM_SPEC] + [ANY] * len(after),
        out_specs=(HBM_SPEC,) * nw, input_output_aliases={i: i for i in range(nw)}, compiler_params=SPLIT_COPY,
    )(*bufs, d2d_send, d2d_recv, *after)
    return list(outs)


def _scatter_start(partials, name):
    nw = len(partials)
    landing = [lax.empty((3,) + p.shape[1:], p.dtype) for p in partials]

    def body(*refs):
        src, land = refs[:nw], refs[nw:2 * nw]
        send_sems, recv_sems = refs[2 * nw], refs[2 * nw + 1]
        token = refs[-1]
        x, y, c, chips = _place()
        for k, chip in enumerate(chips):
            for w in range(nw):
                pltpu.make_async_remote_copy(src_ref=src[w].at[2 * chip[0] + chip[1]], dst_ref=land[w].at[k], send_sem=send_sems.at[k * nw + w],
                                             recv_sem=recv_sems.at[k * nw + w], device_id=(*chip, c), device_id_type=MESH).start()
        token[...] = jnp.zeros_like(token)

    sem = pltpu.SemaphoreType.DMA((3 * nw,))
    outs = pl.pallas_call(
        body, name=name, out_shape=(sem, sem) + _hbm_like(partials) + _hbm_like(landing) + (_token_shape(),),
        in_specs=[HBM_SPEC] * (2 * nw), out_specs=(SEM_SPEC, SEM_SPEC) + (HBM_SPEC,) * (2 * nw) + (TOKEN_SPEC,),
        input_output_aliases={i: 2 + i for i in range(2 * nw)}, compiler_params=SPLIT_COPY,
    )(*_in_hbm(partials), *_in_hbm(landing))
    return (outs[0], outs[1], list(outs[2:2 + nw]), list(outs[2 + nw:2 + 2 * nw])), outs[-1]


def _scatter_wait(started, after, name):
    send_sems, recv_sems, partials, landing = started
    nw = len(partials)

    def body(*refs):
        src, land = refs[:nw], refs[nw:2 * nw]
        ssem, rsem = refs[2 * nw], refs[2 * nw + 1]
        x, y, c, chips = _place()
        for k, chip in enumerate(chips):
            for w in range(nw):
                cp = pltpu.make_async_remote_copy(src_ref=src[w].at[2 * chip[0] + chip[1]], dst_ref=land[w].at[k], send_sem=ssem.at[k * nw + w],
                                                  recv_sem=rsem.at[k * nw + w], device_id=(*chip, c), device_id_type=MESH)
                cp.wait_send()
                cp.wait_recv()

    outs = pl.pallas_call(
        body, name=name, out_shape=_hbm_like(partials) + _hbm_like(landing),
        in_specs=[HBM_SPEC] * (2 * nw) + [SEM_SPEC, SEM_SPEC] + [ANY] * len(after), out_specs=(HBM_SPEC,) * (2 * nw),
        input_output_aliases={i: i for i in range(2 * nw)}, compiler_params=SPLIT_COPY,
    )(*partials, *landing, send_sems, recv_sems, *after)
    return list(outs[:nw]), list(outs[nw:])


def _swap_halves(grads, name, after=()):
    nw, na = len(grads), len(after)

    def body(*refs):
        ins, outs = refs[:nw], refs[nw + na:2 * nw + na]
        send_sems, recv_sems = refs[2 * nw + na:]
        x, y, c, _ = _place()
        copies = []
        for w in range(nw):
            r = grads[w].shape[0] // N_CHIPS
            h = r // 2
            for j in range(N_CHIPS):
                copies.append(pltpu.make_async_remote_copy(
                    src_ref=ins[w].at[pl.ds(pl.multiple_of(j * r + (1 - c) * h, 16), h), :], dst_ref=outs[w].at[pl.ds(j * h, h), :],
                    send_sem=send_sems.at[w, j], recv_sem=recv_sems.at[w, j], device_id=(x, y, 1 - c), device_id_type=MESH))
                copies[-1].start()
        for cp in copies:
            cp.wait()

    sem = pltpu.SemaphoreType.DMA((nw, N_CHIPS))
    return pl.pallas_call(
        body, out_shape=tuple(jax.ShapeDtypeStruct((g.shape[0] // 2, g.shape[1]), g.dtype) for g in grads),
        in_specs=[ANY] * (nw + na), out_specs=(ANY,) * nw, scratch_shapes=[sem, sem], name=name,
    )(*grads, *after)


def _add_halves(grad, other, name):
    k = grad.shape[1]
    h = other.shape[0] // N_CHIPS
    tb = _tile(h, 512, 16)
    g4 = grad.reshape(N_CHIPS, 2, h, k)
    o3 = other.reshape(N_CHIPS, h, k)
    core = lax.axis_index("c").astype(jnp.int32).reshape(1)

    def body(c_ref, g_ref, o_ref, p_ref):
        p_ref[...] = (g_ref[...].astype(F32) + o_ref[...].astype(F32)).astype(BF16)

    return pl.pallas_call(
        body, out_shape=jax.ShapeDtypeStruct((N_CHIPS, h, k), BF16),
        grid_spec=pltpu.PrefetchScalarGridSpec(
            num_scalar_prefetch=1, grid=(N_CHIPS, h // tb),
            in_specs=[pl.BlockSpec((None, None, tb, k), lambda j, i, c_ref: (j, c_ref[0], i, 0)),
                      pl.BlockSpec((None, tb, k), lambda j, i, c_ref: (j, i, 0))],
            out_specs=pl.BlockSpec((None, tb, k), lambda j, i, c_ref: (j, i, 0))),
        name=name, compiler_params=_params(("parallel", "parallel")),
    )(core, g4, o3)


def _add_partials(partial, others, name):
    _, h, k = partial.shape
    tb = _tile(h, 512, 16)
    nb = h // tb
    place = jnp.stack([2 * lax.axis_index("x") + lax.axis_index("y"), lax.axis_index("c")]).astype(jnp.int32)

    def body(s_ref, p_ref, o0_ref, o1_ref, o2_ref, f_ref):
        f_ref[...] = ((p_ref[...].astype(F32) + o0_ref[...].astype(F32)) + o1_ref[...].astype(F32)) + o2_ref[...].astype(F32)

    def other(s):
        return pl.BlockSpec((None, tb, k), lambda i, s_ref, s=s: (s, i, 0))

    return pl.pallas_call(
        body, out_shape=jax.ShapeDtypeStruct((2 * h, k), F32),
        grid_spec=pltpu.PrefetchScalarGridSpec(
            num_scalar_prefetch=1, grid=(nb,),
            in_specs=[pl.BlockSpec((None, tb, k), lambda i, s_ref: (s_ref[0], i, 0)), other(0), other(1), other(2)],
            out_specs=pl.BlockSpec((tb, k), lambda i, s_ref: (s_ref[1] * nb + i, 0))),
        name=name, compiler_params=_params(("parallel",)),
    )(place, partial, others, others, others)


def _share_halves(fulls, name):
    nw = len(fulls)

    def body(*refs):
        ins, outs = refs[:nw], refs[nw:2 * nw]
        send_sems, recv_sems = refs[2 * nw:]
        x, y, c, _ = _place()
        copies = []
        for w in range(nw):
            h = fulls[w].shape[0] // 2
            start = pl.multiple_of(c * h, 8)
            copies.append(pltpu.make_async_remote_copy(
                src_ref=ins[w].at[pl.ds(start, h), :], dst_ref=outs[w].at[pl.ds(start, h), :], send_sem=send_sems.at[w],
                recv_sem=recv_sems.at[w], device_id=(x, y, 1 - c), device_id_type=MESH))
            copies[-1].start()
        for cp in copies:
            cp.wait()

    sem = pltpu.SemaphoreType.DMA((nw,))
    return pl.pallas_call(
        body, out_shape=tuple(jax.ShapeDtypeStruct(f.shape, f.dtype) for f in fulls),
        in_specs=[ANY] * nw, out_specs=(ANY,) * nw, scratch_shapes=[sem, sem], name=name,
        input_output_aliases={w: w for w in range(nw)},
    )(*fulls)


def _forward_then_finish(started_group, after, tag):
    ici_send, ici_recv, bufs = started_group
    d2d_send, d2d_recv, bufs = _gather_forward(bufs, ici_send, ici_recv, after, f"gather_forward_{tag}")
    return _gather_finish(bufs, d2d_send, d2d_recv, f"gather_finish_{tag}")


def _reduce_start(grads, tag, after=()):
    from_sibling = _swap_halves(grads, f"swap_halves_{tag}", after)
    chip_sums = [_add_halves(g, o, f"add_halves_{tag}_{i}") for i, (g, o) in enumerate(zip(grads, from_sibling))]
    return _scatter_start(chip_sums, f"scatter_start_{tag}")


def _reduce_finish(started, after, tag):
    chip_sums, from_chips = _scatter_wait(started, after, f"scatter_wait_{tag}")
    fulls = [_add_partials(p, o, f"add_partials_{tag}_{i}") for i, (p, o) in enumerate(zip(chip_sums, from_chips))]
    return _share_halves(fulls, f"share_halves_{tag}")


def _flatten_pad(parts, cols=SMALL_COLS):
    flat = jnp.concatenate([p.reshape(-1) for p in parts])
    rows = -(-flat.shape[0] // (16 * cols)) * 16
    return jnp.pad(flat, (0, rows * cols - flat.shape[0])).reshape(rows, cols)


def _split_flat(buf, shapes):
    flat = buf.reshape(-1)
    out, off = [], 0
    for s in shapes:
        n = math.prod(s)
        out.append(flat[off:off + n].reshape(s))
        off += n
    return out


def _ssm_setup(seq_len, ssm_a_re, ssm_a_im, ssm_log_dt, ssm_b_re, ssm_b_im, ssm_c_re, ssm_c_im):
    lam_r, lam_i, bbar_r, bbar_i = _ssm_discretize(ssm_a_re, ssm_a_im, ssm_log_dt, ssm_b_re, ssm_b_im)
    tab_f, tab_b = _scan_tables(lam_r, lam_i, seq_len // N_SEG)
    pk = {"br": _pack_in(bbar_r), "bi": _pack_in(bbar_i), "cr": _pack_out(ssm_c_re), "ci": _pack_out(ssm_c_im)}
    packs = {k: v.astype(BF16) for k, v in pk.items()}
    packs.update({"brt": jnp.swapaxes(packs["br"], 1, 2), "bit": jnp.swapaxes(packs["bi"], 1, 2),
                  "crt": jnp.swapaxes(packs["cr"], 1, 2), "cit": jnp.swapaxes(packs["ci"], 1, 2)})
    return packs, tab_f, tab_b


def _local_step(xs, target, mod, w_in_t, later_weights, ffn_grads_ready, norm_mix_g, attn_sinks, ssm, norm_ffn_g, conv_w_full,
                ffn_conv_b, final_g, aw, sw, ff):
    l, d = xs.shape
    u_off = aw + 2 * KV_WIDTH
    ga_off = u_off + sw
    gs_off = ga_off + d
    packs, tab_f, tab_b = _ssm_setup(l, *ssm[:7])
    dvec = ssm[7].reshape(1, sw)

    h1 = _norm_mod(xs, norm_mix_g, mod, 1, 0, "norm_mod1")
    proj = _matmul(h1, w_in_t, "nt", "mm_in")
    attn = _attn_fwd(proj, attn_sinks, aw, "attn_fwd")
    u_il = _interleave(proj[:, u_off:u_off + sw])
    ys_il = _s5_fwd(u_il, 0, packs, dvec, tab_f, sw, "s5_fwd")
    gy = _deinterleave(_gelu(ys_il, "gelu"))
    (w_ap_t, w_glu_t, w_out_f), ffn_weights = later_weights((gy, attn))
    attn_out = _matmul(attn, w_ap_t, "nt", "mm_attn_proj")
    glu = _matmul(gy, w_glu_t, "nt", "mm_glu")
    g_attn, g_ssm = proj[:, ga_off:ga_off + d], proj[:, gs_off:gs_off + d]
    mixed = _mix(g_attn, g_ssm, attn_out, glu, "mix")
    mo = _matmul(mixed, w_out_f, "nn", "mm_out", out_dtype=F32)
    x2, h2 = _resid_norm_mod(xs, mo, norm_ffn_g, mod, 2, 4, 3, "resid_norm_mod2")
    w_up_t, w_down_f = ffn_weights((h2,))
    cw = ff // CONV_BLOCKS
    up = _matmul(h2, w_up_t, "nt", "mm_up", interleave=cw)
    act = _conv_act(up, conv_w_full, ffn_conv_b, ff, cw, "conv_act")
    fo = _matmul(act, w_down_f, "nn", "mm_down", out_dtype=F32)
    loss_part, d_final_g, d_gate2, dx3, dfo = _final_loss(x2, fo, mod, 5, final_g.reshape(1, d), target, "final_loss")

    dact = _matmul(dfo, w_down_f, "nt", "mm_down_dx")
    g_down = _matmul(act, dfo, "tn", "mm_down_dw")
    dup, dcw_parts, dcb_parts = _conv_act_bwd(up, dact, conv_w_full, ffn_conv_b, ff, cw, "conv_act_bwd")
    d_conv_w = _sum_rows8(dcw_parts, "sum_conv_w")[:3]
    d_conv_b = _sum_rows8(dcb_parts, "sum_conv_b")[:1]
    dh2 = _matmul(dup, w_up_t, "nn", "mm_up_dx", interleave=cw)
    g_up = _matmul(dup, h2, "tn", "mm_up_dw", interleave=cw)
    mod = ffn_grads_ready(g_up, g_down, mod)
    dx2, d_shift2, d_scale2, d_gain2, dmo, d_gate1 = _norm_mod_bwd(dh2, x2, dx3, norm_ffn_g, mod, 4, "norm_mod2_bwd", branch=mo, gate_col=2)
    dmixed = _matmul(dmo, w_out_f, "nt", "mm_out_dx")
    g_out = _matmul(mixed, dmo, "tn", "mm_out_dw")
    dga, dgs, dattn_out, dglu = _mix_bwd(g_attn, g_ssm, attn_out, glu, dmixed, "mix_bwd")
    dgy = _matmul(dglu, w_glu_t, "nn", "mm_glu_dx")
    g_glu = _matmul(dglu, gy, "tn", "mm_glu_dw")
    dys_il = _gelu_bwd(ys_il, _interleave(dgy), "gelu_bwd")
    du_il, dlam, dbr_p, dbi_p, dcr_p, dci_p, d_dvec = _s5_bwd(u_il, 0, dys_il, packs, dvec, tab_f, tab_b, sw, "s5_bwd")
    du = _deinterleave(du_il)
    dattn = _matmul(dattn_out, w_ap_t, "nn", "mm_attn_proj_dx")
    g_ap = _matmul(dattn_out, attn, "tn", "mm_attn_proj_dw")
    dq, dkv_cur, dkv_prev, d_sinks = _attn_bwd(proj, attn_sinks, dattn, aw, "attn_bwd")
    dkv = dkv_cur + jnp.concatenate([dkv_prev[ATTN_BLOCK:], jnp.zeros((ATTN_BLOCK, 2 * KV_WIDTH), F32)], axis=0)
    dproj = jnp.concatenate([dq, dkv.astype(BF16), du, dga, dgs], axis=1)
    dh1 = _matmul(dproj, w_in_t, "nn", "mm_in_dx")
    g_in = _matmul(dproj, h1, "tn", "mm_in_dw")
    grad_x, d_shift1, d_scale1, d_gain1 = _norm_mod_bwd(dh1, xs, dx2, norm_mix_g, mod, 1, "norm_mod1_bwd")

    dmod = jnp.concatenate([d_shift1, d_scale1, d_gate1, d_shift2, d_scale2, d_gate2], axis=1)
    small_parts = [dmod, d_gain1, d_sinks, dlam[0], dlam[1], _unpack_diag(dbr_p, SSM_STATE, SSM_GROUP),
                   _unpack_diag(dbi_p, SSM_STATE, SSM_GROUP), _unpack_diag(dcr_p, SSM_STATE, SSM_GROUP),
                   _unpack_diag(dci_p, SSM_STATE, SSM_GROUP), d_dvec, d_gain2, d_conv_b, d_conv_w, d_final_g]
    return loss_part, grad_x, [g_in, g_ap, g_glu, g_out], small_parts


def _kernel_impl(x, c, ada_w, ada_b, norm_mix_g, w_in, attn_sinks, w_attn_proj, ssm_a_re, ssm_a_im, ssm_log_dt, ssm_b_re, ssm_b_im,
                 ssm_c_re, ssm_c_im, ssm_d, w_ssm_glu, w_out, norm_ffn_g, w_ffn_up, ffn_conv_w, ffn_conv_b, w_ffn_down, final_g,
                 loss_target, ms, vs):
    ax, ay, ac = lax.axis_index("x"), lax.axis_index("y"), lax.axis_index("c")
    chip = 2 * ax + ay
    batch_row = 4 * ax + 2 * ay + ac
    d = x.shape[2]
    aw = w_attn_proj.shape[1]
    sw = w_ssm_glu.shape[1]
    ff = N_CHIPS * ffn_conv_w.shape[2]
    ngroups = sw // SSM_GROUP

    c_all = _all_gather8(jnp.pad(c, ((0, 7), (0, 0))), "gather_c").reshape(N_DEV, 8, d)[:, 0, :]
    ncol = ada_w.shape[2]
    b_shard = lax.dynamic_slice(ada_b, (0, chip * ncol), (1, ncol))
    mod_blk = _ada_fwd(c_all, ada_w[0], b_shard, "ada_fwd")
    mod_all = _all_gather8(mod_blk, "gather_mod").reshape(N_CHIPS, 2, 8, ncol)[:, 0]
    mod = lax.dynamic_slice(mod_all, (0, batch_row, 0), (N_CHIPS, 1, ncol)).reshape(1, 6 * d)

    shards = [w_in[0].T.astype(BF16), w_attn_proj[0].T.astype(BF16), w_ssm_glu[0].T.astype(BF16), w_out[0],
              w_ffn_up[0].T.astype(BF16), w_ffn_down[0]]
    conv_w_all = _all_gather8(jnp.pad(ffn_conv_w[0], ((0, 5), (0, 0))), "gather_conv_w")
    conv_w_full = conv_w_all.reshape(N_CHIPS, 2, 8, ff // N_CHIPS)[:, 0, :3].transpose(1, 0, 2).reshape(3, ff)
    placed = [_place_shard(s, f"place_shard_{i}", after=(mod, conv_w_full) if i == 0 else ()) for i, s in enumerate(shards)]
    (first, mixer, ffn), started = _gather_start([placed[:1], placed[1:4], placed[4:]], "gather_start")
    (w_in_t,) = _forward_then_finish(first, (started,), "w_in")
    mod = mod + started[0:1, 0:1]

    def later_weights(after):
        mixer_weights = _forward_then_finish(mixer, after, "mixer")
        ffn_send, ffn_recv, ffn_bufs = _gather_forward(ffn[2], ffn[0], ffn[1], after, "gather_forward_ffn")
        return mixer_weights, lambda later: _gather_finish(ffn_bufs, ffn_send, ffn_recv, "gather_finish_ffn", later)

    pending = {}

    def ffn_grads_ready(g_up, g_down, mod_now):
        pending["ffn"], token = _reduce_start([g_up, g_down], "ffn")
        return mod_now + token[0:1, 0:1]

    ssm = (ssm_a_re[0], ssm_a_im[0], ssm_log_dt[0], ssm_b_re[0], ssm_b_im[0], ssm_c_re[0], ssm_c_im[0], ssm_d[0])
    loss_part, grad_x, grads, small_parts = _local_step(
        x[0], loss_target[0], mod, w_in_t, later_weights, ffn_grads_ready, norm_mix_g, attn_sinks, ssm, norm_ffn_g, conv_w_full,
        ffn_conv_b, final_g, aw, sw, ff)
    loss = lax.psum(loss_part[0, 0], ("x", "y", "c"))

    small_shapes = [p.shape for p in small_parts]
    part_buf = _flatten_pad(small_parts).astype(BF16)
    rows = part_buf.shape[0]
    gathered = _all_gather8(part_buf, "gather_small").reshape(N_DEV, rows, SMALL_COLS)
    pending["rest"], rest_token = _reduce_start(grads, "rest", after=(gathered,))
    gup_t, grad_w_down = _reduce_finish(pending["ffn"], (rest_token,), "ffn")
    grad_w_up = gup_t.T
    summed = _sum_devices(gathered, "sum_small")
    (s_dmod, s_gain1, s_sinks, s_lr, s_li, s_bbr, s_bbi, s_cr, s_ci, s_dd, s_gain2, s_cb, s_cw, s_fg) = _split_flat(summed, small_shapes)
    _, ssm_vjp = jax.vjp(_ssm_discretize, *ssm[:5])
    g_a_re, g_a_im, g_log_dt, g_b_re, g_b_im = ssm_vjp((s_lr.reshape(ngroups, SSM_STATE), s_li.reshape(ngroups, SSM_STATE), s_bbr, s_bbi))
    g_c_re, g_c_im = jnp.swapaxes(s_cr, 1, 2), jnp.swapaxes(s_ci, 1, 2)
    g_conv_w = lax.dynamic_slice(s_cw, (0, chip * (ff // N_CHIPS)), (3, ff // N_CHIPS))

    dmod_all = gathered.reshape(N_DEV, -1)[:, :6 * d].astype(F32)
    dmod_shard = lax.dynamic_slice(dmod_all, (0, chip * ncol), (N_DEV, ncol))
    ada_res = _ada_bwd_adam(c_all.T, dmod_shard, ada_w[0], ms["ada_w"][0], vs["ada_w"][0], "ada_bwd_adam")

    res = {"ada_w": tuple(o[None] for o in ada_res)}

    def adam_big(nm, w, g):
        res[nm] = (g[None],) + tuple(o[None] for o in _adam(w[0], g, ms[nm][0], vs[nm][0], "adam_" + nm))

    adam_big("w_ffn_up", w_ffn_up, grad_w_up)
    adam_big("w_ffn_down", w_ffn_down, grad_w_down)

    small = [("ada_b", ada_b, s_dmod), ("norm_mix_g", norm_mix_g, s_gain1), ("attn_sinks", attn_sinks, s_sinks),
             ("ssm_a_re", ssm_a_re, g_a_re), ("ssm_a_im", ssm_a_im, g_a_im), ("ssm_log_dt", ssm_log_dt, g_log_dt),
             ("ssm_b_re", ssm_b_re, g_b_re), ("ssm_b_im", ssm_b_im, g_b_im), ("ssm_c_re", ssm_c_re, g_c_re),
             ("ssm_c_im", ssm_c_im, g_c_im), ("ssm_d", ssm_d, s_dd), ("norm_ffn_g", norm_ffn_g, s_gain2),
             ("ffn_conv_w", ffn_conv_w, g_conv_w), ("ffn_conv_b", ffn_conv_b, s_cb), ("final_g", final_g, s_fg)]
    shapes = [t[1].shape for t in small]
    bufs = [_flatten_pad([t[1] for t in small]), _flatten_pad([t[2] for t in small]),
            _flatten_pad([ms[t[0]] for t in small]), _flatten_pad([vs[t[0]] for t in small])]
    s_delta, s_m, s_v = _adam(*bufs, "adam_small")
    for t, dl, m2, v2 in zip(small, _split_flat(s_delta, shapes), _split_flat(s_m, shapes), _split_flat(s_v, shapes)):
        res[t[0]] = (t[2].reshape(t[1].shape), dl, m2, v2)

    done = (s_delta, res["w_ffn_up"][1], res["w_ffn_down"][1], res["ada_w"][1])
    gi_t, gap_t, gglu_t, grad_w_out = _reduce_finish(pending["rest"], done, "rest")
    adam_big("w_in", w_in, gi_t.T)
    adam_big("w_attn_proj", w_attn_proj, gap_t.T)
    adam_big("w_ssm_glu", w_ssm_glu, gglu_t.T)
    adam_big("w_out", w_out, grad_w_out)

    outs = [loss, grad_x[None]]
    for i in range(4):
        outs += [res[nm][i] for nm in WEIGHT_ORDER]
    return tuple(outs)


WEIGHT_ORDER = ("ada_w", "ada_b", "norm_mix_g", "w_in", "attn_sinks", "w_attn_proj", "ssm_a_re", "ssm_a_im", "ssm_log_dt", "ssm_b_re",
                "ssm_b_im", "ssm_c_re", "ssm_c_im", "ssm_d", "w_ssm_glu", "w_out", "norm_ffn_g", "w_ffn_up", "ffn_conv_w", "ffn_conv_b",
                "w_ffn_down", "final_g")


def kernel(x, c, ada_w, ada_b, norm_mix_g, w_in, attn_sinks, w_attn_proj, ssm_a_re, ssm_a_im, ssm_log_dt, ssm_b_re, ssm_b_im, ssm_c_re, ssm_c_im, ssm_d, w_ssm_glu, w_out, norm_ffn_g, w_ffn_up, ffn_conv_w, ffn_conv_b, w_ffn_down, final_g, loss_target, m_ada_w, m_ada_b, m_norm_mix_g, m_w_in, m_attn_sinks, m_w_attn_proj, m_ssm_a_re, m_ssm_a_im, m_ssm_log_dt, m_ssm_b_re, m_ssm_b_im, m_ssm_c_re, m_ssm_c_im, m_ssm_d, m_w_ssm_glu, m_w_out, m_norm_ffn_g, m_w_ffn_up, m_ffn_conv_w, m_ffn_conv_b, m_w_ffn_down, m_final_g, v_ada_w, v_ada_b, v_norm_mix_g, v_w_in, v_attn_sinks, v_w_attn_proj, v_ssm_a_re, v_ssm_a_im, v_ssm_log_dt, v_ssm_b_re, v_ssm_b_im, v_ssm_c_re, v_ssm_c_im, v_ssm_d, v_w_ssm_glu, v_w_out, v_norm_ffn_g, v_w_ffn_up, v_ffn_conv_w, v_ffn_conv_b, v_w_ffn_down, v_final_g):
    ms = dict(zip(WEIGHT_ORDER, (m_ada_w, m_ada_b, m_norm_mix_g, m_w_in, m_attn_sinks, m_w_attn_proj, m_ssm_a_re, m_ssm_a_im, m_ssm_log_dt,
                                 m_ssm_b_re, m_ssm_b_im, m_ssm_c_re, m_ssm_c_im, m_ssm_d, m_w_ssm_glu, m_w_out, m_norm_ffn_g, m_w_ffn_up,
                                 m_ffn_conv_w, m_ffn_conv_b, m_w_ffn_down, m_final_g)))
    vs = dict(zip(WEIGHT_ORDER, (v_ada_w, v_ada_b, v_norm_mix_g, v_w_in, v_attn_sinks, v_w_attn_proj, v_ssm_a_re, v_ssm_a_im, v_ssm_log_dt,
                                 v_ssm_b_re, v_ssm_b_im, v_ssm_c_re, v_ssm_c_im, v_ssm_d, v_w_ssm_glu, v_w_out, v_norm_ffn_g, v_w_ffn_up,
                                 v_ffn_conv_w, v_ffn_conv_b, v_w_ffn_down, v_final_g)))
    return _kernel_impl(x, c, ada_w, ada_b, norm_mix_g, w_in, attn_sinks, w_attn_proj, ssm_a_re, ssm_a_im, ssm_log_dt, ssm_b_re, ssm_b_im,
                        ssm_c_re, ssm_c_im, ssm_d, w_ssm_glu, w_out, norm_ffn_g, w_ffn_up, ffn_conv_w, ffn_conv_b, w_ffn_down, final_g,
                        loss_target, ms, vs)
```

```python
import math

import jax
import jax.numpy as jnp
from jax import lax
from jax.experimental import pallas as pl
from jax.experimental.pallas import tpu as pltpu

F32 = jnp.float32
BF16 = jnp.bfloat16
MESH = pl.DeviceIdType.MESH

HEAD_DIM = 64
N_KV_HEADS = 2
KV_WIDTH = N_KV_HEADS * HEAD_DIM
ATTN_BLOCK = 128
NEG_INF = -1e30
SSM_GROUP = 16
SSM_STATE = 64
GROUPS_PER_TILE = 8
RMS_EPS = 1e-6
ADAM_LR = 0.001
ADAM_B1 = 0.9
ADAM_B2 = 0.999
ADAM_EPS = 1e-08
ADAM_WD = 0.01
ADAM_STEP = 10
N_CHIPS = 4
N_DEV = 8
VMEM_LIMIT_BYTES = 56 * 1024 * 1024
LANES = 128
SMALL_COLS = 1024


def _tile(dim, target, mult=LANES):
    if dim <= target:
        return dim
    for t in range(target // mult * mult, 0, -mult):
        if dim % t == 0:
            return t
    raise ValueError(f"no tile for {dim}")


def _params(sem=None):
    return pltpu.CompilerParams(dimension_semantics=sem, vmem_limit_bytes=VMEM_LIMIT_BYTES)


def _sigmoid(x):
    return 1.0 / (1.0 + jnp.exp(-x))


def _matmul(a, b, mode, name, out_dtype=BF16, tm=1536, tn=1536, tk=2048, interleave=None):
    if mode == "nn":
        (m, k), (k2, n) = a.shape, b.shape
    elif mode == "nt":
        (m, k), (n, k2) = a.shape, b.shape
    else:
        (k, m), (k2, n) = a.shape, b.shape
    assert k == k2, (a.shape, b.shape, mode)
    if interleave is not None:
        tn, tk, tm = (interleave, tk, tm) if mode == "nt" else (tn, interleave, tm) if mode == "nn" else (tn, tk, interleave)
        half = {"nt": n, "nn": k, "tn": m}[mode] // (2 * interleave)

        def perm(blk):
            return blk // 2 + (blk % 2) * half
    else:
        def perm(blk):
            return blk
    tm, tn, tk = _tile(m, tm), _tile(n, tn), _tile(k, tk)
    nk = k // tk
    if mode == "tn":
        a_spec = pl.BlockSpec((tk, tm), lambda i, j, kk: (kk, i))
    else:
        a_spec = pl.BlockSpec((tm, tk), lambda i, j, kk: (i, kk))
    if mode == "nt":
        b_spec = pl.BlockSpec((tn, tk), lambda i, j, kk: (perm(j), kk))
    elif mode == "nn":
        b_spec = pl.BlockSpec((tk, tn), lambda i, j, kk: (perm(kk), j))
    else:
        b_spec = pl.BlockSpec((tk, tn), lambda i, j, kk: (kk, j))
    out_rows = perm if mode == "tn" else (lambda blk: blk)
    dims = {"nn": (((1,), (0,)), ((), ())), "nt": (((1,), (1,)), ((), ())), "tn": (((0,), (0,)), ((), ()))}[mode]

    def body(a_ref, b_ref, o_ref, acc_ref):
        kk = pl.program_id(2)

        @pl.when(kk == 0)
        def _():
            acc_ref[...] = jnp.zeros_like(acc_ref)

        acc_ref[...] += lax.dot_general(a_ref[...], b_ref[...], dims, preferred_element_type=F32)

        @pl.when(kk == nk - 1)
        def _():
            o_ref[...] = acc_ref[...].astype(o_ref.dtype)

    return pl.pallas_call(
        body,
        out_shape=jax.ShapeDtypeStruct((m, n), out_dtype),
        grid=(m // tm, n // tn, nk),
        in_specs=[a_spec, b_spec],
        out_specs=pl.BlockSpec((tm, tn), lambda i, j, kk: (out_rows(i), j)),
        scratch_shapes=[pltpu.VMEM((tm, tn), F32)],
        name=name,
        compiler_params=_params(("parallel", "parallel", "arbitrary")),
    )(a, b)


def _row_spec(tl, w, col=0):
    return pl.BlockSpec((tl, w), lambda i, col=col: (i, col))


def _vec_spec(w, col=0):
    return pl.BlockSpec((1, w), lambda i, col=col: (0, col))


def _norm_mod(x, gain, mod, sc_col, sh_col, name):
    l, d = x.shape
    tl = _tile(l, 256, 8)

    def body(x_ref, g_ref, sc_ref, sh_ref, h_ref):
        xv = x_ref[...]
        r = lax.rsqrt(jnp.mean(xv * xv, axis=-1, keepdims=True) + RMS_EPS)
        h_ref[...] = ((xv * r) * g_ref[...] * (1.0 + sc_ref[...]) + sh_ref[...]).astype(BF16)

    return pl.pallas_call(
        body,
        out_shape=jax.ShapeDtypeStruct((l, d), BF16),
        grid=(l // tl,),
        in_specs=[_row_spec(tl, d), _vec_spec(d), _vec_spec(d, sc_col), _vec_spec(d, sh_col)],
        out_specs=_row_spec(tl, d),
        name=name,
        compiler_params=_params(("parallel",)),
    )(x, gain, mod, mod)


def _resid_norm_mod(x, mo, gain, mod, gate_col, sc_col, sh_col, name):
    l, d = x.shape
    tl = _tile(l, 256, 8)

    def body(x_ref, mo_ref, g_ref, gate_ref, sc_ref, sh_ref, x2_ref, h_ref):
        xv = x_ref[...] + gate_ref[...] * mo_ref[...]
        x2_ref[...] = xv
        r = lax.rsqrt(jnp.mean(xv * xv, axis=-1, keepdims=True) + RMS_EPS)
        h_ref[...] = ((xv * r) * g_ref[...] * (1.0 + sc_ref[...]) + sh_ref[...]).astype(BF16)

    return pl.pallas_call(
        body,
        out_shape=(jax.ShapeDtypeStruct((l, d), F32), jax.ShapeDtypeStruct((l, d), BF16)),
        grid=(l // tl,),
        in_specs=[_row_spec(tl, d), _row_spec(tl, d), _vec_spec(d), _vec_spec(d, gate_col), _vec_spec(d, sc_col),
                  _vec_spec(d, sh_col)],
        out_specs=(_row_spec(tl, d), _row_spec(tl, d)),
        name=name,
        compiler_params=_params(("parallel",)),
    )(x, mo, gain, mod, mod, mod)


def _final_loss(x2, f, mod, gate_col, final_g, target, name):
    l, d = x2.shape
    tl = _tile(l, 256, 8)

    def body(x2_ref, f_ref, gate_ref, fg_ref, t_ref, loss_ref, dfg_ref, dgate_ref, dx3_ref, df_ref):
        i = pl.program_id(0)
        fv = f_ref[...]
        x3 = x2_ref[...] + gate_ref[...] * fv
        r = lax.rsqrt(jnp.mean(x3 * x3, axis=-1, keepdims=True) + RMS_EPS)
        xh = x3 * r
        err = xh * fg_ref[...] - t_ref[...]
        part = 0.5 * jnp.sum(jnp.mean(err * err, axis=-1, keepdims=True), axis=0, keepdims=True)
        dout = err * (1.0 / d)
        dxh = dout * fg_ref[...]
        dx3 = r * (dxh - xh * jnp.mean(dxh * xh, axis=-1, keepdims=True))
        dx3_ref[...] = dx3
        df_ref[...] = (gate_ref[...] * dx3).astype(BF16)

        @pl.when(i == 0)
        def _():
            loss_ref[...] = jnp.zeros_like(loss_ref)
            dfg_ref[...] = jnp.zeros_like(dfg_ref)
            dgate_ref[...] = jnp.zeros_like(dgate_ref)

        loss_ref[...] += jnp.broadcast_to(part, loss_ref.shape)
        dfg_ref[...] += jnp.sum(dout * xh, axis=0, keepdims=True)
        dgate_ref[...] += jnp.sum(dx3 * fv, axis=0, keepdims=True)

    vec = pl.BlockSpec((1, d), lambda i: (0, 0))
    return pl.pallas_call(
        body,
        out_shape=(jax.ShapeDtypeStruct((1, LANES), F32), jax.ShapeDtypeStruct((1, d), F32),
                   jax.ShapeDtypeStruct((1, d), F32), jax.ShapeDtypeStruct((l, d), F32),
                   jax.ShapeDtypeStruct((l, d), BF16)),
        grid=(l // tl,),
        in_specs=[_row_spec(tl, d), _row_spec(tl, d), _vec_spec(d, gate_col), vec, _row_spec(tl, d)],
        out_specs=(pl.BlockSpec((1, LANES), lambda i: (0, 0)), vec, vec, _row_spec(tl, d), _row_spec(tl, d)),
        name=name,
        compiler_params=_params(("arbitrary",)),
    )(x2, f, mod, final_g, target)


def _norm_mod_bwd(dh, x, dx_res, gain, mod, sc_col, name, branch=None, gate_col=None):
    l, d = x.shape
    tl = _tile(l, 256, 8)
    with_gate = branch is not None

    def body(*refs):
        if with_gate:
            dh_ref, x_ref, dr_ref, g_ref, sc_ref, br_ref, gate_ref, dx_ref, dsh_ref, dsc_ref, dg_ref, dm_ref, dgate_ref = refs
        else:
            dh_ref, x_ref, dr_ref, g_ref, sc_ref, dx_ref, dsh_ref, dsc_ref, dg_ref = refs
        i = pl.program_id(0)
        xv = x_ref[...]
        dhv = dh_ref[...].astype(F32)
        r = lax.rsqrt(jnp.mean(xv * xv, axis=-1, keepdims=True) + RMS_EPS)
        xh = xv * r
        dn = dhv * (1.0 + sc_ref[...])
        dxh = dn * g_ref[...]
        dx = dr_ref[...] + r * (dxh - xh * jnp.mean(dxh * xh, axis=-1, keepdims=True))
        dx_ref[...] = dx

        @pl.when(i == 0)
        def _():
            dsh_ref[...] = jnp.zeros_like(dsh_ref)
            dsc_ref[...] = jnp.zeros_like(dsc_ref)
            dg_ref[...] = jnp.zeros_like(dg_ref)
            if with_gate:
                dgate_ref[...] = jnp.zeros_like(dgate_ref)

        dsh_ref[...] += jnp.sum(dhv, axis=0, keepdims=True)
        dsc_ref[...] += jnp.sum(dhv * (xh * g_ref[...]), axis=0, keepdims=True)
        dg_ref[...] += jnp.sum(dn * xh, axis=0, keepdims=True)
        if with_gate:
            dm_ref[...] = (gate_ref[...] * dx).astype(BF16)
            dgate_ref[...] += jnp.sum(dx * br_ref[...], axis=0, keepdims=True)

    vec = pl.BlockSpec((1, d), lambda i: (0, 0))
    in_specs = [_row_spec(tl, d), _row_spec(tl, d), _row_spec(tl, d), vec, _vec_spec(d, sc_col)]
    args = [dh, x, dx_res, gain, mod]
    out_shape = [jax.ShapeDtypeStruct((l, d), F32)] + [jax.ShapeDtypeStruct((1, d), F32)] * 3
    out_specs = [_row_spec(tl, d), vec, vec, vec]
    if with_gate:
        in_specs += [_row_spec(tl, d), _vec_spec(d, gate_col)]
        args += [branch, mod]
        out_shape += [jax.ShapeDtypeStruct((l, d), BF16), jax.ShapeDtypeStruct((1, d), F32)]
        out_specs += [_row_spec(tl, d), vec]
    return pl.pallas_call(
        body, out_shape=tuple(out_shape), grid=(l // tl,), in_specs=in_specs, out_specs=tuple(out_specs),
        name=name, compiler_params=_params(("arbitrary",)),
    )(*args)


def _attn_mask(n, rows):
    del rows
    qi = lax.broadcasted_iota(jnp.int32, (ATTN_BLOCK, 2 * ATTN_BLOCK), 0)
    kj = lax.broadcasted_iota(jnp.int32, (ATTN_BLOCK, 2 * ATTN_BLOCK), 1)
    rel = qi + ATTN_BLOCK - kj
    return jnp.where((rel >= 0) & (rel < ATTN_BLOCK) & ((kj >= ATTN_BLOCK) | (n > 0)), 0.0, NEG_INF)


def _attn_probs(qs, kh, sink, mask):
    rows = qs.shape[0]
    s = lax.dot_general(qs, kh, (((1,), (1,)), ((), ())), preferred_element_type=F32) * (HEAD_DIM ** -0.5)
    s = s.reshape(-1, ATTN_BLOCK, 2 * ATTN_BLOCK) + mask[None]
    m = jnp.maximum(jnp.max(s, axis=-1, keepdims=True), sink)
    p = jnp.exp(s - m)
    es = jnp.exp(sink - m)
    inv = 1.0 / (jnp.sum(p, axis=-1, keepdims=True) + es)
    return (p * inv).reshape(rows, 2 * ATTN_BLOCK), (es * inv).reshape(rows, 1)


def _stack_heads(src_ref, dst_ref, g, qpk):
    for i in range(qpk):
        h = g * qpk + i
        dst_ref[i * ATTN_BLOCK:(i + 1) * ATTN_BLOCK, :] = src_ref[:, h * HEAD_DIM:(h + 1) * HEAD_DIM]


def _unstack_heads(val, dst_ref, g, qpk):
    for i in range(qpk):
        h = g * qpk + i
        dst_ref[:, h * HEAD_DIM:(h + 1) * HEAD_DIM] = val[i * ATTN_BLOCK:(i + 1) * ATTN_BLOCK, :].astype(dst_ref.dtype)


def _sink_column(sinks):
    return sinks.reshape(-1, 1, 1)


def _sink_spec(nq):
    return pl.BlockSpec((nq, 1, 1), lambda n: (0, 0, 0))


def _attn_specs(aw):
    kvb = aw // (2 * KV_WIDTH)
    q_spec = pl.BlockSpec((ATTN_BLOCK, aw), lambda n: (n, 0))
    kv_cur = pl.BlockSpec((ATTN_BLOCK, 2 * KV_WIDTH), lambda n: (n, kvb))
    kv_prev = pl.BlockSpec((ATTN_BLOCK, 2 * KV_WIDTH), lambda n: (jnp.maximum(n - 1, 0), kvb))
    return q_spec, kv_cur, kv_prev


def _attn_fwd(proj, sinks, aw, name):
    l = proj.shape[0]
    nq = aw // HEAD_DIM
    qpk = nq // N_KV_HEADS
    assert aw % (2 * KV_WIDTH) == 0

    rows = qpk * ATTN_BLOCK

    def body(q_ref, kvc_ref, kvp_ref, sink_ref, o_ref):
        n = pl.program_id(0)
        valid = _attn_mask(n, rows) == 0.0
        kv = jnp.concatenate([kvp_ref[...], kvc_ref[...]], axis=0)
        for h in range(nq):
            g = h // qpk
            qh = q_ref[:, h * HEAD_DIM:(h + 1) * HEAD_DIM]
            kh = kv[:, g * HEAD_DIM:(g + 1) * HEAD_DIM]
            vh = kv[:, KV_WIDTH + g * HEAD_DIM:KV_WIDTH + (g + 1) * HEAD_DIM]
            sink = sink_ref[0:1, h:h + 1]
            s = lax.dot_general(qh, kh, (((1,), (1,)), ((), ())), preferred_element_type=F32) * (HEAD_DIM ** -0.5)
            s = jnp.where(valid, s, NEG_INF)
            m = jnp.maximum(jnp.max(s, axis=-1, keepdims=True), sink)
            p = jnp.exp(s - m)
            p = p * (1.0 / (jnp.sum(p, axis=-1, keepdims=True) + jnp.exp(sink - m)))
            o = jnp.dot(p.astype(BF16), vh, preferred_element_type=F32)
            o_ref[:, h * HEAD_DIM:(h + 1) * HEAD_DIM] = o.astype(BF16)

    q_spec, kv_cur, kv_prev = _attn_specs(aw)
    return pl.pallas_call(
        body,
        out_shape=jax.ShapeDtypeStruct((l, aw), BF16),
        grid=(l // ATTN_BLOCK,),
        in_specs=[q_spec, kv_cur, kv_prev, pl.BlockSpec((1, nq), lambda n: (0, 0))],
        out_specs=pl.BlockSpec((ATTN_BLOCK, aw), lambda n: (n, 0)),
        name=name,
        compiler_params=_params(("parallel",)),
    )(proj, proj, proj, sinks)


def _attn_bwd(proj, sinks, dattn, aw, name):
    l = proj.shape[0]
    nq = aw // HEAD_DIM
    qpk = nq // N_KV_HEADS
    scale = HEAD_DIM ** -0.5

    rows = qpk * ATTN_BLOCK
    tn_dims = (((0,), (0,)), ((), ()))

    def body(q_ref, kvc_ref, kvp_ref, sink_ref, do_ref, dq_ref, dcur_ref, dprev_ref, dsink_ref, q_scr, do_scr):
        n = pl.program_id(0)
        mask = _attn_mask(n, rows)
        kv = jnp.concatenate([kvp_ref[...], kvc_ref[...]], axis=0)
        lane = lax.broadcasted_iota(jnp.int32, (1, nq), 1)
        dsink = jnp.zeros((1, nq), F32)
        dks, dvs = [], []
        for g in range(N_KV_HEADS):
            kh = kv[:, g * HEAD_DIM:(g + 1) * HEAD_DIM]
            vh = kv[:, KV_WIDTH + g * HEAD_DIM:KV_WIDTH + (g + 1) * HEAD_DIM]
            _stack_heads(q_ref, q_scr.at[g], g, qpk)
            _stack_heads(do_ref, do_scr.at[g], g, qpk)
            qs, dos = q_scr[g], do_scr[g]
            p, ps = _attn_probs(qs, kh, sink_ref[g * qpk:(g + 1) * qpk], mask)
            pb = p.astype(BF16)
            o = jnp.dot(pb, vh, preferred_element_type=F32)
            delta = jnp.sum(dos.astype(F32) * o, axis=-1, keepdims=True)
            dp = lax.dot_general(dos, vh, (((1,), (1,)), ((), ())), preferred_element_type=F32)
            ds = (p * (dp - delta)).astype(BF16)
            _unstack_heads(jnp.dot(ds, kh, preferred_element_type=F32) * scale, dq_ref, g, qpk)
            dks.append(lax.dot_general(ds, qs, tn_dims, preferred_element_type=F32) * scale)
            dvs.append(lax.dot_general(pb, dos, tn_dims, preferred_element_type=F32))
            t = ps * delta
            for i in range(qpk):
                part = -jnp.sum(t[i * ATTN_BLOCK:(i + 1) * ATTN_BLOCK, :], axis=0, keepdims=True)
                dsink += jnp.where(lane == g * qpk + i, part, 0.0)
        dkv = jnp.concatenate(dks + dvs, axis=1)
        dprev_ref[...] = dkv[:ATTN_BLOCK]
        dcur_ref[...] = dkv[ATTN_BLOCK:]

        @pl.when(n == 0)
        def _():
            dsink_ref[...] = jnp.zeros_like(dsink_ref)

        dsink_ref[...] += dsink

    q_spec, kv_cur, kv_prev = _attn_specs(aw)
    blk = pl.BlockSpec((ATTN_BLOCK, 2 * KV_WIDTH), lambda n: (n, 0))
    return pl.pallas_call(
        body,
        out_shape=(jax.ShapeDtypeStruct((l, aw), BF16), jax.ShapeDtypeStruct((l, 2 * KV_WIDTH), F32),
                   jax.ShapeDtypeStruct((l, 2 * KV_WIDTH), F32), jax.ShapeDtypeStruct((1, nq), F32)),
        grid=(l // ATTN_BLOCK,),
        in_specs=[q_spec, kv_cur, kv_prev, _sink_spec(nq),
                  pl.BlockSpec((ATTN_BLOCK, aw), lambda n: (n, 0))],
        out_specs=(pl.BlockSpec((ATTN_BLOCK, aw), lambda n: (n, 0)), blk, blk, pl.BlockSpec((1, nq), lambda n: (0, 0))),
        scratch_shapes=[pltpu.VMEM((N_KV_HEADS, rows, HEAD_DIM), BF16)] * 2,
        name=name,
        compiler_params=_params(("arbitrary",)),
    )(proj, proj, proj, _sink_column(sinks), dattn)


def _ssm_discretize(a_re, a_im, log_dt, b_re, b_im):
    dt = jnp.exp(log_dt)[:, None]
    mag = jnp.exp(a_re * dt)
    lr, li = mag * jnp.cos(a_im * dt), mag * jnp.sin(a_im * dt)
    den = a_re * a_re + a_im * a_im
    zr = ((lr - 1.0) * a_re + li * a_im) / den
    zi = (li * a_re - (lr - 1.0) * a_im) / den
    bbar_r = zr[:, :, None] * b_re - zi[:, :, None] * b_im
    bbar_i = zr[:, :, None] * b_im + zi[:, :, None] * b_re
    return lr, li, bbar_r, bbar_i


def _cmul(ar, ai, br, bi):
    return ar * br - ai * bi, ar * bi + ai * br


N_SEG = 8


def _cpow(ar, ai, n):
    out, br, bi = None, ar, ai
    while n:
        if n & 1:
            out = (br, bi) if out is None else _cmul(*out, br, bi)
        br, bi = _cmul(br, bi, br, bi)
        n >>= 1
    return out


def _scan_tables(lr, li, seg):
    lr, li = lr.reshape(1, -1), li.reshape(1, -1)
    row = jnp.arange(N_SEG)[:, None]
    ones = jnp.ones((N_SEG, 1), F32)
    fwd, bwd = [], []
    for d in (1, 2, 4):
        pr, pi = _cpow(lr, li, seg * d)
        fwd += [jnp.where(row >= d, pr, 0.0), jnp.where(row >= d, pi, 0.0)]
        bwd += [jnp.where(row < N_SEG - d, pr, 0.0), jnp.where(row < N_SEG - d, -pi, 0.0)]
    fwd += [ones * lr, ones * li]
    bwd += [ones * lr, ones * -li]
    return jnp.concatenate(fwd, 0), jnp.concatenate(bwd, 0)


def _pack_in(b):
    g, n, p = b.shape
    t = g // GROUPS_PER_TILE
    eye = jnp.eye(GROUPS_PER_TILE, dtype=b.dtype)
    bb = b.reshape(t, GROUPS_PER_TILE, n, p)
    return jnp.einsum("tgnp,gh->tgphn", bb, eye).reshape(t, GROUPS_PER_TILE * p, GROUPS_PER_TILE * n)


def _pack_out(c):
    g, p, n = c.shape
    t = g // GROUPS_PER_TILE
    eye = jnp.eye(GROUPS_PER_TILE, dtype=c.dtype)
    cc = c.reshape(t, GROUPS_PER_TILE, p, n)
    return jnp.einsum("tgpn,gh->tgnhp", cc, eye).reshape(t, GROUPS_PER_TILE * n, GROUPS_PER_TILE * p)


def _unpack_diag(x, n, p):
    t = x.shape[0]
    xx = x.reshape(t, GROUPS_PER_TILE, n, GROUPS_PER_TILE, p)
    eye = jnp.eye(GROUPS_PER_TILE, dtype=x.dtype)
    return jnp.einsum("tgnhp,gh->tgnp", xx, eye).reshape(t * GROUPS_PER_TILE, n, p)


def _scan_rows(hr_ref, hi_ref, tab_ref, l, reverse, prev_refs=None):
    w = hr_ref.shape[1]
    tabs = [tab_ref[pl.ds(8 * i, 8), :] for i in range(8)]
    nchunk = l // 8
    row = lax.broadcasted_iota(jnp.int32, (8, w), 0)

    def step(s, carry):
        k = nchunk - 1 - s if reverse else s
        t8 = pl.multiple_of(k * 8, 8)
        hr = hr_ref[pl.ds(t8, 8), :]
        hi = hi_ref[pl.ds(t8, 8), :]
        for idx, d in enumerate((1, 2, 4)):
            mr, mi = tabs[2 * idx], tabs[2 * idx + 1]
            shift = 8 - d if reverse else d
            sr = pltpu.roll(hr, shift, 0)
            si = pltpu.roll(hi, shift, 0)
            hr, hi = hr + mr * sr - mi * si, hi + mr * si + mi * sr
        cr, ci = carry[0], carry[1]
        hr, hi = hr + tabs[6] * cr - tabs[7] * ci, hi + tabs[6] * ci + tabs[7] * cr
        hr_ref[pl.ds(t8, 8), :] = hr
        hi_ref[pl.ds(t8, 8), :] = hi
        if not reverse:
            return hr[7:8, :], hi[7:8, :]
        out = (hr[0:1, :], hi[0:1, :])
        if prev_refs is None:
            return out
        fr_ref, fi_ref = prev_refs
        tp = pl.multiple_of(jnp.maximum(k - 1, 0) * 8, 8)
        keep = jnp.where(k > 0, 1.0, 0.0)
        lr_last = fr_ref[pl.ds(tp, 8), :][7:8, :] * keep
        li_last = fi_ref[pl.ds(tp, 8), :][7:8, :] * keep
        pr = jnp.where(row == 0, lr_last, pltpu.roll(fr_ref[pl.ds(t8, 8), :], 1, 0))
        pi = jnp.where(row == 0, li_last, pltpu.roll(fi_ref[pl.ds(t8, 8), :], 1, 0))
        return out + (carry[2] + hr * pr + hi * pi, carry[3] + hi * pr - hr * pi)

    zero = jnp.zeros((1, w), F32)
    init = (zero, zero)
    if reverse and prev_refs is not None:
        init += (jnp.zeros((8, w), F32), jnp.zeros((8, w), F32))
    return lax.fori_loop(0, nchunk, step, init)


def _seg_scan(hr_ref, hi_ref, tab_ref, l, reverse, states_refs=None):
    nq = hr_ref.shape[0]
    seg = l // N_SEG
    span = 8 * N_SEG
    nblk = seg // 8
    row = lax.broadcasted_iota(jnp.int32, (N_SEG, LANES), 0)

    def tab(r0, q):
        return tab_ref[r0:r0 + 8, q * LANES:(q + 1) * LANES]

    lam = [(tab(48, q), tab(56, q)) for q in range(nq)]

    def views(refs, q, jb):
        base = pl.multiple_of((nblk - 1 - jb if reverse else jb) * span, span)
        return [r.at[q, pl.ds(base, span), :] for r in refs]

    def local_rows():
        return range(7, -1, -1) if reverse else range(8)

    def at(r):
        return pl.ds(r * N_SEG, N_SEG)

    def pass1(jb, carry):
        hs = list(carry)
        for q in range(nq):
            vr, vi = views((hr_ref, hi_ref), q, jb)
            lr, li = lam[q]
            h_r, h_i = hs[2 * q], hs[2 * q + 1]
            for r in local_rows():
                h_r, h_i = lr * h_r - li * h_i + vr[at(r), :], lr * h_i + li * h_r + vi[at(r), :]
                vr[at(r), :] = h_r
                vi[at(r), :] = h_i
            hs[2 * q], hs[2 * q + 1] = h_r, h_i
        return tuple(hs)

    zero = jnp.zeros((N_SEG, LANES), F32)
    ends = lax.fori_loop(0, nblk, pass1, (zero,) * (2 * nq))

    carry_in = []
    for q in range(nq):
        er, ei = ends[2 * q], ends[2 * q + 1]
        for idx, d in enumerate((1, 2, 4)):
            mr, mi = tab(16 * idx, q), tab(16 * idx + 8, q)
            shift = N_SEG - d if reverse else d
            sr, si = pltpu.roll(er, shift, 0), pltpu.roll(ei, shift, 0)
            er, ei = er + mr * sr - mi * si, ei + mr * si + mi * sr
        if reverse:
            keep, shift = row < N_SEG - 1, N_SEG - 1
        else:
            keep, shift = row >= 1, 1
        carry_in += [jnp.where(keep, pltpu.roll(er, shift, 0), 0.0), jnp.where(keep, pltpu.roll(ei, shift, 0), 0.0)]

    with_acc = states_refs is not None

    def pass2(jb, carry):
        cs = list(carry)
        for q in range(nq):
            vr, vi = views((hr_ref, hi_ref), q, jb)
            lr, li = lam[q]
            d_r, d_i = cs[2 * q], cs[2 * q + 1]
            if with_acc:
                fr, fi = views(states_refs, q, jb)
                n_r, n_i, a_r, a_i = cs[2 * nq + 4 * q:2 * nq + 4 * q + 4]
            for r in local_rows():
                d_r, d_i = lr * d_r - li * d_i, lr * d_i + li * d_r
                g_r, g_i = vr[at(r), :] + d_r, vi[at(r), :] + d_i
                vr[at(r), :] = g_r
                vi[at(r), :] = g_i
                if with_acc:
                    p_r, p_i = fr[at(r), :], fi[at(r), :]
                    a_r, a_i = a_r + n_r * p_r + n_i * p_i, a_i + n_i * p_r - n_r * p_i
                    n_r, n_i = g_r, g_i
            cs[2 * q], cs[2 * q + 1] = d_r, d_i
            if with_acc:
                cs[2 * nq + 4 * q:2 * nq + 4 * q + 4] = [n_r, n_i, a_r, a_i]
        return tuple(cs)

    init = list(carry_in)
    if with_acc:
        for q in range(nq):
            init += [carry_in[2 * q], carry_in[2 * q + 1], zero, zero]
    out = lax.fori_loop(0, nblk, pass2, tuple(init))
    if with_acc:
        return [(out[2 * nq + 4 * q + 2], out[2 * nq + 4 * q + 3]) for q in range(nq)]
    return None


def _put_states(ref, rows, val):
    for q in range(ref.shape[0]):
        ref[q, rows, :] = val[:, q * LANES:(q + 1) * LANES]


def _get_states(ref, rows):
    return jnp.concatenate([ref[q, rows, :] for q in range(ref.shape[0])], axis=1)


def _s5_dims(sw):
    chan = GROUPS_PER_TILE * SSM_GROUP
    states = GROUPS_PER_TILE * SSM_STATE
    assert chan == LANES and sw % chan == 0
    return sw // chan, chan, states


def _interleave(x):
    l, w = x.shape
    return x.reshape(N_SEG, l // N_SEG, w).transpose(1, 0, 2).reshape(l, w)


def _deinterleave(x):
    l, w = x.shape
    return x.reshape(l // N_SEG, N_SEG, w).transpose(1, 0, 2).reshape(l, w)


def _s5_fwd(proj, u_off, packs, dvec, tab_f, sw, name):
    l = proj.shape[0]
    nt, chan, states = _s5_dims(sw)
    ch = _tile(l, 512, 8)
    ub = u_off // chan
    assert u_off % chan == 0

    def body(u_ref, br_ref, bi_ref, cr_ref, ci_ref, d_ref, tab_ref, y_ref, hr_ref, hi_ref):
        for i in range(l // ch):
            rows = pl.ds(i * ch, ch)
            u = u_ref[rows, :]
            _put_states(hr_ref, rows, jnp.dot(u, br_ref[0], preferred_element_type=F32))
            _put_states(hi_ref, rows, jnp.dot(u, bi_ref[0], preferred_element_type=F32))
        _seg_scan(hr_ref, hi_ref, tab_ref, l, reverse=False)
        for i in range(l // ch):
            rows = pl.ds(i * ch, ch)
            y = jnp.dot(_get_states(hr_ref, rows).astype(BF16), cr_ref[0], preferred_element_type=F32)
            y -= jnp.dot(_get_states(hi_ref, rows).astype(BF16), ci_ref[0], preferred_element_type=F32)
            y_ref[rows, :] = y + d_ref[...] * u_ref[rows, :].astype(F32)

    pin = pl.BlockSpec((1, chan, states), lambda t: (t, 0, 0))
    pout = pl.BlockSpec((1, states, chan), lambda t: (t, 0, 0))
    return pl.pallas_call(
        body,
        out_shape=jax.ShapeDtypeStruct((l, sw), F32),
        grid=(nt,),
        in_specs=[pl.BlockSpec((l, chan), lambda t: (0, ub + t)), pin, pin, pout, pout,
                  pl.BlockSpec((1, chan), lambda t: (0, t)), pl.BlockSpec((64, states), lambda t: (0, t))],
        out_specs=pl.BlockSpec((l, chan), lambda t: (0, t)),
        scratch_shapes=[pltpu.VMEM((states // LANES, l, LANES), F32)] * 2,
        name=name,
        compiler_params=_params(("parallel",)),
    )(proj, packs["br"], packs["bi"], packs["cr"], packs["ci"], dvec, tab_f)


def _s5_bwd(proj, u_off, dy, packs, dvec, tab_f, tab_b, sw, name):
    l = proj.shape[0]
    nt, chan, states = _s5_dims(sw)
    ch = _tile(l, 512, 8)
    ub = u_off // chan
    tn_dims = (((0,), (0,)), ((), ()))

    def body(u_ref, dy_ref, br_ref, bi_ref, brt_ref, bit_ref, crt_ref, cit_ref, d_ref, tabf_ref, tabb_ref,
             du_ref, dlam_ref, dbr_ref, dbi_ref, dcr_ref, dci_ref, dd_ref, hr_ref, hi_ref, gr_ref, gi_ref):
        for i in range(l // ch):
            rows = pl.ds(i * ch, ch)
            u = u_ref[rows, :]
            _put_states(hr_ref, rows, jnp.dot(u, br_ref[0], preferred_element_type=F32))
            _put_states(hi_ref, rows, jnp.dot(u, bi_ref[0], preferred_element_type=F32))
            dyv = dy_ref[rows, :]
            _put_states(gr_ref, rows, jnp.dot(dyv, crt_ref[0], preferred_element_type=F32))
            _put_states(gi_ref, rows, -jnp.dot(dyv, cit_ref[0], preferred_element_type=F32))
        _seg_scan(hr_ref, hi_ref, tabf_ref, l, reverse=False)
        accs = _seg_scan(gr_ref, gi_ref, tabb_ref, l, reverse=True, states_refs=(hr_ref, hi_ref))
        dlam_ref[...] = jnp.concatenate(
            [jnp.concatenate([jnp.sum(a[0], axis=0, keepdims=True) for a in accs], axis=1),
             jnp.concatenate([jnp.sum(a[1], axis=0, keepdims=True) for a in accs], axis=1), jnp.zeros((6, states), F32)], axis=0)
        dbr_ref[...] = jnp.zeros_like(dbr_ref)
        dbi_ref[...] = jnp.zeros_like(dbi_ref)
        dcr_ref[...] = jnp.zeros_like(dcr_ref)
        dci_ref[...] = jnp.zeros_like(dci_ref)
        dd = jnp.zeros((1, chan), F32)
        for i in range(l // ch):
            rows = pl.ds(i * ch, ch)
            u = u_ref[rows, :]
            dyv = dy_ref[rows, :]
            grb = _get_states(gr_ref, rows).astype(BF16)
            gib = _get_states(gi_ref, rows).astype(BF16)
            dbr_ref[0] += lax.dot_general(grb, u, tn_dims, preferred_element_type=F32)
            dbi_ref[0] += lax.dot_general(gib, u, tn_dims, preferred_element_type=F32)
            dcr_ref[0] += lax.dot_general(_get_states(hr_ref, rows).astype(BF16), dyv, tn_dims, preferred_element_type=F32)
            dci_ref[0] -= lax.dot_general(_get_states(hi_ref, rows).astype(BF16), dyv, tn_dims, preferred_element_type=F32)
            du = jnp.dot(grb, brt_ref[0], preferred_element_type=F32) + jnp.dot(gib, bit_ref[0], preferred_element_type=F32)
            dyf = dyv.astype(F32)
            du_ref[rows, :] = (du + d_ref[...] * dyf).astype(BF16)
            dd += jnp.sum(dyf * u.astype(F32), axis=0, keepdims=True)
        dd_ref[...] = dd

    pin = pl.BlockSpec((1, chan, states), lambda t: (t, 0, 0))
    pout = pl.BlockSpec((1, states, chan), lambda t: (t, 0, 0))
    seq = pl.BlockSpec((l, chan), lambda t: (0, t))
    tab = pl.BlockSpec((64, states), lambda t: (0, t))
    vec = pl.BlockSpec((1, chan), lambda t: (0, t))
    pack_shape = jax.ShapeDtypeStruct((nt, states, chan), F32)
    return pl.pallas_call(
        body,
        out_shape=(jax.ShapeDtypeStruct((l, sw), BF16), jax.ShapeDtypeStruct((8, nt * states), F32),
                   pack_shape, pack_shape, pack_shape, pack_shape, jax.ShapeDtypeStruct((1, sw), F32)),
        grid=(nt,),
        in_specs=[pl.BlockSpec((l, chan), lambda t: (0, ub + t)), seq, pin, pin, pout, pout, pin, pin, vec, tab, tab],
        out_specs=(seq, pl.BlockSpec((8, states), lambda t: (0, t)), pout, pout, pout, pout, vec),
        scratch_shapes=[pltpu.VMEM((states // LANES, l, LANES), F32)] * 4,
        name=name,
        compiler_params=_params(("parallel",)),
    )(proj, dy, packs["br"], packs["bi"], packs["brt"], packs["bit"], packs["crt"], packs["cit"], dvec, tab_f, tab_b)


GELU_K = math.sqrt(2.0 / math.pi)
GELU_C = 0.044715


def _gelu(y, name):
    l, w = y.shape
    tl = _tile(l, 512, 8)

    def body(y_ref, o_ref):
        v = y_ref[...]
        o_ref[...] = (0.5 * v * (1.0 + jnp.tanh(GELU_K * (v + GELU_C * v * v * v)))).astype(BF16)

    return pl.pallas_call(body, out_shape=jax.ShapeDtypeStruct((l, w), BF16), grid=(l // tl,),
                          in_specs=[_row_spec(tl, w)], out_specs=_row_spec(tl, w), name=name,
                          compiler_params=_params(("parallel",)))(y)


def _gelu_bwd(y, dg, name):
    l, w = y.shape
    tl = _tile(l, 512, 8)

    def body(y_ref, dg_ref, o_ref):
        v = y_ref[...]
        t = jnp.tanh(GELU_K * (v + GELU_C * v * v * v))
        grad = 0.5 * (1.0 + t) + 0.5 * v * (1.0 - t * t) * GELU_K * (1.0 + 3.0 * GELU_C * v * v)
        o_ref[...] = (dg_ref[...].astype(F32) * grad).astype(BF16)

    return pl.pallas_call(body, out_shape=jax.ShapeDtypeStruct((l, w), BF16), grid=(l // tl,),
                          in_specs=[_row_spec(tl, w), _row_spec(tl, w)], out_specs=_row_spec(tl, w), name=name,
                          compiler_params=_params(("parallel",)))(y, dg)


def _mix(ga, gs, attn_out, glu, name):
    l, d = ga.shape
    tl = _tile(l, 256, 16)

    def body(ga_ref, gs_ref, a_ref, u_ref, o_ref):
        ssm = u_ref[:, :d].astype(F32) * _sigmoid(u_ref[:, d:].astype(F32))
        o_ref[...] = (_sigmoid(ga_ref[...].astype(F32)) * a_ref[...].astype(F32)
                      + _sigmoid(gs_ref[...].astype(F32)) * ssm).astype(BF16)

    return pl.pallas_call(
        body, out_shape=jax.ShapeDtypeStruct((l, d), BF16), grid=(l // tl,),
        in_specs=[_row_spec(tl, d), _row_spec(tl, d), _row_spec(tl, d), _row_spec(tl, 2 * d)], out_specs=_row_spec(tl, d),
        name=name, compiler_params=_params(("parallel",)),
    )(ga, gs, attn_out, glu)


def _mix_bwd(ga, gs, attn_out, glu, dmixed, name):
    l, d = ga.shape
    tl = _tile(l, 256, 16)

    def body(ga_ref, gs_ref, a_ref, u_ref, dm_ref, dga_ref, dgs_ref, da_ref, dglu_ref):
        dm = dm_ref[...].astype(F32)
        sa = _sigmoid(ga_ref[...].astype(F32))
        ss = _sigmoid(gs_ref[...].astype(F32))
        sb = _sigmoid(u_ref[:, d:].astype(F32))
        ua = u_ref[:, :d].astype(F32)
        dssm = dm * ss
        dga_ref[...] = (dm * a_ref[...].astype(F32) * sa * (1.0 - sa)).astype(BF16)
        da_ref[...] = (dm * sa).astype(BF16)
        dgs_ref[...] = (dm * (ua * sb) * ss * (1.0 - ss)).astype(BF16)
        dglu_ref[:, :d] = (dssm * sb).astype(BF16)
        dglu_ref[:, d:] = (dssm * ua * sb * (1.0 - sb)).astype(BF16)

    out = jax.ShapeDtypeStruct((l, d), BF16)
    return pl.pallas_call(
        body, out_shape=(out, out, out, jax.ShapeDtypeStruct((l, 2 * d), BF16)), grid=(l // tl,),
        in_specs=[_row_spec(tl, d), _row_spec(tl, d), _row_spec(tl, d), _row_spec(tl, 2 * d), _row_spec(tl, d)],
        out_specs=(_row_spec(tl, d), _row_spec(tl, d), _row_spec(tl, d), _row_spec(tl, 2 * d)), name=name,
        compiler_params=_params(("parallel",)),
    )(ga, gs, attn_out, glu, dmixed)


CONV_BLOCKS = 4
HALO = 16


def _shift_rows(v, k, head):
    row = lax.broadcasted_iota(jnp.int32, v.shape, 0)
    out = pltpu.roll(v, k, 0)
    for r in range(k):
        out = jnp.where(row == r, head[HALO - k + r:HALO - k + r + 1, :], out)
    return out


def _shift_rows_up(v, k, tail):
    n = v.shape[0]
    row = lax.broadcasted_iota(jnp.int32, v.shape, 0)
    out = pltpu.roll(v, n - k, 0)
    for r in range(k):
        out = jnp.where(row == n - k + r, tail[r:r + 1, :], out)
    return out


def _conv_gate(g, head, w_ref, b_ref):
    return w_ref[0:1, :] * _shift_rows(g, 2, head) + w_ref[1:2, :] * _shift_rows(g, 1, head) + w_ref[2:3, :] * g + b_ref[...]


def _conv_act(up, conv_w, conv_b, ff, cw, name):
    l = up.shape[0]
    tl = _tile(l, 256, HALO)
    nj = ff // cw
    hb = tl // HALO

    def body(g_ref, gp_ref, v_ref, w_ref, b_ref, o_ref):
        i = pl.program_id(0)
        head = gp_ref[...].astype(F32) * jnp.where(i > 0, 1.0, 0.0)
        gc = _conv_gate(g_ref[...].astype(F32), head, w_ref, b_ref)
        o_ref[...] = (gc * _sigmoid(gc) * v_ref[...].astype(F32)).astype(BF16)

    return pl.pallas_call(
        body, out_shape=jax.ShapeDtypeStruct((l, ff), BF16), grid=(l // tl, nj),
        in_specs=[pl.BlockSpec((tl, cw), lambda i, j: (i, 2 * j)),
                  pl.BlockSpec((HALO, cw), lambda i, j: (jnp.maximum(i * hb - 1, 0), 2 * j)),
                  pl.BlockSpec((tl, cw), lambda i, j: (i, 2 * j + 1)),
                  pl.BlockSpec((3, cw), lambda i, j: (0, j)), pl.BlockSpec((1, cw), lambda i, j: (0, j))],
        out_specs=pl.BlockSpec((tl, cw), lambda i, j: (i, j)), name=name,
        compiler_params=_params(("parallel", "parallel")),
    )(up, up, up, conv_w, conv_b)


def _conv_act_bwd(up, da, conv_w, conv_b, ff, cw, name):
    l = up.shape[0]
    tl = _tile(l, 256, HALO)
    nj = ff // cw
    hb = tl // HALO
    ni = l // tl

    def body(g_ref, gp_ref, gn_ref, v_ref, vn_ref, da_ref, dan_ref, w_ref, b_ref, dup_ref, dw_ref, db_ref):
        i = pl.program_id(0)
        g = g_ref[...].astype(F32)
        head = gp_ref[...].astype(F32) * jnp.where(i > 0, 1.0, 0.0)
        g1 = _shift_rows(g, 1, head)
        g2 = _shift_rows(g, 2, head)
        gc = w_ref[0:1, :] * g2 + w_ref[1:2, :] * g1 + w_ref[2:3, :] * g + b_ref[...]
        sg = _sigmoid(gc)
        dav = da_ref[...].astype(F32)
        dgc = dav * v_ref[...].astype(F32) * (sg * (1.0 + gc * (1.0 - sg)))
        gn = gn_ref[...].astype(F32)
        gcn = _conv_gate(gn, g[tl - HALO:, :], w_ref, b_ref)
        sgn = _sigmoid(gcn)
        dgcn = dan_ref[...].astype(F32) * vn_ref[...].astype(F32) * (sgn * (1.0 + gcn * (1.0 - sgn)))
        dgcn = dgcn * jnp.where(i < ni - 1, 1.0, 0.0)
        dgate = w_ref[2:3, :] * dgc + w_ref[1:2, :] * _shift_rows_up(dgc, 1, dgcn) + w_ref[0:1, :] * _shift_rows_up(dgc, 2, dgcn)
        dup_ref[:, :cw] = dgate.astype(BF16)
        dup_ref[:, cw:] = (dav * (gc * sg)).astype(BF16)
        zero = jnp.zeros((1, cw), F32)
        dw_ref[...] = jnp.concatenate(
            [jnp.sum(dgc * g2, axis=0, keepdims=True), jnp.sum(dgc * g1, axis=0, keepdims=True),
             jnp.sum(dgc * g, axis=0, keepdims=True)] + [zero] * 5, axis=0)
        db_ref[...] = jnp.concatenate([jnp.sum(dgc, axis=0, keepdims=True)] + [zero] * 7, axis=0)

    def cur(col):
        return pl.BlockSpec((tl, cw), lambda i, j, col=col: (i, 2 * j + col))

    def prev(col):
        return pl.BlockSpec((HALO, cw), lambda i, j, col=col: (jnp.maximum(i * hb - 1, 0), 2 * j + col))

    def nxt(col):
        return pl.BlockSpec((HALO, cw), lambda i, j, col=col: (jnp.minimum((i + 1) * hb, l // HALO - 1), 2 * j + col))

    part = jax.ShapeDtypeStruct((ni * 8, ff), F32)
    part_spec = pl.BlockSpec((8, cw), lambda i, j: (i, j))
    return pl.pallas_call(
        body, out_shape=(jax.ShapeDtypeStruct((l, 2 * ff), BF16), part, part), grid=(ni, nj),
        in_specs=[cur(0), prev(0), nxt(0), cur(1), nxt(1), pl.BlockSpec((tl, cw), lambda i, j: (i, j)),
                  pl.BlockSpec((HALO, cw), lambda i, j: (jnp.minimum((i + 1) * hb, l // HALO - 1), j)),
                  pl.BlockSpec((3, cw), lambda i, j: (0, j)), pl.BlockSpec((1, cw), lambda i, j: (0, j))],
        out_specs=(pl.BlockSpec((tl, 2 * cw), lambda i, j: (i, j)), part_spec, part_spec), name=name,
        compiler_params=_params(("parallel", "parallel")),
    )(up, up, up, up, up, da, da, conv_w, conv_b)


def _sum_rows8(parts, name):
    n8, w = parts.shape
    n = n8 // 8
    cw = _tile(w, 2048)

    def body(p_ref, o_ref):
        acc = p_ref[0:8, :]
        for k in range(1, n):
            acc = acc + p_ref[8 * k:8 * k + 8, :]
        o_ref[...] = acc

    return pl.pallas_call(body, out_shape=jax.ShapeDtypeStruct((8, w), F32), grid=(w // cw,),
                          in_specs=[pl.BlockSpec((n8, cw), lambda j: (0, j))], out_specs=pl.BlockSpec((8, cw), lambda j: (0, j)),
                          name=name, compiler_params=_params(("parallel",)))(parts)


def _ada_fwd(c_all, w_shard, b_shard, name):
    nb, d = c_all.shape
    n = w_shard.shape[1]
    tn = _tile(n, 512)

    def body(c_ref, w_ref, b_ref, o_ref):
        cv = c_ref[...]
        cond = (cv * _sigmoid(cv)).astype(BF16)
        o_ref[...] = jnp.dot(cond, w_ref[...].astype(BF16), preferred_element_type=F32) + b_ref[...]

    return pl.pallas_call(
        body, out_shape=jax.ShapeDtypeStruct((nb, n), F32), grid=(n // tn,),
        in_specs=[pl.BlockSpec((nb, d), lambda j: (0, 0)), pl.BlockSpec((d, tn), lambda j: (0, j)),
                  pl.BlockSpec((1, tn), lambda j: (0, j))],
        out_specs=pl.BlockSpec((nb, tn), lambda j: (0, j)), name=name, compiler_params=_params(("parallel",)),
    )(c_all, w_shard, b_shard)


def _adam_update(w, g, m, v):
    m2 = ADAM_B1 * m + (1.0 - ADAM_B1) * g
    v2 = ADAM_B2 * v + (1.0 - ADAM_B2) * (g * g)
    m_hat = m2 / (1.0 - ADAM_B1 ** ADAM_STEP)
    v_hat = v2 / (1.0 - ADAM_B2 ** ADAM_STEP)
    return -ADAM_LR * (m_hat / (jnp.sqrt(v_hat) + ADAM_EPS) + ADAM_WD * w), m2, v2


def _ada_bwd_adam(c_all_t, dmod_shard, w, m, v, name):
    d, nb = c_all_t.shape
    n = w.shape[1]
    tr, tn = _tile(d, 512, 8), _tile(n, 512)

    def body(c_ref, dm_ref, w_ref, m_ref, v_ref, g_ref, dl_ref, m2_ref, v2_ref):
        cv = c_ref[...]
        cond = cv * _sigmoid(cv)
        g = cond[:, 0:1] * dm_ref[0:1, :]
        for b in range(1, nb):
            g = g + cond[:, b:b + 1] * dm_ref[b:b + 1, :]
        g_ref[...] = g
        dl_ref[...], m2_ref[...], v2_ref[...] = _adam_update(w_ref[...], g, m_ref[...], v_ref[...])

    blk = pl.BlockSpec((tr, tn), lambda i, j: (i, j))
    out = jax.ShapeDtypeStruct((d, n), F32)
    return pl.pallas_call(
        body, out_shape=(out, out, out, out), grid=(d // tr, n // tn),
        in_specs=[pl.BlockSpec((tr, nb), lambda i, j: (i, 0)), pl.BlockSpec((nb, tn), lambda i, j: (0, j)), blk, blk, blk],
        out_specs=(blk, blk, blk, blk), name=name, compiler_params=_params(("parallel", "parallel")),
    )(c_all_t, dmod_shard, w, m, v)


def _adam(w, g, m, v, name):
    r, c = w.shape
    tr = _tile(r, 256, 8)

    def body(w_ref, g_ref, m_ref, v_ref, dl_ref, m2_ref, v2_ref):
        dl_ref[...], m2_ref[...], v2_ref[...] = _adam_update(w_ref[...], g_ref[...], m_ref[...], v_ref[...])

    blk = pl.BlockSpec((tr, c), lambda i: (i, 0))
    out = jax.ShapeDtypeStruct((r, c), F32)
    return pl.pallas_call(body, out_shape=(out, out, out), grid=(r // tr,), in_specs=[blk] * 4, out_specs=(blk,) * 3,
                          name=name, compiler_params=_params(("parallel",)))(w, g, m, v)


def _sum_devices(gathered, name):
    nd, r, c = gathered.shape
    tr = _tile(r, 64, 16)

    def body(g_ref, o_ref):
        acc = g_ref[0].astype(F32)
        for k in range(1, nd):
            acc = acc + g_ref[k].astype(F32)
        o_ref[...] = acc

    return pl.pallas_call(body, out_shape=jax.ShapeDtypeStruct((r, c), F32), grid=(r // tr,),
                          in_specs=[pl.BlockSpec((nd, tr, c), lambda i: (0, i, 0))], out_specs=pl.BlockSpec((tr, c), lambda i: (i, 0)),
                          name=name, compiler_params=_params(("parallel",)))(gathered)


def _place():
    x, y, c = lax.axis_index("x"), lax.axis_index("y"), lax.axis_index("c")
    chips = [(1 - x, y), (x, 1 - y), (1 - x, 1 - y)]
    return x, y, c, chips


def _all_gather8(block, name):
    m_per, n = block.shape

    def body(x_ref, out_ref, send_sems, recv_sems, local_sem):
        x, y, c, chips = _place()
        me, sibling = (x, y, c), (x, y, 1 - c)

        def rows(px, py, pc):
            return out_ref.at[pl.ds((4 * px + 2 * py + pc) * m_per, m_per), :]

        def copy(k, blk, to, src=None):
            return pltpu.make_async_remote_copy(
                src_ref=rows(*blk) if src is None else src, dst_ref=rows(*blk), send_sem=send_sems.at[k],
                recv_sem=recv_sems.at[k], device_id=to, device_id_type=MESH)

        mine = pltpu.make_async_copy(x_ref, rows(*me), local_sem)
        mine.start()
        first = [copy(0, me, sibling, src=x_ref)]
        first += [copy(1 + j, me, (*chip, c), src=x_ref) for j, chip in enumerate(chips)]
        for cp in first:
            cp.start()
        passed = [copy(4 + j, (*chip, c), sibling) for j, chip in enumerate(chips)]
        for j, chip in enumerate(chips):
            copy(1 + j, (*chip, c), me).wait_recv()
            passed[j].start()
        copy(0, sibling, me).wait_recv()
        for j, chip in enumerate(chips):
            copy(4 + j, (*chip, 1 - c), me).wait_recv()
        for cp in first + passed:
            cp.wait_send()
        mine.wait()

    return pl.pallas_call(
        body,
        out_shape=jax.ShapeDtypeStruct((N_DEV * m_per, n), block.dtype),
        in_specs=[pl.BlockSpec(memory_space=pltpu.VMEM)],
        out_specs=pl.BlockSpec(memory_space=pltpu.VMEM),
        scratch_shapes=[pltpu.SemaphoreType.DMA((7,)), pltpu.SemaphoreType.DMA((7,)), pltpu.SemaphoreType.DMA],
        name=name,
        compiler_params=pltpu.CompilerParams(vmem_limit_bytes=VMEM_LIMIT_BYTES),
    )(block)


ANY = pl.BlockSpec(memory_space=pl.ANY)


def _place_shard(shard, name, after=()):
    r, k = shard.shape
    tb = _tile(r, 512, 16)
    nb = r // tb
    chip = (2 * lax.axis_index("x") + lax.axis_index("y")).astype(jnp.int32).reshape(1)

    def body(j_ref, s_ref, *rest):
        rest[-1][...] = s_ref[...].astype(BF16)

    return pl.pallas_call(
        body, out_shape=jax.ShapeDtypeStruct((N_CHIPS * r, k), BF16),
        grid_spec=pltpu.PrefetchScalarGridSpec(
            num_scalar_prefetch=1, grid=(nb,),
            in_specs=[pl.BlockSpec((tb, k), lambda i, j_ref: (i, 0))] + [ANY] * len(after),
            out_specs=pl.BlockSpec((tb, k), lambda i, j_ref: (j_ref[0] * nb + i, 0))),
        name=name, compiler_params=_params(("parallel",)),
    )(chip, shard, *after)


HBM_SPEC = pl.BlockSpec(memory_space=pltpu.HBM)
SEM_SPEC = pl.BlockSpec(memory_space=pltpu.SEMAPHORE)
TOKEN_SPEC = pl.BlockSpec(memory_space=pltpu.VMEM)
SPLIT_COPY = pltpu.CompilerParams(has_side_effects=pltpu.SideEffectType.DATAFLOW_SIDE_EFFECTING)


def _in_hbm(arrays):
    return [pltpu.with_memory_space_constraint(a, pltpu.HBM) for a in arrays]


def _hbm_like(arrays):
    return tuple(pltpu.HBM(a.shape, a.dtype) for a in arrays)


def _token_shape():
    return jax.ShapeDtypeStruct((8, LANES), F32)


def _gathered_rows(buf, px, py, half):
    r = buf.shape[0] // N_CHIPS
    return buf.at[pl.ds(pl.multiple_of((2 * px + py) * r + half * (r // 2), 16), r // 2), :]


def _gather_start(groups, name):
    sizes = [len(g) for g in groups]
    flat = [b for g in groups for b in g]
    nb, ng = len(flat), len(groups)

    def body(*refs):
        bufs = refs[:nb]
        sems = refs[nb:nb + 2 * ng]
        token = refs[-1]
        x, y, c, chips = _place()
        pos = 0
        for gi, nw in enumerate(sizes):
            for k, chip in enumerate(chips):
                for w in range(nw):
                    mine = _gathered_rows(bufs[pos + w], x, y, c)
                    pltpu.make_async_remote_copy(src_ref=mine, dst_ref=mine, send_sem=sems[2 * gi].at[k * nw + w], recv_sem=sems[2 * gi + 1].at[k * nw + w],
                                                 device_id=(*chip, c), device_id_type=MESH).start()
            pos += nw
        token[...] = jnp.zeros_like(token)

    sem_shapes = tuple(pltpu.SemaphoreType.DMA((3 * n,)) for n in sizes for _ in range(2))
    outs = pl.pallas_call(
        body, name=name, out_shape=sem_shapes + _hbm_like(flat) + (_token_shape(),),
        in_specs=[HBM_SPEC] * nb, out_specs=(SEM_SPEC,) * (2 * ng) + (HBM_SPEC,) * nb + (TOKEN_SPEC,),
        input_output_aliases={i: 2 * ng + i for i in range(nb)}, compiler_params=SPLIT_COPY,
    )(*_in_hbm(flat))
    res, pos = [], 2 * ng
    for gi, n in enumerate(sizes):
        res.append((outs[2 * gi], outs[2 * gi + 1], list(outs[pos:pos + n])))
        pos += n
    return res, outs[-1]


def _gather_forward(bufs, ici_send, ici_recv, after, name):
    nw, na = len(bufs), len(after)

    def body(*refs):
        b = refs[:nw]
        isend, irecv = refs[nw], refs[nw + 1]
        dsend, drecv = refs[nw + 2 + na], refs[nw + 3 + na]
        x, y, c, chips = _place()
        for k, chip in enumerate(chips):
            for w in range(nw):
                landed = _gathered_rows(b[w], *chip, c)
                pltpu.make_async_remote_copy(src_ref=landed, dst_ref=landed, send_sem=isend.at[k * nw + w], recv_sem=irecv.at[k * nw + w],
                                             device_id=(*chip, c), device_id_type=MESH).wait_recv()
                pltpu.make_async_remote_copy(src_ref=landed, dst_ref=landed, send_sem=dsend.at[k * nw + w], recv_sem=drecv.at[k * nw + w],
                                             device_id=(x, y, 1 - c), device_id_type=MESH).start()
        for k, chip in enumerate(chips):
            for w in range(nw):
                mine = _gathered_rows(b[w], x, y, c)
                pltpu.make_async_remote_copy(src_ref=mine, dst_ref=mine, send_sem=isend.at[k * nw + w], recv_sem=irecv.at[k * nw + w],
                                             device_id=(*chip, c), device_id_type=MESH).wait_send()
        refs[-1][...] = jnp.zeros_like(refs[-1])

    sem = pltpu.SemaphoreType.DMA((3 * nw,))
    outs = pl.pallas_call(
        body, name=name, out_shape=(sem, sem) + _hbm_like(bufs) + (_token_shape(),),
        in_specs=[HBM_SPEC] * nw + [SEM_SPEC, SEM_SPEC] + [ANY] * na, out_specs=(SEM_SPEC, SEM_SPEC) + (HBM_SPEC,) * nw + (TOKEN_SPEC,),
        input_output_aliases={i: 2 + i for i in range(nw)}, compiler_params=SPLIT_COPY,
    )(*bufs, ici_send, ici_recv, *after)
    return outs[0], outs[1], list(outs[2:2 + nw]), outs[-1]


def _gather_finish(bufs, d2d_send, d2d_recv, name, after=()):
    nw = len(bufs)

    def body(*refs):
        b = refs[:nw]
        dsend, drecv = refs[nw], refs[nw + 1]
        x, y, c, chips = _place()
        for k, chip in enumerate(chips):
            for w in range(nw):
                theirs = _gathered_rows(b[w], *chip, 1 - c)
                pltpu.make_async_remote_copy(src_ref=theirs, dst_ref=theirs, send_sem=dsend.at[k * nw + w], recv_sem=drecv.at[k * nw + w],
                                             device_id=(x, y, 1 - c), device_id_type=MESH).wait_recv()
                passed = _gathered_rows(b[w], *chip, c)
                pltpu.make_async_remote_copy(src_ref=passed, dst_ref=passed, send_sem=dsend.at[k * nw + w], recv_sem=drecv.at[k * nw + w],
                                             device_id=(x, y, 1 - c), device_id_type=MESH).wait_send()

    outs = pl.pallas_call(
        body, name=name, out_shape=_hbm_like(bufs), in_specs=[HBM_SPEC] * nw + [SEM_SPEC, SEM_SPEC] + [ANY] * len(after),
        out_specs=(HBM_SPEC,) * nw, input_output_aliases={i: i for i in range(nw)}, compiler_params=SPLIT_COPY,
    )(*bufs, d2d_send, d2d_recv, *after)
    return list(outs)


def _scatter_start(partials, name):
    nw = len(partials)
    landing = [lax.empty((3,) + p.shape[1:], p.dtype) for p in partials]

    def body(*refs):
        src, land = refs[:nw], refs[nw:2 * nw]
        send_sems, recv_sems = refs[2 * nw], refs[2 * nw + 1]
        token = refs[-1]
        x, y, c, chips = _place()
        for k, chip in enumerate(chips):
            for w in range(nw):
                pltpu.make_async_remote_copy(src_ref=src[w].at[2 * chip[0] + chip[1]], dst_ref=land[w].at[k], send_sem=send_sems.at[k * nw + w],
                                             recv_sem=recv_sems.at[k * nw + w], device_id=(*chip, c), device_id_type=MESH).start()
        token[...] = jnp.zeros_like(token)

    sem = pltpu.SemaphoreType.DMA((3 * nw,))
    outs = pl.pallas_call(
        body, name=name, out_shape=(sem, sem) + _hbm_like(partials) + _hbm_like(landing) + (_token_shape(),),
        in_specs=[HBM_SPEC] * (2 * nw), out_specs=(SEM_SPEC, SEM_SPEC) + (HBM_SPEC,) * (2 * nw) + (TOKEN_SPEC,),
        input_output_aliases={i: 2 + i for i in range(2 * nw)}, compiler_params=SPLIT_COPY,
    )(*_in_hbm(partials), *_in_hbm(landing))
    return (outs[0], outs[1], list(outs[2:2 + nw]), list(outs[2 + nw:2 + 2 * nw])), outs[-1]


def _scatter_wait(started, after, name):
    send_sems, recv_sems, partials, landing = started
    nw = len(partials)

    def body(*refs):
        src, land = refs[:nw], refs[nw:2 * nw]
        ssem, rsem = refs[2 * nw], refs[2 * nw + 1]
        x, y, c, chips = _place()
        for k, chip in enumerate(chips):
            for w in range(nw):
                cp = pltpu.make_async_remote_copy(src_ref=src[w].at[2 * chip[0] + chip[1]], dst_ref=land[w].at[k], send_sem=ssem.at[k * nw + w],
                                                  recv_sem=rsem.at[k * nw + w], device_id=(*chip, c), device_id_type=MESH)
                cp.wait_send()
                cp.wait_recv()

    outs = pl.pallas_call(
        body, name=name, out_shape=_hbm_like(partials) + _hbm_like(landing),
        in_specs=[HBM_SPEC] * (2 * nw) + [SEM_SPEC, SEM_SPEC] + [ANY] * len(after), out_specs=(HBM_SPEC,) * (2 * nw),
        input_output_aliases={i: i for i in range(2 * nw)}, compiler_params=SPLIT_COPY,
    )(*partials, *landing, send_sems, recv_sems, *after)
    return list(outs[:nw]), list(outs[nw:])


def _swap_halves(grads, name, after=()):
    nw, na = len(grads), len(after)

    def body(*refs):
        ins, outs = refs[:nw], refs[nw + na:2 * nw + na]
        send_sems, recv_sems = refs[2 * nw + na:]
        x, y, c, _ = _place()
        copies = []
        for w in range(nw):
            r = grads[w].shape[0] // N_CHIPS
            h = r // 2
            for j in range(N_CHIPS):
                copies.append(pltpu.make_async_remote_copy(
                    src_ref=ins[w].at[pl.ds(pl.multiple_of(j * r + (1 - c) * h, 16), h), :], dst_ref=outs[w].at[pl.ds(j * h, h), :],
                    send_sem=send_sems.at[w, j], recv_sem=recv_sems.at[w, j], device_id=(x, y, 1 - c), device_id_type=MESH))
                copies[-1].start()
        for cp in copies:
            cp.wait()

    sem = pltpu.SemaphoreType.DMA((nw, N_CHIPS))
    return pl.pallas_call(
        body, out_shape=tuple(jax.ShapeDtypeStruct((g.shape[0] // 2, g.shape[1]), g.dtype) for g in grads),
        in_specs=[ANY] * (nw + na), out_specs=(ANY,) * nw, scratch_shapes=[sem, sem], name=name,
    )(*grads, *after)


def _add_halves(grad, other, name):
    k = grad.shape[1]
    h = other.shape[0] // N_CHIPS
    tb = _tile(h, 512, 16)
    g4 = grad.reshape(N_CHIPS, 2, h, k)
    o3 = other.reshape(N_CHIPS, h, k)
    core = lax.axis_index("c").astype(jnp.int32).reshape(1)

    def body(c_ref, g_ref, o_ref, p_ref):
        p_ref[...] = (g_ref[...].astype(F32) + o_ref[...].astype(F32)).astype(BF16)

    return pl.pallas_call(
        body, out_shape=jax.ShapeDtypeStruct((N_CHIPS, h, k), BF16),
        grid_spec=pltpu.PrefetchScalarGridSpec(
            num_scalar_prefetch=1, grid=(N_CHIPS, h // tb),
            in_specs=[pl.BlockSpec((None, None, tb, k), lambda j, i, c_ref: (j, c_ref[0], i, 0)),
                      pl.BlockSpec((None, tb, k), lambda j, i, c_ref: (j, i, 0))],
            out_specs=pl.BlockSpec((None, tb, k), lambda j, i, c_ref: (j, i, 0))),
        name=name, compiler_params=_params(("parallel", "parallel")),
    )(core, g4, o3)


def _add_partials(partial, others, name):
    _, h, k = partial.shape
    tb = _tile(h, 512, 16)
    nb = h // tb
    place = jnp.stack([2 * lax.axis_index("x") + lax.axis_index("y"), lax.axis_index("c")]).astype(jnp.int32)

    def body(s_ref, p_ref, o0_ref, o1_ref, o2_ref, f_ref):
        f_ref[...] = ((p_ref[...].astype(F32) + o0_ref[...].astype(F32)) + o1_ref[...].astype(F32)) + o2_ref[...].astype(F32)

    def other(s):
        return pl.BlockSpec((None, tb, k), lambda i, s_ref, s=s: (s, i, 0))

    return pl.pallas_call(
        body, out_shape=jax.ShapeDtypeStruct((2 * h, k), F32),
        grid_spec=pltpu.PrefetchScalarGridSpec(
            num_scalar_prefetch=1, grid=(nb,),
            in_specs=[pl.BlockSpec((None, tb, k), lambda i, s_ref: (s_ref[0], i, 0)), other(0), other(1), other(2)],
            out_specs=pl.BlockSpec((tb, k), lambda i, s_ref: (s_ref[1] * nb + i, 0))),
        name=name, compiler_params=_params(("parallel",)),
    )(place, partial, others, others, others)


def _share_halves(fulls, name):
    nw = len(fulls)

    def body(*refs):
        ins, outs = refs[:nw], refs[nw:2 * nw]
        send_sems, recv_sems = refs[2 * nw:]
        x, y, c, _ = _place()
        copies = []
        for w in range(nw):
            h = fulls[w].shape[0] // 2
            start = pl.multiple_of(c * h, 8)
            copies.append(pltpu.make_async_remote_copy(
                src_ref=ins[w].at[pl.ds(start, h), :], dst_ref=outs[w].at[pl.ds(start, h), :], send_sem=send_sems.at[w],
                recv_sem=recv_sems.at[w], device_id=(x, y, 1 - c), device_id_type=MESH))
            copies[-1].start()
        for cp in copies:
            cp.wait()

    sem = pltpu.SemaphoreType.DMA((nw,))
    return pl.pallas_call(
        body, out_shape=tuple(jax.ShapeDtypeStruct(f.shape, f.dtype) for f in fulls),
        in_specs=[ANY] * nw, out_specs=(ANY,) * nw, scratch_shapes=[sem, sem], name=name,
        input_output_aliases={w: w for w in range(nw)},
    )(*fulls)


def _forward_then_finish(started_group, after, tag):
    ici_send, ici_recv, bufs = started_group
    d2d_send, d2d_recv, bufs, _ = _gather_forward(bufs, ici_send, ici_recv, after, f"gather_forward_{tag}")
    return _gather_finish(bufs, d2d_send, d2d_recv, f"gather_finish_{tag}")


def _reduce_start(grads, tag, after=()):
    from_sibling = _swap_halves(grads, f"swap_halves_{tag}", after)
    chip_sums = [_add_halves(g, o, f"add_halves_{tag}_{i}") for i, (g, o) in enumerate(zip(grads, from_sibling))]
    return _scatter_start(chip_sums, f"scatter_start_{tag}")


def _reduce_finish(started, after, tag):
    chip_sums, from_chips = _scatter_wait(started, after, f"scatter_wait_{tag}")
    fulls = [_add_partials(p, o, f"add_partials_{tag}_{i}") for i, (p, o) in enumerate(zip(chip_sums, from_chips))]
    return _share_halves(fulls, f"share_halves_{tag}")


def _flatten_pad(parts, cols=SMALL_COLS):
    flat = jnp.concatenate([p.reshape(-1) for p in parts])
    rows = -(-flat.shape[0] // (16 * cols)) * 16
    return jnp.pad(flat, (0, rows * cols - flat.shape[0])).reshape(rows, cols)


def _split_flat(buf, shapes):
    flat = buf.reshape(-1)
    out, off = [], 0
    for s in shapes:
        n = math.prod(s)
        out.append(flat[off:off + n].reshape(s))
        off += n
    return out


def _ssm_setup(seq_len, ssm_a_re, ssm_a_im, ssm_log_dt, ssm_b_re, ssm_b_im, ssm_c_re, ssm_c_im):
    lam_r, lam_i, bbar_r, bbar_i = _ssm_discretize(ssm_a_re, ssm_a_im, ssm_log_dt, ssm_b_re, ssm_b_im)
    tab_f, tab_b = _scan_tables(lam_r, lam_i, seq_len // N_SEG)
    pk = {"br": _pack_in(bbar_r), "bi": _pack_in(bbar_i), "cr": _pack_out(ssm_c_re), "ci": _pack_out(ssm_c_im)}
    packs = {k: v.astype(BF16) for k, v in pk.items()}
    packs.update({"brt": jnp.swapaxes(packs["br"], 1, 2), "bit": jnp.swapaxes(packs["bi"], 1, 2),
                  "crt": jnp.swapaxes(packs["cr"], 1, 2), "cit": jnp.swapaxes(packs["ci"], 1, 2)})
    return packs, tab_f, tab_b


def _local_step(xs, target, mod, w_in_t, later_weights, ffn_grads_ready, norm_mix_g, attn_sinks, ssm, ssm_d, norm_ffn_g, conv_w_full,
                ffn_conv_b, final_g, aw, sw, ff):
    l, d = xs.shape
    u_off = aw + 2 * KV_WIDTH
    ga_off = u_off + sw
    gs_off = ga_off + d
    packs, tab_f, tab_b = ssm
    dvec = ssm_d.reshape(1, sw)

    h1 = _norm_mod(xs, norm_mix_g, mod, 1, 0, "norm_mod1")
    proj = _matmul(h1, w_in_t, "nt", "mm_in")
    attn = _attn_fwd(proj, attn_sinks, aw, "attn_fwd")
    u_il = _interleave(proj[:, u_off:u_off + sw])
    ys_il = _s5_fwd(u_il, 0, packs, dvec, tab_f, sw, "s5_fwd")
    gy = _deinterleave(_gelu(ys_il, "gelu"))
    (w_ap_t, w_glu_t, w_out_f), ffn_weights = later_weights((gy, attn))
    attn_out = _matmul(attn, w_ap_t, "nt", "mm_attn_proj")
    glu = _matmul(gy, w_glu_t, "nt", "mm_glu")
    g_attn, g_ssm = proj[:, ga_off:ga_off + d], proj[:, gs_off:gs_off + d]
    mixed = _mix(g_attn, g_ssm, attn_out, glu, "mix")
    mo = _matmul(mixed, w_out_f, "nn", "mm_out", out_dtype=F32)
    x2, h2 = _resid_norm_mod(xs, mo, norm_ffn_g, mod, 2, 4, 3, "resid_norm_mod2")
    w_up_t, w_down_f = ffn_weights((h2,))
    cw = ff // CONV_BLOCKS
    up = _matmul(h2, w_up_t, "nt", "mm_up", interleave=cw)
    act = _conv_act(up, conv_w_full, ffn_conv_b, ff, cw, "conv_act")
    fo = _matmul(act, w_down_f, "nn", "mm_down", out_dtype=F32)
    loss_part, d_final_g, d_gate2, dx3, dfo = _final_loss(x2, fo, mod, 5, final_g.reshape(1, d), target, "final_loss")

    dact = _matmul(dfo, w_down_f, "nt", "mm_down_dx")
    g_down = _matmul(act, dfo, "tn", "mm_down_dw")
    dup, dcw_parts, dcb_parts = _conv_act_bwd(up, dact, conv_w_full, ffn_conv_b, ff, cw, "conv_act_bwd")
    d_conv_w = _sum_rows8(dcw_parts, "sum_conv_w")[:3]
    d_conv_b = _sum_rows8(dcb_parts, "sum_conv_b")[:1]
    dh2 = _matmul(dup, w_up_t, "nn", "mm_up_dx", interleave=cw)
    g_up = _matmul(dup, h2, "tn", "mm_up_dw", interleave=cw)
    mod = ffn_grads_ready(g_up, g_down, mod)
    dx2, d_shift2, d_scale2, d_gain2, dmo, d_gate1 = _norm_mod_bwd(dh2, x2, dx3, norm_ffn_g, mod, 4, "norm_mod2_bwd", branch=mo, gate_col=2)
    dmixed = _matmul(dmo, w_out_f, "nt", "mm_out_dx")
    g_out = _matmul(mixed, dmo, "tn", "mm_out_dw")
    dga, dgs, dattn_out, dglu = _mix_bwd(g_attn, g_ssm, attn_out, glu, dmixed, "mix_bwd")
    dgy = _matmul(dglu, w_glu_t, "nn", "mm_glu_dx")
    g_glu = _matmul(dglu, gy, "tn", "mm_glu_dw")
    dys_il = _gelu_bwd(ys_il, _interleave(dgy), "gelu_bwd")
    du_il, dlam, dbr_p, dbi_p, dcr_p, dci_p, d_dvec = _s5_bwd(u_il, 0, dys_il, packs, dvec, tab_f, tab_b, sw, "s5_bwd")
    du = _deinterleave(du_il)
    dattn = _matmul(dattn_out, w_ap_t, "nn", "mm_attn_proj_dx")
    g_ap = _matmul(dattn_out, attn, "tn", "mm_attn_proj_dw")
    dq, dkv_cur, dkv_prev, d_sinks = _attn_bwd(proj, attn_sinks, dattn, aw, "attn_bwd")
    dkv = dkv_cur + jnp.concatenate([dkv_prev[ATTN_BLOCK:], jnp.zeros((ATTN_BLOCK, 2 * KV_WIDTH), F32)], axis=0)
    dproj = jnp.concatenate([dq, dkv.astype(BF16), du, dga, dgs], axis=1)
    dh1 = _matmul(dproj, w_in_t, "nn", "mm_in_dx")
    g_in = _matmul(dproj, h1, "tn", "mm_in_dw")
    grad_x, d_shift1, d_scale1, d_gain1 = _norm_mod_bwd(dh1, xs, dx2, norm_mix_g, mod, 1, "norm_mod1_bwd")

    dmod = jnp.concatenate([d_shift1, d_scale1, d_gate1, d_shift2, d_scale2, d_gate2], axis=1)
    small_parts = [dmod, d_gain1, d_sinks, dlam[0], dlam[1], _unpack_diag(dbr_p, SSM_STATE, SSM_GROUP),
                   _unpack_diag(dbi_p, SSM_STATE, SSM_GROUP), _unpack_diag(dcr_p, SSM_STATE, SSM_GROUP),
                   _unpack_diag(dci_p, SSM_STATE, SSM_GROUP), d_dvec, d_gain2, d_conv_b, d_conv_w, d_final_g]
    return loss_part, grad_x, [g_in, g_ap, g_glu, g_out], small_parts


def _kernel_impl(x, c, ada_w, ada_b, norm_mix_g, w_in, attn_sinks, w_attn_proj, ssm_a_re, ssm_a_im, ssm_log_dt, ssm_b_re, ssm_b_im,
                 ssm_c_re, ssm_c_im, ssm_d, w_ssm_glu, w_out, norm_ffn_g, w_ffn_up, ffn_conv_w, ffn_conv_b, w_ffn_down, final_g,
                 loss_target, ms, vs):
    ax, ay, ac = lax.axis_index("x"), lax.axis_index("y"), lax.axis_index("c")
    chip = 2 * ax + ay
    batch_row = 4 * ax + 2 * ay + ac
    d = x.shape[2]
    aw = w_attn_proj.shape[1]
    sw = w_ssm_glu.shape[1]
    ff = N_CHIPS * ffn_conv_w.shape[2]
    ngroups = sw // SSM_GROUP

    c_all = _all_gather8(jnp.pad(c, ((0, 7), (0, 0))), "gather_c").reshape(N_DEV, 8, d)[:, 0, :]
    ncol = ada_w.shape[2]
    b_shard = lax.dynamic_slice(ada_b, (0, chip * ncol), (1, ncol))
    mod_blk = _ada_fwd(c_all, ada_w[0], b_shard, "ada_fwd")
    mod_all = _all_gather8(mod_blk, "gather_mod").reshape(N_CHIPS, 2, 8, ncol)[:, 0]
    mod = lax.dynamic_slice(mod_all, (0, batch_row, 0), (N_CHIPS, 1, ncol)).reshape(1, 6 * d)

    conv_w_all = _all_gather8(jnp.pad(ffn_conv_w[0], ((0, 5), (0, 0))), "gather_conv_w")
    conv_w_full = conv_w_all.reshape(N_CHIPS, 2, 8, ff // N_CHIPS)[:, 0, :3].transpose(1, 0, 2).reshape(3, ff)
    placed_in = _place_shard(w_in[0].T.astype(BF16), "place_shard_0", after=(mod, conv_w_full))
    (first,), started_in = _gather_start([[placed_in]], "gather_start_w_in")
    shards = [w_attn_proj[0].T.astype(BF16), w_ssm_glu[0].T.astype(BF16), w_out[0], w_ffn_up[0].T.astype(BF16), w_ffn_down[0]]
    placed = [_place_shard(s, f"place_shard_{i + 1}", after=(started_in,)) for i, s in enumerate(shards)]
    (mixer, ffn), started = _gather_start([placed[:3], placed[3:]], "gather_start_rest")
    ssm = (ssm_a_re[0], ssm_a_im[0], ssm_log_dt[0], ssm_b_re[0], ssm_b_im[0], ssm_c_re[0], ssm_c_im[0], ssm_d[0])
    ssm_tables = _ssm_setup(x.shape[1], *ssm[:7])
    (w_in_t,) = _forward_then_finish(first, (started, ssm_tables[1], ssm_tables[2], *ssm_tables[0].values()), "w_in")
    mod = mod + (started_in[0:1, 0:1] + started[0:1, 0:1])

    def later_weights(after):
        m_send, m_recv, m_bufs, _ = _gather_forward(mixer[2], mixer[0], mixer[1], after, "gather_forward_mixer")
        f_send, f_recv, f_bufs, f_started = _gather_forward(ffn[2], ffn[0], ffn[1], after, "gather_forward_ffn")
        mixer_weights = _gather_finish(m_bufs, m_send, m_recv, "gather_finish_mixer", (f_started,))
        return mixer_weights, lambda later: _gather_finish(f_bufs, f_send, f_recv, "gather_finish_ffn", later)

    pending = {}

    def ffn_grads_ready(g_up, g_down, mod_now):
        pending["ffn"], token = _reduce_start([g_up, g_down], "ffn")
        return mod_now + token[0:1, 0:1]

    loss_part, grad_x, grads, small_parts = _local_step(
        x[0], loss_target[0], mod, w_in_t, later_weights, ffn_grads_ready, norm_mix_g, attn_sinks, ssm_tables, ssm[7], norm_ffn_g,
        conv_w_full, ffn_conv_b, final_g, aw, sw, ff)
    loss = lax.psum(loss_part[0, 0], ("x", "y", "c"))

    small_shapes = [p.shape for p in small_parts]
    part_buf = _flatten_pad(small_parts).astype(BF16)
    rows = part_buf.shape[0]
    gathered = _all_gather8(part_buf, "gather_small").reshape(N_DEV, rows, SMALL_COLS)
    pending["rest"], rest_token = _reduce_start(grads, "rest", after=(gathered,))
    gup_t, grad_w_down = _reduce_finish(pending["ffn"], (rest_token,), "ffn")
    grad_w_up = gup_t.T
    summed = _sum_devices(gathered, "sum_small")
    (s_dmod, s_gain1, s_sinks, s_lr, s_li, s_bbr, s_bbi, s_cr, s_ci, s_dd, s_gain2, s_cb, s_cw, s_fg) = _split_flat(summed, small_shapes)
    _, ssm_vjp = jax.vjp(_ssm_discretize, *ssm[:5])
    g_a_re, g_a_im, g_log_dt, g_b_re, g_b_im = ssm_vjp((s_lr.reshape(ngroups, SSM_STATE), s_li.reshape(ngroups, SSM_STATE), s_bbr, s_bbi))
    g_c_re, g_c_im = jnp.swapaxes(s_cr, 1, 2), jnp.swapaxes(s_ci, 1, 2)
    g_conv_w = lax.dynamic_slice(s_cw, (0, chip * (ff // N_CHIPS)), (3, ff // N_CHIPS))

    dmod_all = gathered.reshape(N_DEV, -1)[:, :6 * d].astype(F32)
    dmod_shard = lax.dynamic_slice(dmod_all, (0, chip * ncol), (N_DEV, ncol))
    ada_res = _ada_bwd_adam(c_all.T, dmod_shard, ada_w[0], ms["ada_w"][0], vs["ada_w"][0], "ada_bwd_adam")

    res = {"ada_w": tuple(o[None] for o in ada_res)}

    def adam_big(nm, w, g):
        res[nm] = (g[None],) + tuple(o[None] for o in _adam(w[0], g, ms[nm][0], vs[nm][0], "adam_" + nm))

    adam_big("w_ffn_up", w_ffn_up, grad_w_up)
    adam_big("w_ffn_down", w_ffn_down, grad_w_down)

    small = [("ada_b", ada_b, s_dmod), ("norm_mix_g", norm_mix_g, s_gain1), ("attn_sinks", attn_sinks, s_sinks),
             ("ssm_a_re", ssm_a_re, g_a_re), ("ssm_a_im", ssm_a_im, g_a_im), ("ssm_log_dt", ssm_log_dt, g_log_dt),
             ("ssm_b_re", ssm_b_re, g_b_re), ("ssm_b_im", ssm_b_im, g_b_im), ("ssm_c_re", ssm_c_re, g_c_re),
             ("ssm_c_im", ssm_c_im, g_c_im), ("ssm_d", ssm_d, s_dd), ("norm_ffn_g", norm_ffn_g, s_gain2),
             ("ffn_conv_w", ffn_conv_w, g_conv_w), ("ffn_conv_b", ffn_conv_b, s_cb), ("final_g", final_g, s_fg)]
    shapes = [t[1].shape for t in small]
    bufs = [_flatten_pad([t[1] for t in small]), _flatten_pad([t[2] for t in small]),
            _flatten_pad([ms[t[0]] for t in small]), _flatten_pad([vs[t[0]] for t in small])]
    s_delta, s_m, s_v = _adam(*bufs, "adam_small")
    for t, dl, m2, v2 in zip(small, _split_flat(s_delta, shapes), _split_flat(s_m, shapes), _split_flat(s_v, shapes)):
        res[t[0]] = (t[2].reshape(t[1].shape), dl, m2, v2)

    done = (s_delta, res["w_ffn_up"][1], res["w_ffn_down"][1], res["ada_w"][1])
    gi_t, gap_t, gglu_t, grad_w_out = _reduce_finish(pending["rest"], done, "rest")
    adam_big("w_in", w_in, gi_t.T)
    adam_big("w_attn_proj", w_attn_proj, gap_t.T)
    adam_big("w_ssm_glu", w_ssm_glu, gglu_t.T)
    adam_big("w_out", w_out, grad_w_out)

    outs = [loss, grad_x[None]]
    for i in range(4):
        outs += [res[nm][i] for nm in WEIGHT_ORDER]
    return tuple(outs)


WEIGHT_ORDER = ("ada_w", "ada_b", "norm_mix_g", "w_in", "attn_sinks", "w_attn_proj", "ssm_a_re", "ssm_a_im", "ssm_log_dt", "ssm_b_re",
                "ssm_b_im", "ssm_c_re", "ssm_c_im", "ssm_d", "w_ssm_glu", "w_out", "norm_ffn_g", "w_ffn_up", "ffn_conv_w", "ffn_conv_b",
                "w_ffn_down", "final_g")


def kernel(x, c, ada_w, ada_b, norm_mix_g, w_in, attn_sinks, w_attn_proj, ssm_a_re, ssm_a_im, ssm_log_dt, ssm_b_re, ssm_b_im, ssm_c_re, ssm_c_im, ssm_d, w_ssm_glu, w_out, norm_ffn_g, w_ffn_up, ffn_conv_w, ffn_conv_b, w_ffn_down, final_g, loss_target, m_ada_w, m_ada_b, m_norm_mix_g, m_w_in, m_attn_sinks, m_w_attn_proj, m_ssm_a_re, m_ssm_a_im, m_ssm_log_dt, m_ssm_b_re, m_ssm_b_im, m_ssm_c_re, m_ssm_c_im, m_ssm_d, m_w_ssm_glu, m_w_out, m_norm_ffn_g, m_w_ffn_up, m_ffn_conv_w, m_ffn_conv_b, m_w_ffn_down, m_final_g, v_ada_w, v_ada_b, v_norm_mix_g, v_w_in, v_attn_sinks, v_w_attn_proj, v_ssm_a_re, v_ssm_a_im, v_ssm_log_dt, v_ssm_b_re, v_ssm_b_im, v_ssm_c_re, v_ssm_c_im, v_ssm_d, v_w_ssm_glu, v_w_out, v_norm_ffn_g, v_w_ffn_up, v_ffn_conv_w, v_ffn_conv_b, v_w_ffn_down, v_final_g):
    ms = dict(zip(WEIGHT_ORDER, (m_ada_w, m_ada_b, m_norm_mix_g, m_w_in, m_attn_sinks, m_w_attn_proj, m_ssm_a_re, m_ssm_a_im, m_ssm_log_dt,
                                 m_ssm_b_re, m_ssm_b_im, m_ssm_c_re, m_ssm_c_im, m_ssm_d, m_w_ssm_glu, m_w_out, m_norm_ffn_g, m_w_ffn_up,
                                 m_ffn_conv_w, m_ffn_conv_b, m_w_ffn_down, m_final_g)))
    vs = dict(zip(WEIGHT_ORDER, (v_ada_w, v_ada_b, v_norm_mix_g, v_w_in, v_attn_sinks, v_w_attn_proj, v_ssm_a_re, v_ssm_a_im, v_ssm_log_dt,
                                 v_ssm_b_re, v_ssm_b_im, v_ssm_c_re, v_ssm_c_im, v_ssm_d, v_w_ssm_glu, v_w_out, v_norm_ffn_g, v_w_ffn_up,
                                 v_ffn_conv_w, v_ffn_conv_b, v_w_ffn_down, v_final_g)))
    return _kernel_impl(x, c, ada_w, ada_b, norm_mix_g, w_in, attn_sinks, w_attn_proj, ssm_a_re, ssm_a_im, ssm_log_dt, ssm_b_re, ssm_b_im,
                        ssm_c_re, ssm_c_im, ssm_d, w_ssm_glu, w_out, norm_ffn_g, w_ffn_up, ffn_conv_w, ffn_conv_b, w_ffn_down, final_g,
                        loss_target, ms, vs)
```

```python
import math

import jax
import jax.numpy as jnp
from jax import lax
from jax.experimental import pallas as pl
from jax.experimental.pallas import tpu as pltpu

F32 = jnp.float32
BF16 = jnp.bfloat16
MESH = pl.DeviceIdType.MESH

HEAD_DIM = 64
N_KV_HEADS = 2
KV_WIDTH = N_KV_HEADS * HEAD_DIM
ATTN_BLOCK = 128
NEG_INF = -1e30
SSM_GROUP = 16
SSM_STATE = 64
GROUPS_PER_TILE = 8
RMS_EPS = 1e-6
ADAM_LR = 0.001
ADAM_B1 = 0.9
ADAM_B2 = 0.999
ADAM_EPS = 1e-08
ADAM_WD = 0.01
ADAM_STEP = 10
N_CHIPS = 4
N_DEV = 8
VMEM_LIMIT_BYTES = 56 * 1024 * 1024
LANES = 128
SMALL_COLS = 1024


def _tile(dim, target, mult=LANES):
    if dim <= target:
        return dim
    for t in range(target // mult * mult, 0, -mult):
        if dim % t == 0:
            return t
    raise ValueError(f"no tile for {dim}")


def _params(sem=None):
    return pltpu.CompilerParams(dimension_semantics=sem, vmem_limit_bytes=VMEM_LIMIT_BYTES)


def _sigmoid(x):
    return 1.0 / (1.0 + jnp.exp(-x))


def _matmul(a, b, mode, name, out_dtype=BF16, tm=1536, tn=1536, tk=2048, interleave=None, after=()):
    if mode == "nn":
        (m, k), (k2, n) = a.shape, b.shape
    elif mode == "nt":
        (m, k), (n, k2) = a.shape, b.shape
    else:
        (k, m), (k2, n) = a.shape, b.shape
    assert k == k2, (a.shape, b.shape, mode)
    if interleave is not None:
        tn, tk, tm = (interleave, tk, tm) if mode == "nt" else (tn, interleave, tm) if mode == "nn" else (tn, tk, interleave)
        half = {"nt": n, "nn": k, "tn": m}[mode] // (2 * interleave)

        def perm(blk):
            return blk // 2 + (blk % 2) * half
    else:
        def perm(blk):
            return blk
    tm, tn, tk = _tile(m, tm), _tile(n, tn), _tile(k, tk)
    nk = k // tk
    if mode == "tn":
        a_spec = pl.BlockSpec((tk, tm), lambda i, j, kk: (kk, i))
    else:
        a_spec = pl.BlockSpec((tm, tk), lambda i, j, kk: (i, kk))
    if mode == "nt":
        b_spec = pl.BlockSpec((tn, tk), lambda i, j, kk: (perm(j), kk))
    elif mode == "nn":
        b_spec = pl.BlockSpec((tk, tn), lambda i, j, kk: (perm(kk), j))
    else:
        b_spec = pl.BlockSpec((tk, tn), lambda i, j, kk: (kk, j))
    out_rows = perm if mode == "tn" else (lambda blk: blk)
    dims = {"nn": (((1,), (0,)), ((), ())), "nt": (((1,), (1,)), ((), ())), "tn": (((0,), (0,)), ((), ()))}[mode]

    def body(a_ref, b_ref, *rest):
        o_ref, acc_ref = rest[-2:]
        kk = pl.program_id(2)

        @pl.when(kk == 0)
        def _():
            acc_ref[...] = jnp.zeros_like(acc_ref)

        acc_ref[...] += lax.dot_general(a_ref[...], b_ref[...], dims, preferred_element_type=F32)

        @pl.when(kk == nk - 1)
        def _():
            o_ref[...] = acc_ref[...].astype(o_ref.dtype)

    return pl.pallas_call(
        body,
        out_shape=jax.ShapeDtypeStruct((m, n), out_dtype),
        grid=(m // tm, n // tn, nk),
        in_specs=[a_spec, b_spec] + [pl.BlockSpec(memory_space=pl.ANY)] * len(after),
        out_specs=pl.BlockSpec((tm, tn), lambda i, j, kk: (out_rows(i), j)),
        scratch_shapes=[pltpu.VMEM((tm, tn), F32)],
        name=name,
        compiler_params=_params(("parallel", "parallel", "arbitrary")),
    )(a, b, *after)


def _row_spec(tl, w, col=0):
    return pl.BlockSpec((tl, w), lambda i, col=col: (i, col))


def _vec_spec(w, col=0):
    return pl.BlockSpec((1, w), lambda i, col=col: (0, col))


def _norm_mod(x, gain, mod, sc_col, sh_col, name):
    l, d = x.shape
    tl = _tile(l, 256, 8)

    def body(x_ref, g_ref, sc_ref, sh_ref, h_ref):
        xv = x_ref[...]
        r = lax.rsqrt(jnp.mean(xv * xv, axis=-1, keepdims=True) + RMS_EPS)
        h_ref[...] = ((xv * r) * g_ref[...] * (1.0 + sc_ref[...]) + sh_ref[...]).astype(BF16)

    return pl.pallas_call(
        body,
        out_shape=jax.ShapeDtypeStruct((l, d), BF16),
        grid=(l // tl,),
        in_specs=[_row_spec(tl, d), _vec_spec(d), _vec_spec(d, sc_col), _vec_spec(d, sh_col)],
        out_specs=_row_spec(tl, d),
        name=name,
        compiler_params=_params(("parallel",)),
    )(x, gain, mod, mod)


def _resid_norm_mod(x, mo, gain, mod, gate_col, sc_col, sh_col, name):
    l, d = x.shape
    tl = _tile(l, 256, 8)

    def body(x_ref, mo_ref, g_ref, gate_ref, sc_ref, sh_ref, x2_ref, h_ref):
        xv = x_ref[...] + gate_ref[...] * mo_ref[...]
        x2_ref[...] = xv
        r = lax.rsqrt(jnp.mean(xv * xv, axis=-1, keepdims=True) + RMS_EPS)
        h_ref[...] = ((xv * r) * g_ref[...] * (1.0 + sc_ref[...]) + sh_ref[...]).astype(BF16)

    return pl.pallas_call(
        body,
        out_shape=(jax.ShapeDtypeStruct((l, d), F32), jax.ShapeDtypeStruct((l, d), BF16)),
        grid=(l // tl,),
        in_specs=[_row_spec(tl, d), _row_spec(tl, d), _vec_spec(d), _vec_spec(d, gate_col), _vec_spec(d, sc_col),
                  _vec_spec(d, sh_col)],
        out_specs=(_row_spec(tl, d), _row_spec(tl, d)),
        name=name,
        compiler_params=_params(("parallel",)),
    )(x, mo, gain, mod, mod, mod)


def _final_loss(x2, f, mod, gate_col, final_g, target, name):
    l, d = x2.shape
    tl = _tile(l, 256, 8)

    def body(x2_ref, f_ref, gate_ref, fg_ref, t_ref, loss_ref, dfg_ref, dgate_ref, dx3_ref, df_ref):
        i = pl.program_id(0)
        fv = f_ref[...]
        x3 = x2_ref[...] + gate_ref[...] * fv
        r = lax.rsqrt(jnp.mean(x3 * x3, axis=-1, keepdims=True) + RMS_EPS)
        xh = x3 * r
        err = xh * fg_ref[...] - t_ref[...]
        part = 0.5 * jnp.sum(jnp.mean(err * err, axis=-1, keepdims=True), axis=0, keepdims=True)
        dout = err * (1.0 / d)
        dxh = dout * fg_ref[...]
        dx3 = r * (dxh - xh * jnp.mean(dxh * xh, axis=-1, keepdims=True))
        dx3_ref[...] = dx3
        df_ref[...] = (gate_ref[...] * dx3).astype(BF16)

        @pl.when(i == 0)
        def _():
            loss_ref[...] = jnp.zeros_like(loss_ref)
            dfg_ref[...] = jnp.zeros_like(dfg_ref)
            dgate_ref[...] = jnp.zeros_like(dgate_ref)

        loss_ref[...] += jnp.broadcast_to(part, loss_ref.shape)
        dfg_ref[...] += jnp.sum(dout * xh, axis=0, keepdims=True)
        dgate_ref[...] += jnp.sum(dx3 * fv, axis=0, keepdims=True)

    vec = pl.BlockSpec((1, d), lambda i: (0, 0))
    return pl.pallas_call(
        body,
        out_shape=(jax.ShapeDtypeStruct((1, LANES), F32), jax.ShapeDtypeStruct((1, d), F32),
                   jax.ShapeDtypeStruct((1, d), F32), jax.ShapeDtypeStruct((l, d), F32),
                   jax.ShapeDtypeStruct((l, d), BF16)),
        grid=(l // tl,),
        in_specs=[_row_spec(tl, d), _row_spec(tl, d), _vec_spec(d, gate_col), vec, _row_spec(tl, d)],
        out_specs=(pl.BlockSpec((1, LANES), lambda i: (0, 0)), vec, vec, _row_spec(tl, d), _row_spec(tl, d)),
        name=name,
        compiler_params=_params(("arbitrary",)),
    )(x2, f, mod, final_g, target)


def _norm_mod_bwd(dh, x, dx_res, gain, mod, sc_col, name, branch=None, gate_col=None):
    l, d = x.shape
    tl = _tile(l, 256, 8)
    with_gate = branch is not None

    def body(*refs):
        if with_gate:
            dh_ref, x_ref, dr_ref, g_ref, sc_ref, br_ref, gate_ref, dx_ref, dsh_ref, dsc_ref, dg_ref, dm_ref, dgate_ref = refs
        else:
            dh_ref, x_ref, dr_ref, g_ref, sc_ref, dx_ref, dsh_ref, dsc_ref, dg_ref = refs
        i = pl.program_id(0)
        xv = x_ref[...]
        dhv = dh_ref[...].astype(F32)
        r = lax.rsqrt(jnp.mean(xv * xv, axis=-1, keepdims=True) + RMS_EPS)
        xh = xv * r
        dn = dhv * (1.0 + sc_ref[...])
        dxh = dn * g_ref[...]
        dx = dr_ref[...] + r * (dxh - xh * jnp.mean(dxh * xh, axis=-1, keepdims=True))
        dx_ref[...] = dx

        @pl.when(i == 0)
        def _():
            dsh_ref[...] = jnp.zeros_like(dsh_ref)
            dsc_ref[...] = jnp.zeros_like(dsc_ref)
            dg_ref[...] = jnp.zeros_like(dg_ref)
            if with_gate:
                dgate_ref[...] = jnp.zeros_like(dgate_ref)

        dsh_ref[...] += jnp.sum(dhv, axis=0, keepdims=True)
        dsc_ref[...] += jnp.sum(dhv * (xh * g_ref[...]), axis=0, keepdims=True)
        dg_ref[...] += jnp.sum(dn * xh, axis=0, keepdims=True)
        if with_gate:
            dm_ref[...] = (gate_ref[...] * dx).astype(BF16)
            dgate_ref[...] += jnp.sum(dx * br_ref[...], axis=0, keepdims=True)

    vec = pl.BlockSpec((1, d), lambda i: (0, 0))
    in_specs = [_row_spec(tl, d), _row_spec(tl, d), _row_spec(tl, d), vec, _vec_spec(d, sc_col)]
    args = [dh, x, dx_res, gain, mod]
    out_shape = [jax.ShapeDtypeStruct((l, d), F32)] + [jax.ShapeDtypeStruct((1, d), F32)] * 3
    out_specs = [_row_spec(tl, d), vec, vec, vec]
    if with_gate:
        in_specs += [_row_spec(tl, d), _vec_spec(d, gate_col)]
        args += [branch, mod]
        out_shape += [jax.ShapeDtypeStruct((l, d), BF16), jax.ShapeDtypeStruct((1, d), F32)]
        out_specs += [_row_spec(tl, d), vec]
    return pl.pallas_call(
        body, out_shape=tuple(out_shape), grid=(l // tl,), in_specs=in_specs, out_specs=tuple(out_specs),
        name=name, compiler_params=_params(("arbitrary",)),
    )(*args)


def _attn_mask(n, rows):
    del rows
    qi = lax.broadcasted_iota(jnp.int32, (ATTN_BLOCK, 2 * ATTN_BLOCK), 0)
    kj = lax.broadcasted_iota(jnp.int32, (ATTN_BLOCK, 2 * ATTN_BLOCK), 1)
    rel = qi + ATTN_BLOCK - kj
    return jnp.where((rel >= 0) & (rel < ATTN_BLOCK) & ((kj >= ATTN_BLOCK) | (n > 0)), 0.0, NEG_INF)


def _attn_probs(qs, kh, sink, mask):
    rows = qs.shape[0]
    s = lax.dot_general(qs, kh, (((1,), (1,)), ((), ())), preferred_element_type=F32) * (HEAD_DIM ** -0.5)
    s = s.reshape(-1, ATTN_BLOCK, 2 * ATTN_BLOCK) + mask[None]
    m = jnp.maximum(jnp.max(s, axis=-1, keepdims=True), sink)
    p = jnp.exp(s - m)
    es = jnp.exp(sink - m)
    inv = 1.0 / (jnp.sum(p, axis=-1, keepdims=True) + es)
    return (p * inv).reshape(rows, 2 * ATTN_BLOCK), (es * inv).reshape(rows, 1)


def _stack_heads(src_ref, dst_ref, g, qpk):
    for i in range(qpk):
        h = g * qpk + i
        dst_ref[i * ATTN_BLOCK:(i + 1) * ATTN_BLOCK, :] = src_ref[:, h * HEAD_DIM:(h + 1) * HEAD_DIM]


def _unstack_heads(val, dst_ref, g, qpk):
    for i in range(qpk):
        h = g * qpk + i
        dst_ref[:, h * HEAD_DIM:(h + 1) * HEAD_DIM] = val[i * ATTN_BLOCK:(i + 1) * ATTN_BLOCK, :].astype(dst_ref.dtype)


def _sink_column(sinks):
    return sinks.reshape(-1, 1, 1)


def _sink_spec(nq):
    return pl.BlockSpec((nq, 1, 1), lambda n: (0, 0, 0))


def _attn_specs(aw):
    kvb = aw // (2 * KV_WIDTH)
    q_spec = pl.BlockSpec((ATTN_BLOCK, aw), lambda n: (n, 0))
    kv_cur = pl.BlockSpec((ATTN_BLOCK, 2 * KV_WIDTH), lambda n: (n, kvb))
    kv_prev = pl.BlockSpec((ATTN_BLOCK, 2 * KV_WIDTH), lambda n: (jnp.maximum(n - 1, 0), kvb))
    return q_spec, kv_cur, kv_prev


def _attn_fwd(proj, sinks, aw, name, after=()):
    l = proj.shape[0]
    nq = aw // HEAD_DIM
    qpk = nq // N_KV_HEADS
    assert aw % (2 * KV_WIDTH) == 0

    rows = qpk * ATTN_BLOCK

    def body(q_ref, kvc_ref, kvp_ref, sink_ref, *rest):
        o_ref = rest[-1]
        n = pl.program_id(0)
        valid = _attn_mask(n, rows) == 0.0
        kv = jnp.concatenate([kvp_ref[...], kvc_ref[...]], axis=0)
        for h in range(nq):
            g = h // qpk
            qh = q_ref[:, h * HEAD_DIM:(h + 1) * HEAD_DIM]
            kh = kv[:, g * HEAD_DIM:(g + 1) * HEAD_DIM]
            vh = kv[:, KV_WIDTH + g * HEAD_DIM:KV_WIDTH + (g + 1) * HEAD_DIM]
            sink = sink_ref[0:1, h:h + 1]
            s = lax.dot_general(qh, kh, (((1,), (1,)), ((), ())), preferred_element_type=F32) * (HEAD_DIM ** -0.5)
            s = jnp.where(valid, s, NEG_INF)
            m = jnp.maximum(jnp.max(s, axis=-1, keepdims=True), sink)
            p = jnp.exp(s - m)
            p = p * (1.0 / (jnp.sum(p, axis=-1, keepdims=True) + jnp.exp(sink - m)))
            o = jnp.dot(p.astype(BF16), vh, preferred_element_type=F32)
            o_ref[:, h * HEAD_DIM:(h + 1) * HEAD_DIM] = o.astype(BF16)

    q_spec, kv_cur, kv_prev = _attn_specs(aw)
    return pl.pallas_call(
        body,
        out_shape=jax.ShapeDtypeStruct((l, aw), BF16),
        grid=(l // ATTN_BLOCK,),
        in_specs=[q_spec, kv_cur, kv_prev, pl.BlockSpec((1, nq), lambda n: (0, 0))] + [pl.BlockSpec(memory_space=pl.ANY)] * len(after),
        out_specs=pl.BlockSpec((ATTN_BLOCK, aw), lambda n: (n, 0)),
        name=name,
        compiler_params=_params(("parallel",)),
    )(proj, proj, proj, sinks, *after)


def _attn_bwd(proj, sinks, dattn, aw, name):
    l = proj.shape[0]
    nq = aw // HEAD_DIM
    qpk = nq // N_KV_HEADS
    scale = HEAD_DIM ** -0.5

    rows = qpk * ATTN_BLOCK
    tn_dims = (((0,), (0,)), ((), ()))

    def body(q_ref, kvc_ref, kvp_ref, sink_ref, do_ref, dq_ref, dcur_ref, dprev_ref, dsink_ref, q_scr, do_scr):
        n = pl.program_id(0)
        mask = _attn_mask(n, rows)
        kv = jnp.concatenate([kvp_ref[...], kvc_ref[...]], axis=0)
        lane = lax.broadcasted_iota(jnp.int32, (1, nq), 1)
        dsink = jnp.zeros((1, nq), F32)
        dks, dvs = [], []
        for g in range(N_KV_HEADS):
            kh = kv[:, g * HEAD_DIM:(g + 1) * HEAD_DIM]
            vh = kv[:, KV_WIDTH + g * HEAD_DIM:KV_WIDTH + (g + 1) * HEAD_DIM]
            _stack_heads(q_ref, q_scr.at[g], g, qpk)
            _stack_heads(do_ref, do_scr.at[g], g, qpk)
            qs, dos = q_scr[g], do_scr[g]
            p, ps = _attn_probs(qs, kh, sink_ref[g * qpk:(g + 1) * qpk], mask)
            pb = p.astype(BF16)
            o = jnp.dot(pb, vh, preferred_element_type=F32)
            delta = jnp.sum(dos.astype(F32) * o, axis=-1, keepdims=True)
            dp = lax.dot_general(dos, vh, (((1,), (1,)), ((), ())), preferred_element_type=F32)
            ds = (p * (dp - delta)).astype(BF16)
            _unstack_heads(jnp.dot(ds, kh, preferred_element_type=F32) * scale, dq_ref, g, qpk)
            dks.append(lax.dot_general(ds, qs, tn_dims, preferred_element_type=F32) * scale)
            dvs.append(lax.dot_general(pb, dos, tn_dims, preferred_element_type=F32))
            t = ps * delta
            for i in range(qpk):
                part = -jnp.sum(t[i * ATTN_BLOCK:(i + 1) * ATTN_BLOCK, :], axis=0, keepdims=True)
                dsink += jnp.where(lane == g * qpk + i, part, 0.0)
        dkv = jnp.concatenate(dks + dvs, axis=1)
        dprev_ref[...] = dkv[:ATTN_BLOCK]
        dcur_ref[...] = dkv[ATTN_BLOCK:]

        @pl.when(n == 0)
        def _():
            dsink_ref[...] = jnp.zeros_like(dsink_ref)

        dsink_ref[...] += dsink

    q_spec, kv_cur, kv_prev = _attn_specs(aw)
    blk = pl.BlockSpec((ATTN_BLOCK, 2 * KV_WIDTH), lambda n: (n, 0))
    return pl.pallas_call(
        body,
        out_shape=(jax.ShapeDtypeStruct((l, aw), BF16), jax.ShapeDtypeStruct((l, 2 * KV_WIDTH), F32),
                   jax.ShapeDtypeStruct((l, 2 * KV_WIDTH), F32), jax.ShapeDtypeStruct((1, nq), F32)),
        grid=(l // ATTN_BLOCK,),
        in_specs=[q_spec, kv_cur, kv_prev, _sink_spec(nq),
                  pl.BlockSpec((ATTN_BLOCK, aw), lambda n: (n, 0))],
        out_specs=(pl.BlockSpec((ATTN_BLOCK, aw), lambda n: (n, 0)), blk, blk, pl.BlockSpec((1, nq), lambda n: (0, 0))),
        scratch_shapes=[pltpu.VMEM((N_KV_HEADS, rows, HEAD_DIM), BF16)] * 2,
        name=name,
        compiler_params=_params(("arbitrary",)),
    )(proj, proj, proj, _sink_column(sinks), dattn)


def _ssm_discretize(a_re, a_im, log_dt, b_re, b_im):
    dt = jnp.exp(log_dt)[:, None]
    mag = jnp.exp(a_re * dt)
    lr, li = mag * jnp.cos(a_im * dt), mag * jnp.sin(a_im * dt)
    den = a_re * a_re + a_im * a_im
    zr = ((lr - 1.0) * a_re + li * a_im) / den
    zi = (li * a_re - (lr - 1.0) * a_im) / den
    bbar_r = zr[:, :, None] * b_re - zi[:, :, None] * b_im
    bbar_i = zr[:, :, None] * b_im + zi[:, :, None] * b_re
    return lr, li, bbar_r, bbar_i


def _cmul(ar, ai, br, bi):
    return ar * br - ai * bi, ar * bi + ai * br


N_SEG = 8


def _cpow(ar, ai, n):
    out, br, bi = None, ar, ai
    while n:
        if n & 1:
            out = (br, bi) if out is None else _cmul(*out, br, bi)
        br, bi = _cmul(br, bi, br, bi)
        n >>= 1
    return out


def _scan_tables(lr, li, seg):
    lr, li = lr.reshape(1, -1), li.reshape(1, -1)
    row = jnp.arange(N_SEG)[:, None]
    ones = jnp.ones((N_SEG, 1), F32)
    fwd, bwd = [], []
    for d in (1, 2, 4):
        pr, pi = _cpow(lr, li, seg * d)
        fwd += [jnp.where(row >= d, pr, 0.0), jnp.where(row >= d, pi, 0.0)]
        bwd += [jnp.where(row < N_SEG - d, pr, 0.0), jnp.where(row < N_SEG - d, -pi, 0.0)]
    fwd += [ones * lr, ones * li]
    bwd += [ones * lr, ones * -li]
    return jnp.concatenate(fwd, 0), jnp.concatenate(bwd, 0)


def _pack_in(b):
    g, n, p = b.shape
    t = g // GROUPS_PER_TILE
    eye = jnp.eye(GROUPS_PER_TILE, dtype=b.dtype)
    bb = b.reshape(t, GROUPS_PER_TILE, n, p)
    return jnp.einsum("tgnp,gh->tgphn", bb, eye).reshape(t, GROUPS_PER_TILE * p, GROUPS_PER_TILE * n)


def _pack_out(c):
    g, p, n = c.shape
    t = g // GROUPS_PER_TILE
    eye = jnp.eye(GROUPS_PER_TILE, dtype=c.dtype)
    cc = c.reshape(t, GROUPS_PER_TILE, p, n)
    return jnp.einsum("tgpn,gh->tgnhp", cc, eye).reshape(t, GROUPS_PER_TILE * n, GROUPS_PER_TILE * p)


def _unpack_diag(x, n, p):
    t = x.shape[0]
    xx = x.reshape(t, GROUPS_PER_TILE, n, GROUPS_PER_TILE, p)
    eye = jnp.eye(GROUPS_PER_TILE, dtype=x.dtype)
    return jnp.einsum("tgnhp,gh->tgnp", xx, eye).reshape(t * GROUPS_PER_TILE, n, p)


def _scan_rows(hr_ref, hi_ref, tab_ref, l, reverse, prev_refs=None):
    w = hr_ref.shape[1]
    tabs = [tab_ref[pl.ds(8 * i, 8), :] for i in range(8)]
    nchunk = l // 8
    row = lax.broadcasted_iota(jnp.int32, (8, w), 0)

    def step(s, carry):
        k = nchunk - 1 - s if reverse else s
        t8 = pl.multiple_of(k * 8, 8)
        hr = hr_ref[pl.ds(t8, 8), :]
        hi = hi_ref[pl.ds(t8, 8), :]
        for idx, d in enumerate((1, 2, 4)):
            mr, mi = tabs[2 * idx], tabs[2 * idx + 1]
            shift = 8 - d if reverse else d
            sr = pltpu.roll(hr, shift, 0)
            si = pltpu.roll(hi, shift, 0)
            hr, hi = hr + mr * sr - mi * si, hi + mr * si + mi * sr
        cr, ci = carry[0], carry[1]
        hr, hi = hr + tabs[6] * cr - tabs[7] * ci, hi + tabs[6] * ci + tabs[7] * cr
        hr_ref[pl.ds(t8, 8), :] = hr
        hi_ref[pl.ds(t8, 8), :] = hi
        if not reverse:
            return hr[7:8, :], hi[7:8, :]
        out = (hr[0:1, :], hi[0:1, :])
        if prev_refs is None:
            return out
        fr_ref, fi_ref = prev_refs
        tp = pl.multiple_of(jnp.maximum(k - 1, 0) * 8, 8)
        keep = jnp.where(k > 0, 1.0, 0.0)
        lr_last = fr_ref[pl.ds(tp, 8), :][7:8, :] * keep
        li_last = fi_ref[pl.ds(tp, 8), :][7:8, :] * keep
        pr = jnp.where(row == 0, lr_last, pltpu.roll(fr_ref[pl.ds(t8, 8), :], 1, 0))
        pi = jnp.where(row == 0, li_last, pltpu.roll(fi_ref[pl.ds(t8, 8), :], 1, 0))
        return out + (carry[2] + hr * pr + hi * pi, carry[3] + hi * pr - hr * pi)

    zero = jnp.zeros((1, w), F32)
    init = (zero, zero)
    if reverse and prev_refs is not None:
        init += (jnp.zeros((8, w), F32), jnp.zeros((8, w), F32))
    return lax.fori_loop(0, nchunk, step, init)


def _seg_scan(hr_ref, hi_ref, tab_ref, l, reverse, states_refs=None):
    nq = hr_ref.shape[0]
    seg = l // N_SEG
    span = 8 * N_SEG
    nblk = seg // 8
    row = lax.broadcasted_iota(jnp.int32, (N_SEG, LANES), 0)

    def tab(r0, q):
        return tab_ref[r0:r0 + 8, q * LANES:(q + 1) * LANES]

    lam = [(tab(48, q), tab(56, q)) for q in range(nq)]

    def views(refs, q, jb):
        base = pl.multiple_of((nblk - 1 - jb if reverse else jb) * span, span)
        return [r.at[q, pl.ds(base, span), :] for r in refs]

    def local_rows():
        return range(7, -1, -1) if reverse else range(8)

    def at(r):
        return pl.ds(r * N_SEG, N_SEG)

    def pass1(jb, carry):
        hs = list(carry)
        for q in range(nq):
            vr, vi = views((hr_ref, hi_ref), q, jb)
            lr, li = lam[q]
            h_r, h_i = hs[2 * q], hs[2 * q + 1]
            for r in local_rows():
                h_r, h_i = lr * h_r - li * h_i + vr[at(r), :], lr * h_i + li * h_r + vi[at(r), :]
                vr[at(r), :] = h_r
                vi[at(r), :] = h_i
            hs[2 * q], hs[2 * q + 1] = h_r, h_i
        return tuple(hs)

    zero = jnp.zeros((N_SEG, LANES), F32)
    ends = lax.fori_loop(0, nblk, pass1, (zero,) * (2 * nq))

    carry_in = []
    for q in range(nq):
        er, ei = ends[2 * q], ends[2 * q + 1]
        for idx, d in enumerate((1, 2, 4)):
            mr, mi = tab(16 * idx, q), tab(16 * idx + 8, q)
            shift = N_SEG - d if reverse else d
            sr, si = pltpu.roll(er, shift, 0), pltpu.roll(ei, shift, 0)
            er, ei = er + mr * sr - mi * si, ei + mr * si + mi * sr
        if reverse:
            keep, shift = row < N_SEG - 1, N_SEG - 1
        else:
            keep, shift = row >= 1, 1
        carry_in += [jnp.where(keep, pltpu.roll(er, shift, 0), 0.0), jnp.where(keep, pltpu.roll(ei, shift, 0), 0.0)]

    with_acc = states_refs is not None

    def pass2(jb, carry):
        cs = list(carry)
        for q in range(nq):
            vr, vi = views((hr_ref, hi_ref), q, jb)
            lr, li = lam[q]
            d_r, d_i = cs[2 * q], cs[2 * q + 1]
            if with_acc:
                fr, fi = views(states_refs, q, jb)
                n_r, n_i, a_r, a_i = cs[2 * nq + 4 * q:2 * nq + 4 * q + 4]
            for r in local_rows():
                d_r, d_i = lr * d_r - li * d_i, lr * d_i + li * d_r
                g_r, g_i = vr[at(r), :] + d_r, vi[at(r), :] + d_i
                vr[at(r), :] = g_r
                vi[at(r), :] = g_i
                if with_acc:
                    p_r, p_i = fr[at(r), :], fi[at(r), :]
                    a_r, a_i = a_r + n_r * p_r + n_i * p_i, a_i + n_i * p_r - n_r * p_i
                    n_r, n_i = g_r, g_i
            cs[2 * q], cs[2 * q + 1] = d_r, d_i
            if with_acc:
                cs[2 * nq + 4 * q:2 * nq + 4 * q + 4] = [n_r, n_i, a_r, a_i]
        return tuple(cs)

    init = list(carry_in)
    if with_acc:
        for q in range(nq):
            init += [carry_in[2 * q], carry_in[2 * q + 1], zero, zero]
    out = lax.fori_loop(0, nblk, pass2, tuple(init))
    if with_acc:
        return [(out[2 * nq + 4 * q + 2], out[2 * nq + 4 * q + 3]) for q in range(nq)]
    return None


def _put_states(ref, rows, val):
    for q in range(ref.shape[0]):
        ref[q, rows, :] = val[:, q * LANES:(q + 1) * LANES]


def _get_states(ref, rows):
    return jnp.concatenate([ref[q, rows, :] for q in range(ref.shape[0])], axis=1)


def _s5_dims(sw):
    chan = GROUPS_PER_TILE * SSM_GROUP
    states = GROUPS_PER_TILE * SSM_STATE
    assert chan == LANES and sw % chan == 0
    return sw // chan, chan, states


def _interleave(x):
    l, w = x.shape
    return x.reshape(N_SEG, l // N_SEG, w).transpose(1, 0, 2).reshape(l, w)


def _deinterleave(x):
    l, w = x.shape
    return x.reshape(l // N_SEG, N_SEG, w).transpose(1, 0, 2).reshape(l, w)


def _s5_fwd(proj, u_off, packs, dvec, tab_f, sw, name):
    l = proj.shape[0]
    nt, chan, states = _s5_dims(sw)
    ch = _tile(l, 512, 8)
    ub = u_off // chan
    assert u_off % chan == 0

    def body(u_ref, br_ref, bi_ref, cr_ref, ci_ref, d_ref, tab_ref, y_ref, hr_ref, hi_ref):
        for i in range(l // ch):
            rows = pl.ds(i * ch, ch)
            u = u_ref[rows, :]
            _put_states(hr_ref, rows, jnp.dot(u, br_ref[0], preferred_element_type=F32))
            _put_states(hi_ref, rows, jnp.dot(u, bi_ref[0], preferred_element_type=F32))
        _seg_scan(hr_ref, hi_ref, tab_ref, l, reverse=False)
        for i in range(l // ch):
            rows = pl.ds(i * ch, ch)
            y = jnp.dot(_get_states(hr_ref, rows).astype(BF16), cr_ref[0], preferred_element_type=F32)
            y -= jnp.dot(_get_states(hi_ref, rows).astype(BF16), ci_ref[0], preferred_element_type=F32)
            y_ref[rows, :] = y + d_ref[...] * u_ref[rows, :].astype(F32)

    pin = pl.BlockSpec((1, chan, states), lambda t: (t, 0, 0))
    pout = pl.BlockSpec((1, states, chan), lambda t: (t, 0, 0))
    return pl.pallas_call(
        body,
        out_shape=jax.ShapeDtypeStruct((l, sw), F32),
        grid=(nt,),
        in_specs=[pl.BlockSpec((l, chan), lambda t: (0, ub + t)), pin, pin, pout, pout,
                  pl.BlockSpec((1, chan), lambda t: (0, t)), pl.BlockSpec((64, states), lambda t: (0, t))],
        out_specs=pl.BlockSpec((l, chan), lambda t: (0, t)),
        scratch_shapes=[pltpu.VMEM((states // LANES, l, LANES), F32)] * 2,
        name=name,
        compiler_params=_params(("parallel",)),
    )(proj, packs["br"], packs["bi"], packs["cr"], packs["ci"], dvec, tab_f)


def _s5_bwd(proj, u_off, dy, packs, dvec, tab_f, tab_b, sw, name):
    l = proj.shape[0]
    nt, chan, states = _s5_dims(sw)
    ch = _tile(l, 512, 8)
    ub = u_off // chan
    tn_dims = (((0,), (0,)), ((), ()))

    def body(u_ref, dy_ref, br_ref, bi_ref, brt_ref, bit_ref, crt_ref, cit_ref, d_ref, tabf_ref, tabb_ref,
             du_ref, dlam_ref, dbr_ref, dbi_ref, dcr_ref, dci_ref, dd_ref, hr_ref, hi_ref, gr_ref, gi_ref):
        for i in range(l // ch):
            rows = pl.ds(i * ch, ch)
            u = u_ref[rows, :]
            _put_states(hr_ref, rows, jnp.dot(u, br_ref[0], preferred_element_type=F32))
            _put_states(hi_ref, rows, jnp.dot(u, bi_ref[0], preferred_element_type=F32))
            dyv = dy_ref[rows, :]
            _put_states(gr_ref, rows, jnp.dot(dyv, crt_ref[0], preferred_element_type=F32))
            _put_states(gi_ref, rows, -jnp.dot(dyv, cit_ref[0], preferred_element_type=F32))
        _seg_scan(hr_ref, hi_ref, tabf_ref, l, reverse=False)
        accs = _seg_scan(gr_ref, gi_ref, tabb_ref, l, reverse=True, states_refs=(hr_ref, hi_ref))
        dlam_ref[...] = jnp.concatenate(
            [jnp.concatenate([jnp.sum(a[0], axis=0, keepdims=True) for a in accs], axis=1),
             jnp.concatenate([jnp.sum(a[1], axis=0, keepdims=True) for a in accs], axis=1), jnp.zeros((6, states), F32)], axis=0)
        dbr_ref[...] = jnp.zeros_like(dbr_ref)
        dbi_ref[...] = jnp.zeros_like(dbi_ref)
        dcr_ref[...] = jnp.zeros_like(dcr_ref)
        dci_ref[...] = jnp.zeros_like(dci_ref)
        dd = jnp.zeros((1, chan), F32)
        for i in range(l // ch):
            rows = pl.ds(i * ch, ch)
            u = u_ref[rows, :]
            dyv = dy_ref[rows, :]
            grb = _get_states(gr_ref, rows).astype(BF16)
            gib = _get_states(gi_ref, rows).astype(BF16)
            dbr_ref[0] += lax.dot_general(grb, u, tn_dims, preferred_element_type=F32)
            dbi_ref[0] += lax.dot_general(gib, u, tn_dims, preferred_element_type=F32)
            dcr_ref[0] += lax.dot_general(_get_states(hr_ref, rows).astype(BF16), dyv, tn_dims, preferred_element_type=F32)
            dci_ref[0] -= lax.dot_general(_get_states(hi_ref, rows).astype(BF16), dyv, tn_dims, preferred_element_type=F32)
            du = jnp.dot(grb, brt_ref[0], preferred_element_type=F32) + jnp.dot(gib, bit_ref[0], preferred_element_type=F32)
            dyf = dyv.astype(F32)
            du_ref[rows, :] = (du + d_ref[...] * dyf).astype(BF16)
            dd += jnp.sum(dyf * u.astype(F32), axis=0, keepdims=True)
        dd_ref[...] = dd

    pin = pl.BlockSpec((1, chan, states), lambda t: (t, 0, 0))
    pout = pl.BlockSpec((1, states, chan), lambda t: (t, 0, 0))
    seq = pl.BlockSpec((l, chan), lambda t: (0, t))
    tab = pl.BlockSpec((64, states), lambda t: (0, t))
    vec = pl.BlockSpec((1, chan), lambda t: (0, t))
    pack_shape = jax.ShapeDtypeStruct((nt, states, chan), F32)
    return pl.pallas_call(
        body,
        out_shape=(jax.ShapeDtypeStruct((l, sw), BF16), jax.ShapeDtypeStruct((8, nt * states), F32),
                   pack_shape, pack_shape, pack_shape, pack_shape, jax.ShapeDtypeStruct((1, sw), F32)),
        grid=(nt,),
        in_specs=[pl.BlockSpec((l, chan), lambda t: (0, ub + t)), seq, pin, pin, pout, pout, pin, pin, vec, tab, tab],
        out_specs=(seq, pl.BlockSpec((8, states), lambda t: (0, t)), pout, pout, pout, pout, vec),
        scratch_shapes=[pltpu.VMEM((states // LANES, l, LANES), F32)] * 4,
        name=name,
        compiler_params=_params(("parallel",)),
    )(proj, dy, packs["br"], packs["bi"], packs["brt"], packs["bit"], packs["crt"], packs["cit"], dvec, tab_f, tab_b)


GELU_K = math.sqrt(2.0 / math.pi)
GELU_C = 0.044715


def _gelu(y, name):
    l, w = y.shape
    tl = _tile(l, 512, 8)

    def body(y_ref, o_ref):
        v = y_ref[...]
        o_ref[...] = (0.5 * v * (1.0 + jnp.tanh(GELU_K * (v + GELU_C * v * v * v)))).astype(BF16)

    return pl.pallas_call(body, out_shape=jax.ShapeDtypeStruct((l, w), BF16), grid=(l // tl,),
                          in_specs=[_row_spec(tl, w)], out_specs=_row_spec(tl, w), name=name,
                          compiler_params=_params(("parallel",)))(y)


def _gelu_bwd(y, dg, name):
    l, w = y.shape
    tl = _tile(l, 512, 8)

    def body(y_ref, dg_ref, o_ref):
        v = y_ref[...]
        t = jnp.tanh(GELU_K * (v + GELU_C * v * v * v))
        grad = 0.5 * (1.0 + t) + 0.5 * v * (1.0 - t * t) * GELU_K * (1.0 + 3.0 * GELU_C * v * v)
        o_ref[...] = (dg_ref[...].astype(F32) * grad).astype(BF16)

    return pl.pallas_call(body, out_shape=jax.ShapeDtypeStruct((l, w), BF16), grid=(l // tl,),
                          in_specs=[_row_spec(tl, w), _row_spec(tl, w)], out_specs=_row_spec(tl, w), name=name,
                          compiler_params=_params(("parallel",)))(y, dg)


def _mix(ga, gs, attn_out, glu, name):
    l, d = ga.shape
    tl = _tile(l, 256, 16)

    def body(ga_ref, gs_ref, a_ref, u_ref, o_ref):
        ssm = u_ref[:, :d].astype(F32) * _sigmoid(u_ref[:, d:].astype(F32))
        o_ref[...] = (_sigmoid(ga_ref[...].astype(F32)) * a_ref[...].astype(F32)
                      + _sigmoid(gs_ref[...].astype(F32)) * ssm).astype(BF16)

    return pl.pallas_call(
        body, out_shape=jax.ShapeDtypeStruct((l, d), BF16), grid=(l // tl,),
        in_specs=[_row_spec(tl, d), _row_spec(tl, d), _row_spec(tl, d), _row_spec(tl, 2 * d)], out_specs=_row_spec(tl, d),
        name=name, compiler_params=_params(("parallel",)),
    )(ga, gs, attn_out, glu)


def _mix_bwd(ga, gs, attn_out, glu, dmixed, name):
    l, d = ga.shape
    tl = _tile(l, 256, 16)

    def body(ga_ref, gs_ref, a_ref, u_ref, dm_ref, dga_ref, dgs_ref, da_ref, dglu_ref):
        dm = dm_ref[...].astype(F32)
        sa = _sigmoid(ga_ref[...].astype(F32))
        ss = _sigmoid(gs_ref[...].astype(F32))
        sb = _sigmoid(u_ref[:, d:].astype(F32))
        ua = u_ref[:, :d].astype(F32)
        dssm = dm * ss
        dga_ref[...] = (dm * a_ref[...].astype(F32) * sa * (1.0 - sa)).astype(BF16)
        da_ref[...] = (dm * sa).astype(BF16)
        dgs_ref[...] = (dm * (ua * sb) * ss * (1.0 - ss)).astype(BF16)
        dglu_ref[:, :d] = (dssm * sb).astype(BF16)
        dglu_ref[:, d:] = (dssm * ua * sb * (1.0 - sb)).astype(BF16)

    out = jax.ShapeDtypeStruct((l, d), BF16)
    return pl.pallas_call(
        body, out_shape=(out, out, out, jax.ShapeDtypeStruct((l, 2 * d), BF16)), grid=(l // tl,),
        in_specs=[_row_spec(tl, d), _row_spec(tl, d), _row_spec(tl, d), _row_spec(tl, 2 * d), _row_spec(tl, d)],
        out_specs=(_row_spec(tl, d), _row_spec(tl, d), _row_spec(tl, d), _row_spec(tl, 2 * d)), name=name,
        compiler_params=_params(("parallel",)),
    )(ga, gs, attn_out, glu, dmixed)


CONV_BLOCKS = 4
HALO = 16


def _shift_rows(v, k, head):
    row = lax.broadcasted_iota(jnp.int32, v.shape, 0)
    out = pltpu.roll(v, k, 0)
    for r in range(k):
        out = jnp.where(row == r, head[HALO - k + r:HALO - k + r + 1, :], out)
    return out


def _shift_rows_up(v, k, tail):
    n = v.shape[0]
    row = lax.broadcasted_iota(jnp.int32, v.shape, 0)
    out = pltpu.roll(v, n - k, 0)
    for r in range(k):
        out = jnp.where(row == n - k + r, tail[r:r + 1, :], out)
    return out


def _conv_gate(g, head, w_ref, b_ref):
    return w_ref[0:1, :] * _shift_rows(g, 2, head) + w_ref[1:2, :] * _shift_rows(g, 1, head) + w_ref[2:3, :] * g + b_ref[...]


def _conv_act(up, conv_w, conv_b, ff, cw, name):
    l = up.shape[0]
    tl = _tile(l, 256, HALO)
    nj = ff // cw
    hb = tl // HALO

    def body(g_ref, gp_ref, v_ref, w_ref, b_ref, o_ref):
        i = pl.program_id(0)
        head = gp_ref[...].astype(F32) * jnp.where(i > 0, 1.0, 0.0)
        gc = _conv_gate(g_ref[...].astype(F32), head, w_ref, b_ref)
        o_ref[...] = (gc * _sigmoid(gc) * v_ref[...].astype(F32)).astype(BF16)

    return pl.pallas_call(
        body, out_shape=jax.ShapeDtypeStruct((l, ff), BF16), grid=(l // tl, nj),
        in_specs=[pl.BlockSpec((tl, cw), lambda i, j: (i, 2 * j)),
                  pl.BlockSpec((HALO, cw), lambda i, j: (jnp.maximum(i * hb - 1, 0), 2 * j)),
                  pl.BlockSpec((tl, cw), lambda i, j: (i, 2 * j + 1)),
                  pl.BlockSpec((3, cw), lambda i, j: (0, j)), pl.BlockSpec((1, cw), lambda i, j: (0, j))],
        out_specs=pl.BlockSpec((tl, cw), lambda i, j: (i, j)), name=name,
        compiler_params=_params(("parallel", "parallel")),
    )(up, up, up, conv_w, conv_b)


def _conv_act_bwd(up, da, conv_w, conv_b, ff, cw, name):
    l = up.shape[0]
    tl = _tile(l, 256, HALO)
    nj = ff // cw
    hb = tl // HALO
    ni = l // tl

    def body(g_ref, gp_ref, gn_ref, v_ref, vn_ref, da_ref, dan_ref, w_ref, b_ref, dup_ref, dw_ref, db_ref):
        i = pl.program_id(0)
        g = g_ref[...].astype(F32)
        head = gp_ref[...].astype(F32) * jnp.where(i > 0, 1.0, 0.0)
        g1 = _shift_rows(g, 1, head)
        g2 = _shift_rows(g, 2, head)
        gc = w_ref[0:1, :] * g2 + w_ref[1:2, :] * g1 + w_ref[2:3, :] * g + b_ref[...]
        sg = _sigmoid(gc)
        dav = da_ref[...].astype(F32)
        dgc = dav * v_ref[...].astype(F32) * (sg * (1.0 + gc * (1.0 - sg)))
        gn = gn_ref[...].astype(F32)
        gcn = _conv_gate(gn, g[tl - HALO:, :], w_ref, b_ref)
        sgn = _sigmoid(gcn)
        dgcn = dan_ref[...].astype(F32) * vn_ref[...].astype(F32) * (sgn * (1.0 + gcn * (1.0 - sgn)))
        dgcn = dgcn * jnp.where(i < ni - 1, 1.0, 0.0)
        dgate = w_ref[2:3, :] * dgc + w_ref[1:2, :] * _shift_rows_up(dgc, 1, dgcn) + w_ref[0:1, :] * _shift_rows_up(dgc, 2, dgcn)
        dup_ref[:, :cw] = dgate.astype(BF16)
        dup_ref[:, cw:] = (dav * (gc * sg)).astype(BF16)
        zero = jnp.zeros((1, cw), F32)
        dw_ref[...] = jnp.concatenate(
            [jnp.sum(dgc * g2, axis=0, keepdims=True), jnp.sum(dgc * g1, axis=0, keepdims=True),
             jnp.sum(dgc * g, axis=0, keepdims=True)] + [zero] * 5, axis=0)
        db_ref[...] = jnp.concatenate([jnp.sum(dgc, axis=0, keepdims=True)] + [zero] * 7, axis=0)

    def cur(col):
        return pl.BlockSpec((tl, cw), lambda i, j, col=col: (i, 2 * j + col))

    def prev(col):
        return pl.BlockSpec((HALO, cw), lambda i, j, col=col: (jnp.maximum(i * hb - 1, 0), 2 * j + col))

    def nxt(col):
        return pl.BlockSpec((HALO, cw), lambda i, j, col=col: (jnp.minimum((i + 1) * hb, l // HALO - 1), 2 * j + col))

    part = jax.ShapeDtypeStruct((ni * 8, ff), F32)
    part_spec = pl.BlockSpec((8, cw), lambda i, j: (i, j))
    return pl.pallas_call(
        body, out_shape=(jax.ShapeDtypeStruct((l, 2 * ff), BF16), part, part), grid=(ni, nj),
        in_specs=[cur(0), prev(0), nxt(0), cur(1), nxt(1), pl.BlockSpec((tl, cw), lambda i, j: (i, j)),
                  pl.BlockSpec((HALO, cw), lambda i, j: (jnp.minimum((i + 1) * hb, l // HALO - 1), j)),
                  pl.BlockSpec((3, cw), lambda i, j: (0, j)), pl.BlockSpec((1, cw), lambda i, j: (0, j))],
        out_specs=(pl.BlockSpec((tl, 2 * cw), lambda i, j: (i, j)), part_spec, part_spec), name=name,
        compiler_params=_params(("parallel", "parallel")),
    )(up, up, up, up, up, da, da, conv_w, conv_b)


def _sum_rows8(parts, name):
    n8, w = parts.shape
    n = n8 // 8
    cw = _tile(w, 2048)

    def body(p_ref, o_ref):
        acc = p_ref[0:8, :]
        for k in range(1, n):
            acc = acc + p_ref[8 * k:8 * k + 8, :]
        o_ref[...] = acc

    return pl.pallas_call(body, out_shape=jax.ShapeDtypeStruct((8, w), F32), grid=(w // cw,),
                          in_specs=[pl.BlockSpec((n8, cw), lambda j: (0, j))], out_specs=pl.BlockSpec((8, cw), lambda j: (0, j)),
                          name=name, compiler_params=_params(("parallel",)))(parts)


def _ada_fwd(c_all, w_shard, b_shard, name):
    nb, d = c_all.shape
    n = w_shard.shape[1]
    tn = _tile(n, 512)

    def body(c_ref, w_ref, b_ref, o_ref):
        cv = c_ref[...]
        cond = (cv * _sigmoid(cv)).astype(BF16)
        o_ref[...] = jnp.dot(cond, w_ref[...].astype(BF16), preferred_element_type=F32) + b_ref[...]

    return pl.pallas_call(
        body, out_shape=jax.ShapeDtypeStruct((nb, n), F32), grid=(n // tn,),
        in_specs=[pl.BlockSpec((nb, d), lambda j: (0, 0)), pl.BlockSpec((d, tn), lambda j: (0, j)),
                  pl.BlockSpec((1, tn), lambda j: (0, j))],
        out_specs=pl.BlockSpec((nb, tn), lambda j: (0, j)), name=name, compiler_params=_params(("parallel",)),
    )(c_all, w_shard, b_shard)


def _adam_update(w, g, m, v):
    m2 = ADAM_B1 * m + (1.0 - ADAM_B1) * g
    v2 = ADAM_B2 * v + (1.0 - ADAM_B2) * (g * g)
    m_hat = m2 / (1.0 - ADAM_B1 ** ADAM_STEP)
    v_hat = v2 / (1.0 - ADAM_B2 ** ADAM_STEP)
    return -ADAM_LR * (m_hat / (jnp.sqrt(v_hat) + ADAM_EPS) + ADAM_WD * w), m2, v2


def _ada_bwd_adam(c_all_t, dmod_shard, w, m, v, name):
    d, nb = c_all_t.shape
    n = w.shape[1]
    tr, tn = _tile(d, 512, 8), _tile(n, 512)

    def body(c_ref, dm_ref, w_ref, m_ref, v_ref, g_ref, dl_ref, m2_ref, v2_ref):
        cv = c_ref[...]
        cond = cv * _sigmoid(cv)
        g = cond[:, 0:1] * dm_ref[0:1, :]
        for b in range(1, nb):
            g = g + cond[:, b:b + 1] * dm_ref[b:b + 1, :]
        g_ref[...] = g
        dl_ref[...], m2_ref[...], v2_ref[...] = _adam_update(w_ref[...], g, m_ref[...], v_ref[...])

    blk = pl.BlockSpec((tr, tn), lambda i, j: (i, j))
    out = jax.ShapeDtypeStruct((d, n), F32)
    return pl.pallas_call(
        body, out_shape=(out, out, out, out), grid=(d // tr, n // tn),
        in_specs=[pl.BlockSpec((tr, nb), lambda i, j: (i, 0)), pl.BlockSpec((nb, tn), lambda i, j: (0, j)), blk, blk, blk],
        out_specs=(blk, blk, blk, blk), name=name, compiler_params=_params(("parallel", "parallel")),
    )(c_all_t, dmod_shard, w, m, v)


def _adam(w, g, m, v, name):
    r, c = w.shape
    tr = _tile(r, 256, 8)

    def body(w_ref, g_ref, m_ref, v_ref, dl_ref, m2_ref, v2_ref):
        dl_ref[...], m2_ref[...], v2_ref[...] = _adam_update(w_ref[...], g_ref[...], m_ref[...], v_ref[...])

    blk = pl.BlockSpec((tr, c), lambda i: (i, 0))
    out = jax.ShapeDtypeStruct((r, c), F32)
    return pl.pallas_call(body, out_shape=(out, out, out), grid=(r // tr,), in_specs=[blk] * 4, out_specs=(blk,) * 3,
                          name=name, compiler_params=_params(("parallel",)))(w, g, m, v)


def _sum_devices(gathered, name):
    nd, r, c = gathered.shape
    tr = _tile(r, 64, 16)

    def body(g_ref, o_ref):
        acc = g_ref[0].astype(F32)
        for k in range(1, nd):
            acc = acc + g_ref[k].astype(F32)
        o_ref[...] = acc

    return pl.pallas_call(body, out_shape=jax.ShapeDtypeStruct((r, c), F32), grid=(r // tr,),
                          in_specs=[pl.BlockSpec((nd, tr, c), lambda i: (0, i, 0))], out_specs=pl.BlockSpec((tr, c), lambda i: (i, 0)),
                          name=name, compiler_params=_params(("parallel",)))(gathered)


def _place():
    x, y, c = lax.axis_index("x"), lax.axis_index("y"), lax.axis_index("c")
    chips = [(1 - x, y), (x, 1 - y), (1 - x, 1 - y)]
    return x, y, c, chips


def _all_gather8(block, name):
    m_per, n = block.shape

    def body(x_ref, out_ref, send_sems, recv_sems, local_sem):
        x, y, c, chips = _place()
        me, sibling = (x, y, c), (x, y, 1 - c)

        def rows(px, py, pc):
            return out_ref.at[pl.ds((4 * px + 2 * py + pc) * m_per, m_per), :]

        def copy(k, blk, to, src=None):
            return pltpu.make_async_remote_copy(
                src_ref=rows(*blk) if src is None else src, dst_ref=rows(*blk), send_sem=send_sems.at[k],
                recv_sem=recv_sems.at[k], device_id=to, device_id_type=MESH)

        mine = pltpu.make_async_copy(x_ref, rows(*me), local_sem)
        mine.start()
        first = [copy(0, me, sibling, src=x_ref)]
        first += [copy(1 + j, me, (*chip, c), src=x_ref) for j, chip in enumerate(chips)]
        for cp in first:
            cp.start()
        passed = [copy(4 + j, (*chip, c), sibling) for j, chip in enumerate(chips)]
        for j, chip in enumerate(chips):
            copy(1 + j, (*chip, c), me).wait_recv()
            passed[j].start()
        copy(0, sibling, me).wait_recv()
        for j, chip in enumerate(chips):
            copy(4 + j, (*chip, 1 - c), me).wait_recv()
        for cp in first + passed:
            cp.wait_send()
        mine.wait()

    return pl.pallas_call(
        body,
        out_shape=jax.ShapeDtypeStruct((N_DEV * m_per, n), block.dtype),
        in_specs=[pl.BlockSpec(memory_space=pltpu.VMEM)],
        out_specs=pl.BlockSpec(memory_space=pltpu.VMEM),
        scratch_shapes=[pltpu.SemaphoreType.DMA((7,)), pltpu.SemaphoreType.DMA((7,)), pltpu.SemaphoreType.DMA],
        name=name,
        compiler_params=pltpu.CompilerParams(vmem_limit_bytes=VMEM_LIMIT_BYTES),
    )(block)


ANY = pl.BlockSpec(memory_space=pl.ANY)


def _place_shard(shard, name, after=()):
    r, k = shard.shape
    tb = _tile(r, 512, 16)
    nb = r // tb
    chip = (2 * lax.axis_index("x") + lax.axis_index("y")).astype(jnp.int32).reshape(1)

    def body(j_ref, s_ref, *rest):
        rest[-1][...] = s_ref[...].astype(BF16)

    return pl.pallas_call(
        body, out_shape=jax.ShapeDtypeStruct((N_CHIPS * r, k), BF16),
        grid_spec=pltpu.PrefetchScalarGridSpec(
            num_scalar_prefetch=1, grid=(nb,),
            in_specs=[pl.BlockSpec((tb, k), lambda i, j_ref: (i, 0))] + [ANY] * len(after),
            out_specs=pl.BlockSpec((tb, k), lambda i, j_ref: (j_ref[0] * nb + i, 0))),
        name=name, compiler_params=_params(("parallel",)),
    )(chip, shard, *after)


HBM_SPEC = pl.BlockSpec(memory_space=pltpu.HBM)
SEM_SPEC = pl.BlockSpec(memory_space=pltpu.SEMAPHORE)
TOKEN_SPEC = pl.BlockSpec(memory_space=pltpu.VMEM)
SPLIT_COPY = pltpu.CompilerParams(has_side_effects=pltpu.SideEffectType.DATAFLOW_SIDE_EFFECTING)


def _in_hbm(arrays):
    return [pltpu.with_memory_space_constraint(a, pltpu.HBM) for a in arrays]


def _hbm_like(arrays):
    return tuple(pltpu.HBM(a.shape, a.dtype) for a in arrays)


def _token_shape():
    return jax.ShapeDtypeStruct((8, LANES), F32)


def _gathered_rows(buf, px, py, half):
    r = buf.shape[0] // N_CHIPS
    return buf.at[pl.ds(pl.multiple_of((2 * px + py) * r + half * (r // 2), 16), r // 2), :]


def _gather_start(groups, name):
    sizes = [len(g) for g in groups]
    flat = [b for g in groups for b in g]
    nb, ng = len(flat), len(groups)

    def body(*refs):
        bufs = refs[:nb]
        sems = refs[nb:nb + 2 * ng]
        token = refs[-1]
        x, y, c, chips = _place()
        pos = 0
        for gi, nw in enumerate(sizes):
            for k, chip in enumerate(chips):
                for w in range(nw):
                    mine = _gathered_rows(bufs[pos + w], x, y, c)
                    pltpu.make_async_remote_copy(src_ref=mine, dst_ref=mine, send_sem=sems[2 * gi].at[k * nw + w], recv_sem=sems[2 * gi + 1].at[k * nw + w],
                                                 device_id=(*chip, c), device_id_type=MESH).start()
            pos += nw
        token[...] = jnp.zeros_like(token)

    sem_shapes = tuple(pltpu.SemaphoreType.DMA((3 * n,)) for n in sizes for _ in range(2))
    outs = pl.pallas_call(
        body, name=name, out_shape=sem_shapes + _hbm_like(flat) + (_token_shape(),),
        in_specs=[HBM_SPEC] * nb, out_specs=(SEM_SPEC,) * (2 * ng) + (HBM_SPEC,) * nb + (TOKEN_SPEC,),
        input_output_aliases={i: 2 * ng + i for i in range(nb)}, compiler_params=SPLIT_COPY,
    )(*_in_hbm(flat))
    res, pos = [], 2 * ng
    for gi, n in enumerate(sizes):
        res.append((outs[2 * gi], outs[2 * gi + 1], list(outs[pos:pos + n])))
        pos += n
    return res, outs[-1]


def _gather_forward(bufs, ici_send, ici_recv, after, name):
    nw, na = len(bufs), len(after)

    def body(*refs):
        b = refs[:nw]
        isend, irecv = refs[nw], refs[nw + 1]
        dsend, drecv = refs[nw + 2 + na], refs[nw + 3 + na]
        x, y, c, chips = _place()
        for k, chip in enumerate(chips):
            for w in range(nw):
                landed = _gathered_rows(b[w], *chip, c)
                pltpu.make_async_remote_copy(src_ref=landed, dst_ref=landed, send_sem=isend.at[k * nw + w], recv_sem=irecv.at[k * nw + w],
                                             device_id=(*chip, c), device_id_type=MESH).wait_recv()
                pltpu.make_async_remote_copy(src_ref=landed, dst_ref=landed, send_sem=dsend.at[k * nw + w], recv_sem=drecv.at[k * nw + w],
                                             device_id=(x, y, 1 - c), device_id_type=MESH).start()
        for k, chip in enumerate(chips):
            for w in range(nw):
                mine = _gathered_rows(b[w], x, y, c)
                pltpu.make_async_remote_copy(src_ref=mine, dst_ref=mine, send_sem=isend.at[k * nw + w], recv_sem=irecv.at[k * nw + w],
                                             device_id=(*chip, c), device_id_type=MESH).wait_send()
        refs[-1][...] = jnp.zeros_like(refs[-1])

    sem = pltpu.SemaphoreType.DMA((3 * nw,))
    outs = pl.pallas_call(
        body, name=name, out_shape=(sem, sem) + _hbm_like(bufs) + (_token_shape(),),
        in_specs=[HBM_SPEC] * nw + [SEM_SPEC, SEM_SPEC] + [ANY] * na, out_specs=(SEM_SPEC, SEM_SPEC) + (HBM_SPEC,) * nw + (TOKEN_SPEC,),
        input_output_aliases={i: 2 + i for i in range(nw)}, compiler_params=SPLIT_COPY,
    )(*bufs, ici_send, ici_recv, *after)
    return outs[0], outs[1], list(outs[2:2 + nw]), outs[-1]


def _gather_finish(bufs, d2d_send, d2d_recv, name, after=()):
    nw = len(bufs)

    def body(*refs):
        b = refs[:nw]
        dsend, drecv = refs[nw], refs[nw + 1]
        x, y, c, chips = _place()
        for k, chip in enumerate(chips):
            for w in range(nw):
                theirs = _gathered_rows(b[w], *chip, 1 - c)
                pltpu.make_async_remote_copy(src_ref=theirs, dst_ref=theirs, send_sem=dsend.at[k * nw + w], recv_sem=drecv.at[k * nw + w],
                                             device_id=(x, y, 1 - c), device_id_type=MESH).wait_recv()
                passed = _gathered_rows(b[w], *chip, c)
                pltpu.make_async_remote_copy(src_ref=passed, dst_ref=passed, send_sem=dsend.at[k * nw + w], recv_sem=drecv.at[k * nw + w],
                                             device_id=(x, y, 1 - c), device_id_type=MESH).wait_send()

    outs = pl.pallas_call(
        body, name=name, out_shape=_hbm_like(bufs), in_specs=[HBM_SPEC] * nw + [SEM_SPEC, SEM_SPEC] + [ANY] * len(after),
        out_specs=(HBM_SPEC,) * nw, input_output_aliases={i: i for i in range(nw)}, compiler_params=SPLIT_COPY,
    )(*bufs, d2d_send, d2d_recv, *after)
    return list(outs)


def _swap_copies(src, land, send_sems, recv_sems):
    x, y, c, _ = _place()
    copies = []
    for w in range(len(src)):
        r = src[w].shape[0] // N_CHIPS
        h = r // 2
        for j in range(N_CHIPS):
            copies.append(pltpu.make_async_remote_copy(
                src_ref=src[w].at[pl.ds(pl.multiple_of(j * r + (1 - c) * h, 16), h), :], dst_ref=land[w].at[pl.ds(j * h, h), :],
                send_sem=send_sems.at[w * N_CHIPS + j], recv_sem=recv_sems.at[w * N_CHIPS + j], device_id=(x, y, 1 - c), device_id_type=MESH))
    return copies


def _swap_start(grads, name):
    nw = len(grads)
    landing = [lax.empty((g.shape[0] // 2, g.shape[1]), g.dtype) for g in grads]

    def body(*refs):
        for cp in _swap_copies(refs[:nw], refs[nw:2 * nw], refs[2 * nw], refs[2 * nw + 1]):
            cp.start()
        refs[-1][...] = jnp.zeros_like(refs[-1])

    sem = pltpu.SemaphoreType.DMA((N_CHIPS * nw,))
    outs = pl.pallas_call(
        body, name=name, out_shape=(sem, sem) + _hbm_like(grads) + _hbm_like(landing) + (_token_shape(),),
        in_specs=[HBM_SPEC] * (2 * nw), out_specs=(SEM_SPEC, SEM_SPEC) + (HBM_SPEC,) * (2 * nw) + (TOKEN_SPEC,),
        input_output_aliases={i: 2 + i for i in range(2 * nw)}, compiler_params=SPLIT_COPY,
    )(*_in_hbm(grads), *_in_hbm(landing))
    return (outs[0], outs[1], list(outs[2:2 + nw]), list(outs[2 + nw:2 + 2 * nw])), outs[-1]


def _swap_wait(started, after, name):
    send_sems, recv_sems, grads, landing = started
    nw = len(grads)

    def body(*refs):
        for cp in _swap_copies(refs[:nw], refs[nw:2 * nw], refs[2 * nw], refs[2 * nw + 1]):
            cp.wait_send()
            cp.wait_recv()

    outs = pl.pallas_call(
        body, name=name, out_shape=_hbm_like(grads) + _hbm_like(landing),
        in_specs=[HBM_SPEC] * (2 * nw) + [SEM_SPEC, SEM_SPEC] + [ANY] * len(after), out_specs=(HBM_SPEC,) * (2 * nw),
        input_output_aliases={i: i for i in range(2 * nw)}, compiler_params=SPLIT_COPY,
    )(*grads, *landing, send_sems, recv_sems, *after)
    return list(outs[:nw]), list(outs[nw:])


def _scatter_start(partials, name):
    nw = len(partials)
    landing = [lax.empty((3,) + p.shape[1:], p.dtype) for p in partials]

    def body(*refs):
        src, land = refs[:nw], refs[nw:2 * nw]
        send_sems, recv_sems = refs[2 * nw], refs[2 * nw + 1]
        token = refs[-1]
        x, y, c, chips = _place()
        for k, chip in enumerate(chips):
            for w in range(nw):
                pltpu.make_async_remote_copy(src_ref=src[w].at[2 * chip[0] + chip[1]], dst_ref=land[w].at[k], send_sem=send_sems.at[k * nw + w],
                                             recv_sem=recv_sems.at[k * nw + w], device_id=(*chip, c), device_id_type=MESH).start()
        token[...] = jnp.zeros_like(token)

    sem = pltpu.SemaphoreType.DMA((3 * nw,))
    outs = pl.pallas_call(
        body, name=name, out_shape=(sem, sem) + _hbm_like(partials) + _hbm_like(landing) + (_token_shape(),),
        in_specs=[HBM_SPEC] * (2 * nw), out_specs=(SEM_SPEC, SEM_SPEC) + (HBM_SPEC,) * (2 * nw) + (TOKEN_SPEC,),
        input_output_aliases={i: 2 + i for i in range(2 * nw)}, compiler_params=SPLIT_COPY,
    )(*_in_hbm(partials), *_in_hbm(landing))
    return (outs[0], outs[1], list(outs[2:2 + nw]), list(outs[2 + nw:2 + 2 * nw])), outs[-1]


def _scatter_wait(started, after, name):
    send_sems, recv_sems, partials, landing = started
    nw = len(partials)

    def body(*refs):
        src, land = refs[:nw], refs[nw:2 * nw]
        ssem, rsem = refs[2 * nw], refs[2 * nw + 1]
        x, y, c, chips = _place()
        for k, chip in enumerate(chips):
            for w in range(nw):
                cp = pltpu.make_async_remote_copy(src_ref=src[w].at[2 * chip[0] + chip[1]], dst_ref=land[w].at[k], send_sem=ssem.at[k * nw + w],
                                                  recv_sem=rsem.at[k * nw + w], device_id=(*chip, c), device_id_type=MESH)
                cp.wait_send()
                cp.wait_recv()

    outs = pl.pallas_call(
        body, name=name, out_shape=_hbm_like(partials) + _hbm_like(landing),
        in_specs=[HBM_SPEC] * (2 * nw) + [SEM_SPEC, SEM_SPEC] + [ANY] * len(after), out_specs=(HBM_SPEC,) * (2 * nw),
        input_output_aliases={i: i for i in range(2 * nw)}, compiler_params=SPLIT_COPY,
    )(*partials, *landing, send_sems, recv_sems, *after)
    return list(outs[:nw]), list(outs[nw:])


def _add_halves(grad, other, name):
    k = grad.shape[1]
    h = other.shape[0] // N_CHIPS
    tb = _tile(h, 512, 16)
    g4 = grad.reshape(N_CHIPS, 2, h, k)
    o3 = other.reshape(N_CHIPS, h, k)
    core = lax.axis_index("c").astype(jnp.int32).reshape(1)

    def body(c_ref, g_ref, o_ref, p_ref):
        p_ref[...] = (g_ref[...].astype(F32) + o_ref[...].astype(F32)).astype(BF16)

    return pl.pallas_call(
        body, out_shape=jax.ShapeDtypeStruct((N_CHIPS, h, k), BF16),
        grid_spec=pltpu.PrefetchScalarGridSpec(
            num_scalar_prefetch=1, grid=(N_CHIPS, h // tb),
            in_specs=[pl.BlockSpec((None, None, tb, k), lambda j, i, c_ref: (j, c_ref[0], i, 0)),
                      pl.BlockSpec((None, tb, k), lambda j, i, c_ref: (j, i, 0))],
            out_specs=pl.BlockSpec((None, tb, k), lambda j, i, c_ref: (j, i, 0))),
        name=name, compiler_params=_params(("parallel", "parallel")),
    )(core, g4, o3)


def _add_partials(partial, others, name):
    _, h, k = partial.shape
    tb = _tile(h, 512, 16)
    nb = h // tb
    place = jnp.stack([2 * lax.axis_index("x") + lax.axis_index("y"), lax.axis_index("c")]).astype(jnp.int32)

    def body(s_ref, p_ref, o0_ref, o1_ref, o2_ref, f_ref):
        f_ref[...] = ((p_ref[...].astype(F32) + o0_ref[...].astype(F32)) + o1_ref[...].astype(F32)) + o2_ref[...].astype(F32)

    def other(s):
        return pl.BlockSpec((None, tb, k), lambda i, s_ref, s=s: (s, i, 0))

    return pl.pallas_call(
        body, out_shape=jax.ShapeDtypeStruct((2 * h, k), F32),
        grid_spec=pltpu.PrefetchScalarGridSpec(
            num_scalar_prefetch=1, grid=(nb,),
            in_specs=[pl.BlockSpec((None, tb, k), lambda i, s_ref: (s_ref[0], i, 0)), other(0), other(1), other(2)],
            out_specs=pl.BlockSpec((tb, k), lambda i, s_ref: (s_ref[1] * nb + i, 0))),
        name=name, compiler_params=_params(("parallel",)),
    )(place, partial, others, others, others)


def _share_halves(fulls, name):
    nw = len(fulls)

    def body(*refs):
        ins, outs = refs[:nw], refs[nw:2 * nw]
        send_sems, recv_sems = refs[2 * nw:]
        x, y, c, _ = _place()
        copies = []
        for w in range(nw):
            h = fulls[w].shape[0] // 2
            start = pl.multiple_of(c * h, 8)
            copies.append(pltpu.make_async_remote_copy(
                src_ref=ins[w].at[pl.ds(start, h), :], dst_ref=outs[w].at[pl.ds(start, h), :], send_sem=send_sems.at[w],
                recv_sem=recv_sems.at[w], device_id=(x, y, 1 - c), device_id_type=MESH))
            copies[-1].start()
        for cp in copies:
            cp.wait()

    sem = pltpu.SemaphoreType.DMA((nw,))
    return pl.pallas_call(
        body, out_shape=tuple(jax.ShapeDtypeStruct(f.shape, f.dtype) for f in fulls),
        in_specs=[ANY] * nw, out_specs=(ANY,) * nw, scratch_shapes=[sem, sem], name=name,
        input_output_aliases={w: w for w in range(nw)},
    )(*fulls)


def _forward_then_finish(started_group, after, tag):
    ici_send, ici_recv, bufs = started_group
    d2d_send, d2d_recv, bufs, _ = _gather_forward(bufs, ici_send, ici_recv, after, f"gather_forward_{tag}")
    return _gather_finish(bufs, d2d_send, d2d_recv, f"gather_finish_{tag}")


def _reduce_start(swapping, tag, after=()):
    grads, from_sibling = _swap_wait(swapping, after, f"swap_wait_{tag}")
    chip_sums = [_add_halves(g, o, f"add_halves_{tag}_{i}") for i, (g, o) in enumerate(zip(grads, from_sibling))]
    return _scatter_start(chip_sums, f"scatter_start_{tag}")


def _reduce_finish(started, after, tag):
    chip_sums, from_chips = _scatter_wait(started, after, f"scatter_wait_{tag}")
    fulls = [_add_partials(p, o, f"add_partials_{tag}_{i}") for i, (p, o) in enumerate(zip(chip_sums, from_chips))]
    return _share_halves(fulls, f"share_halves_{tag}")


def _flatten_pad(parts, cols=SMALL_COLS):
    flat = jnp.concatenate([p.reshape(-1) for p in parts])
    rows = -(-flat.shape[0] // (16 * cols)) * 16
    return jnp.pad(flat, (0, rows * cols - flat.shape[0])).reshape(rows, cols)


def _split_flat(buf, shapes):
    flat = buf.reshape(-1)
    out, off = [], 0
    for s in shapes:
        n = math.prod(s)
        out.append(flat[off:off + n].reshape(s))
        off += n
    return out


def _ssm_setup(seq_len, ssm_a_re, ssm_a_im, ssm_log_dt, ssm_b_re, ssm_b_im, ssm_c_re, ssm_c_im):
    lam_r, lam_i, bbar_r, bbar_i = _ssm_discretize(ssm_a_re, ssm_a_im, ssm_log_dt, ssm_b_re, ssm_b_im)
    tab_f, tab_b = _scan_tables(lam_r, lam_i, seq_len // N_SEG)
    pk = {"br": _pack_in(bbar_r), "bi": _pack_in(bbar_i), "cr": _pack_out(ssm_c_re), "ci": _pack_out(ssm_c_im)}
    packs = {k: v.astype(BF16) for k, v in pk.items()}
    packs.update({"brt": jnp.swapaxes(packs["br"], 1, 2), "bit": jnp.swapaxes(packs["bi"], 1, 2),
                  "crt": jnp.swapaxes(packs["cr"], 1, 2), "cit": jnp.swapaxes(packs["ci"], 1, 2)})
    return packs, tab_f, tab_b


def _local_step(xs, target, mod, w_in_t, comm, norm_mix_g, attn_sinks, ssm, ssm_d, norm_ffn_g, conv_w_full, ffn_conv_b, final_g,
                aw, sw, ff):
    l, d = xs.shape
    u_off = aw + 2 * KV_WIDTH
    ga_off = u_off + sw
    gs_off = ga_off + d
    packs, tab_f, tab_b = ssm
    dvec = ssm_d.reshape(1, sw)

    h1 = _norm_mod(xs, norm_mix_g, mod, 1, 0, "norm_mod1")
    proj = _matmul(h1, w_in_t, "nt", "mm_in")
    attn = _attn_fwd(proj, attn_sinks, aw, "attn_fwd", after=(comm["mixer_arrived"]((proj,)),))
    u_il = _interleave(proj[:, u_off:u_off + sw])
    ys_il = _s5_fwd(u_il, 0, packs, dvec, tab_f, sw, "s5_fwd")
    gy = _deinterleave(_gelu(ys_il, "gelu"))
    (w_ap_t, w_glu_t, w_out_f), ffn_weights = comm["later_weights"]((gy, attn))
    attn_out = _matmul(attn, w_ap_t, "nt", "mm_attn_proj")
    glu = _matmul(gy, w_glu_t, "nt", "mm_glu")
    g_attn, g_ssm = proj[:, ga_off:ga_off + d], proj[:, gs_off:gs_off + d]
    mixed = _mix(g_attn, g_ssm, attn_out, glu, "mix")
    mo = _matmul(mixed, w_out_f, "nn", "mm_out", out_dtype=F32)
    x2, h2 = _resid_norm_mod(xs, mo, norm_ffn_g, mod, 2, 4, 3, "resid_norm_mod2")
    w_up_t, w_down_f = ffn_weights((h2,))
    cw = ff // CONV_BLOCKS
    up = _matmul(h2, w_up_t, "nt", "mm_up", interleave=cw)
    act = _conv_act(up, conv_w_full, ffn_conv_b, ff, cw, "conv_act")
    fo = _matmul(act, w_down_f, "nn", "mm_down", out_dtype=F32)
    loss_part, d_final_g, d_gate2, dx3, dfo = _final_loss(x2, fo, mod, 5, final_g.reshape(1, d), target, "final_loss")

    dact = _matmul(dfo, w_down_f, "nt", "mm_down_dx")
    g_down = _matmul(act, dfo, "tn", "mm_down_dw")
    dup, dcw_parts, dcb_parts = _conv_act_bwd(up, dact, conv_w_full, ffn_conv_b, ff, cw, "conv_act_bwd")
    d_conv_w = _sum_rows8(dcw_parts, "sum_conv_w")[:3]
    d_conv_b = _sum_rows8(dcb_parts, "sum_conv_b")[:1]
    g_up = _matmul(dup, h2, "tn", "mm_up_dw", interleave=cw)
    dh2 = _matmul(dup, w_up_t, "nn", "mm_up_dx", interleave=cw, after=(comm["grads_started"]("ffn", [g_up, g_down]),))
    mod = comm["ffn_grads_ready"](mod, (dh2,))
    dx2, d_shift2, d_scale2, d_gain2, dmo, d_gate1 = _norm_mod_bwd(dh2, x2, dx3, norm_ffn_g, mod, 4, "norm_mod2_bwd", branch=mo, gate_col=2)
    dmixed = _matmul(dmo, w_out_f, "nt", "mm_out_dx")
    g_out = _matmul(mixed, dmo, "tn", "mm_out_dw")
    dga, dgs, dattn_out, dglu = _mix_bwd(g_attn, g_ssm, attn_out, glu, dmixed, "mix_bwd")
    dgy = _matmul(dglu, w_glu_t, "nn", "mm_glu_dx")
    g_glu = _matmul(dglu, gy, "tn", "mm_glu_dw")
    dys_il = _gelu_bwd(ys_il, _interleave(dgy), "gelu_bwd")
    du_il, dlam, dbr_p, dbi_p, dcr_p, dci_p, d_dvec = _s5_bwd(u_il, 0, dys_il, packs, dvec, tab_f, tab_b, sw, "s5_bwd")
    du = _deinterleave(du_il)
    dattn = _matmul(dattn_out, w_ap_t, "nn", "mm_attn_proj_dx")
    g_ap = _matmul(dattn_out, attn, "tn", "mm_attn_proj_dw")
    dq, dkv_cur, dkv_prev, d_sinks = _attn_bwd(proj, attn_sinks, dattn, aw, "attn_bwd")
    dkv = dkv_cur + jnp.concatenate([dkv_prev[ATTN_BLOCK:], jnp.zeros((ATTN_BLOCK, 2 * KV_WIDTH), F32)], axis=0)
    dproj = jnp.concatenate([dq, dkv.astype(BF16), du, dga, dgs], axis=1)
    g_in = _matmul(dproj, h1, "tn", "mm_in_dw")
    dh1 = _matmul(dproj, w_in_t, "nn", "mm_in_dx", after=(comm["grads_started"]("rest", [g_in, g_ap, g_glu, g_out]),))
    grad_x, d_shift1, d_scale1, d_gain1 = _norm_mod_bwd(dh1, xs, dx2, norm_mix_g, mod, 1, "norm_mod1_bwd")

    dmod = jnp.concatenate([d_shift1, d_scale1, d_gate1, d_shift2, d_scale2, d_gate2], axis=1)
    small_parts = [dmod, d_gain1, d_sinks, dlam[0], dlam[1], _unpack_diag(dbr_p, SSM_STATE, SSM_GROUP),
                   _unpack_diag(dbi_p, SSM_STATE, SSM_GROUP), _unpack_diag(dcr_p, SSM_STATE, SSM_GROUP),
                   _unpack_diag(dci_p, SSM_STATE, SSM_GROUP), d_dvec, d_gain2, d_conv_b, d_conv_w, d_final_g]
    return loss_part, grad_x, small_parts


def _kernel_impl(x, c, ada_w, ada_b, norm_mix_g, w_in, attn_sinks, w_attn_proj, ssm_a_re, ssm_a_im, ssm_log_dt, ssm_b_re, ssm_b_im,
                 ssm_c_re, ssm_c_im, ssm_d, w_ssm_glu, w_out, norm_ffn_g, w_ffn_up, ffn_conv_w, ffn_conv_b, w_ffn_down, final_g,
                 loss_target, ms, vs):
    ax, ay, ac = lax.axis_index("x"), lax.axis_index("y"), lax.axis_index("c")
    chip = 2 * ax + ay
    batch_row = 4 * ax + 2 * ay + ac
    d = x.shape[2]
    aw = w_attn_proj.shape[1]
    sw = w_ssm_glu.shape[1]
    ff = N_CHIPS * ffn_conv_w.shape[2]
    ngroups = sw // SSM_GROUP

    c_all = _all_gather8(jnp.pad(c, ((0, 7), (0, 0))), "gather_c").reshape(N_DEV, 8, d)[:, 0, :]
    ncol = ada_w.shape[2]
    b_shard = lax.dynamic_slice(ada_b, (0, chip * ncol), (1, ncol))
    mod_blk = _ada_fwd(c_all, ada_w[0], b_shard, "ada_fwd")
    mod_all = _all_gather8(mod_blk, "gather_mod").reshape(N_CHIPS, 2, 8, ncol)[:, 0]
    mod = lax.dynamic_slice(mod_all, (0, batch_row, 0), (N_CHIPS, 1, ncol)).reshape(1, 6 * d)

    conv_w_all = _all_gather8(jnp.pad(ffn_conv_w[0], ((0, 5), (0, 0))), "gather_conv_w")
    conv_w_full = conv_w_all.reshape(N_CHIPS, 2, 8, ff // N_CHIPS)[:, 0, :3].transpose(1, 0, 2).reshape(3, ff)
    placed_in = _place_shard(w_in[0].T.astype(BF16), "place_shard_0", after=(mod, conv_w_full))
    (first,), started_in = _gather_start([[placed_in]], "gather_start_w_in")
    shards = [w_attn_proj[0].T.astype(BF16), w_ssm_glu[0].T.astype(BF16), w_out[0], w_ffn_up[0].T.astype(BF16), w_ffn_down[0]]
    placed = [_place_shard(s, f"place_shard_{i + 1}", after=(started_in,)) for i, s in enumerate(shards)]
    (mixer, ffn), started = _gather_start([placed[:3], placed[3:]], "gather_start_rest")
    ssm = (ssm_a_re[0], ssm_a_im[0], ssm_log_dt[0], ssm_b_re[0], ssm_b_im[0], ssm_c_re[0], ssm_c_im[0], ssm_d[0])
    ssm_tables = _ssm_setup(x.shape[1], *ssm[:7])
    (w_in_t,) = _forward_then_finish(first, (started, ssm_tables[1], ssm_tables[2], *ssm_tables[0].values()), "w_in")
    mod = mod + (started_in[0:1, 0:1] + started[0:1, 0:1])

    pending = {}

    def mixer_arrived(after):
        pending["mixer"] = _gather_forward(mixer[2], mixer[0], mixer[1], after, "gather_forward_mixer")
        return pending["mixer"][3]

    def later_weights(after):
        m_send, m_recv, m_bufs, _ = pending["mixer"]
        f_send, f_recv, f_bufs, f_started = _gather_forward(ffn[2], ffn[0], ffn[1], after, "gather_forward_ffn")
        mixer_weights = _gather_finish(m_bufs, m_send, m_recv, "gather_finish_mixer", (f_started,))
        return mixer_weights, lambda later: _gather_finish(f_bufs, f_send, f_recv, "gather_finish_ffn", later)

    def grads_started(tag, grads):
        pending["swap_" + tag], token = _swap_start(grads, f"swap_start_{tag}")
        return token

    def ffn_grads_ready(mod_now, after):
        pending["ffn"], token = _reduce_start(pending["swap_ffn"], "ffn", after)
        return mod_now + token[0:1, 0:1]

    comm = {"mixer_arrived": mixer_arrived, "later_weights": later_weights, "grads_started": grads_started,
            "ffn_grads_ready": ffn_grads_ready}
    loss_part, grad_x, small_parts = _local_step(
        x[0], loss_target[0], mod, w_in_t, comm, norm_mix_g, attn_sinks, ssm_tables, ssm[7], norm_ffn_g, conv_w_full, ffn_conv_b,
        final_g, aw, sw, ff)
    loss = lax.psum(loss_part[0, 0], ("x", "y", "c"))

    small_shapes = [p.shape for p in small_parts]
    part_buf = _flatten_pad(small_parts).astype(BF16)
    rows = part_buf.shape[0]
    gathered = _all_gather8(part_buf, "gather_small").reshape(N_DEV, rows, SMALL_COLS)
    pending["rest"], rest_token = _reduce_start(pending["swap_rest"], "rest", after=(gathered, grad_x))
    gup_t, grad_w_down = _reduce_finish(pending["ffn"], (rest_token,), "ffn")
    grad_w_up = gup_t.T
    summed = _sum_devices(gathered, "sum_small")
    (s_dmod, s_gain1, s_sinks, s_lr, s_li, s_bbr, s_bbi, s_cr, s_ci, s_dd, s_gain2, s_cb, s_cw, s_fg) = _split_flat(summed, small_shapes)
    _, ssm_vjp = jax.vjp(_ssm_discretize, *ssm[:5])
    g_a_re, g_a_im, g_log_dt, g_b_re, g_b_im = ssm_vjp((s_lr.reshape(ngroups, SSM_STATE), s_li.reshape(ngroups, SSM_STATE), s_bbr, s_bbi))
    g_c_re, g_c_im = jnp.swapaxes(s_cr, 1, 2), jnp.swapaxes(s_ci, 1, 2)
    g_conv_w = lax.dynamic_slice(s_cw, (0, chip * (ff // N_CHIPS)), (3, ff // N_CHIPS))

    dmod_all = gathered.reshape(N_DEV, -1)[:, :6 * d].astype(F32)
    dmod_shard = lax.dynamic_slice(dmod_all, (0, chip * ncol), (N_DEV, ncol))
    ada_res = _ada_bwd_adam(c_all.T, dmod_shard, ada_w[0], ms["ada_w"][0], vs["ada_w"][0], "ada_bwd_adam")

    res = {"ada_w": tuple(o[None] for o in ada_res)}

    def adam_big(nm, w, g):
        res[nm] = (g[None],) + tuple(o[None] for o in _adam(w[0], g, ms[nm][0], vs[nm][0], "adam_" + nm))

    adam_big("w_ffn_up", w_ffn_up, grad_w_up)
    adam_big("w_ffn_down", w_ffn_down, grad_w_down)

    small = [("ada_b", ada_b, s_dmod), ("norm_mix_g", norm_mix_g, s_gain1), ("attn_sinks", attn_sinks, s_sinks),
             ("ssm_a_re", ssm_a_re, g_a_re), ("ssm_a_im", ssm_a_im, g_a_im), ("ssm_log_dt", ssm_log_dt, g_log_dt),
             ("ssm_b_re", ssm_b_re, g_b_re), ("ssm_b_im", ssm_b_im, g_b_im), ("ssm_c_re", ssm_c_re, g_c_re),
             ("ssm_c_im", ssm_c_im, g_c_im), ("ssm_d", ssm_d, s_dd), ("norm_ffn_g", norm_ffn_g, s_gain2),
             ("ffn_conv_w", ffn_conv_w, g_conv_w), ("ffn_conv_b", ffn_conv_b, s_cb), ("final_g", final_g, s_fg)]
    shapes = [t[1].shape for t in small]
    bufs = [_flatten_pad([t[1] for t in small]), _flatten_pad([t[2] for t in small]),
            _flatten_pad([ms[t[0]] for t in small]), _flatten_pad([vs[t[0]] for t in small])]
    s_delta, s_m, s_v = _adam(*bufs, "adam_small")
    for t, dl, m2, v2 in zip(small, _split_flat(s_delta, shapes), _split_flat(s_m, shapes), _split_flat(s_v, shapes)):
        res[t[0]] = (t[2].reshape(t[1].shape), dl, m2, v2)

    done = (s_delta, res["w_ffn_up"][1], res["w_ffn_down"][1], res["ada_w"][1])
    gi_t, gap_t, gglu_t, grad_w_out = _reduce_finish(pending["rest"], done, "rest")
    adam_big("w_in", w_in, gi_t.T)
    adam_big("w_attn_proj", w_attn_proj, gap_t.T)
    adam_big("w_ssm_glu", w_ssm_glu, gglu_t.T)
    adam_big("w_out", w_out, grad_w_out)

    outs = [loss, grad_x[None]]
    for i in range(4):
        outs += [res[nm][i] for nm in WEIGHT_ORDER]
    return tuple(outs)


WEIGHT_ORDER = ("ada_w", "ada_b", "norm_mix_g", "w_in", "attn_sinks", "w_attn_proj", "ssm_a_re", "ssm_a_im", "ssm_log_dt", "ssm_b_re",
                "ssm_b_im", "ssm_c_re", "ssm_c_im", "ssm_d", "w_ssm_glu", "w_out", "norm_ffn_g", "w_ffn_up", "ffn_conv_w", "ffn_conv_b",
                "w_ffn_down", "final_g")


def kernel(x, c, ada_w, ada_b, norm_mix_g, w_in, attn_sinks, w_attn_proj, ssm_a_re, ssm_a_im, ssm_log_dt, ssm_b_re, ssm_b_im, ssm_c_re, ssm_c_im, ssm_d, w_ssm_glu, w_out, norm_ffn_g, w_ffn_up, ffn_conv_w, ffn_conv_b, w_ffn_down, final_g, loss_target, m_ada_w, m_ada_b, m_norm_mix_g, m_w_in, m_attn_sinks, m_w_attn_proj, m_ssm_a_re, m_ssm_a_im, m_ssm_log_dt, m_ssm_b_re, m_ssm_b_im, m_ssm_c_re, m_ssm_c_im, m_ssm_d, m_w_ssm_glu, m_w_out, m_norm_ffn_g, m_w_ffn_up, m_ffn_conv_w, m_ffn_conv_b, m_w_ffn_down, m_final_g, v_ada_w, v_ada_b, v_norm_mix_g, v_w_in, v_attn_sinks, v_w_attn_proj, v_ssm_a_re, v_ssm_a_im, v_ssm_log_dt, v_ssm_b_re, v_ssm_b_im, v_ssm_c_re, v_ssm_c_im, v_ssm_d, v_w_ssm_glu, v_w_out, v_norm_ffn_g, v_w_ffn_up, v_ffn_conv_w, v_ffn_conv_b, v_w_ffn_down, v_final_g):
    ms = dict(zip(WEIGHT_ORDER, (m_ada_w, m_ada_b, m_norm_mix_g, m_w_in, m_attn_sinks, m_w_attn_proj, m_ssm_a_re, m_ssm_a_im, m_ssm_log_dt,
                                 m_ssm_b_re, m_ssm_b_im, m_ssm_c_re, m_ssm_c_im, m_ssm_d, m_w_ssm_glu, m_w_out, m_norm_ffn_g, m_w_ffn_up,
                                 m_ffn_conv_w, m_ffn_conv_b, m_w_ffn_down, m_final_g)))
    vs = dict(zip(WEIGHT_ORDER, (v_ada_w, v_ada_b, v_norm_mix_g, v_w_in, v_attn_sinks, v_w_attn_proj, v_ssm_a_re, v_ssm_a_im, v_ssm_log_dt,
                                 v_ssm_b_re, v_ssm_b_im, v_ssm_c_re, v_ssm_c_im, v_ssm_d, v_w_ssm_glu, v_w_out, v_norm_ffn_g, v_w_ffn_up,
                                 v_ffn_conv_w, v_ffn_conv_b, v_w_ffn_down, v_final_g)))
    return _kernel_impl(x, c, ada_w, ada_b, norm_mix_g, w_in, attn_sinks, w_attn_proj, ssm_a_re, ssm_a_im, ssm_log_dt, ssm_b_re, ssm_b_im,
                        ssm_c_re, ssm_c_im, ssm_d, w_ssm_glu, w_out, norm_ffn_g, w_ffn_up, ffn_conv_w, ffn_conv_b, w_ffn_down, final_g,
                        loss_target, ms, vs)
```

```python
import math

import jax
import jax.numpy as jnp
from jax import lax
from jax.experimental import pallas as pl
from jax.experimental.pallas import tpu as pltpu

F32 = jnp.float32
BF16 = jnp.bfloat16
MESH = pl.DeviceIdType.MESH

HEAD_DIM = 64
N_KV_HEADS = 2
KV_WIDTH = N_KV_HEADS * HEAD_DIM
ATTN_BLOCK = 128
NEG_INF = -1e30
SSM_GROUP = 16
SSM_STATE = 64
GROUPS_PER_TILE = 8
RMS_EPS = 1e-6
ADAM_LR = 0.001
ADAM_B1 = 0.9
ADAM_B2 = 0.999
ADAM_EPS = 1e-08
ADAM_WD = 0.01
ADAM_STEP = 10
N_CHIPS = 4
N_DEV = 8
VMEM_LIMIT_BYTES = 56 * 1024 * 1024
LANES = 128
SMALL_COLS = 1024


def _tile(dim, target, mult=LANES):
    if dim <= target:
        return dim
    for t in range(target // mult * mult, 0, -mult):
        if dim % t == 0:
            return t
    raise ValueError(f"no tile for {dim}")


def _params(sem=None):
    return pltpu.CompilerParams(dimension_semantics=sem, vmem_limit_bytes=VMEM_LIMIT_BYTES)


def _sigmoid(x):
    return 1.0 / (1.0 + jnp.exp(-x))


def _matmul(a, b, mode, name, out_dtype=BF16, tm=1536, tn=1536, tk=2048, interleave=None, after=()):
    if mode == "nn":
        (m, k), (k2, n) = a.shape, b.shape
    elif mode == "nt":
        (m, k), (n, k2) = a.shape, b.shape
    else:
        (k, m), (k2, n) = a.shape, b.shape
    assert k == k2, (a.shape, b.shape, mode)
    if interleave is not None:
        tn, tk, tm = (interleave, tk, tm) if mode == "nt" else (tn, interleave, tm) if mode == "nn" else (tn, tk, interleave)
        half = {"nt": n, "nn": k, "tn": m}[mode] // (2 * interleave)

        def perm(blk):
            return blk // 2 + (blk % 2) * half
    else:
        def perm(blk):
            return blk
    tm, tn, tk = _tile(m, tm), _tile(n, tn), _tile(k, tk)
    nk = k // tk
    if mode == "tn":
        a_spec = pl.BlockSpec((tk, tm), lambda i, j, kk: (kk, i))
    else:
        a_spec = pl.BlockSpec((tm, tk), lambda i, j, kk: (i, kk))
    if mode == "nt":
        b_spec = pl.BlockSpec((tn, tk), lambda i, j, kk: (perm(j), kk))
    elif mode == "nn":
        b_spec = pl.BlockSpec((tk, tn), lambda i, j, kk: (perm(kk), j))
    else:
        b_spec = pl.BlockSpec((tk, tn), lambda i, j, kk: (kk, j))
    out_rows = perm if mode == "tn" else (lambda blk: blk)
    dims = {"nn": (((1,), (0,)), ((), ())), "nt": (((1,), (1,)), ((), ())), "tn": (((0,), (0,)), ((), ()))}[mode]

    def body(a_ref, b_ref, *rest):
        o_ref, acc_ref = rest[-2:]
        kk = pl.program_id(2)

        @pl.when(kk == 0)
        def _():
            acc_ref[...] = jnp.zeros_like(acc_ref)

        acc_ref[...] += lax.dot_general(a_ref[...], b_ref[...], dims, preferred_element_type=F32)

        @pl.when(kk == nk - 1)
        def _():
            o_ref[...] = acc_ref[...].astype(o_ref.dtype)

    return pl.pallas_call(
        body,
        out_shape=jax.ShapeDtypeStruct((m, n), out_dtype),
        grid=(m // tm, n // tn, nk),
        in_specs=[a_spec, b_spec] + [pl.BlockSpec(memory_space=pl.ANY)] * len(after),
        out_specs=pl.BlockSpec((tm, tn), lambda i, j, kk: (out_rows(i), j)),
        scratch_shapes=[pltpu.VMEM((tm, tn), F32)],
        name=name,
        compiler_params=_params(("parallel", "parallel", "arbitrary")),
    )(a, b, *after)


def _row_spec(tl, w, col=0):
    return pl.BlockSpec((tl, w), lambda i, col=col: (i, col))


def _vec_spec(w, col=0):
    return pl.BlockSpec((1, w), lambda i, col=col: (0, col))


def _norm_mod(x, gain, mod, sc_col, sh_col, name):
    l, d = x.shape
    tl = _tile(l, 256, 8)

    def body(x_ref, g_ref, sc_ref, sh_ref, h_ref):
        xv = x_ref[...]
        r = lax.rsqrt(jnp.mean(xv * xv, axis=-1, keepdims=True) + RMS_EPS)
        h_ref[...] = ((xv * r) * g_ref[...] * (1.0 + sc_ref[...]) + sh_ref[...]).astype(BF16)

    return pl.pallas_call(
        body,
        out_shape=jax.ShapeDtypeStruct((l, d), BF16),
        grid=(l // tl,),
        in_specs=[_row_spec(tl, d), _vec_spec(d), _vec_spec(d, sc_col), _vec_spec(d, sh_col)],
        out_specs=_row_spec(tl, d),
        name=name,
        compiler_params=_params(("parallel",)),
    )(x, gain, mod, mod)


def _resid_norm_mod(x, mo, gain, mod, gate_col, sc_col, sh_col, name):
    l, d = x.shape
    tl = _tile(l, 256, 8)

    def body(x_ref, mo_ref, g_ref, gate_ref, sc_ref, sh_ref, x2_ref, h_ref):
        xv = x_ref[...] + gate_ref[...] * mo_ref[...]
        x2_ref[...] = xv
        r = lax.rsqrt(jnp.mean(xv * xv, axis=-1, keepdims=True) + RMS_EPS)
        h_ref[...] = ((xv * r) * g_ref[...] * (1.0 + sc_ref[...]) + sh_ref[...]).astype(BF16)

    return pl.pallas_call(
        body,
        out_shape=(jax.ShapeDtypeStruct((l, d), F32), jax.ShapeDtypeStruct((l, d), BF16)),
        grid=(l // tl,),
        in_specs=[_row_spec(tl, d), _row_spec(tl, d), _vec_spec(d), _vec_spec(d, gate_col), _vec_spec(d, sc_col),
                  _vec_spec(d, sh_col)],
        out_specs=(_row_spec(tl, d), _row_spec(tl, d)),
        name=name,
        compiler_params=_params(("parallel",)),
    )(x, mo, gain, mod, mod, mod)


def _final_loss(x2, f, mod, gate_col, final_g, target, name):
    l, d = x2.shape
    tl = _tile(l, 256, 8)

    def body(x2_ref, f_ref, gate_ref, fg_ref, t_ref, loss_ref, dfg_ref, dgate_ref, dx3_ref, df_ref):
        i = pl.program_id(0)
        fv = f_ref[...]
        x3 = x2_ref[...] + gate_ref[...] * fv
        r = lax.rsqrt(jnp.mean(x3 * x3, axis=-1, keepdims=True) + RMS_EPS)
        xh = x3 * r
        err = xh * fg_ref[...] - t_ref[...]
        part = 0.5 * jnp.sum(jnp.mean(err * err, axis=-1, keepdims=True), axis=0, keepdims=True)
        dout = err * (1.0 / d)
        dxh = dout * fg_ref[...]
        dx3 = r * (dxh - xh * jnp.mean(dxh * xh, axis=-1, keepdims=True))
        dx3_ref[...] = dx3
        df_ref[...] = (gate_ref[...] * dx3).astype(BF16)

        @pl.when(i == 0)
        def _():
            loss_ref[...] = jnp.zeros_like(loss_ref)
            dfg_ref[...] = jnp.zeros_like(dfg_ref)
            dgate_ref[...] = jnp.zeros_like(dgate_ref)

        loss_ref[...] += jnp.broadcast_to(part, loss_ref.shape)
        dfg_ref[...] += jnp.sum(dout * xh, axis=0, keepdims=True)
        dgate_ref[...] += jnp.sum(dx3 * fv, axis=0, keepdims=True)

    vec = pl.BlockSpec((1, d), lambda i: (0, 0))
    return pl.pallas_call(
        body,
        out_shape=(jax.ShapeDtypeStruct((1, LANES), F32), jax.ShapeDtypeStruct((1, d), F32),
                   jax.ShapeDtypeStruct((1, d), F32), jax.ShapeDtypeStruct((l, d), F32),
                   jax.ShapeDtypeStruct((l, d), BF16)),
        grid=(l // tl,),
        in_specs=[_row_spec(tl, d), _row_spec(tl, d), _vec_spec(d, gate_col), vec, _row_spec(tl, d)],
        out_specs=(pl.BlockSpec((1, LANES), lambda i: (0, 0)), vec, vec, _row_spec(tl, d), _row_spec(tl, d)),
        name=name,
        compiler_params=_params(("arbitrary",)),
    )(x2, f, mod, final_g, target)


def _norm_mod_bwd(dh, x, dx_res, gain, mod, sc_col, name, branch=None, gate_col=None):
    l, d = x.shape
    tl = _tile(l, 256, 8)
    with_gate = branch is not None

    def body(*refs):
        if with_gate:
            dh_ref, x_ref, dr_ref, g_ref, sc_ref, br_ref, gate_ref, dx_ref, dsh_ref, dsc_ref, dg_ref, dm_ref, dgate_ref = refs
        else:
            dh_ref, x_ref, dr_ref, g_ref, sc_ref, dx_ref, dsh_ref, dsc_ref, dg_ref = refs
        i = pl.program_id(0)
        xv = x_ref[...]
        dhv = dh_ref[...].astype(F32)
        r = lax.rsqrt(jnp.mean(xv * xv, axis=-1, keepdims=True) + RMS_EPS)
        xh = xv * r
        dn = dhv * (1.0 + sc_ref[...])
        dxh = dn * g_ref[...]
        dx = dr_ref[...] + r * (dxh - xh * jnp.mean(dxh * xh, axis=-1, keepdims=True))
        dx_ref[...] = dx

        @pl.when(i == 0)
        def _():
            dsh_ref[...] = jnp.zeros_like(dsh_ref)
            dsc_ref[...] = jnp.zeros_like(dsc_ref)
            dg_ref[...] = jnp.zeros_like(dg_ref)
            if with_gate:
                dgate_ref[...] = jnp.zeros_like(dgate_ref)

        dsh_ref[...] += jnp.sum(dhv, axis=0, keepdims=True)
        dsc_ref[...] += jnp.sum(dhv * (xh * g_ref[...]), axis=0, keepdims=True)
        dg_ref[...] += jnp.sum(dn * xh, axis=0, keepdims=True)
        if with_gate:
            dm_ref[...] = (gate_ref[...] * dx).astype(BF16)
            dgate_ref[...] += jnp.sum(dx * br_ref[...], axis=0, keepdims=True)

    vec = pl.BlockSpec((1, d), lambda i: (0, 0))
    in_specs = [_row_spec(tl, d), _row_spec(tl, d), _row_spec(tl, d), vec, _vec_spec(d, sc_col)]
    args = [dh, x, dx_res, gain, mod]
    out_shape = [jax.ShapeDtypeStruct((l, d), F32)] + [jax.ShapeDtypeStruct((1, d), F32)] * 3
    out_specs = [_row_spec(tl, d), vec, vec, vec]
    if with_gate:
        in_specs += [_row_spec(tl, d), _vec_spec(d, gate_col)]
        args += [branch, mod]
        out_shape += [jax.ShapeDtypeStruct((l, d), BF16), jax.ShapeDtypeStruct((1, d), F32)]
        out_specs += [_row_spec(tl, d), vec]
    return pl.pallas_call(
        body, out_shape=tuple(out_shape), grid=(l // tl,), in_specs=in_specs, out_specs=tuple(out_specs),
        name=name, compiler_params=_params(("arbitrary",)),
    )(*args)


def _attn_mask(n, rows):
    del rows
    qi = lax.broadcasted_iota(jnp.int32, (ATTN_BLOCK, 2 * ATTN_BLOCK), 0)
    kj = lax.broadcasted_iota(jnp.int32, (ATTN_BLOCK, 2 * ATTN_BLOCK), 1)
    rel = qi + ATTN_BLOCK - kj
    return jnp.where((rel >= 0) & (rel < ATTN_BLOCK) & ((kj >= ATTN_BLOCK) | (n > 0)), 0.0, NEG_INF)


def _attn_probs(qs, kh, sink, mask):
    rows = qs.shape[0]
    s = lax.dot_general(qs, kh, (((1,), (1,)), ((), ())), preferred_element_type=F32) * (HEAD_DIM ** -0.5)
    s = s.reshape(-1, ATTN_BLOCK, 2 * ATTN_BLOCK) + mask[None]
    m = jnp.maximum(jnp.max(s, axis=-1, keepdims=True), sink)
    p = jnp.exp(s - m)
    es = jnp.exp(sink - m)
    inv = 1.0 / (jnp.sum(p, axis=-1, keepdims=True) + es)
    return (p * inv).reshape(rows, 2 * ATTN_BLOCK), (es * inv).reshape(rows, 1)


def _stack_heads(src_ref, dst_ref, g, qpk):
    for i in range(qpk):
        h = g * qpk + i
        dst_ref[i * ATTN_BLOCK:(i + 1) * ATTN_BLOCK, :] = src_ref[:, h * HEAD_DIM:(h + 1) * HEAD_DIM]


def _unstack_heads(val, dst_ref, g, qpk):
    for i in range(qpk):
        h = g * qpk + i
        dst_ref[:, h * HEAD_DIM:(h + 1) * HEAD_DIM] = val[i * ATTN_BLOCK:(i + 1) * ATTN_BLOCK, :].astype(dst_ref.dtype)


def _sink_column(sinks):
    return sinks.reshape(-1, 1, 1)


def _sink_spec(nq):
    return pl.BlockSpec((nq, 1, 1), lambda n: (0, 0, 0))


def _attn_specs(aw):
    kvb = aw // (2 * KV_WIDTH)
    q_spec = pl.BlockSpec((ATTN_BLOCK, aw), lambda n: (n, 0))
    kv_cur = pl.BlockSpec((ATTN_BLOCK, 2 * KV_WIDTH), lambda n: (n, kvb))
    kv_prev = pl.BlockSpec((ATTN_BLOCK, 2 * KV_WIDTH), lambda n: (jnp.maximum(n - 1, 0), kvb))
    return q_spec, kv_cur, kv_prev


def _attn_fwd(proj, sinks, aw, name, after=()):
    l = proj.shape[0]
    nq = aw // HEAD_DIM
    qpk = nq // N_KV_HEADS
    assert aw % (2 * KV_WIDTH) == 0

    rows = qpk * ATTN_BLOCK

    def body(q_ref, kvc_ref, kvp_ref, sink_ref, *rest):
        o_ref = rest[-1]
        n = pl.program_id(0)
        valid = _attn_mask(n, rows) == 0.0
        kv = jnp.concatenate([kvp_ref[...], kvc_ref[...]], axis=0)
        for h in range(nq):
            g = h // qpk
            qh = q_ref[:, h * HEAD_DIM:(h + 1) * HEAD_DIM]
            kh = kv[:, g * HEAD_DIM:(g + 1) * HEAD_DIM]
            vh = kv[:, KV_WIDTH + g * HEAD_DIM:KV_WIDTH + (g + 1) * HEAD_DIM]
            sink = sink_ref[0:1, h:h + 1]
            s = lax.dot_general(qh, kh, (((1,), (1,)), ((), ())), preferred_element_type=F32) * (HEAD_DIM ** -0.5)
            s = jnp.where(valid, s, NEG_INF)
            m = jnp.maximum(jnp.max(s, axis=-1, keepdims=True), sink)
            p = jnp.exp(s - m)
            p = p * (1.0 / (jnp.sum(p, axis=-1, keepdims=True) + jnp.exp(sink - m)))
            o = jnp.dot(p.astype(BF16), vh, preferred_element_type=F32)
            o_ref[:, h * HEAD_DIM:(h + 1) * HEAD_DIM] = o.astype(BF16)

    q_spec, kv_cur, kv_prev = _attn_specs(aw)
    return pl.pallas_call(
        body,
        out_shape=jax.ShapeDtypeStruct((l, aw), BF16),
        grid=(l // ATTN_BLOCK,),
        in_specs=[q_spec, kv_cur, kv_prev, pl.BlockSpec((1, nq), lambda n: (0, 0))] + [pl.BlockSpec(memory_space=pl.ANY)] * len(after),
        out_specs=pl.BlockSpec((ATTN_BLOCK, aw), lambda n: (n, 0)),
        name=name,
        compiler_params=_params(("parallel",)),
    )(proj, proj, proj, sinks, *after)


def _attn_bwd(proj, sinks, dattn, aw, name):
    l = proj.shape[0]
    nq = aw // HEAD_DIM
    qpk = nq // N_KV_HEADS
    scale = HEAD_DIM ** -0.5

    rows = qpk * ATTN_BLOCK
    tn_dims = (((0,), (0,)), ((), ()))

    def body(q_ref, kvc_ref, kvp_ref, sink_ref, do_ref, dq_ref, dcur_ref, dprev_ref, dsink_ref, q_scr, do_scr):
        n = pl.program_id(0)
        mask = _attn_mask(n, rows)
        kv = jnp.concatenate([kvp_ref[...], kvc_ref[...]], axis=0)
        lane = lax.broadcasted_iota(jnp.int32, (1, nq), 1)
        dsink = jnp.zeros((1, nq), F32)
        dks, dvs = [], []
        for g in range(N_KV_HEADS):
            kh = kv[:, g * HEAD_DIM:(g + 1) * HEAD_DIM]
            vh = kv[:, KV_WIDTH + g * HEAD_DIM:KV_WIDTH + (g + 1) * HEAD_DIM]
            _stack_heads(q_ref, q_scr.at[g], g, qpk)
            _stack_heads(do_ref, do_scr.at[g], g, qpk)
            qs, dos = q_scr[g], do_scr[g]
            p, ps = _attn_probs(qs, kh, sink_ref[g * qpk:(g + 1) * qpk], mask)
            pb = p.astype(BF16)
            o = jnp.dot(pb, vh, preferred_element_type=F32)
            delta = jnp.sum(dos.astype(F32) * o, axis=-1, keepdims=True)
            dp = lax.dot_general(dos, vh, (((1,), (1,)), ((), ())), preferred_element_type=F32)
            ds = (p * (dp - delta)).astype(BF16)
            _unstack_heads(jnp.dot(ds, kh, preferred_element_type=F32) * scale, dq_ref, g, qpk)
            dks.append(lax.dot_general(ds, qs, tn_dims, preferred_element_type=F32) * scale)
            dvs.append(lax.dot_general(pb, dos, tn_dims, preferred_element_type=F32))
            t = ps * delta
            for i in range(qpk):
                part = -jnp.sum(t[i * ATTN_BLOCK:(i + 1) * ATTN_BLOCK, :], axis=0, keepdims=True)
                dsink += jnp.where(lane == g * qpk + i, part, 0.0)
        dkv = jnp.concatenate(dks + dvs, axis=1)
        dprev_ref[...] = dkv[:ATTN_BLOCK]
        dcur_ref[...] = dkv[ATTN_BLOCK:]

        @pl.when(n == 0)
        def _():
            dsink_ref[...] = jnp.zeros_like(dsink_ref)

        dsink_ref[...] += dsink

    q_spec, kv_cur, kv_prev = _attn_specs(aw)
    blk = pl.BlockSpec((ATTN_BLOCK, 2 * KV_WIDTH), lambda n: (n, 0))
    return pl.pallas_call(
        body,
        out_shape=(jax.ShapeDtypeStruct((l, aw), BF16), jax.ShapeDtypeStruct((l, 2 * KV_WIDTH), F32),
                   jax.ShapeDtypeStruct((l, 2 * KV_WIDTH), F32), jax.ShapeDtypeStruct((1, nq), F32)),
        grid=(l // ATTN_BLOCK,),
        in_specs=[q_spec, kv_cur, kv_prev, _sink_spec(nq),
                  pl.BlockSpec((ATTN_BLOCK, aw), lambda n: (n, 0))],
        out_specs=(pl.BlockSpec((ATTN_BLOCK, aw), lambda n: (n, 0)), blk, blk, pl.BlockSpec((1, nq), lambda n: (0, 0))),
        scratch_shapes=[pltpu.VMEM((N_KV_HEADS, rows, HEAD_DIM), BF16)] * 2,
        name=name,
        compiler_params=_params(("arbitrary",)),
    )(proj, proj, proj, _sink_column(sinks), dattn)


def _ssm_discretize(a_re, a_im, log_dt, b_re, b_im):
    dt = jnp.exp(log_dt)[:, None]
    mag = jnp.exp(a_re * dt)
    lr, li = mag * jnp.cos(a_im * dt), mag * jnp.sin(a_im * dt)
    den = a_re * a_re + a_im * a_im
    zr = ((lr - 1.0) * a_re + li * a_im) / den
    zi = (li * a_re - (lr - 1.0) * a_im) / den
    bbar_r = zr[:, :, None] * b_re - zi[:, :, None] * b_im
    bbar_i = zr[:, :, None] * b_im + zi[:, :, None] * b_re
    return lr, li, bbar_r, bbar_i


def _cmul(ar, ai, br, bi):
    return ar * br - ai * bi, ar * bi + ai * br


N_SEG = 8


def _cpow(ar, ai, n):
    out, br, bi = None, ar, ai
    while n:
        if n & 1:
            out = (br, bi) if out is None else _cmul(*out, br, bi)
        br, bi = _cmul(br, bi, br, bi)
        n >>= 1
    return out


def _scan_tables(lr, li, seg):
    lr, li = lr.reshape(1, -1), li.reshape(1, -1)
    row = jnp.arange(N_SEG)[:, None]
    ones = jnp.ones((N_SEG, 1), F32)
    fwd, bwd = [], []
    for d in (1, 2, 4):
        pr, pi = _cpow(lr, li, seg * d)
        fwd += [jnp.where(row >= d, pr, 0.0), jnp.where(row >= d, pi, 0.0)]
        bwd += [jnp.where(row < N_SEG - d, pr, 0.0), jnp.where(row < N_SEG - d, -pi, 0.0)]
    fwd += [ones * lr, ones * li]
    bwd += [ones * lr, ones * -li]
    return jnp.concatenate(fwd, 0), jnp.concatenate(bwd, 0)


def _pack_in(b):
    g, n, p = b.shape
    t = g // GROUPS_PER_TILE
    eye = jnp.eye(GROUPS_PER_TILE, dtype=b.dtype)
    bb = b.reshape(t, GROUPS_PER_TILE, n, p)
    return jnp.einsum("tgnp,gh->tgphn", bb, eye).reshape(t, GROUPS_PER_TILE * p, GROUPS_PER_TILE * n)


def _pack_out(c):
    g, p, n = c.shape
    t = g // GROUPS_PER_TILE
    eye = jnp.eye(GROUPS_PER_TILE, dtype=c.dtype)
    cc = c.reshape(t, GROUPS_PER_TILE, p, n)
    return jnp.einsum("tgpn,gh->tgnhp", cc, eye).reshape(t, GROUPS_PER_TILE * n, GROUPS_PER_TILE * p)


def _unpack_diag(x, n, p):
    t = x.shape[0]
    xx = x.reshape(t, GROUPS_PER_TILE, n, GROUPS_PER_TILE, p)
    eye = jnp.eye(GROUPS_PER_TILE, dtype=x.dtype)
    return jnp.einsum("tgnhp,gh->tgnp", xx, eye).reshape(t * GROUPS_PER_TILE, n, p)


def _scan_rows(hr_ref, hi_ref, tab_ref, l, reverse, prev_refs=None):
    w = hr_ref.shape[1]
    tabs = [tab_ref[pl.ds(8 * i, 8), :] for i in range(8)]
    nchunk = l // 8
    row = lax.broadcasted_iota(jnp.int32, (8, w), 0)

    def step(s, carry):
        k = nchunk - 1 - s if reverse else s
        t8 = pl.multiple_of(k * 8, 8)
        hr = hr_ref[pl.ds(t8, 8), :]
        hi = hi_ref[pl.ds(t8, 8), :]
        for idx, d in enumerate((1, 2, 4)):
            mr, mi = tabs[2 * idx], tabs[2 * idx + 1]
            shift = 8 - d if reverse else d
            sr = pltpu.roll(hr, shift, 0)
            si = pltpu.roll(hi, shift, 0)
            hr, hi = hr + mr * sr - mi * si, hi + mr * si + mi * sr
        cr, ci = carry[0], carry[1]
        hr, hi = hr + tabs[6] * cr - tabs[7] * ci, hi + tabs[6] * ci + tabs[7] * cr
        hr_ref[pl.ds(t8, 8), :] = hr
        hi_ref[pl.ds(t8, 8), :] = hi
        if not reverse:
            return hr[7:8, :], hi[7:8, :]
        out = (hr[0:1, :], hi[0:1, :])
        if prev_refs is None:
            return out
        fr_ref, fi_ref = prev_refs
        tp = pl.multiple_of(jnp.maximum(k - 1, 0) * 8, 8)
        keep = jnp.where(k > 0, 1.0, 0.0)
        lr_last = fr_ref[pl.ds(tp, 8), :][7:8, :] * keep
        li_last = fi_ref[pl.ds(tp, 8), :][7:8, :] * keep
        pr = jnp.where(row == 0, lr_last, pltpu.roll(fr_ref[pl.ds(t8, 8), :], 1, 0))
        pi = jnp.where(row == 0, li_last, pltpu.roll(fi_ref[pl.ds(t8, 8), :], 1, 0))
        return out + (carry[2] + hr * pr + hi * pi, carry[3] + hi * pr - hr * pi)

    zero = jnp.zeros((1, w), F32)
    init = (zero, zero)
    if reverse and prev_refs is not None:
        init += (jnp.zeros((8, w), F32), jnp.zeros((8, w), F32))
    return lax.fori_loop(0, nchunk, step, init)


def _seg_scan(hr_ref, hi_ref, tab_ref, l, reverse, states_refs=None):
    nq = hr_ref.shape[0]
    seg = l // N_SEG
    span = 8 * N_SEG
    nblk = seg // 8
    row = lax.broadcasted_iota(jnp.int32, (N_SEG, LANES), 0)

    def tab(r0, q):
        return tab_ref[r0:r0 + 8, q * LANES:(q + 1) * LANES]

    lam = [(tab(48, q), tab(56, q)) for q in range(nq)]

    def views(refs, q, jb):
        base = pl.multiple_of((nblk - 1 - jb if reverse else jb) * span, span)
        return [r.at[q, pl.ds(base, span), :] for r in refs]

    def local_rows():
        return range(7, -1, -1) if reverse else range(8)

    def at(r):
        return pl.ds(r * N_SEG, N_SEG)

    def pass1(jb, carry):
        hs = list(carry)
        for q in range(nq):
            vr, vi = views((hr_ref, hi_ref), q, jb)
            lr, li = lam[q]
            h_r, h_i = hs[2 * q], hs[2 * q + 1]
            for r in local_rows():
                h_r, h_i = lr * h_r - li * h_i + vr[at(r), :], lr * h_i + li * h_r + vi[at(r), :]
                vr[at(r), :] = h_r
                vi[at(r), :] = h_i
            hs[2 * q], hs[2 * q + 1] = h_r, h_i
        return tuple(hs)

    zero = jnp.zeros((N_SEG, LANES), F32)
    ends = lax.fori_loop(0, nblk, pass1, (zero,) * (2 * nq))

    carry_in = []
    for q in range(nq):
        er, ei = ends[2 * q], ends[2 * q + 1]
        for idx, d in enumerate((1, 2, 4)):
            mr, mi = tab(16 * idx, q), tab(16 * idx + 8, q)
            shift = N_SEG - d if reverse else d
            sr, si = pltpu.roll(er, shift, 0), pltpu.roll(ei, shift, 0)
            er, ei = er + mr * sr - mi * si, ei + mr * si + mi * sr
        if reverse:
            keep, shift = row < N_SEG - 1, N_SEG - 1
        else:
            keep, shift = row >= 1, 1
        carry_in += [jnp.where(keep, pltpu.roll(er, shift, 0), 0.0), jnp.where(keep, pltpu.roll(ei, shift, 0), 0.0)]

    with_acc = states_refs is not None

    def pass2(jb, carry):
        cs = list(carry)
        for q in range(nq):
            vr, vi = views((hr_ref, hi_ref), q, jb)
            lr, li = lam[q]
            d_r, d_i = cs[2 * q], cs[2 * q + 1]
            if with_acc:
                fr, fi = views(states_refs, q, jb)
                n_r, n_i, a_r, a_i = cs[2 * nq + 4 * q:2 * nq + 4 * q + 4]
            for r in local_rows():
                d_r, d_i = lr * d_r - li * d_i, lr * d_i + li * d_r
                g_r, g_i = vr[at(r), :] + d_r, vi[at(r), :] + d_i
                vr[at(r), :] = g_r
                vi[at(r), :] = g_i
                if with_acc:
                    p_r, p_i = fr[at(r), :], fi[at(r), :]
                    a_r, a_i = a_r + n_r * p_r + n_i * p_i, a_i + n_i * p_r - n_r * p_i
                    n_r, n_i = g_r, g_i
            cs[2 * q], cs[2 * q + 1] = d_r, d_i
            if with_acc:
                cs[2 * nq + 4 * q:2 * nq + 4 * q + 4] = [n_r, n_i, a_r, a_i]
        return tuple(cs)

    init = list(carry_in)
    if with_acc:
        for q in range(nq):
            init += [carry_in[2 * q], carry_in[2 * q + 1], zero, zero]
    out = lax.fori_loop(0, nblk, pass2, tuple(init))
    if with_acc:
        return [(out[2 * nq + 4 * q + 2], out[2 * nq + 4 * q + 3]) for q in range(nq)]
    return None


def _put_states(ref, rows, val):
    for q in range(ref.shape[0]):
        ref[q, rows, :] = val[:, q * LANES:(q + 1) * LANES]


def _get_states(ref, rows):
    return jnp.concatenate([ref[q, rows, :] for q in range(ref.shape[0])], axis=1)


def _s5_dims(sw):
    chan = GROUPS_PER_TILE * SSM_GROUP
    states = GROUPS_PER_TILE * SSM_STATE
    assert chan == LANES and sw % chan == 0
    return sw // chan, chan, states


def _interleave(x):
    l, w = x.shape
    return x.reshape(N_SEG, l // N_SEG, w).transpose(1, 0, 2).reshape(l, w)


def _deinterleave(x):
    l, w = x.shape
    return x.reshape(l // N_SEG, N_SEG, w).transpose(1, 0, 2).reshape(l, w)


def _s5_fwd(proj, u_off, packs, dvec, tab_f, sw, name):
    l = proj.shape[0]
    nt, chan, states = _s5_dims(sw)
    ch = _tile(l, 512, 8)
    ub = u_off // chan
    assert u_off % chan == 0

    def body(u_ref, br_ref, bi_ref, cr_ref, ci_ref, d_ref, tab_ref, y_ref, gy_ref, hr_ref, hi_ref):
        for i in range(l // ch):
            rows = pl.ds(i * ch, ch)
            u = u_ref[rows, :]
            _put_states(hr_ref, rows, jnp.dot(u, br_ref[0], preferred_element_type=F32))
            _put_states(hi_ref, rows, jnp.dot(u, bi_ref[0], preferred_element_type=F32))
        _seg_scan(hr_ref, hi_ref, tab_ref, l, reverse=False)
        for i in range(l // ch):
            rows = pl.ds(i * ch, ch)
            y = jnp.dot(_get_states(hr_ref, rows).astype(BF16), cr_ref[0], preferred_element_type=F32)
            y -= jnp.dot(_get_states(hi_ref, rows).astype(BF16), ci_ref[0], preferred_element_type=F32)
            y = y + d_ref[...] * u_ref[rows, :].astype(F32)
            y_ref[rows, :] = y
            gy_ref[rows, :] = _gelu_value(y).astype(BF16)

    pin = pl.BlockSpec((1, chan, states), lambda t: (t, 0, 0))
    pout = pl.BlockSpec((1, states, chan), lambda t: (t, 0, 0))
    return pl.pallas_call(
        body,
        out_shape=(jax.ShapeDtypeStruct((l, sw), F32), jax.ShapeDtypeStruct((l, sw), BF16)),
        grid=(nt,),
        in_specs=[pl.BlockSpec((l, chan), lambda t: (0, ub + t)), pin, pin, pout, pout,
                  pl.BlockSpec((1, chan), lambda t: (0, t)), pl.BlockSpec((64, states), lambda t: (0, t))],
        out_specs=(pl.BlockSpec((l, chan), lambda t: (0, t)), pl.BlockSpec((l, chan), lambda t: (0, t))),
        scratch_shapes=[pltpu.VMEM((states // LANES, l, LANES), F32)] * 2,
        name=name,
        compiler_params=_params(("parallel",)),
    )(proj, packs["br"], packs["bi"], packs["cr"], packs["ci"], dvec, tab_f)


def _s5_bwd(proj, u_off, y, dgy, packs, dvec, tab_f, tab_b, sw, name):
    l = proj.shape[0]
    nt, chan, states = _s5_dims(sw)
    ch = _tile(l, 512, 8)
    ub = u_off // chan
    tn_dims = (((0,), (0,)), ((), ()))

    def body(u_ref, y_ref, dgy_ref, br_ref, bi_ref, brt_ref, bit_ref, crt_ref, cit_ref, d_ref, tabf_ref, tabb_ref,
             du_ref, dlam_ref, dbr_ref, dbi_ref, dcr_ref, dci_ref, dd_ref, hr_ref, hi_ref, gr_ref, gi_ref, dy_ref):
        for i in range(l // ch):
            rows = pl.ds(i * ch, ch)
            u = u_ref[rows, :]
            _put_states(hr_ref, rows, jnp.dot(u, br_ref[0], preferred_element_type=F32))
            _put_states(hi_ref, rows, jnp.dot(u, bi_ref[0], preferred_element_type=F32))
            dyv = (dgy_ref[rows, :].astype(F32) * _gelu_slope(y_ref[rows, :])).astype(BF16)
            dy_ref[rows, :] = dyv
            _put_states(gr_ref, rows, jnp.dot(dyv, crt_ref[0], preferred_element_type=F32))
            _put_states(gi_ref, rows, -jnp.dot(dyv, cit_ref[0], preferred_element_type=F32))
        _seg_scan(hr_ref, hi_ref, tabf_ref, l, reverse=False)
        accs = _seg_scan(gr_ref, gi_ref, tabb_ref, l, reverse=True, states_refs=(hr_ref, hi_ref))
        dlam_ref[...] = jnp.concatenate(
            [jnp.concatenate([jnp.sum(a[0], axis=0, keepdims=True) for a in accs], axis=1),
             jnp.concatenate([jnp.sum(a[1], axis=0, keepdims=True) for a in accs], axis=1), jnp.zeros((6, states), F32)], axis=0)
        dbr_ref[...] = jnp.zeros_like(dbr_ref)
        dbi_ref[...] = jnp.zeros_like(dbi_ref)
        dcr_ref[...] = jnp.zeros_like(dcr_ref)
        dci_ref[...] = jnp.zeros_like(dci_ref)
        dd = jnp.zeros((1, chan), F32)
        for i in range(l // ch):
            rows = pl.ds(i * ch, ch)
            u = u_ref[rows, :]
            dyv = dy_ref[rows, :]
            grb = _get_states(gr_ref, rows).astype(BF16)
            gib = _get_states(gi_ref, rows).astype(BF16)
            dbr_ref[0] += lax.dot_general(grb, u, tn_dims, preferred_element_type=F32)
            dbi_ref[0] += lax.dot_general(gib, u, tn_dims, preferred_element_type=F32)
            dcr_ref[0] += lax.dot_general(_get_states(hr_ref, rows).astype(BF16), dyv, tn_dims, preferred_element_type=F32)
            dci_ref[0] -= lax.dot_general(_get_states(hi_ref, rows).astype(BF16), dyv, tn_dims, preferred_element_type=F32)
            du = jnp.dot(grb, brt_ref[0], preferred_element_type=F32) + jnp.dot(gib, bit_ref[0], preferred_element_type=F32)
            dyf = dyv.astype(F32)
            du_ref[rows, :] = (du + d_ref[...] * dyf).astype(BF16)
            dd += jnp.sum(dyf * u.astype(F32), axis=0, keepdims=True)
        dd_ref[...] = dd

    pin = pl.BlockSpec((1, chan, states), lambda t: (t, 0, 0))
    pout = pl.BlockSpec((1, states, chan), lambda t: (t, 0, 0))
    seq = pl.BlockSpec((l, chan), lambda t: (0, t))
    tab = pl.BlockSpec((64, states), lambda t: (0, t))
    vec = pl.BlockSpec((1, chan), lambda t: (0, t))
    pack_shape = jax.ShapeDtypeStruct((nt, states, chan), F32)
    return pl.pallas_call(
        body,
        out_shape=(jax.ShapeDtypeStruct((l, sw), BF16), jax.ShapeDtypeStruct((8, nt * states), F32),
                   pack_shape, pack_shape, pack_shape, pack_shape, jax.ShapeDtypeStruct((1, sw), F32)),
        grid=(nt,),
        in_specs=[pl.BlockSpec((l, chan), lambda t: (0, ub + t)), seq, seq, pin, pin, pout, pout, pin, pin, vec, tab, tab],
        out_specs=(seq, pl.BlockSpec((8, states), lambda t: (0, t)), pout, pout, pout, pout, vec),
        scratch_shapes=[pltpu.VMEM((states // LANES, l, LANES), F32)] * 4 + [pltpu.VMEM((l, chan), BF16)],
        name=name,
        compiler_params=_params(("parallel",)),
    )(proj, y, dgy, packs["br"], packs["bi"], packs["brt"], packs["bit"], packs["crt"], packs["cit"], dvec, tab_f, tab_b)


GELU_K = math.sqrt(2.0 / math.pi)
GELU_C = 0.044715


def _gelu_value(v):
    return 0.5 * v * (1.0 + jnp.tanh(GELU_K * (v + GELU_C * v * v * v)))


def _gelu_slope(v):
    t = jnp.tanh(GELU_K * (v + GELU_C * v * v * v))
    return 0.5 * (1.0 + t) + 0.5 * v * (1.0 - t * t) * GELU_K * (1.0 + 3.0 * GELU_C * v * v)


def _mix(ga, gs, attn_out, glu, name):
    l, d = ga.shape
    tl = _tile(l, 256, 16)

    def body(ga_ref, gs_ref, a_ref, u_ref, o_ref):
        ssm = u_ref[:, :d].astype(F32) * _sigmoid(u_ref[:, d:].astype(F32))
        o_ref[...] = (_sigmoid(ga_ref[...].astype(F32)) * a_ref[...].astype(F32)
                      + _sigmoid(gs_ref[...].astype(F32)) * ssm).astype(BF16)

    return pl.pallas_call(
        body, out_shape=jax.ShapeDtypeStruct((l, d), BF16), grid=(l // tl,),
        in_specs=[_row_spec(tl, d), _row_spec(tl, d), _row_spec(tl, d), _row_spec(tl, 2 * d)], out_specs=_row_spec(tl, d),
        name=name, compiler_params=_params(("parallel",)),
    )(ga, gs, attn_out, glu)


def _mix_bwd(ga, gs, attn_out, glu, dmixed, name):
    l, d = ga.shape
    tl = _tile(l, 256, 16)

    def body(ga_ref, gs_ref, a_ref, u_ref, dm_ref, dga_ref, dgs_ref, da_ref, dglu_ref):
        dm = dm_ref[...].astype(F32)
        sa = _sigmoid(ga_ref[...].astype(F32))
        ss = _sigmoid(gs_ref[...].astype(F32))
        sb = _sigmoid(u_ref[:, d:].astype(F32))
        ua = u_ref[:, :d].astype(F32)
        dssm = dm * ss
        dga_ref[...] = (dm * a_ref[...].astype(F32) * sa * (1.0 - sa)).astype(BF16)
        da_ref[...] = (dm * sa).astype(BF16)
        dgs_ref[...] = (dm * (ua * sb) * ss * (1.0 - ss)).astype(BF16)
        dglu_ref[:, :d] = (dssm * sb).astype(BF16)
        dglu_ref[:, d:] = (dssm * ua * sb * (1.0 - sb)).astype(BF16)

    out = jax.ShapeDtypeStruct((l, d), BF16)
    return pl.pallas_call(
        body, out_shape=(out, out, out, jax.ShapeDtypeStruct((l, 2 * d), BF16)), grid=(l // tl,),
        in_specs=[_row_spec(tl, d), _row_spec(tl, d), _row_spec(tl, d), _row_spec(tl, 2 * d), _row_spec(tl, d)],
        out_specs=(_row_spec(tl, d), _row_spec(tl, d), _row_spec(tl, d), _row_spec(tl, 2 * d)), name=name,
        compiler_params=_params(("parallel",)),
    )(ga, gs, attn_out, glu, dmixed)


CONV_BLOCKS = 4
HALO = 16


def _shift_rows(v, k, head):
    row = lax.broadcasted_iota(jnp.int32, v.shape, 0)
    out = pltpu.roll(v, k, 0)
    for r in range(k):
        out = jnp.where(row == r, head[HALO - k + r:HALO - k + r + 1, :], out)
    return out


def _shift_rows_up(v, k, tail):
    n = v.shape[0]
    row = lax.broadcasted_iota(jnp.int32, v.shape, 0)
    out = pltpu.roll(v, n - k, 0)
    for r in range(k):
        out = jnp.where(row == n - k + r, tail[r:r + 1, :], out)
    return out


def _conv_gate(g, head, w_ref, b_ref):
    return w_ref[0:1, :] * _shift_rows(g, 2, head) + w_ref[1:2, :] * _shift_rows(g, 1, head) + w_ref[2:3, :] * g + b_ref[...]


def _conv_act(up, conv_w, conv_b, ff, cw, name):
    l = up.shape[0]
    tl = _tile(l, 256, HALO)
    nj = ff // cw
    hb = tl // HALO

    def body(g_ref, gp_ref, v_ref, w_ref, b_ref, o_ref):
        i = pl.program_id(0)
        head = gp_ref[...].astype(F32) * jnp.where(i > 0, 1.0, 0.0)
        gc = _conv_gate(g_ref[...].astype(F32), head, w_ref, b_ref)
        o_ref[...] = (gc * _sigmoid(gc) * v_ref[...].astype(F32)).astype(BF16)

    return pl.pallas_call(
        body, out_shape=jax.ShapeDtypeStruct((l, ff), BF16), grid=(l // tl, nj),
        in_specs=[pl.BlockSpec((tl, cw), lambda i, j: (i, 2 * j)),
                  pl.BlockSpec((HALO, cw), lambda i, j: (jnp.maximum(i * hb - 1, 0), 2 * j)),
                  pl.BlockSpec((tl, cw), lambda i, j: (i, 2 * j + 1)),
                  pl.BlockSpec((3, cw), lambda i, j: (0, j)), pl.BlockSpec((1, cw), lambda i, j: (0, j))],
        out_specs=pl.BlockSpec((tl, cw), lambda i, j: (i, j)), name=name,
        compiler_params=_params(("parallel", "parallel")),
    )(up, up, up, conv_w, conv_b)


def _conv_act_bwd(up, da, conv_w, conv_b, ff, cw, name):
    l = up.shape[0]
    tl = _tile(l, 256, HALO)
    nj = ff // cw
    hb = tl // HALO
    ni = l // tl

    def body(g_ref, gp_ref, gn_ref, v_ref, vn_ref, da_ref, dan_ref, w_ref, b_ref, dup_ref, dw_ref, db_ref):
        i = pl.program_id(0)
        g = g_ref[...].astype(F32)
        head = gp_ref[...].astype(F32) * jnp.where(i > 0, 1.0, 0.0)
        g1 = _shift_rows(g, 1, head)
        g2 = _shift_rows(g, 2, head)
        gc = w_ref[0:1, :] * g2 + w_ref[1:2, :] * g1 + w_ref[2:3, :] * g + b_ref[...]
        sg = _sigmoid(gc)
        dav = da_ref[...].astype(F32)
        dgc = dav * v_ref[...].astype(F32) * (sg * (1.0 + gc * (1.0 - sg)))
        gn = gn_ref[...].astype(F32)
        gcn = _conv_gate(gn, g[tl - HALO:, :], w_ref, b_ref)
        sgn = _sigmoid(gcn)
        dgcn = dan_ref[...].astype(F32) * vn_ref[...].astype(F32) * (sgn * (1.0 + gcn * (1.0 - sgn)))
        dgcn = dgcn * jnp.where(i < ni - 1, 1.0, 0.0)
        dgate = w_ref[2:3, :] * dgc + w_ref[1:2, :] * _shift_rows_up(dgc, 1, dgcn) + w_ref[0:1, :] * _shift_rows_up(dgc, 2, dgcn)
        dup_ref[:, :cw] = dgate.astype(BF16)
        dup_ref[:, cw:] = (dav * (gc * sg)).astype(BF16)
        zero = jnp.zeros((1, cw), F32)
        dw_ref[...] = jnp.concatenate(
            [jnp.sum(dgc * g2, axis=0, keepdims=True), jnp.sum(dgc * g1, axis=0, keepdims=True),
             jnp.sum(dgc * g, axis=0, keepdims=True)] + [zero] * 5, axis=0)
        db_ref[...] = jnp.concatenate([jnp.sum(dgc, axis=0, keepdims=True)] + [zero] * 7, axis=0)

    def cur(col):
        return pl.BlockSpec((tl, cw), lambda i, j, col=col: (i, 2 * j + col))

    def prev(col):
        return pl.BlockSpec((HALO, cw), lambda i, j, col=col: (jnp.maximum(i * hb - 1, 0), 2 * j + col))

    def nxt(col):
        return pl.BlockSpec((HALO, cw), lambda i, j, col=col: (jnp.minimum((i + 1) * hb, l // HALO - 1), 2 * j + col))

    part = jax.ShapeDtypeStruct((ni * 8, ff), F32)
    part_spec = pl.BlockSpec((8, cw), lambda i, j: (i, j))
    return pl.pallas_call(
        body, out_shape=(jax.ShapeDtypeStruct((l, 2 * ff), BF16), part, part), grid=(ni, nj),
        in_specs=[cur(0), prev(0), nxt(0), cur(1), nxt(1), pl.BlockSpec((tl, cw), lambda i, j: (i, j)),
                  pl.BlockSpec((HALO, cw), lambda i, j: (jnp.minimum((i + 1) * hb, l // HALO - 1), j)),
                  pl.BlockSpec((3, cw), lambda i, j: (0, j)), pl.BlockSpec((1, cw), lambda i, j: (0, j))],
        out_specs=(pl.BlockSpec((tl, 2 * cw), lambda i, j: (i, j)), part_spec, part_spec), name=name,
        compiler_params=_params(("parallel", "parallel")),
    )(up, up, up, up, up, da, da, conv_w, conv_b)


def _sum_rows8(parts, name):
    n8, w = parts.shape
    n = n8 // 8
    cw = _tile(w, 2048)

    def body(p_ref, o_ref):
        acc = p_ref[0:8, :]
        for k in range(1, n):
            acc = acc + p_ref[8 * k:8 * k + 8, :]
        o_ref[...] = acc

    return pl.pallas_call(body, out_shape=jax.ShapeDtypeStruct((8, w), F32), grid=(w // cw,),
                          in_specs=[pl.BlockSpec((n8, cw), lambda j: (0, j))], out_specs=pl.BlockSpec((8, cw), lambda j: (0, j)),
                          name=name, compiler_params=_params(("parallel",)))(parts)


def _ada_fwd(c_all, w_shard, b_shard, name):
    nb, d = c_all.shape
    n = w_shard.shape[1]
    tn = _tile(n, 512)

    def body(c_ref, w_ref, b_ref, o_ref):
        cv = c_ref[...]
        cond = (cv * _sigmoid(cv)).astype(BF16)
        o_ref[...] = jnp.dot(cond, w_ref[...].astype(BF16), preferred_element_type=F32) + b_ref[...]

    return pl.pallas_call(
        body, out_shape=jax.ShapeDtypeStruct((nb, n), F32), grid=(n // tn,),
        in_specs=[pl.BlockSpec((nb, d), lambda j: (0, 0)), pl.BlockSpec((d, tn), lambda j: (0, j)),
                  pl.BlockSpec((1, tn), lambda j: (0, j))],
        out_specs=pl.BlockSpec((nb, tn), lambda j: (0, j)), name=name, compiler_params=_params(("parallel",)),
    )(c_all, w_shard, b_shard)


def _adam_update(w, g, m, v):
    m2 = ADAM_B1 * m + (1.0 - ADAM_B1) * g
    v2 = ADAM_B2 * v + (1.0 - ADAM_B2) * (g * g)
    m_hat = m2 / (1.0 - ADAM_B1 ** ADAM_STEP)
    v_hat = v2 / (1.0 - ADAM_B2 ** ADAM_STEP)
    return -ADAM_LR * (m_hat / (jnp.sqrt(v_hat) + ADAM_EPS) + ADAM_WD * w), m2, v2


def _ada_bwd_adam(c_all_t, dmod_shard, w, m, v, name):
    d, nb = c_all_t.shape
    n = w.shape[1]
    tr, tn = _tile(d, 512, 8), _tile(n, 512)

    def body(c_ref, dm_ref, w_ref, m_ref, v_ref, g_ref, dl_ref, m2_ref, v2_ref):
        cv = c_ref[...]
        cond = cv * _sigmoid(cv)
        g = cond[:, 0:1] * dm_ref[0:1, :]
        for b in range(1, nb):
            g = g + cond[:, b:b + 1] * dm_ref[b:b + 1, :]
        g_ref[...] = g
        dl_ref[...], m2_ref[...], v2_ref[...] = _adam_update(w_ref[...], g, m_ref[...], v_ref[...])

    blk = pl.BlockSpec((tr, tn), lambda i, j: (i, j))
    out = jax.ShapeDtypeStruct((d, n), F32)
    return pl.pallas_call(
        body, out_shape=(out, out, out, out), grid=(d // tr, n // tn),
        in_specs=[pl.BlockSpec((tr, nb), lambda i, j: (i, 0)), pl.BlockSpec((nb, tn), lambda i, j: (0, j)), blk, blk, blk],
        out_specs=(blk, blk, blk, blk), name=name, compiler_params=_params(("parallel", "parallel")),
    )(c_all_t, dmod_shard, w, m, v)


def _adam(w, g, m, v, name):
    r, c = w.shape
    tr = _tile(r, 256, 8)

    def body(w_ref, g_ref, m_ref, v_ref, dl_ref, m2_ref, v2_ref):
        dl_ref[...], m2_ref[...], v2_ref[...] = _adam_update(w_ref[...], g_ref[...], m_ref[...], v_ref[...])

    blk = pl.BlockSpec((tr, c), lambda i: (i, 0))
    out = jax.ShapeDtypeStruct((r, c), F32)
    return pl.pallas_call(body, out_shape=(out, out, out), grid=(r // tr,), in_specs=[blk] * 4, out_specs=(blk,) * 3,
                          name=name, compiler_params=_params(("parallel",)))(w, g, m, v)


def _sum_devices(gathered, name):
    nd, r, c = gathered.shape
    tr = _tile(r, 64, 16)

    def body(g_ref, o_ref):
        acc = g_ref[0].astype(F32)
        for k in range(1, nd):
            acc = acc + g_ref[k].astype(F32)
        o_ref[...] = acc

    return pl.pallas_call(body, out_shape=jax.ShapeDtypeStruct((r, c), F32), grid=(r // tr,),
                          in_specs=[pl.BlockSpec((nd, tr, c), lambda i: (0, i, 0))], out_specs=pl.BlockSpec((tr, c), lambda i: (i, 0)),
                          name=name, compiler_params=_params(("parallel",)))(gathered)


def _place():
    x, y, c = lax.axis_index("x"), lax.axis_index("y"), lax.axis_index("c")
    chips = [(1 - x, y), (x, 1 - y), (1 - x, 1 - y)]
    return x, y, c, chips


def _all_gather8(block, name):
    m_per, n = block.shape

    def body(x_ref, out_ref, send_sems, recv_sems, local_sem):
        x, y, c, chips = _place()
        me, sibling = (x, y, c), (x, y, 1 - c)

        def rows(px, py, pc):
            return out_ref.at[pl.ds((4 * px + 2 * py + pc) * m_per, m_per), :]

        def copy(k, blk, to, src=None):
            return pltpu.make_async_remote_copy(
                src_ref=rows(*blk) if src is None else src, dst_ref=rows(*blk), send_sem=send_sems.at[k],
                recv_sem=recv_sems.at[k], device_id=to, device_id_type=MESH)

        mine = pltpu.make_async_copy(x_ref, rows(*me), local_sem)
        mine.start()
        first = [copy(0, me, sibling, src=x_ref)]
        first += [copy(1 + j, me, (*chip, c), src=x_ref) for j, chip in enumerate(chips)]
        for cp in first:
            cp.start()
        passed = [copy(4 + j, (*chip, c), sibling) for j, chip in enumerate(chips)]
        for j, chip in enumerate(chips):
            copy(1 + j, (*chip, c), me).wait_recv()
            passed[j].start()
        copy(0, sibling, me).wait_recv()
        for j, chip in enumerate(chips):
            copy(4 + j, (*chip, 1 - c), me).wait_recv()
        for cp in first + passed:
            cp.wait_send()
        mine.wait()

    return pl.pallas_call(
        body,
        out_shape=jax.ShapeDtypeStruct((N_DEV * m_per, n), block.dtype),
        in_specs=[pl.BlockSpec(memory_space=pltpu.VMEM)],
        out_specs=pl.BlockSpec(memory_space=pltpu.VMEM),
        scratch_shapes=[pltpu.SemaphoreType.DMA((7,)), pltpu.SemaphoreType.DMA((7,)), pltpu.SemaphoreType.DMA],
        name=name,
        compiler_params=pltpu.CompilerParams(vmem_limit_bytes=VMEM_LIMIT_BYTES),
    )(block)


ANY = pl.BlockSpec(memory_space=pl.ANY)


def _place_shard(shard, name, after=()):
    r, k = shard.shape
    tb = _tile(r, 512, 16)
    nb = r // tb
    chip = (2 * lax.axis_index("x") + lax.axis_index("y")).astype(jnp.int32).reshape(1)

    def body(j_ref, s_ref, *rest):
        rest[-1][...] = s_ref[...].astype(BF16)

    return pl.pallas_call(
        body, out_shape=jax.ShapeDtypeStruct((N_CHIPS * r, k), BF16),
        grid_spec=pltpu.PrefetchScalarGridSpec(
            num_scalar_prefetch=1, grid=(nb,),
            in_specs=[pl.BlockSpec((tb, k), lambda i, j_ref: (i, 0))] + [ANY] * len(after),
            out_specs=pl.BlockSpec((tb, k), lambda i, j_ref: (j_ref[0] * nb + i, 0))),
        name=name, compiler_params=_params(("parallel",)),
    )(chip, shard, *after)


HBM_SPEC = pl.BlockSpec(memory_space=pltpu.HBM)
SEM_SPEC = pl.BlockSpec(memory_space=pltpu.SEMAPHORE)
TOKEN_SPEC = pl.BlockSpec(memory_space=pltpu.VMEM)
SPLIT_COPY = pltpu.CompilerParams(has_side_effects=pltpu.SideEffectType.DATAFLOW_SIDE_EFFECTING)


def _in_hbm(arrays):
    return [pltpu.with_memory_space_constraint(a, pltpu.HBM) for a in arrays]


def _hbm_like(arrays):
    return tuple(pltpu.HBM(a.shape, a.dtype) for a in arrays)


def _token_shape():
    return jax.ShapeDtypeStruct((8, LANES), F32)


def _gathered_rows(buf, px, py, half):
    r = buf.shape[0] // N_CHIPS
    return buf.at[pl.ds(pl.multiple_of((2 * px + py) * r + half * (r // 2), 16), r // 2), :]


def _gather_start(groups, name):
    sizes = [len(g) for g in groups]
    flat = [b for g in groups for b in g]
    nb, ng = len(flat), len(groups)

    def body(*refs):
        bufs = refs[:nb]
        sems = refs[nb:nb + 2 * ng]
        token = refs[-1]
        x, y, c, chips = _place()
        pos = 0
        for gi, nw in enumerate(sizes):
            for k, chip in enumerate(chips):
                for w in range(nw):
                    mine = _gathered_rows(bufs[pos + w], x, y, c)
                    pltpu.make_async_remote_copy(src_ref=mine, dst_ref=mine, send_sem=sems[2 * gi].at[k * nw + w], recv_sem=sems[2 * gi + 1].at[k * nw + w],
                                                 device_id=(*chip, c), device_id_type=MESH).start()
            pos += nw
        token[...] = jnp.zeros_like(token)

    sem_shapes = tuple(pltpu.SemaphoreType.DMA((3 * n,)) for n in sizes for _ in range(2))
    outs = pl.pallas_call(
        body, name=name, out_shape=sem_shapes + _hbm_like(flat) + (_token_shape(),),
        in_specs=[HBM_SPEC] * nb, out_specs=(SEM_SPEC,) * (2 * ng) + (HBM_SPEC,) * nb + (TOKEN_SPEC,),
        input_output_aliases={i: 2 * ng + i for i in range(nb)}, compiler_params=SPLIT_COPY,
    )(*_in_hbm(flat))
    res, pos = [], 2 * ng
    for gi, n in enumerate(sizes):
        res.append((outs[2 * gi], outs[2 * gi + 1], list(outs[pos:pos + n])))
        pos += n
    return res, outs[-1]


def _gather_forward(bufs, ici_send, ici_recv, after, name):
    nw, na = len(bufs), len(after)

    def body(*refs):
        b = refs[:nw]
        isend, irecv = refs[nw], refs[nw + 1]
        dsend, drecv = refs[nw + 2 + na], refs[nw + 3 + na]
        x, y, c, chips = _place()
        for k, chip in enumerate(chips):
            for w in range(nw):
                landed = _gathered_rows(b[w], *chip, c)
                pltpu.make_async_remote_copy(src_ref=landed, dst_ref=landed, send_sem=isend.at[k * nw + w], recv_sem=irecv.at[k * nw + w],
                                             device_id=(*chip, c), device_id_type=MESH).wait_recv()
                pltpu.make_async_remote_copy(src_ref=landed, dst_ref=landed, send_sem=dsend.at[k * nw + w], recv_sem=drecv.at[k * nw + w],
                                             device_id=(x, y, 1 - c), device_id_type=MESH).start()
        for k, chip in enumerate(chips):
            for w in range(nw):
                mine = _gathered_rows(b[w], x, y, c)
                pltpu.make_async_remote_copy(src_ref=mine, dst_ref=mine, send_sem=isend.at[k * nw + w], recv_sem=irecv.at[k * nw + w],
                                             device_id=(*chip, c), device_id_type=MESH).wait_send()
        refs[-1][...] = jnp.zeros_like(refs[-1])

    sem = pltpu.SemaphoreType.DMA((3 * nw,))
    outs = pl.pallas_call(
        body, name=name, out_shape=(sem, sem) + _hbm_like(bufs) + (_token_shape(),),
        in_specs=[HBM_SPEC] * nw + [SEM_SPEC, SEM_SPEC] + [ANY] * na, out_specs=(SEM_SPEC, SEM_SPEC) + (HBM_SPEC,) * nw + (TOKEN_SPEC,),
        input_output_aliases={i: 2 + i for i in range(nw)}, compiler_params=SPLIT_COPY,
    )(*bufs, ici_send, ici_recv, *after)
    return outs[0], outs[1], list(outs[2:2 + nw]), outs[-1]


def _gather_finish(bufs, d2d_send, d2d_recv, name, after=()):
    nw = len(bufs)

    def body(*refs):
        b = refs[:nw]
        dsend, drecv = refs[nw], refs[nw + 1]
        x, y, c, chips = _place()
        for k, chip in enumerate(chips):
            for w in range(nw):
                theirs = _gathered_rows(b[w], *chip, 1 - c)
                pltpu.make_async_remote_copy(src_ref=theirs, dst_ref=theirs, send_sem=dsend.at[k * nw + w], recv_sem=drecv.at[k * nw + w],
                                             device_id=(x, y, 1 - c), device_id_type=MESH).wait_recv()
                passed = _gathered_rows(b[w], *chip, c)
                pltpu.make_async_remote_copy(src_ref=passed, dst_ref=passed, send_sem=dsend.at[k * nw + w], recv_sem=drecv.at[k * nw + w],
                                             device_id=(x, y, 1 - c), device_id_type=MESH).wait_send()

    outs = pl.pallas_call(
        body, name=name, out_shape=_hbm_like(bufs), in_specs=[HBM_SPEC] * nw + [SEM_SPEC, SEM_SPEC] + [ANY] * len(after),
        out_specs=(HBM_SPEC,) * nw, input_output_aliases={i: i for i in range(nw)}, compiler_params=SPLIT_COPY,
    )(*bufs, d2d_send, d2d_recv, *after)
    return list(outs)


def _swap_copies(src, land, send_sems, recv_sems):
    x, y, c, _ = _place()
    copies = []
    for w in range(len(src)):
        r = src[w].shape[0] // N_CHIPS
        h = r // 2
        for j in range(N_CHIPS):
            copies.append(pltpu.make_async_remote_copy(
                src_ref=src[w].at[pl.ds(pl.multiple_of(j * r + (1 - c) * h, 16), h), :], dst_ref=land[w].at[pl.ds(j * h, h), :],
                send_sem=send_sems.at[w * N_CHIPS + j], recv_sem=recv_sems.at[w * N_CHIPS + j], device_id=(x, y, 1 - c), device_id_type=MESH))
    return copies


def _swap_start(grads, name):
    nw = len(grads)
    landing = [lax.empty((g.shape[0] // 2, g.shape[1]), g.dtype) for g in grads]

    def body(*refs):
        for cp in _swap_copies(refs[:nw], refs[nw:2 * nw], refs[2 * nw], refs[2 * nw + 1]):
            cp.start()
        refs[-1][...] = jnp.zeros_like(refs[-1])

    sem = pltpu.SemaphoreType.DMA((N_CHIPS * nw,))
    outs = pl.pallas_call(
        body, name=name, out_shape=(sem, sem) + _hbm_like(grads) + _hbm_like(landing) + (_token_shape(),),
        in_specs=[HBM_SPEC] * (2 * nw), out_specs=(SEM_SPEC, SEM_SPEC) + (HBM_SPEC,) * (2 * nw) + (TOKEN_SPEC,),
        input_output_aliases={i: 2 + i for i in range(2 * nw)}, compiler_params=SPLIT_COPY,
    )(*_in_hbm(grads), *_in_hbm(landing))
    return (outs[0], outs[1], list(outs[2:2 + nw]), list(outs[2 + nw:2 + 2 * nw])), outs[-1]


def _swap_wait(started, after, name):
    send_sems, recv_sems, grads, landing = started
    nw = len(grads)

    def body(*refs):
        for cp in _swap_copies(refs[:nw], refs[nw:2 * nw], refs[2 * nw], refs[2 * nw + 1]):
            cp.wait_send()
            cp.wait_recv()

    outs = pl.pallas_call(
        body, name=name, out_shape=_hbm_like(grads) + _hbm_like(landing),
        in_specs=[HBM_SPEC] * (2 * nw) + [SEM_SPEC, SEM_SPEC] + [ANY] * len(after), out_specs=(HBM_SPEC,) * (2 * nw),
        input_output_aliases={i: i for i in range(2 * nw)}, compiler_params=SPLIT_COPY,
    )(*grads, *landing, send_sems, recv_sems, *after)
    return list(outs[:nw]), list(outs[nw:])


def _scatter_start(partials, name):
    nw = len(partials)
    landing = [lax.empty((3,) + p.shape[1:], p.dtype) for p in partials]

    def body(*refs):
        src, land = refs[:nw], refs[nw:2 * nw]
        send_sems, recv_sems = refs[2 * nw], refs[2 * nw + 1]
        token = refs[-1]
        x, y, c, chips = _place()
        for k, chip in enumerate(chips):
            for w in range(nw):
                pltpu.make_async_remote_copy(src_ref=src[w].at[2 * chip[0] + chip[1]], dst_ref=land[w].at[k], send_sem=send_sems.at[k * nw + w],
                                             recv_sem=recv_sems.at[k * nw + w], device_id=(*chip, c), device_id_type=MESH).start()
        token[...] = jnp.zeros_like(token)

    sem = pltpu.SemaphoreType.DMA((3 * nw,))
    outs = pl.pallas_call(
        body, name=name, out_shape=(sem, sem) + _hbm_like(partials) + _hbm_like(landing) + (_token_shape(),),
        in_specs=[HBM_SPEC] * (2 * nw), out_specs=(SEM_SPEC, SEM_SPEC) + (HBM_SPEC,) * (2 * nw) + (TOKEN_SPEC,),
        input_output_aliases={i: 2 + i for i in range(2 * nw)}, compiler_params=SPLIT_COPY,
    )(*_in_hbm(partials), *_in_hbm(landing))
    return (outs[0], outs[1], list(outs[2:2 + nw]), list(outs[2 + nw:2 + 2 * nw])), outs[-1]


def _scatter_wait(started, after, name):
    send_sems, recv_sems, partials, landing = started
    nw = len(partials)

    def body(*refs):
        src, land = refs[:nw], refs[nw:2 * nw]
        ssem, rsem = refs[2 * nw], refs[2 * nw + 1]
        x, y, c, chips = _place()
        for k, chip in enumerate(chips):
            for w in range(nw):
                cp = pltpu.make_async_remote_copy(src_ref=src[w].at[2 * chip[0] + chip[1]], dst_ref=land[w].at[k], send_sem=ssem.at[k * nw + w],
                                                  recv_sem=rsem.at[k * nw + w], device_id=(*chip, c), device_id_type=MESH)
                cp.wait_send()
                cp.wait_recv()

    outs = pl.pallas_call(
        body, name=name, out_shape=_hbm_like(partials) + _hbm_like(landing),
        in_specs=[HBM_SPEC] * (2 * nw) + [SEM_SPEC, SEM_SPEC] + [ANY] * len(after), out_specs=(HBM_SPEC,) * (2 * nw),
        input_output_aliases={i: i for i in range(2 * nw)}, compiler_params=SPLIT_COPY,
    )(*partials, *landing, send_sems, recv_sems, *after)
    return list(outs[:nw]), list(outs[nw:])


def _add_halves(grad, other, name):
    k = grad.shape[1]
    h = other.shape[0] // N_CHIPS
    tb = _tile(h, 512, 16)
    g4 = grad.reshape(N_CHIPS, 2, h, k)
    o3 = other.reshape(N_CHIPS, h, k)
    core = lax.axis_index("c").astype(jnp.int32).reshape(1)

    def body(c_ref, g_ref, o_ref, p_ref):
        p_ref[...] = (g_ref[...].astype(F32) + o_ref[...].astype(F32)).astype(BF16)

    return pl.pallas_call(
        body, out_shape=jax.ShapeDtypeStruct((N_CHIPS, h, k), BF16),
        grid_spec=pltpu.PrefetchScalarGridSpec(
            num_scalar_prefetch=1, grid=(N_CHIPS, h // tb),
            in_specs=[pl.BlockSpec((None, None, tb, k), lambda j, i, c_ref: (j, c_ref[0], i, 0)),
                      pl.BlockSpec((None, tb, k), lambda j, i, c_ref: (j, i, 0))],
            out_specs=pl.BlockSpec((None, tb, k), lambda j, i, c_ref: (j, i, 0))),
        name=name, compiler_params=_params(("parallel", "parallel")),
    )(core, g4, o3)


def _add_partials(partial, others, name):
    _, h, k = partial.shape
    tb = _tile(h, 512, 16)
    nb = h // tb
    place = jnp.stack([2 * lax.axis_index("x") + lax.axis_index("y"), lax.axis_index("c")]).astype(jnp.int32)

    def body(s_ref, p_ref, o0_ref, o1_ref, o2_ref, f_ref):
        f_ref[...] = ((p_ref[...].astype(F32) + o0_ref[...].astype(F32)) + o1_ref[...].astype(F32)) + o2_ref[...].astype(F32)

    def other(s):
        return pl.BlockSpec((None, tb, k), lambda i, s_ref, s=s: (s, i, 0))

    return pl.pallas_call(
        body, out_shape=jax.ShapeDtypeStruct((2 * h, k), F32),
        grid_spec=pltpu.PrefetchScalarGridSpec(
            num_scalar_prefetch=1, grid=(nb,),
            in_specs=[pl.BlockSpec((None, tb, k), lambda i, s_ref: (s_ref[0], i, 0)), other(0), other(1), other(2)],
            out_specs=pl.BlockSpec((tb, k), lambda i, s_ref: (s_ref[1] * nb + i, 0))),
        name=name, compiler_params=_params(("parallel",)),
    )(place, partial, others, others, others)


def _share_halves(fulls, name):
    nw = len(fulls)

    def body(*refs):
        ins, outs = refs[:nw], refs[nw:2 * nw]
        send_sems, recv_sems = refs[2 * nw:]
        x, y, c, _ = _place()
        copies = []
        for w in range(nw):
            h = fulls[w].shape[0] // 2
            start = pl.multiple_of(c * h, 8)
            copies.append(pltpu.make_async_remote_copy(
                src_ref=ins[w].at[pl.ds(start, h), :], dst_ref=outs[w].at[pl.ds(start, h), :], send_sem=send_sems.at[w],
                recv_sem=recv_sems.at[w], device_id=(x, y, 1 - c), device_id_type=MESH))
            copies[-1].start()
        for cp in copies:
            cp.wait()

    sem = pltpu.SemaphoreType.DMA((nw,))
    return pl.pallas_call(
        body, out_shape=tuple(jax.ShapeDtypeStruct(f.shape, f.dtype) for f in fulls),
        in_specs=[ANY] * nw, out_specs=(ANY,) * nw, scratch_shapes=[sem, sem], name=name,
        input_output_aliases={w: w for w in range(nw)},
    )(*fulls)


def _forward_then_finish(started_group, after, tag):
    ici_send, ici_recv, bufs = started_group
    d2d_send, d2d_recv, bufs, _ = _gather_forward(bufs, ici_send, ici_recv, after, f"gather_forward_{tag}")
    return _gather_finish(bufs, d2d_send, d2d_recv, f"gather_finish_{tag}")


def _reduce_start(swapping, tag, after=()):
    grads, from_sibling = _swap_wait(swapping, after, f"swap_wait_{tag}")
    chip_sums = [_add_halves(g, o, f"add_halves_{tag}_{i}") for i, (g, o) in enumerate(zip(grads, from_sibling))]
    return _scatter_start(chip_sums, f"scatter_start_{tag}")


def _reduce_finish(started, after, tag):
    chip_sums, from_chips = _scatter_wait(started, after, f"scatter_wait_{tag}")
    fulls = [_add_partials(p, o, f"add_partials_{tag}_{i}") for i, (p, o) in enumerate(zip(chip_sums, from_chips))]
    return _share_halves(fulls, f"share_halves_{tag}")


def _flatten_pad(parts, cols=SMALL_COLS):
    flat = jnp.concatenate([p.reshape(-1) for p in parts])
    rows = -(-flat.shape[0] // (16 * cols)) * 16
    return jnp.pad(flat, (0, rows * cols - flat.shape[0])).reshape(rows, cols)


def _split_flat(buf, shapes):
    flat = buf.reshape(-1)
    out, off = [], 0
    for s in shapes:
        n = math.prod(s)
        out.append(flat[off:off + n].reshape(s))
        off += n
    return out


def _ssm_setup(seq_len, ssm_a_re, ssm_a_im, ssm_log_dt, ssm_b_re, ssm_b_im, ssm_c_re, ssm_c_im):
    lam_r, lam_i, bbar_r, bbar_i = _ssm_discretize(ssm_a_re, ssm_a_im, ssm_log_dt, ssm_b_re, ssm_b_im)
    tab_f, tab_b = _scan_tables(lam_r, lam_i, seq_len // N_SEG)
    pk = {"br": _pack_in(bbar_r), "bi": _pack_in(bbar_i), "cr": _pack_out(ssm_c_re), "ci": _pack_out(ssm_c_im)}
    packs = {k: v.astype(BF16) for k, v in pk.items()}
    packs.update({"brt": jnp.swapaxes(packs["br"], 1, 2), "bit": jnp.swapaxes(packs["bi"], 1, 2),
                  "crt": jnp.swapaxes(packs["cr"], 1, 2), "cit": jnp.swapaxes(packs["ci"], 1, 2)})
    return packs, tab_f, tab_b


def _local_step(xs, target, mod, w_in_t, comm, norm_mix_g, attn_sinks, ssm, ssm_d, norm_ffn_g, conv_w_full, ffn_conv_b, final_g,
                aw, sw, ff):
    l, d = xs.shape
    u_off = aw + 2 * KV_WIDTH
    ga_off = u_off + sw
    gs_off = ga_off + d
    packs, tab_f, tab_b = ssm
    dvec = ssm_d.reshape(1, sw)

    h1 = _norm_mod(xs, norm_mix_g, mod, 1, 0, "norm_mod1")
    proj = _matmul(h1, w_in_t, "nt", "mm_in")
    attn = _attn_fwd(proj, attn_sinks, aw, "attn_fwd", after=(comm["mixer_arrived"]((proj,)),))
    u_il = _interleave(proj[:, u_off:u_off + sw])
    ys_il, gy_il = _s5_fwd(u_il, 0, packs, dvec, tab_f, sw, "s5_fwd")
    gy = _deinterleave(gy_il)
    (w_ap_t, w_glu_t, w_out_f), ffn_weights = comm["later_weights"]((gy, attn))
    attn_out = _matmul(attn, w_ap_t, "nt", "mm_attn_proj")
    glu = _matmul(gy, w_glu_t, "nt", "mm_glu")
    g_attn, g_ssm = proj[:, ga_off:ga_off + d], proj[:, gs_off:gs_off + d]
    mixed = _mix(g_attn, g_ssm, attn_out, glu, "mix")
    mo = _matmul(mixed, w_out_f, "nn", "mm_out", out_dtype=F32)
    x2, h2 = _resid_norm_mod(xs, mo, norm_ffn_g, mod, 2, 4, 3, "resid_norm_mod2")
    w_up_t, w_down_f = ffn_weights((h2,))
    cw = ff // CONV_BLOCKS
    up = _matmul(h2, w_up_t, "nt", "mm_up", interleave=cw)
    act = _conv_act(up, conv_w_full, ffn_conv_b, ff, cw, "conv_act")
    fo = _matmul(act, w_down_f, "nn", "mm_down", out_dtype=F32)
    loss_part, d_final_g, d_gate2, dx3, dfo = _final_loss(x2, fo, mod, 5, final_g.reshape(1, d), target, "final_loss")

    dact = _matmul(dfo, w_down_f, "nt", "mm_down_dx")
    g_down = _matmul(act, dfo, "tn", "mm_down_dw")
    dup, dcw_parts, dcb_parts = _conv_act_bwd(up, dact, conv_w_full, ffn_conv_b, ff, cw, "conv_act_bwd")
    d_conv_w = _sum_rows8(dcw_parts, "sum_conv_w")[:3]
    d_conv_b = _sum_rows8(dcb_parts, "sum_conv_b")[:1]
    g_up = _matmul(dup, h2, "tn", "mm_up_dw", interleave=cw)
    dh2 = _matmul(dup, w_up_t, "nn", "mm_up_dx", interleave=cw, after=(comm["grads_started"]("ffn", [g_up, g_down]),))
    mod = comm["ffn_grads_ready"](mod, (dh2,))
    dx2, d_shift2, d_scale2, d_gain2, dmo, d_gate1 = _norm_mod_bwd(dh2, x2, dx3, norm_ffn_g, mod, 4, "norm_mod2_bwd", branch=mo, gate_col=2)
    dmixed = _matmul(dmo, w_out_f, "nt", "mm_out_dx")
    g_out = _matmul(mixed, dmo, "tn", "mm_out_dw")
    dga, dgs, dattn_out, dglu = _mix_bwd(g_attn, g_ssm, attn_out, glu, dmixed, "mix_bwd")
    dgy = _matmul(dglu, w_glu_t, "nn", "mm_glu_dx")
    g_glu = _matmul(dglu, gy, "tn", "mm_glu_dw")
    du_il, dlam, dbr_p, dbi_p, dcr_p, dci_p, d_dvec = _s5_bwd(u_il, 0, ys_il, _interleave(dgy), packs, dvec, tab_f, tab_b, sw, "s5_bwd")
    du = _deinterleave(du_il)
    dattn = _matmul(dattn_out, w_ap_t, "nn", "mm_attn_proj_dx")
    g_ap = _matmul(dattn_out, attn, "tn", "mm_attn_proj_dw")
    dq, dkv_cur, dkv_prev, d_sinks = _attn_bwd(proj, attn_sinks, dattn, aw, "attn_bwd")
    dkv = dkv_cur + jnp.concatenate([dkv_prev[ATTN_BLOCK:], jnp.zeros((ATTN_BLOCK, 2 * KV_WIDTH), F32)], axis=0)
    dproj = jnp.concatenate([dq, dkv.astype(BF16), du, dga, dgs], axis=1)
    g_in = _matmul(dproj, h1, "tn", "mm_in_dw")
    dh1 = _matmul(dproj, w_in_t, "nn", "mm_in_dx", after=(comm["grads_started"]("rest", [g_in, g_ap, g_glu, g_out]),))
    grad_x, d_shift1, d_scale1, d_gain1 = _norm_mod_bwd(dh1, xs, dx2, norm_mix_g, mod, 1, "norm_mod1_bwd")

    dmod = jnp.concatenate([d_shift1, d_scale1, d_gate1, d_shift2, d_scale2, d_gate2], axis=1)
    small_parts = [dmod, d_gain1, d_sinks, dlam[0], dlam[1], _unpack_diag(dbr_p, SSM_STATE, SSM_GROUP),
                   _unpack_diag(dbi_p, SSM_STATE, SSM_GROUP), _unpack_diag(dcr_p, SSM_STATE, SSM_GROUP),
                   _unpack_diag(dci_p, SSM_STATE, SSM_GROUP), d_dvec, d_gain2, d_conv_b, d_conv_w, d_final_g]
    return loss_part, grad_x, small_parts


def _kernel_impl(x, c, ada_w, ada_b, norm_mix_g, w_in, attn_sinks, w_attn_proj, ssm_a_re, ssm_a_im, ssm_log_dt, ssm_b_re, ssm_b_im,
                 ssm_c_re, ssm_c_im, ssm_d, w_ssm_glu, w_out, norm_ffn_g, w_ffn_up, ffn_conv_w, ffn_conv_b, w_ffn_down, final_g,
                 loss_target, ms, vs):
    ax, ay, ac = lax.axis_index("x"), lax.axis_index("y"), lax.axis_index("c")
    chip = 2 * ax + ay
    batch_row = 4 * ax + 2 * ay + ac
    d = x.shape[2]
    aw = w_attn_proj.shape[1]
    sw = w_ssm_glu.shape[1]
    ff = N_CHIPS * ffn_conv_w.shape[2]
    ngroups = sw // SSM_GROUP

    c_all = _all_gather8(jnp.pad(c, ((0, 7), (0, 0))), "gather_c").reshape(N_DEV, 8, d)[:, 0, :]
    ncol = ada_w.shape[2]
    b_shard = lax.dynamic_slice(ada_b, (0, chip * ncol), (1, ncol))
    mod_blk = _ada_fwd(c_all, ada_w[0], b_shard, "ada_fwd")
    mod_all = _all_gather8(mod_blk, "gather_mod").reshape(N_CHIPS, 2, 8, ncol)[:, 0]
    mod = lax.dynamic_slice(mod_all, (0, batch_row, 0), (N_CHIPS, 1, ncol)).reshape(1, 6 * d)

    conv_w_all = _all_gather8(jnp.pad(ffn_conv_w[0], ((0, 5), (0, 0))), "gather_conv_w")
    conv_w_full = conv_w_all.reshape(N_CHIPS, 2, 8, ff // N_CHIPS)[:, 0, :3].transpose(1, 0, 2).reshape(3, ff)
    placed_in = _place_shard(w_in[0].T.astype(BF16), "place_shard_0", after=(mod, conv_w_full))
    (first,), started_in = _gather_start([[placed_in]], "gather_start_w_in")
    shards = [w_attn_proj[0].T.astype(BF16), w_ssm_glu[0].T.astype(BF16), w_out[0], w_ffn_up[0].T.astype(BF16), w_ffn_down[0]]
    placed = [_place_shard(s, f"place_shard_{i + 1}", after=(started_in,)) for i, s in enumerate(shards)]
    (mixer, ffn), started = _gather_start([placed[:3], placed[3:]], "gather_start_rest")
    ssm = (ssm_a_re[0], ssm_a_im[0], ssm_log_dt[0], ssm_b_re[0], ssm_b_im[0], ssm_c_re[0], ssm_c_im[0], ssm_d[0])
    ssm_tables = _ssm_setup(x.shape[1], *ssm[:7])
    (w_in_t,) = _forward_then_finish(first, (started, ssm_tables[1], ssm_tables[2], *ssm_tables[0].values()), "w_in")
    mod = mod + (started_in[0:1, 0:1] + started[0:1, 0:1])

    pending = {}

    def mixer_arrived(after):
        pending["mixer"] = _gather_forward(mixer[2], mixer[0], mixer[1], after, "gather_forward_mixer")
        return pending["mixer"][3]

    def later_weights(after):
        m_send, m_recv, m_bufs, _ = pending["mixer"]
        f_send, f_recv, f_bufs, f_started = _gather_forward(ffn[2], ffn[0], ffn[1], after, "gather_forward_ffn")
        mixer_weights = _gather_finish(m_bufs, m_send, m_recv, "gather_finish_mixer", (f_started,))
        return mixer_weights, lambda later: _gather_finish(f_bufs, f_send, f_recv, "gather_finish_ffn", later)

    def grads_started(tag, grads):
        pending["swap_" + tag], token = _swap_start(grads, f"swap_start_{tag}")
        return token

    def ffn_grads_ready(mod_now, after):
        pending["ffn"], token = _reduce_start(pending["swap_ffn"], "ffn", after)
        return mod_now + token[0:1, 0:1]

    comm = {"mixer_arrived": mixer_arrived, "later_weights": later_weights, "grads_started": grads_started,
            "ffn_grads_ready": ffn_grads_ready}
    loss_part, grad_x, small_parts = _local_step(
        x[0], loss_target[0], mod, w_in_t, comm, norm_mix_g, attn_sinks, ssm_tables, ssm[7], norm_ffn_g, conv_w_full, ffn_conv_b,
        final_g, aw, sw, ff)
    loss = lax.psum(loss_part[0, 0], ("x", "y", "c"))

    small_shapes = [p.shape for p in small_parts]
    part_buf = _flatten_pad(small_parts).astype(BF16)
    rows = part_buf.shape[0]
    gathered = _all_gather8(part_buf, "gather_small").reshape(N_DEV, rows, SMALL_COLS)
    pending["rest"], rest_token = _reduce_start(pending["swap_rest"], "rest", after=(gathered, grad_x))
    gup_t, grad_w_down = _reduce_finish(pending["ffn"], (rest_token,), "ffn")
    grad_w_up = gup_t.T
    summed = _sum_devices(gathered, "sum_small")
    (s_dmod, s_gain1, s_sinks, s_lr, s_li, s_bbr, s_bbi, s_cr, s_ci, s_dd, s_gain2, s_cb, s_cw, s_fg) = _split_flat(summed, small_shapes)
    _, ssm_vjp = jax.vjp(_ssm_discretize, *ssm[:5])
    g_a_re, g_a_im, g_log_dt, g_b_re, g_b_im = ssm_vjp((s_lr.reshape(ngroups, SSM_STATE), s_li.reshape(ngroups, SSM_STATE), s_bbr, s_bbi))
    g_c_re, g_c_im = jnp.swapaxes(s_cr, 1, 2), jnp.swapaxes(s_ci, 1, 2)
    g_conv_w = lax.dynamic_slice(s_cw, (0, chip * (ff // N_CHIPS)), (3, ff // N_CHIPS))

    dmod_all = gathered.reshape(N_DEV, -1)[:, :6 * d].astype(F32)
    dmod_shard = lax.dynamic_slice(dmod_all, (0, chip * ncol), (N_DEV, ncol))
    ada_res = _ada_bwd_adam(c_all.T, dmod_shard, ada_w[0], ms["ada_w"][0], vs["ada_w"][0], "ada_bwd_adam")

    res = {"ada_w": tuple(o[None] for o in ada_res)}

    def adam_big(nm, w, g):
        res[nm] = (g[None],) + tuple(o[None] for o in _adam(w[0], g, ms[nm][0], vs[nm][0], "adam_" + nm))

    adam_big("w_ffn_up", w_ffn_up, grad_w_up)
    adam_big("w_ffn_down", w_ffn_down, grad_w_down)

    small = [("ada_b", ada_b, s_dmod), ("norm_mix_g", norm_mix_g, s_gain1), ("attn_sinks", attn_sinks, s_sinks),
             ("ssm_a_re", ssm_a_re, g_a_re), ("ssm_a_im", ssm_a_im, g_a_im), ("ssm_log_dt", ssm_log_dt, g_log_dt),
             ("ssm_b_re", ssm_b_re, g_b_re), ("ssm_b_im", ssm_b_im, g_b_im), ("ssm_c_re", ssm_c_re, g_c_re),
             ("ssm_c_im", ssm_c_im, g_c_im), ("ssm_d", ssm_d, s_dd), ("norm_ffn_g", norm_ffn_g, s_gain2),
             ("ffn_conv_w", ffn_conv_w, g_conv_w), ("ffn_conv_b", ffn_conv_b, s_cb), ("final_g", final_g, s_fg)]
    shapes = [t[1].shape for t in small]
    bufs = [_flatten_pad([t[1] for t in small]), _flatten_pad([t[2] for t in small]),
            _flatten_pad([ms[t[0]] for t in small]), _flatten_pad([vs[t[0]] for t in small])]
    s_delta, s_m, s_v = _adam(*bufs, "adam_small")
    for t, dl, m2, v2 in zip(small, _split_flat(s_delta, shapes), _split_flat(s_m, shapes), _split_flat(s_v, shapes)):
        res[t[0]] = (t[2].reshape(t[1].shape), dl, m2, v2)

    done = (s_delta, res["w_ffn_up"][1], res["w_ffn_down"][1], res["ada_w"][1])
    gi_t, gap_t, gglu_t, grad_w_out = _reduce_finish(pending["rest"], done, "rest")
    adam_big("w_in", w_in, gi_t.T)
    adam_big("w_attn_proj", w_attn_proj, gap_t.T)
    adam_big("w_ssm_glu", w_ssm_glu, gglu_t.T)
    adam_big("w_out", w_out, grad_w_out)

    outs = [loss, grad_x[None]]
    for i in range(4):
        outs += [res[nm][i] for nm in WEIGHT_ORDER]
    return tuple(outs)


WEIGHT_ORDER = ("ada_w", "ada_b", "norm_mix_g", "w_in", "attn_sinks", "w_attn_proj", "ssm_a_re", "ssm_a_im", "ssm_log_dt", "ssm_b_re",
                "ssm_b_im", "ssm_c_re", "ssm_c_im", "ssm_d", "w_ssm_glu", "w_out", "norm_ffn_g", "w_ffn_up", "ffn_conv_w", "ffn_conv_b",
                "w_ffn_down", "final_g")


def kernel(x, c, ada_w, ada_b, norm_mix_g, w_in, attn_sinks, w_attn_proj, ssm_a_re, ssm_a_im, ssm_log_dt, ssm_b_re, ssm_b_im, ssm_c_re, ssm_c_im, ssm_d, w_ssm_glu, w_out, norm_ffn_g, w_ffn_up, ffn_conv_w, ffn_conv_b, w_ffn_down, final_g, loss_target, m_ada_w, m_ada_b, m_norm_mix_g, m_w_in, m_attn_sinks, m_w_attn_proj, m_ssm_a_re, m_ssm_a_im, m_ssm_log_dt, m_ssm_b_re, m_ssm_b_im, m_ssm_c_re, m_ssm_c_im, m_ssm_d, m_w_ssm_glu, m_w_out, m_norm_ffn_g, m_w_ffn_up, m_ffn_conv_w, m_ffn_conv_b, m_w_ffn_down, m_final_g, v_ada_w, v_ada_b, v_norm_mix_g, v_w_in, v_attn_sinks, v_w_attn_proj, v_ssm_a_re, v_ssm_a_im, v_ssm_log_dt, v_ssm_b_re, v_ssm_b_im, v_ssm_c_re, v_ssm_c_im, v_ssm_d, v_w_ssm_glu, v_w_out, v_norm_ffn_g, v_w_ffn_up, v_ffn_conv_w, v_ffn_conv_b, v_w_ffn_down, v_final_g):
    ms = dict(zip(WEIGHT_ORDER, (m_ada_w, m_ada_b, m_norm_mix_g, m_w_in, m_attn_sinks, m_w_attn_proj, m_ssm_a_re, m_ssm_a_im, m_ssm_log_dt,
                                 m_ssm_b_re, m_ssm_b_im, m_ssm_c_re, m_ssm_c_im, m_ssm_d, m_w_ssm_glu, m_w_out, m_norm_ffn_g, m_w_ffn_up,
                                 m_ffn_conv_w, m_ffn_conv_b, m_w_ffn_down, m_final_g)))
    vs = dict(zip(WEIGHT_ORDER, (v_ada_w, v_ada_b, v_norm_mix_g, v_w_in, v_attn_sinks, v_w_attn_proj, v_ssm_a_re, v_ssm_a_im, v_ssm_log_dt,
                                 v_ssm_b_re, v_ssm_b_im, v_ssm_c_re, v_ssm_c_im, v_ssm_d, v_w_ssm_glu, v_w_out, v_norm_ffn_g, v_w_ffn_up,
                                 v_ffn_conv_w, v_ffn_conv_b, v_w_ffn_down, v_final_g)))
    return _kernel_impl(x, c, ada_w, ada_b, norm_mix_g, w_in, attn_sinks, w_attn_proj, ssm_a_re, ssm_a_im, ssm_log_dt, ssm_b_re, ssm_b_im,
                        ssm_c_re, ssm_c_im, ssm_d, w_ssm_glu, w_out, norm_ffn_g, w_ffn_up, ffn_conv_w, ffn_conv_b, w_ffn_down, final_g,
                        loss_target, ms, vs)
```

```python
import math

import jax
import jax.numpy as jnp
from jax import lax
from jax.experimental import pallas as pl
from jax.experimental.pallas import tpu as pltpu

F32 = jnp.float32
BF16 = jnp.bfloat16
MESH = pl.DeviceIdType.MESH

HEAD_DIM = 64
N_KV_HEADS = 2
KV_WIDTH = N_KV_HEADS * HEAD_DIM
ATTN_BLOCK = 128
NEG_INF = -1e30
SSM_GROUP = 16
SSM_STATE = 64
GROUPS_PER_TILE = 8
RMS_EPS = 1e-6
ADAM_LR = 0.001
ADAM_B1 = 0.9
ADAM_B2 = 0.999
ADAM_EPS = 1e-08
ADAM_WD = 0.01
ADAM_STEP = 10
N_CHIPS = 4
N_DEV = 8
VMEM_LIMIT_BYTES = 56 * 1024 * 1024
LANES = 128
SMALL_COLS = 1024


def _tile(dim, target, mult=LANES):
    if dim <= target:
        return dim
    for t in range(target // mult * mult, 0, -mult):
        if dim % t == 0:
            return t
    raise ValueError(f"no tile for {dim}")


def _params(sem=None):
    return pltpu.CompilerParams(dimension_semantics=sem, vmem_limit_bytes=VMEM_LIMIT_BYTES)


def _sigmoid(x):
    return 1.0 / (1.0 + jnp.exp(-x))


def _matmul(a, b, mode, name, out_dtype=BF16, tm=1536, tn=1536, tk=2048, interleave=None, after=()):
    if mode == "nn":
        (m, k), (k2, n) = a.shape, b.shape
    elif mode == "nt":
        (m, k), (n, k2) = a.shape, b.shape
    else:
        (k, m), (k2, n) = a.shape, b.shape
    assert k == k2, (a.shape, b.shape, mode)
    if interleave is not None:
        tn, tk, tm = (interleave, tk, tm) if mode == "nt" else (tn, interleave, tm) if mode == "nn" else (tn, tk, interleave)
        half = {"nt": n, "nn": k, "tn": m}[mode] // (2 * interleave)

        def perm(blk):
            return blk // 2 + (blk % 2) * half
    else:
        def perm(blk):
            return blk
        if mode == "tn":
            tm, tk = min(tm, 1024), max(tk, 4096)
    tm, tn, tk = _tile(m, tm), _tile(n, tn), _tile(k, tk)
    nk = k // tk
    if mode == "tn":
        a_spec = pl.BlockSpec((tk, tm), lambda i, j, kk: (kk, i))
    else:
        a_spec = pl.BlockSpec((tm, tk), lambda i, j, kk: (i, kk))
    if mode == "nt":
        b_spec = pl.BlockSpec((tn, tk), lambda i, j, kk: (perm(j), kk))
    elif mode == "nn":
        b_spec = pl.BlockSpec((tk, tn), lambda i, j, kk: (perm(kk), j))
    else:
        b_spec = pl.BlockSpec((tk, tn), lambda i, j, kk: (kk, j))
    out_rows = perm if mode == "tn" else (lambda blk: blk)
    dims = {"nn": (((1,), (0,)), ((), ())), "nt": (((1,), (1,)), ((), ())), "tn": (((0,), (0,)), ((), ()))}[mode]

    def body(a_ref, b_ref, *rest):
        o_ref, acc_ref = rest[-2:]
        kk = pl.program_id(2)

        @pl.when(kk == 0)
        def _():
            acc_ref[...] = jnp.zeros_like(acc_ref)

        acc_ref[...] += lax.dot_general(a_ref[...], b_ref[...], dims, preferred_element_type=F32)

        @pl.when(kk == nk - 1)
        def _():
            o_ref[...] = acc_ref[...].astype(o_ref.dtype)

    return pl.pallas_call(
        body,
        out_shape=jax.ShapeDtypeStruct((m, n), out_dtype),
        grid=(m // tm, n // tn, nk),
        in_specs=[a_spec, b_spec] + [pl.BlockSpec(memory_space=pl.ANY)] * len(after),
        out_specs=pl.BlockSpec((tm, tn), lambda i, j, kk: (out_rows(i), j)),
        scratch_shapes=[pltpu.VMEM((tm, tn), F32)],
        name=name,
        compiler_params=_params(("parallel", "parallel", "arbitrary")),
    )(a, b, *after)


def _row_spec(tl, w, col=0):
    return pl.BlockSpec((tl, w), lambda i, col=col: (i, col))


def _vec_spec(w, col=0):
    return pl.BlockSpec((1, w), lambda i, col=col: (0, col))


def _norm_mod(x, gain, mod, sc_col, sh_col, name):
    l, d = x.shape
    tl = _tile(l, 256, 8)

    def body(x_ref, g_ref, sc_ref, sh_ref, h_ref):
        xv = x_ref[...]
        r = lax.rsqrt(jnp.mean(xv * xv, axis=-1, keepdims=True) + RMS_EPS)
        h_ref[...] = ((xv * r) * g_ref[...] * (1.0 + sc_ref[...]) + sh_ref[...]).astype(BF16)

    return pl.pallas_call(
        body,
        out_shape=jax.ShapeDtypeStruct((l, d), BF16),
        grid=(l // tl,),
        in_specs=[_row_spec(tl, d), _vec_spec(d), _vec_spec(d, sc_col), _vec_spec(d, sh_col)],
        out_specs=_row_spec(tl, d),
        name=name,
        compiler_params=_params(("parallel",)),
    )(x, gain, mod, mod)


def _resid_norm_mod(x, mo, gain, mod, gate_col, sc_col, sh_col, name):
    l, d = x.shape
    tl = _tile(l, 256, 8)

    def body(x_ref, mo_ref, g_ref, gate_ref, sc_ref, sh_ref, x2_ref, h_ref):
        xv = x_ref[...] + gate_ref[...] * mo_ref[...]
        x2_ref[...] = xv
        r = lax.rsqrt(jnp.mean(xv * xv, axis=-1, keepdims=True) + RMS_EPS)
        h_ref[...] = ((xv * r) * g_ref[...] * (1.0 + sc_ref[...]) + sh_ref[...]).astype(BF16)

    return pl.pallas_call(
        body,
        out_shape=(jax.ShapeDtypeStruct((l, d), F32), jax.ShapeDtypeStruct((l, d), BF16)),
        grid=(l // tl,),
        in_specs=[_row_spec(tl, d), _row_spec(tl, d), _vec_spec(d), _vec_spec(d, gate_col), _vec_spec(d, sc_col),
                  _vec_spec(d, sh_col)],
        out_specs=(_row_spec(tl, d), _row_spec(tl, d)),
        name=name,
        compiler_params=_params(("parallel",)),
    )(x, mo, gain, mod, mod, mod)


def _final_loss(x2, f, mod, gate_col, final_g, target, name):
    l, d = x2.shape
    tl = _tile(l, 256, 8)

    def body(x2_ref, f_ref, gate_ref, fg_ref, t_ref, loss_ref, dfg_ref, dgate_ref, dx3_ref, df_ref):
        i = pl.program_id(0)
        fv = f_ref[...]
        x3 = x2_ref[...] + gate_ref[...] * fv
        r = lax.rsqrt(jnp.mean(x3 * x3, axis=-1, keepdims=True) + RMS_EPS)
        xh = x3 * r
        err = xh * fg_ref[...] - t_ref[...]
        part = 0.5 * jnp.sum(jnp.mean(err * err, axis=-1, keepdims=True), axis=0, keepdims=True)
        dout = err * (1.0 / d)
        dxh = dout * fg_ref[...]
        dx3 = r * (dxh - xh * jnp.mean(dxh * xh, axis=-1, keepdims=True))
        dx3_ref[...] = dx3
        df_ref[...] = (gate_ref[...] * dx3).astype(BF16)

        @pl.when(i == 0)
        def _():
            loss_ref[...] = jnp.zeros_like(loss_ref)
            dfg_ref[...] = jnp.zeros_like(dfg_ref)
            dgate_ref[...] = jnp.zeros_like(dgate_ref)

        loss_ref[...] += jnp.broadcast_to(part, loss_ref.shape)
        dfg_ref[...] += jnp.sum(dout * xh, axis=0, keepdims=True)
        dgate_ref[...] += jnp.sum(dx3 * fv, axis=0, keepdims=True)

    vec = pl.BlockSpec((1, d), lambda i: (0, 0))
    return pl.pallas_call(
        body,
        out_shape=(jax.ShapeDtypeStruct((1, LANES), F32), jax.ShapeDtypeStruct((1, d), F32),
                   jax.ShapeDtypeStruct((1, d), F32), jax.ShapeDtypeStruct((l, d), F32),
                   jax.ShapeDtypeStruct((l, d), BF16)),
        grid=(l // tl,),
        in_specs=[_row_spec(tl, d), _row_spec(tl, d), _vec_spec(d, gate_col), vec, _row_spec(tl, d)],
        out_specs=(pl.BlockSpec((1, LANES), lambda i: (0, 0)), vec, vec, _row_spec(tl, d), _row_spec(tl, d)),
        name=name,
        compiler_params=_params(("arbitrary",)),
    )(x2, f, mod, final_g, target)


def _norm_mod_bwd(dh, x, dx_res, gain, mod, sc_col, name, branch=None, gate_col=None):
    l, d = x.shape
    tl = _tile(l, 256, 8)
    with_gate = branch is not None

    def body(*refs):
        if with_gate:
            dh_ref, x_ref, dr_ref, g_ref, sc_ref, br_ref, gate_ref, dx_ref, dsh_ref, dsc_ref, dg_ref, dm_ref, dgate_ref = refs
        else:
            dh_ref, x_ref, dr_ref, g_ref, sc_ref, dx_ref, dsh_ref, dsc_ref, dg_ref = refs
        i = pl.program_id(0)
        xv = x_ref[...]
        dhv = dh_ref[...].astype(F32)
        r = lax.rsqrt(jnp.mean(xv * xv, axis=-1, keepdims=True) + RMS_EPS)
        xh = xv * r
        dn = dhv * (1.0 + sc_ref[...])
        dxh = dn * g_ref[...]
        dx = dr_ref[...] + r * (dxh - xh * jnp.mean(dxh * xh, axis=-1, keepdims=True))
        dx_ref[...] = dx

        @pl.when(i == 0)
        def _():
            dsh_ref[...] = jnp.zeros_like(dsh_ref)
            dsc_ref[...] = jnp.zeros_like(dsc_ref)
            dg_ref[...] = jnp.zeros_like(dg_ref)
            if with_gate:
                dgate_ref[...] = jnp.zeros_like(dgate_ref)

        dsh_ref[...] += jnp.sum(dhv, axis=0, keepdims=True)
        dsc_ref[...] += jnp.sum(dhv * (xh * g_ref[...]), axis=0, keepdims=True)
        dg_ref[...] += jnp.sum(dn * xh, axis=0, keepdims=True)
        if with_gate:
            dm_ref[...] = (gate_ref[...] * dx).astype(BF16)
            dgate_ref[...] += jnp.sum(dx * br_ref[...], axis=0, keepdims=True)

    vec = pl.BlockSpec((1, d), lambda i: (0, 0))
    in_specs = [_row_spec(tl, d), _row_spec(tl, d), _row_spec(tl, d), vec, _vec_spec(d, sc_col)]
    args = [dh, x, dx_res, gain, mod]
    out_shape = [jax.ShapeDtypeStruct((l, d), F32)] + [jax.ShapeDtypeStruct((1, d), F32)] * 3
    out_specs = [_row_spec(tl, d), vec, vec, vec]
    if with_gate:
        in_specs += [_row_spec(tl, d), _vec_spec(d, gate_col)]
        args += [branch, mod]
        out_shape += [jax.ShapeDtypeStruct((l, d), BF16), jax.ShapeDtypeStruct((1, d), F32)]
        out_specs += [_row_spec(tl, d), vec]
    return pl.pallas_call(
        body, out_shape=tuple(out_shape), grid=(l // tl,), in_specs=in_specs, out_specs=tuple(out_specs),
        name=name, compiler_params=_params(("arbitrary",)),
    )(*args)


def _attn_mask(n, rows):
    del rows
    qi = lax.broadcasted_iota(jnp.int32, (ATTN_BLOCK, 2 * ATTN_BLOCK), 0)
    kj = lax.broadcasted_iota(jnp.int32, (ATTN_BLOCK, 2 * ATTN_BLOCK), 1)
    rel = qi + ATTN_BLOCK - kj
    return jnp.where((rel >= 0) & (rel < ATTN_BLOCK) & ((kj >= ATTN_BLOCK) | (n > 0)), 0.0, NEG_INF)


def _attn_probs(qs, kh, sink, mask):
    rows = qs.shape[0]
    s = lax.dot_general(qs, kh, (((1,), (1,)), ((), ())), preferred_element_type=F32) * (HEAD_DIM ** -0.5)
    s = s.reshape(-1, ATTN_BLOCK, 2 * ATTN_BLOCK) + mask[None]
    m = jnp.maximum(jnp.max(s, axis=-1, keepdims=True), sink)
    p = jnp.exp(s - m)
    es = jnp.exp(sink - m)
    inv = 1.0 / (jnp.sum(p, axis=-1, keepdims=True) + es)
    return (p * inv).reshape(rows, 2 * ATTN_BLOCK), (es * inv).reshape(rows, 1)


def _stack_heads(src_ref, dst_ref, g, qpk):
    for i in range(qpk):
        h = g * qpk + i
        dst_ref[i * ATTN_BLOCK:(i + 1) * ATTN_BLOCK, :] = src_ref[:, h * HEAD_DIM:(h + 1) * HEAD_DIM]


def _unstack_heads(val, dst_ref, g, qpk):
    for i in range(qpk):
        h = g * qpk + i
        dst_ref[:, h * HEAD_DIM:(h + 1) * HEAD_DIM] = val[i * ATTN_BLOCK:(i + 1) * ATTN_BLOCK, :].astype(dst_ref.dtype)


def _sink_column(sinks):
    return sinks.reshape(-1, 1, 1)


def _sink_spec(nq):
    return pl.BlockSpec((nq, 1, 1), lambda n: (0, 0, 0))


def _attn_specs(aw):
    kvb = aw // (2 * KV_WIDTH)
    q_spec = pl.BlockSpec((ATTN_BLOCK, aw), lambda n: (n, 0))
    kv_cur = pl.BlockSpec((ATTN_BLOCK, 2 * KV_WIDTH), lambda n: (n, kvb))
    kv_prev = pl.BlockSpec((ATTN_BLOCK, 2 * KV_WIDTH), lambda n: (jnp.maximum(n - 1, 0), kvb))
    return q_spec, kv_cur, kv_prev


def _attn_fwd(proj, sinks, aw, name, after=()):
    l = proj.shape[0]
    nq = aw // HEAD_DIM
    qpk = nq // N_KV_HEADS
    assert aw % (2 * KV_WIDTH) == 0

    rows = qpk * ATTN_BLOCK

    def body(q_ref, kvc_ref, kvp_ref, sink_ref, *rest):
        o_ref = rest[-1]
        n = pl.program_id(0)
        valid = _attn_mask(n, rows) == 0.0
        kv = jnp.concatenate([kvp_ref[...], kvc_ref[...]], axis=0)
        for h in range(nq):
            g = h // qpk
            qh = q_ref[:, h * HEAD_DIM:(h + 1) * HEAD_DIM]
            kh = kv[:, g * HEAD_DIM:(g + 1) * HEAD_DIM]
            vh = kv[:, KV_WIDTH + g * HEAD_DIM:KV_WIDTH + (g + 1) * HEAD_DIM]
            sink = sink_ref[0:1, h:h + 1]
            s = lax.dot_general(qh, kh, (((1,), (1,)), ((), ())), preferred_element_type=F32) * (HEAD_DIM ** -0.5)
            s = jnp.where(valid, s, NEG_INF)
            m = jnp.maximum(jnp.max(s, axis=-1, keepdims=True), sink)
            p = jnp.exp(s - m)
            p = p * (1.0 / (jnp.sum(p, axis=-1, keepdims=True) + jnp.exp(sink - m)))
            o = jnp.dot(p.astype(BF16), vh, preferred_element_type=F32)
            o_ref[:, h * HEAD_DIM:(h + 1) * HEAD_DIM] = o.astype(BF16)

    q_spec, kv_cur, kv_prev = _attn_specs(aw)
    return pl.pallas_call(
        body,
        out_shape=jax.ShapeDtypeStruct((l, aw), BF16),
        grid=(l // ATTN_BLOCK,),
        in_specs=[q_spec, kv_cur, kv_prev, pl.BlockSpec((1, nq), lambda n: (0, 0))] + [pl.BlockSpec(memory_space=pl.ANY)] * len(after),
        out_specs=pl.BlockSpec((ATTN_BLOCK, aw), lambda n: (n, 0)),
        name=name,
        compiler_params=_params(("parallel",)),
    )(proj, proj, proj, sinks, *after)


def _attn_bwd(proj, sinks, dattn, aw, name):
    l = proj.shape[0]
    nq = aw // HEAD_DIM
    qpk = nq // N_KV_HEADS
    scale = HEAD_DIM ** -0.5

    rows = qpk * ATTN_BLOCK
    tn_dims = (((0,), (0,)), ((), ()))

    def body(q_ref, kvc_ref, kvp_ref, sink_ref, do_ref, dq_ref, dcur_ref, dprev_ref, dsink_ref, q_scr, do_scr):
        n = pl.program_id(0)
        mask = _attn_mask(n, rows)
        kv = jnp.concatenate([kvp_ref[...], kvc_ref[...]], axis=0)
        lane = lax.broadcasted_iota(jnp.int32, (1, nq), 1)
        dsink = jnp.zeros((1, nq), F32)
        dks, dvs = [], []
        for g in range(N_KV_HEADS):
            kh = kv[:, g * HEAD_DIM:(g + 1) * HEAD_DIM]
            vh = kv[:, KV_WIDTH + g * HEAD_DIM:KV_WIDTH + (g + 1) * HEAD_DIM]
            _stack_heads(q_ref, q_scr.at[g], g, qpk)
            _stack_heads(do_ref, do_scr.at[g], g, qpk)
            qs, dos = q_scr[g], do_scr[g]
            p, ps = _attn_probs(qs, kh, sink_ref[g * qpk:(g + 1) * qpk], mask)
            pb = p.astype(BF16)
            o = jnp.dot(pb, vh, preferred_element_type=F32)
            delta = jnp.sum(dos.astype(F32) * o, axis=-1, keepdims=True)
            dp = lax.dot_general(dos, vh, (((1,), (1,)), ((), ())), preferred_element_type=F32)
            ds = (p * (dp - delta)).astype(BF16)
            _unstack_heads(jnp.dot(ds, kh, preferred_element_type=F32) * scale, dq_ref, g, qpk)
            dks.append(lax.dot_general(ds, qs, tn_dims, preferred_element_type=F32) * scale)
            dvs.append(lax.dot_general(pb, dos, tn_dims, preferred_element_type=F32))
            t = ps * delta
            for i in range(qpk):
                part = -jnp.sum(t[i * ATTN_BLOCK:(i + 1) * ATTN_BLOCK, :], axis=0, keepdims=True)
                dsink += jnp.where(lane == g * qpk + i, part, 0.0)
        dkv = jnp.concatenate(dks + dvs, axis=1)
        dprev_ref[...] = dkv[:ATTN_BLOCK]
        dcur_ref[...] = dkv[ATTN_BLOCK:]

        @pl.when(n == 0)
        def _():
            dsink_ref[...] = jnp.zeros_like(dsink_ref)

        dsink_ref[...] += dsink

    q_spec, kv_cur, kv_prev = _attn_specs(aw)
    blk = pl.BlockSpec((ATTN_BLOCK, 2 * KV_WIDTH), lambda n: (n, 0))
    return pl.pallas_call(
        body,
        out_shape=(jax.ShapeDtypeStruct((l, aw), BF16), jax.ShapeDtypeStruct((l, 2 * KV_WIDTH), F32),
                   jax.ShapeDtypeStruct((l, 2 * KV_WIDTH), F32), jax.ShapeDtypeStruct((1, nq), F32)),
        grid=(l // ATTN_BLOCK,),
        in_specs=[q_spec, kv_cur, kv_prev, _sink_spec(nq),
                  pl.BlockSpec((ATTN_BLOCK, aw), lambda n: (n, 0))],
        out_specs=(pl.BlockSpec((ATTN_BLOCK, aw), lambda n: (n, 0)), blk, blk, pl.BlockSpec((1, nq), lambda n: (0, 0))),
        scratch_shapes=[pltpu.VMEM((N_KV_HEADS, rows, HEAD_DIM), BF16)] * 2,
        name=name,
        compiler_params=_params(("arbitrary",)),
    )(proj, proj, proj, _sink_column(sinks), dattn)


def _ssm_discretize(a_re, a_im, log_dt, b_re, b_im):
    dt = jnp.exp(log_dt)[:, None]
    mag = jnp.exp(a_re * dt)
    lr, li = mag * jnp.cos(a_im * dt), mag * jnp.sin(a_im * dt)
    den = a_re * a_re + a_im * a_im
    zr = ((lr - 1.0) * a_re + li * a_im) / den
    zi = (li * a_re - (lr - 1.0) * a_im) / den
    bbar_r = zr[:, :, None] * b_re - zi[:, :, None] * b_im
    bbar_i = zr[:, :, None] * b_im + zi[:, :, None] * b_re
    return lr, li, bbar_r, bbar_i


def _cmul(ar, ai, br, bi):
    return ar * br - ai * bi, ar * bi + ai * br


N_SEG = 8


def _cpow(ar, ai, n):
    out, br, bi = None, ar, ai
    while n:
        if n & 1:
            out = (br, bi) if out is None else _cmul(*out, br, bi)
        br, bi = _cmul(br, bi, br, bi)
        n >>= 1
    return out


def _scan_tables(lr, li, seg):
    lr, li = lr.reshape(1, -1), li.reshape(1, -1)
    row = jnp.arange(N_SEG)[:, None]
    ones = jnp.ones((N_SEG, 1), F32)
    fwd, bwd = [], []
    for d in (1, 2, 4):
        pr, pi = _cpow(lr, li, seg * d)
        fwd += [jnp.where(row >= d, pr, 0.0), jnp.where(row >= d, pi, 0.0)]
        bwd += [jnp.where(row < N_SEG - d, pr, 0.0), jnp.where(row < N_SEG - d, -pi, 0.0)]
    fwd += [ones * lr, ones * li]
    bwd += [ones * lr, ones * -li]
    return jnp.concatenate(fwd, 0), jnp.concatenate(bwd, 0)


def _pack_in(b):
    g, n, p = b.shape
    t = g // GROUPS_PER_TILE
    eye = jnp.eye(GROUPS_PER_TILE, dtype=b.dtype)
    bb = b.reshape(t, GROUPS_PER_TILE, n, p)
    return jnp.einsum("tgnp,gh->tgphn", bb, eye).reshape(t, GROUPS_PER_TILE * p, GROUPS_PER_TILE * n)


def _pack_out(c):
    g, p, n = c.shape
    t = g // GROUPS_PER_TILE
    eye = jnp.eye(GROUPS_PER_TILE, dtype=c.dtype)
    cc = c.reshape(t, GROUPS_PER_TILE, p, n)
    return jnp.einsum("tgpn,gh->tgnhp", cc, eye).reshape(t, GROUPS_PER_TILE * n, GROUPS_PER_TILE * p)


def _unpack_diag(x, n, p):
    t = x.shape[0]
    xx = x.reshape(t, GROUPS_PER_TILE, n, GROUPS_PER_TILE, p)
    eye = jnp.eye(GROUPS_PER_TILE, dtype=x.dtype)
    return jnp.einsum("tgnhp,gh->tgnp", xx, eye).reshape(t * GROUPS_PER_TILE, n, p)


def _seg_scan(hr_ref, hi_ref, tab_ref, l, reverse, states_refs=None):
    nq = hr_ref.shape[0]
    seg = l // N_SEG
    span = 8 * N_SEG
    nblk = seg // 8
    row = lax.broadcasted_iota(jnp.int32, (N_SEG, LANES), 0)

    def tab(r0, q):
        return tab_ref[r0:r0 + 8, q * LANES:(q + 1) * LANES]

    lam = [(tab(48, q), tab(56, q)) for q in range(nq)]

    def views(refs, q, jb):
        base = pl.multiple_of((nblk - 1 - jb if reverse else jb) * span, span)
        return [r.at[q, pl.ds(base, span), :] for r in refs]

    def local_rows():
        return range(7, -1, -1) if reverse else range(8)

    def at(r):
        return pl.ds(r * N_SEG, N_SEG)

    def pass1(jb, carry):
        hs = list(carry)
        for q in range(nq):
            vr, vi = views((hr_ref, hi_ref), q, jb)
            lr, li = lam[q]
            h_r, h_i = hs[2 * q], hs[2 * q + 1]
            for r in local_rows():
                h_r, h_i = lr * h_r - li * h_i + vr[at(r), :], lr * h_i + li * h_r + vi[at(r), :]
                vr[at(r), :] = h_r
                vi[at(r), :] = h_i
            hs[2 * q], hs[2 * q + 1] = h_r, h_i
        return tuple(hs)

    zero = jnp.zeros((N_SEG, LANES), F32)
    ends = lax.fori_loop(0, nblk, pass1, (zero,) * (2 * nq))

    carry_in = []
    for q in range(nq):
        er, ei = ends[2 * q], ends[2 * q + 1]
        for idx, d in enumerate((1, 2, 4)):
            mr, mi = tab(16 * idx, q), tab(16 * idx + 8, q)
            shift = N_SEG - d if reverse else d
            sr, si = pltpu.roll(er, shift, 0), pltpu.roll(ei, shift, 0)
            er, ei = er + mr * sr - mi * si, ei + mr * si + mi * sr
        if reverse:
            keep, shift = row < N_SEG - 1, N_SEG - 1
        else:
            keep, shift = row >= 1, 1
        carry_in += [jnp.where(keep, pltpu.roll(er, shift, 0), 0.0), jnp.where(keep, pltpu.roll(ei, shift, 0), 0.0)]

    with_acc = states_refs is not None

    def pass2(jb, carry):
        cs = list(carry)
        for q in range(nq):
            vr, vi = views((hr_ref, hi_ref), q, jb)
            lr, li = lam[q]
            d_r, d_i = cs[2 * q], cs[2 * q + 1]
            if with_acc:
                fr, fi = views(states_refs, q, jb)
                n_r, n_i, a_r, a_i = cs[2 * nq + 4 * q:2 * nq + 4 * q + 4]
            for r in local_rows():
                d_r, d_i = lr * d_r - li * d_i, lr * d_i + li * d_r
                g_r, g_i = vr[at(r), :] + d_r, vi[at(r), :] + d_i
                vr[at(r), :] = g_r
                vi[at(r), :] = g_i
                if with_acc:
                    p_r, p_i = fr[at(r), :], fi[at(r), :]
                    a_r, a_i = a_r + n_r * p_r + n_i * p_i, a_i + n_i * p_r - n_r * p_i
                    n_r, n_i = g_r, g_i
            cs[2 * q], cs[2 * q + 1] = d_r, d_i
            if with_acc:
                cs[2 * nq + 4 * q:2 * nq + 4 * q + 4] = [n_r, n_i, a_r, a_i]
        return tuple(cs)

    init = list(carry_in)
    if with_acc:
        for q in range(nq):
            init += [carry_in[2 * q], carry_in[2 * q + 1], zero, zero]
    out = lax.fori_loop(0, nblk, pass2, tuple(init))
    if with_acc:
        return [(out[2 * nq + 4 * q + 2], out[2 * nq + 4 * q + 3]) for q in range(nq)]
    return None


def _put_states(ref, rows, val):
    for q in range(ref.shape[0]):
        ref[q, rows, :] = val[:, q * LANES:(q + 1) * LANES]


def _get_states(ref, rows):
    return jnp.concatenate([ref[q, rows, :] for q in range(ref.shape[0])], axis=1)


def _s5_dims(sw):
    chan = GROUPS_PER_TILE * SSM_GROUP
    states = GROUPS_PER_TILE * SSM_STATE
    assert chan == LANES and sw % chan == 0
    return sw // chan, chan, states


def _interleave(x):
    l, w = x.shape
    return x.reshape(N_SEG, l // N_SEG, w).transpose(1, 0, 2).reshape(l, w)


def _deinterleave(x):
    l, w = x.shape
    return x.reshape(l // N_SEG, N_SEG, w).transpose(1, 0, 2).reshape(l, w)


def _s5_fwd(proj, u_off, packs, dvec, tab_f, sw, name):
    l = proj.shape[0]
    nt, chan, states = _s5_dims(sw)
    ch = _tile(l, 512, 8)
    ub = u_off // chan
    assert u_off % chan == 0

    def body(u_ref, br_ref, bi_ref, cr_ref, ci_ref, d_ref, tab_ref, y_ref, gy_ref, hr_ref, hi_ref):
        for i in range(l // ch):
            rows = pl.ds(i * ch, ch)
            u = u_ref[rows, :]
            _put_states(hr_ref, rows, jnp.dot(u, br_ref[0], preferred_element_type=F32))
            _put_states(hi_ref, rows, jnp.dot(u, bi_ref[0], preferred_element_type=F32))
        _seg_scan(hr_ref, hi_ref, tab_ref, l, reverse=False)
        for i in range(l // ch):
            rows = pl.ds(i * ch, ch)
            y = jnp.dot(_get_states(hr_ref, rows).astype(BF16), cr_ref[0], preferred_element_type=F32)
            y -= jnp.dot(_get_states(hi_ref, rows).astype(BF16), ci_ref[0], preferred_element_type=F32)
            y = y + d_ref[...] * u_ref[rows, :].astype(F32)
            y_ref[rows, :] = y
            gy_ref[rows, :] = _gelu_value(y).astype(BF16)

    pin = pl.BlockSpec((1, chan, states), lambda t: (t, 0, 0))
    pout = pl.BlockSpec((1, states, chan), lambda t: (t, 0, 0))
    return pl.pallas_call(
        body,
        out_shape=(jax.ShapeDtypeStruct((l, sw), F32), jax.ShapeDtypeStruct((l, sw), BF16)),
        grid=(nt,),
        in_specs=[pl.BlockSpec((l, chan), lambda t: (0, ub + t)), pin, pin, pout, pout,
                  pl.BlockSpec((1, chan), lambda t: (0, t)), pl.BlockSpec((64, states), lambda t: (0, t))],
        out_specs=(pl.BlockSpec((l, chan), lambda t: (0, t)), pl.BlockSpec((l, chan), lambda t: (0, t))),
        scratch_shapes=[pltpu.VMEM((states // LANES, l, LANES), F32)] * 2,
        name=name,
        compiler_params=_params(("parallel",)),
    )(proj, packs["br"], packs["bi"], packs["cr"], packs["ci"], dvec, tab_f)


def _s5_bwd(proj, u_off, y, dgy, packs, dvec, tab_f, tab_b, sw, name):
    l = proj.shape[0]
    nt, chan, states = _s5_dims(sw)
    ch = _tile(l, 512, 8)
    ub = u_off // chan
    tn_dims = (((0,), (0,)), ((), ()))

    def body(u_ref, y_ref, dgy_ref, br_ref, bi_ref, brt_ref, bit_ref, crt_ref, cit_ref, d_ref, tabf_ref, tabb_ref,
             du_ref, dlam_ref, dbr_ref, dbi_ref, dcr_ref, dci_ref, dd_ref, hr_ref, hi_ref, gr_ref, gi_ref, dy_ref):
        for i in range(l // ch):
            rows = pl.ds(i * ch, ch)
            u = u_ref[rows, :]
            _put_states(hr_ref, rows, jnp.dot(u, br_ref[0], preferred_element_type=F32))
            _put_states(hi_ref, rows, jnp.dot(u, bi_ref[0], preferred_element_type=F32))
            dyv = (dgy_ref[rows, :].astype(F32) * _gelu_slope(y_ref[rows, :])).astype(BF16)
            dy_ref[rows, :] = dyv
            _put_states(gr_ref, rows, jnp.dot(dyv, crt_ref[0], preferred_element_type=F32))
            _put_states(gi_ref, rows, -jnp.dot(dyv, cit_ref[0], preferred_element_type=F32))
        _seg_scan(hr_ref, hi_ref, tabf_ref, l, reverse=False)
        accs = _seg_scan(gr_ref, gi_ref, tabb_ref, l, reverse=True, states_refs=(hr_ref, hi_ref))
        dlam_ref[...] = jnp.concatenate(
            [jnp.concatenate([jnp.sum(a[0], axis=0, keepdims=True) for a in accs], axis=1),
             jnp.concatenate([jnp.sum(a[1], axis=0, keepdims=True) for a in accs], axis=1), jnp.zeros((6, states), F32)], axis=0)
        dbr_ref[...] = jnp.zeros_like(dbr_ref)
        dbi_ref[...] = jnp.zeros_like(dbi_ref)
        dcr_ref[...] = jnp.zeros_like(dcr_ref)
        dci_ref[...] = jnp.zeros_like(dci_ref)
        dd = jnp.zeros((1, chan), F32)
        for i in range(l // ch):
            rows = pl.ds(i * ch, ch)
            u = u_ref[rows, :]
            dyv = dy_ref[rows, :]
            grb = _get_states(gr_ref, rows).astype(BF16)
            gib = _get_states(gi_ref, rows).astype(BF16)
            dbr_ref[0] += lax.dot_general(grb, u, tn_dims, preferred_element_type=F32)
            dbi_ref[0] += lax.dot_general(gib, u, tn_dims, preferred_element_type=F32)
            dcr_ref[0] += lax.dot_general(_get_states(hr_ref, rows).astype(BF16), dyv, tn_dims, preferred_element_type=F32)
            dci_ref[0] -= lax.dot_general(_get_states(hi_ref, rows).astype(BF16), dyv, tn_dims, preferred_element_type=F32)
            du = jnp.dot(grb, brt_ref[0], preferred_element_type=F32) + jnp.dot(gib, bit_ref[0], preferred_element_type=F32)
            dyf = dyv.astype(F32)
            du_ref[rows, :] = (du + d_ref[...] * dyf).astype(BF16)
            dd += jnp.sum(dyf * u.astype(F32), axis=0, keepdims=True)
        dd_ref[...] = dd

    pin = pl.BlockSpec((1, chan, states), lambda t: (t, 0, 0))
    pout = pl.BlockSpec((1, states, chan), lambda t: (t, 0, 0))
    seq = pl.BlockSpec((l, chan), lambda t: (0, t))
    tab = pl.BlockSpec((64, states), lambda t: (0, t))
    vec = pl.BlockSpec((1, chan), lambda t: (0, t))
    pack_shape = jax.ShapeDtypeStruct((nt, states, chan), F32)
    return pl.pallas_call(
        body,
        out_shape=(jax.ShapeDtypeStruct((l, sw), BF16), jax.ShapeDtypeStruct((8, nt * states), F32),
                   pack_shape, pack_shape, pack_shape, pack_shape, jax.ShapeDtypeStruct((1, sw), F32)),
        grid=(nt,),
        in_specs=[pl.BlockSpec((l, chan), lambda t: (0, ub + t)), seq, seq, pin, pin, pout, pout, pin, pin, vec, tab, tab],
        out_specs=(seq, pl.BlockSpec((8, states), lambda t: (0, t)), pout, pout, pout, pout, vec),
        scratch_shapes=[pltpu.VMEM((states // LANES, l, LANES), F32)] * 4 + [pltpu.VMEM((l, chan), BF16)],
        name=name,
        compiler_params=_params(("parallel",)),
    )(proj, y, dgy, packs["br"], packs["bi"], packs["brt"], packs["bit"], packs["crt"], packs["cit"], dvec, tab_f, tab_b)


GELU_K = math.sqrt(2.0 / math.pi)
GELU_C = 0.044715


def _gelu_value(v):
    return 0.5 * v * (1.0 + jnp.tanh(GELU_K * (v + GELU_C * v * v * v)))


def _gelu_slope(v):
    t = jnp.tanh(GELU_K * (v + GELU_C * v * v * v))
    return 0.5 * (1.0 + t) + 0.5 * v * (1.0 - t * t) * GELU_K * (1.0 + 3.0 * GELU_C * v * v)


def _mix(ga, gs, attn_out, glu, name):
    l, d = ga.shape
    tl = _tile(l, 256, 16)

    def body(ga_ref, gs_ref, a_ref, u_ref, o_ref):
        ssm = u_ref[:, :d].astype(F32) * _sigmoid(u_ref[:, d:].astype(F32))
        o_ref[...] = (_sigmoid(ga_ref[...].astype(F32)) * a_ref[...].astype(F32)
                      + _sigmoid(gs_ref[...].astype(F32)) * ssm).astype(BF16)

    return pl.pallas_call(
        body, out_shape=jax.ShapeDtypeStruct((l, d), BF16), grid=(l // tl,),
        in_specs=[_row_spec(tl, d), _row_spec(tl, d), _row_spec(tl, d), _row_spec(tl, 2 * d)], out_specs=_row_spec(tl, d),
        name=name, compiler_params=_params(("parallel",)),
    )(ga, gs, attn_out, glu)


def _mix_bwd(ga, gs, attn_out, glu, dmixed, name):
    l, d = ga.shape
    tl = _tile(l, 256, 16)

    def body(ga_ref, gs_ref, a_ref, u_ref, dm_ref, dga_ref, dgs_ref, da_ref, dglu_ref):
        dm = dm_ref[...].astype(F32)
        sa = _sigmoid(ga_ref[...].astype(F32))
        ss = _sigmoid(gs_ref[...].astype(F32))
        sb = _sigmoid(u_ref[:, d:].astype(F32))
        ua = u_ref[:, :d].astype(F32)
        dssm = dm * ss
        dga_ref[...] = (dm * a_ref[...].astype(F32) * sa * (1.0 - sa)).astype(BF16)
        da_ref[...] = (dm * sa).astype(BF16)
        dgs_ref[...] = (dm * (ua * sb) * ss * (1.0 - ss)).astype(BF16)
        dglu_ref[:, :d] = (dssm * sb).astype(BF16)
        dglu_ref[:, d:] = (dssm * ua * sb * (1.0 - sb)).astype(BF16)

    out = jax.ShapeDtypeStruct((l, d), BF16)
    return pl.pallas_call(
        body, out_shape=(out, out, out, jax.ShapeDtypeStruct((l, 2 * d), BF16)), grid=(l // tl,),
        in_specs=[_row_spec(tl, d), _row_spec(tl, d), _row_spec(tl, d), _row_spec(tl, 2 * d), _row_spec(tl, d)],
        out_specs=(_row_spec(tl, d), _row_spec(tl, d), _row_spec(tl, d), _row_spec(tl, 2 * d)), name=name,
        compiler_params=_params(("parallel",)),
    )(ga, gs, attn_out, glu, dmixed)


CONV_BLOCKS = 4
HALO = 16


def _shift_rows(v, k, head):
    row = lax.broadcasted_iota(jnp.int32, v.shape, 0)
    out = pltpu.roll(v, k, 0)
    for r in range(k):
        out = jnp.where(row == r, head[HALO - k + r:HALO - k + r + 1, :], out)
    return out


def _shift_rows_up(v, k, tail):
    n = v.shape[0]
    row = lax.broadcasted_iota(jnp.int32, v.shape, 0)
    out = pltpu.roll(v, n - k, 0)
    for r in range(k):
        out = jnp.where(row == n - k + r, tail[r:r + 1, :], out)
    return out


def _conv_gate(g, head, w_ref, b_ref):
    return w_ref[0:1, :] * _shift_rows(g, 2, head) + w_ref[1:2, :] * _shift_rows(g, 1, head) + w_ref[2:3, :] * g + b_ref[...]


def _conv_act(up, conv_w, conv_b, ff, cw, name):
    l = up.shape[0]
    tl = _tile(l, 256, HALO)
    nj = ff // cw
    hb = tl // HALO

    def body(g_ref, gp_ref, v_ref, w_ref, b_ref, o_ref):
        i = pl.program_id(0)
        head = gp_ref[...].astype(F32) * jnp.where(i > 0, 1.0, 0.0)
        gc = _conv_gate(g_ref[...].astype(F32), head, w_ref, b_ref)
        o_ref[...] = (gc * _sigmoid(gc) * v_ref[...].astype(F32)).astype(BF16)

    return pl.pallas_call(
        body, out_shape=jax.ShapeDtypeStruct((l, ff), BF16), grid=(l // tl, nj),
        in_specs=[pl.BlockSpec((tl, cw), lambda i, j: (i, 2 * j)),
                  pl.BlockSpec((HALO, cw), lambda i, j: (jnp.maximum(i * hb - 1, 0), 2 * j)),
                  pl.BlockSpec((tl, cw), lambda i, j: (i, 2 * j + 1)),
                  pl.BlockSpec((3, cw), lambda i, j: (0, j)), pl.BlockSpec((1, cw), lambda i, j: (0, j))],
        out_specs=pl.BlockSpec((tl, cw), lambda i, j: (i, j)), name=name,
        compiler_params=_params(("parallel", "parallel")),
    )(up, up, up, conv_w, conv_b)


def _conv_act_bwd(up, da, conv_w, conv_b, ff, cw, name):
    l = up.shape[0]
    tl = _tile(l, 256, HALO)
    nj = ff // cw
    hb = tl // HALO
    ni = l // tl

    def body(g_ref, gp_ref, gn_ref, v_ref, vn_ref, da_ref, dan_ref, w_ref, b_ref, dup_ref, dw_ref, db_ref):
        i = pl.program_id(0)
        g = g_ref[...].astype(F32)
        head = gp_ref[...].astype(F32) * jnp.where(i > 0, 1.0, 0.0)
        g1 = _shift_rows(g, 1, head)
        g2 = _shift_rows(g, 2, head)
        gc = w_ref[0:1, :] * g2 + w_ref[1:2, :] * g1 + w_ref[2:3, :] * g + b_ref[...]
        sg = _sigmoid(gc)
        dav = da_ref[...].astype(F32)
        dgc = dav * v_ref[...].astype(F32) * (sg * (1.0 + gc * (1.0 - sg)))
        gn = gn_ref[...].astype(F32)
        gcn = _conv_gate(gn, g[tl - HALO:, :], w_ref, b_ref)
        sgn = _sigmoid(gcn)
        dgcn = dan_ref[...].astype(F32) * vn_ref[...].astype(F32) * (sgn * (1.0 + gcn * (1.0 - sgn)))
        dgcn = dgcn * jnp.where(i < ni - 1, 1.0, 0.0)
        dgate = w_ref[2:3, :] * dgc + w_ref[1:2, :] * _shift_rows_up(dgc, 1, dgcn) + w_ref[0:1, :] * _shift_rows_up(dgc, 2, dgcn)
        dup_ref[:, :cw] = dgate.astype(BF16)
        dup_ref[:, cw:] = (dav * (gc * sg)).astype(BF16)
        zero = jnp.zeros((1, cw), F32)
        dw_ref[...] = jnp.concatenate(
            [jnp.sum(dgc * g2, axis=0, keepdims=True), jnp.sum(dgc * g1, axis=0, keepdims=True),
             jnp.sum(dgc * g, axis=0, keepdims=True)] + [zero] * 5, axis=0)
        db_ref[...] = jnp.concatenate([jnp.sum(dgc, axis=0, keepdims=True)] + [zero] * 7, axis=0)

    def cur(col):
        return pl.BlockSpec((tl, cw), lambda i, j, col=col: (i, 2 * j + col))

    def prev(col):
        return pl.BlockSpec((HALO, cw), lambda i, j, col=col: (jnp.maximum(i * hb - 1, 0), 2 * j + col))

    def nxt(col):
        return pl.BlockSpec((HALO, cw), lambda i, j, col=col: (jnp.minimum((i + 1) * hb, l // HALO - 1), 2 * j + col))

    part = jax.ShapeDtypeStruct((ni * 8, ff), F32)
    part_spec = pl.BlockSpec((8, cw), lambda i, j: (i, j))
    return pl.pallas_call(
        body, out_shape=(jax.ShapeDtypeStruct((l, 2 * ff), BF16), part, part), grid=(ni, nj),
        in_specs=[cur(0), prev(0), nxt(0), cur(1), nxt(1), pl.BlockSpec((tl, cw), lambda i, j: (i, j)),
                  pl.BlockSpec((HALO, cw), lambda i, j: (jnp.minimum((i + 1) * hb, l // HALO - 1), j)),
                  pl.BlockSpec((3, cw), lambda i, j: (0, j)), pl.BlockSpec((1, cw), lambda i, j: (0, j))],
        out_specs=(pl.BlockSpec((tl, 2 * cw), lambda i, j: (i, j)), part_spec, part_spec), name=name,
        compiler_params=_params(("parallel", "parallel")),
    )(up, up, up, up, up, da, da, conv_w, conv_b)


def _sum_rows8(parts, name):
    n8, w = parts.shape
    n = n8 // 8
    cw = _tile(w, 2048)

    def body(p_ref, o_ref):
        acc = p_ref[0:8, :]
        for k in range(1, n):
            acc = acc + p_ref[8 * k:8 * k + 8, :]
        o_ref[...] = acc

    return pl.pallas_call(body, out_shape=jax.ShapeDtypeStruct((8, w), F32), grid=(w // cw,),
                          in_specs=[pl.BlockSpec((n8, cw), lambda j: (0, j))], out_specs=pl.BlockSpec((8, cw), lambda j: (0, j)),
                          name=name, compiler_params=_params(("parallel",)))(parts)


def _ada_fwd(c_all, w_shard, b_shard, name):
    nb, d = c_all.shape
    n = w_shard.shape[1]
    tn = _tile(n, 512)

    def body(c_ref, w_ref, b_ref, o_ref):
        cv = c_ref[...]
        cond = (cv * _sigmoid(cv)).astype(BF16)
        o_ref[...] = jnp.dot(cond, w_ref[...].astype(BF16), preferred_element_type=F32) + b_ref[...]

    return pl.pallas_call(
        body, out_shape=jax.ShapeDtypeStruct((nb, n), F32), grid=(n // tn,),
        in_specs=[pl.BlockSpec((nb, d), lambda j: (0, 0)), pl.BlockSpec((d, tn), lambda j: (0, j)),
                  pl.BlockSpec((1, tn), lambda j: (0, j))],
        out_specs=pl.BlockSpec((nb, tn), lambda j: (0, j)), name=name, compiler_params=_params(("parallel",)),
    )(c_all, w_shard, b_shard)


def _adam_update(w, g, m, v):
    m2 = ADAM_B1 * m + (1.0 - ADAM_B1) * g
    v2 = ADAM_B2 * v + (1.0 - ADAM_B2) * (g * g)
    m_hat = m2 / (1.0 - ADAM_B1 ** ADAM_STEP)
    v_hat = v2 / (1.0 - ADAM_B2 ** ADAM_STEP)
    return -ADAM_LR * (m_hat / (jnp.sqrt(v_hat) + ADAM_EPS) + ADAM_WD * w), m2, v2


def _ada_bwd_adam(c_all_t, dmod_shard, w, m, v, name):
    d, nb = c_all_t.shape
    n = w.shape[1]
    tr, tn = _tile(d, 512, 8), _tile(n, 512)

    def body(c_ref, dm_ref, w_ref, m_ref, v_ref, g_ref, dl_ref, m2_ref, v2_ref):
        cv = c_ref[...]
        cond = cv * _sigmoid(cv)
        g = cond[:, 0:1] * dm_ref[0:1, :]
        for b in range(1, nb):
            g = g + cond[:, b:b + 1] * dm_ref[b:b + 1, :]
        g_ref[...] = g
        dl_ref[...], m2_ref[...], v2_ref[...] = _adam_update(w_ref[...], g, m_ref[...], v_ref[...])

    blk = pl.BlockSpec((tr, tn), lambda i, j: (i, j))
    out = jax.ShapeDtypeStruct((d, n), F32)
    return pl.pallas_call(
        body, out_shape=(out, out, out, out), grid=(d // tr, n // tn),
        in_specs=[pl.BlockSpec((tr, nb), lambda i, j: (i, 0)), pl.BlockSpec((nb, tn), lambda i, j: (0, j)), blk, blk, blk],
        out_specs=(blk, blk, blk, blk), name=name, compiler_params=_params(("parallel", "parallel")),
    )(c_all_t, dmod_shard, w, m, v)


def _adam(w, g, m, v, name):
    r, c = w.shape
    tr = _tile(r, 256, 8)

    def body(w_ref, g_ref, m_ref, v_ref, dl_ref, m2_ref, v2_ref):
        dl_ref[...], m2_ref[...], v2_ref[...] = _adam_update(w_ref[...], g_ref[...], m_ref[...], v_ref[...])

    blk = pl.BlockSpec((tr, c), lambda i: (i, 0))
    out = jax.ShapeDtypeStruct((r, c), F32)
    return pl.pallas_call(body, out_shape=(out, out, out), grid=(r // tr,), in_specs=[blk] * 4, out_specs=(blk,) * 3,
                          name=name, compiler_params=_params(("parallel",)))(w, g, m, v)


def _sum_devices(gathered, name):
    nd, r, c = gathered.shape
    tr = _tile(r, 64, 16)

    def body(g_ref, o_ref):
        acc = g_ref[0].astype(F32)
        for k in range(1, nd):
            acc = acc + g_ref[k].astype(F32)
        o_ref[...] = acc

    return pl.pallas_call(body, out_shape=jax.ShapeDtypeStruct((r, c), F32), grid=(r // tr,),
                          in_specs=[pl.BlockSpec((nd, tr, c), lambda i: (0, i, 0))], out_specs=pl.BlockSpec((tr, c), lambda i: (i, 0)),
                          name=name, compiler_params=_params(("parallel",)))(gathered)


def _place():
    x, y, c = lax.axis_index("x"), lax.axis_index("y"), lax.axis_index("c")
    chips = [(1 - x, y), (x, 1 - y), (1 - x, 1 - y)]
    return x, y, c, chips


def _all_gather8(block, name):
    m_per, n = block.shape

    def body(x_ref, out_ref, send_sems, recv_sems, local_sem):
        x, y, c, chips = _place()
        me, sibling = (x, y, c), (x, y, 1 - c)

        def rows(px, py, pc):
            return out_ref.at[pl.ds((4 * px + 2 * py + pc) * m_per, m_per), :]

        def copy(k, blk, to, src=None):
            return pltpu.make_async_remote_copy(
                src_ref=rows(*blk) if src is None else src, dst_ref=rows(*blk), send_sem=send_sems.at[k],
                recv_sem=recv_sems.at[k], device_id=to, device_id_type=MESH)

        mine = pltpu.make_async_copy(x_ref, rows(*me), local_sem)
        mine.start()
        first = [copy(0, me, sibling, src=x_ref)]
        first += [copy(1 + j, me, (*chip, c), src=x_ref) for j, chip in enumerate(chips)]
        for cp in first:
            cp.start()
        passed = [copy(4 + j, (*chip, c), sibling) for j, chip in enumerate(chips)]
        for j, chip in enumerate(chips):
            copy(1 + j, (*chip, c), me).wait_recv()
            passed[j].start()
        copy(0, sibling, me).wait_recv()
        for j, chip in enumerate(chips):
            copy(4 + j, (*chip, 1 - c), me).wait_recv()
        for cp in first + passed:
            cp.wait_send()
        mine.wait()

    return pl.pallas_call(
        body,
        out_shape=jax.ShapeDtypeStruct((N_DEV * m_per, n), block.dtype),
        in_specs=[pl.BlockSpec(memory_space=pltpu.VMEM)],
        out_specs=pl.BlockSpec(memory_space=pltpu.VMEM),
        scratch_shapes=[pltpu.SemaphoreType.DMA((7,)), pltpu.SemaphoreType.DMA((7,)), pltpu.SemaphoreType.DMA],
        name=name,
        compiler_params=pltpu.CompilerParams(vmem_limit_bytes=VMEM_LIMIT_BYTES),
    )(block)


ANY = pl.BlockSpec(memory_space=pl.ANY)


def _place_shard(shard, name, after=()):
    r, k = shard.shape
    tb = _tile(r, 512, 16)
    nb = r // tb
    chip = (2 * lax.axis_index("x") + lax.axis_index("y")).astype(jnp.int32).reshape(1)

    def body(j_ref, s_ref, *rest):
        rest[-1][...] = s_ref[...].astype(BF16)

    return pl.pallas_call(
        body, out_shape=jax.ShapeDtypeStruct((N_CHIPS * r, k), BF16),
        grid_spec=pltpu.PrefetchScalarGridSpec(
            num_scalar_prefetch=1, grid=(nb,),
            in_specs=[pl.BlockSpec((tb, k), lambda i, j_ref: (i, 0))] + [ANY] * len(after),
            out_specs=pl.BlockSpec((tb, k), lambda i, j_ref: (j_ref[0] * nb + i, 0))),
        name=name, compiler_params=_params(("parallel",)),
    )(chip, shard, *after)


HBM_SPEC = pl.BlockSpec(memory_space=pltpu.HBM)
SEM_SPEC = pl.BlockSpec(memory_space=pltpu.SEMAPHORE)
TOKEN_SPEC = pl.BlockSpec(memory_space=pltpu.VMEM)
SPLIT_COPY = pltpu.CompilerParams(has_side_effects=pltpu.SideEffectType.DATAFLOW_SIDE_EFFECTING)


def _in_hbm(arrays):
    return [pltpu.with_memory_space_constraint(a, pltpu.HBM) for a in arrays]


def _hbm_like(arrays):
    return tuple(pltpu.HBM(a.shape, a.dtype) for a in arrays)


def _token_shape():
    return jax.ShapeDtypeStruct((8, LANES), F32)


def _gathered_rows(buf, px, py, half):
    r = buf.shape[0] // N_CHIPS
    return buf.at[pl.ds(pl.multiple_of((2 * px + py) * r + half * (r // 2), 16), r // 2), :]


def _gather_start(groups, name):
    sizes = [len(g) for g in groups]
    flat = [b for g in groups for b in g]
    nb, ng = len(flat), len(groups)

    def body(*refs):
        bufs = refs[:nb]
        sems = refs[nb:nb + 2 * ng]
        token = refs[-1]
        x, y, c, chips = _place()
        pos = 0
        for gi, nw in enumerate(sizes):
            for k, chip in enumerate(chips):
                for w in range(nw):
                    mine = _gathered_rows(bufs[pos + w], x, y, c)
                    pltpu.make_async_remote_copy(src_ref=mine, dst_ref=mine, send_sem=sems[2 * gi].at[k * nw + w], recv_sem=sems[2 * gi + 1].at[k * nw + w],
                                                 device_id=(*chip, c), device_id_type=MESH).start()
            pos += nw
        token[...] = jnp.zeros_like(token)

    sem_shapes = tuple(pltpu.SemaphoreType.DMA((3 * n,)) for n in sizes for _ in range(2))
    outs = pl.pallas_call(
        body, name=name, out_shape=sem_shapes + _hbm_like(flat) + (_token_shape(),),
        in_specs=[HBM_SPEC] * nb, out_specs=(SEM_SPEC,) * (2 * ng) + (HBM_SPEC,) * nb + (TOKEN_SPEC,),
        input_output_aliases={i: 2 * ng + i for i in range(nb)}, compiler_params=SPLIT_COPY,
    )(*_in_hbm(flat))
    res, pos = [], 2 * ng
    for gi, n in enumerate(sizes):
        res.append((outs[2 * gi], outs[2 * gi + 1], list(outs[pos:pos + n])))
        pos += n
    return res, outs[-1]


def _gather_forward(bufs, ici_send, ici_recv, after, name):
    nw, na = len(bufs), len(after)

    def body(*refs):
        b = refs[:nw]
        isend, irecv = refs[nw], refs[nw + 1]
        dsend, drecv = refs[nw + 2 + na], refs[nw + 3 + na]
        x, y, c, chips = _place()
        for k, chip in enumerate(chips):
            for w in range(nw):
                landed = _gathered_rows(b[w], *chip, c)
                pltpu.make_async_remote_copy(src_ref=landed, dst_ref=landed, send_sem=isend.at[k * nw + w], recv_sem=irecv.at[k * nw + w],
                                             device_id=(*chip, c), device_id_type=MESH).wait_recv()
                pltpu.make_async_remote_copy(src_ref=landed, dst_ref=landed, send_sem=dsend.at[k * nw + w], recv_sem=drecv.at[k * nw + w],
                                             device_id=(x, y, 1 - c), device_id_type=MESH).start()
        for k, chip in enumerate(chips):
            for w in range(nw):
                mine = _gathered_rows(b[w], x, y, c)
                pltpu.make_async_remote_copy(src_ref=mine, dst_ref=mine, send_sem=isend.at[k * nw + w], recv_sem=irecv.at[k * nw + w],
                                             device_id=(*chip, c), device_id_type=MESH).wait_send()
        refs[-1][...] = jnp.zeros_like(refs[-1])

    sem = pltpu.SemaphoreType.DMA((3 * nw,))
    outs = pl.pallas_call(
        body, name=name, out_shape=(sem, sem) + _hbm_like(bufs) + (_token_shape(),),
        in_specs=[HBM_SPEC] * nw + [SEM_SPEC, SEM_SPEC] + [ANY] * na, out_specs=(SEM_SPEC, SEM_SPEC) + (HBM_SPEC,) * nw + (TOKEN_SPEC,),
        input_output_aliases={i: 2 + i for i in range(nw)}, compiler_params=SPLIT_COPY,
    )(*bufs, ici_send, ici_recv, *after)
    return outs[0], outs[1], list(outs[2:2 + nw]), outs[-1]


def _gather_finish(bufs, d2d_send, d2d_recv, name, after=()):
    nw = len(bufs)

    def body(*refs):
        b = refs[:nw]
        dsend, drecv = refs[nw], refs[nw + 1]
        x, y, c, chips = _place()
        for k, chip in enumerate(chips):
            for w in range(nw):
                theirs = _gathered_rows(b[w], *chip, 1 - c)
                pltpu.make_async_remote_copy(src_ref=theirs, dst_ref=theirs, send_sem=dsend.at[k * nw + w], recv_sem=drecv.at[k * nw + w],
                                             device_id=(x, y, 1 - c), device_id_type=MESH).wait_recv()
                passed = _gathered_rows(b[w], *chip, c)
                pltpu.make_async_remote_copy(src_ref=passed, dst_ref=passed, send_sem=dsend.at[k * nw + w], recv_sem=drecv.at[k * nw + w],
                                             device_id=(x, y, 1 - c), device_id_type=MESH).wait_send()

    outs = pl.pallas_call(
        body, name=name, out_shape=_hbm_like(bufs), in_specs=[HBM_SPEC] * nw + [SEM_SPEC, SEM_SPEC] + [ANY] * len(after),
        out_specs=(HBM_SPEC,) * nw, input_output_aliases={i: i for i in range(nw)}, compiler_params=SPLIT_COPY,
    )(*bufs, d2d_send, d2d_recv, *after)
    return list(outs)


def _swap_copies(src, land, send_sems, recv_sems):
    x, y, c, _ = _place()
    copies = []
    for w in range(len(src)):
        r = src[w].shape[0] // N_CHIPS
        h = r // 2
        for j in range(N_CHIPS):
            copies.append(pltpu.make_async_remote_copy(
                src_ref=src[w].at[pl.ds(pl.multiple_of(j * r + (1 - c) * h, 16), h), :], dst_ref=land[w].at[pl.ds(j * h, h), :],
                send_sem=send_sems.at[w * N_CHIPS + j], recv_sem=recv_sems.at[w * N_CHIPS + j], device_id=(x, y, 1 - c), device_id_type=MESH))
    return copies


def _swap_start(grads, name):
    nw = len(grads)
    landing = [lax.empty((g.shape[0] // 2, g.shape[1]), g.dtype) for g in grads]

    def body(*refs):
        for cp in _swap_copies(refs[:nw], refs[nw:2 * nw], refs[2 * nw], refs[2 * nw + 1]):
            cp.start()
        refs[-1][...] = jnp.zeros_like(refs[-1])

    sem = pltpu.SemaphoreType.DMA((N_CHIPS * nw,))
    outs = pl.pallas_call(
        body, name=name, out_shape=(sem, sem) + _hbm_like(grads) + _hbm_like(landing) + (_token_shape(),),
        in_specs=[HBM_SPEC] * (2 * nw), out_specs=(SEM_SPEC, SEM_SPEC) + (HBM_SPEC,) * (2 * nw) + (TOKEN_SPEC,),
        input_output_aliases={i: 2 + i for i in range(2 * nw)}, compiler_params=SPLIT_COPY,
    )(*_in_hbm(grads), *_in_hbm(landing))
    return (outs[0], outs[1], list(outs[2:2 + nw]), list(outs[2 + nw:2 + 2 * nw])), outs[-1]


def _swap_wait(started, after, name):
    send_sems, recv_sems, grads, landing = started
    nw = len(grads)

    def body(*refs):
        for cp in _swap_copies(refs[:nw], refs[nw:2 * nw], refs[2 * nw], refs[2 * nw + 1]):
            cp.wait_send()
            cp.wait_recv()

    outs = pl.pallas_call(
        body, name=name, out_shape=_hbm_like(grads) + _hbm_like(landing),
        in_specs=[HBM_SPEC] * (2 * nw) + [SEM_SPEC, SEM_SPEC] + [ANY] * len(after), out_specs=(HBM_SPEC,) * (2 * nw),
        input_output_aliases={i: i for i in range(2 * nw)}, compiler_params=SPLIT_COPY,
    )(*grads, *landing, send_sems, recv_sems, *after)
    return list(outs[:nw]), list(outs[nw:])


def _scatter_start(partials, name):
    nw = len(partials)
    landing = [lax.empty((3,) + p.shape[1:], p.dtype) for p in partials]

    def body(*refs):
        src, land = refs[:nw], refs[nw:2 * nw]
        send_sems, recv_sems = refs[2 * nw], refs[2 * nw + 1]
        token = refs[-1]
        x, y, c, chips = _place()
        for k, chip in enumerate(chips):
            for w in range(nw):
                pltpu.make_async_remote_copy(src_ref=src[w].at[2 * chip[0] + chip[1]], dst_ref=land[w].at[k], send_sem=send_sems.at[k * nw + w],
                                             recv_sem=recv_sems.at[k * nw + w], device_id=(*chip, c), device_id_type=MESH).start()
        token[...] = jnp.zeros_like(token)

    sem = pltpu.SemaphoreType.DMA((3 * nw,))
    outs = pl.pallas_call(
        body, name=name, out_shape=(sem, sem) + _hbm_like(partials) + _hbm_like(landing) + (_token_shape(),),
        in_specs=[HBM_SPEC] * (2 * nw), out_specs=(SEM_SPEC, SEM_SPEC) + (HBM_SPEC,) * (2 * nw) + (TOKEN_SPEC,),
        input_output_aliases={i: 2 + i for i in range(2 * nw)}, compiler_params=SPLIT_COPY,
    )(*_in_hbm(partials), *_in_hbm(landing))
    return (outs[0], outs[1], list(outs[2:2 + nw]), list(outs[2 + nw:2 + 2 * nw])), outs[-1]


def _scatter_wait(started, after, name):
    send_sems, recv_sems, partials, landing = started
    nw = len(partials)

    def body(*refs):
        src, land = refs[:nw], refs[nw:2 * nw]
        ssem, rsem = refs[2 * nw], refs[2 * nw + 1]
        x, y, c, chips = _place()
        for k, chip in enumerate(chips):
            for w in range(nw):
                cp = pltpu.make_async_remote_copy(src_ref=src[w].at[2 * chip[0] + chip[1]], dst_ref=land[w].at[k], send_sem=ssem.at[k * nw + w],
                                                  recv_sem=rsem.at[k * nw + w], device_id=(*chip, c), device_id_type=MESH)
                cp.wait_send()
                cp.wait_recv()

    outs = pl.pallas_call(
        body, name=name, out_shape=_hbm_like(partials) + _hbm_like(landing),
        in_specs=[HBM_SPEC] * (2 * nw) + [SEM_SPEC, SEM_SPEC] + [ANY] * len(after), out_specs=(HBM_SPEC,) * (2 * nw),
        input_output_aliases={i: i for i in range(2 * nw)}, compiler_params=SPLIT_COPY,
    )(*partials, *landing, send_sems, recv_sems, *after)
    return list(outs[:nw]), list(outs[nw:])


def _add_halves(grad, other, name):
    k = grad.shape[1]
    h = other.shape[0] // N_CHIPS
    tb = _tile(h, 512, 16)
    g4 = grad.reshape(N_CHIPS, 2, h, k)
    o3 = other.reshape(N_CHIPS, h, k)
    core = lax.axis_index("c").astype(jnp.int32).reshape(1)

    def body(c_ref, g_ref, o_ref, p_ref):
        p_ref[...] = (g_ref[...].astype(F32) + o_ref[...].astype(F32)).astype(BF16)

    return pl.pallas_call(
        body, out_shape=jax.ShapeDtypeStruct((N_CHIPS, h, k), BF16),
        grid_spec=pltpu.PrefetchScalarGridSpec(
            num_scalar_prefetch=1, grid=(N_CHIPS, h // tb),
            in_specs=[pl.BlockSpec((None, None, tb, k), lambda j, i, c_ref: (j, c_ref[0], i, 0)),
                      pl.BlockSpec((None, tb, k), lambda j, i, c_ref: (j, i, 0))],
            out_specs=pl.BlockSpec((None, tb, k), lambda j, i, c_ref: (j, i, 0))),
        name=name, compiler_params=_params(("parallel", "parallel")),
    )(core, g4, o3)


def _add_partials(partial, others, name):
    _, h, k = partial.shape
    tb = _tile(h, 512, 16)
    nb = h // tb
    place = jnp.stack([2 * lax.axis_index("x") + lax.axis_index("y"), lax.axis_index("c")]).astype(jnp.int32)

    def body(s_ref, p_ref, o0_ref, o1_ref, o2_ref, f_ref):
        f_ref[...] = ((p_ref[...].astype(F32) + o0_ref[...].astype(F32)) + o1_ref[...].astype(F32)) + o2_ref[...].astype(F32)

    def other(s):
        return pl.BlockSpec((None, tb, k), lambda i, s_ref, s=s: (s, i, 0))

    return pl.pallas_call(
        body, out_shape=jax.ShapeDtypeStruct((2 * h, k), F32),
        grid_spec=pltpu.PrefetchScalarGridSpec(
            num_scalar_prefetch=1, grid=(nb,),
            in_specs=[pl.BlockSpec((None, tb, k), lambda i, s_ref: (s_ref[0], i, 0)), other(0), other(1), other(2)],
            out_specs=pl.BlockSpec((tb, k), lambda i, s_ref: (s_ref[1] * nb + i, 0))),
        name=name, compiler_params=_params(("parallel",)),
    )(place, partial, others, others, others)


def _share_halves(fulls, name):
    nw = len(fulls)

    def body(*refs):
        ins, outs = refs[:nw], refs[nw:2 * nw]
        send_sems, recv_sems = refs[2 * nw:]
        x, y, c, _ = _place()
        copies = []
        for w in range(nw):
            h = fulls[w].shape[0] // 2
            start = pl.multiple_of(c * h, 8)
            copies.append(pltpu.make_async_remote_copy(
                src_ref=ins[w].at[pl.ds(start, h), :], dst_ref=outs[w].at[pl.ds(start, h), :], send_sem=send_sems.at[w],
                recv_sem=recv_sems.at[w], device_id=(x, y, 1 - c), device_id_type=MESH))
            copies[-1].start()
        for cp in copies:
            cp.wait()

    sem = pltpu.SemaphoreType.DMA((nw,))
    return pl.pallas_call(
        body, out_shape=tuple(jax.ShapeDtypeStruct(f.shape, f.dtype) for f in fulls),
        in_specs=[ANY] * nw, out_specs=(ANY,) * nw, scratch_shapes=[sem, sem], name=name,
        input_output_aliases={w: w for w in range(nw)},
    )(*fulls)


def _forward_then_finish(started_group, after, tag):
    ici_send, ici_recv, bufs = started_group
    d2d_send, d2d_recv, bufs, _ = _gather_forward(bufs, ici_send, ici_recv, after, f"gather_forward_{tag}")
    return _gather_finish(bufs, d2d_send, d2d_recv, f"gather_finish_{tag}")


def _reduce_start(swapping, tag, after=()):
    grads, from_sibling = _swap_wait(swapping, after, f"swap_wait_{tag}")
    chip_sums = [_add_halves(g, o, f"add_halves_{tag}_{i}") for i, (g, o) in enumerate(zip(grads, from_sibling))]
    return _scatter_start(chip_sums, f"scatter_start_{tag}")


def _reduce_finish(started, after, tag):
    chip_sums, from_chips = _scatter_wait(started, after, f"scatter_wait_{tag}")
    fulls = [_add_partials(p, o, f"add_partials_{tag}_{i}") for i, (p, o) in enumerate(zip(chip_sums, from_chips))]
    return _share_halves(fulls, f"share_halves_{tag}")


def _flatten_pad(parts, cols=SMALL_COLS):
    flat = jnp.concatenate([p.reshape(-1) for p in parts])
    rows = -(-flat.shape[0] // (16 * cols)) * 16
    return jnp.pad(flat, (0, rows * cols - flat.shape[0])).reshape(rows, cols)


def _split_flat(buf, shapes):
    flat = buf.reshape(-1)
    out, off = [], 0
    for s in shapes:
        n = math.prod(s)
        out.append(flat[off:off + n].reshape(s))
        off += n
    return out


def _ssm_setup(seq_len, ssm_a_re, ssm_a_im, ssm_log_dt, ssm_b_re, ssm_b_im, ssm_c_re, ssm_c_im):
    lam_r, lam_i, bbar_r, bbar_i = _ssm_discretize(ssm_a_re, ssm_a_im, ssm_log_dt, ssm_b_re, ssm_b_im)
    tab_f, tab_b = _scan_tables(lam_r, lam_i, seq_len // N_SEG)
    pk = {"br": _pack_in(bbar_r), "bi": _pack_in(bbar_i), "cr": _pack_out(ssm_c_re), "ci": _pack_out(ssm_c_im)}
    packs = {k: v.astype(BF16) for k, v in pk.items()}
    packs.update({"brt": jnp.swapaxes(packs["br"], 1, 2), "bit": jnp.swapaxes(packs["bi"], 1, 2),
                  "crt": jnp.swapaxes(packs["cr"], 1, 2), "cit": jnp.swapaxes(packs["ci"], 1, 2)})
    return packs, tab_f, tab_b


def _local_step(xs, target, mod, w_in_t, comm, norm_mix_g, attn_sinks, ssm, ssm_d, norm_ffn_g, conv_w_full, ffn_conv_b, final_g,
                aw, sw, ff):
    l, d = xs.shape
    u_off = aw + 2 * KV_WIDTH
    ga_off = u_off + sw
    gs_off = ga_off + d
    packs, tab_f, tab_b = ssm
    dvec = ssm_d.reshape(1, sw)

    h1 = _norm_mod(xs, norm_mix_g, mod, 1, 0, "norm_mod1")
    proj = _matmul(h1, w_in_t, "nt", "mm_in")
    attn = _attn_fwd(proj, attn_sinks, aw, "attn_fwd", after=(comm["mixer_arrived"]((proj,)),))
    u_il = _interleave(proj[:, u_off:u_off + sw])
    ys_il, gy_il = _s5_fwd(u_il, 0, packs, dvec, tab_f, sw, "s5_fwd")
    gy = _deinterleave(gy_il)
    (w_ap_t, w_glu_t, w_out_f), ffn_weights = comm["later_weights"]((gy, attn))
    attn_out = _matmul(attn, w_ap_t, "nt", "mm_attn_proj")
    glu = _matmul(gy, w_glu_t, "nt", "mm_glu")
    g_attn, g_ssm = proj[:, ga_off:ga_off + d], proj[:, gs_off:gs_off + d]
    mixed = _mix(g_attn, g_ssm, attn_out, glu, "mix")
    mo = _matmul(mixed, w_out_f, "nn", "mm_out", out_dtype=F32)
    x2, h2 = _resid_norm_mod(xs, mo, norm_ffn_g, mod, 2, 4, 3, "resid_norm_mod2")
    w_up_t, w_down_f = ffn_weights((h2,))
    cw = ff // CONV_BLOCKS
    up = _matmul(h2, w_up_t, "nt", "mm_up", interleave=cw)
    act = _conv_act(up, conv_w_full, ffn_conv_b, ff, cw, "conv_act")
    fo = _matmul(act, w_down_f, "nn", "mm_down", out_dtype=F32)
    loss_part, d_final_g, d_gate2, dx3, dfo = _final_loss(x2, fo, mod, 5, final_g.reshape(1, d), target, "final_loss")

    dact = _matmul(dfo, w_down_f, "nt", "mm_down_dx")
    g_down = _matmul(act, dfo, "tn", "mm_down_dw")
    dup, dcw_parts, dcb_parts = _conv_act_bwd(up, dact, conv_w_full, ffn_conv_b, ff, cw, "conv_act_bwd")
    d_conv_w = _sum_rows8(dcw_parts, "sum_conv_w")[:3]
    d_conv_b = _sum_rows8(dcb_parts, "sum_conv_b")[:1]
    g_up = _matmul(dup, h2, "tn", "mm_up_dw", interleave=cw)
    dh2 = _matmul(dup, w_up_t, "nn", "mm_up_dx", interleave=cw, after=(comm["grads_started"]("ffn", [g_up, g_down]),))
    mod = comm["ffn_grads_ready"](mod, (dh2,))
    dx2, d_shift2, d_scale2, d_gain2, dmo, d_gate1 = _norm_mod_bwd(dh2, x2, dx3, norm_ffn_g, mod, 4, "norm_mod2_bwd", branch=mo, gate_col=2)
    dmixed = _matmul(dmo, w_out_f, "nt", "mm_out_dx")
    g_out = _matmul(mixed, dmo, "tn", "mm_out_dw")
    dga, dgs, dattn_out, dglu = _mix_bwd(g_attn, g_ssm, attn_out, glu, dmixed, "mix_bwd")
    dgy = _matmul(dglu, w_glu_t, "nn", "mm_glu_dx")
    g_glu = _matmul(dglu, gy, "tn", "mm_glu_dw")
    du_il, dlam, dbr_p, dbi_p, dcr_p, dci_p, d_dvec = _s5_bwd(u_il, 0, ys_il, _interleave(dgy), packs, dvec, tab_f, tab_b, sw, "s5_bwd")
    du = _deinterleave(du_il)
    dattn = _matmul(dattn_out, w_ap_t, "nn", "mm_attn_proj_dx")
    g_ap = _matmul(dattn_out, attn, "tn", "mm_attn_proj_dw")
    dq, dkv_cur, dkv_prev, d_sinks = _attn_bwd(proj, attn_sinks, dattn, aw, "attn_bwd")
    dkv = dkv_cur + jnp.concatenate([dkv_prev[ATTN_BLOCK:], jnp.zeros((ATTN_BLOCK, 2 * KV_WIDTH), F32)], axis=0)
    dproj = jnp.concatenate([dq, dkv.astype(BF16), du, dga, dgs], axis=1)
    g_in = _matmul(dproj, h1, "tn", "mm_in_dw")
    dh1 = _matmul(dproj, w_in_t, "nn", "mm_in_dx", after=(comm["grads_started"]("rest", [g_in, g_ap, g_glu, g_out]),))
    grad_x, d_shift1, d_scale1, d_gain1 = _norm_mod_bwd(dh1, xs, dx2, norm_mix_g, mod, 1, "norm_mod1_bwd")

    dmod = jnp.concatenate([d_shift1, d_scale1, d_gate1, d_shift2, d_scale2, d_gate2], axis=1)
    small_parts = [dmod, d_gain1, d_sinks, dlam[0], dlam[1], _unpack_diag(dbr_p, SSM_STATE, SSM_GROUP),
                   _unpack_diag(dbi_p, SSM_STATE, SSM_GROUP), _unpack_diag(dcr_p, SSM_STATE, SSM_GROUP),
                   _unpack_diag(dci_p, SSM_STATE, SSM_GROUP), d_dvec, d_gain2, d_conv_b, d_conv_w, d_final_g]
    return loss_part, grad_x, small_parts


def _kernel_impl(x, c, ada_w, ada_b, norm_mix_g, w_in, attn_sinks, w_attn_proj, ssm_a_re, ssm_a_im, ssm_log_dt, ssm_b_re, ssm_b_im,
                 ssm_c_re, ssm_c_im, ssm_d, w_ssm_glu, w_out, norm_ffn_g, w_ffn_up, ffn_conv_w, ffn_conv_b, w_ffn_down, final_g,
                 loss_target, ms, vs):
    ax, ay, ac = lax.axis_index("x"), lax.axis_index("y"), lax.axis_index("c")
    chip = 2 * ax + ay
    batch_row = 4 * ax + 2 * ay + ac
    d = x.shape[2]
    aw = w_attn_proj.shape[1]
    sw = w_ssm_glu.shape[1]
    ff = N_CHIPS * ffn_conv_w.shape[2]
    ngroups = sw // SSM_GROUP

    c_all = _all_gather8(jnp.pad(c, ((0, 7), (0, 0))), "gather_c").reshape(N_DEV, 8, d)[:, 0, :]
    ncol = ada_w.shape[2]
    b_shard = lax.dynamic_slice(ada_b, (0, chip * ncol), (1, ncol))
    mod_blk = _ada_fwd(c_all, ada_w[0], b_shard, "ada_fwd")
    mod_all = _all_gather8(mod_blk, "gather_mod").reshape(N_CHIPS, 2, 8, ncol)[:, 0]
    mod = lax.dynamic_slice(mod_all, (0, batch_row, 0), (N_CHIPS, 1, ncol)).reshape(1, 6 * d)

    conv_w_all = _all_gather8(jnp.pad(ffn_conv_w[0], ((0, 5), (0, 0))), "gather_conv_w")
    conv_w_full = conv_w_all.reshape(N_CHIPS, 2, 8, ff // N_CHIPS)[:, 0, :3].transpose(1, 0, 2).reshape(3, ff)
    placed_in = _place_shard(w_in[0].T.astype(BF16), "place_shard_0", after=(mod, conv_w_full))
    (first,), started_in = _gather_start([[placed_in]], "gather_start_w_in")
    shards = [w_attn_proj[0].T.astype(BF16), w_ssm_glu[0].T.astype(BF16), w_out[0], w_ffn_up[0].T.astype(BF16), w_ffn_down[0]]
    placed = [_place_shard(s, f"place_shard_{i + 1}", after=(started_in,)) for i, s in enumerate(shards)]
    (mixer, ffn), started = _gather_start([placed[:3], placed[3:]], "gather_start_rest")
    ssm = (ssm_a_re[0], ssm_a_im[0], ssm_log_dt[0], ssm_b_re[0], ssm_b_im[0], ssm_c_re[0], ssm_c_im[0], ssm_d[0])
    ssm_tables = _ssm_setup(x.shape[1], *ssm[:7])
    (w_in_t,) = _forward_then_finish(first, (started, ssm_tables[1], ssm_tables[2], *ssm_tables[0].values()), "w_in")
    mod = mod + (started_in[0:1, 0:1] + started[0:1, 0:1])

    pending = {}

    def mixer_arrived(after):
        pending["mixer"] = _gather_forward(mixer[2], mixer[0], mixer[1], after, "gather_forward_mixer")
        return pending["mixer"][3]

    def later_weights(after):
        m_send, m_recv, m_bufs, _ = pending["mixer"]
        f_send, f_recv, f_bufs, f_started = _gather_forward(ffn[2], ffn[0], ffn[1], after, "gather_forward_ffn")
        mixer_weights = _gather_finish(m_bufs, m_send, m_recv, "gather_finish_mixer", (f_started,))
        return mixer_weights, lambda later: _gather_finish(f_bufs, f_send, f_recv, "gather_finish_ffn", later)

    def grads_started(tag, grads):
        pending["swap_" + tag], token = _swap_start(grads, f"swap_start_{tag}")
        return token

    def ffn_grads_ready(mod_now, after):
        pending["ffn"], token = _reduce_start(pending["swap_ffn"], "ffn", after)
        return mod_now + token[0:1, 0:1]

    comm = {"mixer_arrived": mixer_arrived, "later_weights": later_weights, "grads_started": grads_started,
            "ffn_grads_ready": ffn_grads_ready}
    loss_part, grad_x, small_parts = _local_step(
        x[0], loss_target[0], mod, w_in_t, comm, norm_mix_g, attn_sinks, ssm_tables, ssm[7], norm_ffn_g, conv_w_full, ffn_conv_b,
        final_g, aw, sw, ff)
    loss = lax.psum(loss_part[0, 0], ("x", "y", "c"))

    small_shapes = [p.shape for p in small_parts]
    part_buf = _flatten_pad(small_parts).astype(BF16)
    rows = part_buf.shape[0]
    gathered = _all_gather8(part_buf, "gather_small").reshape(N_DEV, rows, SMALL_COLS)
    pending["rest"], rest_token = _reduce_start(pending["swap_rest"], "rest", after=(gathered, grad_x))
    gup_t, grad_w_down = _reduce_finish(pending["ffn"], (rest_token,), "ffn")
    grad_w_up = gup_t.T
    summed = _sum_devices(gathered, "sum_small")
    (s_dmod, s_gain1, s_sinks, s_lr, s_li, s_bbr, s_bbi, s_cr, s_ci, s_dd, s_gain2, s_cb, s_cw, s_fg) = _split_flat(summed, small_shapes)
    _, ssm_vjp = jax.vjp(_ssm_discretize, *ssm[:5])
    g_a_re, g_a_im, g_log_dt, g_b_re, g_b_im = ssm_vjp((s_lr.reshape(ngroups, SSM_STATE), s_li.reshape(ngroups, SSM_STATE), s_bbr, s_bbi))
    g_c_re, g_c_im = jnp.swapaxes(s_cr, 1, 2), jnp.swapaxes(s_ci, 1, 2)
    g_conv_w = lax.dynamic_slice(s_cw, (0, chip * (ff // N_CHIPS)), (3, ff // N_CHIPS))

    dmod_all = gathered.reshape(N_DEV, -1)[:, :6 * d].astype(F32)
    dmod_shard = lax.dynamic_slice(dmod_all, (0, chip * ncol), (N_DEV, ncol))
    ada_res = _ada_bwd_adam(c_all.T, dmod_shard, ada_w[0], ms["ada_w"][0], vs["ada_w"][0], "ada_bwd_adam")

    res = {"ada_w": tuple(o[None] for o in ada_res)}

    def adam_big(nm, w, g):
        res[nm] = (g[None],) + tuple(o[None] for o in _adam(w[0], g, ms[nm][0], vs[nm][0], "adam_" + nm))

    adam_big("w_ffn_up", w_ffn_up, grad_w_up)
    adam_big("w_ffn_down", w_ffn_down, grad_w_down)

    small = [("ada_b", ada_b, s_dmod), ("norm_mix_g", norm_mix_g, s_gain1), ("attn_sinks", attn_sinks, s_sinks),
             ("ssm_a_re", ssm_a_re, g_a_re), ("ssm_a_im", ssm_a_im, g_a_im), ("ssm_log_dt", ssm_log_dt, g_log_dt),
             ("ssm_b_re", ssm_b_re, g_b_re), ("ssm_b_im", ssm_b_im, g_b_im), ("ssm_c_re", ssm_c_re, g_c_re),
             ("ssm_c_im", ssm_c_im, g_c_im), ("ssm_d", ssm_d, s_dd), ("norm_ffn_g", norm_ffn_g, s_gain2),
             ("ffn_conv_w", ffn_conv_w, g_conv_w), ("ffn_conv_b", ffn_conv_b, s_cb), ("final_g", final_g, s_fg)]
    shapes = [t[1].shape for t in small]
    bufs = [_flatten_pad([t[1] for t in small]), _flatten_pad([t[2] for t in small]),
            _flatten_pad([ms[t[0]] for t in small]), _flatten_pad([vs[t[0]] for t in small])]
    s_delta, s_m, s_v = _adam(*bufs, "adam_small")
    for t, dl, m2, v2 in zip(small, _split_flat(s_delta, shapes), _split_flat(s_m, shapes), _split_flat(s_v, shapes)):
        res[t[0]] = (t[2].reshape(t[1].shape), dl, m2, v2)

    done = (s_delta, res["w_ffn_up"][1], res["w_ffn_down"][1], res["ada_w"][1])
    gi_t, gap_t, gglu_t, grad_w_out = _reduce_finish(pending["rest"], done, "rest")
    adam_big("w_in", w_in, gi_t.T)
    adam_big("w_attn_proj", w_attn_proj, gap_t.T)
    adam_big("w_ssm_glu", w_ssm_glu, gglu_t.T)
    adam_big("w_out", w_out, grad_w_out)

    outs = [loss, grad_x[None]]
    for i in range(4):
        outs += [res[nm][i] for nm in WEIGHT_ORDER]
    return tuple(outs)


WEIGHT_ORDER = ("ada_w", "ada_b", "norm_mix_g", "w_in", "attn_sinks", "w_attn_proj", "ssm_a_re", "ssm_a_im", "ssm_log_dt", "ssm_b_re",
                "ssm_b_im", "ssm_c_re", "ssm_c_im", "ssm_d", "w_ssm_glu", "w_out", "norm_ffn_g", "w_ffn_up", "ffn_conv_w", "ffn_conv_b",
                "w_ffn_down", "final_g")


def kernel(x, c, ada_w, ada_b, norm_mix_g, w_in, attn_sinks, w_attn_proj, ssm_a_re, ssm_a_im, ssm_log_dt, ssm_b_re, ssm_b_im, ssm_c_re, ssm_c_im, ssm_d, w_ssm_glu, w_out, norm_ffn_g, w_ffn_up, ffn_conv_w, ffn_conv_b, w_ffn_down, final_g, loss_target, m_ada_w, m_ada_b, m_norm_mix_g, m_w_in, m_attn_sinks, m_w_attn_proj, m_ssm_a_re, m_ssm_a_im, m_ssm_log_dt, m_ssm_b_re, m_ssm_b_im, m_ssm_c_re, m_ssm_c_im, m_ssm_d, m_w_ssm_glu, m_w_out, m_norm_ffn_g, m_w_ffn_up, m_ffn_conv_w, m_ffn_conv_b, m_w_ffn_down, m_final_g, v_ada_w, v_ada_b, v_norm_mix_g, v_w_in, v_attn_sinks, v_w_attn_proj, v_ssm_a_re, v_ssm_a_im, v_ssm_log_dt, v_ssm_b_re, v_ssm_b_im, v_ssm_c_re, v_ssm_c_im, v_ssm_d, v_w_ssm_glu, v_w_out, v_norm_ffn_g, v_w_ffn_up, v_ffn_conv_w, v_ffn_conv_b, v_w_ffn_down, v_final_g):
    ms = dict(zip(WEIGHT_ORDER, (m_ada_w, m_ada_b, m_norm_mix_g, m_w_in, m_attn_sinks, m_w_attn_proj, m_ssm_a_re, m_ssm_a_im, m_ssm_log_dt,
                                 m_ssm_b_re, m_ssm_b_im, m_ssm_c_re, m_ssm_c_im, m_ssm_d, m_w_ssm_glu, m_w_out, m_norm_ffn_g, m_w_ffn_up,
                                 m_ffn_conv_w, m_ffn_conv_b, m_w_ffn_down, m_final_g)))
    vs = dict(zip(WEIGHT_ORDER, (v_ada_w, v_ada_b, v_norm_mix_g, v_w_in, v_attn_sinks, v_w_attn_proj, v_ssm_a_re, v_ssm_a_im, v_ssm_log_dt,
                                 v_ssm_b_re, v_ssm_b_im, v_ssm_c_re, v_ssm_c_im, v_ssm_d, v_w_ssm_glu, v_w_out, v_norm_ffn_g, v_w_ffn_up,
                                 v_ffn_conv_w, v_ffn_conv_b, v_w_ffn_down, v_final_g)))
    return _kernel_impl(x, c, ada_w, ada_b, norm_mix_g, w_in, attn_sinks, w_attn_proj, ssm_a_re, ssm_a_im, ssm_log_dt, ssm_b_re, ssm_b_im,
                        ssm_c_re, ssm_c_im, ssm_d, w_ssm_glu, w_out, norm_ffn_g, w_ffn_up, ffn_conv_w, ffn_conv_b, w_ffn_down, final_g,
                        loss_target, ms, vs)
```

```python
import math

import jax
import jax.numpy as jnp
from jax import lax
from jax.experimental import pallas as pl
from jax.experimental.pallas import tpu as pltpu

F32 = jnp.float32
BF16 = jnp.bfloat16
MESH = pl.DeviceIdType.MESH

HEAD_DIM = 64
N_KV_HEADS = 2
KV_WIDTH = N_KV_HEADS * HEAD_DIM
ATTN_BLOCK = 128
NEG_INF = -1e30
SSM_GROUP = 16
SSM_STATE = 64
GROUPS_PER_TILE = 8
RMS_EPS = 1e-6
ADAM_LR = 0.001
ADAM_B1 = 0.9
ADAM_B2 = 0.999
ADAM_EPS = 1e-08
ADAM_WD = 0.01
ADAM_STEP = 10
N_CHIPS = 4
N_DEV = 8
VMEM_LIMIT_BYTES = 56 * 1024 * 1024
LANES = 128
SMALL_COLS = 1024


def _tile(dim, target, mult=LANES):
    if dim <= target:
        return dim
    for t in range(target // mult * mult, 0, -mult):
        if dim % t == 0:
            return t
    raise ValueError(f"no tile for {dim}")


def _params(sem=None):
    return pltpu.CompilerParams(dimension_semantics=sem, vmem_limit_bytes=VMEM_LIMIT_BYTES)


def _sigmoid(x):
    return 1.0 / (1.0 + jnp.exp(-x))


def _matmul(a, b, mode, name, out_dtype=BF16, tm=1536, tn=1536, tk=2048, interleave=None, after=()):
    if mode == "nn":
        (m, k), (k2, n) = a.shape, b.shape
    elif mode == "nt":
        (m, k), (n, k2) = a.shape, b.shape
    else:
        (k, m), (k2, n) = a.shape, b.shape
    assert k == k2, (a.shape, b.shape, mode)
    if interleave is not None:
        tn, tk, tm = (interleave, tk, tm) if mode == "nt" else (tn, interleave, tm) if mode == "nn" else (tn, tk, interleave)
        half = {"nt": n, "nn": k, "tn": m}[mode] // (2 * interleave)

        def perm(blk):
            return blk // 2 + (blk % 2) * half
    else:
        def perm(blk):
            return blk
        if mode == "tn":
            tm, tk = min(tm, 1024), max(tk, 4096)
    tm, tn, tk = _tile(m, tm), _tile(n, tn), _tile(k, tk)
    nk = k // tk
    if mode == "tn":
        a_spec = pl.BlockSpec((tk, tm), lambda i, j, kk: (kk, i))
    else:
        a_spec = pl.BlockSpec((tm, tk), lambda i, j, kk: (i, kk))
    if mode == "nt":
        b_spec = pl.BlockSpec((tn, tk), lambda i, j, kk: (perm(j), kk))
    elif mode == "nn":
        b_spec = pl.BlockSpec((tk, tn), lambda i, j, kk: (perm(kk), j))
    else:
        b_spec = pl.BlockSpec((tk, tn), lambda i, j, kk: (kk, j))
    out_rows = perm if mode == "tn" else (lambda blk: blk)
    dims = {"nn": (((1,), (0,)), ((), ())), "nt": (((1,), (1,)), ((), ())), "tn": (((0,), (0,)), ((), ()))}[mode]

    def body(a_ref, b_ref, *rest):
        o_ref, acc_ref = rest[-2:]
        kk = pl.program_id(2)

        @pl.when(kk == 0)
        def _():
            acc_ref[...] = jnp.zeros_like(acc_ref)

        acc_ref[...] += lax.dot_general(a_ref[...], b_ref[...], dims, preferred_element_type=F32)

        @pl.when(kk == nk - 1)
        def _():
            o_ref[...] = acc_ref[...].astype(o_ref.dtype)

    return pl.pallas_call(
        body,
        out_shape=jax.ShapeDtypeStruct((m, n), out_dtype),
        grid=(m // tm, n // tn, nk),
        in_specs=[a_spec, b_spec] + [pl.BlockSpec(memory_space=pl.ANY)] * len(after),
        out_specs=pl.BlockSpec((tm, tn), lambda i, j, kk: (out_rows(i), j)),
        scratch_shapes=[pltpu.VMEM((tm, tn), F32)],
        name=name,
        compiler_params=_params(("parallel", "parallel", "arbitrary")),
    )(a, b, *after)


def _row_spec(tl, w, col=0):
    return pl.BlockSpec((tl, w), lambda i, col=col: (i, col))


def _vec_spec(w, col=0):
    return pl.BlockSpec((1, w), lambda i, col=col: (0, col))


def _norm_mod(x, gain, mod, sc_col, sh_col, name):
    l, d = x.shape
    tl = _tile(l, 256, 8)

    def body(x_ref, g_ref, sc_ref, sh_ref, h_ref):
        xv = x_ref[...]
        r = lax.rsqrt(jnp.mean(xv * xv, axis=-1, keepdims=True) + RMS_EPS)
        h_ref[...] = ((xv * r) * g_ref[...] * (1.0 + sc_ref[...]) + sh_ref[...]).astype(BF16)

    return pl.pallas_call(
        body,
        out_shape=jax.ShapeDtypeStruct((l, d), BF16),
        grid=(l // tl,),
        in_specs=[_row_spec(tl, d), _vec_spec(d), _vec_spec(d, sc_col), _vec_spec(d, sh_col)],
        out_specs=_row_spec(tl, d),
        name=name,
        compiler_params=_params(("parallel",)),
    )(x, gain, mod, mod)


def _resid_norm_mod(x, mo, gain, mod, gate_col, sc_col, sh_col, name):
    l, d = x.shape
    tl = _tile(l, 256, 8)

    def body(x_ref, mo_ref, g_ref, gate_ref, sc_ref, sh_ref, x2_ref, h_ref):
        xv = x_ref[...] + gate_ref[...] * mo_ref[...]
        x2_ref[...] = xv
        r = lax.rsqrt(jnp.mean(xv * xv, axis=-1, keepdims=True) + RMS_EPS)
        h_ref[...] = ((xv * r) * g_ref[...] * (1.0 + sc_ref[...]) + sh_ref[...]).astype(BF16)

    return pl.pallas_call(
        body,
        out_shape=(jax.ShapeDtypeStruct((l, d), F32), jax.ShapeDtypeStruct((l, d), BF16)),
        grid=(l // tl,),
        in_specs=[_row_spec(tl, d), _row_spec(tl, d), _vec_spec(d), _vec_spec(d, gate_col), _vec_spec(d, sc_col),
                  _vec_spec(d, sh_col)],
        out_specs=(_row_spec(tl, d), _row_spec(tl, d)),
        name=name,
        compiler_params=_params(("parallel",)),
    )(x, mo, gain, mod, mod, mod)


def _final_loss(x2, f, mod, gate_col, final_g, target, name):
    l, d = x2.shape
    tl = _tile(l, 256, 8)

    def body(x2_ref, f_ref, gate_ref, fg_ref, t_ref, loss_ref, dfg_ref, dgate_ref, dx3_ref, df_ref):
        i = pl.program_id(0)
        fv = f_ref[...]
        x3 = x2_ref[...] + gate_ref[...] * fv
        r = lax.rsqrt(jnp.mean(x3 * x3, axis=-1, keepdims=True) + RMS_EPS)
        xh = x3 * r
        err = xh * fg_ref[...] - t_ref[...]
        part = 0.5 * jnp.sum(jnp.mean(err * err, axis=-1, keepdims=True), axis=0, keepdims=True)
        dout = err * (1.0 / d)
        dxh = dout * fg_ref[...]
        dx3 = r * (dxh - xh * jnp.mean(dxh * xh, axis=-1, keepdims=True))
        dx3_ref[...] = dx3
        df_ref[...] = (gate_ref[...] * dx3).astype(BF16)

        @pl.when(i == 0)
        def _():
            loss_ref[...] = jnp.zeros_like(loss_ref)
            dfg_ref[...] = jnp.zeros_like(dfg_ref)
            dgate_ref[...] = jnp.zeros_like(dgate_ref)

        loss_ref[...] += jnp.broadcast_to(part, loss_ref.shape)
        dfg_ref[...] += jnp.sum(dout * xh, axis=0, keepdims=True)
        dgate_ref[...] += jnp.sum(dx3 * fv, axis=0, keepdims=True)

    vec = pl.BlockSpec((1, d), lambda i: (0, 0))
    return pl.pallas_call(
        body,
        out_shape=(jax.ShapeDtypeStruct((1, LANES), F32), jax.ShapeDtypeStruct((1, d), F32),
                   jax.ShapeDtypeStruct((1, d), F32), jax.ShapeDtypeStruct((l, d), F32),
                   jax.ShapeDtypeStruct((l, d), BF16)),
        grid=(l // tl,),
        in_specs=[_row_spec(tl, d), _row_spec(tl, d), _vec_spec(d, gate_col), vec, _row_spec(tl, d)],
        out_specs=(pl.BlockSpec((1, LANES), lambda i: (0, 0)), vec, vec, _row_spec(tl, d), _row_spec(tl, d)),
        name=name,
        compiler_params=_params(("arbitrary",)),
    )(x2, f, mod, final_g, target)


def _norm_mod_bwd(dh, x, dx_res, gain, mod, sc_col, name, branch=None, gate_col=None):
    l, d = x.shape
    tl = _tile(l, 256, 8)
    with_gate = branch is not None

    def body(*refs):
        if with_gate:
            dh_ref, x_ref, dr_ref, g_ref, sc_ref, br_ref, gate_ref, dx_ref, dsh_ref, dsc_ref, dg_ref, dm_ref, dgate_ref = refs
        else:
            dh_ref, x_ref, dr_ref, g_ref, sc_ref, dx_ref, dsh_ref, dsc_ref, dg_ref = refs
        i = pl.program_id(0)
        xv = x_ref[...]
        dhv = dh_ref[...].astype(F32)
        r = lax.rsqrt(jnp.mean(xv * xv, axis=-1, keepdims=True) + RMS_EPS)
        xh = xv * r
        dn = dhv * (1.0 + sc_ref[...])
        dxh = dn * g_ref[...]
        dx = dr_ref[...] + r * (dxh - xh * jnp.mean(dxh * xh, axis=-1, keepdims=True))
        dx_ref[...] = dx

        @pl.when(i == 0)
        def _():
            dsh_ref[...] = jnp.zeros_like(dsh_ref)
            dsc_ref[...] = jnp.zeros_like(dsc_ref)
            dg_ref[...] = jnp.zeros_like(dg_ref)
            if with_gate:
                dgate_ref[...] = jnp.zeros_like(dgate_ref)

        dsh_ref[...] += jnp.sum(dhv, axis=0, keepdims=True)
        dsc_ref[...] += jnp.sum(dhv * (xh * g_ref[...]), axis=0, keepdims=True)
        dg_ref[...] += jnp.sum(dn * xh, axis=0, keepdims=True)
        if with_gate:
            dm_ref[...] = (gate_ref[...] * dx).astype(BF16)
            dgate_ref[...] += jnp.sum(dx * br_ref[...], axis=0, keepdims=True)

    vec = pl.BlockSpec((1, d), lambda i: (0, 0))
    in_specs = [_row_spec(tl, d), _row_spec(tl, d), _row_spec(tl, d), vec, _vec_spec(d, sc_col)]
    args = [dh, x, dx_res, gain, mod]
    out_shape = [jax.ShapeDtypeStruct((l, d), F32)] + [jax.ShapeDtypeStruct((1, d), F32)] * 3
    out_specs = [_row_spec(tl, d), vec, vec, vec]
    if with_gate:
        in_specs += [_row_spec(tl, d), _vec_spec(d, gate_col)]
        args += [branch, mod]
        out_shape += [jax.ShapeDtypeStruct((l, d), BF16), jax.ShapeDtypeStruct((1, d), F32)]
        out_specs += [_row_spec(tl, d), vec]
    return pl.pallas_call(
        body, out_shape=tuple(out_shape), grid=(l // tl,), in_specs=in_specs, out_specs=tuple(out_specs),
        name=name, compiler_params=_params(("arbitrary",)),
    )(*args)


def _attn_mask(n, rows):
    del rows
    qi = lax.broadcasted_iota(jnp.int32, (ATTN_BLOCK, 2 * ATTN_BLOCK), 0)
    kj = lax.broadcasted_iota(jnp.int32, (ATTN_BLOCK, 2 * ATTN_BLOCK), 1)
    rel = qi + ATTN_BLOCK - kj
    return jnp.where((rel >= 0) & (rel < ATTN_BLOCK) & ((kj >= ATTN_BLOCK) | (n > 0)), 0.0, NEG_INF)


def _attn_probs(qs, kh, sink, mask):
    rows = qs.shape[0]
    s = lax.dot_general(qs, kh, (((1,), (1,)), ((), ())), preferred_element_type=F32) * (HEAD_DIM ** -0.5)
    s = s.reshape(-1, ATTN_BLOCK, 2 * ATTN_BLOCK) + mask[None]
    m = jnp.maximum(jnp.max(s, axis=-1, keepdims=True), sink)
    p = jnp.exp(s - m)
    es = jnp.exp(sink - m)
    inv = 1.0 / (jnp.sum(p, axis=-1, keepdims=True) + es)
    return (p * inv).reshape(rows, 2 * ATTN_BLOCK), (es * inv).reshape(rows, 1)


def _stack_heads(src_ref, dst_ref, g, qpk):
    for i in range(qpk):
        h = g * qpk + i
        dst_ref[i * ATTN_BLOCK:(i + 1) * ATTN_BLOCK, :] = src_ref[:, h * HEAD_DIM:(h + 1) * HEAD_DIM]


def _unstack_heads(val, dst_ref, g, qpk):
    for i in range(qpk):
        h = g * qpk + i
        dst_ref[:, h * HEAD_DIM:(h + 1) * HEAD_DIM] = val[i * ATTN_BLOCK:(i + 1) * ATTN_BLOCK, :].astype(dst_ref.dtype)


def _sink_column(sinks):
    return sinks.reshape(-1, 1, 1)


def _sink_spec(nq):
    return pl.BlockSpec((nq, 1, 1), lambda n: (0, 0, 0))


def _attn_specs(aw):
    kvb = aw // (2 * KV_WIDTH)
    q_spec = pl.BlockSpec((ATTN_BLOCK, aw), lambda n: (n, 0))
    kv_cur = pl.BlockSpec((ATTN_BLOCK, 2 * KV_WIDTH), lambda n: (n, kvb))
    kv_prev = pl.BlockSpec((ATTN_BLOCK, 2 * KV_WIDTH), lambda n: (jnp.maximum(n - 1, 0), kvb))
    return q_spec, kv_cur, kv_prev


def _attn_fwd(proj, sinks, aw, name, after=()):
    l = proj.shape[0]
    nq = aw // HEAD_DIM
    qpk = nq // N_KV_HEADS
    assert aw % (2 * KV_WIDTH) == 0

    rows = qpk * ATTN_BLOCK

    def body(q_ref, kvc_ref, kvp_ref, sink_ref, *rest):
        o_ref = rest[-1]
        n = pl.program_id(0)
        valid = _attn_mask(n, rows) == 0.0
        kv = jnp.concatenate([kvp_ref[...], kvc_ref[...]], axis=0)
        for h in range(nq):
            g = h // qpk
            qh = q_ref[:, h * HEAD_DIM:(h + 1) * HEAD_DIM]
            kh = kv[:, g * HEAD_DIM:(g + 1) * HEAD_DIM]
            vh = kv[:, KV_WIDTH + g * HEAD_DIM:KV_WIDTH + (g + 1) * HEAD_DIM]
            sink = sink_ref[0:1, h:h + 1]
            s = lax.dot_general(qh, kh, (((1,), (1,)), ((), ())), preferred_element_type=F32) * (HEAD_DIM ** -0.5)
            s = jnp.where(valid, s, NEG_INF)
            m = jnp.maximum(jnp.max(s, axis=-1, keepdims=True), sink)
            p = jnp.exp(s - m)
            o = jnp.dot(p.astype(BF16), vh, preferred_element_type=F32)
            o = o * (1.0 / (jnp.sum(p, axis=-1, keepdims=True) + jnp.exp(sink - m)))
            o_ref[:, h * HEAD_DIM:(h + 1) * HEAD_DIM] = o.astype(BF16)

    q_spec, kv_cur, kv_prev = _attn_specs(aw)
    return pl.pallas_call(
        body,
        out_shape=jax.ShapeDtypeStruct((l, aw), BF16),
        grid=(l // ATTN_BLOCK,),
        in_specs=[q_spec, kv_cur, kv_prev, pl.BlockSpec((1, nq), lambda n: (0, 0))] + [pl.BlockSpec(memory_space=pl.ANY)] * len(after),
        out_specs=pl.BlockSpec((ATTN_BLOCK, aw), lambda n: (n, 0)),
        name=name,
        compiler_params=_params(("parallel",)),
    )(proj, proj, proj, sinks, *after)


def _attn_bwd(proj, sinks, dattn, aw, name):
    l = proj.shape[0]
    nq = aw // HEAD_DIM
    qpk = nq // N_KV_HEADS
    scale = HEAD_DIM ** -0.5

    rows = qpk * ATTN_BLOCK
    tn_dims = (((0,), (0,)), ((), ()))

    def body(q_ref, kvc_ref, kvp_ref, sink_ref, do_ref, dq_ref, dcur_ref, dprev_ref, dsink_ref, q_scr, do_scr):
        n = pl.program_id(0)
        mask = _attn_mask(n, rows)
        kv = jnp.concatenate([kvp_ref[...], kvc_ref[...]], axis=0)
        lane = lax.broadcasted_iota(jnp.int32, (1, nq), 1)
        dsink = jnp.zeros((1, nq), F32)
        dks, dvs = [], []
        for g in range(N_KV_HEADS):
            kh = kv[:, g * HEAD_DIM:(g + 1) * HEAD_DIM]
            vh = kv[:, KV_WIDTH + g * HEAD_DIM:KV_WIDTH + (g + 1) * HEAD_DIM]
            _stack_heads(q_ref, q_scr.at[g], g, qpk)
            _stack_heads(do_ref, do_scr.at[g], g, qpk)
            qs, dos = q_scr[g], do_scr[g]
            p, ps = _attn_probs(qs, kh, sink_ref[g * qpk:(g + 1) * qpk], mask)
            pb = p.astype(BF16)
            o = jnp.dot(pb, vh, preferred_element_type=F32)
            delta = jnp.sum(dos.astype(F32) * o, axis=-1, keepdims=True)
            dp = lax.dot_general(dos, vh, (((1,), (1,)), ((), ())), preferred_element_type=F32)
            ds = (p * (dp - delta)).astype(BF16)
            _unstack_heads(jnp.dot(ds, kh, preferred_element_type=F32) * scale, dq_ref, g, qpk)
            dks.append(lax.dot_general(ds, qs, tn_dims, preferred_element_type=F32) * scale)
            dvs.append(lax.dot_general(pb, dos, tn_dims, preferred_element_type=F32))
            t = ps * delta
            for i in range(qpk):
                part = -jnp.sum(t[i * ATTN_BLOCK:(i + 1) * ATTN_BLOCK, :], axis=0, keepdims=True)
                dsink += jnp.where(lane == g * qpk + i, part, 0.0)
        dkv = jnp.concatenate(dks + dvs, axis=1)
        dprev_ref[...] = dkv[:ATTN_BLOCK]
        dcur_ref[...] = dkv[ATTN_BLOCK:]

        @pl.when(n == 0)
        def _():
            dsink_ref[...] = jnp.zeros_like(dsink_ref)

        dsink_ref[...] += dsink

    q_spec, kv_cur, kv_prev = _attn_specs(aw)
    blk = pl.BlockSpec((ATTN_BLOCK, 2 * KV_WIDTH), lambda n: (n, 0))
    return pl.pallas_call(
        body,
        out_shape=(jax.ShapeDtypeStruct((l, aw), BF16), jax.ShapeDtypeStruct((l, 2 * KV_WIDTH), F32),
                   jax.ShapeDtypeStruct((l, 2 * KV_WIDTH), F32), jax.ShapeDtypeStruct((1, nq), F32)),
        grid=(l // ATTN_BLOCK,),
        in_specs=[q_spec, kv_cur, kv_prev, _sink_spec(nq),
                  pl.BlockSpec((ATTN_BLOCK, aw), lambda n: (n, 0))],
        out_specs=(pl.BlockSpec((ATTN_BLOCK, aw), lambda n: (n, 0)), blk, blk, pl.BlockSpec((1, nq), lambda n: (0, 0))),
        scratch_shapes=[pltpu.VMEM((N_KV_HEADS, rows, HEAD_DIM), BF16)] * 2,
        name=name,
        compiler_params=_params(("arbitrary",)),
    )(proj, proj, proj, _sink_column(sinks), dattn)


def _ssm_discretize(a_re, a_im, log_dt, b_re, b_im):
    dt = jnp.exp(log_dt)[:, None]
    mag = jnp.exp(a_re * dt)
    lr, li = mag * jnp.cos(a_im * dt), mag * jnp.sin(a_im * dt)
    den = a_re * a_re + a_im * a_im
    zr = ((lr - 1.0) * a_re + li * a_im) / den
    zi = (li * a_re - (lr - 1.0) * a_im) / den
    bbar_r = zr[:, :, None] * b_re - zi[:, :, None] * b_im
    bbar_i = zr[:, :, None] * b_im + zi[:, :, None] * b_re
    return lr, li, bbar_r, bbar_i


def _cmul(ar, ai, br, bi):
    return ar * br - ai * bi, ar * bi + ai * br


N_SEG = 8


def _cpow(ar, ai, n):
    out, br, bi = None, ar, ai
    while n:
        if n & 1:
            out = (br, bi) if out is None else _cmul(*out, br, bi)
        br, bi = _cmul(br, bi, br, bi)
        n >>= 1
    return out


def _scan_tables(lr, li, seg):
    lr, li = lr.reshape(1, -1), li.reshape(1, -1)
    row = jnp.arange(N_SEG)[:, None]
    ones = jnp.ones((N_SEG, 1), F32)
    fwd, bwd = [], []
    for d in (1, 2, 4):
        pr, pi = _cpow(lr, li, seg * d)
        fwd += [jnp.where(row >= d, pr, 0.0), jnp.where(row >= d, pi, 0.0)]
        bwd += [jnp.where(row < N_SEG - d, pr, 0.0), jnp.where(row < N_SEG - d, -pi, 0.0)]
    fwd += [ones * lr, ones * li]
    bwd += [ones * lr, ones * -li]
    return jnp.concatenate(fwd, 0), jnp.concatenate(bwd, 0)


def _pack_in(b):
    g, n, p = b.shape
    t = g // GROUPS_PER_TILE
    eye = jnp.eye(GROUPS_PER_TILE, dtype=b.dtype)
    bb = b.reshape(t, GROUPS_PER_TILE, n, p)
    return jnp.einsum("tgnp,gh->tgphn", bb, eye).reshape(t, GROUPS_PER_TILE * p, GROUPS_PER_TILE * n)


def _pack_out(c):
    g, p, n = c.shape
    t = g // GROUPS_PER_TILE
    eye = jnp.eye(GROUPS_PER_TILE, dtype=c.dtype)
    cc = c.reshape(t, GROUPS_PER_TILE, p, n)
    return jnp.einsum("tgpn,gh->tgnhp", cc, eye).reshape(t, GROUPS_PER_TILE * n, GROUPS_PER_TILE * p)


def _unpack_diag(x, n, p):
    t = x.shape[0]
    xx = x.reshape(t, GROUPS_PER_TILE, n, GROUPS_PER_TILE, p)
    eye = jnp.eye(GROUPS_PER_TILE, dtype=x.dtype)
    return jnp.einsum("tgnhp,gh->tgnp", xx, eye).reshape(t * GROUPS_PER_TILE, n, p)


def _seg_scan(hr_ref, hi_ref, tab_ref, l, reverse, states_refs=None):
    nq = hr_ref.shape[0]
    seg = l // N_SEG
    span = 8 * N_SEG
    nblk = seg // 8
    row = lax.broadcasted_iota(jnp.int32, (N_SEG, LANES), 0)

    def tab(r0, q):
        return tab_ref[r0:r0 + 8, q * LANES:(q + 1) * LANES]

    lam = [(tab(48, q), tab(56, q)) for q in range(nq)]

    def views(refs, q, jb):
        base = pl.multiple_of((nblk - 1 - jb if reverse else jb) * span, span)
        return [r.at[q, pl.ds(base, span), :] for r in refs]

    def local_rows():
        return range(7, -1, -1) if reverse else range(8)

    def at(r):
        return pl.ds(r * N_SEG, N_SEG)

    def pass1(jb, carry):
        hs = list(carry)
        for q in range(nq):
            vr, vi = views((hr_ref, hi_ref), q, jb)
            lr, li = lam[q]
            h_r, h_i = hs[2 * q], hs[2 * q + 1]
            for r in local_rows():
                h_r, h_i = lr * h_r - li * h_i + vr[at(r), :], lr * h_i + li * h_r + vi[at(r), :]
                vr[at(r), :] = h_r
                vi[at(r), :] = h_i
            hs[2 * q], hs[2 * q + 1] = h_r, h_i
        return tuple(hs)

    zero = jnp.zeros((N_SEG, LANES), F32)
    ends = lax.fori_loop(0, nblk, pass1, (zero,) * (2 * nq))

    carry_in = []
    for q in range(nq):
        er, ei = ends[2 * q], ends[2 * q + 1]
        for idx, d in enumerate((1, 2, 4)):
            mr, mi = tab(16 * idx, q), tab(16 * idx + 8, q)
            shift = N_SEG - d if reverse else d
            sr, si = pltpu.roll(er, shift, 0), pltpu.roll(ei, shift, 0)
            er, ei = er + mr * sr - mi * si, ei + mr * si + mi * sr
        if reverse:
            keep, shift = row < N_SEG - 1, N_SEG - 1
        else:
            keep, shift = row >= 1, 1
        carry_in += [jnp.where(keep, pltpu.roll(er, shift, 0), 0.0), jnp.where(keep, pltpu.roll(ei, shift, 0), 0.0)]

    with_acc = states_refs is not None

    def pass2(jb, carry):
        cs = list(carry)
        for q in range(nq):
            vr, vi = views((hr_ref, hi_ref), q, jb)
            lr, li = lam[q]
            d_r, d_i = cs[2 * q], cs[2 * q + 1]
            if with_acc:
                fr, fi = views(states_refs, q, jb)
                n_r, n_i, a_r, a_i = cs[2 * nq + 4 * q:2 * nq + 4 * q + 4]
            for r in local_rows():
                d_r, d_i = lr * d_r - li * d_i, lr * d_i + li * d_r
                g_r, g_i = vr[at(r), :] + d_r, vi[at(r), :] + d_i
                vr[at(r), :] = g_r
                vi[at(r), :] = g_i
                if with_acc:
                    p_r, p_i = fr[at(r), :], fi[at(r), :]
                    a_r, a_i = a_r + n_r * p_r + n_i * p_i, a_i + n_i * p_r - n_r * p_i
                    n_r, n_i = g_r, g_i
            cs[2 * q], cs[2 * q + 1] = d_r, d_i
            if with_acc:
                cs[2 * nq + 4 * q:2 * nq + 4 * q + 4] = [n_r, n_i, a_r, a_i]
        return tuple(cs)

    init = list(carry_in)
    if with_acc:
        for q in range(nq):
            init += [carry_in[2 * q], carry_in[2 * q + 1], zero, zero]
    out = lax.fori_loop(0, nblk, pass2, tuple(init))
    if with_acc:
        return [(out[2 * nq + 4 * q + 2], out[2 * nq + 4 * q + 3]) for q in range(nq)]
    return None


def _put_states(ref, rows, val):
    for q in range(ref.shape[0]):
        ref[q, rows, :] = val[:, q * LANES:(q + 1) * LANES]


def _get_states(ref, rows):
    return jnp.concatenate([ref[q, rows, :] for q in range(ref.shape[0])], axis=1)


def _s5_dims(sw):
    chan = GROUPS_PER_TILE * SSM_GROUP
    states = GROUPS_PER_TILE * SSM_STATE
    assert chan == LANES and sw % chan == 0
    return sw // chan, chan, states


def _interleave(x):
    l, w = x.shape
    return x.reshape(N_SEG, l // N_SEG, w).transpose(1, 0, 2).reshape(l, w)


def _deinterleave(x):
    l, w = x.shape
    return x.reshape(l // N_SEG, N_SEG, w).transpose(1, 0, 2).reshape(l, w)


def _s5_fwd(proj, u_off, packs, dvec, tab_f, sw, name):
    l = proj.shape[0]
    nt, chan, states = _s5_dims(sw)
    ch = _tile(l, 512, 8)
    ub = u_off // chan
    assert u_off % chan == 0

    def body(u_ref, br_ref, bi_ref, cr_ref, ci_ref, d_ref, tab_ref, y_ref, gy_ref, hr_ref, hi_ref):
        for i in range(l // ch):
            rows = pl.ds(i * ch, ch)
            u = u_ref[rows, :]
            _put_states(hr_ref, rows, jnp.dot(u, br_ref[0], preferred_element_type=F32))
            _put_states(hi_ref, rows, jnp.dot(u, bi_ref[0], preferred_element_type=F32))
        _seg_scan(hr_ref, hi_ref, tab_ref, l, reverse=False)
        for i in range(l // ch):
            rows = pl.ds(i * ch, ch)
            y = jnp.dot(_get_states(hr_ref, rows).astype(BF16), cr_ref[0], preferred_element_type=F32)
            y -= jnp.dot(_get_states(hi_ref, rows).astype(BF16), ci_ref[0], preferred_element_type=F32)
            y = y + d_ref[...] * u_ref[rows, :].astype(F32)
            y_ref[rows, :] = y
            gy_ref[rows, :] = _gelu_value(y).astype(BF16)

    pin = pl.BlockSpec((1, chan, states), lambda t: (t, 0, 0))
    pout = pl.BlockSpec((1, states, chan), lambda t: (t, 0, 0))
    return pl.pallas_call(
        body,
        out_shape=(jax.ShapeDtypeStruct((l, sw), F32), jax.ShapeDtypeStruct((l, sw), BF16)),
        grid=(nt,),
        in_specs=[pl.BlockSpec((l, chan), lambda t: (0, ub + t)), pin, pin, pout, pout,
                  pl.BlockSpec((1, chan), lambda t: (0, t)), pl.BlockSpec((64, states), lambda t: (0, t))],
        out_specs=(pl.BlockSpec((l, chan), lambda t: (0, t)), pl.BlockSpec((l, chan), lambda t: (0, t))),
        scratch_shapes=[pltpu.VMEM((states // LANES, l, LANES), F32)] * 2,
        name=name,
        compiler_params=_params(("parallel",)),
    )(proj, packs["br"], packs["bi"], packs["cr"], packs["ci"], dvec, tab_f)


def _s5_bwd(proj, u_off, y, dgy, packs, dvec, tab_f, tab_b, sw, name):
    l = proj.shape[0]
    nt, chan, states = _s5_dims(sw)
    ch = _tile(l, 512, 8)
    ub = u_off // chan
    tn_dims = (((0,), (0,)), ((), ()))

    def body(u_ref, y_ref, dgy_ref, br_ref, bi_ref, brt_ref, bit_ref, crt_ref, cit_ref, d_ref, tabf_ref, tabb_ref,
             du_ref, dlam_ref, dbr_ref, dbi_ref, dcr_ref, dci_ref, dd_ref, hr_ref, hi_ref, gr_ref, gi_ref, dy_ref):
        for i in range(l // ch):
            rows = pl.ds(i * ch, ch)
            u = u_ref[rows, :]
            _put_states(hr_ref, rows, jnp.dot(u, br_ref[0], preferred_element_type=F32))
            _put_states(hi_ref, rows, jnp.dot(u, bi_ref[0], preferred_element_type=F32))
            dyv = (dgy_ref[rows, :].astype(F32) * _gelu_slope(y_ref[rows, :])).astype(BF16)
            dy_ref[rows, :] = dyv
            _put_states(gr_ref, rows, jnp.dot(dyv, crt_ref[0], preferred_element_type=F32))
            _put_states(gi_ref, rows, -jnp.dot(dyv, cit_ref[0], preferred_element_type=F32))
        _seg_scan(hr_ref, hi_ref, tabf_ref, l, reverse=False)
        accs = _seg_scan(gr_ref, gi_ref, tabb_ref, l, reverse=True, states_refs=(hr_ref, hi_ref))
        dlam_ref[...] = jnp.concatenate(
            [jnp.concatenate([jnp.sum(a[0], axis=0, keepdims=True) for a in accs], axis=1),
             jnp.concatenate([jnp.sum(a[1], axis=0, keepdims=True) for a in accs], axis=1), jnp.zeros((6, states), F32)], axis=0)
        dbr_ref[...] = jnp.zeros_like(dbr_ref)
        dbi_ref[...] = jnp.zeros_like(dbi_ref)
        dcr_ref[...] = jnp.zeros_like(dcr_ref)
        dci_ref[...] = jnp.zeros_like(dci_ref)
        dd = jnp.zeros((1, chan), F32)
        for i in range(l // ch):
            rows = pl.ds(i * ch, ch)
            u = u_ref[rows, :]
            dyv = dy_ref[rows, :]
            grb = _get_states(gr_ref, rows).astype(BF16)
            gib = _get_states(gi_ref, rows).astype(BF16)
            dbr_ref[0] += lax.dot_general(grb, u, tn_dims, preferred_element_type=F32)
            dbi_ref[0] += lax.dot_general(gib, u, tn_dims, preferred_element_type=F32)
            dcr_ref[0] += lax.dot_general(_get_states(hr_ref, rows).astype(BF16), dyv, tn_dims, preferred_element_type=F32)
            dci_ref[0] -= lax.dot_general(_get_states(hi_ref, rows).astype(BF16), dyv, tn_dims, preferred_element_type=F32)
            du = jnp.dot(grb, brt_ref[0], preferred_element_type=F32) + jnp.dot(gib, bit_ref[0], preferred_element_type=F32)
            dyf = dyv.astype(F32)
            du_ref[rows, :] = (du + d_ref[...] * dyf).astype(BF16)
            dd += jnp.sum(dyf * u.astype(F32), axis=0, keepdims=True)
        dd_ref[...] = dd

    pin = pl.BlockSpec((1, chan, states), lambda t: (t, 0, 0))
    pout = pl.BlockSpec((1, states, chan), lambda t: (t, 0, 0))
    seq = pl.BlockSpec((l, chan), lambda t: (0, t))
    tab = pl.BlockSpec((64, states), lambda t: (0, t))
    vec = pl.BlockSpec((1, chan), lambda t: (0, t))
    pack_shape = jax.ShapeDtypeStruct((nt, states, chan), F32)
    return pl.pallas_call(
        body,
        out_shape=(jax.ShapeDtypeStruct((l, sw), BF16), jax.ShapeDtypeStruct((8, nt * states), F32),
                   pack_shape, pack_shape, pack_shape, pack_shape, jax.ShapeDtypeStruct((1, sw), F32)),
        grid=(nt,),
        in_specs=[pl.BlockSpec((l, chan), lambda t: (0, ub + t)), seq, seq, pin, pin, pout, pout, pin, pin, vec, tab, tab],
        out_specs=(seq, pl.BlockSpec((8, states), lambda t: (0, t)), pout, pout, pout, pout, vec),
        scratch_shapes=[pltpu.VMEM((states // LANES, l, LANES), F32)] * 4 + [pltpu.VMEM((l, chan), BF16)],
        name=name,
        compiler_params=_params(("parallel",)),
    )(proj, y, dgy, packs["br"], packs["bi"], packs["brt"], packs["bit"], packs["crt"], packs["cit"], dvec, tab_f, tab_b)


GELU_K = math.sqrt(2.0 / math.pi)
GELU_C = 0.044715


def _gelu_value(v):
    return 0.5 * v * (1.0 + jnp.tanh(GELU_K * (v + GELU_C * v * v * v)))


def _gelu_slope(v):
    t = jnp.tanh(GELU_K * (v + GELU_C * v * v * v))
    return 0.5 * (1.0 + t) + 0.5 * v * (1.0 - t * t) * GELU_K * (1.0 + 3.0 * GELU_C * v * v)


def _mix(ga, gs, attn_out, glu, name):
    l, d = ga.shape
    tl = _tile(l, 256, 16)

    def body(ga_ref, gs_ref, a_ref, u_ref, o_ref):
        ssm = u_ref[:, :d].astype(F32) * _sigmoid(u_ref[:, d:].astype(F32))
        o_ref[...] = (_sigmoid(ga_ref[...].astype(F32)) * a_ref[...].astype(F32)
                      + _sigmoid(gs_ref[...].astype(F32)) * ssm).astype(BF16)

    return pl.pallas_call(
        body, out_shape=jax.ShapeDtypeStruct((l, d), BF16), grid=(l // tl,),
        in_specs=[_row_spec(tl, d), _row_spec(tl, d), _row_spec(tl, d), _row_spec(tl, 2 * d)], out_specs=_row_spec(tl, d),
        name=name, compiler_params=_params(("parallel",)),
    )(ga, gs, attn_out, glu)


def _mix_bwd(ga, gs, attn_out, glu, dmixed, name):
    l, d = ga.shape
    tl = _tile(l, 256, 16)

    def body(ga_ref, gs_ref, a_ref, u_ref, dm_ref, dga_ref, dgs_ref, da_ref, dglu_ref):
        dm = dm_ref[...].astype(F32)
        sa = _sigmoid(ga_ref[...].astype(F32))
        ss = _sigmoid(gs_ref[...].astype(F32))
        sb = _sigmoid(u_ref[:, d:].astype(F32))
        ua = u_ref[:, :d].astype(F32)
        dssm = dm * ss
        dga_ref[...] = (dm * a_ref[...].astype(F32) * sa * (1.0 - sa)).astype(BF16)
        da_ref[...] = (dm * sa).astype(BF16)
        dgs_ref[...] = (dm * (ua * sb) * ss * (1.0 - ss)).astype(BF16)
        dglu_ref[:, :d] = (dssm * sb).astype(BF16)
        dglu_ref[:, d:] = (dssm * ua * sb * (1.0 - sb)).astype(BF16)

    out = jax.ShapeDtypeStruct((l, d), BF16)
    return pl.pallas_call(
        body, out_shape=(out, out, out, jax.ShapeDtypeStruct((l, 2 * d), BF16)), grid=(l // tl,),
        in_specs=[_row_spec(tl, d), _row_spec(tl, d), _row_spec(tl, d), _row_spec(tl, 2 * d), _row_spec(tl, d)],
        out_specs=(_row_spec(tl, d), _row_spec(tl, d), _row_spec(tl, d), _row_spec(tl, 2 * d)), name=name,
        compiler_params=_params(("parallel",)),
    )(ga, gs, attn_out, glu, dmixed)


CONV_BLOCKS = 4
HALO = 16


def _shift_rows(v, k, head):
    row = lax.broadcasted_iota(jnp.int32, v.shape, 0)
    out = pltpu.roll(v, k, 0)
    for r in range(k):
        out = jnp.where(row == r, head[HALO - k + r:HALO - k + r + 1, :], out)
    return out


def _shift_rows_up(v, k, tail):
    n = v.shape[0]
    row = lax.broadcasted_iota(jnp.int32, v.shape, 0)
    out = pltpu.roll(v, n - k, 0)
    for r in range(k):
        out = jnp.where(row == n - k + r, tail[r:r + 1, :], out)
    return out


def _conv_gate(g, head, w_ref, b_ref):
    return w_ref[0:1, :] * _shift_rows(g, 2, head) + w_ref[1:2, :] * _shift_rows(g, 1, head) + w_ref[2:3, :] * g + b_ref[...]


def _conv_act(up, conv_w, conv_b, ff, cw, name):
    l = up.shape[0]
    tl = _tile(l, 256, HALO)
    nj = ff // cw
    hb = tl // HALO

    def body(g_ref, gp_ref, v_ref, w_ref, b_ref, o_ref):
        i = pl.program_id(0)
        head = gp_ref[...].astype(F32) * jnp.where(i > 0, 1.0, 0.0)
        gc = _conv_gate(g_ref[...].astype(F32), head, w_ref, b_ref)
        o_ref[...] = (gc * _sigmoid(gc) * v_ref[...].astype(F32)).astype(BF16)

    return pl.pallas_call(
        body, out_shape=jax.ShapeDtypeStruct((l, ff), BF16), grid=(l // tl, nj),
        in_specs=[pl.BlockSpec((tl, cw), lambda i, j: (i, 2 * j)),
                  pl.BlockSpec((HALO, cw), lambda i, j: (jnp.maximum(i * hb - 1, 0), 2 * j)),
                  pl.BlockSpec((tl, cw), lambda i, j: (i, 2 * j + 1)),
                  pl.BlockSpec((3, cw), lambda i, j: (0, j)), pl.BlockSpec((1, cw), lambda i, j: (0, j))],
        out_specs=pl.BlockSpec((tl, cw), lambda i, j: (i, j)), name=name,
        compiler_params=_params(("parallel", "parallel")),
    )(up, up, up, conv_w, conv_b)


def _conv_act_bwd(up, da, conv_w, conv_b, ff, cw, name):
    l = up.shape[0]
    tl = _tile(l, 256, HALO)
    nj = ff // cw
    hb = tl // HALO
    ni = l // tl

    def body(g_ref, gp_ref, gn_ref, v_ref, vn_ref, da_ref, dan_ref, w_ref, b_ref, dup_ref, dw_ref, db_ref):
        i = pl.program_id(0)
        g = g_ref[...].astype(F32)
        head = gp_ref[...].astype(F32) * jnp.where(i > 0, 1.0, 0.0)
        g1 = _shift_rows(g, 1, head)
        g2 = _shift_rows(g, 2, head)
        gc = w_ref[0:1, :] * g2 + w_ref[1:2, :] * g1 + w_ref[2:3, :] * g + b_ref[...]
        sg = _sigmoid(gc)
        dav = da_ref[...].astype(F32)
        dgc = dav * v_ref[...].astype(F32) * (sg * (1.0 + gc * (1.0 - sg)))
        gn = gn_ref[...].astype(F32)
        gcn = _conv_gate(gn, g[tl - HALO:, :], w_ref, b_ref)
        sgn = _sigmoid(gcn)
        dgcn = dan_ref[...].astype(F32) * vn_ref[...].astype(F32) * (sgn * (1.0 + gcn * (1.0 - sgn)))
        dgcn = dgcn * jnp.where(i < ni - 1, 1.0, 0.0)
        dgate = w_ref[2:3, :] * dgc + w_ref[1:2, :] * _shift_rows_up(dgc, 1, dgcn) + w_ref[0:1, :] * _shift_rows_up(dgc, 2, dgcn)
        dup_ref[:, :cw] = dgate.astype(BF16)
        dup_ref[:, cw:] = (dav * (gc * sg)).astype(BF16)
        zero = jnp.zeros((1, cw), F32)
        dw_ref[...] = jnp.concatenate(
            [jnp.sum(dgc * g2, axis=0, keepdims=True), jnp.sum(dgc * g1, axis=0, keepdims=True),
             jnp.sum(dgc * g, axis=0, keepdims=True)] + [zero] * 5, axis=0)
        db_ref[...] = jnp.concatenate([jnp.sum(dgc, axis=0, keepdims=True)] + [zero] * 7, axis=0)

    def cur(col):
        return pl.BlockSpec((tl, cw), lambda i, j, col=col: (i, 2 * j + col))

    def prev(col):
        return pl.BlockSpec((HALO, cw), lambda i, j, col=col: (jnp.maximum(i * hb - 1, 0), 2 * j + col))

    def nxt(col):
        return pl.BlockSpec((HALO, cw), lambda i, j, col=col: (jnp.minimum((i + 1) * hb, l // HALO - 1), 2 * j + col))

    part = jax.ShapeDtypeStruct((ni * 8, ff), F32)
    part_spec = pl.BlockSpec((8, cw), lambda i, j: (i, j))
    return pl.pallas_call(
        body, out_shape=(jax.ShapeDtypeStruct((l, 2 * ff), BF16), part, part), grid=(ni, nj),
        in_specs=[cur(0), prev(0), nxt(0), cur(1), nxt(1), pl.BlockSpec((tl, cw), lambda i, j: (i, j)),
                  pl.BlockSpec((HALO, cw), lambda i, j: (jnp.minimum((i + 1) * hb, l // HALO - 1), j)),
                  pl.BlockSpec((3, cw), lambda i, j: (0, j)), pl.BlockSpec((1, cw), lambda i, j: (0, j))],
        out_specs=(pl.BlockSpec((tl, 2 * cw), lambda i, j: (i, j)), part_spec, part_spec), name=name,
        compiler_params=_params(("parallel", "parallel")),
    )(up, up, up, up, up, da, da, conv_w, conv_b)


def _sum_rows8(parts, name):
    n8, w = parts.shape
    n = n8 // 8
    cw = _tile(w, 2048)

    def body(p_ref, o_ref):
        acc = p_ref[0:8, :]
        for k in range(1, n):
            acc = acc + p_ref[8 * k:8 * k + 8, :]
        o_ref[...] = acc

    return pl.pallas_call(body, out_shape=jax.ShapeDtypeStruct((8, w), F32), grid=(w // cw,),
                          in_specs=[pl.BlockSpec((n8, cw), lambda j: (0, j))], out_specs=pl.BlockSpec((8, cw), lambda j: (0, j)),
                          name=name, compiler_params=_params(("parallel",)))(parts)


def _ada_fwd(c_all, w_shard, b_shard, name):
    nb, d = c_all.shape
    n = w_shard.shape[1]
    tn = _tile(n, 512)

    def body(c_ref, w_ref, b_ref, o_ref):
        cv = c_ref[...]
        cond = (cv * _sigmoid(cv)).astype(BF16)
        o_ref[...] = jnp.dot(cond, w_ref[...].astype(BF16), preferred_element_type=F32) + b_ref[...]

    return pl.pallas_call(
        body, out_shape=jax.ShapeDtypeStruct((nb, n), F32), grid=(n // tn,),
        in_specs=[pl.BlockSpec((nb, d), lambda j: (0, 0)), pl.BlockSpec((d, tn), lambda j: (0, j)),
                  pl.BlockSpec((1, tn), lambda j: (0, j))],
        out_specs=pl.BlockSpec((nb, tn), lambda j: (0, j)), name=name, compiler_params=_params(("parallel",)),
    )(c_all, w_shard, b_shard)


def _adam_update(w, g, m, v):
    m2 = ADAM_B1 * m + (1.0 - ADAM_B1) * g
    v2 = ADAM_B2 * v + (1.0 - ADAM_B2) * (g * g)
    m_hat = m2 / (1.0 - ADAM_B1 ** ADAM_STEP)
    v_hat = v2 / (1.0 - ADAM_B2 ** ADAM_STEP)
    return -ADAM_LR * (m_hat / (jnp.sqrt(v_hat) + ADAM_EPS) + ADAM_WD * w), m2, v2


def _ada_bwd_adam(c_all_t, dmod_shard, w, m, v, name):
    d, nb = c_all_t.shape
    n = w.shape[1]
    tr, tn = _tile(d, 512, 8), _tile(n, 512)

    def body(c_ref, dm_ref, w_ref, m_ref, v_ref, g_ref, dl_ref, m2_ref, v2_ref):
        cv = c_ref[...]
        cond = cv * _sigmoid(cv)
        g = cond[:, 0:1] * dm_ref[0:1, :]
        for b in range(1, nb):
            g = g + cond[:, b:b + 1] * dm_ref[b:b + 1, :]
        g_ref[...] = g
        dl_ref[...], m2_ref[...], v2_ref[...] = _adam_update(w_ref[...], g, m_ref[...], v_ref[...])

    blk = pl.BlockSpec((tr, tn), lambda i, j: (i, j))
    out = jax.ShapeDtypeStruct((d, n), F32)
    return pl.pallas_call(
        body, out_shape=(out, out, out, out), grid=(d // tr, n // tn),
        in_specs=[pl.BlockSpec((tr, nb), lambda i, j: (i, 0)), pl.BlockSpec((nb, tn), lambda i, j: (0, j)), blk, blk, blk],
        out_specs=(blk, blk, blk, blk), name=name, compiler_params=_params(("parallel", "parallel")),
    )(c_all_t, dmod_shard, w, m, v)


def _adam(w, g, m, v, name):
    r, c = w.shape
    tr = _tile(r, 256, 8)

    def body(w_ref, g_ref, m_ref, v_ref, dl_ref, m2_ref, v2_ref):
        dl_ref[...], m2_ref[...], v2_ref[...] = _adam_update(w_ref[...], g_ref[...], m_ref[...], v_ref[...])

    blk = pl.BlockSpec((tr, c), lambda i: (i, 0))
    out = jax.ShapeDtypeStruct((r, c), F32)
    return pl.pallas_call(body, out_shape=(out, out, out), grid=(r // tr,), in_specs=[blk] * 4, out_specs=(blk,) * 3,
                          name=name, compiler_params=_params(("parallel",)))(w, g, m, v)


def _sum_devices(gathered, name):
    nd, r, c = gathered.shape
    tr = _tile(r, 64, 16)

    def body(g_ref, o_ref):
        acc = g_ref[0].astype(F32)
        for k in range(1, nd):
            acc = acc + g_ref[k].astype(F32)
        o_ref[...] = acc

    return pl.pallas_call(body, out_shape=jax.ShapeDtypeStruct((r, c), F32), grid=(r // tr,),
                          in_specs=[pl.BlockSpec((nd, tr, c), lambda i: (0, i, 0))], out_specs=pl.BlockSpec((tr, c), lambda i: (i, 0)),
                          name=name, compiler_params=_params(("parallel",)))(gathered)


def _place():
    x, y, c = lax.axis_index("x"), lax.axis_index("y"), lax.axis_index("c")
    chips = [(1 - x, y), (x, 1 - y), (1 - x, 1 - y)]
    return x, y, c, chips


def _all_gather8(block, name):
    m_per, n = block.shape

    def body(x_ref, out_ref, send_sems, recv_sems, local_sem):
        x, y, c, chips = _place()
        me, sibling = (x, y, c), (x, y, 1 - c)

        def rows(px, py, pc):
            return out_ref.at[pl.ds((4 * px + 2 * py + pc) * m_per, m_per), :]

        def copy(k, blk, to, src=None):
            return pltpu.make_async_remote_copy(
                src_ref=rows(*blk) if src is None else src, dst_ref=rows(*blk), send_sem=send_sems.at[k],
                recv_sem=recv_sems.at[k], device_id=to, device_id_type=MESH)

        mine = pltpu.make_async_copy(x_ref, rows(*me), local_sem)
        mine.start()
        first = [copy(0, me, sibling, src=x_ref)]
        first += [copy(1 + j, me, (*chip, c), src=x_ref) for j, chip in enumerate(chips)]
        for cp in first:
            cp.start()
        passed = [copy(4 + j, (*chip, c), sibling) for j, chip in enumerate(chips)]
        for j, chip in enumerate(chips):
            copy(1 + j, (*chip, c), me).wait_recv()
            passed[j].start()
        copy(0, sibling, me).wait_recv()
        for j, chip in enumerate(chips):
            copy(4 + j, (*chip, 1 - c), me).wait_recv()
        for cp in first + passed:
            cp.wait_send()
        mine.wait()

    return pl.pallas_call(
        body,
        out_shape=jax.ShapeDtypeStruct((N_DEV * m_per, n), block.dtype),
        in_specs=[pl.BlockSpec(memory_space=pltpu.VMEM)],
        out_specs=pl.BlockSpec(memory_space=pltpu.VMEM),
        scratch_shapes=[pltpu.SemaphoreType.DMA((7,)), pltpu.SemaphoreType.DMA((7,)), pltpu.SemaphoreType.DMA],
        name=name,
        compiler_params=pltpu.CompilerParams(vmem_limit_bytes=VMEM_LIMIT_BYTES),
    )(block)


ANY = pl.BlockSpec(memory_space=pl.ANY)


def _place_shard(shard, name, after=()):
    r, k = shard.shape
    tb = _tile(r, 512, 16)
    nb = r // tb
    chip = (2 * lax.axis_index("x") + lax.axis_index("y")).astype(jnp.int32).reshape(1)

    def body(j_ref, s_ref, *rest):
        rest[-1][...] = s_ref[...].astype(BF16)

    return pl.pallas_call(
        body, out_shape=jax.ShapeDtypeStruct((N_CHIPS * r, k), BF16),
        grid_spec=pltpu.PrefetchScalarGridSpec(
            num_scalar_prefetch=1, grid=(nb,),
            in_specs=[pl.BlockSpec((tb, k), lambda i, j_ref: (i, 0))] + [ANY] * len(after),
            out_specs=pl.BlockSpec((tb, k), lambda i, j_ref: (j_ref[0] * nb + i, 0))),
        name=name, compiler_params=_params(("parallel",)),
    )(chip, shard, *after)


HBM_SPEC = pl.BlockSpec(memory_space=pltpu.HBM)
SEM_SPEC = pl.BlockSpec(memory_space=pltpu.SEMAPHORE)
TOKEN_SPEC = pl.BlockSpec(memory_space=pltpu.VMEM)
SPLIT_COPY = pltpu.CompilerParams(has_side_effects=pltpu.SideEffectType.DATAFLOW_SIDE_EFFECTING)


def _in_hbm(arrays):
    return [pltpu.with_memory_space_constraint(a, pltpu.HBM) for a in arrays]


def _hbm_like(arrays):
    return tuple(pltpu.HBM(a.shape, a.dtype) for a in arrays)


def _token_shape():
    return jax.ShapeDtypeStruct((8, LANES), F32)


def _gathered_rows(buf, px, py, half):
    r = buf.shape[0] // N_CHIPS
    return buf.at[pl.ds(pl.multiple_of((2 * px + py) * r + half * (r // 2), 16), r // 2), :]


def _gather_start(groups, name):
    sizes = [len(g) for g in groups]
    flat = [b for g in groups for b in g]
    nb, ng = len(flat), len(groups)

    def body(*refs):
        bufs = refs[:nb]
        sems = refs[nb:nb + 2 * ng]
        token = refs[-1]
        x, y, c, chips = _place()
        pos = 0
        for gi, nw in enumerate(sizes):
            for k, chip in enumerate(chips):
                for w in range(nw):
                    mine = _gathered_rows(bufs[pos + w], x, y, c)
                    pltpu.make_async_remote_copy(src_ref=mine, dst_ref=mine, send_sem=sems[2 * gi].at[k * nw + w], recv_sem=sems[2 * gi + 1].at[k * nw + w],
                                                 device_id=(*chip, c), device_id_type=MESH).start()
            pos += nw
        token[...] = jnp.zeros_like(token)

    sem_shapes = tuple(pltpu.SemaphoreType.DMA((3 * n,)) for n in sizes for _ in range(2))
    outs = pl.pallas_call(
        body, name=name, out_shape=sem_shapes + _hbm_like(flat) + (_token_shape(),),
        in_specs=[HBM_SPEC] * nb, out_specs=(SEM_SPEC,) * (2 * ng) + (HBM_SPEC,) * nb + (TOKEN_SPEC,),
        input_output_aliases={i: 2 * ng + i for i in range(nb)}, compiler_params=SPLIT_COPY,
    )(*_in_hbm(flat))
    res, pos = [], 2 * ng
    for gi, n in enumerate(sizes):
        res.append((outs[2 * gi], outs[2 * gi + 1], list(outs[pos:pos + n])))
        pos += n
    return res, outs[-1]


def _gather_forward(bufs, ici_send, ici_recv, after, name):
    nw, na = len(bufs), len(after)

    def body(*refs):
        b = refs[:nw]
        isend, irecv = refs[nw], refs[nw + 1]
        dsend, drecv = refs[nw + 2 + na], refs[nw + 3 + na]
        x, y, c, chips = _place()
        for k, chip in enumerate(chips):
            for w in range(nw):
                landed = _gathered_rows(b[w], *chip, c)
                pltpu.make_async_remote_copy(src_ref=landed, dst_ref=landed, send_sem=isend.at[k * nw + w], recv_sem=irecv.at[k * nw + w],
                                             device_id=(*chip, c), device_id_type=MESH).wait_recv()
                pltpu.make_async_remote_copy(src_ref=landed, dst_ref=landed, send_sem=dsend.at[k * nw + w], recv_sem=drecv.at[k * nw + w],
                                             device_id=(x, y, 1 - c), device_id_type=MESH).start()
        for k, chip in enumerate(chips):
            for w in range(nw):
                mine = _gathered_rows(b[w], x, y, c)
                pltpu.make_async_remote_copy(src_ref=mine, dst_ref=mine, send_sem=isend.at[k * nw + w], recv_sem=irecv.at[k * nw + w],
                                             device_id=(*chip, c), device_id_type=MESH).wait_send()
        refs[-1][...] = jnp.zeros_like(refs[-1])

    sem = pltpu.SemaphoreType.DMA((3 * nw,))
    outs = pl.pallas_call(
        body, name=name, out_shape=(sem, sem) + _hbm_like(bufs) + (_token_shape(),),
        in_specs=[HBM_SPEC] * nw + [SEM_SPEC, SEM_SPEC] + [ANY] * na, out_specs=(SEM_SPEC, SEM_SPEC) + (HBM_SPEC,) * nw + (TOKEN_SPEC,),
        input_output_aliases={i: 2 + i for i in range(nw)}, compiler_params=SPLIT_COPY,
    )(*bufs, ici_send, ici_recv, *after)
    return outs[0], outs[1], list(outs[2:2 + nw]), outs[-1]


def _gather_finish(bufs, d2d_send, d2d_recv, name, after=()):
    nw = len(bufs)

    def body(*refs):
        b = refs[:nw]
        dsend, drecv = refs[nw], refs[nw + 1]
        x, y, c, chips = _place()
        for k, chip in enumerate(chips):
            for w in range(nw):
                theirs = _gathered_rows(b[w], *chip, 1 - c)
                pltpu.make_async_remote_copy(src_ref=theirs, dst_ref=theirs, send_sem=dsend.at[k * nw + w], recv_sem=drecv.at[k * nw + w],
                                             device_id=(x, y, 1 - c), device_id_type=MESH).wait_recv()
                passed = _gathered_rows(b[w], *chip, c)
                pltpu.make_async_remote_copy(src_ref=passed, dst_ref=passed, send_sem=dsend.at[k * nw + w], recv_sem=drecv.at[k * nw + w],
                                             device_id=(x, y, 1 - c), device_id_type=MESH).wait_send()

    outs = pl.pallas_call(
        body, name=name, out_shape=_hbm_like(bufs), in_specs=[HBM_SPEC] * nw + [SEM_SPEC, SEM_SPEC] + [ANY] * len(after),
        out_specs=(HBM_SPEC,) * nw, input_output_aliases={i: i for i in range(nw)}, compiler_params=SPLIT_COPY,
    )(*bufs, d2d_send, d2d_recv, *after)
    return list(outs)


def _swap_copies(src, land, send_sems, recv_sems):
    x, y, c, _ = _place()
    copies = []
    for w in range(len(src)):
        r = src[w].shape[0] // N_CHIPS
        h = r // 2
        for j in range(N_CHIPS):
            copies.append(pltpu.make_async_remote_copy(
                src_ref=src[w].at[pl.ds(pl.multiple_of(j * r + (1 - c) * h, 16), h), :], dst_ref=land[w].at[pl.ds(j * h, h), :],
                send_sem=send_sems.at[w * N_CHIPS + j], recv_sem=recv_sems.at[w * N_CHIPS + j], device_id=(x, y, 1 - c), device_id_type=MESH))
    return copies


def _swap_start(grads, name):
    nw = len(grads)
    landing = [lax.empty((g.shape[0] // 2, g.shape[1]), g.dtype) for g in grads]

    def body(*refs):
        for cp in _swap_copies(refs[:nw], refs[nw:2 * nw], refs[2 * nw], refs[2 * nw + 1]):
            cp.start()
        refs[-1][...] = jnp.zeros_like(refs[-1])

    sem = pltpu.SemaphoreType.DMA((N_CHIPS * nw,))
    outs = pl.pallas_call(
        body, name=name, out_shape=(sem, sem) + _hbm_like(grads) + _hbm_like(landing) + (_token_shape(),),
        in_specs=[HBM_SPEC] * (2 * nw), out_specs=(SEM_SPEC, SEM_SPEC) + (HBM_SPEC,) * (2 * nw) + (TOKEN_SPEC,),
        input_output_aliases={i: 2 + i for i in range(2 * nw)}, compiler_params=SPLIT_COPY,
    )(*_in_hbm(grads), *_in_hbm(landing))
    return (outs[0], outs[1], list(outs[2:2 + nw]), list(outs[2 + nw:2 + 2 * nw])), outs[-1]


def _swap_wait(started, after, name):
    send_sems, recv_sems, grads, landing = started
    nw = len(grads)

    def body(*refs):
        for cp in _swap_copies(refs[:nw], refs[nw:2 * nw], refs[2 * nw], refs[2 * nw + 1]):
            cp.wait_send()
            cp.wait_recv()

    outs = pl.pallas_call(
        body, name=name, out_shape=_hbm_like(grads) + _hbm_like(landing),
        in_specs=[HBM_SPEC] * (2 * nw) + [SEM_SPEC, SEM_SPEC] + [ANY] * len(after), out_specs=(HBM_SPEC,) * (2 * nw),
        input_output_aliases={i: i for i in range(2 * nw)}, compiler_params=SPLIT_COPY,
    )(*grads, *landing, send_sems, recv_sems, *after)
    return list(outs[:nw]), list(outs[nw:])


def _scatter_start(partials, name):
    nw = len(partials)
    landing = [lax.empty((3,) + p.shape[1:], p.dtype) for p in partials]

    def body(*refs):
        src, land = refs[:nw], refs[nw:2 * nw]
        send_sems, recv_sems = refs[2 * nw], refs[2 * nw + 1]
        token = refs[-1]
        x, y, c, chips = _place()
        for k, chip in enumerate(chips):
            for w in range(nw):
                pltpu.make_async_remote_copy(src_ref=src[w].at[2 * chip[0] + chip[1]], dst_ref=land[w].at[k], send_sem=send_sems.at[k * nw + w],
                                             recv_sem=recv_sems.at[k * nw + w], device_id=(*chip, c), device_id_type=MESH).start()
        token[...] = jnp.zeros_like(token)

    sem = pltpu.SemaphoreType.DMA((3 * nw,))
    outs = pl.pallas_call(
        body, name=name, out_shape=(sem, sem) + _hbm_like(partials) + _hbm_like(landing) + (_token_shape(),),
        in_specs=[HBM_SPEC] * (2 * nw), out_specs=(SEM_SPEC, SEM_SPEC) + (HBM_SPEC,) * (2 * nw) + (TOKEN_SPEC,),
        input_output_aliases={i: 2 + i for i in range(2 * nw)}, compiler_params=SPLIT_COPY,
    )(*_in_hbm(partials), *_in_hbm(landing))
    return (outs[0], outs[1], list(outs[2:2 + nw]), list(outs[2 + nw:2 + 2 * nw])), outs[-1]


def _scatter_wait(started, after, name):
    send_sems, recv_sems, partials, landing = started
    nw = len(partials)

    def body(*refs):
        src, land = refs[:nw], refs[nw:2 * nw]
        ssem, rsem = refs[2 * nw], refs[2 * nw + 1]
        x, y, c, chips = _place()
        for k, chip in enumerate(chips):
            for w in range(nw):
                cp = pltpu.make_async_remote_copy(src_ref=src[w].at[2 * chip[0] + chip[1]], dst_ref=land[w].at[k], send_sem=ssem.at[k * nw + w],
                                                  recv_sem=rsem.at[k * nw + w], device_id=(*chip, c), device_id_type=MESH)
                cp.wait_send()
                cp.wait_recv()

    outs = pl.pallas_call(
        body, name=name, out_shape=_hbm_like(partials) + _hbm_like(landing),
        in_specs=[HBM_SPEC] * (2 * nw) + [SEM_SPEC, SEM_SPEC] + [ANY] * len(after), out_specs=(HBM_SPEC,) * (2 * nw),
        input_output_aliases={i: i for i in range(2 * nw)}, compiler_params=SPLIT_COPY,
    )(*partials, *landing, send_sems, recv_sems, *after)
    return list(outs[:nw]), list(outs[nw:])


def _add_halves(grad, other, name):
    k = grad.shape[1]
    h = other.shape[0] // N_CHIPS
    tb = _tile(h, 512, 16)
    g4 = grad.reshape(N_CHIPS, 2, h, k)
    o3 = other.reshape(N_CHIPS, h, k)
    core = lax.axis_index("c").astype(jnp.int32).reshape(1)

    def body(c_ref, g_ref, o_ref, p_ref):
        p_ref[...] = (g_ref[...].astype(F32) + o_ref[...].astype(F32)).astype(BF16)

    return pl.pallas_call(
        body, out_shape=jax.ShapeDtypeStruct((N_CHIPS, h, k), BF16),
        grid_spec=pltpu.PrefetchScalarGridSpec(
            num_scalar_prefetch=1, grid=(N_CHIPS, h // tb),
            in_specs=[pl.BlockSpec((None, None, tb, k), lambda j, i, c_ref: (j, c_ref[0], i, 0)),
                      pl.BlockSpec((None, tb, k), lambda j, i, c_ref: (j, i, 0))],
            out_specs=pl.BlockSpec((None, tb, k), lambda j, i, c_ref: (j, i, 0))),
        name=name, compiler_params=_params(("parallel", "parallel")),
    )(core, g4, o3)


def _add_partials(partial, others, name):
    _, h, k = partial.shape
    tb = _tile(h, 512, 16)
    nb = h // tb
    place = jnp.stack([2 * lax.axis_index("x") + lax.axis_index("y"), lax.axis_index("c")]).astype(jnp.int32)

    def body(s_ref, p_ref, o0_ref, o1_ref, o2_ref, f_ref):
        f_ref[...] = ((p_ref[...].astype(F32) + o0_ref[...].astype(F32)) + o1_ref[...].astype(F32)) + o2_ref[...].astype(F32)

    def other(s):
        return pl.BlockSpec((None, tb, k), lambda i, s_ref, s=s: (s, i, 0))

    return pl.pallas_call(
        body, out_shape=jax.ShapeDtypeStruct((2 * h, k), F32),
        grid_spec=pltpu.PrefetchScalarGridSpec(
            num_scalar_prefetch=1, grid=(nb,),
            in_specs=[pl.BlockSpec((None, tb, k), lambda i, s_ref: (s_ref[0], i, 0)), other(0), other(1), other(2)],
            out_specs=pl.BlockSpec((tb, k), lambda i, s_ref: (s_ref[1] * nb + i, 0))),
        name=name, compiler_params=_params(("parallel",)),
    )(place, partial, others, others, others)


def _share_halves(fulls, name):
    nw = len(fulls)

    def body(*refs):
        ins, outs = refs[:nw], refs[nw:2 * nw]
        send_sems, recv_sems = refs[2 * nw:]
        x, y, c, _ = _place()
        copies = []
        for w in range(nw):
            h = fulls[w].shape[0] // 2
            start = pl.multiple_of(c * h, 8)
            copies.append(pltpu.make_async_remote_copy(
                src_ref=ins[w].at[pl.ds(start, h), :], dst_ref=outs[w].at[pl.ds(start, h), :], send_sem=send_sems.at[w],
                recv_sem=recv_sems.at[w], device_id=(x, y, 1 - c), device_id_type=MESH))
            copies[-1].start()
        for cp in copies:
            cp.wait()

    sem = pltpu.SemaphoreType.DMA((nw,))
    return pl.pallas_call(
        body, out_shape=tuple(jax.ShapeDtypeStruct(f.shape, f.dtype) for f in fulls),
        in_specs=[ANY] * nw, out_specs=(ANY,) * nw, scratch_shapes=[sem, sem], name=name,
        input_output_aliases={w: w for w in range(nw)},
    )(*fulls)


def _forward_then_finish(started_group, after, tag):
    ici_send, ici_recv, bufs = started_group
    d2d_send, d2d_recv, bufs, _ = _gather_forward(bufs, ici_send, ici_recv, after, f"gather_forward_{tag}")
    return _gather_finish(bufs, d2d_send, d2d_recv, f"gather_finish_{tag}")


def _reduce_start(swapping, tag, after=()):
    grads, from_sibling = _swap_wait(swapping, after, f"swap_wait_{tag}")
    chip_sums = [_add_halves(g, o, f"add_halves_{tag}_{i}") for i, (g, o) in enumerate(zip(grads, from_sibling))]
    return _scatter_start(chip_sums, f"scatter_start_{tag}")


def _reduce_finish(started, after, tag):
    chip_sums, from_chips = _scatter_wait(started, after, f"scatter_wait_{tag}")
    fulls = [_add_partials(p, o, f"add_partials_{tag}_{i}") for i, (p, o) in enumerate(zip(chip_sums, from_chips))]
    return _share_halves(fulls, f"share_halves_{tag}")


def _flatten_pad(parts, cols=SMALL_COLS):
    flat = jnp.concatenate([p.reshape(-1) for p in parts])
    rows = -(-flat.shape[0] // (16 * cols)) * 16
    return jnp.pad(flat, (0, rows * cols - flat.shape[0])).reshape(rows, cols)


def _split_flat(buf, shapes):
    flat = buf.reshape(-1)
    out, off = [], 0
    for s in shapes:
        n = math.prod(s)
        out.append(flat[off:off + n].reshape(s))
        off += n
    return out


def _ssm_setup(seq_len, ssm_a_re, ssm_a_im, ssm_log_dt, ssm_b_re, ssm_b_im, ssm_c_re, ssm_c_im):
    lam_r, lam_i, bbar_r, bbar_i = _ssm_discretize(ssm_a_re, ssm_a_im, ssm_log_dt, ssm_b_re, ssm_b_im)
    tab_f, tab_b = _scan_tables(lam_r, lam_i, seq_len // N_SEG)
    pk = {"br": _pack_in(bbar_r), "bi": _pack_in(bbar_i), "cr": _pack_out(ssm_c_re), "ci": _pack_out(ssm_c_im)}
    packs = {k: v.astype(BF16) for k, v in pk.items()}
    packs.update({"brt": jnp.swapaxes(packs["br"], 1, 2), "bit": jnp.swapaxes(packs["bi"], 1, 2),
                  "crt": jnp.swapaxes(packs["cr"], 1, 2), "cit": jnp.swapaxes(packs["ci"], 1, 2)})
    return packs, tab_f, tab_b


def _local_step(xs, target, mod, w_in_t, comm, norm_mix_g, attn_sinks, ssm, ssm_d, norm_ffn_g, conv_w_full, ffn_conv_b, final_g,
                aw, sw, ff):
    l, d = xs.shape
    u_off = aw + 2 * KV_WIDTH
    ga_off = u_off + sw
    gs_off = ga_off + d
    packs, tab_f, tab_b = ssm
    dvec = ssm_d.reshape(1, sw)

    h1 = _norm_mod(xs, norm_mix_g, mod, 1, 0, "norm_mod1")
    proj = _matmul(h1, w_in_t, "nt", "mm_in")
    attn = _attn_fwd(proj, attn_sinks, aw, "attn_fwd", after=(comm["mixer_arrived"]((proj,)),))
    u_il = _interleave(proj[:, u_off:u_off + sw])
    ys_il, gy_il = _s5_fwd(u_il, 0, packs, dvec, tab_f, sw, "s5_fwd")
    gy = _deinterleave(gy_il)
    (w_ap_t, w_glu_t, w_out_f), ffn_weights = comm["later_weights"]((gy, attn))
    attn_out = _matmul(attn, w_ap_t, "nt", "mm_attn_proj")
    glu = _matmul(gy, w_glu_t, "nt", "mm_glu")
    g_attn, g_ssm = proj[:, ga_off:ga_off + d], proj[:, gs_off:gs_off + d]
    mixed = _mix(g_attn, g_ssm, attn_out, glu, "mix")
    mo = _matmul(mixed, w_out_f, "nn", "mm_out", out_dtype=F32)
    x2, h2 = _resid_norm_mod(xs, mo, norm_ffn_g, mod, 2, 4, 3, "resid_norm_mod2")
    w_up_t, w_down_f = ffn_weights((h2,))
    cw = ff // CONV_BLOCKS
    up = _matmul(h2, w_up_t, "nt", "mm_up", interleave=cw)
    act = _conv_act(up, conv_w_full, ffn_conv_b, ff, cw, "conv_act")
    fo = _matmul(act, w_down_f, "nn", "mm_down", out_dtype=F32)
    loss_part, d_final_g, d_gate2, dx3, dfo = _final_loss(x2, fo, mod, 5, final_g.reshape(1, d), target, "final_loss")

    dact = _matmul(dfo, w_down_f, "nt", "mm_down_dx")
    g_down = _matmul(act, dfo, "tn", "mm_down_dw")
    dup, dcw_parts, dcb_parts = _conv_act_bwd(up, dact, conv_w_full, ffn_conv_b, ff, cw, "conv_act_bwd")
    d_conv_w = _sum_rows8(dcw_parts, "sum_conv_w")[:3]
    d_conv_b = _sum_rows8(dcb_parts, "sum_conv_b")[:1]
    g_up = _matmul(dup, h2, "tn", "mm_up_dw", interleave=cw)
    dh2 = _matmul(dup, w_up_t, "nn", "mm_up_dx", interleave=cw, after=(comm["grads_started"]("ffn", [g_up, g_down]),))
    mod = comm["ffn_grads_ready"](mod, (dh2,))
    dx2, d_shift2, d_scale2, d_gain2, dmo, d_gate1 = _norm_mod_bwd(dh2, x2, dx3, norm_ffn_g, mod, 4, "norm_mod2_bwd", branch=mo, gate_col=2)
    dmixed = _matmul(dmo, w_out_f, "nt", "mm_out_dx")
    g_out = _matmul(mixed, dmo, "tn", "mm_out_dw")
    dga, dgs, dattn_out, dglu = _mix_bwd(g_attn, g_ssm, attn_out, glu, dmixed, "mix_bwd")
    dgy = _matmul(dglu, w_glu_t, "nn", "mm_glu_dx")
    g_glu = _matmul(dglu, gy, "tn", "mm_glu_dw")
    du_il, dlam, dbr_p, dbi_p, dcr_p, dci_p, d_dvec = _s5_bwd(u_il, 0, ys_il, _interleave(dgy), packs, dvec, tab_f, tab_b, sw, "s5_bwd")
    du = _deinterleave(du_il)
    dattn = _matmul(dattn_out, w_ap_t, "nn", "mm_attn_proj_dx")
    g_ap = _matmul(dattn_out, attn, "tn", "mm_attn_proj_dw")
    dq, dkv_cur, dkv_prev, d_sinks = _attn_bwd(proj, attn_sinks, dattn, aw, "attn_bwd")
    dkv = dkv_cur + jnp.concatenate([dkv_prev[ATTN_BLOCK:], jnp.zeros((ATTN_BLOCK, 2 * KV_WIDTH), F32)], axis=0)
    dproj = jnp.concatenate([dq, dkv.astype(BF16), du, dga, dgs], axis=1)
    g_in = _matmul(dproj, h1, "tn", "mm_in_dw")
    dh1 = _matmul(dproj, w_in_t, "nn", "mm_in_dx", after=(comm["grads_started"]("rest", [g_in, g_ap, g_glu, g_out]),))
    grad_x, d_shift1, d_scale1, d_gain1 = _norm_mod_bwd(dh1, xs, dx2, norm_mix_g, mod, 1, "norm_mod1_bwd")

    dmod = jnp.concatenate([d_shift1, d_scale1, d_gate1, d_shift2, d_scale2, d_gate2], axis=1)
    small_parts = [dmod, d_gain1, d_sinks, dlam[0], dlam[1], _unpack_diag(dbr_p, SSM_STATE, SSM_GROUP),
                   _unpack_diag(dbi_p, SSM_STATE, SSM_GROUP), _unpack_diag(dcr_p, SSM_STATE, SSM_GROUP),
                   _unpack_diag(dci_p, SSM_STATE, SSM_GROUP), d_dvec, d_gain2, d_conv_b, d_conv_w, d_final_g]
    return loss_part, grad_x, small_parts


def _kernel_impl(x, c, ada_w, ada_b, norm_mix_g, w_in, attn_sinks, w_attn_proj, ssm_a_re, ssm_a_im, ssm_log_dt, ssm_b_re, ssm_b_im,
                 ssm_c_re, ssm_c_im, ssm_d, w_ssm_glu, w_out, norm_ffn_g, w_ffn_up, ffn_conv_w, ffn_conv_b, w_ffn_down, final_g,
                 loss_target, ms, vs):
    ax, ay, ac = lax.axis_index("x"), lax.axis_index("y"), lax.axis_index("c")
    chip = 2 * ax + ay
    batch_row = 4 * ax + 2 * ay + ac
    d = x.shape[2]
    aw = w_attn_proj.shape[1]
    sw = w_ssm_glu.shape[1]
    ff = N_CHIPS * ffn_conv_w.shape[2]
    ngroups = sw // SSM_GROUP

    c_all = _all_gather8(jnp.pad(c, ((0, 7), (0, 0))), "gather_c").reshape(N_DEV, 8, d)[:, 0, :]
    ncol = ada_w.shape[2]
    b_shard = lax.dynamic_slice(ada_b, (0, chip * ncol), (1, ncol))
    mod_blk = _ada_fwd(c_all, ada_w[0], b_shard, "ada_fwd")
    mod_all = _all_gather8(mod_blk, "gather_mod").reshape(N_CHIPS, 2, 8, ncol)[:, 0]
    mod = lax.dynamic_slice(mod_all, (0, batch_row, 0), (N_CHIPS, 1, ncol)).reshape(1, 6 * d)

    conv_w_all = _all_gather8(jnp.pad(ffn_conv_w[0], ((0, 5), (0, 0))), "gather_conv_w")
    conv_w_full = conv_w_all.reshape(N_CHIPS, 2, 8, ff // N_CHIPS)[:, 0, :3].transpose(1, 0, 2).reshape(3, ff)
    placed_in = _place_shard(w_in[0].T.astype(BF16), "place_shard_0", after=(mod, conv_w_full))
    (first,), started_in = _gather_start([[placed_in]], "gather_start_w_in")
    shards = [w_attn_proj[0].T.astype(BF16), w_ssm_glu[0].T.astype(BF16), w_out[0], w_ffn_up[0].T.astype(BF16), w_ffn_down[0]]
    placed = [_place_shard(s, f"place_shard_{i + 1}", after=(started_in,)) for i, s in enumerate(shards)]
    (mixer, ffn), started = _gather_start([placed[:3], placed[3:]], "gather_start_rest")
    ssm = (ssm_a_re[0], ssm_a_im[0], ssm_log_dt[0], ssm_b_re[0], ssm_b_im[0], ssm_c_re[0], ssm_c_im[0], ssm_d[0])
    ssm_tables = _ssm_setup(x.shape[1], *ssm[:7])
    (w_in_t,) = _forward_then_finish(first, (started, ssm_tables[1], ssm_tables[2], *ssm_tables[0].values()), "w_in")
    mod = mod + (started_in[0:1, 0:1] + started[0:1, 0:1])

    pending = {}

    def mixer_arrived(after):
        pending["mixer"] = _gather_forward(mixer[2], mixer[0], mixer[1], after, "gather_forward_mixer")
        return pending["mixer"][3]

    def later_weights(after):
        m_send, m_recv, m_bufs, _ = pending["mixer"]
        f_send, f_recv, f_bufs, f_started = _gather_forward(ffn[2], ffn[0], ffn[1], after, "gather_forward_ffn")
        mixer_weights = _gather_finish(m_bufs, m_send, m_recv, "gather_finish_mixer", (f_started,))
        return mixer_weights, lambda later: _gather_finish(f_bufs, f_send, f_recv, "gather_finish_ffn", later)

    def grads_started(tag, grads):
        pending["swap_" + tag], token = _swap_start(grads, f"swap_start_{tag}")
        return token

    def ffn_grads_ready(mod_now, after):
        pending["ffn"], token = _reduce_start(pending["swap_ffn"], "ffn", after)
        return mod_now + token[0:1, 0:1]

    comm = {"mixer_arrived": mixer_arrived, "later_weights": later_weights, "grads_started": grads_started,
            "ffn_grads_ready": ffn_grads_ready}
    loss_part, grad_x, small_parts = _local_step(
        x[0], loss_target[0], mod, w_in_t, comm, norm_mix_g, attn_sinks, ssm_tables, ssm[7], norm_ffn_g, conv_w_full, ffn_conv_b,
        final_g, aw, sw, ff)
    loss = lax.psum(loss_part[0, 0], ("x", "y", "c"))

    small_shapes = [p.shape for p in small_parts]
    part_buf = _flatten_pad(small_parts).astype(BF16)
    rows = part_buf.shape[0]
    gathered = _all_gather8(part_buf, "gather_small").reshape(N_DEV, rows, SMALL_COLS)
    pending["rest"], rest_token = _reduce_start(pending["swap_rest"], "rest", after=(gathered, grad_x))
    gup_t, grad_w_down = _reduce_finish(pending["ffn"], (rest_token,), "ffn")
    grad_w_up = gup_t.T
    summed = _sum_devices(gathered, "sum_small")
    (s_dmod, s_gain1, s_sinks, s_lr, s_li, s_bbr, s_bbi, s_cr, s_ci, s_dd, s_gain2, s_cb, s_cw, s_fg) = _split_flat(summed, small_shapes)
    _, ssm_vjp = jax.vjp(_ssm_discretize, *ssm[:5])
    g_a_re, g_a_im, g_log_dt, g_b_re, g_b_im = ssm_vjp((s_lr.reshape(ngroups, SSM_STATE), s_li.reshape(ngroups, SSM_STATE), s_bbr, s_bbi))
    g_c_re, g_c_im = jnp.swapaxes(s_cr, 1, 2), jnp.swapaxes(s_ci, 1, 2)
    g_conv_w = lax.dynamic_slice(s_cw, (0, chip * (ff // N_CHIPS)), (3, ff // N_CHIPS))

    dmod_all = gathered.reshape(N_DEV, -1)[:, :6 * d].astype(F32)
    dmod_shard = lax.dynamic_slice(dmod_all, (0, chip * ncol), (N_DEV, ncol))
    ada_res = _ada_bwd_adam(c_all.T, dmod_shard, ada_w[0], ms["ada_w"][0], vs["ada_w"][0], "ada_bwd_adam")

    res = {"ada_w": tuple(o[None] for o in ada_res)}

    def adam_big(nm, w, g):
        res[nm] = (g[None],) + tuple(o[None] for o in _adam(w[0], g, ms[nm][0], vs[nm][0], "adam_" + nm))

    adam_big("w_ffn_up", w_ffn_up, grad_w_up)
    adam_big("w_ffn_down", w_ffn_down, grad_w_down)

    small = [("ada_b", ada_b, s_dmod), ("norm_mix_g", norm_mix_g, s_gain1), ("attn_sinks", attn_sinks, s_sinks),
             ("ssm_a_re", ssm_a_re, g_a_re), ("ssm_a_im", ssm_a_im, g_a_im), ("ssm_log_dt", ssm_log_dt, g_log_dt),
             ("ssm_b_re", ssm_b_re, g_b_re), ("ssm_b_im", ssm_b_im, g_b_im), ("ssm_c_re", ssm_c_re, g_c_re),
             ("ssm_c_im", ssm_c_im, g_c_im), ("ssm_d", ssm_d, s_dd), ("norm_ffn_g", norm_ffn_g, s_gain2),
             ("ffn_conv_w", ffn_conv_w, g_conv_w), ("ffn_conv_b", ffn_conv_b, s_cb), ("final_g", final_g, s_fg)]
    shapes = [t[1].shape for t in small]
    bufs = [_flatten_pad([t[1] for t in small]), _flatten_pad([t[2] for t in small]),
            _flatten_pad([ms[t[0]] for t in small]), _flatten_pad([vs[t[0]] for t in small])]
    s_delta, s_m, s_v = _adam(*bufs, "adam_small")
    for t, dl, m2, v2 in zip(small, _split_flat(s_delta, shapes), _split_flat(s_m, shapes), _split_flat(s_v, shapes)):
        res[t[0]] = (t[2].reshape(t[1].shape), dl, m2, v2)

    done = (s_delta, res["w_ffn_up"][1], res["w_ffn_down"][1], res["ada_w"][1])
    gi_t, gap_t, gglu_t, grad_w_out = _reduce_finish(pending["rest"], done, "rest")
    adam_big("w_in", w_in, gi_t.T)
    adam_big("w_attn_proj", w_attn_proj, gap_t.T)
    adam_big("w_ssm_glu", w_ssm_glu, gglu_t.T)
    adam_big("w_out", w_out, grad_w_out)

    outs = [loss, grad_x[None]]
    for i in range(4):
        outs += [res[nm][i] for nm in WEIGHT_ORDER]
    return tuple(outs)


WEIGHT_ORDER = ("ada_w", "ada_b", "norm_mix_g", "w_in", "attn_sinks", "w_attn_proj", "ssm_a_re", "ssm_a_im", "ssm_log_dt", "ssm_b_re",
                "ssm_b_im", "ssm_c_re", "ssm_c_im", "ssm_d", "w_ssm_glu", "w_out", "norm_ffn_g", "w_ffn_up", "ffn_conv_w", "ffn_conv_b",
                "w_ffn_down", "final_g")


def kernel(x, c, ada_w, ada_b, norm_mix_g, w_in, attn_sinks, w_attn_proj, ssm_a_re, ssm_a_im, ssm_log_dt, ssm_b_re, ssm_b_im, ssm_c_re, ssm_c_im, ssm_d, w_ssm_glu, w_out, norm_ffn_g, w_ffn_up, ffn_conv_w, ffn_conv_b, w_ffn_down, final_g, loss_target, m_ada_w, m_ada_b, m_norm_mix_g, m_w_in, m_attn_sinks, m_w_attn_proj, m_ssm_a_re, m_ssm_a_im, m_ssm_log_dt, m_ssm_b_re, m_ssm_b_im, m_ssm_c_re, m_ssm_c_im, m_ssm_d, m_w_ssm_glu, m_w_out, m_norm_ffn_g, m_w_ffn_up, m_ffn_conv_w, m_ffn_conv_b, m_w_ffn_down, m_final_g, v_ada_w, v_ada_b, v_norm_mix_g, v_w_in, v_attn_sinks, v_w_attn_proj, v_ssm_a_re, v_ssm_a_im, v_ssm_log_dt, v_ssm_b_re, v_ssm_b_im, v_ssm_c_re, v_ssm_c_im, v_ssm_d, v_w_ssm_glu, v_w_out, v_norm_ffn_g, v_w_ffn_up, v_ffn_conv_w, v_ffn_conv_b, v_w_ffn_down, v_final_g):
    ms = dict(zip(WEIGHT_ORDER, (m_ada_w, m_ada_b, m_norm_mix_g, m_w_in, m_attn_sinks, m_w_attn_proj, m_ssm_a_re, m_ssm_a_im, m_ssm_log_dt,
                                 m_ssm_b_re, m_ssm_b_im, m_ssm_c_re, m_ssm_c_im, m_ssm_d, m_w_ssm_glu, m_w_out, m_norm_ffn_g, m_w_ffn_up,
                                 m_ffn_conv_w, m_ffn_conv_b, m_w_ffn_down, m_final_g)))
    vs = dict(zip(WEIGHT_ORDER, (v_ada_w, v_ada_b, v_norm_mix_g, v_w_in, v_attn_sinks, v_w_attn_proj, v_ssm_a_re, v_ssm_a_im, v_ssm_log_dt,
                                 v_ssm_b_re, v_ssm_b_im, v_ssm_c_re, v_ssm_c_im, v_ssm_d, v_w_ssm_glu, v_w_out, v_norm_ffn_g, v_w_ffn_up,
                                 v_ffn_conv_w, v_ffn_conv_b, v_w_ffn_down, v_final_g)))
    return _kernel_impl(x, c, ada_w, ada_b, norm_mix_g, w_in, attn_sinks, w_attn_proj, ssm_a_re, ssm_a_im, ssm_log_dt, ssm_b_re, ssm_b_im,
                        ssm_c_re, ssm_c_im, ssm_d, w_ssm_glu, w_out, norm_ffn_g, w_ffn_up, ffn_conv_w, ffn_conv_b, w_ffn_down, final_g,
                        loss_target, ms, vs)
```
